```python
import jax, jax.numpy as jnp
from jax import lax
import numpy as np

D_MODEL = 1024
BATCH = 8
SEQ = 2048
DEPTH = 4

BRANCH_WIDTH = D_MODEL // 2
N_BRANCH = 3
HGRN_KEY = 128
HGRN_VAL = 128
HGRN_HEADS = BRANCH_WIDTH // HGRN_KEY
HGRN_CHUNK = 64
LB_FLOOR = 1e-30
CONV_K = 3
SG_CHUNK = 128
SG_GROUPS = 4
SG_GROUP_DIM = BRANCH_WIDTH // SG_GROUPS
D_FF = 4 * D_MODEL
NORM_EPS = 1e-6
LN_EPS = 1e-5
IN_COLS = 9 * BRANCH_WIDTH + N_BRANCH * D_MODEL

kernel_name = 'hybrid_hgrn2_shortconv_spatialgate_block'


def rms_norm(x, g, eps=NORM_EPS):
    xf = x.astype(jnp.float32)
    y = xf * lax.rsqrt(jnp.mean(xf * xf, axis=-1, keepdims=True) + eps)
    return (y * g.astype(jnp.float32)).astype(x.dtype)


def layer_norm(x, g, b, eps=LN_EPS):
    xf = x.astype(jnp.float32)
    mu = jnp.mean(xf, axis=-1, keepdims=True)
    xc = xf - mu
    y = xc * lax.rsqrt(jnp.mean(xc * xc, axis=-1, keepdims=True) + eps)
    return (y * g.astype(jnp.float32) + b.astype(jnp.float32)).astype(x.dtype)


def hgrn2_mix(q, fp, iv, go, lb, g_out):
    B, S, _ = q.shape
    H, K, V, L = HGRN_HEADS, HGRN_KEY, HGRN_VAL, HGRN_CHUNK
    N = S // L
    f32 = jnp.float32
    q = jax.nn.silu(q.astype(f32))
    fp = fp.astype(f32)
    lb = lb.astype(f32)
    logf = jnp.logaddexp(jnp.log(jnp.maximum(lb, LB_FLOOR)),
                         jnp.log1p(-lb) + jax.nn.log_sigmoid(fp))
    k = (1.0 - lb) * jax.nn.sigmoid(-fp)
    v = iv.astype(f32)

    def to_chunks(t, d):
        return t.reshape(B, N, L, H, d).transpose(1, 0, 3, 2, 4)

    qs, ks, vs, ls = to_chunks(q, K), to_chunks(k, K), to_chunks(v, V), to_chunks(logf, K)
    causal = jnp.tril(jnp.ones((L, L), dtype=bool))[:, :, None]

    def step(state, inp):
        qc, kc, vc, lc = inp
        b = jnp.cumsum(lc, axis=2)
        diff = b[:, :, :, None, :] - b[:, :, None, :, :]
        decay = jnp.where(causal, jnp.exp(jnp.where(causal, diff, 0.0)), 0.0)
        attn = jnp.einsum('bhtk,bhsk,bhtsk->bhts', qc, kc, decay)
        o = (jnp.einsum('bhts,bhsv->bhtv', attn, vc)
             + jnp.einsum('bhtk,bhkv->bhtv', qc * jnp.exp(b), state))
        b_last = b[:, :, -1:, :]
        new_state = (jnp.exp(b_last[:, :, 0, :])[..., None] * state
                     + jnp.einsum('bhsk,bhsv->bhkv', kc * jnp.exp(b_last - b), vc))
        return new_state, o

    state0 = jnp.zeros((B, H, K, V), f32)
    _, o = lax.scan(step, state0, (qs, ks, vs, ls))
    o = o.transpose(1, 0, 3, 2, 4).reshape(B, S, H, V)
    o = rms_norm(o, g_out.reshape(H, V)) * jax.nn.sigmoid(go.astype(f32)).reshape(B, S, H, V)
    return o.reshape(B, S, H * V).astype(iv.dtype)


def short_conv_mix(bg, cg, xc, w_conv):
    z = cg * xc
    ch = z.shape[-1]
    y = lax.conv_general_dilated(z, w_conv[:, None, :].astype(z.dtype), window_strides=(1,),
                                 padding=[(CONV_K - 1, 0)],
                                 dimension_numbers=('NWC', 'WIO', 'NWC'),
                                 feature_group_count=ch)
    return bg * y


def spatial_gating_mix(u, v, ln_g, ln_b, w_s, b_s):
    B, S, _ = u.shape
    N = S // SG_CHUNK
    u = jax.nn.gelu(u)
    v = layer_norm(jax.nn.gelu(v), ln_g, ln_b)
    vg = v.reshape(B, N, SG_CHUNK, SG_GROUPS, SG_GROUP_DIM)
    mask = jnp.tril(jnp.ones((SG_CHUNK, SG_CHUNK), dtype=w_s.dtype))
    sv = jnp.einsum('gts,bnsgd->bntgd', w_s * mask, vg) + b_s.T[:, :, None]
    return u * sv.reshape(B, S, BRANCH_WIDTH)


def _fwd_setup_inputs(seed: int = 0) -> dict:
    key = jax.random.key(seed)
    ks = jax.random.split(key, 17)
    W, D = BRANCH_WIDTH, D_MODEL
    nrm = jax.random.normal
    f32 = jnp.float32
    return {
        'x': nrm(ks[0], (BATCH, SEQ, D), f32),
        'w_in': nrm(ks[1], (DEPTH, D, IN_COLS), f32) * D ** -0.5,
        'g_mix': 1.0 + 0.01 * nrm(ks[2], (DEPTH, D), f32),
        'lower_bounds': 0.1 * nrm(ks[3], (DEPTH, W), f32),
        'g_hgrn_out': 1.0 + 0.01 * nrm(ks[4], (DEPTH, W), f32),
        'w_conv': nrm(ks[5], (DEPTH, CONV_K, W), f32) * CONV_K ** -0.5,
        'sg_ln_g': 1.0 + 0.01 * nrm(ks[6], (DEPTH, W), f32),
        'sg_ln_b': 0.01 * nrm(ks[7], (DEPTH, W), f32),
        'w_sg': nrm(ks[8], (DEPTH, SG_GROUPS, SG_CHUNK, SG_CHUNK), f32) * SG_CHUNK ** -0.5,
        'b_sg': 1.0 + 0.01 * nrm(ks[9], (DEPTH, SG_GROUPS, SG_CHUNK), f32),
        'w_branch': nrm(ks[10], (DEPTH, N_BRANCH, W, D), f32) * W ** -0.5,
        'w_o': nrm(ks[11], (DEPTH, D, D), f32) * D ** -0.5,
        'g_ffn': 1.0 + 0.01 * nrm(ks[12], (DEPTH, D), f32),
        'w_ff1': nrm(ks[13], (DEPTH, D, D_FF), f32) * D ** -0.5,
        'w_ff2': nrm(ks[14], (DEPTH, D_FF, D), f32) * D_FF ** -0.5,
        'g_final': 1.0 + 0.01 * nrm(ks[15], (D,), f32),
    }


def _fwd_reference(x, w_in, g_mix, lower_bounds, g_hgrn_out, w_conv, sg_ln_g, sg_ln_b,
              w_sg, b_sg, w_branch, w_o, g_ffn, w_ff1, w_ff2, g_final):
    B, S, D = x.shape
    W = BRANCH_WIDTH
    lbs = jax.nn.softmax(lower_bounds.astype(jnp.float32), axis=0)
    lbs = jnp.cumsum(lbs, axis=0) - lbs[0]
    offsets = [W * i for i in range(1, 10)]
    for l in range(DEPTH):
        h = rms_norm(x, g_mix[l])
        p = h @ w_in[l]
        q, fp, iv, go, bg, cg, xc, u, v, gates = jnp.split(p, offsets, axis=-1)
        o_a = hgrn2_mix(q, fp, iv, go, lbs[l], g_hgrn_out[l])
        o_b = short_conv_mix(bg, cg, xc, w_conv[l])
        o_c = spatial_gating_mix(u, v, sg_ln_g[l], sg_ln_b[l], w_sg[l], b_sg[l])
        z = jnp.stack([o_a, o_b, o_c], axis=2)
        y = jnp.einsum('bsnw,nwd->bsnd', z, w_branch[l])
        gate = jax.nn.sigmoid(gates).reshape(B, S, N_BRANCH, D)
        merged = jnp.sum(gate * y, axis=2)
        x = x + merged @ w_o[l]
        h2 = rms_norm(x, g_ffn[l])
        x = x + jnp.square(jax.nn.relu(h2 @ w_ff1[l])) @ w_ff2[l]
    return rms_norm(x, g_final)


import jax as _jax
import jax.numpy as _jnp

TWIN_FORMAT = 'train_step'
FWD_PARAMS = ['x', 'w_in', 'g_mix', 'lower_bounds', 'g_hgrn_out', 'w_conv', 'sg_ln_g', 'sg_ln_b', 'w_sg', 'b_sg', 'w_branch', 'w_o', 'g_ffn', 'w_ff1', 'w_ff2', 'g_final']
TWIN_WEIGHTS = ['w_in', 'g_mix', 'lower_bounds', 'g_hgrn_out', 'w_conv', 'sg_ln_g', 'sg_ln_b', 'w_sg', 'b_sg', 'w_branch', 'w_o', 'g_ffn', 'w_ff1', 'w_ff2', 'g_final']
TWIN_DIFF_INPUT = 'x'
TWIN_INPUTS = ['x', 'w_in', 'g_mix', 'lower_bounds', 'g_hgrn_out', 'w_conv', 'sg_ln_g', 'sg_ln_b', 'w_sg', 'b_sg', 'w_branch', 'w_o', 'g_ffn', 'w_ff1', 'w_ff2', 'g_final', 'loss_target', 'm_w_in', 'm_g_mix', 'm_lower_bounds', 'm_g_hgrn_out', 'm_w_conv', 'm_sg_ln_g', 'm_sg_ln_b', 'm_w_sg', 'm_b_sg', 'm_w_branch', 'm_w_o', 'm_g_ffn', 'm_w_ff1', 'm_w_ff2', 'm_g_final', 'v_w_in', 'v_g_mix', 'v_lower_bounds', 'v_g_hgrn_out', 'v_w_conv', 'v_sg_ln_g', 'v_sg_ln_b', 'v_w_sg', 'v_b_sg', 'v_w_branch', 'v_w_o', 'v_g_ffn', 'v_w_ff1', 'v_w_ff2', 'v_g_final']
TWIN_OUTPUTS = ['loss', 'grad_x', 'grad_w_in', 'grad_g_mix', 'grad_lower_bounds', 'grad_g_hgrn_out', 'grad_w_conv', 'grad_sg_ln_g', 'grad_sg_ln_b', 'grad_w_sg', 'grad_b_sg', 'grad_w_branch', 'grad_w_o', 'grad_g_ffn', 'grad_w_ff1', 'grad_w_ff2', 'grad_g_final', 'delta_w_in', 'delta_g_mix', 'delta_lower_bounds', 'delta_g_hgrn_out', 'delta_w_conv', 'delta_sg_ln_g', 'delta_sg_ln_b', 'delta_w_sg', 'delta_b_sg', 'delta_w_branch', 'delta_w_o', 'delta_g_ffn', 'delta_w_ff1', 'delta_w_ff2', 'delta_g_final', 'new_m_w_in', 'new_m_g_mix', 'new_m_lower_bounds', 'new_m_g_hgrn_out', 'new_m_w_conv', 'new_m_sg_ln_g', 'new_m_sg_ln_b', 'new_m_w_sg', 'new_m_b_sg', 'new_m_w_branch', 'new_m_w_o', 'new_m_g_ffn', 'new_m_w_ff1', 'new_m_w_ff2', 'new_m_g_final', 'new_v_w_in', 'new_v_g_mix', 'new_v_lower_bounds', 'new_v_g_hgrn_out', 'new_v_w_conv', 'new_v_sg_ln_g', 'new_v_sg_ln_b', 'new_v_w_sg', 'new_v_b_sg', 'new_v_w_branch', 'new_v_w_o', 'new_v_g_ffn', 'new_v_w_ff1', 'new_v_w_ff2', 'new_v_g_final']
TWIN_LEAF_KINDS = {'loss': 'loss', 'grad_x': 'grad_x', 'grad_w_in': 'grad_w', 'grad_g_mix': 'grad_w', 'grad_lower_bounds': 'grad_w', 'grad_g_hgrn_out': 'grad_w', 'grad_w_conv': 'grad_w', 'grad_sg_ln_g': 'grad_w', 'grad_sg_ln_b': 'grad_w', 'grad_w_sg': 'grad_w', 'grad_b_sg': 'grad_w', 'grad_w_branch': 'grad_w', 'grad_w_o': 'grad_w', 'grad_g_ffn': 'grad_w', 'grad_w_ff1': 'grad_w', 'grad_w_ff2': 'grad_w', 'grad_g_final': 'grad_w', 'delta_w_in': 'delta_w', 'delta_g_mix': 'delta_w', 'delta_lower_bounds': 'delta_w', 'delta_g_hgrn_out': 'delta_w', 'delta_w_conv': 'delta_w', 'delta_sg_ln_g': 'delta_w', 'delta_sg_ln_b': 'delta_w', 'delta_w_sg': 'delta_w', 'delta_b_sg': 'delta_w', 'delta_w_branch': 'delta_w', 'delta_w_o': 'delta_w', 'delta_g_ffn': 'delta_w', 'delta_w_ff1': 'delta_w', 'delta_w_ff2': 'delta_w', 'delta_g_final': 'delta_w', 'new_m_w_in': 'new_m', 'new_m_g_mix': 'new_m', 'new_m_lower_bounds': 'new_m', 'new_m_g_hgrn_out': 'new_m', 'new_m_w_conv': 'new_m', 'new_m_sg_ln_g': 'new_m', 'new_m_sg_ln_b': 'new_m', 'new_m_w_sg': 'new_m', 'new_m_b_sg': 'new_m', 'new_m_w_branch': 'new_m', 'new_m_w_o': 'new_m', 'new_m_g_ffn': 'new_m', 'new_m_w_ff1': 'new_m', 'new_m_w_ff2': 'new_m', 'new_m_g_final': 'new_m', 'new_v_w_in': 'new_v', 'new_v_g_mix': 'new_v', 'new_v_lower_bounds': 'new_v', 'new_v_g_hgrn_out': 'new_v', 'new_v_w_conv': 'new_v', 'new_v_sg_ln_g': 'new_v', 'new_v_sg_ln_b': 'new_v', 'new_v_w_sg': 'new_v', 'new_v_b_sg': 'new_v', 'new_v_w_branch': 'new_v', 'new_v_w_o': 'new_v', 'new_v_g_ffn': 'new_v', 'new_v_w_ff1': 'new_v', 'new_v_w_ff2': 'new_v', 'new_v_g_final': 'new_v'}


def _forward(args):
    return _fwd_reference(*[args[k] for k in FWD_PARAMS])


def _output_shape():
    out = _jax.eval_shape(lambda: _forward(_fwd_setup_inputs(0)))
    return out.shape, out.dtype

N_MICROBATCH = 1
ADAM_LR = 0.001
ADAM_B1 = 0.9
ADAM_B2 = 0.999
ADAM_EPS = 1e-08
ADAM_WD = 0.01
ADAM_STEP = 10
PER_EXAMPLE_BATCH_AXIS = {'x': 0, 'loss_target': 0}
SHARED_INPUTS = []
_WEIGHT_DTYPES = {'w_in': _jnp.float32, 'g_mix': _jnp.float32, 'lower_bounds': _jnp.float32, 'g_hgrn_out': _jnp.float32, 'w_conv': _jnp.float32, 'sg_ln_g': _jnp.float32, 'sg_ln_b': _jnp.float32, 'w_sg': _jnp.float32, 'b_sg': _jnp.float32, 'w_branch': _jnp.float32, 'w_o': _jnp.float32, 'g_ffn': _jnp.float32, 'w_ff1': _jnp.float32, 'w_ff2': _jnp.float32, 'g_final': _jnp.float32}
MOMENT_SCALE = {'w_in': 3.938029e-02, 'g_mix': 1.144622e-01, 'lower_bounds': 1.602272e-03, 'g_hgrn_out': 3.957779e-02, 'w_conv': 7.234644e-02, 'sg_ln_g': 3.230790e-02, 'sg_ln_b': 3.163141e-02, 'w_sg': 3.213050e-02, 'b_sg': 4.658265e-02, 'w_branch': 4.043258e-02, 'w_o': 6.987436e-02, 'g_ffn': 9.654873e-02, 'w_ff1': 4.758445e-02, 'w_ff2': 8.399676e-02, 'g_final': 1.633337e+01}


def _to_microbatches(a, axis):
    t = _jnp.moveaxis(a, axis, 0)
    t = t.reshape((N_MICROBATCH, t.shape[0] // N_MICROBATCH) + t.shape[1:])
    return _jnp.moveaxis(t, 1, axis + 1)


def setup_inputs(seed: int = 0) -> dict:
    inp = _fwd_setup_inputs(seed)
    key = _jax.random.fold_in(_jax.random.key(seed), 7919)
    shape, _ = _output_shape()
    out = dict(inp)
    out["loss_target"] = _jax.random.normal(_jax.random.fold_in(key, 0), shape, _jnp.float32)
    for i, name in enumerate(TWIN_WEIGHTS):
        w = inp[name].astype(_jnp.float32)
        if MOMENT_SCALE is None:
            s = _jnp.sqrt(_jnp.mean(_jnp.square(w)) + 1e-30)
        else:
            s = MOMENT_SCALE[name]
        km, kv = _jax.random.split(_jax.random.fold_in(key, i + 1))
        out[name] = w
        out["m_" + name] = s * _jax.random.normal(km, w.shape, _jnp.float32)
        out["v_" + name] = (s * s) * _jax.random.uniform(kv, w.shape, _jnp.float32, 0.5, 1.5)
    if N_MICROBATCH > 1:
        for name, axis in PER_EXAMPLE_BATCH_AXIS.items():
            out[name] = _to_microbatches(out[name], axis)
    return {'x': out['x'], 'w_in': out['w_in'], 'g_mix': out['g_mix'], 'lower_bounds': out['lower_bounds'], 'g_hgrn_out': out['g_hgrn_out'], 'w_conv': out['w_conv'], 'sg_ln_g': out['sg_ln_g'], 'sg_ln_b': out['sg_ln_b'], 'w_sg': out['w_sg'], 'b_sg': out['b_sg'], 'w_branch': out['w_branch'], 'w_o': out['w_o'], 'g_ffn': out['g_ffn'], 'w_ff1': out['w_ff1'], 'w_ff2': out['w_ff2'], 'g_final': out['g_final'], 'loss_target': out['loss_target'], 'm_w_in': out['m_w_in'], 'm_g_mix': out['m_g_mix'], 'm_lower_bounds': out['m_lower_bounds'], 'm_g_hgrn_out': out['m_g_hgrn_out'], 'm_w_conv': out['m_w_conv'], 'm_sg_ln_g': out['m_sg_ln_g'], 'm_sg_ln_b': out['m_sg_ln_b'], 'm_w_sg': out['m_w_sg'], 'm_b_sg': out['m_b_sg'], 'm_w_branch': out['m_w_branch'], 'm_w_o': out['m_w_o'], 'm_g_ffn': out['m_g_ffn'], 'm_w_ff1': out['m_w_ff1'], 'm_w_ff2': out['m_w_ff2'], 'm_g_final': out['m_g_final'], 'v_w_in': out['v_w_in'], 'v_g_mix': out['v_g_mix'], 'v_lower_bounds': out['v_lower_bounds'], 'v_g_hgrn_out': out['v_g_hgrn_out'], 'v_w_conv': out['v_w_conv'], 'v_sg_ln_g': out['v_sg_ln_g'], 'v_sg_ln_b': out['v_sg_ln_b'], 'v_w_sg': out['v_w_sg'], 'v_b_sg': out['v_b_sg'], 'v_w_branch': out['v_w_branch'], 'v_w_o': out['v_w_o'], 'v_g_ffn': out['v_g_ffn'], 'v_w_ff1': out['v_w_ff1'], 'v_w_ff2': out['v_w_ff2'], 'v_g_final': out['v_g_final']}


def _loss(weights, diff, rest, loss_target):
    with _jax.named_scope("forward"):
        args = {**rest, TWIN_DIFF_INPUT: diff, **{k: w.astype(_WEIGHT_DTYPES[k]) for k, w in weights.items()}}
        y = _forward(args)
    with _jax.named_scope("loss_head"):
        err = _jnp.square(y.astype(_jnp.float32) - loss_target)
        return 0.5 * _jnp.sum(_jnp.mean(err, axis=-1)) if err.ndim else 0.5 * err


def _adamw(w, g, m, v):
    m = ADAM_B1 * m + (1.0 - ADAM_B1) * g
    v = ADAM_B2 * v + (1.0 - ADAM_B2) * _jnp.square(g)
    m_hat = m / (1.0 - ADAM_B1 ** ADAM_STEP)
    v_hat = v / (1.0 - ADAM_B2 ** ADAM_STEP)
    delta = -ADAM_LR * (m_hat / (_jnp.sqrt(v_hat) + ADAM_EPS) + ADAM_WD * w)
    return delta, m, v


def reference(x, w_in, g_mix, lower_bounds, g_hgrn_out, w_conv, sg_ln_g, sg_ln_b, w_sg, b_sg, w_branch, w_o, g_ffn, w_ff1, w_ff2, g_final, loss_target, m_w_in, m_g_mix, m_lower_bounds, m_g_hgrn_out, m_w_conv, m_sg_ln_g, m_sg_ln_b, m_w_sg, m_b_sg, m_w_branch, m_w_o, m_g_ffn, m_w_ff1, m_w_ff2, m_g_final, v_w_in, v_g_mix, v_lower_bounds, v_g_hgrn_out, v_w_conv, v_sg_ln_g, v_sg_ln_b, v_w_sg, v_b_sg, v_w_branch, v_w_o, v_g_ffn, v_w_ff1, v_w_ff2, v_g_final):
    given = dict(x=x, w_in=w_in, g_mix=g_mix, lower_bounds=lower_bounds, g_hgrn_out=g_hgrn_out, w_conv=w_conv, sg_ln_g=sg_ln_g, sg_ln_b=sg_ln_b, w_sg=w_sg, b_sg=b_sg, w_branch=w_branch, w_o=w_o, g_ffn=g_ffn, w_ff1=w_ff1, w_ff2=w_ff2, g_final=g_final, loss_target=loss_target, m_w_in=m_w_in, m_g_mix=m_g_mix, m_lower_bounds=m_lower_bounds, m_g_hgrn_out=m_g_hgrn_out, m_w_conv=m_w_conv, m_sg_ln_g=m_sg_ln_g, m_sg_ln_b=m_sg_ln_b, m_w_sg=m_w_sg, m_b_sg=m_b_sg, m_w_branch=m_w_branch, m_w_o=m_w_o, m_g_ffn=m_g_ffn, m_w_ff1=m_w_ff1, m_w_ff2=m_w_ff2, m_g_final=m_g_final, v_w_in=v_w_in, v_g_mix=v_g_mix, v_lower_bounds=v_lower_bounds, v_g_hgrn_out=v_g_hgrn_out, v_w_conv=v_w_conv, v_sg_ln_g=v_sg_ln_g, v_sg_ln_b=v_sg_ln_b, v_w_sg=v_w_sg, v_b_sg=v_b_sg, v_w_branch=v_w_branch, v_w_o=v_w_o, v_g_ffn=v_g_ffn, v_w_ff1=v_w_ff1, v_w_ff2=v_w_ff2, v_g_final=v_g_final)
    weights = {n: given[n] for n in TWIN_WEIGHTS}
    shared = {n: given[n] for n in SHARED_INPUTS}
    per_example = {n: given[n] for n in ['x']}
    grad_fn = _jax.value_and_grad(_loss, argnums=(0, 1))

    def one_microbatch(ex, loss_target):
        ex = dict(ex)
        diff = ex.pop(TWIN_DIFF_INPUT)
        return grad_fn(weights, diff, {**shared, **ex}, loss_target)

    if N_MICROBATCH == 1:
        loss, (grad_w, grad_x) = one_microbatch(per_example, given["loss_target"])
    else:
        def body(carry, xs):
            loss_sum, grad_sum = carry
            l_k, (gw_k, gx_k) = one_microbatch(xs[0], xs[1])
            with _jax.named_scope("update"):
                return (loss_sum + l_k, _jax.tree.map(_jnp.add, grad_sum, gw_k)), gx_k

        init = (_jnp.zeros((), _jnp.float32), _jax.tree.map(_jnp.zeros_like, weights))
        (loss, grad_w), grad_x = _jax.lax.scan(body, init, (per_example, given["loss_target"]))
    with _jax.named_scope("update"):
        delta_w, new_m, new_v = {}, {}, {}
        for n in TWIN_WEIGHTS:
            delta_w[n], new_m[n], new_v[n] = _adamw(weights[n], grad_w[n], given["m_" + n], given["v_" + n])
    return (loss, grad_x, *[grad_w[n] for n in TWIN_WEIGHTS], *[delta_w[n] for n in TWIN_WEIGHTS],
            *[new_m[n] for n in TWIN_WEIGHTS], *[new_v[n] for n in TWIN_WEIGHTS])
```

```python
import numpy as np
import jax
import jax.numpy as jnp
from jax import lax
from jax.experimental import pallas as pl
from jax.experimental.pallas import tpu as pltpu

F32, BF16 = jnp.float32, jnp.bfloat16

D_MODEL = 1024
WIDTH = 512
N_BRANCH = 3
N_HEAD = 4
HEAD = 128
H_CHUNK = 64
CONV_K = 3
SG_CHUNK = 128
SG_GROUPS = 4
D_FF = 4096
DEPTH = 4
N_CHIP = 4
IN_COLS = 9 * WIDTH + N_BRANCH * D_MODEL
GATE_COL0 = 9 * WIDTH
LB_FLOOR = 1e-30
NORM_EPS = 1e-6
LN_EPS = 1e-5
ADAM_LR, ADAM_B1, ADAM_B2, ADAM_EPS, ADAM_WD, ADAM_STEP = 0.001, 0.9, 0.999, 1e-08, 0.01, 10

VMEM_LIMIT_BYTES = 48 * 1024 * 1024
SUBLANES, LANES = 8, 128
ELEMWISE_BLOCK_BYTES = 2 * 1024 * 1024

NN = (((1,), (0,)), ((), ()))
NT = (((1,), (1,)), ((), ()))
TN = (((0,), (0,)), ((), ()))
MESH = pl.DeviceIdType.MESH
ANY = pl.BlockSpec(memory_space=pl.ANY)


def _dot(a, b, dims=NN):
    return lax.dot_general(a.astype(BF16), b.astype(BF16), dims, preferred_element_type=F32)


def _params(n_axes):
    return pltpu.CompilerParams(dimension_semantics=("arbitrary",) * n_axes, vmem_limit_bytes=VMEM_LIMIT_BYTES)


def _row0(part, rows=SUBLANES):
    r = lax.broadcasted_iota(jnp.int32, (rows, part.shape[1]), 0)
    return jnp.where(r == 0, part, 0.0)


def _token_tile(T):
    return min(512, T)


def _matmul(name, a, b, *, dims, grid, a_spec, b_spec, out_specs, out_shapes, acc_shape,
            extra=(), extra_specs=(), epilogue=None):
    nk = grid[2]
    n_extra, n_out = len(extra), len(out_shapes)

    def body(*refs):
        a_ref, b_ref = refs[0], refs[1]
        ex = refs[2:2 + n_extra]
        outs = refs[2 + n_extra:2 + n_extra + n_out]
        acc = refs[-1]
        kk = pl.program_id(2)
        part = _dot(a_ref[...], b_ref[...], dims)

        @pl.when(kk == 0)
        def _():
            acc[...] = part

        @pl.when(kk > 0)
        def _():
            acc[...] += part

        @pl.when(kk == nk - 1)
        def _():
            res = epilogue(acc[...], *[e[...] for e in ex]) if epilogue else (acc[...],)
            for o, r in zip(outs, res):
                o[...] = r.astype(o.dtype)

    return pl.pallas_call(
        body, name=name, grid=grid,
        in_specs=[a_spec, b_spec, *extra_specs], out_specs=list(out_specs), out_shape=list(out_shapes),
        scratch_shapes=[pltpu.VMEM(acc_shape, F32)], compiler_params=_params(3),
    )(a, b, *extra)


def _mm_cols(name, a, w, out_dtypes, epilogue=None, extra=()):
    T, K = a.shape
    N = w.shape[2]
    tm = _token_tile(T)
    blk = pl.BlockSpec((tm, N), lambda j, i, kk: (i, j))
    return _matmul(
        name, a, w, dims=NN, grid=(N_CHIP, T // tm, 1),
        a_spec=pl.BlockSpec((tm, K), lambda j, i, kk: (i, 0)),
        b_spec=pl.BlockSpec((None, K, N), lambda j, i, kk: (j, 0, 0)),
        out_specs=[blk] * len(out_dtypes),
        out_shapes=[jax.ShapeDtypeStruct((T, N_CHIP * N), dt) for dt in out_dtypes],
        acc_shape=(tm, N), extra=extra, extra_specs=[blk] * len(extra), epilogue=epilogue)


def _mm_rows(name, a, w, res):
    T = a.shape[0]
    K, N = w.shape[1], w.shape[2]
    tm = _token_tile(T)
    blk = pl.BlockSpec((tm, N), lambda i, j, kk: (i, 0))
    return _matmul(
        name, a, w, dims=NN, grid=(T // tm, 1, N_CHIP),
        a_spec=pl.BlockSpec((tm, K), lambda i, j, kk: (i, kk)),
        b_spec=pl.BlockSpec((None, K, N), lambda i, j, kk: (kk, 0, 0)),
        out_specs=[blk], out_shapes=[jax.ShapeDtypeStruct((T, N), F32)], acc_shape=(tm, N),
        extra=(res,), extra_specs=[blk], epilogue=lambda acc, r: (acc + r,))[0]


def _mm_cols_t(name, g, w, out_dtype, epilogue=None, extra=()):
    T, N = g.shape
    K = w.shape[1]
    tm = _token_tile(T)
    blk = pl.BlockSpec((tm, K), lambda j, i, kk: (i, j))
    return _matmul(
        name, g, w, dims=NT, grid=(N_CHIP, T // tm, 1),
        a_spec=pl.BlockSpec((tm, N), lambda j, i, kk: (i, 0)),
        b_spec=pl.BlockSpec((None, K, N), lambda j, i, kk: (j, 0, 0)),
        out_specs=[blk], out_shapes=[jax.ShapeDtypeStruct((T, N_CHIP * K), out_dtype)], acc_shape=(tm, K),
        extra=extra, extra_specs=[blk] * len(extra), epilogue=epilogue)[0]


def _mm_rows_t(name, g, w):
    T = g.shape[0]
    K, N = w.shape[1], w.shape[2]
    tm = _token_tile(T)
    blk = pl.BlockSpec((tm, K), lambda i, j, kk: (i, 0))
    return _matmul(
        name, g, w, dims=NT, grid=(T // tm, 1, N_CHIP),
        a_spec=pl.BlockSpec((tm, N), lambda i, j, kk: (i, kk)),
        b_spec=pl.BlockSpec((None, K, N), lambda i, j, kk: (kk, 0, 0)),
        out_specs=[blk], out_shapes=[jax.ShapeDtypeStruct((T, K), F32)], acc_shape=(tm, K))[0]


def _mm_wgrad(name, a, g, a_cols, g_cols, a_blocked, g_blocked):
    T = a.shape[0]
    tt = _token_tile(T)
    return _matmul(
        name, a, g, dims=TN, grid=(N_CHIP, 1, T // tt),
        a_spec=pl.BlockSpec((tt, a_cols), (lambda j, i, kk: (kk, j)) if a_blocked else (lambda j, i, kk: (kk, 0))),
        b_spec=pl.BlockSpec((tt, g_cols), (lambda j, i, kk: (kk, j)) if g_blocked else (lambda j, i, kk: (kk, 0))),
        out_specs=[pl.BlockSpec((None, a_cols, g_cols), lambda j, i, kk: (j, 0, 0))],
        out_shapes=[jax.ShapeDtypeStruct((N_CHIP, a_cols, g_cols), F32)], acc_shape=(a_cols, g_cols))[0]


def _rms_fwd(name, x, g):
    T, Dm = x.shape
    tm = min(256, T)

    def body(x_ref, g_ref, h_ref):
        xv = x_ref[...]
        r = lax.rsqrt(jnp.mean(xv * xv, axis=-1, keepdims=True) + NORM_EPS)
        h_ref[...] = (xv * r * g_ref[...]).astype(BF16)

    return pl.pallas_call(
        body, name=name, grid=(T // tm,),
        in_specs=[pl.BlockSpec((tm, Dm), lambda i: (i, 0)), pl.BlockSpec((1, Dm), lambda i: (0, 0))],
        out_specs=pl.BlockSpec((tm, Dm), lambda i: (i, 0)),
        out_shape=jax.ShapeDtypeStruct((T, Dm), BF16), compiler_params=_params(1))(x, g)


def _rms_bwd(name, x, g, dh, dres):
    T, Dm = x.shape
    tm = min(256, T)

    def body(x_ref, g_ref, dh_ref, dres_ref, dx_ref, dg_ref):
        xv = x_ref[...]
        r = lax.rsqrt(jnp.mean(xv * xv, axis=-1, keepdims=True) + NORM_EPS)
        xn = xv * r
        dhv = dh_ref[...]
        dxn = dhv * g_ref[...]
        dx_ref[...] = dres_ref[...] + r * (dxn - xn * jnp.mean(dxn * xn, axis=-1, keepdims=True))

        @pl.when(pl.program_id(0) == 0)
        def _():
            dg_ref[...] = jnp.zeros_like(dg_ref)

        dg_ref[...] += _row0(jnp.sum(dhv * xn, axis=0, keepdims=True))

    tile = pl.BlockSpec((tm, Dm), lambda i: (i, 0))
    return pl.pallas_call(
        body, name=name, grid=(T // tm,),
        in_specs=[tile, pl.BlockSpec((1, Dm), lambda i: (0, 0)), tile, tile],
        out_specs=[tile, pl.BlockSpec((SUBLANES, Dm), lambda i: (0, 0))],
        out_shape=[jax.ShapeDtypeStruct((T, Dm), F32), jax.ShapeDtypeStruct((SUBLANES, Dm), F32)],
        compiler_params=_params(1))(x, g, dh, dres)


def _loss_head(x, g, tgt):
    T, Dm = x.shape
    tm = min(256, T)

    def body(x_ref, g_ref, t_ref, loss_ref, dx_ref, dg_ref):
        xv = x_ref[...]
        gv = g_ref[...]
        r = lax.rsqrt(jnp.mean(xv * xv, axis=-1, keepdims=True) + NORM_EPS)
        xn = xv * r
        err = xn * gv - t_ref[...]
        dy = err * (1.0 / Dm)
        dxn = dy * gv
        dx_ref[...] = r * (dxn - xn * jnp.mean(dxn * xn, axis=-1, keepdims=True))

        @pl.when(pl.program_id(0) == 0)
        def _():
            dg_ref[...] = jnp.zeros_like(dg_ref)
            loss_ref[...] = jnp.zeros_like(loss_ref)

        dg_ref[...] += _row0(jnp.sum(dy * xn, axis=0, keepdims=True))
        part = jnp.sum(jnp.sum(err * err, axis=-1, keepdims=True), axis=0, keepdims=True) * (0.5 / Dm)
        loss_ref[...] += jnp.broadcast_to(part, loss_ref.shape)

    tile = pl.BlockSpec((tm, Dm), lambda i: (i, 0))
    return pl.pallas_call(
        body, name="loss_head", grid=(T // tm,),
        in_specs=[tile, pl.BlockSpec((1, Dm), lambda i: (0, 0)), tile],
        out_specs=[pl.BlockSpec((SUBLANES, LANES), lambda i: (0, 0)), tile,
                   pl.BlockSpec((SUBLANES, Dm), lambda i: (0, 0))],
        out_shape=[jax.ShapeDtypeStruct((SUBLANES, LANES), F32), jax.ShapeDtypeStruct((T, Dm), F32),
                   jax.ShapeDtypeStruct((SUBLANES, Dm), F32)],
        compiler_params=_params(1))(x, g, tgt)


def _softmax_rows(lb_ref):
    rows = [lb_ref[pl.ds(i, 1), :] for i in range(DEPTH)]
    mx = rows[0]
    for r in rows[1:]:
        mx = jnp.maximum(mx, r)
    es = [jnp.exp(r - mx) for r in rows]
    tot = es[0]
    for e in es[1:]:
        tot = tot + e
    return [e / tot for e in es]


def _lbs_fwd(lower_bounds):
    def body(lb_ref, out_ref):
        sm = _softmax_rows(lb_ref)
        run = jnp.zeros_like(sm[0])
        out_ref[pl.ds(0, 1), :] = run
        for i in range(1, DEPTH):
            run = run + sm[i]
            out_ref[pl.ds(i, 1), :] = run

    return pl.pallas_call(body, name="lbs_fwd", out_shape=jax.ShapeDtypeStruct(lower_bounds.shape, F32))(lower_bounds)


def _lbs_bwd(lower_bounds, dlbs):
    def body(lb_ref, d_ref, out_ref):
        sm = _softmax_rows(lb_ref)
        dsm = [jnp.zeros_like(sm[0])]
        for i in range(1, DEPTH):
            acc = d_ref[pl.ds(i, 1), :]
            for l in range(i + 1, DEPTH):
                acc = acc + d_ref[pl.ds(l, 1), :]
            dsm.append(acc)
        inner = dsm[0] * sm[0]
        for i in range(1, DEPTH):
            inner = inner + dsm[i] * sm[i]
        for i in range(DEPTH):
            out_ref[pl.ds(i, 1), :] = sm[i] * (dsm[i] - inner)

    return pl.pallas_call(body, name="lbs_bwd", out_shape=jax.ShapeDtypeStruct(lower_bounds.shape, F32))(lower_bounds, dlbs)


N_LEVEL = 6


def _hgrn_consts():
    L = H_CHUNK
    t = np.arange(L)
    blocks = [(t[:, None] >= t[None, :]).astype(np.float32)]
    masks = []
    m = L // 2
    while m >= 1:
        blk, pos = t // (2 * m), t % (2 * m)
        start = blk * 2 * m
        mat = np.zeros((L, L), np.float32)
        for r in range(L):
            if pos[r] >= m:
                mat[r, start[r] + m:r + 1] = 1.0
            else:
                mat[r, r + 1:start[r] + m] = -1.0
        blocks.append(mat)
        masks.append(((blk[:, None] == blk[None, :]) & (pos[:, None] >= m) & (pos[None, :] < m)).astype(np.float32))
        m //= 2
    blocks.append(np.ones((L, L), np.float32))
    return jnp.asarray(np.concatenate(blocks, 0), BF16), jnp.asarray(np.stack(masks), F32)


def _hgrn_core(qraw, fp, lb, sum_mat, mask_ref):
    L = H_CHUNK
    sq = jax.nn.sigmoid(qraw)
    q = qraw * sq
    sneg = jax.nn.sigmoid(-fp)
    log_sig = jnp.minimum(fp, 0.0) - jnp.log1p(jnp.exp(-jnp.abs(fp)))
    a1 = jnp.log(jnp.maximum(lb, LB_FLOOR))
    a2 = jnp.log1p(-lb) + log_sig
    logf = jnp.maximum(a1, a2) + jnp.log1p(jnp.exp(-jnp.abs(a1 - a2)))
    w1 = jnp.exp(a1 - logf)
    w2 = jnp.exp(a2 - logf)
    k = (1.0 - lb) * sneg
    hi = logf.astype(BF16)
    r1 = logf - hi.astype(F32)
    mid = r1.astype(BF16)
    lo = (r1 - mid.astype(F32)).astype(BF16)
    sums = lax.dot_general(sum_mat, jnp.concatenate([hi, mid, lo], axis=1), NN, preferred_element_type=F32)
    sums = sums[:, 0:HEAD] + sums[:, HEAD:2 * HEAD] + sums[:, 2 * HEAD:3 * HEAD]
    b = sums[0:L]
    b_last = sums[(N_LEVEL + 1) * L:(N_LEVEL + 2) * L]
    eye = lax.broadcasted_iota(jnp.int32, (L, L), 0) == lax.broadcasted_iota(jnp.int32, (L, L), 1)
    attn = jnp.where(eye, jnp.sum(q * k, axis=1, keepdims=True), 0.0)
    fa, fb, ea, eb = [], [], [], []
    for l in range(N_LEVEL):
        d = sums[(l + 1) * L:(l + 2) * L]
        e_a = jnp.exp(jnp.minimum(d, 0.0))
        e_b = jnp.exp(jnp.minimum(-d, 0.0))
        a_l, b_l = q * e_a, k * e_b
        attn = attn + mask_ref[l] * _dot(a_l, b_l, NT)
        fa.append(a_l), fb.append(b_l), ea.append(e_a), eb.append(e_b)
    return dict(sq=sq, q=q, sneg=sneg, logf=logf, w1=w1, w2=w2, k=k, b=b, b_last=b_last, attn=attn,
                fa=fa, fb=fb, ea=ea, eb=eb)


def _hgrn_fwd(p, lbrow, gout):
    T = p.shape[0]
    nch = T // H_CHUNK
    sum_mat, masks = _hgrn_consts()

    def body(p_ref, lb_ref, g_ref, m_ref, mask_ref, o_ref, z_ref, st_ref, state):
        @pl.when(pl.program_id(0) == 0)
        def _():
            state[...] = jnp.zeros_like(state)

        sum_m = m_ref[...]
        for h in range(N_HEAD):
            col = lambda part: pl.ds(part * WIDTH + h * HEAD, HEAD)
            hs = pl.ds(h * HEAD, HEAD)
            v = p_ref[:, col(2)]
            c = _hgrn_core(p_ref[:, col(0)], p_ref[:, col(1)], lb_ref[:, hs], sum_m, mask_ref)
            s0 = state[h]
            st_ref[h] = s0
            o = _dot(c["attn"], v) + _dot(c["q"] * jnp.exp(c["b"]), s0, NT)
            k_dec = c["k"] * jnp.exp(c["b_last"] - c["b"])
            decay = jnp.exp(jnp.max(c["b_last"], axis=0, keepdims=True))
            state[h] = s0 * decay + _dot(v, k_dec, TN)
            o_ref[:, hs] = o
            r = lax.rsqrt(jnp.mean(o * o, axis=-1, keepdims=True) + NORM_EPS)
            z_ref[:, hs] = (o * r * g_ref[:, hs] * jax.nn.sigmoid(p_ref[:, col(3)])).astype(BF16)

    full = lambda shape: pl.BlockSpec(shape, lambda c: (0,) * len(shape))
    return pl.pallas_call(
        body, name="hgrn_fwd", grid=(nch,),
        in_specs=[pl.BlockSpec((H_CHUNK, 4 * WIDTH), lambda c: (c, 0)), full((1, WIDTH)), full((1, WIDTH)),
                  full(sum_mat.shape), full(masks.shape)],
        out_specs=[pl.BlockSpec((H_CHUNK, WIDTH), lambda c: (c, 0)),
                   pl.BlockSpec((None, H_CHUNK, WIDTH), lambda c: (0, c, 0)),
                   pl.BlockSpec((None, N_HEAD, HEAD, HEAD), lambda c: (c, 0, 0, 0))],
        out_shape=[jax.ShapeDtypeStruct((T, WIDTH), F32), jax.ShapeDtypeStruct((N_BRANCH, T, WIDTH), BF16),
                   jax.ShapeDtypeStruct((nch, N_HEAD, HEAD, HEAD), F32)],
        scratch_shapes=[pltpu.VMEM((N_HEAD, HEAD, HEAD), F32)], compiler_params=_params(1),
    )(p, lbrow, gout, sum_mat, masks)


def _hgrn_bwd(p, o_saved, dz, states, lbrow, gout, dp):
    T = p.shape[0]
    nch = T // H_CHUNK
    L = H_CHUNK
    sum_mat, masks = _hgrn_consts()

    def body(p_ref, o_ref, dz_ref, st_ref, lb_ref, g_ref, m_ref, mask_ref, dp_in, dp_ref, dlb_ref, dg_ref, dstate):
        del dp_in

        @pl.when(pl.program_id(0) == 0)
        def _():
            dstate[...] = jnp.zeros_like(dstate)
            dlb_ref[...] = jnp.zeros_like(dlb_ref)
            dg_ref[...] = jnp.zeros_like(dg_ref)

        sum_m = m_ref[...]
        for h in range(N_HEAD):
            col = lambda part: pl.ds(part * WIDTH + h * HEAD, HEAD)
            hs = pl.ds(h * HEAD, HEAD)
            qraw, fp, v, go = p_ref[:, col(0)], p_ref[:, col(1)], p_ref[:, col(2)], p_ref[:, col(3)]
            lb, g = lb_ref[:, hs], g_ref[:, hs]
            c = _hgrn_core(qraw, fp, lb, sum_m, mask_ref)
            q, k, b, b_last = c["q"], c["k"], c["b"], c["b_last"]
            s0, ds1 = st_ref[h], dstate[h]
            e_b = jnp.exp(b)
            q_dec = q * e_b
            e_bl = jnp.exp(b_last - b)
            k_dec = k * e_bl
            decay = jnp.exp(jnp.max(b_last, axis=0, keepdims=True))
            o = o_ref[:, hs]
            r = lax.rsqrt(jnp.mean(o * o, axis=-1, keepdims=True) + NORM_EPS)
            n = o * r
            sgo = jax.nn.sigmoid(go)
            dza = dz_ref[:, hs]
            dgo = dza * n * g * sgo * (1.0 - sgo)
            dg_ref[:, hs] += _row0(jnp.sum(dza * n * sgo, axis=0, keepdims=True))
            dn = dza * g * sgo
            do = r * (dn - n * jnp.mean(dn * n, axis=-1, keepdims=True))
            dattn = _dot(do, v, NT)
            dv = _dot(c["attn"], do, TN) + _dot(k_dec, ds1, NT)
            dq_dec = _dot(do, s0)
            dk_dec = _dot(v, ds1)
            ddiag = jnp.sum(do * v, axis=1, keepdims=True)
            dq = dq_dec * e_b + ddiag * k
            dk = dk_dec * e_bl + ddiag * q
            dsums = [dq_dec * q_dec - dk_dec * k_dec]
            for l in range(N_LEVEL):
                dm = mask_ref[l] * dattn
                da = _dot(dm, c["fb"][l])
                db = _dot(dm, c["fa"][l], TN)
                dq = dq + da * c["ea"][l]
                dk = dk + db * c["eb"][l]
                dsums.append(da * c["fa"][l] - db * c["fb"][l])
            dlast = jnp.sum(ds1 * s0, axis=0, keepdims=True) * decay
            dsums.append(dk_dec * k_dec + _row0(dlast, L))
            dlogf = _dot(sum_m, jnp.concatenate(dsums, axis=0), TN)
            dstate[h] = ds1 * decay + _dot(do, q_dec, TN)
            sq, sneg = c["sq"], c["sneg"]
            dqraw = dq * sq * (1.0 + qraw * (1.0 - sq))
            dfp = dlogf * c["w2"] * sneg - dk * (1.0 - lb) * sneg * (1.0 - sneg)
            inv_lb = jnp.where(lb > LB_FLOOR, 1.0 / jnp.maximum(lb, LB_FLOOR), 0.0)
            dlb_tok = dlogf * (c["w1"] * inv_lb - c["w2"] / (1.0 - lb)) - dk * sneg
            dlb_ref[:, hs] += _row0(jnp.sum(dlb_tok, axis=0, keepdims=True))
            dp_ref[:, col(0)] = dqraw.astype(BF16)
            dp_ref[:, col(1)] = dfp.astype(BF16)
            dp_ref[:, col(2)] = dv.astype(BF16)
            dp_ref[:, col(3)] = dgo.astype(BF16)

    full = lambda shape: pl.BlockSpec(shape, lambda c: (0,) * len(shape))
    rev = lambda c: nch - 1 - c
    return pl.pallas_call(
        body, name="hgrn_bwd", grid=(nch,),
        in_specs=[pl.BlockSpec((L, 4 * WIDTH), lambda c: (rev(c), 0)), pl.BlockSpec((L, WIDTH), lambda c: (rev(c), 0)),
                  pl.BlockSpec((None, L, WIDTH), lambda c: (0, rev(c), 0)),
                  pl.BlockSpec((None, N_HEAD, HEAD, HEAD), lambda c: (rev(c), 0, 0, 0)),
                  full((1, WIDTH)), full((1, WIDTH)), full(sum_mat.shape), full(masks.shape), ANY],
        out_specs=[pl.BlockSpec((L, 4 * WIDTH), lambda c: (rev(c), 0)), full((SUBLANES, WIDTH)), full((SUBLANES, WIDTH))],
        out_shape=[jax.ShapeDtypeStruct(dp.shape, dp.dtype), jax.ShapeDtypeStruct((SUBLANES, WIDTH), F32),
                   jax.ShapeDtypeStruct((SUBLANES, WIDTH), F32)],
        scratch_shapes=[pltpu.VMEM((N_HEAD, HEAD, HEAD), F32)], input_output_aliases={8: 0},
        compiler_params=_params(1),
    )(p, o_saved, dz, states, lbrow, gout, sum_mat, masks, dp)


def _shift_down(tile, halo, s):
    tm = tile.shape[0]
    rows = lax.broadcasted_iota(jnp.int32, tile.shape, 0)
    head = jnp.concatenate([pltpu.roll(halo, s, 0), jnp.zeros((tm - SUBLANES, tile.shape[1]), tile.dtype)], axis=0)
    return jnp.where(rows < s, head, pltpu.roll(tile, s, 0))


def _shift_up(tile, halo, s):
    tm = tile.shape[0]
    rows = lax.broadcasted_iota(jnp.int32, tile.shape, 0)
    tail = jnp.concatenate([jnp.zeros((tm - SUBLANES, tile.shape[1]), tile.dtype), pltpu.roll(halo, SUBLANES - s, 0)], axis=0)
    return jnp.where(rows >= tm - s, tail, pltpu.roll(tile, tm - s, 0))


def _conv_fwd(p, w, z):
    T = p.shape[0]
    tm = _token_tile(T)
    per = tm // SUBLANES

    def body(bg_ref, cg_ref, xc_ref, hcg_ref, hxc_ref, w_ref, z_in, z_ref):
        del z_in
        zc = cg_ref[...] * xc_ref[...]
        hz = jnp.where(pl.program_id(0) > 0, hcg_ref[...] * hxc_ref[...], 0.0)
        y = (w_ref[pl.ds(0, 1), :] * _shift_down(zc, hz, 2) + w_ref[pl.ds(1, 1), :] * _shift_down(zc, hz, 1)
             + w_ref[pl.ds(2, 1), :] * zc)
        z_ref[...] = (bg_ref[...] * y).astype(BF16)

    tile = lambda cb: pl.BlockSpec((tm, WIDTH), lambda i: (i, cb))
    prev = lambda cb: pl.BlockSpec((SUBLANES, WIDTH), lambda i: (jnp.maximum(i * per - 1, 0), cb))
    return pl.pallas_call(
        body, name="conv_fwd", grid=(T // tm,),
        in_specs=[tile(4), tile(5), tile(6), prev(5), prev(6), pl.BlockSpec((CONV_K, WIDTH), lambda i: (0, 0)), ANY],
        out_specs=pl.BlockSpec((None, tm, WIDTH), lambda i: (1, i, 0)),
        out_shape=jax.ShapeDtypeStruct(z.shape, z.dtype), input_output_aliases={6: 0}, compiler_params=_params(1),
    )(p, p, p, p, p, w, z)


def _conv_bwd(p, w, dz, dp):
    T = p.shape[0]
    tm = _token_tile(T)
    per = tm // SUBLANES
    last = T // SUBLANES - 1

    def body(bg_ref, cg_ref, xc_ref, hcg_ref, hxc_ref, nbg_ref, dzb_ref, ndzb_ref, w_ref, dp_in, dp_ref, dw_ref, stash):
        del dp_in
        i, jj = pl.program_id(0), pl.program_id(1)

        @pl.when(jnp.logical_and(i == 0, jj == 0))
        def _():
            dw_ref[...] = jnp.zeros_like(dw_ref)

        @pl.when(jj == 0)
        def _():
            cg, xc, bg = cg_ref[...], xc_ref[...], bg_ref[...]
            w0, w1, w2 = w_ref[pl.ds(0, 1), :], w_ref[pl.ds(1, 1), :], w_ref[pl.ds(2, 1), :]
            zc = cg * xc
            hz = jnp.where(i > 0, hcg_ref[...] * hxc_ref[...], 0.0)
            z2, z1 = _shift_down(zc, hz, 2), _shift_down(zc, hz, 1)
            y = w0 * z2 + w1 * z1 + w2 * zc
            dzb = dzb_ref[...]
            dy = dzb * bg
            hdy = jnp.where(i < pl.num_programs(0) - 1, ndzb_ref[...] * nbg_ref[...], 0.0)
            dzc = w2 * dy + w1 * _shift_up(dy, hdy, 1) + w0 * _shift_up(dy, hdy, 2)
            rows = lax.broadcasted_iota(jnp.int32, (SUBLANES, WIDTH), 0)
            colsum = lambda t: jnp.sum(t, axis=0, keepdims=True)
            dw_ref[...] += (jnp.where(rows == 0, colsum(dy * z2), 0.0) + jnp.where(rows == 1, colsum(dy * z1), 0.0)
                            + jnp.where(rows == 2, colsum(dy * zc), 0.0))
            dp_ref[...] = (dzb * y).astype(BF16)
            stash[0] = dzc * xc
            stash[1] = dzc * cg

        @pl.when(jj > 0)
        def _():
            dp_ref[...] = stash[jj - 1].astype(BF16)

    n_tiles = T // tm
    tile = lambda cb: pl.BlockSpec((tm, WIDTH), lambda i, jj: (i, cb))
    prev = lambda cb: pl.BlockSpec((SUBLANES, WIDTH), lambda i, jj: (jnp.maximum(i * per - 1, 0), cb))
    nxt = lambda i: jnp.minimum((i + 1) * per, last)
    return pl.pallas_call(
        body, name="conv_bwd", grid=(n_tiles, 3),
        in_specs=[tile(4), tile(5), tile(6), prev(5), prev(6),
                  pl.BlockSpec((SUBLANES, WIDTH), lambda i, jj: (nxt(i), 4)),
                  pl.BlockSpec((None, tm, WIDTH), lambda i, jj: (1, i, 0)),
                  pl.BlockSpec((None, SUBLANES, WIDTH), lambda i, jj: (1, nxt(i), 0)),
                  pl.BlockSpec((CONV_K, WIDTH), lambda i, jj: (0, 0)), ANY],
        out_specs=[pl.BlockSpec((tm, WIDTH), lambda i, jj: (i, 4 + jj)),
                   pl.BlockSpec((SUBLANES, WIDTH), lambda i, jj: (0, 0))],
        out_shape=[jax.ShapeDtypeStruct(dp.shape, dp.dtype), jax.ShapeDtypeStruct((SUBLANES, WIDTH), F32)],
        scratch_shapes=[pltpu.VMEM((2, tm, WIDTH), F32)], input_output_aliases={9: 0}, compiler_params=_params(2),
    )(p, p, p, p, p, p, dz, dz, w, dp)


GELU_C = float(np.sqrt(2.0 / np.pi))
GELU_A = 0.044715


def _gelu(x):
    th = jnp.tanh(GELU_C * (x + GELU_A * x * x * x))
    return 0.5 * x * (1.0 + th), th


def _gelu_grad(x, th):
    return 0.5 * (1.0 + th) + 0.5 * x * (1.0 - th * th) * GELU_C * (1.0 + 3.0 * GELU_A * x * x)


def _sg_core(u, v, lng, lnb, ws_ref, bs_ref):
    gu, thu = _gelu(u)
    gv, thv = _gelu(v)
    xc = gv - jnp.mean(gv, axis=-1, keepdims=True)
    rs = lax.rsqrt(jnp.mean(xc * xc, axis=-1, keepdims=True) + LN_EPS)
    xh = xc * rs
    vp = xh * lng + lnb
    tril = (lax.broadcasted_iota(jnp.int32, (SG_CHUNK, SG_CHUNK), 0)
            >= lax.broadcasted_iota(jnp.int32, (SG_CHUNK, SG_CHUNK), 1))
    wm = [jnp.where(tril, ws_ref[g], 0.0).astype(BF16) for g in range(SG_GROUPS)]
    gs = lambda t, g: t[:, g * LANES:(g + 1) * LANES]
    sv = jnp.concatenate([_dot(wm[g], gs(vp, g)) + bs_ref[g] for g in range(SG_GROUPS)], axis=1)
    return dict(gu=gu, thu=thu, thv=thv, rs=rs, xh=xh, vp=vp, tril=tril, wm=wm, sv=sv)


def _sg_fwd(p, lng, lnb, ws, bs, z):
    T = p.shape[0]

    def body(u_ref, v_ref, lng_ref, lnb_ref, ws_ref, bs_ref, z_in, z_ref):
        del z_in
        c = _sg_core(u_ref[...], v_ref[...], lng_ref[...], lnb_ref[...], ws_ref, bs_ref)
        z_ref[...] = (c["gu"] * c["sv"]).astype(BF16)

    full = lambda shape: pl.BlockSpec(shape, lambda c: (0,) * len(shape))
    return pl.pallas_call(
        body, name="sg_fwd", grid=(T // SG_CHUNK,),
        in_specs=[pl.BlockSpec((SG_CHUNK, WIDTH), lambda c: (c, 7)), pl.BlockSpec((SG_CHUNK, WIDTH), lambda c: (c, 8)),
                  full((1, WIDTH)), full((1, WIDTH)), full(ws.shape), full(bs.shape), ANY],
        out_specs=pl.BlockSpec((None, SG_CHUNK, WIDTH), lambda c: (2, c, 0)),
        out_shape=jax.ShapeDtypeStruct(z.shape, z.dtype), input_output_aliases={6: 0}, compiler_params=_params(1),
    )(p, p, lng, lnb, ws, bs, z)


def _sg_bwd(p, lng, lnb, ws, bs, dz, dp):
    T = p.shape[0]

    def body(u_ref, v_ref, lng_ref, lnb_ref, ws_ref, bs_ref, dz_ref, dp_in, dp_ref, dws_ref, dbs_ref, dlng_ref, dlnb_ref,
             stash):
        del dp_in
        cidx, jj = pl.program_id(0), pl.program_id(1)

        @pl.when(jnp.logical_and(cidx == 0, jj == 0))
        def _():
            dws_ref[...] = jnp.zeros_like(dws_ref)
            dbs_ref[...] = jnp.zeros_like(dbs_ref)
            dlng_ref[...] = jnp.zeros_like(dlng_ref)
            dlnb_ref[...] = jnp.zeros_like(dlnb_ref)

        @pl.when(jj == 0)
        def _():
            u, v, lng = u_ref[...], v_ref[...], lng_ref[...]
            c = _sg_core(u, v, lng, lnb_ref[...], ws_ref, bs_ref)
            dzc = dz_ref[...]
            gs = lambda t, g: t[:, g * LANES:(g + 1) * LANES]
            dsv = dzc * c["gu"]
            dvp = []
            for g in range(SG_GROUPS):
                dsv_g = gs(dsv, g)
                dws_ref[g] += jnp.where(c["tril"], _dot(dsv_g, gs(c["vp"], g), NT), 0.0)
                dbs_ref[g] += jnp.sum(dsv_g, axis=1, keepdims=True)
                dvp.append(_dot(c["wm"][g], dsv_g, TN))
            dvp = jnp.concatenate(dvp, axis=1)
            xh = c["xh"]
            dlng_ref[...] += _row0(jnp.sum(dvp * xh, axis=0, keepdims=True))
            dlnb_ref[...] += _row0(jnp.sum(dvp, axis=0, keepdims=True))
            dxh = dvp * lng
            dgv = c["rs"] * (dxh - jnp.mean(dxh, axis=-1, keepdims=True) - xh * jnp.mean(dxh * xh, axis=-1, keepdims=True))
            dp_ref[...] = (dzc * c["sv"] * _gelu_grad(u, c["thu"])).astype(BF16)
            stash[...] = dgv * _gelu_grad(v, c["thv"])

        @pl.when(jj == 1)
        def _():
            dp_ref[...] = stash[...].astype(BF16)

    full = lambda shape: pl.BlockSpec(shape, lambda c, jj: (0,) * len(shape))
    return pl.pallas_call(
        body, name="sg_bwd", grid=(T // SG_CHUNK, 2),
        in_specs=[pl.BlockSpec((SG_CHUNK, WIDTH), lambda c, jj: (c, 7)), pl.BlockSpec((SG_CHUNK, WIDTH), lambda c, jj: (c, 8)),
                  full((1, WIDTH)), full((1, WIDTH)), full(ws.shape), full(bs.shape),
                  pl.BlockSpec((None, SG_CHUNK, WIDTH), lambda c, jj: (2, c, 0)), ANY],
        out_specs=[pl.BlockSpec((SG_CHUNK, WIDTH), lambda c, jj: (c, 7 + jj)), full(ws.shape), full(bs.shape),
                   full((SUBLANES, WIDTH)), full((SUBLANES, WIDTH))],
        out_shape=[jax.ShapeDtypeStruct(dp.shape, dp.dtype), jax.ShapeDtypeStruct(ws.shape, F32),
                   jax.ShapeDtypeStruct(bs.shape, F32), jax.ShapeDtypeStruct((SUBLANES, WIDTH), F32),
                   jax.ShapeDtypeStruct((SUBLANES, WIDTH), F32)],
        scratch_shapes=[pltpu.VMEM((SG_CHUNK, WIDTH), F32)], input_output_aliases={7: 0}, compiler_params=_params(2),
    )(p, p, lng, lnb, ws, bs, dz, dp)


BRANCH_COLS = D_MODEL // N_CHIP
GATE_BLOCK0 = GATE_COL0 // BRANCH_COLS


def _merge_fwd(z, p, wb):
    T = z.shape[1]
    tm = _token_tile(T)

    def body(z_ref, wb_ref, gt_ref, out_ref, acc):
        n = pl.program_id(2)
        part = jax.nn.sigmoid(gt_ref[...]) * _dot(z_ref[...], wb_ref[...])

        @pl.when(n == 0)
        def _():
            acc[...] = part

        @pl.when(n > 0)
        def _():
            acc[...] += part

        @pl.when(n == N_BRANCH - 1)
        def _():
            out_ref[...] = acc[...].astype(BF16)

    return pl.pallas_call(
        body, name="merge_fwd", grid=(T // tm, N_CHIP, N_BRANCH),
        in_specs=[pl.BlockSpec((None, tm, WIDTH), lambda i, j, n: (n, i, 0)),
                  pl.BlockSpec((None, None, WIDTH, BRANCH_COLS), lambda i, j, n: (j, n, 0, 0)),
                  pl.BlockSpec((tm, BRANCH_COLS), lambda i, j, n: (i, GATE_BLOCK0 + 4 * n + j))],
        out_specs=pl.BlockSpec((tm, BRANCH_COLS), lambda i, j, n: (i, j)),
        out_shape=jax.ShapeDtypeStruct((T, D_MODEL), BF16),
        scratch_shapes=[pltpu.VMEM((tm, BRANCH_COLS), F32)], compiler_params=_params(3))(z, wb, p)


def _merge_bwd(z, p, wb, dmerged):
    T = z.shape[1]
    tm = _token_tile(T)

    def body(z_ref, wb_ref, gt_ref, dm_ref, dp_ref, dy_ref, dz_ref):
        j = pl.program_id(2)
        wbv = wb_ref[...]
        y = _dot(z_ref[...], wbv)
        gate = jax.nn.sigmoid(gt_ref[...])
        dm = dm_ref[...]
        dp_ref[...] = (dm * y * gate * (1.0 - gate)).astype(BF16)
        dyv = dm * gate
        dy_ref[...] = dyv.astype(BF16)
        part = _dot(dyv, wbv, NT)

        @pl.when(j == 0)
        def _():
            dz_ref[...] = part

        @pl.when(j > 0)
        def _():
            dz_ref[...] += part

    return pl.pallas_call(
        body, name="merge_bwd", grid=(T // tm, N_BRANCH, N_CHIP),
        in_specs=[pl.BlockSpec((None, tm, WIDTH), lambda i, n, j: (n, i, 0)),
                  pl.BlockSpec((None, None, WIDTH, BRANCH_COLS), lambda i, n, j: (j, n, 0, 0)),
                  pl.BlockSpec((tm, BRANCH_COLS), lambda i, n, j: (i, GATE_BLOCK0 + 4 * n + j)),
                  pl.BlockSpec((tm, BRANCH_COLS), lambda i, n, j: (i, j))],
        out_specs=[pl.BlockSpec((tm, BRANCH_COLS), lambda i, n, j: (i, GATE_BLOCK0 + 4 * n + j)),
                   pl.BlockSpec((None, tm, BRANCH_COLS), lambda i, n, j: (n, i, j)),
                   pl.BlockSpec((None, tm, WIDTH), lambda i, n, j: (n, i, 0))],
        out_shape=[jax.ShapeDtypeStruct((T, IN_COLS), BF16), jax.ShapeDtypeStruct((N_BRANCH, T, D_MODEL), BF16),
                   jax.ShapeDtypeStruct((N_BRANCH, T, WIDTH), F32)],
        compiler_params=_params(3))(z, wb, p, dmerged)


def _branch_wgrad(z, dy):
    T = z.shape[1]
    tt = _token_tile(T)
    return _matmul(
        "branch_wgrad", z, dy, dims=TN, grid=(N_CHIP * N_BRANCH, 1, T // tt),
        a_spec=pl.BlockSpec((None, tt, WIDTH), lambda j, i, kk: (j % N_BRANCH, kk, 0)),
        b_spec=pl.BlockSpec((None, tt, BRANCH_COLS), lambda j, i, kk: (j % N_BRANCH, kk, j // N_BRANCH)),
        out_specs=[pl.BlockSpec((None, None, WIDTH, BRANCH_COLS), lambda j, i, kk: (j // N_BRANCH, j % N_BRANCH, 0, 0))],
        out_shapes=[jax.ShapeDtypeStruct((N_CHIP, N_BRANCH, WIDTH, BRANCH_COLS), F32)],
        acc_shape=(WIDTH, BRANCH_COLS))[0]


def _layer_fwd(x, wts, small):
    win, wb, wo, w1, w2 = wts
    h = _rms_fwd("rms_mix", x, small["g_mix"])
    p = _mm_cols("in_proj", h, win, [F32])[0]
    o_hgrn, z, states = _hgrn_fwd(p, small["lbs"], small["g_hgrn_out"])
    z = _conv_fwd(p, small["w_conv"], z)
    z = _sg_fwd(p, small["sg_ln_g"], small["sg_ln_b"], small["w_sg"], small["b_sg"], z)
    merged = _merge_fwd(z, p, wb)
    x_mid = _mm_rows("out_proj", merged, wo, x)
    h2 = _rms_fwd("rms_ffn", x_mid, small["g_ffn"])
    a, s = _mm_cols("ff1", h2, w1, [F32, BF16], epilogue=lambda acc: (acc, jnp.square(jnp.maximum(acc, 0.0))))
    x_out = _mm_rows("ff2", s, w2, x_mid)
    saved = dict(x=x, h=h, p=p, o_hgrn=o_hgrn, z=z, states=states, merged=merged, x_mid=x_mid, h2=h2, a=a, s=s)
    return x_out, saved


def _layer_bwd(dx_out, sv, wts, small):
    win, wb, wo, w1, w2 = wts
    g = {}
    da = _mm_cols_t("ff2_dgrad", dx_out, w2, BF16, extra=(sv["a"],),
                    epilogue=lambda acc, a: (acc * 2.0 * jnp.maximum(a, 0.0),))
    g["w_ff2"] = _mm_wgrad("ff2_wgrad", sv["s"], dx_out, w2.shape[1], D_MODEL, True, False)
    g["w_ff1"] = _mm_wgrad("ff1_wgrad", sv["h2"], da, D_MODEL, w1.shape[2], False, True)
    dh2 = _mm_rows_t("ff1_dgrad", da, w1)
    dx_mid, g["g_ffn"] = _rms_bwd("rms_ffn_bwd", sv["x_mid"], small["g_ffn"], dh2, dx_out)
    dmerged = _mm_cols_t("out_proj_dgrad", dx_mid, wo, F32)
    g["w_o"] = _mm_wgrad("out_proj_wgrad", sv["merged"], dx_mid, wo.shape[1], D_MODEL, True, False)
    dp, dy, dz = _merge_bwd(sv["z"], sv["p"], wb, dmerged)
    g["w_branch"] = _branch_wgrad(sv["z"], dy)
    dp, g["lbs"], g["g_hgrn_out"] = _hgrn_bwd(sv["p"], sv["o_hgrn"], dz, sv["states"], small["lbs"],
                                              small["g_hgrn_out"], dp)
    dp, g["w_conv"] = _conv_bwd(sv["p"], small["w_conv"], dz, dp)
    dp, g["w_sg"], g["b_sg"], g["sg_ln_g"], g["sg_ln_b"] = _sg_bwd(
        sv["p"], small["sg_ln_g"], small["sg_ln_b"], small["w_sg"], small["b_sg"], dz, dp)
    g["w_in"] = _mm_wgrad("in_proj_wgrad", sv["h"], dp, D_MODEL, win.shape[2], False, True)
    dh = _mm_rows_t("in_proj_dgrad", dp, win)
    dx, g["g_mix"] = _rms_bwd("rms_mix_bwd", sv["x"], small["g_mix"], dh, dx_mid)
    return dx, g


def _mesh_pos():
    return lax.axis_index("x"), lax.axis_index("y"), lax.axis_index("c")


def _other_chips(x, y):
    return [(1 - x, y), (x, 1 - y), (1 - x, 1 - y)]


def _remote(src, dst, send_sems, recv_sems, k, to):
    return pltpu.make_async_remote_copy(src_ref=src, dst_ref=dst, send_sem=send_sems.at[k], recv_sem=recv_sems.at[k],
                                        device_id=to, device_id_type=MESH)


def _comm_call(name, body, ins, out_shapes, n_remote, n_local):
    scratch = [pltpu.SemaphoreType.DMA((n_remote,)), pltpu.SemaphoreType.DMA((n_remote,))]
    if n_local:
        scratch.append(pltpu.SemaphoreType.DMA((n_local,)))
    return pl.pallas_call(
        body, name=name, in_specs=[ANY] * len(ins), out_specs=[ANY] * len(out_shapes), out_shape=list(out_shapes),
        scratch_shapes=scratch)(*ins)


def _gather_chips(name, shards):
    n = len(shards)

    def body(*refs):
        ins, outs = refs[:n], refs[n:2 * n]
        send_sems, recv_sems, local_sems = refs[2 * n:]
        x, y, c = _mesh_pos()
        chips = _other_chips(x, y)
        sibling = (x, y, 1 - c)
        started = []
        locals_ = []
        for t in range(n):
            mine = pltpu.make_async_copy(ins[t], outs[t].at[2 * x + y], local_sems.at[t])
            mine.start()
            locals_.append(mine)
        half = lambda t, chip, cc: outs[t].at[2 * chip[0] + chip[1], pl.ds(cc * (shards[t].shape[0] // 2), shards[t].shape[0] // 2), :]
        for t in range(n):
            rh = shards[t].shape[0] // 2
            for j, chip in enumerate(chips):
                cp = _remote(ins[t].at[pl.ds(c * rh, rh), :], half(t, (x, y), c), send_sems, recv_sems, 6 * t + j, (*chip, c))
                cp.start()
                started.append(cp)
        for t in range(n):
            for j, chip in enumerate(chips):
                _remote(half(t, chip, c), half(t, chip, c), send_sems, recv_sems, 6 * t + j, (*chip, c)).wait_recv()
                fw = _remote(half(t, chip, c), half(t, chip, c), send_sems, recv_sems, 6 * t + 3 + j, sibling)
                fw.start()
                started.append(fw)
        for t in range(n):
            for j, chip in enumerate(chips):
                _remote(half(t, chip, 1 - c), half(t, chip, 1 - c), send_sems, recv_sems, 6 * t + 3 + j, sibling).wait_recv()
        for cp in started:
            cp.wait_send()
        for mine in locals_:
            mine.wait()

    out_shapes = [jax.ShapeDtypeStruct((N_CHIP, *s.shape), s.dtype) for s in shards]
    return _comm_call(name, body, shards, out_shapes, 6 * n, n)


def _sibling_swap_halves(name, grads):
    n = len(grads)

    def body(*refs):
        ins, outs = refs[:n], refs[n:2 * n]
        send_sems, recv_sems = refs[2 * n:]
        x, y, c = _mesh_pos()
        copies = []
        for t in range(n):
            rh = grads[t].shape[1] // 2
            cp = _remote(ins[t].at[:, pl.ds((1 - c) * rh, rh), :], outs[t], send_sems, recv_sems, t, (x, y, 1 - c))
            cp.start()
            copies.append(cp)
        for cp in copies:
            cp.wait_send()
            cp.wait_recv()

    out_shapes = [jax.ShapeDtypeStruct((N_CHIP, g.shape[1] // 2, g.shape[2]), g.dtype) for g in grads]
    return _comm_call(name, body, grads, out_shapes, n, 0)


def _chip_exchange(name, parts):
    n = len(parts)

    def body(*refs):
        ins, outs = refs[:n], refs[n:2 * n]
        send_sems, recv_sems = refs[2 * n:]
        x, y, c = _mesh_pos()
        copies = []
        for t in range(n):
            for j, chip in enumerate(_other_chips(x, y)):
                cp = _remote(ins[t].at[2 * chip[0] + chip[1]], outs[t].at[j], send_sems, recv_sems, 3 * t + j, (*chip, c))
                cp.start()
                copies.append(cp)
        for cp in copies:
            cp.wait_send()
            cp.wait_recv()

    out_shapes = [jax.ShapeDtypeStruct((3, *q.shape[1:]), q.dtype) for q in parts]
    return _comm_call(name, body, parts, out_shapes, 3 * n, 0)


def _sibling_gather(name, halves):
    n = len(halves)

    def body(*refs):
        ins, outs = refs[:n], refs[n:2 * n]
        send_sems, recv_sems, local_sems = refs[2 * n:]
        x, y, c = _mesh_pos()
        copies = []
        for t in range(n):
            mine = pltpu.make_async_copy(ins[t], outs[t].at[c], local_sems.at[t])
            mine.start()
            cp = _remote(ins[t], outs[t].at[c], send_sems, recv_sems, t, (x, y, 1 - c))
            cp.start()
            copies.append((mine, cp))
        for mine, cp in copies:
            cp.wait_send()
            cp.wait_recv()
            mine.wait()

    out_shapes = [jax.ShapeDtypeStruct((2, *h.shape), h.dtype) for h in halves]
    return _comm_call(name, body, halves, out_shapes, n, n)


def _gather_all(name, block):
    def body(in_ref, out_ref, send_sems, recv_sems, local_sems):
        x, y, c = _mesh_pos()
        chips = _other_chips(x, y)
        sibling = (x, y, 1 - c)
        slot = lambda px, py, pc: out_ref.at[4 * px + 2 * py + pc]
        mine = pltpu.make_async_copy(in_ref, slot(x, y, c), local_sems.at[0])
        mine.start()
        started = [_remote(in_ref, slot(x, y, c), send_sems, recv_sems, 0, sibling)]
        started += [_remote(in_ref, slot(x, y, c), send_sems, recv_sems, 1 + j, (*chip, c)) for j, chip in enumerate(chips)]
        for cp in started:
            cp.start()
        for j, chip in enumerate(chips):
            _remote(slot(*chip, c), slot(*chip, c), send_sems, recv_sems, 1 + j, (*chip, c)).wait_recv()
            fw = _remote(slot(*chip, c), slot(*chip, c), send_sems, recv_sems, 4 + j, sibling)
            fw.start()
            started.append(fw)
        _remote(slot(x, y, 1 - c), slot(x, y, 1 - c), send_sems, recv_sems, 0, sibling).wait_recv()
        for j, chip in enumerate(chips):
            _remote(slot(*chip, 1 - c), slot(*chip, 1 - c), send_sems, recv_sems, 4 + j, sibling).wait_recv()
        for cp in started:
            cp.wait_send()
        mine.wait()

    return _comm_call(name, body, [block], [jax.ShapeDtypeStruct((8, *block.shape), block.dtype)], 7, 1)[0]


def _row_tile(rows, cols):
    cap = max(SUBLANES, ELEMWISE_BLOCK_BYTES // (4 * cols))
    tr = rows
    while tr > cap and tr % 2 == 0:
        tr //= 2
    return tr


def _pair_sum(name, grad, recv, core):
    _, rh, cols = recv.shape
    tr = _row_tile(rh, cols)
    per = rh // tr

    def body(core_ref, g_ref, r_ref, out32_ref, out16_ref):
        del core_ref
        s = g_ref[...] + r_ref[...]
        out32_ref[...] = s
        out16_ref[...] = s.astype(BF16)

    blk = pl.BlockSpec((None, tr, cols), lambda k, i, core_ref: (k, i, 0))
    return pl.pallas_call(
        body, name=name,
        grid_spec=pltpu.PrefetchScalarGridSpec(
            num_scalar_prefetch=1, grid=(N_CHIP, per),
            in_specs=[pl.BlockSpec((None, tr, cols), lambda k, i, core_ref: (k, core_ref[0] * per + i, 0)), blk],
            out_specs=[blk, blk]),
        out_shape=[jax.ShapeDtypeStruct(recv.shape, F32), jax.ShapeDtypeStruct(recv.shape, BF16)],
        compiler_params=_params(2))(core, grad, recv)


def _chip_sum(name, part32, recv, chip):
    _, rh, cols = part32.shape
    tr = _row_tile(rh, cols)

    def body(chip_ref, own_ref, r_ref, out_ref):
        del chip_ref
        out_ref[...] = ((own_ref[...] + r_ref[0].astype(F32)) + r_ref[1].astype(F32)) + r_ref[2].astype(F32)

    return pl.pallas_call(
        body, name=name,
        grid_spec=pltpu.PrefetchScalarGridSpec(
            num_scalar_prefetch=1, grid=(rh // tr,),
            in_specs=[pl.BlockSpec((None, tr, cols), lambda i, chip_ref: (chip_ref[0], i, 0)),
                      pl.BlockSpec((3, tr, cols), lambda i, chip_ref: (0, i, 0))],
            out_specs=pl.BlockSpec((tr, cols), lambda i, chip_ref: (i, 0))),
        out_shape=jax.ShapeDtypeStruct((rh, cols), F32), compiler_params=_params(1))(chip, part32, recv)


def _adamw_math(w, g, m, v):
    m = ADAM_B1 * m + (1.0 - ADAM_B1) * g
    v = ADAM_B2 * v + (1.0 - ADAM_B2) * jnp.square(g)
    m_hat = m / (1.0 - ADAM_B1 ** ADAM_STEP)
    v_hat = v / (1.0 - ADAM_B2 ** ADAM_STEP)
    delta = -ADAM_LR * (m_hat / (jnp.sqrt(v_hat) + ADAM_EPS) + ADAM_WD * w)
    return delta, m, v


def _adamw_layers(name, w, m, v, grads):
    _, rows, cols = w.shape
    tr = _row_tile(rows, cols)

    def body(w_ref, m_ref, v_ref, *rest):
        g_refs, (grad_ref, d_ref, nm_ref, nv_ref) = rest[:DEPTH], rest[DEPTH:]
        layer = pl.program_id(0)
        g = g_refs[0][...]
        for l in range(1, DEPTH):
            g = jnp.where(layer == l, g_refs[l][...], g)
        grad_ref[...] = g
        d_ref[...], nm_ref[...], nv_ref[...] = _adamw_math(w_ref[...], g, m_ref[...], v_ref[...])

    blk = pl.BlockSpec((None, tr, cols), lambda l, i: (l, i, 0))
    g_spec = lambda k: pl.BlockSpec((tr, cols), lambda l, i: (jnp.where(l == k, i, 0), 0))
    return pl.pallas_call(
        body, name=name, grid=(DEPTH, rows // tr),
        in_specs=[blk, blk, blk] + [g_spec(k) for k in range(DEPTH)], out_specs=[blk] * 4,
        out_shape=[jax.ShapeDtypeStruct(w.shape, F32)] * 4, compiler_params=_params(2))(w, m, v, *grads)


def _sum_devices(gathered):
    _, rows, cols = gathered.shape

    def body(g_ref, out_ref):
        s = g_ref[0]
        for d in range(1, 8):
            s = s + g_ref[d]
        out_ref[...] = s

    return pl.pallas_call(body, name="sum_devices", out_shape=jax.ShapeDtypeStruct((rows, cols), F32),
                          compiler_params=pltpu.CompilerParams(vmem_limit_bytes=VMEM_LIMIT_BYTES))(gathered)


def _adamw_flat(w, g, m, v):
    def body(w_ref, g_ref, m_ref, v_ref, d_ref, nm_ref, nv_ref):
        d_ref[...], nm_ref[...], nv_ref[...] = _adamw_math(w_ref[...], g_ref[...], m_ref[...], v_ref[...])

    return pl.pallas_call(body, name="adamw_small", out_shape=[jax.ShapeDtypeStruct(w.shape, F32)] * 3,
                          compiler_params=pltpu.CompilerParams(vmem_limit_bytes=VMEM_LIMIT_BYTES))(w, g, m, v)


SMALL_NAMES = ["g_mix", "lower_bounds", "g_hgrn_out", "w_conv", "sg_ln_g", "sg_ln_b", "w_sg", "b_sg", "g_ffn", "g_final"]
BIG_NAMES = ["w_in", "w_branch", "w_o", "w_ff1", "w_ff2"]
WEIGHT_ORDER = ["w_in", "g_mix", "lower_bounds", "g_hgrn_out", "w_conv", "sg_ln_g", "sg_ln_b", "w_sg", "b_sg", "w_branch",
                "w_o", "g_ffn", "w_ff1", "w_ff2", "g_final"]


def _pack(arrays):
    return jnp.concatenate([a.reshape(-1, LANES) for a in arrays], axis=0)


def _unpack(flat, shapes):
    out, row = [], 0
    for s in shapes:
        n = int(np.prod(s)) // LANES
        out.append(flat[row:row + n].reshape(s))
        row += n
    return out


def _as_2d(name, a):
    return a.reshape(DEPTH, N_BRANCH * WIDTH, BRANCH_COLS) if name == "w_branch" else a


def kernel(x, w_in, g_mix, lower_bounds, g_hgrn_out, w_conv, sg_ln_g, sg_ln_b, w_sg, b_sg, w_branch, w_o, g_ffn, w_ff1, w_ff2, g_final, loss_target, m_w_in, m_g_mix, m_lower_bounds, m_g_hgrn_out, m_w_conv, m_sg_ln_g, m_sg_ln_b, m_w_sg, m_b_sg, m_w_branch, m_w_o, m_g_ffn, m_w_ff1, m_w_ff2, m_g_final, v_w_in, v_g_mix, v_lower_bounds, v_g_hgrn_out, v_w_conv, v_sg_ln_g, v_sg_ln_b, v_w_sg, v_b_sg, v_w_branch, v_w_o, v_g_ffn, v_w_ff1, v_w_ff2, v_g_final):
    weights = dict(w_in=w_in, g_mix=g_mix, lower_bounds=lower_bounds, g_hgrn_out=g_hgrn_out, w_conv=w_conv,
                   sg_ln_g=sg_ln_g, sg_ln_b=sg_ln_b, w_sg=w_sg, b_sg=b_sg, w_branch=w_branch, w_o=w_o, g_ffn=g_ffn,
                   w_ff1=w_ff1, w_ff2=w_ff2, g_final=g_final)
    mom1 = dict(w_in=m_w_in, g_mix=m_g_mix, lower_bounds=m_lower_bounds, g_hgrn_out=m_g_hgrn_out, w_conv=m_w_conv,
                sg_ln_g=m_sg_ln_g, sg_ln_b=m_sg_ln_b, w_sg=m_w_sg, b_sg=m_b_sg, w_branch=m_w_branch, w_o=m_w_o,
                g_ffn=m_g_ffn, w_ff1=m_w_ff1, w_ff2=m_w_ff2, g_final=m_g_final)
    mom2 = dict(w_in=v_w_in, g_mix=v_g_mix, lower_bounds=v_lower_bounds, g_hgrn_out=v_g_hgrn_out, w_conv=v_w_conv,
                sg_ln_g=v_sg_ln_g, sg_ln_b=v_sg_ln_b, w_sg=v_w_sg, b_sg=v_b_sg, w_branch=v_w_branch, w_o=v_w_o,
                g_ffn=v_g_ffn, w_ff1=v_w_ff1, w_ff2=v_w_ff2, g_final=v_g_final)
    xi, yi, ci = _mesh_pos()
    core = jnp.reshape(ci, (1,)).astype(jnp.int32)
    chip = jnp.reshape(2 * xi + yi, (1,)).astype(jnp.int32)
    conv_cols = w_conv.shape[2]

    conv_all = _gather_all("gather_w_conv", w_conv.reshape(DEPTH * CONV_K, conv_cols))
    conv_full = conv_all.reshape(N_CHIP, 2, DEPTH, CONV_K, conv_cols)[:, 0].transpose(1, 2, 0, 3).reshape(DEPTH, CONV_K, WIDTH)
    lbs = _lbs_fwd(lower_bounds)

    big16 = {n: _as_2d(n, weights[n]).astype(BF16) for n in BIG_NAMES}
    act = x[0]
    layers = []
    for l in range(DEPTH):
        gathered = _gather_chips("gather_weights", [big16[n][l] for n in BIG_NAMES])
        wts = [gathered[0], gathered[1].reshape(N_CHIP, N_BRANCH, WIDTH, BRANCH_COLS), *gathered[2:]]
        small = dict(g_mix=g_mix[l:l + 1], lbs=lbs[l:l + 1], g_hgrn_out=g_hgrn_out[l:l + 1], w_conv=conv_full[l],
                     sg_ln_g=sg_ln_g[l:l + 1], sg_ln_b=sg_ln_b[l:l + 1], w_sg=w_sg[l],
                     b_sg=b_sg[l].reshape(SG_GROUPS, SG_CHUNK, 1), g_ffn=g_ffn[l:l + 1])
        act, saved = _layer_fwd(act, wts, small)
        layers.append((wts, small, saved))
    loss_blk, dact, dg_final = _loss_head(act, g_final.reshape(1, D_MODEL), loss_target[0])

    reduced = {n: [None] * DEPTH for n in BIG_NAMES}
    small_grads = [None] * DEPTH
    for l in reversed(range(DEPTH)):
        wts, small, saved = layers[l]
        dact, g = _layer_bwd(dact, saved, wts, small)
        small_grads[l] = g
        full = [g[n].reshape(N_CHIP, -1, g[n].shape[-1]) for n in BIG_NAMES]
        from_sibling = _sibling_swap_halves("grad_pair_swap", full)
        pair = [_pair_sum("grad_pair_sum_" + n, f, r, core) for n, f, r in zip(BIG_NAMES, full, from_sibling)]
        from_chips = _chip_exchange("grad_chip_exchange", [p16 for _, p16 in pair])
        halves = [_chip_sum("grad_chip_sum_" + n, p32, r, chip) for n, (p32, _), r in zip(BIG_NAMES, pair, from_chips)]
        both = _sibling_gather("grad_half_gather", halves)
        for n, b in zip(BIG_NAMES, both):
            reduced[n][l] = b.reshape(-1, b.shape[-1])
    grad_x = dact[None]

    stack = lambda key, rows=None: jnp.stack([small_grads[l][key][0] if rows is None else small_grads[l][key][:rows]
                                              for l in range(DEPTH)])
    local_small = dict(
        g_mix=stack("g_mix"), lower_bounds=stack("lbs"), g_hgrn_out=stack("g_hgrn_out"), w_conv=stack("w_conv", CONV_K),
        sg_ln_g=stack("sg_ln_g"), sg_ln_b=stack("sg_ln_b"), w_sg=jnp.stack([small_grads[l]["w_sg"] for l in range(DEPTH)]),
        b_sg=jnp.stack([small_grads[l]["b_sg"].reshape(SG_GROUPS, SG_CHUNK) for l in range(DEPTH)]),
        g_ffn=stack("g_ffn"), g_final=dg_final[0])
    shapes = [local_small[n].shape for n in SMALL_NAMES] + [(SUBLANES, LANES)]
    summed = _sum_devices(_gather_all("gather_small_grads", _pack([local_small[n] for n in SMALL_NAMES] + [loss_blk])))
    parts = _unpack(summed, shapes)
    loss = parts[-1][0, 0]
    small_grad = dict(zip(SMALL_NAMES, parts[:-1]))
    small_grad["lower_bounds"] = _lbs_bwd(lower_bounds, small_grad["lower_bounds"])
    small_grad["w_conv"] = lax.dynamic_slice_in_dim(small_grad["w_conv"], chip[0] * conv_cols, conv_cols, axis=2)
    g_flat = _pack([small_grad[n] for n in SMALL_NAMES])
    d_flat, m_flat, v_flat = _adamw_flat(_pack([weights[n] for n in SMALL_NAMES]), g_flat,
                                         _pack([mom1[n] for n in SMALL_NAMES]), _pack([mom2[n] for n in SMALL_NAMES]))
    small_shapes = [weights[n].shape for n in SMALL_NAMES]
    grads = dict(small_grad)
    delta = dict(zip(SMALL_NAMES, _unpack(d_flat, small_shapes)))
    new_m = dict(zip(SMALL_NAMES, _unpack(m_flat, small_shapes)))
    new_v = dict(zip(SMALL_NAMES, _unpack(v_flat, small_shapes)))

    for n in BIG_NAMES:
        outs = _adamw_layers("adamw_" + n, _as_2d(n, weights[n]), _as_2d(n, mom1[n]), _as_2d(n, mom2[n]), reduced[n])
        grads[n], delta[n], new_m[n], new_v[n] = [o.reshape(weights[n].shape) for o in outs]

    return (loss, grad_x, *[grads[n] for n in WEIGHT_ORDER], *[delta[n] for n in WEIGHT_ORDER],
            *[new_m[n] for n in WEIGHT_ORDER], *[new_v[n] for n in WEIGHT_ORDER])
```

```python
import numpy as np
import jax
import jax.numpy as jnp
from jax import lax
from jax.experimental import pallas as pl
from jax.experimental.pallas import tpu as pltpu

F32, BF16 = jnp.float32, jnp.bfloat16

D_MODEL = 1024
WIDTH = 512
N_BRANCH = 3
N_HEAD = 4
HEAD = 128
H_CHUNK = 64
CONV_K = 3
SG_CHUNK = 128
SG_GROUPS = 4
D_FF = 4096
DEPTH = 4
N_CHIP = 4
IN_COLS = 9 * WIDTH + N_BRANCH * D_MODEL
GATE_COL0 = 9 * WIDTH
LB_FLOOR = 1e-30
NORM_EPS = 1e-6
LN_EPS = 1e-5
ADAM_LR, ADAM_B1, ADAM_B2, ADAM_EPS, ADAM_WD, ADAM_STEP = 0.001, 0.9, 0.999, 1e-08, 0.01, 10

VMEM_LIMIT_BYTES = 48 * 1024 * 1024
SUBLANES, LANES = 8, 128
ELEMWISE_BLOCK_BYTES = 2 * 1024 * 1024

NN = (((1,), (0,)), ((), ()))
NT = (((1,), (1,)), ((), ()))
TN = (((0,), (0,)), ((), ()))
MESH = pl.DeviceIdType.MESH
ANY = pl.BlockSpec(memory_space=pl.ANY)


def _dot(a, b, dims=NN):
    return lax.dot_general(a.astype(BF16), b.astype(BF16), dims, preferred_element_type=F32)


def _params(n_axes):
    return pltpu.CompilerParams(dimension_semantics=("arbitrary",) * n_axes, vmem_limit_bytes=VMEM_LIMIT_BYTES)


def _row0(part, rows=SUBLANES):
    r = lax.broadcasted_iota(jnp.int32, (rows, part.shape[1]), 0)
    return jnp.where(r == 0, part, 0.0)


def _token_tile(T):
    return min(512, T)


def _matmul(name, a, b, *, dims, grid, a_spec, b_spec, out_specs, out_shapes, acc_shape,
            extra=(), extra_specs=(), epilogue=None):
    nk = grid[2]
    n_extra, n_out = len(extra), len(out_shapes)

    def body(*refs):
        a_ref, b_ref = refs[0], refs[1]
        ex = refs[2:2 + n_extra]
        outs = refs[2 + n_extra:2 + n_extra + n_out]
        acc = refs[-1]
        kk = pl.program_id(2)
        part = _dot(a_ref[...], b_ref[...], dims)

        @pl.when(kk == 0)
        def _():
            acc[...] = part

        @pl.when(kk > 0)
        def _():
            acc[...] += part

        @pl.when(kk == nk - 1)
        def _():
            res = epilogue(acc[...], *[e[...] for e in ex]) if epilogue else (acc[...],)
            for o, r in zip(outs, res):
                o[...] = r.astype(o.dtype)

    return pl.pallas_call(
        body, name=name, grid=grid,
        in_specs=[a_spec, b_spec, *extra_specs], out_specs=list(out_specs), out_shape=list(out_shapes),
        scratch_shapes=[pltpu.VMEM(acc_shape, F32)], compiler_params=_params(3),
    )(a, b, *extra)


def _mm_cols(name, a, w, out_dtypes, epilogue=None, extra=()):
    T, K = a.shape
    N = w.shape[2]
    tm = _token_tile(T)
    blk = pl.BlockSpec((tm, N), lambda j, i, kk: (i, j))
    return _matmul(
        name, a, w, dims=NN, grid=(N_CHIP, T // tm, 1),
        a_spec=pl.BlockSpec((tm, K), lambda j, i, kk: (i, 0)),
        b_spec=pl.BlockSpec((None, K, N), lambda j, i, kk: (j, 0, 0)),
        out_specs=[blk] * len(out_dtypes),
        out_shapes=[jax.ShapeDtypeStruct((T, N_CHIP * N), dt) for dt in out_dtypes],
        acc_shape=(tm, N), extra=extra, extra_specs=[blk] * len(extra), epilogue=epilogue)


def _mm_rows(name, a, w, res):
    T = a.shape[0]
    K, N = w.shape[1], w.shape[2]
    tm = _token_tile(T)
    blk = pl.BlockSpec((tm, N), lambda i, j, kk: (i, 0))
    return _matmul(
        name, a, w, dims=NN, grid=(T // tm, 1, N_CHIP),
        a_spec=pl.BlockSpec((tm, K), lambda i, j, kk: (i, kk)),
        b_spec=pl.BlockSpec((None, K, N), lambda i, j, kk: (kk, 0, 0)),
        out_specs=[blk], out_shapes=[jax.ShapeDtypeStruct((T, N), F32)], acc_shape=(tm, N),
        extra=(res,), extra_specs=[blk], epilogue=lambda acc, r: (acc + r,))[0]


def _mm_cols_t(name, g, w, out_dtype, epilogue=None, extra=()):
    T, N = g.shape
    K = w.shape[1]
    tm = _token_tile(T)
    blk = pl.BlockSpec((tm, K), lambda j, i, kk: (i, j))
    return _matmul(
        name, g, w, dims=NT, grid=(N_CHIP, T // tm, 1),
        a_spec=pl.BlockSpec((tm, N), lambda j, i, kk: (i, 0)),
        b_spec=pl.BlockSpec((None, K, N), lambda j, i, kk: (j, 0, 0)),
        out_specs=[blk], out_shapes=[jax.ShapeDtypeStruct((T, N_CHIP * K), out_dtype)], acc_shape=(tm, K),
        extra=extra, extra_specs=[blk] * len(extra), epilogue=epilogue)[0]


def _mm_rows_t(name, g, w):
    T = g.shape[0]
    K, N = w.shape[1], w.shape[2]
    tm = _token_tile(T)
    blk = pl.BlockSpec((tm, K), lambda i, j, kk: (i, 0))
    return _matmul(
        name, g, w, dims=NT, grid=(T // tm, 1, N_CHIP),
        a_spec=pl.BlockSpec((tm, N), lambda i, j, kk: (i, kk)),
        b_spec=pl.BlockSpec((None, K, N), lambda i, j, kk: (kk, 0, 0)),
        out_specs=[blk], out_shapes=[jax.ShapeDtypeStruct((T, K), F32)], acc_shape=(tm, K))[0]


def _mm_wgrad(name, a, g, a_cols, g_cols, a_blocked, g_blocked):
    T = a.shape[0]
    tt = _token_tile(T)
    return _matmul(
        name, a, g, dims=TN, grid=(N_CHIP, 1, T // tt),
        a_spec=pl.BlockSpec((tt, a_cols), (lambda j, i, kk: (kk, j)) if a_blocked else (lambda j, i, kk: (kk, 0))),
        b_spec=pl.BlockSpec((tt, g_cols), (lambda j, i, kk: (kk, j)) if g_blocked else (lambda j, i, kk: (kk, 0))),
        out_specs=[pl.BlockSpec((None, a_cols, g_cols), lambda j, i, kk: (j, 0, 0))],
        out_shapes=[jax.ShapeDtypeStruct((N_CHIP, a_cols, g_cols), F32)], acc_shape=(a_cols, g_cols))[0]


def _rms_fwd(name, x, g):
    T, Dm = x.shape
    tm = min(256, T)

    def body(x_ref, g_ref, h_ref):
        xv = x_ref[...]
        r = lax.rsqrt(jnp.mean(xv * xv, axis=-1, keepdims=True) + NORM_EPS)
        h_ref[...] = (xv * r * g_ref[...]).astype(BF16)

    return pl.pallas_call(
        body, name=name, grid=(T // tm,),
        in_specs=[pl.BlockSpec((tm, Dm), lambda i: (i, 0)), pl.BlockSpec((1, Dm), lambda i: (0, 0))],
        out_specs=pl.BlockSpec((tm, Dm), lambda i: (i, 0)),
        out_shape=jax.ShapeDtypeStruct((T, Dm), BF16), compiler_params=_params(1))(x, g)


def _rms_bwd(name, x, g, dh, dres):
    T, Dm = x.shape
    tm = min(256, T)

    def body(x_ref, g_ref, dh_ref, dres_ref, dx_ref, dg_ref):
        xv = x_ref[...]
        r = lax.rsqrt(jnp.mean(xv * xv, axis=-1, keepdims=True) + NORM_EPS)
        xn = xv * r
        dhv = dh_ref[...]
        dxn = dhv * g_ref[...]
        dx_ref[...] = dres_ref[...] + r * (dxn - xn * jnp.mean(dxn * xn, axis=-1, keepdims=True))

        @pl.when(pl.program_id(0) == 0)
        def _():
            dg_ref[...] = jnp.zeros_like(dg_ref)

        dg_ref[...] += _row0(jnp.sum(dhv * xn, axis=0, keepdims=True))

    tile = pl.BlockSpec((tm, Dm), lambda i: (i, 0))
    return pl.pallas_call(
        body, name=name, grid=(T // tm,),
        in_specs=[tile, pl.BlockSpec((1, Dm), lambda i: (0, 0)), tile, tile],
        out_specs=[tile, pl.BlockSpec((SUBLANES, Dm), lambda i: (0, 0))],
        out_shape=[jax.ShapeDtypeStruct((T, Dm), F32), jax.ShapeDtypeStruct((SUBLANES, Dm), F32)],
        compiler_params=_params(1))(x, g, dh, dres)


def _loss_head(x, g, tgt):
    T, Dm = x.shape
    tm = min(256, T)

    def body(x_ref, g_ref, t_ref, loss_ref, dx_ref, dg_ref):
        xv = x_ref[...]
        gv = g_ref[...]
        r = lax.rsqrt(jnp.mean(xv * xv, axis=-1, keepdims=True) + NORM_EPS)
        xn = xv * r
        err = xn * gv - t_ref[...]
        dy = err * (1.0 / Dm)
        dxn = dy * gv
        dx_ref[...] = r * (dxn - xn * jnp.mean(dxn * xn, axis=-1, keepdims=True))

        @pl.when(pl.program_id(0) == 0)
        def _():
            dg_ref[...] = jnp.zeros_like(dg_ref)
            loss_ref[...] = jnp.zeros_like(loss_ref)

        dg_ref[...] += _row0(jnp.sum(dy * xn, axis=0, keepdims=True))
        part = jnp.sum(jnp.sum(err * err, axis=-1, keepdims=True), axis=0, keepdims=True) * (0.5 / Dm)
        loss_ref[...] += jnp.broadcast_to(part, loss_ref.shape)

    tile = pl.BlockSpec((tm, Dm), lambda i: (i, 0))
    return pl.pallas_call(
        body, name="loss_head", grid=(T // tm,),
        in_specs=[tile, pl.BlockSpec((1, Dm), lambda i: (0, 0)), tile],
        out_specs=[pl.BlockSpec((SUBLANES, LANES), lambda i: (0, 0)), tile,
                   pl.BlockSpec((SUBLANES, Dm), lambda i: (0, 0))],
        out_shape=[jax.ShapeDtypeStruct((SUBLANES, LANES), F32), jax.ShapeDtypeStruct((T, Dm), F32),
                   jax.ShapeDtypeStruct((SUBLANES, Dm), F32)],
        compiler_params=_params(1))(x, g, tgt)


def _softmax_rows(lb_ref):
    rows = [lb_ref[pl.ds(i, 1), :] for i in range(DEPTH)]
    mx = rows[0]
    for r in rows[1:]:
        mx = jnp.maximum(mx, r)
    es = [jnp.exp(r - mx) for r in rows]
    tot = es[0]
    for e in es[1:]:
        tot = tot + e
    return [e / tot for e in es]


def _lbs_fwd(lower_bounds):
    def body(lb_ref, out_ref):
        sm = _softmax_rows(lb_ref)
        run = jnp.zeros_like(sm[0])
        out_ref[pl.ds(0, 1), :] = run
        for i in range(1, DEPTH):
            run = run + sm[i]
            out_ref[pl.ds(i, 1), :] = run

    return pl.pallas_call(body, name="lbs_fwd", out_shape=jax.ShapeDtypeStruct(lower_bounds.shape, F32))(lower_bounds)


def _lbs_bwd(lower_bounds, dlbs):
    def body(lb_ref, d_ref, out_ref):
        sm = _softmax_rows(lb_ref)
        dsm = [jnp.zeros_like(sm[0])]
        for i in range(1, DEPTH):
            acc = d_ref[pl.ds(i, 1), :]
            for l in range(i + 1, DEPTH):
                acc = acc + d_ref[pl.ds(l, 1), :]
            dsm.append(acc)
        inner = dsm[0] * sm[0]
        for i in range(1, DEPTH):
            inner = inner + dsm[i] * sm[i]
        for i in range(DEPTH):
            out_ref[pl.ds(i, 1), :] = sm[i] * (dsm[i] - inner)

    return pl.pallas_call(body, name="lbs_bwd", out_shape=jax.ShapeDtypeStruct(lower_bounds.shape, F32))(lower_bounds, dlbs)


N_LEVEL = 6


def _hgrn_consts():
    L = H_CHUNK
    t = np.arange(L)
    blocks = [(t[:, None] >= t[None, :]).astype(np.float32)]
    masks = []
    m = L // 2
    while m >= 1:
        blk, pos = t // (2 * m), t % (2 * m)
        start = blk * 2 * m
        mat = np.zeros((L, L), np.float32)
        for r in range(L):
            if pos[r] >= m:
                mat[r, start[r] + m:r + 1] = 1.0
            else:
                mat[r, r + 1:start[r] + m] = -1.0
        blocks.append(mat)
        masks.append(((blk[:, None] == blk[None, :]) & (pos[:, None] >= m) & (pos[None, :] < m)).astype(np.float32))
        m //= 2
    blocks.append(np.ones((L, L), np.float32))
    return jnp.asarray(np.concatenate(blocks, 0), BF16), jnp.asarray(np.stack(masks), F32)


def _hgrn_core(qraw, fp, lb, sum_mat, mask_ref):
    L = H_CHUNK
    sq = jax.nn.sigmoid(qraw)
    q = qraw * sq
    sneg = jax.nn.sigmoid(-fp)
    log_sig = jnp.minimum(fp, 0.0) - jnp.log1p(jnp.exp(-jnp.abs(fp)))
    a1 = jnp.log(jnp.maximum(lb, LB_FLOOR))
    a2 = jnp.log1p(-lb) + log_sig
    logf = jnp.maximum(a1, a2) + jnp.log1p(jnp.exp(-jnp.abs(a1 - a2)))
    w1 = jnp.exp(a1 - logf)
    w2 = jnp.exp(a2 - logf)
    k = (1.0 - lb) * sneg
    hi = logf.astype(BF16)
    r1 = logf - hi.astype(F32)
    mid = r1.astype(BF16)
    lo = (r1 - mid.astype(F32)).astype(BF16)
    sums = lax.dot_general(sum_mat, jnp.concatenate([hi, mid, lo], axis=1), NN, preferred_element_type=F32)
    sums = sums[:, 0:HEAD] + sums[:, HEAD:2 * HEAD] + sums[:, 2 * HEAD:3 * HEAD]
    b = sums[0:L]
    b_last = sums[(N_LEVEL + 1) * L:(N_LEVEL + 2) * L]
    eye = lax.broadcasted_iota(jnp.int32, (L, L), 0) == lax.broadcasted_iota(jnp.int32, (L, L), 1)
    attn = jnp.where(eye, jnp.sum(q * k, axis=1, keepdims=True), 0.0)
    fa, fb, ea, eb = [], [], [], []
    for l in range(N_LEVEL):
        d = sums[(l + 1) * L:(l + 2) * L]
        e_a = jnp.exp(jnp.minimum(d, 0.0))
        e_b = jnp.exp(jnp.minimum(-d, 0.0))
        a_l, b_l = q * e_a, k * e_b
        attn = attn + mask_ref[l] * _dot(a_l, b_l, NT)
        fa.append(a_l), fb.append(b_l), ea.append(e_a), eb.append(e_b)
    return dict(sq=sq, q=q, sneg=sneg, logf=logf, w1=w1, w2=w2, k=k, b=b, b_last=b_last, attn=attn,
                fa=fa, fb=fb, ea=ea, eb=eb)


def _hgrn_fwd(p, lbrow, gout):
    T = p.shape[0]
    nch = T // H_CHUNK
    sum_mat, masks = _hgrn_consts()

    def body(p_ref, lb_ref, g_ref, m_ref, mask_ref, o_ref, z_ref, st_ref, state):
        @pl.when(pl.program_id(0) == 0)
        def _():
            state[...] = jnp.zeros_like(state)

        sum_m = m_ref[...]
        for h in range(N_HEAD):
            col = lambda part: pl.ds(part * WIDTH + h * HEAD, HEAD)
            hs = pl.ds(h * HEAD, HEAD)
            v = p_ref[:, col(2)]
            c = _hgrn_core(p_ref[:, col(0)], p_ref[:, col(1)], lb_ref[:, hs], sum_m, mask_ref)
            s0 = state[h]
            st_ref[h] = s0
            o = _dot(c["attn"], v) + _dot(c["q"] * jnp.exp(c["b"]), s0, NT)
            k_dec = c["k"] * jnp.exp(c["b_last"] - c["b"])
            decay = jnp.exp(jnp.max(c["b_last"], axis=0, keepdims=True))
            state[h] = s0 * decay + _dot(v, k_dec, TN)
            o_ref[:, hs] = o
            r = lax.rsqrt(jnp.mean(o * o, axis=-1, keepdims=True) + NORM_EPS)
            z_ref[:, hs] = (o * r * g_ref[:, hs] * jax.nn.sigmoid(p_ref[:, col(3)])).astype(BF16)

    full = lambda shape: pl.BlockSpec(shape, lambda c: (0,) * len(shape))
    return pl.pallas_call(
        body, name="hgrn_fwd", grid=(nch,),
        in_specs=[pl.BlockSpec((H_CHUNK, 4 * WIDTH), lambda c: (c, 0)), full((1, WIDTH)), full((1, WIDTH)),
                  full(sum_mat.shape), full(masks.shape)],
        out_specs=[pl.BlockSpec((H_CHUNK, WIDTH), lambda c: (c, 0)),
                   pl.BlockSpec((None, H_CHUNK, WIDTH), lambda c: (0, c, 0)),
                   pl.BlockSpec((None, N_HEAD, HEAD, HEAD), lambda c: (c, 0, 0, 0))],
        out_shape=[jax.ShapeDtypeStruct((T, WIDTH), F32), jax.ShapeDtypeStruct((N_BRANCH, T, WIDTH), BF16),
                   jax.ShapeDtypeStruct((nch, N_HEAD, HEAD, HEAD), F32)],
        scratch_shapes=[pltpu.VMEM((N_HEAD, HEAD, HEAD), F32)], compiler_params=_params(1),
    )(p, lbrow, gout, sum_mat, masks)


def _hgrn_bwd(p, o_saved, dz, states, lbrow, gout, dp):
    T = p.shape[0]
    nch = T // H_CHUNK
    L = H_CHUNK
    sum_mat, masks = _hgrn_consts()

    def body(p_ref, o_ref, dz_ref, st_ref, lb_ref, g_ref, m_ref, mask_ref, dp_in, dp_ref, dlb_ref, dg_ref, dstate):
        del dp_in

        @pl.when(pl.program_id(0) == 0)
        def _():
            dstate[...] = jnp.zeros_like(dstate)
            dlb_ref[...] = jnp.zeros_like(dlb_ref)
            dg_ref[...] = jnp.zeros_like(dg_ref)

        sum_m = m_ref[...]
        for h in range(N_HEAD):
            col = lambda part: pl.ds(part * WIDTH + h * HEAD, HEAD)
            hs = pl.ds(h * HEAD, HEAD)
            qraw, fp, v, go = p_ref[:, col(0)], p_ref[:, col(1)], p_ref[:, col(2)], p_ref[:, col(3)]
            lb, g = lb_ref[:, hs], g_ref[:, hs]
            c = _hgrn_core(qraw, fp, lb, sum_m, mask_ref)
            q, k, b, b_last = c["q"], c["k"], c["b"], c["b_last"]
            s0, ds1 = st_ref[h], dstate[h]
            e_b = jnp.exp(b)
            q_dec = q * e_b
            e_bl = jnp.exp(b_last - b)
            k_dec = k * e_bl
            decay = jnp.exp(jnp.max(b_last, axis=0, keepdims=True))
            o = o_ref[:, hs]
            r = lax.rsqrt(jnp.mean(o * o, axis=-1, keepdims=True) + NORM_EPS)
            n = o * r
            sgo = jax.nn.sigmoid(go)
            dza = dz_ref[:, hs]
            dgo = dza * n * g * sgo * (1.0 - sgo)
            dg_ref[:, hs] += _row0(jnp.sum(dza * n * sgo, axis=0, keepdims=True))
            dn = dza * g * sgo
            do = r * (dn - n * jnp.mean(dn * n, axis=-1, keepdims=True))
            dattn = _dot(do, v, NT)
            dv = _dot(c["attn"], do, TN) + _dot(k_dec, ds1, NT)
            dq_dec = _dot(do, s0)
            dk_dec = _dot(v, ds1)
            ddiag = jnp.sum(do * v, axis=1, keepdims=True)
            dq = dq_dec * e_b + ddiag * k
            dk = dk_dec * e_bl + ddiag * q
            dsums = [dq_dec * q_dec - dk_dec * k_dec]
            for l in range(N_LEVEL):
                dm = mask_ref[l] * dattn
                da = _dot(dm, c["fb"][l])
                db = _dot(dm, c["fa"][l], TN)
                dq = dq + da * c["ea"][l]
                dk = dk + db * c["eb"][l]
                dsums.append(da * c["fa"][l] - db * c["fb"][l])
            dlast = jnp.sum(ds1 * s0, axis=0, keepdims=True) * decay
            dsums.append(dk_dec * k_dec + _row0(dlast, L))
            dlogf = _dot(sum_m, jnp.concatenate(dsums, axis=0), TN)
            dstate[h] = ds1 * decay + _dot(do, q_dec, TN)
            sq, sneg = c["sq"], c["sneg"]
            dqraw = dq * sq * (1.0 + qraw * (1.0 - sq))
            dfp = dlogf * c["w2"] * sneg - dk * (1.0 - lb) * sneg * (1.0 - sneg)
            inv_lb = jnp.where(lb > LB_FLOOR, 1.0 / jnp.maximum(lb, LB_FLOOR), 0.0)
            dlb_tok = dlogf * (c["w1"] * inv_lb - c["w2"] / (1.0 - lb)) - dk * sneg
            dlb_ref[:, hs] += _row0(jnp.sum(dlb_tok, axis=0, keepdims=True))
            dp_ref[:, col(0)] = dqraw.astype(BF16)
            dp_ref[:, col(1)] = dfp.astype(BF16)
            dp_ref[:, col(2)] = dv.astype(BF16)
            dp_ref[:, col(3)] = dgo.astype(BF16)

    full = lambda shape: pl.BlockSpec(shape, lambda c: (0,) * len(shape))
    rev = lambda c: nch - 1 - c
    return pl.pallas_call(
        body, name="hgrn_bwd", grid=(nch,),
        in_specs=[pl.BlockSpec((L, 4 * WIDTH), lambda c: (rev(c), 0)), pl.BlockSpec((L, WIDTH), lambda c: (rev(c), 0)),
                  pl.BlockSpec((None, L, WIDTH), lambda c: (0, rev(c), 0)),
                  pl.BlockSpec((None, N_HEAD, HEAD, HEAD), lambda c: (rev(c), 0, 0, 0)),
                  full((1, WIDTH)), full((1, WIDTH)), full(sum_mat.shape), full(masks.shape), ANY],
        out_specs=[pl.BlockSpec((L, 4 * WIDTH), lambda c: (rev(c), 0)), full((SUBLANES, WIDTH)), full((SUBLANES, WIDTH))],
        out_shape=[jax.ShapeDtypeStruct(dp.shape, dp.dtype), jax.ShapeDtypeStruct((SUBLANES, WIDTH), F32),
                   jax.ShapeDtypeStruct((SUBLANES, WIDTH), F32)],
        scratch_shapes=[pltpu.VMEM((N_HEAD, HEAD, HEAD), F32)], input_output_aliases={8: 0},
        compiler_params=_params(1),
    )(p, o_saved, dz, states, lbrow, gout, sum_mat, masks, dp)


def _shift_down(tile, halo, s):
    tm = tile.shape[0]
    rows = lax.broadcasted_iota(jnp.int32, tile.shape, 0)
    head = jnp.concatenate([pltpu.roll(halo, s, 0), jnp.zeros((tm - SUBLANES, tile.shape[1]), tile.dtype)], axis=0)
    return jnp.where(rows < s, head, pltpu.roll(tile, s, 0))


def _shift_up(tile, halo, s):
    tm = tile.shape[0]
    rows = lax.broadcasted_iota(jnp.int32, tile.shape, 0)
    tail = jnp.concatenate([jnp.zeros((tm - SUBLANES, tile.shape[1]), tile.dtype), pltpu.roll(halo, SUBLANES - s, 0)], axis=0)
    return jnp.where(rows >= tm - s, tail, pltpu.roll(tile, tm - s, 0))


def _conv_fwd(p, w, z):
    T = p.shape[0]
    tm = _token_tile(T)
    per = tm // SUBLANES

    def body(bg_ref, cg_ref, xc_ref, hcg_ref, hxc_ref, w_ref, z_in, z_ref):
        del z_in
        zc = cg_ref[...] * xc_ref[...]
        hz = jnp.where(pl.program_id(0) > 0, hcg_ref[...] * hxc_ref[...], 0.0)
        y = (w_ref[pl.ds(0, 1), :] * _shift_down(zc, hz, 2) + w_ref[pl.ds(1, 1), :] * _shift_down(zc, hz, 1)
             + w_ref[pl.ds(2, 1), :] * zc)
        z_ref[...] = (bg_ref[...] * y).astype(BF16)

    tile = lambda cb: pl.BlockSpec((tm, WIDTH), lambda i: (i, cb))
    prev = lambda cb: pl.BlockSpec((SUBLANES, WIDTH), lambda i: (jnp.maximum(i * per - 1, 0), cb))
    return pl.pallas_call(
        body, name="conv_fwd", grid=(T // tm,),
        in_specs=[tile(4), tile(5), tile(6), prev(5), prev(6), pl.BlockSpec((CONV_K, WIDTH), lambda i: (0, 0)), ANY],
        out_specs=pl.BlockSpec((None, tm, WIDTH), lambda i: (1, i, 0)),
        out_shape=jax.ShapeDtypeStruct(z.shape, z.dtype), input_output_aliases={6: 0}, compiler_params=_params(1),
    )(p, p, p, p, p, w, z)


def _conv_bwd(p, w, dz, dp):
    T = p.shape[0]
    tm = _token_tile(T)
    per = tm // SUBLANES
    last = T // SUBLANES - 1

    def body(bg_ref, cg_ref, xc_ref, hcg_ref, hxc_ref, nbg_ref, dzb_ref, ndzb_ref, w_ref, dp_in, dp_ref, dw_ref, stash):
        del dp_in
        i, jj = pl.program_id(0), pl.program_id(1)

        @pl.when(jnp.logical_and(i == 0, jj == 0))
        def _():
            dw_ref[...] = jnp.zeros_like(dw_ref)

        @pl.when(jj == 0)
        def _():
            cg, xc, bg = cg_ref[...], xc_ref[...], bg_ref[...]
            w0, w1, w2 = w_ref[pl.ds(0, 1), :], w_ref[pl.ds(1, 1), :], w_ref[pl.ds(2, 1), :]
            zc = cg * xc
            hz = jnp.where(i > 0, hcg_ref[...] * hxc_ref[...], 0.0)
            z2, z1 = _shift_down(zc, hz, 2), _shift_down(zc, hz, 1)
            y = w0 * z2 + w1 * z1 + w2 * zc
            dzb = dzb_ref[...]
            dy = dzb * bg
            hdy = jnp.where(i < pl.num_programs(0) - 1, ndzb_ref[...] * nbg_ref[...], 0.0)
            dzc = w2 * dy + w1 * _shift_up(dy, hdy, 1) + w0 * _shift_up(dy, hdy, 2)
            rows = lax.broadcasted_iota(jnp.int32, (SUBLANES, WIDTH), 0)
            colsum = lambda t: jnp.sum(t, axis=0, keepdims=True)
            dw_ref[...] += (jnp.where(rows == 0, colsum(dy * z2), 0.0) + jnp.where(rows == 1, colsum(dy * z1), 0.0)
                            + jnp.where(rows == 2, colsum(dy * zc), 0.0))
            dp_ref[...] = (dzb * y).astype(BF16)
            stash[0] = dzc * xc
            stash[1] = dzc * cg

        @pl.when(jj > 0)
        def _():
            dp_ref[...] = stash[jj - 1].astype(BF16)

    n_tiles = T // tm
    tile = lambda cb: pl.BlockSpec((tm, WIDTH), lambda i, jj: (i, cb))
    prev = lambda cb: pl.BlockSpec((SUBLANES, WIDTH), lambda i, jj: (jnp.maximum(i * per - 1, 0), cb))
    nxt = lambda i: jnp.minimum((i + 1) * per, last)
    return pl.pallas_call(
        body, name="conv_bwd", grid=(n_tiles, 3),
        in_specs=[tile(4), tile(5), tile(6), prev(5), prev(6),
                  pl.BlockSpec((SUBLANES, WIDTH), lambda i, jj: (nxt(i), 4)),
                  pl.BlockSpec((None, tm, WIDTH), lambda i, jj: (1, i, 0)),
                  pl.BlockSpec((None, SUBLANES, WIDTH), lambda i, jj: (1, nxt(i), 0)),
                  pl.BlockSpec((CONV_K, WIDTH), lambda i, jj: (0, 0)), ANY],
        out_specs=[pl.BlockSpec((tm, WIDTH), lambda i, jj: (i, 4 + jj)),
                   pl.BlockSpec((SUBLANES, WIDTH), lambda i, jj: (0, 0))],
        out_shape=[jax.ShapeDtypeStruct(dp.shape, dp.dtype), jax.ShapeDtypeStruct((SUBLANES, WIDTH), F32)],
        scratch_shapes=[pltpu.VMEM((2, tm, WIDTH), F32)], input_output_aliases={9: 0}, compiler_params=_params(2),
    )(p, p, p, p, p, p, dz, dz, w, dp)


GELU_C = float(np.sqrt(2.0 / np.pi))
GELU_A = 0.044715


def _gelu(x):
    th = jnp.tanh(GELU_C * (x + GELU_A * x * x * x))
    return 0.5 * x * (1.0 + th), th


def _gelu_grad(x, th):
    return 0.5 * (1.0 + th) + 0.5 * x * (1.0 - th * th) * GELU_C * (1.0 + 3.0 * GELU_A * x * x)


def _sg_core(u, v, lng, lnb, ws_ref, bs_ref):
    gu, thu = _gelu(u)
    gv, thv = _gelu(v)
    xc = gv - jnp.mean(gv, axis=-1, keepdims=True)
    rs = lax.rsqrt(jnp.mean(xc * xc, axis=-1, keepdims=True) + LN_EPS)
    xh = xc * rs
    vp = xh * lng + lnb
    tril = (lax.broadcasted_iota(jnp.int32, (SG_CHUNK, SG_CHUNK), 0)
            >= lax.broadcasted_iota(jnp.int32, (SG_CHUNK, SG_CHUNK), 1))
    wm = [jnp.where(tril, ws_ref[g], 0.0).astype(BF16) for g in range(SG_GROUPS)]
    gs = lambda t, g: t[:, g * LANES:(g + 1) * LANES]
    sv = jnp.concatenate([_dot(wm[g], gs(vp, g)) + bs_ref[g] for g in range(SG_GROUPS)], axis=1)
    return dict(gu=gu, thu=thu, thv=thv, rs=rs, xh=xh, vp=vp, tril=tril, wm=wm, sv=sv)


def _sg_fwd(p, lng, lnb, ws, bs, z):
    T = p.shape[0]

    def body(u_ref, v_ref, lng_ref, lnb_ref, ws_ref, bs_ref, z_in, z_ref):
        del z_in
        c = _sg_core(u_ref[...], v_ref[...], lng_ref[...], lnb_ref[...], ws_ref, bs_ref)
        z_ref[...] = (c["gu"] * c["sv"]).astype(BF16)

    full = lambda shape: pl.BlockSpec(shape, lambda c: (0,) * len(shape))
    return pl.pallas_call(
        body, name="sg_fwd", grid=(T // SG_CHUNK,),
        in_specs=[pl.BlockSpec((SG_CHUNK, WIDTH), lambda c: (c, 7)), pl.BlockSpec((SG_CHUNK, WIDTH), lambda c: (c, 8)),
                  full((1, WIDTH)), full((1, WIDTH)), full(ws.shape), full(bs.shape), ANY],
        out_specs=pl.BlockSpec((None, SG_CHUNK, WIDTH), lambda c: (2, c, 0)),
        out_shape=jax.ShapeDtypeStruct(z.shape, z.dtype), input_output_aliases={6: 0}, compiler_params=_params(1),
    )(p, p, lng, lnb, ws, bs, z)


def _sg_bwd(p, lng, lnb, ws, bs, dz, dp):
    T = p.shape[0]

    def body(u_ref, v_ref, lng_ref, lnb_ref, ws_ref, bs_ref, dz_ref, dp_in, dp_ref, dws_ref, dbs_ref, dlng_ref, dlnb_ref,
             stash):
        del dp_in
        cidx, jj = pl.program_id(0), pl.program_id(1)

        @pl.when(jnp.logical_and(cidx == 0, jj == 0))
        def _():
            dws_ref[...] = jnp.zeros_like(dws_ref)
            dbs_ref[...] = jnp.zeros_like(dbs_ref)
            dlng_ref[...] = jnp.zeros_like(dlng_ref)
            dlnb_ref[...] = jnp.zeros_like(dlnb_ref)

        @pl.when(jj == 0)
        def _():
            u, v, lng = u_ref[...], v_ref[...], lng_ref[...]
            c = _sg_core(u, v, lng, lnb_ref[...], ws_ref, bs_ref)
            dzc = dz_ref[...]
            gs = lambda t, g: t[:, g * LANES:(g + 1) * LANES]
            dsv = dzc * c["gu"]
            dvp = []
            for g in range(SG_GROUPS):
                dsv_g = gs(dsv, g)
                dws_ref[g] += jnp.where(c["tril"], _dot(dsv_g, gs(c["vp"], g), NT), 0.0)
                dbs_ref[g] += jnp.sum(dsv_g, axis=1, keepdims=True)
                dvp.append(_dot(c["wm"][g], dsv_g, TN))
            dvp = jnp.concatenate(dvp, axis=1)
            xh = c["xh"]
            dlng_ref[...] += _row0(jnp.sum(dvp * xh, axis=0, keepdims=True))
            dlnb_ref[...] += _row0(jnp.sum(dvp, axis=0, keepdims=True))
            dxh = dvp * lng
            dgv = c["rs"] * (dxh - jnp.mean(dxh, axis=-1, keepdims=True) - xh * jnp.mean(dxh * xh, axis=-1, keepdims=True))
            dp_ref[...] = (dzc * c["sv"] * _gelu_grad(u, c["thu"])).astype(BF16)
            stash[...] = dgv * _gelu_grad(v, c["thv"])

        @pl.when(jj == 1)
        def _():
            dp_ref[...] = stash[...].astype(BF16)

    full = lambda shape: pl.BlockSpec(shape, lambda c, jj: (0,) * len(shape))
    return pl.pallas_call(
        body, name="sg_bwd", grid=(T // SG_CHUNK, 2),
        in_specs=[pl.BlockSpec((SG_CHUNK, WIDTH), lambda c, jj: (c, 7)), pl.BlockSpec((SG_CHUNK, WIDTH), lambda c, jj: (c, 8)),
                  full((1, WIDTH)), full((1, WIDTH)), full(ws.shape), full(bs.shape),
                  pl.BlockSpec((None, SG_CHUNK, WIDTH), lambda c, jj: (2, c, 0)), ANY],
        out_specs=[pl.BlockSpec((SG_CHUNK, WIDTH), lambda c, jj: (c, 7 + jj)), full(ws.shape), full(bs.shape),
                   full((SUBLANES, WIDTH)), full((SUBLANES, WIDTH))],
        out_shape=[jax.ShapeDtypeStruct(dp.shape, dp.dtype), jax.ShapeDtypeStruct(ws.shape, F32),
                   jax.ShapeDtypeStruct(bs.shape, F32), jax.ShapeDtypeStruct((SUBLANES, WIDTH), F32),
                   jax.ShapeDtypeStruct((SUBLANES, WIDTH), F32)],
        scratch_shapes=[pltpu.VMEM((SG_CHUNK, WIDTH), F32)], input_output_aliases={7: 0}, compiler_params=_params(2),
    )(p, p, lng, lnb, ws, bs, dz, dp)


BRANCH_COLS = D_MODEL // N_CHIP
GATE_BLOCK0 = GATE_COL0 // BRANCH_COLS


def _merge_fwd(z, p, wb):
    T = z.shape[1]
    tm = _token_tile(T)

    def body(z_ref, wb_ref, gt_ref, out_ref, acc):
        n = pl.program_id(2)
        part = jax.nn.sigmoid(gt_ref[...]) * _dot(z_ref[...], wb_ref[...])

        @pl.when(n == 0)
        def _():
            acc[...] = part

        @pl.when(n > 0)
        def _():
            acc[...] += part

        @pl.when(n == N_BRANCH - 1)
        def _():
            out_ref[...] = acc[...].astype(BF16)

    return pl.pallas_call(
        body, name="merge_fwd", grid=(T // tm, N_CHIP, N_BRANCH),
        in_specs=[pl.BlockSpec((None, tm, WIDTH), lambda i, j, n: (n, i, 0)),
                  pl.BlockSpec((None, None, WIDTH, BRANCH_COLS), lambda i, j, n: (j, n, 0, 0)),
                  pl.BlockSpec((tm, BRANCH_COLS), lambda i, j, n: (i, GATE_BLOCK0 + 4 * n + j))],
        out_specs=pl.BlockSpec((tm, BRANCH_COLS), lambda i, j, n: (i, j)),
        out_shape=jax.ShapeDtypeStruct((T, D_MODEL), BF16),
        scratch_shapes=[pltpu.VMEM((tm, BRANCH_COLS), F32)], compiler_params=_params(3))(z, wb, p)


def _merge_bwd(z, p, wb, dmerged):
    T = z.shape[1]
    tm = _token_tile(T)

    def body(z_ref, wb_ref, gt_ref, dm_ref, dp_ref, dy_ref, dz_ref):
        j = pl.program_id(2)
        wbv = wb_ref[...]
        y = _dot(z_ref[...], wbv)
        gate = jax.nn.sigmoid(gt_ref[...])
        dm = dm_ref[...]
        dp_ref[...] = (dm * y * gate * (1.0 - gate)).astype(BF16)
        dyv = dm * gate
        dy_ref[...] = dyv.astype(BF16)
        part = _dot(dyv, wbv, NT)

        @pl.when(j == 0)
        def _():
            dz_ref[...] = part

        @pl.when(j > 0)
        def _():
            dz_ref[...] += part

    return pl.pallas_call(
        body, name="merge_bwd", grid=(T // tm, N_BRANCH, N_CHIP),
        in_specs=[pl.BlockSpec((None, tm, WIDTH), lambda i, n, j: (n, i, 0)),
                  pl.BlockSpec((None, None, WIDTH, BRANCH_COLS), lambda i, n, j: (j, n, 0, 0)),
                  pl.BlockSpec((tm, BRANCH_COLS), lambda i, n, j: (i, GATE_BLOCK0 + 4 * n + j)),
                  pl.BlockSpec((tm, BRANCH_COLS), lambda i, n, j: (i, j))],
        out_specs=[pl.BlockSpec((tm, BRANCH_COLS), lambda i, n, j: (i, GATE_BLOCK0 + 4 * n + j)),
                   pl.BlockSpec((None, tm, BRANCH_COLS), lambda i, n, j: (n, i, j)),
                   pl.BlockSpec((None, tm, WIDTH), lambda i, n, j: (n, i, 0))],
        out_shape=[jax.ShapeDtypeStruct((T, IN_COLS), BF16), jax.ShapeDtypeStruct((N_BRANCH, T, D_MODEL), BF16),
                   jax.ShapeDtypeStruct((N_BRANCH, T, WIDTH), F32)],
        compiler_params=_params(3))(z, wb, p, dmerged)


def _branch_wgrad(z, dy):
    T = z.shape[1]
    tt = _token_tile(T)
    return _matmul(
        "branch_wgrad", z, dy, dims=TN, grid=(N_CHIP * N_BRANCH, 1, T // tt),
        a_spec=pl.BlockSpec((None, tt, WIDTH), lambda j, i, kk: (j % N_BRANCH, kk, 0)),
        b_spec=pl.BlockSpec((None, tt, BRANCH_COLS), lambda j, i, kk: (j % N_BRANCH, kk, j // N_BRANCH)),
        out_specs=[pl.BlockSpec((None, None, WIDTH, BRANCH_COLS), lambda j, i, kk: (j // N_BRANCH, j % N_BRANCH, 0, 0))],
        out_shapes=[jax.ShapeDtypeStruct((N_CHIP, N_BRANCH, WIDTH, BRANCH_COLS), F32)],
        acc_shape=(WIDTH, BRANCH_COLS))[0]


def _layer_fwd(x, wts, small):
    win, wb, wo, w1, w2 = wts
    h = _rms_fwd("rms_mix", x, small["g_mix"])
    p = _mm_cols("in_proj", h, win, [F32])[0]
    o_hgrn, z, states = _hgrn_fwd(p, small["lbs"], small["g_hgrn_out"])
    z = _conv_fwd(p, small["w_conv"], z)
    z = _sg_fwd(p, small["sg_ln_g"], small["sg_ln_b"], small["w_sg"], small["b_sg"], z)
    merged = _merge_fwd(z, p, wb)
    x_mid = _mm_rows("out_proj", merged, wo, x)
    h2 = _rms_fwd("rms_ffn", x_mid, small["g_ffn"])
    a, s = _mm_cols("ff1", h2, w1, [F32, BF16], epilogue=lambda acc: (acc, jnp.square(jnp.maximum(acc, 0.0))))
    x_out = _mm_rows("ff2", s, w2, x_mid)
    saved = dict(x=x, h=h, p=p, o_hgrn=o_hgrn, z=z, states=states, merged=merged, x_mid=x_mid, h2=h2, a=a, s=s)
    return x_out, saved


def _layer_bwd(dx_out, sv, wts, small):
    win, wb, wo, w1, w2 = wts
    g = {}
    da = _mm_cols_t("ff2_dgrad", dx_out, w2, BF16, extra=(sv["a"],),
                    epilogue=lambda acc, a: (acc * 2.0 * jnp.maximum(a, 0.0),))
    g["w_ff2"] = _mm_wgrad("ff2_wgrad", sv["s"], dx_out, w2.shape[1], D_MODEL, True, False)
    g["w_ff1"] = _mm_wgrad("ff1_wgrad", sv["h2"], da, D_MODEL, w1.shape[2], False, True)
    dh2 = _mm_rows_t("ff1_dgrad", da, w1)
    dx_mid, g["g_ffn"] = _rms_bwd("rms_ffn_bwd", sv["x_mid"], small["g_ffn"], dh2, dx_out)
    dmerged = _mm_cols_t("out_proj_dgrad", dx_mid, wo, F32)
    g["w_o"] = _mm_wgrad("out_proj_wgrad", sv["merged"], dx_mid, wo.shape[1], D_MODEL, True, False)
    dp, dy, dz = _merge_bwd(sv["z"], sv["p"], wb, dmerged)
    g["w_branch"] = _branch_wgrad(sv["z"], dy)
    dp, g["lbs"], g["g_hgrn_out"] = _hgrn_bwd(sv["p"], sv["o_hgrn"], dz, sv["states"], small["lbs"],
                                              small["g_hgrn_out"], dp)
    dp, g["w_conv"] = _conv_bwd(sv["p"], small["w_conv"], dz, dp)
    dp, g["w_sg"], g["b_sg"], g["sg_ln_g"], g["sg_ln_b"] = _sg_bwd(
        sv["p"], small["sg_ln_g"], small["sg_ln_b"], small["w_sg"], small["b_sg"], dz, dp)
    g["w_in"] = _mm_wgrad("in_proj_wgrad", sv["h"], dp, D_MODEL, win.shape[2], False, True)
    dh = _mm_rows_t("in_proj_dgrad", dp, win)
    dx, g["g_mix"] = _rms_bwd("rms_mix_bwd", sv["x"], small["g_mix"], dh, dx_mid)
    return dx, g


def _mesh_pos():
    return lax.axis_index("x"), lax.axis_index("y"), lax.axis_index("c")


def _other_chips(x, y):
    return [(1 - x, y), (x, 1 - y), (1 - x, 1 - y)]


def _remote(src, dst, send_sems, recv_sems, k, to):
    return pltpu.make_async_remote_copy(src_ref=src, dst_ref=dst, send_sem=send_sems.at[k], recv_sem=recv_sems.at[k],
                                        device_id=to, device_id_type=MESH)


def _comm_call(name, body, ins, out_shapes, n_remote, in_place=False):
    scratch = [pltpu.SemaphoreType.DMA((n_remote,)), pltpu.SemaphoreType.DMA((n_remote,))]
    aliases = {t: t for t in range(len(ins))} if in_place else {}
    return pl.pallas_call(
        body, name=name, in_specs=[ANY] * len(ins), out_specs=[ANY] * len(out_shapes), out_shape=list(out_shapes),
        scratch_shapes=scratch, input_output_aliases=aliases)(*ins)


def _gather_chips(name, bufs):
    n = len(bufs)

    def body(*refs):
        outs = refs[n:2 * n]
        send_sems, recv_sems = refs[2 * n:]
        x, y, c = _mesh_pos()
        chips = _other_chips(x, y)
        sibling = (x, y, 1 - c)
        started = []

        def half(t, chip, cc):
            rh = bufs[t].shape[1] // 2
            return outs[t].at[2 * chip[0] + chip[1], pl.ds(cc * rh, rh), :]

        for t in range(n):
            for j, chip in enumerate(chips):
                cp = _remote(half(t, (x, y), c), half(t, (x, y), c), send_sems, recv_sems, 6 * t + j, (*chip, c))
                cp.start()
                started.append(cp)
        for t in range(n):
            for j, chip in enumerate(chips):
                _remote(half(t, chip, c), half(t, chip, c), send_sems, recv_sems, 6 * t + j, (*chip, c)).wait_recv()
                fw = _remote(half(t, chip, c), half(t, chip, c), send_sems, recv_sems, 6 * t + 3 + j, sibling)
                fw.start()
                started.append(fw)
        for t in range(n):
            for j, chip in enumerate(chips):
                _remote(half(t, chip, 1 - c), half(t, chip, 1 - c), send_sems, recv_sems, 6 * t + 3 + j, sibling).wait_recv()
        for cp in started:
            cp.wait_send()

    out_shapes = [jax.ShapeDtypeStruct(b.shape, b.dtype) for b in bufs]
    return _comm_call(name, body, bufs, out_shapes, 6 * n, in_place=True)


def _sibling_swap_halves(name, grads):
    n = len(grads)

    def body(*refs):
        ins, outs = refs[:n], refs[n:2 * n]
        send_sems, recv_sems = refs[2 * n:]
        x, y, c = _mesh_pos()
        copies = []
        for t in range(n):
            rh = grads[t].shape[1] // 2
            cp = _remote(ins[t].at[:, pl.ds((1 - c) * rh, rh), :], outs[t], send_sems, recv_sems, t, (x, y, 1 - c))
            cp.start()
            copies.append(cp)
        for cp in copies:
            cp.wait_send()
            cp.wait_recv()

    out_shapes = [jax.ShapeDtypeStruct((N_CHIP, g.shape[1] // 2, g.shape[2]), g.dtype) for g in grads]
    return _comm_call(name, body, grads, out_shapes, n)


def _chip_exchange(name, parts):
    n = len(parts)

    def body(*refs):
        ins, outs = refs[:n], refs[n:2 * n]
        send_sems, recv_sems = refs[2 * n:]
        x, y, c = _mesh_pos()
        copies = []
        for t in range(n):
            for j, chip in enumerate(_other_chips(x, y)):
                cp = _remote(ins[t].at[2 * chip[0] + chip[1]], outs[t].at[j], send_sems, recv_sems, 3 * t + j, (*chip, c))
                cp.start()
                copies.append(cp)
        for cp in copies:
            cp.wait_send()
            cp.wait_recv()

    out_shapes = [jax.ShapeDtypeStruct((3, *q.shape[1:]), q.dtype) for q in parts]
    return _comm_call(name, body, parts, out_shapes, 3 * n)


def _sibling_gather(name, bufs):
    n = len(bufs)

    def body(*refs):
        outs = refs[n:2 * n]
        send_sems, recv_sems = refs[2 * n:]
        x, y, c = _mesh_pos()
        copies = []
        for t in range(n):
            cp = _remote(outs[t].at[c], outs[t].at[c], send_sems, recv_sems, t, (x, y, 1 - c))
            cp.start()
            copies.append(cp)
        for cp in copies:
            cp.wait_send()
            cp.wait_recv()

    out_shapes = [jax.ShapeDtypeStruct(b.shape, b.dtype) for b in bufs]
    return _comm_call(name, body, bufs, out_shapes, n, in_place=True)


def _gather_all(name, block, slot):
    buf = lax.dynamic_update_slice(jnp.zeros((8, *block.shape), block.dtype), block[None], (slot, 0, 0))

    def body(in_ref, out_ref, send_sems, recv_sems):
        del in_ref
        x, y, c = _mesh_pos()
        chips = _other_chips(x, y)
        sibling = (x, y, 1 - c)
        slot_of = lambda px, py, pc: out_ref.at[4 * px + 2 * py + pc]
        started = [_remote(slot_of(x, y, c), slot_of(x, y, c), send_sems, recv_sems, 0, sibling)]
        started += [_remote(slot_of(x, y, c), slot_of(x, y, c), send_sems, recv_sems, 1 + j, (*chip, c))
                    for j, chip in enumerate(chips)]
        for cp in started:
            cp.start()
        for j, chip in enumerate(chips):
            _remote(slot_of(*chip, c), slot_of(*chip, c), send_sems, recv_sems, 1 + j, (*chip, c)).wait_recv()
            fw = _remote(slot_of(*chip, c), slot_of(*chip, c), send_sems, recv_sems, 4 + j, sibling)
            fw.start()
            started.append(fw)
        _remote(slot_of(x, y, 1 - c), slot_of(x, y, 1 - c), send_sems, recv_sems, 0, sibling).wait_recv()
        for j, chip in enumerate(chips):
            _remote(slot_of(*chip, 1 - c), slot_of(*chip, 1 - c), send_sems, recv_sems, 4 + j, sibling).wait_recv()
        for cp in started:
            cp.wait_send()

    return _comm_call(name, body, [buf], [jax.ShapeDtypeStruct(buf.shape, buf.dtype)], 7, in_place=True)[0]


def _row_tile(rows, cols):
    cap = max(SUBLANES, ELEMWISE_BLOCK_BYTES // (4 * cols))
    tr = rows
    while tr > cap and tr % 2 == 0:
        tr //= 2
    return tr


def _pair_sum(name, grad, recv, core):
    _, rh, cols = recv.shape
    tr = _row_tile(rh, cols)
    per = rh // tr

    def body(core_ref, g_ref, r_ref, out32_ref, out16_ref):
        del core_ref
        s = g_ref[...] + r_ref[...]
        out32_ref[...] = s
        out16_ref[...] = s.astype(BF16)

    blk = pl.BlockSpec((None, tr, cols), lambda k, i, core_ref: (k, i, 0))
    return pl.pallas_call(
        body, name=name,
        grid_spec=pltpu.PrefetchScalarGridSpec(
            num_scalar_prefetch=1, grid=(N_CHIP, per),
            in_specs=[pl.BlockSpec((None, tr, cols), lambda k, i, core_ref: (k, core_ref[0] * per + i, 0)), blk],
            out_specs=[blk, blk]),
        out_shape=[jax.ShapeDtypeStruct(recv.shape, F32), jax.ShapeDtypeStruct(recv.shape, BF16)],
        compiler_params=_params(2))(core, grad, recv)


def _chip_sum(name, part32, recv, pos):
    _, rh, cols = part32.shape
    tr = _row_tile(rh, cols)

    def body(pos_ref, own_ref, r_ref, out_ref):
        del pos_ref
        out_ref[...] = ((own_ref[...] + r_ref[0].astype(F32)) + r_ref[1].astype(F32)) + r_ref[2].astype(F32)

    return pl.pallas_call(
        body, name=name,
        grid_spec=pltpu.PrefetchScalarGridSpec(
            num_scalar_prefetch=1, grid=(rh // tr,),
            in_specs=[pl.BlockSpec((None, tr, cols), lambda i, pos_ref: (pos_ref[0], i, 0)),
                      pl.BlockSpec((3, tr, cols), lambda i, pos_ref: (0, i, 0))],
            out_specs=pl.BlockSpec((None, tr, cols), lambda i, pos_ref: (pos_ref[1], i, 0))),
        out_shape=jax.ShapeDtypeStruct((2, rh, cols), F32), compiler_params=_params(1))(pos, part32, recv)


def _cast_into_slot(name, w, layer, pos):
    _, rows, cols = w.shape
    tr = _row_tile(rows, cols)

    def body(pos_ref, w_ref, out_ref):
        del pos_ref
        out_ref[...] = w_ref[...].astype(BF16)

    return pl.pallas_call(
        body, name=name,
        grid_spec=pltpu.PrefetchScalarGridSpec(
            num_scalar_prefetch=1, grid=(rows // tr,),
            in_specs=[pl.BlockSpec((None, tr, cols), lambda i, pos_ref: (layer, i, 0))],
            out_specs=pl.BlockSpec((None, tr, cols), lambda i, pos_ref: (pos_ref[0], i, 0))),
        out_shape=jax.ShapeDtypeStruct((N_CHIP, rows, cols), BF16), compiler_params=_params(1))(pos, w)


def _adamw_math(w, g, m, v):
    m = ADAM_B1 * m + (1.0 - ADAM_B1) * g
    v = ADAM_B2 * v + (1.0 - ADAM_B2) * jnp.square(g)
    m_hat = m / (1.0 - ADAM_B1 ** ADAM_STEP)
    v_hat = v / (1.0 - ADAM_B2 ** ADAM_STEP)
    delta = -ADAM_LR * (m_hat / (jnp.sqrt(v_hat) + ADAM_EPS) + ADAM_WD * w)
    return delta, m, v


def _adamw_layers(name, w, m, v, grads):
    _, rows, cols = w.shape
    tr = _row_tile(rows, cols)

    def body(w_ref, m_ref, v_ref, *rest):
        g_refs, (grad_ref, d_ref, nm_ref, nv_ref) = rest[:DEPTH], rest[DEPTH:]
        layer = pl.program_id(0)
        g = g_refs[0][...]
        for l in range(1, DEPTH):
            g = jnp.where(layer == l, g_refs[l][...], g)
        grad_ref[...] = g
        d_ref[...], nm_ref[...], nv_ref[...] = _adamw_math(w_ref[...], g, m_ref[...], v_ref[...])

    blk = pl.BlockSpec((None, tr, cols), lambda l, i: (l, i, 0))
    g_spec = lambda k: pl.BlockSpec((tr, cols), lambda l, i: (jnp.where(l == k, i, 0), 0))
    return pl.pallas_call(
        body, name=name, grid=(DEPTH, rows // tr),
        in_specs=[blk, blk, blk] + [g_spec(k) for k in range(DEPTH)], out_specs=[blk] * 4,
        out_shape=[jax.ShapeDtypeStruct(w.shape, F32)] * 4, compiler_params=_params(2))(w, m, v, *grads)


def _sum_devices(gathered):
    _, rows, cols = gathered.shape

    def body(g_ref, out_ref):
        s = g_ref[0]
        for d in range(1, 8):
            s = s + g_ref[d]
        out_ref[...] = s

    return pl.pallas_call(body, name="sum_devices", out_shape=jax.ShapeDtypeStruct((rows, cols), F32),
                          compiler_params=pltpu.CompilerParams(vmem_limit_bytes=VMEM_LIMIT_BYTES))(gathered)


def _adamw_flat(w, g, m, v):
    def body(w_ref, g_ref, m_ref, v_ref, d_ref, nm_ref, nv_ref):
        d_ref[...], nm_ref[...], nv_ref[...] = _adamw_math(w_ref[...], g_ref[...], m_ref[...], v_ref[...])

    return pl.pallas_call(body, name="adamw_small", out_shape=[jax.ShapeDtypeStruct(w.shape, F32)] * 3,
                          compiler_params=pltpu.CompilerParams(vmem_limit_bytes=VMEM_LIMIT_BYTES))(w, g, m, v)


SMALL_NAMES = ["g_mix", "lower_bounds", "g_hgrn_out", "w_conv", "sg_ln_g", "sg_ln_b", "w_sg", "b_sg", "g_ffn", "g_final"]
BIG_NAMES = ["w_in", "w_branch", "w_o", "w_ff1", "w_ff2"]
WEIGHT_ORDER = ["w_in", "g_mix", "lower_bounds", "g_hgrn_out", "w_conv", "sg_ln_g", "sg_ln_b", "w_sg", "b_sg", "w_branch",
                "w_o", "g_ffn", "w_ff1", "w_ff2", "g_final"]


def _pack(arrays):
    return jnp.concatenate([a.reshape(-1, LANES) for a in arrays], axis=0)


def _unpack(flat, shapes):
    out, row = [], 0
    for s in shapes:
        n = int(np.prod(s)) // LANES
        out.append(flat[row:row + n].reshape(s))
        row += n
    return out


def _as_2d(name, a):
    return a.reshape(DEPTH, N_BRANCH * WIDTH, BRANCH_COLS) if name == "w_branch" else a


def kernel(x, w_in, g_mix, lower_bounds, g_hgrn_out, w_conv, sg_ln_g, sg_ln_b, w_sg, b_sg, w_branch, w_o, g_ffn, w_ff1, w_ff2, g_final, loss_target, m_w_in, m_g_mix, m_lower_bounds, m_g_hgrn_out, m_w_conv, m_sg_ln_g, m_sg_ln_b, m_w_sg, m_b_sg, m_w_branch, m_w_o, m_g_ffn, m_w_ff1, m_w_ff2, m_g_final, v_w_in, v_g_mix, v_lower_bounds, v_g_hgrn_out, v_w_conv, v_sg_ln_g, v_sg_ln_b, v_w_sg, v_b_sg, v_w_branch, v_w_o, v_g_ffn, v_w_ff1, v_w_ff2, v_g_final):
    weights = dict(w_in=w_in, g_mix=g_mix, lower_bounds=lower_bounds, g_hgrn_out=g_hgrn_out, w_conv=w_conv,
                   sg_ln_g=sg_ln_g, sg_ln_b=sg_ln_b, w_sg=w_sg, b_sg=b_sg, w_branch=w_branch, w_o=w_o, g_ffn=g_ffn,
                   w_ff1=w_ff1, w_ff2=w_ff2, g_final=g_final)
    mom1 = dict(w_in=m_w_in, g_mix=m_g_mix, lower_bounds=m_lower_bounds, g_hgrn_out=m_g_hgrn_out, w_conv=m_w_conv,
                sg_ln_g=m_sg_ln_g, sg_ln_b=m_sg_ln_b, w_sg=m_w_sg, b_sg=m_b_sg, w_branch=m_w_branch, w_o=m_w_o,
                g_ffn=m_g_ffn, w_ff1=m_w_ff1, w_ff2=m_w_ff2, g_final=m_g_final)
    mom2 = dict(w_in=v_w_in, g_mix=v_g_mix, lower_bounds=v_lower_bounds, g_hgrn_out=v_g_hgrn_out, w_conv=v_w_conv,
                sg_ln_g=v_sg_ln_g, sg_ln_b=v_sg_ln_b, w_sg=v_w_sg, b_sg=v_b_sg, w_branch=v_w_branch, w_o=v_w_o,
                g_ffn=v_g_ffn, w_ff1=v_w_ff1, w_ff2=v_w_ff2, g_final=v_g_final)
    xi, yi, ci = _mesh_pos()
    core = jnp.reshape(ci, (1,)).astype(jnp.int32)
    pos = jnp.stack([2 * xi + yi, ci]).astype(jnp.int32)
    device = 4 * xi + 2 * yi + ci
    conv_cols = w_conv.shape[2]

    conv_all = _gather_all("gather_w_conv", w_conv.reshape(DEPTH * CONV_K, conv_cols), device)
    conv_full = conv_all.reshape(N_CHIP, 2, DEPTH, CONV_K, conv_cols)[:, 0].transpose(1, 2, 0, 3).reshape(DEPTH, CONV_K, WIDTH)
    lbs = _lbs_fwd(lower_bounds)

    act = x[0]
    layers = []
    for l in range(DEPTH):
        gathered = _gather_chips("gather_weights", [_cast_into_slot("cast_" + n, _as_2d(n, weights[n]), l, pos)
                                                    for n in BIG_NAMES])
        wts = [gathered[0], gathered[1].reshape(N_CHIP, N_BRANCH, WIDTH, BRANCH_COLS), *gathered[2:]]
        small = dict(g_mix=g_mix[l:l + 1], lbs=lbs[l:l + 1], g_hgrn_out=g_hgrn_out[l:l + 1], w_conv=conv_full[l],
                     sg_ln_g=sg_ln_g[l:l + 1], sg_ln_b=sg_ln_b[l:l + 1], w_sg=w_sg[l],
                     b_sg=b_sg[l].reshape(SG_GROUPS, SG_CHUNK, 1), g_ffn=g_ffn[l:l + 1])
        act, saved = _layer_fwd(act, wts, small)
        layers.append((wts, small, saved))
    loss_blk, dact, dg_final = _loss_head(act, g_final.reshape(1, D_MODEL), loss_target[0])

    reduced = {n: [None] * DEPTH for n in BIG_NAMES}
    small_grads = [None] * DEPTH
    for l in reversed(range(DEPTH)):
        wts, small, saved = layers[l]
        dact, g = _layer_bwd(dact, saved, wts, small)
        small_grads[l] = g
        full = [g[n].reshape(N_CHIP, -1, g[n].shape[-1]) for n in BIG_NAMES]
        from_sibling = _sibling_swap_halves("grad_pair_swap", full)
        pair = [_pair_sum("grad_pair_sum_" + n, f, r, core) for n, f, r in zip(BIG_NAMES, full, from_sibling)]
        from_chips = _chip_exchange("grad_chip_exchange", [p16 for _, p16 in pair])
        halves = [_chip_sum("grad_chip_sum_" + n, p32, r, pos) for n, (p32, _), r in zip(BIG_NAMES, pair, from_chips)]
        both = _sibling_gather("grad_half_gather", halves)
        for n, b in zip(BIG_NAMES, both):
            reduced[n][l] = b.reshape(-1, b.shape[-1])
    grad_x = dact[None]

    stack = lambda key, rows=None: jnp.stack([small_grads[l][key][0] if rows is None else small_grads[l][key][:rows]
                                              for l in range(DEPTH)])
    local_small = dict(
        g_mix=stack("g_mix"), lower_bounds=stack("lbs"), g_hgrn_out=stack("g_hgrn_out"), w_conv=stack("w_conv", CONV_K),
        sg_ln_g=stack("sg_ln_g"), sg_ln_b=stack("sg_ln_b"), w_sg=jnp.stack([small_grads[l]["w_sg"] for l in range(DEPTH)]),
        b_sg=jnp.stack([small_grads[l]["b_sg"].reshape(SG_GROUPS, SG_CHUNK) for l in range(DEPTH)]),
        g_ffn=stack("g_ffn"), g_final=dg_final[0])
    shapes = [local_small[n].shape for n in SMALL_NAMES] + [(SUBLANES, LANES)]
    summed = _sum_devices(_gather_all("gather_small_grads", _pack([local_small[n] for n in SMALL_NAMES] + [loss_blk]), device))
    parts = _unpack(summed, shapes)
    loss = parts[-1][0, 0]
    small_grad = dict(zip(SMALL_NAMES, parts[:-1]))
    small_grad["lower_bounds"] = _lbs_bwd(lower_bounds, small_grad["lower_bounds"])
    small_grad["w_conv"] = lax.dynamic_slice_in_dim(small_grad["w_conv"], pos[0] * conv_cols, conv_cols, axis=2)
    g_flat = _pack([small_grad[n] for n in SMALL_NAMES])
    d_flat, m_flat, v_flat = _adamw_flat(_pack([weights[n] for n in SMALL_NAMES]), g_flat,
                                         _pack([mom1[n] for n in SMALL_NAMES]), _pack([mom2[n] for n in SMALL_NAMES]))
    small_shapes = [weights[n].shape for n in SMALL_NAMES]
    grads = dict(small_grad)
    delta = dict(zip(SMALL_NAMES, _unpack(d_flat, small_shapes)))
    new_m = dict(zip(SMALL_NAMES, _unpack(m_flat, small_shapes)))
    new_v = dict(zip(SMALL_NAMES, _unpack(v_flat, small_shapes)))

    for n in BIG_NAMES:
        outs = _adamw_layers("adamw_" + n, _as_2d(n, weights[n]), _as_2d(n, mom1[n]), _as_2d(n, mom2[n]), reduced[n])
        grads[n], delta[n], new_m[n], new_v[n] = [o.reshape(weights[n].shape) for o in outs]

    return (loss, grad_x, *[grads[n] for n in WEIGHT_ORDER], *[delta[n] for n in WEIGHT_ORDER],
            *[new_m[n] for n in WEIGHT_ORDER], *[new_v[n] for n in WEIGHT_ORDER])
```

```python
import numpy as np
import jax
import jax.numpy as jnp
from jax import lax
from jax.experimental import pallas as pl
from jax.experimental.pallas import tpu as pltpu

F32, BF16 = jnp.float32, jnp.bfloat16

D_MODEL = 1024
WIDTH = 512
N_BRANCH = 3
N_HEAD = 4
HEAD = 128
H_CHUNK = 64
CONV_K = 3
SG_CHUNK = 128
SG_GROUPS = 4
D_FF = 4096
DEPTH = 4
N_CHIP = 4
IN_COLS = 9 * WIDTH + N_BRANCH * D_MODEL
GATE_COL0 = 9 * WIDTH
LB_FLOOR = 1e-30
NORM_EPS = 1e-6
LN_EPS = 1e-5
ADAM_LR, ADAM_B1, ADAM_B2, ADAM_EPS, ADAM_WD, ADAM_STEP = 0.001, 0.9, 0.999, 1e-08, 0.01, 10

VMEM_LIMIT_BYTES = 48 * 1024 * 1024
SUBLANES, LANES = 8, 128
ELEMWISE_BLOCK_BYTES = 2 * 1024 * 1024

NN = (((1,), (0,)), ((), ()))
NT = (((1,), (1,)), ((), ()))
TN = (((0,), (0,)), ((), ()))
MESH = pl.DeviceIdType.MESH
ANY = pl.BlockSpec(memory_space=pl.ANY)


def _dot(a, b, dims=NN):
    return lax.dot_general(a.astype(BF16), b.astype(BF16), dims, preferred_element_type=F32)


def _params(n_axes):
    return pltpu.CompilerParams(dimension_semantics=("arbitrary",) * n_axes, vmem_limit_bytes=VMEM_LIMIT_BYTES)


def _row0(part, rows=SUBLANES):
    r = lax.broadcasted_iota(jnp.int32, (rows, part.shape[1]), 0)
    return jnp.where(r == 0, part, 0.0)


def _token_tile(T):
    return min(512, T)


def _matmul(name, a, b, *, dims, grid, a_spec, b_spec, out_specs, out_shapes, acc_shape,
            extra=(), extra_specs=(), epilogue=None, after=()):
    nk = grid[2]
    n_extra, n_out, n_in = len(extra), len(out_shapes), 2 + len(extra) + len(after)

    def body(*refs):
        a_ref, b_ref = refs[0], refs[1]
        ex = refs[2:2 + n_extra]
        outs = refs[n_in:n_in + n_out]
        acc = refs[-1]
        kk = pl.program_id(2)
        part = _dot(a_ref[...], b_ref[...], dims)

        @pl.when(kk == 0)
        def _():
            acc[...] = part

        @pl.when(kk > 0)
        def _():
            acc[...] += part

        @pl.when(kk == nk - 1)
        def _():
            res = epilogue(acc[...], *[e[...] for e in ex]) if epilogue else (acc[...],)
            for o, r in zip(outs, res):
                o[...] = r.astype(o.dtype)

    return pl.pallas_call(
        body, name=name, grid=grid,
        in_specs=[a_spec, b_spec, *extra_specs, *[ANY] * len(after)], out_specs=list(out_specs),
        out_shape=list(out_shapes), scratch_shapes=[pltpu.VMEM(acc_shape, F32)], compiler_params=_params(3),
    )(a, b, *extra, *after)


def _mm_cols(name, a, w, out_dtypes, epilogue=None, extra=()):
    T, K = a.shape
    N = w.shape[2]
    tm = _token_tile(T)
    blk = pl.BlockSpec((tm, N), lambda j, i, kk: (i, j))
    return _matmul(
        name, a, w, dims=NN, grid=(N_CHIP, T // tm, 1),
        a_spec=pl.BlockSpec((tm, K), lambda j, i, kk: (i, 0)),
        b_spec=pl.BlockSpec((None, K, N), lambda j, i, kk: (j, 0, 0)),
        out_specs=[blk] * len(out_dtypes),
        out_shapes=[jax.ShapeDtypeStruct((T, N_CHIP * N), dt) for dt in out_dtypes],
        acc_shape=(tm, N), extra=extra, extra_specs=[blk] * len(extra), epilogue=epilogue)


def _mm_rows(name, a, w, res):
    T = a.shape[0]
    K, N = w.shape[1], w.shape[2]
    tm = _token_tile(T)
    blk = pl.BlockSpec((tm, N), lambda i, j, kk: (i, 0))
    return _matmul(
        name, a, w, dims=NN, grid=(T // tm, 1, N_CHIP),
        a_spec=pl.BlockSpec((tm, K), lambda i, j, kk: (i, kk)),
        b_spec=pl.BlockSpec((None, K, N), lambda i, j, kk: (kk, 0, 0)),
        out_specs=[blk], out_shapes=[jax.ShapeDtypeStruct((T, N), F32)], acc_shape=(tm, N),
        extra=(res,), extra_specs=[blk], epilogue=lambda acc, r: (acc + r,))[0]


def _mm_cols_t(name, g, w, out_dtype, epilogue=None, extra=(), after=()):
    T, N = g.shape
    K = w.shape[1]
    tm = _token_tile(T)
    blk = pl.BlockSpec((tm, K), lambda j, i, kk: (i, j))
    return _matmul(
        name, g, w, dims=NT, grid=(N_CHIP, T // tm, 1),
        a_spec=pl.BlockSpec((tm, N), lambda j, i, kk: (i, 0)),
        b_spec=pl.BlockSpec((None, K, N), lambda j, i, kk: (j, 0, 0)),
        out_specs=[blk], out_shapes=[jax.ShapeDtypeStruct((T, N_CHIP * K), out_dtype)], acc_shape=(tm, K),
        extra=extra, extra_specs=[blk] * len(extra), epilogue=epilogue, after=after)[0]


def _mm_rows_t(name, g, w):
    T = g.shape[0]
    K, N = w.shape[1], w.shape[2]
    tm = _token_tile(T)
    blk = pl.BlockSpec((tm, K), lambda i, j, kk: (i, 0))
    return _matmul(
        name, g, w, dims=NT, grid=(T // tm, 1, N_CHIP),
        a_spec=pl.BlockSpec((tm, N), lambda i, j, kk: (i, kk)),
        b_spec=pl.BlockSpec((None, K, N), lambda i, j, kk: (kk, 0, 0)),
        out_specs=[blk], out_shapes=[jax.ShapeDtypeStruct((T, K), F32)], acc_shape=(tm, K))[0]


def _mm_wgrad(name, a, g, a_cols, g_cols, a_blocked, g_blocked):
    T = a.shape[0]
    tt = _token_tile(T)
    return _matmul(
        name, a, g, dims=TN, grid=(N_CHIP, 1, T // tt),
        a_spec=pl.BlockSpec((tt, a_cols), (lambda j, i, kk: (kk, j)) if a_blocked else (lambda j, i, kk: (kk, 0))),
        b_spec=pl.BlockSpec((tt, g_cols), (lambda j, i, kk: (kk, j)) if g_blocked else (lambda j, i, kk: (kk, 0))),
        out_specs=[pl.BlockSpec((None, a_cols, g_cols), lambda j, i, kk: (j, 0, 0))],
        out_shapes=[jax.ShapeDtypeStruct((N_CHIP, a_cols, g_cols), F32)], acc_shape=(a_cols, g_cols))[0]


def _rms_fwd(name, x, g):
    T, Dm = x.shape
    tm = min(256, T)

    def body(x_ref, g_ref, h_ref):
        xv = x_ref[...]
        r = lax.rsqrt(jnp.mean(xv * xv, axis=-1, keepdims=True) + NORM_EPS)
        h_ref[...] = (xv * r * g_ref[...]).astype(BF16)

    return pl.pallas_call(
        body, name=name, grid=(T // tm,),
        in_specs=[pl.BlockSpec((tm, Dm), lambda i: (i, 0)), pl.BlockSpec((1, Dm), lambda i: (0, 0))],
        out_specs=pl.BlockSpec((tm, Dm), lambda i: (i, 0)),
        out_shape=jax.ShapeDtypeStruct((T, Dm), BF16), compiler_params=_params(1))(x, g)


def _rms_bwd(name, x, g, dh, dres):
    T, Dm = x.shape
    tm = min(256, T)

    def body(x_ref, g_ref, dh_ref, dres_ref, dx_ref, dg_ref):
        xv = x_ref[...]
        r = lax.rsqrt(jnp.mean(xv * xv, axis=-1, keepdims=True) + NORM_EPS)
        xn = xv * r
        dhv = dh_ref[...]
        dxn = dhv * g_ref[...]
        dx_ref[...] = dres_ref[...] + r * (dxn - xn * jnp.mean(dxn * xn, axis=-1, keepdims=True))

        @pl.when(pl.program_id(0) == 0)
        def _():
            dg_ref[...] = jnp.zeros_like(dg_ref)

        dg_ref[...] += _row0(jnp.sum(dhv * xn, axis=0, keepdims=True))

    tile = pl.BlockSpec((tm, Dm), lambda i: (i, 0))
    return pl.pallas_call(
        body, name=name, grid=(T // tm,),
        in_specs=[tile, pl.BlockSpec((1, Dm), lambda i: (0, 0)), tile, tile],
        out_specs=[tile, pl.BlockSpec((SUBLANES, Dm), lambda i: (0, 0))],
        out_shape=[jax.ShapeDtypeStruct((T, Dm), F32), jax.ShapeDtypeStruct((SUBLANES, Dm), F32)],
        compiler_params=_params(1))(x, g, dh, dres)


def _loss_head(x, g, tgt):
    T, Dm = x.shape
    tm = min(256, T)

    def body(x_ref, g_ref, t_ref, loss_ref, dx_ref, dg_ref):
        xv = x_ref[...]
        gv = g_ref[...]
        r = lax.rsqrt(jnp.mean(xv * xv, axis=-1, keepdims=True) + NORM_EPS)
        xn = xv * r
        err = xn * gv - t_ref[...]
        dy = err * (1.0 / Dm)
        dxn = dy * gv
        dx_ref[...] = r * (dxn - xn * jnp.mean(dxn * xn, axis=-1, keepdims=True))

        @pl.when(pl.program_id(0) == 0)
        def _():
            dg_ref[...] = jnp.zeros_like(dg_ref)
            loss_ref[...] = jnp.zeros_like(loss_ref)

        dg_ref[...] += _row0(jnp.sum(dy * xn, axis=0, keepdims=True))
        part = jnp.sum(jnp.sum(err * err, axis=-1, keepdims=True), axis=0, keepdims=True) * (0.5 / Dm)
        loss_ref[...] += jnp.broadcast_to(part, loss_ref.shape)

    tile = pl.BlockSpec((tm, Dm), lambda i: (i, 0))
    return pl.pallas_call(
        body, name="loss_head", grid=(T // tm,),
        in_specs=[tile, pl.BlockSpec((1, Dm), lambda i: (0, 0)), tile],
        out_specs=[pl.BlockSpec((SUBLANES, LANES), lambda i: (0, 0)), tile,
                   pl.BlockSpec((SUBLANES, Dm), lambda i: (0, 0))],
        out_shape=[jax.ShapeDtypeStruct((SUBLANES, LANES), F32), jax.ShapeDtypeStruct((T, Dm), F32),
                   jax.ShapeDtypeStruct((SUBLANES, Dm), F32)],
        compiler_params=_params(1))(x, g, tgt)


def _softmax_rows(lb_ref):
    rows = [lb_ref[pl.ds(i, 1), :] for i in range(DEPTH)]
    mx = rows[0]
    for r in rows[1:]:
        mx = jnp.maximum(mx, r)
    es = [jnp.exp(r - mx) for r in rows]
    tot = es[0]
    for e in es[1:]:
        tot = tot + e
    return [e / tot for e in es]


def _lbs_fwd(lower_bounds):
    def body(lb_ref, out_ref):
        sm = _softmax_rows(lb_ref)
        run = jnp.zeros_like(sm[0])
        out_ref[pl.ds(0, 1), :] = run
        for i in range(1, DEPTH):
            run = run + sm[i]
            out_ref[pl.ds(i, 1), :] = run

    return pl.pallas_call(body, name="lbs_fwd", out_shape=jax.ShapeDtypeStruct(lower_bounds.shape, F32))(lower_bounds)


def _lbs_bwd(lower_bounds, dlbs):
    def body(lb_ref, d_ref, out_ref):
        sm = _softmax_rows(lb_ref)
        dsm = [jnp.zeros_like(sm[0])]
        for i in range(1, DEPTH):
            acc = d_ref[pl.ds(i, 1), :]
            for l in range(i + 1, DEPTH):
                acc = acc + d_ref[pl.ds(l, 1), :]
            dsm.append(acc)
        inner = dsm[0] * sm[0]
        for i in range(1, DEPTH):
            inner = inner + dsm[i] * sm[i]
        for i in range(DEPTH):
            out_ref[pl.ds(i, 1), :] = sm[i] * (dsm[i] - inner)

    return pl.pallas_call(body, name="lbs_bwd", out_shape=jax.ShapeDtypeStruct(lower_bounds.shape, F32))(lower_bounds, dlbs)


N_LEVEL = 6


def _hgrn_consts():
    L = H_CHUNK
    t = np.arange(L)
    blocks = [(t[:, None] >= t[None, :]).astype(np.float32)]
    masks = []
    m = L // 2
    while m >= 1:
        blk, pos = t // (2 * m), t % (2 * m)
        start = blk * 2 * m
        mat = np.zeros((L, L), np.float32)
        for r in range(L):
            if pos[r] >= m:
                mat[r, start[r] + m:r + 1] = 1.0
            else:
                mat[r, r + 1:start[r] + m] = -1.0
        blocks.append(mat)
        masks.append(((blk[:, None] == blk[None, :]) & (pos[:, None] >= m) & (pos[None, :] < m)).astype(np.float32))
        m //= 2
    blocks.append(np.ones((L, L), np.float32))
    return jnp.asarray(np.concatenate(blocks, 0), BF16), jnp.asarray(np.stack(masks), F32)


def _hgrn_core(qraw, fp, lb, sum_mat, mask_ref):
    L = H_CHUNK
    sq = jax.nn.sigmoid(qraw)
    q = qraw * sq
    sneg = jax.nn.sigmoid(-fp)
    log_sig = jnp.minimum(fp, 0.0) - jnp.log1p(jnp.exp(-jnp.abs(fp)))
    a1 = jnp.log(jnp.maximum(lb, LB_FLOOR))
    a2 = jnp.log1p(-lb) + log_sig
    logf = jnp.maximum(a1, a2) + jnp.log1p(jnp.exp(-jnp.abs(a1 - a2)))
    w1 = jnp.exp(a1 - logf)
    w2 = jnp.exp(a2 - logf)
    k = (1.0 - lb) * sneg
    hi = logf.astype(BF16)
    r1 = logf - hi.astype(F32)
    mid = r1.astype(BF16)
    lo = (r1 - mid.astype(F32)).astype(BF16)
    sums = lax.dot_general(sum_mat, jnp.concatenate([hi, mid, lo], axis=1), NN, preferred_element_type=F32)
    sums = sums[:, 0:HEAD] + sums[:, HEAD:2 * HEAD] + sums[:, 2 * HEAD:3 * HEAD]
    b = sums[0:L]
    b_last = sums[(N_LEVEL + 1) * L:(N_LEVEL + 2) * L]
    eye = lax.broadcasted_iota(jnp.int32, (L, L), 0) == lax.broadcasted_iota(jnp.int32, (L, L), 1)
    attn = jnp.where(eye, jnp.sum(q * k, axis=1, keepdims=True), 0.0)
    fa, fb, ea, eb = [], [], [], []
    for l in range(N_LEVEL):
        d = sums[(l + 1) * L:(l + 2) * L]
        e_a = jnp.exp(jnp.minimum(d, 0.0))
        e_b = jnp.exp(jnp.minimum(-d, 0.0))
        a_l, b_l = q * e_a, k * e_b
        attn = attn + mask_ref[l] * _dot(a_l, b_l, NT)
        fa.append(a_l), fb.append(b_l), ea.append(e_a), eb.append(e_b)
    return dict(sq=sq, q=q, sneg=sneg, logf=logf, w1=w1, w2=w2, k=k, b=b, b_last=b_last, attn=attn,
                fa=fa, fb=fb, ea=ea, eb=eb)


def _hgrn_fwd(p, lbrow, gout):
    T = p.shape[0]
    nch = T // H_CHUNK
    sum_mat, masks = _hgrn_consts()

    def body(p_ref, lb_ref, g_ref, m_ref, mask_ref, o_ref, z_ref, st_ref, state):
        @pl.when(pl.program_id(0) == 0)
        def _():
            state[...] = jnp.zeros_like(state)

        sum_m = m_ref[...]
        for h in range(N_HEAD):
            col = lambda part: pl.ds(part * WIDTH + h * HEAD, HEAD)
            hs = pl.ds(h * HEAD, HEAD)
            v = p_ref[:, col(2)]
            c = _hgrn_core(p_ref[:, col(0)], p_ref[:, col(1)], lb_ref[:, hs], sum_m, mask_ref)
            s0 = state[h]
            st_ref[h] = s0
            o = _dot(c["attn"], v) + _dot(c["q"] * jnp.exp(c["b"]), s0, NT)
            k_dec = c["k"] * jnp.exp(c["b_last"] - c["b"])
            decay = jnp.exp(jnp.max(c["b_last"], axis=0, keepdims=True))
            state[h] = s0 * decay + _dot(v, k_dec, TN)
            o_ref[:, hs] = o
            r = lax.rsqrt(jnp.mean(o * o, axis=-1, keepdims=True) + NORM_EPS)
            z_ref[:, hs] = (o * r * g_ref[:, hs] * jax.nn.sigmoid(p_ref[:, col(3)])).astype(BF16)

    full = lambda shape: pl.BlockSpec(shape, lambda c: (0,) * len(shape))
    return pl.pallas_call(
        body, name="hgrn_fwd", grid=(nch,),
        in_specs=[pl.BlockSpec((H_CHUNK, 4 * WIDTH), lambda c: (c, 0)), full((1, WIDTH)), full((1, WIDTH)),
                  full(sum_mat.shape), full(masks.shape)],
        out_specs=[pl.BlockSpec((H_CHUNK, WIDTH), lambda c: (c, 0)),
                   pl.BlockSpec((None, H_CHUNK, WIDTH), lambda c: (0, c, 0)),
                   pl.BlockSpec((None, N_HEAD, HEAD, HEAD), lambda c: (c, 0, 0, 0))],
        out_shape=[jax.ShapeDtypeStruct((T, WIDTH), F32), jax.ShapeDtypeStruct((N_BRANCH, T, WIDTH), BF16),
                   jax.ShapeDtypeStruct((nch, N_HEAD, HEAD, HEAD), F32)],
        scratch_shapes=[pltpu.VMEM((N_HEAD, HEAD, HEAD), F32)], compiler_params=_params(1),
    )(p, lbrow, gout, sum_mat, masks)


def _hgrn_bwd(p, o_saved, dz, states, lbrow, gout, dp):
    T = p.shape[0]
    nch = T // H_CHUNK
    L = H_CHUNK
    sum_mat, masks = _hgrn_consts()

    def body(p_ref, o_ref, dz_ref, st_ref, lb_ref, g_ref, m_ref, mask_ref, dp_in, dp_ref, dlb_ref, dg_ref, dstate):
        del dp_in

        @pl.when(pl.program_id(0) == 0)
        def _():
            dstate[...] = jnp.zeros_like(dstate)
            dlb_ref[...] = jnp.zeros_like(dlb_ref)
            dg_ref[...] = jnp.zeros_like(dg_ref)

        sum_m = m_ref[...]
        for h in range(N_HEAD):
            col = lambda part: pl.ds(part * WIDTH + h * HEAD, HEAD)
            hs = pl.ds(h * HEAD, HEAD)
            qraw, fp, v, go = p_ref[:, col(0)], p_ref[:, col(1)], p_ref[:, col(2)], p_ref[:, col(3)]
            lb, g = lb_ref[:, hs], g_ref[:, hs]
            c = _hgrn_core(qraw, fp, lb, sum_m, mask_ref)
            q, k, b, b_last = c["q"], c["k"], c["b"], c["b_last"]
            s0, ds1 = st_ref[h], dstate[h]
            e_b = jnp.exp(b)
            q_dec = q * e_b
            e_bl = jnp.exp(b_last - b)
            k_dec = k * e_bl
            decay = jnp.exp(jnp.max(b_last, axis=0, keepdims=True))
            o = o_ref[:, hs]
            r = lax.rsqrt(jnp.mean(o * o, axis=-1, keepdims=True) + NORM_EPS)
            n = o * r
            sgo = jax.nn.sigmoid(go)
            dza = dz_ref[:, hs]
            dgo = dza * n * g * sgo * (1.0 - sgo)
            dg_ref[:, hs] += _row0(jnp.sum(dza * n * sgo, axis=0, keepdims=True))
            dn = dza * g * sgo
            do = r * (dn - n * jnp.mean(dn * n, axis=-1, keepdims=True))
            dattn = _dot(do, v, NT)
            dv = _dot(c["attn"], do, TN) + _dot(k_dec, ds1, NT)
            dq_dec = _dot(do, s0)
            dk_dec = _dot(v, ds1)
            ddiag = jnp.sum(do * v, axis=1, keepdims=True)
            dq = dq_dec * e_b + ddiag * k
            dk = dk_dec * e_bl + ddiag * q
            dsums = [dq_dec * q_dec - dk_dec * k_dec]
            for l in range(N_LEVEL):
                dm = mask_ref[l] * dattn
                da = _dot(dm, c["fb"][l])
                db = _dot(dm, c["fa"][l], TN)
                dq = dq + da * c["ea"][l]
                dk = dk + db * c["eb"][l]
                dsums.append(da * c["fa"][l] - db * c["fb"][l])
            dlast = jnp.sum(ds1 * s0, axis=0, keepdims=True) * decay
            dsums.append(dk_dec * k_dec + _row0(dlast, L))
            dlogf = _dot(sum_m, jnp.concatenate(dsums, axis=0), TN)
            dstate[h] = ds1 * decay + _dot(do, q_dec, TN)
            sq, sneg = c["sq"], c["sneg"]
            dqraw = dq * sq * (1.0 + qraw * (1.0 - sq))
            dfp = dlogf * c["w2"] * sneg - dk * (1.0 - lb) * sneg * (1.0 - sneg)
            inv_lb = jnp.where(lb > LB_FLOOR, 1.0 / jnp.maximum(lb, LB_FLOOR), 0.0)
            dlb_tok = dlogf * (c["w1"] * inv_lb - c["w2"] / (1.0 - lb)) - dk * sneg
            dlb_ref[:, hs] += _row0(jnp.sum(dlb_tok, axis=0, keepdims=True))
            dp_ref[:, col(0)] = dqraw.astype(BF16)
            dp_ref[:, col(1)] = dfp.astype(BF16)
            dp_ref[:, col(2)] = dv.astype(BF16)
            dp_ref[:, col(3)] = dgo.astype(BF16)

    full = lambda shape: pl.BlockSpec(shape, lambda c: (0,) * len(shape))
    rev = lambda c: nch - 1 - c
    return pl.pallas_call(
        body, name="hgrn_bwd", grid=(nch,),
        in_specs=[pl.BlockSpec((L, 4 * WIDTH), lambda c: (rev(c), 0)), pl.BlockSpec((L, WIDTH), lambda c: (rev(c), 0)),
                  pl.BlockSpec((None, L, WIDTH), lambda c: (0, rev(c), 0)),
                  pl.BlockSpec((None, N_HEAD, HEAD, HEAD), lambda c: (rev(c), 0, 0, 0)),
                  full((1, WIDTH)), full((1, WIDTH)), full(sum_mat.shape), full(masks.shape), ANY],
        out_specs=[pl.BlockSpec((L, 4 * WIDTH), lambda c: (rev(c), 0)), full((SUBLANES, WIDTH)), full((SUBLANES, WIDTH))],
        out_shape=[jax.ShapeDtypeStruct(dp.shape, dp.dtype), jax.ShapeDtypeStruct((SUBLANES, WIDTH), F32),
                   jax.ShapeDtypeStruct((SUBLANES, WIDTH), F32)],
        scratch_shapes=[pltpu.VMEM((N_HEAD, HEAD, HEAD), F32)], input_output_aliases={8: 0},
        compiler_params=_params(1),
    )(p, o_saved, dz, states, lbrow, gout, sum_mat, masks, dp)


def _shift_down(tile, halo, s):
    tm = tile.shape[0]
    rows = lax.broadcasted_iota(jnp.int32, tile.shape, 0)
    head = jnp.concatenate([pltpu.roll(halo, s, 0), jnp.zeros((tm - SUBLANES, tile.shape[1]), tile.dtype)], axis=0)
    return jnp.where(rows < s, head, pltpu.roll(tile, s, 0))


def _shift_up(tile, halo, s):
    tm = tile.shape[0]
    rows = lax.broadcasted_iota(jnp.int32, tile.shape, 0)
    tail = jnp.concatenate([jnp.zeros((tm - SUBLANES, tile.shape[1]), tile.dtype), pltpu.roll(halo, SUBLANES - s, 0)], axis=0)
    return jnp.where(rows >= tm - s, tail, pltpu.roll(tile, tm - s, 0))


def _conv_fwd(p, w, z):
    T = p.shape[0]
    tm = _token_tile(T)
    per = tm // SUBLANES

    def body(bg_ref, cg_ref, xc_ref, hcg_ref, hxc_ref, w_ref, z_in, z_ref):
        del z_in
        zc = cg_ref[...] * xc_ref[...]
        hz = jnp.where(pl.program_id(0) > 0, hcg_ref[...] * hxc_ref[...], 0.0)
        y = (w_ref[pl.ds(0, 1), :] * _shift_down(zc, hz, 2) + w_ref[pl.ds(1, 1), :] * _shift_down(zc, hz, 1)
             + w_ref[pl.ds(2, 1), :] * zc)
        z_ref[...] = (bg_ref[...] * y).astype(BF16)

    tile = lambda cb: pl.BlockSpec((tm, WIDTH), lambda i: (i, cb))
    prev = lambda cb: pl.BlockSpec((SUBLANES, WIDTH), lambda i: (jnp.maximum(i * per - 1, 0), cb))
    return pl.pallas_call(
        body, name="conv_fwd", grid=(T // tm,),
        in_specs=[tile(4), tile(5), tile(6), prev(5), prev(6), pl.BlockSpec((CONV_K, WIDTH), lambda i: (0, 0)), ANY],
        out_specs=pl.BlockSpec((None, tm, WIDTH), lambda i: (1, i, 0)),
        out_shape=jax.ShapeDtypeStruct(z.shape, z.dtype), input_output_aliases={6: 0}, compiler_params=_params(1),
    )(p, p, p, p, p, w, z)


def _conv_bwd(p, w, dz, dp):
    T = p.shape[0]
    tm = _token_tile(T)
    per = tm // SUBLANES
    last = T // SUBLANES - 1

    def body(bg_ref, cg_ref, xc_ref, hcg_ref, hxc_ref, nbg_ref, dzb_ref, ndzb_ref, w_ref, dp_in, dp_ref, dw_ref, stash):
        del dp_in
        i, jj = pl.program_id(0), pl.program_id(1)

        @pl.when(jnp.logical_and(i == 0, jj == 0))
        def _():
            dw_ref[...] = jnp.zeros_like(dw_ref)

        @pl.when(jj == 0)
        def _():
            cg, xc, bg = cg_ref[...], xc_ref[...], bg_ref[...]
            w0, w1, w2 = w_ref[pl.ds(0, 1), :], w_ref[pl.ds(1, 1), :], w_ref[pl.ds(2, 1), :]
            zc = cg * xc
            hz = jnp.where(i > 0, hcg_ref[...] * hxc_ref[...], 0.0)
            z2, z1 = _shift_down(zc, hz, 2), _shift_down(zc, hz, 1)
            y = w0 * z2 + w1 * z1 + w2 * zc
            dzb = dzb_ref[...]
            dy = dzb * bg
            hdy = jnp.where(i < pl.num_programs(0) - 1, ndzb_ref[...] * nbg_ref[...], 0.0)
            dzc = w2 * dy + w1 * _shift_up(dy, hdy, 1) + w0 * _shift_up(dy, hdy, 2)
            rows = lax.broadcasted_iota(jnp.int32, (SUBLANES, WIDTH), 0)
            colsum = lambda t: jnp.sum(t, axis=0, keepdims=True)
            dw_ref[...] += (jnp.where(rows == 0, colsum(dy * z2), 0.0) + jnp.where(rows == 1, colsum(dy * z1), 0.0)
                            + jnp.where(rows == 2, colsum(dy * zc), 0.0))
            dp_ref[...] = (dzb * y).astype(BF16)
            stash[0] = dzc * xc
            stash[1] = dzc * cg

        @pl.when(jj > 0)
        def _():
            dp_ref[...] = stash[jj - 1].astype(BF16)

    n_tiles = T // tm
    tile = lambda cb: pl.BlockSpec((tm, WIDTH), lambda i, jj: (i, cb))
    prev = lambda cb: pl.BlockSpec((SUBLANES, WIDTH), lambda i, jj: (jnp.maximum(i * per - 1, 0), cb))
    nxt = lambda i: jnp.minimum((i + 1) * per, last)
    return pl.pallas_call(
        body, name="conv_bwd", grid=(n_tiles, 3),
        in_specs=[tile(4), tile(5), tile(6), prev(5), prev(6),
                  pl.BlockSpec((SUBLANES, WIDTH), lambda i, jj: (nxt(i), 4)),
                  pl.BlockSpec((None, tm, WIDTH), lambda i, jj: (1, i, 0)),
                  pl.BlockSpec((None, SUBLANES, WIDTH), lambda i, jj: (1, nxt(i), 0)),
                  pl.BlockSpec((CONV_K, WIDTH), lambda i, jj: (0, 0)), ANY],
        out_specs=[pl.BlockSpec((tm, WIDTH), lambda i, jj: (i, 4 + jj)),
                   pl.BlockSpec((SUBLANES, WIDTH), lambda i, jj: (0, 0))],
        out_shape=[jax.ShapeDtypeStruct(dp.shape, dp.dtype), jax.ShapeDtypeStruct((SUBLANES, WIDTH), F32)],
        scratch_shapes=[pltpu.VMEM((2, tm, WIDTH), F32)], input_output_aliases={9: 0}, compiler_params=_params(2),
    )(p, p, p, p, p, p, dz, dz, w, dp)


GELU_C = float(np.sqrt(2.0 / np.pi))
GELU_A = 0.044715


def _gelu(x):
    th = jnp.tanh(GELU_C * (x + GELU_A * x * x * x))
    return 0.5 * x * (1.0 + th), th


def _gelu_grad(x, th):
    return 0.5 * (1.0 + th) + 0.5 * x * (1.0 - th * th) * GELU_C * (1.0 + 3.0 * GELU_A * x * x)


def _sg_core(u, v, lng, lnb, ws_ref, bs_ref):
    gu, thu = _gelu(u)
    gv, thv = _gelu(v)
    xc = gv - jnp.mean(gv, axis=-1, keepdims=True)
    rs = lax.rsqrt(jnp.mean(xc * xc, axis=-1, keepdims=True) + LN_EPS)
    xh = xc * rs
    vp = xh * lng + lnb
    tril = (lax.broadcasted_iota(jnp.int32, (SG_CHUNK, SG_CHUNK), 0)
            >= lax.broadcasted_iota(jnp.int32, (SG_CHUNK, SG_CHUNK), 1))
    wm = [jnp.where(tril, ws_ref[g], 0.0).astype(BF16) for g in range(SG_GROUPS)]
    gs = lambda t, g: t[:, g * LANES:(g + 1) * LANES]
    sv = jnp.concatenate([_dot(wm[g], gs(vp, g)) + bs_ref[g] for g in range(SG_GROUPS)], axis=1)
    return dict(gu=gu, thu=thu, thv=thv, rs=rs, xh=xh, vp=vp, tril=tril, wm=wm, sv=sv)


def _sg_fwd(p, lng, lnb, ws, bs, z):
    T = p.shape[0]

    def body(u_ref, v_ref, lng_ref, lnb_ref, ws_ref, bs_ref, z_in, z_ref):
        del z_in
        c = _sg_core(u_ref[...], v_ref[...], lng_ref[...], lnb_ref[...], ws_ref, bs_ref)
        z_ref[...] = (c["gu"] * c["sv"]).astype(BF16)

    full = lambda shape: pl.BlockSpec(shape, lambda c: (0,) * len(shape))
    return pl.pallas_call(
        body, name="sg_fwd", grid=(T // SG_CHUNK,),
        in_specs=[pl.BlockSpec((SG_CHUNK, WIDTH), lambda c: (c, 7)), pl.BlockSpec((SG_CHUNK, WIDTH), lambda c: (c, 8)),
                  full((1, WIDTH)), full((1, WIDTH)), full(ws.shape), full(bs.shape), ANY],
        out_specs=pl.BlockSpec((None, SG_CHUNK, WIDTH), lambda c: (2, c, 0)),
        out_shape=jax.ShapeDtypeStruct(z.shape, z.dtype), input_output_aliases={6: 0}, compiler_params=_params(1),
    )(p, p, lng, lnb, ws, bs, z)


def _sg_bwd(p, lng, lnb, ws, bs, dz, dp):
    T = p.shape[0]

    def body(u_ref, v_ref, lng_ref, lnb_ref, ws_ref, bs_ref, dz_ref, dp_in, dp_ref, dws_ref, dbs_ref, dlng_ref, dlnb_ref,
             stash):
        del dp_in
        cidx, jj = pl.program_id(0), pl.program_id(1)

        @pl.when(jnp.logical_and(cidx == 0, jj == 0))
        def _():
            dws_ref[...] = jnp.zeros_like(dws_ref)
            dbs_ref[...] = jnp.zeros_like(dbs_ref)
            dlng_ref[...] = jnp.zeros_like(dlng_ref)
            dlnb_ref[...] = jnp.zeros_like(dlnb_ref)

        @pl.when(jj == 0)
        def _():
            u, v, lng = u_ref[...], v_ref[...], lng_ref[...]
            c = _sg_core(u, v, lng, lnb_ref[...], ws_ref, bs_ref)
            dzc = dz_ref[...]
            gs = lambda t, g: t[:, g * LANES:(g + 1) * LANES]
            dsv = dzc * c["gu"]
            dvp = []
            for g in range(SG_GROUPS):
                dsv_g = gs(dsv, g)
                dws_ref[g] += jnp.where(c["tril"], _dot(dsv_g, gs(c["vp"], g), NT), 0.0)
                dbs_ref[g] += jnp.sum(dsv_g, axis=1, keepdims=True)
                dvp.append(_dot(c["wm"][g], dsv_g, TN))
            dvp = jnp.concatenate(dvp, axis=1)
            xh = c["xh"]
            dlng_ref[...] += _row0(jnp.sum(dvp * xh, axis=0, keepdims=True))
            dlnb_ref[...] += _row0(jnp.sum(dvp, axis=0, keepdims=True))
            dxh = dvp * lng
            dgv = c["rs"] * (dxh - jnp.mean(dxh, axis=-1, keepdims=True) - xh * jnp.mean(dxh * xh, axis=-1, keepdims=True))
            dp_ref[...] = (dzc * c["sv"] * _gelu_grad(u, c["thu"])).astype(BF16)
            stash[...] = dgv * _gelu_grad(v, c["thv"])

        @pl.when(jj == 1)
        def _():
            dp_ref[...] = stash[...].astype(BF16)

    full = lambda shape: pl.BlockSpec(shape, lambda c, jj: (0,) * len(shape))
    return pl.pallas_call(
        body, name="sg_bwd", grid=(T // SG_CHUNK, 2),
        in_specs=[pl.BlockSpec((SG_CHUNK, WIDTH), lambda c, jj: (c, 7)), pl.BlockSpec((SG_CHUNK, WIDTH), lambda c, jj: (c, 8)),
                  full((1, WIDTH)), full((1, WIDTH)), full(ws.shape), full(bs.shape),
                  pl.BlockSpec((None, SG_CHUNK, WIDTH), lambda c, jj: (2, c, 0)), ANY],
        out_specs=[pl.BlockSpec((SG_CHUNK, WIDTH), lambda c, jj: (c, 7 + jj)), full(ws.shape), full(bs.shape),
                   full((SUBLANES, WIDTH)), full((SUBLANES, WIDTH))],
        out_shape=[jax.ShapeDtypeStruct(dp.shape, dp.dtype), jax.ShapeDtypeStruct(ws.shape, F32),
                   jax.ShapeDtypeStruct(bs.shape, F32), jax.ShapeDtypeStruct((SUBLANES, WIDTH), F32),
                   jax.ShapeDtypeStruct((SUBLANES, WIDTH), F32)],
        scratch_shapes=[pltpu.VMEM((SG_CHUNK, WIDTH), F32)], input_output_aliases={7: 0}, compiler_params=_params(2),
    )(p, p, lng, lnb, ws, bs, dz, dp)


BRANCH_COLS = D_MODEL // N_CHIP
GATE_BLOCK0 = GATE_COL0 // BRANCH_COLS


def _merge_fwd(z, p, wb):
    T = z.shape[1]
    tm = _token_tile(T)

    def body(z_ref, wb_ref, gt_ref, out_ref, acc):
        n = pl.program_id(2)
        part = jax.nn.sigmoid(gt_ref[...]) * _dot(z_ref[...], wb_ref[...])

        @pl.when(n == 0)
        def _():
            acc[...] = part

        @pl.when(n > 0)
        def _():
            acc[...] += part

        @pl.when(n == N_BRANCH - 1)
        def _():
            out_ref[...] = acc[...].astype(BF16)

    return pl.pallas_call(
        body, name="merge_fwd", grid=(T // tm, N_CHIP, N_BRANCH),
        in_specs=[pl.BlockSpec((None, tm, WIDTH), lambda i, j, n: (n, i, 0)),
                  pl.BlockSpec((None, None, WIDTH, BRANCH_COLS), lambda i, j, n: (j, n, 0, 0)),
                  pl.BlockSpec((tm, BRANCH_COLS), lambda i, j, n: (i, GATE_BLOCK0 + 4 * n + j))],
        out_specs=pl.BlockSpec((tm, BRANCH_COLS), lambda i, j, n: (i, j)),
        out_shape=jax.ShapeDtypeStruct((T, D_MODEL), BF16),
        scratch_shapes=[pltpu.VMEM((tm, BRANCH_COLS), F32)], compiler_params=_params(3))(z, wb, p)


def _merge_bwd(z, p, wb, dmerged):
    T = z.shape[1]
    tm = _token_tile(T)

    def body(z_ref, wb_ref, gt_ref, dm_ref, dp_ref, dy_ref, dz_ref):
        j = pl.program_id(2)
        wbv = wb_ref[...]
        y = _dot(z_ref[...], wbv)
        gate = jax.nn.sigmoid(gt_ref[...])
        dm = dm_ref[...]
        dp_ref[...] = (dm * y * gate * (1.0 - gate)).astype(BF16)
        dyv = dm * gate
        dy_ref[...] = dyv.astype(BF16)
        part = _dot(dyv, wbv, NT)

        @pl.when(j == 0)
        def _():
            dz_ref[...] = part

        @pl.when(j > 0)
        def _():
            dz_ref[...] += part

    return pl.pallas_call(
        body, name="merge_bwd", grid=(T // tm, N_BRANCH, N_CHIP),
        in_specs=[pl.BlockSpec((None, tm, WIDTH), lambda i, n, j: (n, i, 0)),
                  pl.BlockSpec((None, None, WIDTH, BRANCH_COLS), lambda i, n, j: (j, n, 0, 0)),
                  pl.BlockSpec((tm, BRANCH_COLS), lambda i, n, j: (i, GATE_BLOCK0 + 4 * n + j)),
                  pl.BlockSpec((tm, BRANCH_COLS), lambda i, n, j: (i, j))],
        out_specs=[pl.BlockSpec((tm, BRANCH_COLS), lambda i, n, j: (i, GATE_BLOCK0 + 4 * n + j)),
                   pl.BlockSpec((None, tm, BRANCH_COLS), lambda i, n, j: (n, i, j)),
                   pl.BlockSpec((None, tm, WIDTH), lambda i, n, j: (n, i, 0))],
        out_shape=[jax.ShapeDtypeStruct((T, IN_COLS), BF16), jax.ShapeDtypeStruct((N_BRANCH, T, D_MODEL), BF16),
                   jax.ShapeDtypeStruct((N_BRANCH, T, WIDTH), F32)],
        compiler_params=_params(3))(z, wb, p, dmerged)


def _branch_wgrad(z, dy):
    T = z.shape[1]
    tt = _token_tile(T)
    return _matmul(
        "branch_wgrad", z, dy, dims=TN, grid=(N_CHIP * N_BRANCH, 1, T // tt),
        a_spec=pl.BlockSpec((None, tt, WIDTH), lambda j, i, kk: (j % N_BRANCH, kk, 0)),
        b_spec=pl.BlockSpec((None, tt, BRANCH_COLS), lambda j, i, kk: (j % N_BRANCH, kk, j // N_BRANCH)),
        out_specs=[pl.BlockSpec((None, None, WIDTH, BRANCH_COLS), lambda j, i, kk: (j // N_BRANCH, j % N_BRANCH, 0, 0))],
        out_shapes=[jax.ShapeDtypeStruct((N_CHIP, N_BRANCH, WIDTH, BRANCH_COLS), F32)],
        acc_shape=(WIDTH, BRANCH_COLS))[0]


def _layer_fwd(x, wts, small):
    win, wb, wo, w1, w2 = wts
    h = _rms_fwd("rms_mix", x, small["g_mix"])
    p = _mm_cols("in_proj", h, win, [F32])[0]
    o_hgrn, z, states = _hgrn_fwd(p, small["lbs"], small["g_hgrn_out"])
    z = _conv_fwd(p, small["w_conv"], z)
    z = _sg_fwd(p, small["sg_ln_g"], small["sg_ln_b"], small["w_sg"], small["b_sg"], z)
    merged = _merge_fwd(z, p, wb)
    x_mid = _mm_rows("out_proj", merged, wo, x)
    h2 = _rms_fwd("rms_ffn", x_mid, small["g_ffn"])
    a, s = _mm_cols("ff1", h2, w1, [F32, BF16], epilogue=lambda acc: (acc, jnp.square(jnp.maximum(acc, 0.0))))
    x_out = _mm_rows("ff2", s, w2, x_mid)
    saved = dict(x=x, h=h, p=p, o_hgrn=o_hgrn, z=z, states=states, merged=merged, x_mid=x_mid, h2=h2, a=a, s=s)
    return x_out, saved


def _layer_bwd(dx_out, sv, wts, small, after=()):
    win, wb, wo, w1, w2 = wts
    g = {}
    da = _mm_cols_t("ff2_dgrad", dx_out, w2, BF16, extra=(sv["a"],), after=after,
                    epilogue=lambda acc, a: (acc * 2.0 * jnp.maximum(a, 0.0),))
    g["w_ff2"] = _mm_wgrad("ff2_wgrad", sv["s"], dx_out, w2.shape[1], D_MODEL, True, False)
    g["w_ff1"] = _mm_wgrad("ff1_wgrad", sv["h2"], da, D_MODEL, w1.shape[2], False, True)
    dh2 = _mm_rows_t("ff1_dgrad", da, w1)
    dx_mid, g["g_ffn"] = _rms_bwd("rms_ffn_bwd", sv["x_mid"], small["g_ffn"], dh2, dx_out)
    dmerged = _mm_cols_t("out_proj_dgrad", dx_mid, wo, F32)
    g["w_o"] = _mm_wgrad("out_proj_wgrad", sv["merged"], dx_mid, wo.shape[1], D_MODEL, True, False)
    dp, dy, dz = _merge_bwd(sv["z"], sv["p"], wb, dmerged)
    g["w_branch"] = _branch_wgrad(sv["z"], dy)
    dp, g["lbs"], g["g_hgrn_out"] = _hgrn_bwd(sv["p"], sv["o_hgrn"], dz, sv["states"], small["lbs"],
                                              small["g_hgrn_out"], dp)
    dp, g["w_conv"] = _conv_bwd(sv["p"], small["w_conv"], dz, dp)
    dp, g["w_sg"], g["b_sg"], g["sg_ln_g"], g["sg_ln_b"] = _sg_bwd(
        sv["p"], small["sg_ln_g"], small["sg_ln_b"], small["w_sg"], small["b_sg"], dz, dp)
    g["w_in"] = _mm_wgrad("in_proj_wgrad", sv["h"], dp, D_MODEL, win.shape[2], False, True)
    dh = _mm_rows_t("in_proj_dgrad", dp, win)
    dx, g["g_mix"] = _rms_bwd("rms_mix_bwd", sv["x"], small["g_mix"], dh, dx_mid)
    return dx, g


def _mesh_pos():
    return lax.axis_index("x"), lax.axis_index("y"), lax.axis_index("c")


def _other_chips(x, y):
    return [(1 - x, y), (x, 1 - y), (1 - x, 1 - y)]


def _remote(src, dst, send_sems, recv_sems, k, to):
    return pltpu.make_async_remote_copy(src_ref=src, dst_ref=dst, send_sem=send_sems.at[k], recv_sem=recv_sems.at[k],
                                        device_id=to, device_id_type=MESH)


def _comm_call(name, body, ins, out_shapes, n_remote, in_place=False):
    scratch = [pltpu.SemaphoreType.DMA((n_remote,)), pltpu.SemaphoreType.DMA((n_remote,))]
    aliases = {t: t for t in range(len(ins))} if in_place else {}
    return pl.pallas_call(
        body, name=name, in_specs=[ANY] * len(ins), out_specs=[ANY] * len(out_shapes), out_shape=list(out_shapes),
        scratch_shapes=scratch, input_output_aliases=aliases)(*ins)


def _gather_chips(name, bufs):
    n = len(bufs)

    def body(*refs):
        outs = refs[n:2 * n]
        send_sems, recv_sems = refs[2 * n:]
        x, y, c = _mesh_pos()
        chips = _other_chips(x, y)
        sibling = (x, y, 1 - c)
        started = []

        def half(t, chip, cc):
            rh = bufs[t].shape[1] // 2
            return outs[t].at[2 * chip[0] + chip[1], pl.ds(cc * rh, rh), :]

        for t in range(n):
            for j, chip in enumerate(chips):
                cp = _remote(half(t, (x, y), c), half(t, (x, y), c), send_sems, recv_sems, 6 * t + j, (*chip, c))
                cp.start()
                started.append(cp)
        for t in range(n):
            for j, chip in enumerate(chips):
                _remote(half(t, chip, c), half(t, chip, c), send_sems, recv_sems, 6 * t + j, (*chip, c)).wait_recv()
                fw = _remote(half(t, chip, c), half(t, chip, c), send_sems, recv_sems, 6 * t + 3 + j, sibling)
                fw.start()
                started.append(fw)
        for t in range(n):
            for j, chip in enumerate(chips):
                _remote(half(t, chip, 1 - c), half(t, chip, 1 - c), send_sems, recv_sems, 6 * t + 3 + j, sibling).wait_recv()
        for cp in started:
            cp.wait_send()

    out_shapes = [jax.ShapeDtypeStruct(b.shape, b.dtype) for b in bufs]
    return _comm_call(name, body, bufs, out_shapes, 6 * n, in_place=True)


HBM = pl.BlockSpec(memory_space=pltpu.HBM)
SEM = pl.BlockSpec(memory_space=pltpu.SEMAPHORE)
DATAFLOW = pltpu.SideEffectType.DATAFLOW_SIDE_EFFECTING


def _split_start(name, bufs, copies, n_copies, after=()):
    n = len(bufs)

    def body(*refs):
        send_sems, recv_sems = refs[n + len(after)], refs[n + len(after) + 1]
        for cp in copies(refs[:n], send_sems, recv_sems):
            cp.start()
        refs[-1][...] = jnp.zeros_like(refs[-1])

    outs = pl.pallas_call(
        body, name=name,
        out_shape=(pltpu.SemaphoreType.DMA((n_copies,)), pltpu.SemaphoreType.DMA((n_copies,)),
                   *[pltpu.HBM(b.shape, b.dtype) for b in bufs], jax.ShapeDtypeStruct((SUBLANES, LANES), F32)),
        in_specs=[HBM] * n + [ANY] * len(after),
        out_specs=(SEM, SEM, *[HBM] * n, pl.BlockSpec(memory_space=pltpu.VMEM)),
        input_output_aliases={t: 2 + t for t in range(n)},
        compiler_params=pltpu.CompilerParams(has_side_effects=DATAFLOW),
    )(*[pltpu.with_memory_space_constraint(b, pltpu.HBM) for b in bufs], *after)
    return outs[0], outs[1], list(outs[2:2 + n]), outs[-1]


def _split_wait(name, started, copies, after):
    send_sems, recv_sems, bufs, _ = started
    n = len(bufs)

    def body(*refs):
        for cp in copies(refs[:n], refs[n], refs[n + 1]):
            cp.wait_send()
            cp.wait_recv()

    return list(pl.pallas_call(
        body, name=name, out_shape=tuple(pltpu.HBM(b.shape, b.dtype) for b in bufs),
        in_specs=[HBM] * n + [SEM, SEM] + [ANY] * len(after), out_specs=tuple([HBM] * n),
        input_output_aliases={t: t for t in range(n)},
        compiler_params=pltpu.CompilerParams(has_side_effects=DATAFLOW),
    )(*bufs, send_sems, recv_sems, *after))


def _weight_ici_copies(refs, send_sems, recv_sems):
    x, y, c = _mesh_pos()
    out = []
    for t, ref in enumerate(refs):
        rh = ref.shape[1] // 2
        mine = ref.at[2 * x + y, pl.ds(c * rh, rh), :]
        out += [_remote(mine, mine, send_sems, recv_sems, 3 * t + j, (*chip, c)) for j, chip in enumerate(_other_chips(x, y))]
    return out


def _weight_d2d_copies(refs, send_sems, recv_sems):
    x, y, c = _mesh_pos()
    out = []
    for t, ref in enumerate(refs):
        rh = ref.shape[1] // 2
        for j, chip in enumerate(_other_chips(x, y)):
            blk = ref.at[2 * chip[0] + chip[1], pl.ds(c * rh, rh), :]
            out.append(_remote(blk, blk, send_sems, recv_sems, 3 * t + j, (x, y, 1 - c)))
    return out


def _grad_exchange_copies(refs, send_sems, recv_sems):
    x, y, c = _mesh_pos()
    n = len(refs) // 2
    out = []
    for t in range(n):
        for j, chip in enumerate(_other_chips(x, y)):
            out.append(_remote(refs[t].at[2 * chip[0] + chip[1]], refs[n + t].at[j], send_sems, recv_sems, 3 * t + j,
                               (*chip, c)))
    return out


def _sibling_swap_halves(name, grads):
    n = len(grads)

    def body(*refs):
        ins, outs = refs[:n], refs[n:2 * n]
        send_sems, recv_sems = refs[2 * n:]
        x, y, c = _mesh_pos()
        copies = []
        for t in range(n):
            rh = grads[t].shape[1] // 2
            cp = _remote(ins[t].at[:, pl.ds((1 - c) * rh, rh), :], outs[t], send_sems, recv_sems, t, (x, y, 1 - c))
            cp.start()
            copies.append(cp)
        for cp in copies:
            cp.wait_send()
            cp.wait_recv()

    out_shapes = [jax.ShapeDtypeStruct((N_CHIP, g.shape[1] // 2, g.shape[2]), g.dtype) for g in grads]
    return _comm_call(name, body, grads, out_shapes, n)


def _chip_exchange(name, parts):
    n = len(parts)

    def body(*refs):
        ins, outs = refs[:n], refs[n:2 * n]
        send_sems, recv_sems = refs[2 * n:]
        x, y, c = _mesh_pos()
        copies = []
        for t in range(n):
            for j, chip in enumerate(_other_chips(x, y)):
                cp = _remote(ins[t].at[2 * chip[0] + chip[1]], outs[t].at[j], send_sems, recv_sems, 3 * t + j, (*chip, c))
                cp.start()
                copies.append(cp)
        for cp in copies:
            cp.wait_send()
            cp.wait_recv()

    out_shapes = [jax.ShapeDtypeStruct((3, *q.shape[1:]), q.dtype) for q in parts]
    return _comm_call(name, body, parts, out_shapes, 3 * n)


def _sibling_gather(name, bufs):
    n = len(bufs)

    def body(*refs):
        outs = refs[n:2 * n]
        send_sems, recv_sems = refs[2 * n:]
        x, y, c = _mesh_pos()
        copies = []
        for t in range(n):
            cp = _remote(outs[t].at[c], outs[t].at[c], send_sems, recv_sems, t, (x, y, 1 - c))
            cp.start()
            copies.append(cp)
        for cp in copies:
            cp.wait_send()
            cp.wait_recv()

    out_shapes = [jax.ShapeDtypeStruct(b.shape, b.dtype) for b in bufs]
    return _comm_call(name, body, bufs, out_shapes, n, in_place=True)


def _gather_all(name, block, slot):
    buf = lax.dynamic_update_slice(jnp.zeros((8, *block.shape), block.dtype), block[None], (slot, 0, 0))

    def body(in_ref, out_ref, send_sems, recv_sems):
        del in_ref
        x, y, c = _mesh_pos()
        chips = _other_chips(x, y)
        sibling = (x, y, 1 - c)
        slot_of = lambda px, py, pc: out_ref.at[4 * px + 2 * py + pc]
        started = [_remote(slot_of(x, y, c), slot_of(x, y, c), send_sems, recv_sems, 0, sibling)]
        started += [_remote(slot_of(x, y, c), slot_of(x, y, c), send_sems, recv_sems, 1 + j, (*chip, c))
                    for j, chip in enumerate(chips)]
        for cp in started:
            cp.start()
        for j, chip in enumerate(chips):
            _remote(slot_of(*chip, c), slot_of(*chip, c), send_sems, recv_sems, 1 + j, (*chip, c)).wait_recv()
            fw = _remote(slot_of(*chip, c), slot_of(*chip, c), send_sems, recv_sems, 4 + j, sibling)
            fw.start()
            started.append(fw)
        _remote(slot_of(x, y, 1 - c), slot_of(x, y, 1 - c), send_sems, recv_sems, 0, sibling).wait_recv()
        for j, chip in enumerate(chips):
            _remote(slot_of(*chip, 1 - c), slot_of(*chip, 1 - c), send_sems, recv_sems, 4 + j, sibling).wait_recv()
        for cp in started:
            cp.wait_send()

    return _comm_call(name, body, [buf], [jax.ShapeDtypeStruct(buf.shape, buf.dtype)], 7, in_place=True)[0]


def _row_tile(rows, cols):
    cap = max(SUBLANES, ELEMWISE_BLOCK_BYTES // (4 * cols))
    tr = rows
    while tr > cap and tr % 2 == 0:
        tr //= 2
    return tr


def _pair_sum(name, grad, recv, core):
    _, rh, cols = recv.shape
    tr = _row_tile(rh, cols)
    per = rh // tr

    def body(core_ref, g_ref, r_ref, out32_ref, out16_ref):
        del core_ref
        s = g_ref[...] + r_ref[...]
        out32_ref[...] = s
        out16_ref[...] = s.astype(BF16)

    blk = pl.BlockSpec((None, tr, cols), lambda k, i, core_ref: (k, i, 0))
    return pl.pallas_call(
        body, name=name,
        grid_spec=pltpu.PrefetchScalarGridSpec(
            num_scalar_prefetch=1, grid=(N_CHIP, per),
            in_specs=[pl.BlockSpec((None, tr, cols), lambda k, i, core_ref: (k, core_ref[0] * per + i, 0)), blk],
            out_specs=[blk, blk]),
        out_shape=[jax.ShapeDtypeStruct(recv.shape, F32), jax.ShapeDtypeStruct(recv.shape, BF16)],
        compiler_params=_params(2))(core, grad, recv)


def _chip_sum(name, part32, recv, pos):
    _, rh, cols = part32.shape
    tr = _row_tile(rh, cols)

    def body(pos_ref, own_ref, r_ref, out_ref):
        del pos_ref
        out_ref[...] = ((own_ref[...] + r_ref[0].astype(F32)) + r_ref[1].astype(F32)) + r_ref[2].astype(F32)

    return pl.pallas_call(
        body, name=name,
        grid_spec=pltpu.PrefetchScalarGridSpec(
            num_scalar_prefetch=1, grid=(rh // tr,),
            in_specs=[pl.BlockSpec((None, tr, cols), lambda i, pos_ref: (pos_ref[0], i, 0)),
                      pl.BlockSpec((3, tr, cols), lambda i, pos_ref: (0, i, 0))],
            out_specs=pl.BlockSpec((None, tr, cols), lambda i, pos_ref: (pos_ref[1], i, 0))),
        out_shape=jax.ShapeDtypeStruct((2, rh, cols), F32), compiler_params=_params(1))(pos, part32, recv)


def _cast_into_slot(name, w, layer, pos):
    _, rows, cols = w.shape
    tr = _row_tile(rows, cols)

    def body(pos_ref, w_ref, out_ref):
        del pos_ref
        out_ref[...] = w_ref[...].astype(BF16)

    return pl.pallas_call(
        body, name=name,
        grid_spec=pltpu.PrefetchScalarGridSpec(
            num_scalar_prefetch=1, grid=(rows // tr,),
            in_specs=[pl.BlockSpec((None, tr, cols), lambda i, pos_ref: (layer, i, 0))],
            out_specs=pl.BlockSpec((None, tr, cols), lambda i, pos_ref: (pos_ref[0], i, 0))),
        out_shape=jax.ShapeDtypeStruct((N_CHIP, rows, cols), BF16), compiler_params=_params(1))(pos, w)


def _adamw_math(w, g, m, v):
    m = ADAM_B1 * m + (1.0 - ADAM_B1) * g
    v = ADAM_B2 * v + (1.0 - ADAM_B2) * jnp.square(g)
    m_hat = m / (1.0 - ADAM_B1 ** ADAM_STEP)
    v_hat = v / (1.0 - ADAM_B2 ** ADAM_STEP)
    delta = -ADAM_LR * (m_hat / (jnp.sqrt(v_hat) + ADAM_EPS) + ADAM_WD * w)
    return delta, m, v


def _adamw_layers(name, w, m, v, grads):
    _, rows, cols = w.shape
    tr = _row_tile(rows, cols)

    def body(w_ref, m_ref, v_ref, *rest):
        g_refs, (grad_ref, d_ref, nm_ref, nv_ref) = rest[:DEPTH], rest[DEPTH:]
        layer = pl.program_id(0)
        g = g_refs[0][...]
        for l in range(1, DEPTH):
            g = jnp.where(layer == l, g_refs[l][...], g)
        grad_ref[...] = g
        d_ref[...], nm_ref[...], nv_ref[...] = _adamw_math(w_ref[...], g, m_ref[...], v_ref[...])

    blk = pl.BlockSpec((None, tr, cols), lambda l, i: (l, i, 0))
    g_spec = lambda k: pl.BlockSpec((tr, cols), lambda l, i: (jnp.where(l == k, i, 0), 0))
    return pl.pallas_call(
        body, name=name, grid=(DEPTH, rows // tr),
        in_specs=[blk, blk, blk] + [g_spec(k) for k in range(DEPTH)], out_specs=[blk] * 4,
        out_shape=[jax.ShapeDtypeStruct(w.shape, F32)] * 4, compiler_params=_params(2))(w, m, v, *grads)


def _sum_devices(gathered):
    _, rows, cols = gathered.shape

    def body(g_ref, out_ref):
        s = g_ref[0]
        for d in range(1, 8):
            s = s + g_ref[d]
        out_ref[...] = s

    return pl.pallas_call(body, name="sum_devices", out_shape=jax.ShapeDtypeStruct((rows, cols), F32),
                          compiler_params=pltpu.CompilerParams(vmem_limit_bytes=VMEM_LIMIT_BYTES))(gathered)


def _adamw_flat(w, g, m, v):
    def body(w_ref, g_ref, m_ref, v_ref, d_ref, nm_ref, nv_ref):
        d_ref[...], nm_ref[...], nv_ref[...] = _adamw_math(w_ref[...], g_ref[...], m_ref[...], v_ref[...])

    return pl.pallas_call(body, name="adamw_small", out_shape=[jax.ShapeDtypeStruct(w.shape, F32)] * 3,
                          compiler_params=pltpu.CompilerParams(vmem_limit_bytes=VMEM_LIMIT_BYTES))(w, g, m, v)


SMALL_NAMES = ["g_mix", "lower_bounds", "g_hgrn_out", "w_conv", "sg_ln_g", "sg_ln_b", "w_sg", "b_sg", "g_ffn", "g_final"]
BIG_NAMES = ["w_in", "w_branch", "w_o", "w_ff1", "w_ff2"]
WEIGHT_ORDER = ["w_in", "g_mix", "lower_bounds", "g_hgrn_out", "w_conv", "sg_ln_g", "sg_ln_b", "w_sg", "b_sg", "w_branch",
                "w_o", "g_ffn", "w_ff1", "w_ff2", "g_final"]


def _padded_rows(n):
    return -(-n // SUBLANES) * SUBLANES


def _pack(arrays):
    parts = []
    for a in arrays:
        a = a.reshape(-1, LANES)
        parts.append(jnp.pad(a, ((0, _padded_rows(a.shape[0]) - a.shape[0]), (0, 0))))
    return jnp.concatenate(parts, axis=0)


def _unpack(flat, shapes):
    out, row = [], 0
    for s in shapes:
        n = int(np.prod(s)) // LANES
        out.append(flat[row:row + n].reshape(s))
        row += _padded_rows(n)
    return out


def _as_2d(name, a):
    return a.reshape(DEPTH, N_BRANCH * WIDTH, BRANCH_COLS) if name == "w_branch" else a


def kernel(x, w_in, g_mix, lower_bounds, g_hgrn_out, w_conv, sg_ln_g, sg_ln_b, w_sg, b_sg, w_branch, w_o, g_ffn, w_ff1, w_ff2, g_final, loss_target, m_w_in, m_g_mix, m_lower_bounds, m_g_hgrn_out, m_w_conv, m_sg_ln_g, m_sg_ln_b, m_w_sg, m_b_sg, m_w_branch, m_w_o, m_g_ffn, m_w_ff1, m_w_ff2, m_g_final, v_w_in, v_g_mix, v_lower_bounds, v_g_hgrn_out, v_w_conv, v_sg_ln_g, v_sg_ln_b, v_w_sg, v_b_sg, v_w_branch, v_w_o, v_g_ffn, v_w_ff1, v_w_ff2, v_g_final):
    weights = dict(w_in=w_in, g_mix=g_mix, lower_bounds=lower_bounds, g_hgrn_out=g_hgrn_out, w_conv=w_conv,
                   sg_ln_g=sg_ln_g, sg_ln_b=sg_ln_b, w_sg=w_sg, b_sg=b_sg, w_branch=w_branch, w_o=w_o, g_ffn=g_ffn,
                   w_ff1=w_ff1, w_ff2=w_ff2, g_final=g_final)
    mom1 = dict(w_in=m_w_in, g_mix=m_g_mix, lower_bounds=m_lower_bounds, g_hgrn_out=m_g_hgrn_out, w_conv=m_w_conv,
                sg_ln_g=m_sg_ln_g, sg_ln_b=m_sg_ln_b, w_sg=m_w_sg, b_sg=m_b_sg, w_branch=m_w_branch, w_o=m_w_o,
                g_ffn=m_g_ffn, w_ff1=m_w_ff1, w_ff2=m_w_ff2, g_final=m_g_final)
    mom2 = dict(w_in=v_w_in, g_mix=v_g_mix, lower_bounds=v_lower_bounds, g_hgrn_out=v_g_hgrn_out, w_conv=v_w_conv,
                sg_ln_g=v_sg_ln_g, sg_ln_b=v_sg_ln_b, w_sg=v_w_sg, b_sg=v_b_sg, w_branch=v_w_branch, w_o=v_w_o,
                g_ffn=v_g_ffn, w_ff1=v_w_ff1, w_ff2=v_w_ff2, g_final=v_g_final)
    xi, yi, ci = _mesh_pos()
    core = jnp.reshape(ci, (1,)).astype(jnp.int32)
    pos = jnp.stack([2 * xi + yi, ci]).astype(jnp.int32)
    device = 4 * xi + 2 * yi + ci
    conv_cols = w_conv.shape[2]

    conv_all = _gather_all("gather_w_conv", w_conv.reshape(DEPTH * CONV_K, conv_cols), device)
    conv_full = conv_all.reshape(N_CHIP, 2, DEPTH, CONV_K, conv_cols)[:, 0].transpose(1, 2, 0, 3).reshape(DEPTH, CONV_K, WIDTH)
    lbs = _lbs_fwd(lower_bounds)

    act = x[0]
    layers = []
    n_big = len(BIG_NAMES)
    cast = lambda l: [_cast_into_slot("cast_" + n, _as_2d(n, weights[n]), l, pos) for n in BIG_NAMES]
    ici = _split_start("weights_ici_start_0", cast(0), _weight_ici_copies, 3 * n_big)
    for l in range(DEPTH):
        landed = _split_wait("weights_ici_wait_%d" % l, ici, _weight_ici_copies, after=[act])
        token = []
        if l + 1 < DEPTH:
            ici = _split_start("weights_ici_start_%d" % (l + 1), cast(l + 1), _weight_ici_copies, 3 * n_big, after=[landed[0]])
            token = [ici[3]]
        d2d = _split_start("weights_d2d_start_%d" % l, landed, _weight_d2d_copies, 3 * n_big, after=token)
        gathered = _split_wait("weights_d2d_wait_%d" % l, d2d, _weight_d2d_copies, after=[])
        wts = [gathered[0], gathered[1].reshape(N_CHIP, N_BRANCH, WIDTH, BRANCH_COLS), *gathered[2:]]
        small = dict(g_mix=g_mix[l:l + 1], lbs=lbs[l:l + 1], g_hgrn_out=g_hgrn_out[l:l + 1], w_conv=conv_full[l],
                     sg_ln_g=sg_ln_g[l:l + 1], sg_ln_b=sg_ln_b[l:l + 1], w_sg=w_sg[l],
                     b_sg=b_sg[l].reshape(SG_GROUPS, SG_CHUNK, 1), g_ffn=g_ffn[l:l + 1])
        act, saved = _layer_fwd(act, wts, small)
        layers.append((wts, small, saved))
    loss_blk, dact, dg_final = _loss_head(act, g_final.reshape(1, D_MODEL), loss_target[0])

    reduced = {n: [None] * DEPTH for n in BIG_NAMES}
    small_grads = [None] * DEPTH

    def finish_exchange(l, own32, started, after):
        bufs = _split_wait("grad_exchange_wait_%d" % l, started, _grad_exchange_copies, after=after)
        halves = [_chip_sum("grad_chip_sum_" + n, p32, r, pos) for n, p32, r in zip(BIG_NAMES, own32, bufs[n_big:])]
        for n, b in zip(BIG_NAMES, _sibling_gather("grad_half_gather", halves)):
            reduced[n][l] = b.reshape(-1, b.shape[-1])

    pending, token = None, ()
    for l in reversed(range(DEPTH)):
        wts, small, saved = layers[l]
        dact, g = _layer_bwd(dact, saved, wts, small, after=token)
        small_grads[l] = g
        if pending is not None:
            finish_exchange(l + 1, *pending, after=[dact])
        full = [g[n].reshape(N_CHIP, -1, g[n].shape[-1]) for n in BIG_NAMES]
        from_sibling = _sibling_swap_halves("grad_pair_swap", full)
        pair = [_pair_sum("grad_pair_sum_" + n, f, r, core) for n, f, r in zip(BIG_NAMES, full, from_sibling)]
        landing = [lax.empty((3, *p16.shape[1:]), BF16) for _, p16 in pair]
        started = _split_start("grad_exchange_start_%d" % l, [p16 for _, p16 in pair] + landing, _grad_exchange_copies,
                               3 * n_big)
        pending, token = ([p32 for p32, _ in pair], started), (started[3],)
    finish_exchange(0, *pending, after=[dact])
    grad_x = dact[None]

    stack = lambda key, rows=None: jnp.stack([small_grads[l][key][0] if rows is None else small_grads[l][key][:rows]
                                              for l in range(DEPTH)])
    local_small = dict(
        g_mix=stack("g_mix"), lower_bounds=stack("lbs"), g_hgrn_out=stack("g_hgrn_out"), w_conv=stack("w_conv", CONV_K),
        sg_ln_g=stack("sg_ln_g"), sg_ln_b=stack("sg_ln_b"), w_sg=jnp.stack([small_grads[l]["w_sg"] for l in range(DEPTH)]),
        b_sg=jnp.stack([small_grads[l]["b_sg"].reshape(SG_GROUPS, SG_CHUNK) for l in range(DEPTH)]),
        g_ffn=stack("g_ffn"), g_final=dg_final[0])
    shapes = [local_small[n].shape for n in SMALL_NAMES] + [(SUBLANES, LANES)]
    summed = _sum_devices(_gather_all("gather_small_grads", _pack([local_small[n] for n in SMALL_NAMES] + [loss_blk]), device))
    parts = _unpack(summed, shapes)
    loss = parts[-1][0, 0]
    small_grad = dict(zip(SMALL_NAMES, parts[:-1]))
    small_grad["lower_bounds"] = _lbs_bwd(lower_bounds, small_grad["lower_bounds"])
    small_grad["w_conv"] = lax.dynamic_slice_in_dim(small_grad["w_conv"], pos[0] * conv_cols, conv_cols, axis=2)
    g_flat = _pack([small_grad[n] for n in SMALL_NAMES])
    d_flat, m_flat, v_flat = _adamw_flat(_pack([weights[n] for n in SMALL_NAMES]), g_flat,
                                         _pack([mom1[n] for n in SMALL_NAMES]), _pack([mom2[n] for n in SMALL_NAMES]))
    small_shapes = [weights[n].shape for n in SMALL_NAMES]
    grads = dict(small_grad)
    delta = dict(zip(SMALL_NAMES, _unpack(d_flat, small_shapes)))
    new_m = dict(zip(SMALL_NAMES, _unpack(m_flat, small_shapes)))
    new_v = dict(zip(SMALL_NAMES, _unpack(v_flat, small_shapes)))

    for n in BIG_NAMES:
        outs = _adamw_layers("adamw_" + n, _as_2d(n, weights[n]), _as_2d(n, mom1[n]), _as_2d(n, mom2[n]), reduced[n])
        grads[n], delta[n], new_m[n], new_v[n] = [o.reshape(weights[n].shape) for o in outs]

    return (loss, grad_x, *[grads[n] for n in WEIGHT_ORDER], *[delta[n] for n in WEIGHT_ORDER],
            *[new_m[n] for n in WEIGHT_ORDER], *[new_v[n] for n in WEIGHT_ORDER])
```

```python
import numpy as np
import jax
import jax.numpy as jnp
from jax import lax
from jax.experimental import pallas as pl
from jax.experimental.pallas import tpu as pltpu

F32, BF16 = jnp.float32, jnp.bfloat16

D_MODEL = 1024
WIDTH = 512
N_BRANCH = 3
N_HEAD = 4
HEAD = 128
H_CHUNK = 64
CONV_K = 3
SG_CHUNK = 128
SG_GROUPS = 4
D_FF = 4096
DEPTH = 4
N_CHIP = 4
IN_COLS = 9 * WIDTH + N_BRANCH * D_MODEL
GATE_COL0 = 9 * WIDTH
LB_FLOOR = 1e-30
NORM_EPS = 1e-6
LN_EPS = 1e-5
ADAM_LR, ADAM_B1, ADAM_B2, ADAM_EPS, ADAM_WD, ADAM_STEP = 0.001, 0.9, 0.999, 1e-08, 0.01, 10

VMEM_LIMIT_BYTES = 48 * 1024 * 1024
SUBLANES, LANES = 8, 128
ELEMWISE_BLOCK_BYTES = 2 * 1024 * 1024

NN = (((1,), (0,)), ((), ()))
NT = (((1,), (1,)), ((), ()))
TN = (((0,), (0,)), ((), ()))
MESH = pl.DeviceIdType.MESH
ANY = pl.BlockSpec(memory_space=pl.ANY)


def _dot(a, b, dims=NN):
    return lax.dot_general(a.astype(BF16), b.astype(BF16), dims, preferred_element_type=F32)


def _params(n_axes):
    return pltpu.CompilerParams(dimension_semantics=("arbitrary",) * n_axes, vmem_limit_bytes=VMEM_LIMIT_BYTES)


def _row0(part, rows=SUBLANES):
    r = lax.broadcasted_iota(jnp.int32, (rows, part.shape[1]), 0)
    return jnp.where(r == 0, part, 0.0)


def _token_tile(T):
    return min(512, T)


def _matmul(name, a, b, *, dims, grid, a_spec, b_spec, out_specs, out_shapes, acc_shape,
            extra=(), extra_specs=(), epilogue=None, after=()):
    nk = grid[2]
    n_extra, n_out, n_in = len(extra), len(out_shapes), 2 + len(extra) + len(after)

    def body(*refs):
        a_ref, b_ref = refs[0], refs[1]
        ex = refs[2:2 + n_extra]
        outs = refs[n_in:n_in + n_out]
        acc = refs[-1]
        kk = pl.program_id(2)
        part = _dot(a_ref[...], b_ref[...], dims)

        @pl.when(kk == 0)
        def _():
            acc[...] = part

        @pl.when(kk > 0)
        def _():
            acc[...] += part

        @pl.when(kk == nk - 1)
        def _():
            res = epilogue(acc[...], *[e[...] for e in ex]) if epilogue else (acc[...],)
            for o, r in zip(outs, res):
                o[...] = r.astype(o.dtype)

    return pl.pallas_call(
        body, name=name, grid=grid,
        in_specs=[a_spec, b_spec, *extra_specs, *[ANY] * len(after)], out_specs=list(out_specs),
        out_shape=list(out_shapes), scratch_shapes=[pltpu.VMEM(acc_shape, F32)], compiler_params=_params(3),
    )(a, b, *extra, *after)


def _mm_cols(name, a, w, out_dtypes, epilogue=None, extra=()):
    T, K = a.shape
    N = w.shape[2]
    tm = _token_tile(T)
    blk = pl.BlockSpec((tm, N), lambda j, i, kk: (i, j))
    return _matmul(
        name, a, w, dims=NN, grid=(N_CHIP, T // tm, 1),
        a_spec=pl.BlockSpec((tm, K), lambda j, i, kk: (i, 0)),
        b_spec=pl.BlockSpec((None, K, N), lambda j, i, kk: (j, 0, 0)),
        out_specs=[blk] * len(out_dtypes),
        out_shapes=[jax.ShapeDtypeStruct((T, N_CHIP * N), dt) for dt in out_dtypes],
        acc_shape=(tm, N), extra=extra, extra_specs=[blk] * len(extra), epilogue=epilogue)


def _mm_rows(name, a, w, res):
    T = a.shape[0]
    K, N = N_CHIP * w.shape[1], w.shape[2]
    tm = _token_tile(T)
    blk = pl.BlockSpec((tm, N), lambda i, j, kk: (i, 0))
    return _matmul(
        name, a, w.reshape(K, N), dims=NN, grid=(T // tm, 1, 1),
        a_spec=pl.BlockSpec((tm, K), lambda i, j, kk: (i, 0)),
        b_spec=pl.BlockSpec((K, N), lambda i, j, kk: (0, 0)),
        out_specs=[blk], out_shapes=[jax.ShapeDtypeStruct((T, N), F32)], acc_shape=(tm, N),
        extra=(res,), extra_specs=[blk], epilogue=lambda acc, r: (acc + r,))[0]


def _mm_cols_t(name, g, w, out_dtype, epilogue=None, extra=(), after=()):
    T, N = g.shape
    K = w.shape[1]
    tm = _token_tile(T)
    blk = pl.BlockSpec((tm, K), lambda j, i, kk: (i, j))
    return _matmul(
        name, g, w, dims=NT, grid=(N_CHIP, T // tm, 1),
        a_spec=pl.BlockSpec((tm, N), lambda j, i, kk: (i, 0)),
        b_spec=pl.BlockSpec((None, K, N), lambda j, i, kk: (j, 0, 0)),
        out_specs=[blk], out_shapes=[jax.ShapeDtypeStruct((T, N_CHIP * K), out_dtype)], acc_shape=(tm, K),
        extra=extra, extra_specs=[blk] * len(extra), epilogue=epilogue, after=after)[0]


def _mm_rows_t(name, g, w):
    T = g.shape[0]
    K, N = w.shape[1], w.shape[2]
    tm = _token_tile(T)
    blk = pl.BlockSpec((tm, K), lambda i, j, kk: (i, 0))
    return _matmul(
        name, g, w, dims=NT, grid=(T // tm, 1, N_CHIP),
        a_spec=pl.BlockSpec((tm, N), lambda i, j, kk: (i, kk)),
        b_spec=pl.BlockSpec((None, K, N), lambda i, j, kk: (kk, 0, 0)),
        out_specs=[blk], out_shapes=[jax.ShapeDtypeStruct((T, K), F32)], acc_shape=(tm, K))[0]


def _mm_wgrad(name, a, g, a_cols, g_cols, a_blocked, g_blocked):
    T = a.shape[0]
    tt = _token_tile(T)
    return _matmul(
        name, a, g, dims=TN, grid=(N_CHIP, 1, T // tt),
        a_spec=pl.BlockSpec((tt, a_cols), (lambda j, i, kk: (kk, j)) if a_blocked else (lambda j, i, kk: (kk, 0))),
        b_spec=pl.BlockSpec((tt, g_cols), (lambda j, i, kk: (kk, j)) if g_blocked else (lambda j, i, kk: (kk, 0))),
        out_specs=[pl.BlockSpec((None, a_cols, g_cols), lambda j, i, kk: (j, 0, 0))],
        out_shapes=[jax.ShapeDtypeStruct((N_CHIP, a_cols, g_cols), F32)], acc_shape=(a_cols, g_cols))[0]


def _rms_fwd(name, x, g):
    T, Dm = x.shape
    tm = min(256, T)

    def body(x_ref, g_ref, h_ref):
        xv = x_ref[...]
        r = lax.rsqrt(jnp.mean(xv * xv, axis=-1, keepdims=True) + NORM_EPS)
        h_ref[...] = (xv * r * g_ref[...]).astype(BF16)

    return pl.pallas_call(
        body, name=name, grid=(T // tm,),
        in_specs=[pl.BlockSpec((tm, Dm), lambda i: (i, 0)), pl.BlockSpec((1, Dm), lambda i: (0, 0))],
        out_specs=pl.BlockSpec((tm, Dm), lambda i: (i, 0)),
        out_shape=jax.ShapeDtypeStruct((T, Dm), BF16), compiler_params=_params(1))(x, g)


def _rms_bwd(name, x, g, dh, dres):
    T, Dm = x.shape
    tm = min(256, T)

    def body(x_ref, g_ref, dh_ref, dres_ref, dx_ref, dg_ref):
        xv = x_ref[...]
        r = lax.rsqrt(jnp.mean(xv * xv, axis=-1, keepdims=True) + NORM_EPS)
        xn = xv * r
        dhv = dh_ref[...]
        dxn = dhv * g_ref[...]
        dx_ref[...] = dres_ref[...] + r * (dxn - xn * jnp.mean(dxn * xn, axis=-1, keepdims=True))

        @pl.when(pl.program_id(0) == 0)
        def _():
            dg_ref[...] = jnp.zeros_like(dg_ref)

        dg_ref[...] += _row0(jnp.sum(dhv * xn, axis=0, keepdims=True))

    tile = pl.BlockSpec((tm, Dm), lambda i: (i, 0))
    return pl.pallas_call(
        body, name=name, grid=(T // tm,),
        in_specs=[tile, pl.BlockSpec((1, Dm), lambda i: (0, 0)), tile, tile],
        out_specs=[tile, pl.BlockSpec((SUBLANES, Dm), lambda i: (0, 0))],
        out_shape=[jax.ShapeDtypeStruct((T, Dm), F32), jax.ShapeDtypeStruct((SUBLANES, Dm), F32)],
        compiler_params=_params(1))(x, g, dh, dres)


def _loss_head(x, g, tgt):
    T, Dm = x.shape
    tm = min(256, T)

    def body(x_ref, g_ref, t_ref, loss_ref, dx_ref, dg_ref):
        xv = x_ref[...]
        gv = g_ref[...]
        r = lax.rsqrt(jnp.mean(xv * xv, axis=-1, keepdims=True) + NORM_EPS)
        xn = xv * r
        err = xn * gv - t_ref[...]
        dy = err * (1.0 / Dm)
        dxn = dy * gv
        dx_ref[...] = r * (dxn - xn * jnp.mean(dxn * xn, axis=-1, keepdims=True))

        @pl.when(pl.program_id(0) == 0)
        def _():
            dg_ref[...] = jnp.zeros_like(dg_ref)
            loss_ref[...] = jnp.zeros_like(loss_ref)

        dg_ref[...] += _row0(jnp.sum(dy * xn, axis=0, keepdims=True))
        part = jnp.sum(jnp.sum(err * err, axis=-1, keepdims=True), axis=0, keepdims=True) * (0.5 / Dm)
        loss_ref[...] += jnp.broadcast_to(part, loss_ref.shape)

    tile = pl.BlockSpec((tm, Dm), lambda i: (i, 0))
    return pl.pallas_call(
        body, name="loss_head", grid=(T // tm,),
        in_specs=[tile, pl.BlockSpec((1, Dm), lambda i: (0, 0)), tile],
        out_specs=[pl.BlockSpec((SUBLANES, LANES), lambda i: (0, 0)), tile,
                   pl.BlockSpec((SUBLANES, Dm), lambda i: (0, 0))],
        out_shape=[jax.ShapeDtypeStruct((SUBLANES, LANES), F32), jax.ShapeDtypeStruct((T, Dm), F32),
                   jax.ShapeDtypeStruct((SUBLANES, Dm), F32)],
        compiler_params=_params(1))(x, g, tgt)


def _softmax_rows(lb_ref):
    rows = [lb_ref[pl.ds(i, 1), :] for i in range(DEPTH)]
    mx = rows[0]
    for r in rows[1:]:
        mx = jnp.maximum(mx, r)
    es = [jnp.exp(r - mx) for r in rows]
    tot = es[0]
    for e in es[1:]:
        tot = tot + e
    return [e / tot for e in es]


def _lbs_fwd(lower_bounds):
    def body(lb_ref, out_ref):
        sm = _softmax_rows(lb_ref)
        run = jnp.zeros_like(sm[0])
        out_ref[pl.ds(0, 1), :] = run
        for i in range(1, DEPTH):
            run = run + sm[i]
            out_ref[pl.ds(i, 1), :] = run

    return pl.pallas_call(body, name="lbs_fwd", out_shape=jax.ShapeDtypeStruct(lower_bounds.shape, F32))(lower_bounds)


def _lbs_bwd(lower_bounds, dlbs):
    def body(lb_ref, d_ref, out_ref):
        sm = _softmax_rows(lb_ref)
        dsm = [jnp.zeros_like(sm[0])]
        for i in range(1, DEPTH):
            acc = d_ref[pl.ds(i, 1), :]
            for l in range(i + 1, DEPTH):
                acc = acc + d_ref[pl.ds(l, 1), :]
            dsm.append(acc)
        inner = dsm[0] * sm[0]
        for i in range(1, DEPTH):
            inner = inner + dsm[i] * sm[i]
        for i in range(DEPTH):
            out_ref[pl.ds(i, 1), :] = sm[i] * (dsm[i] - inner)

    return pl.pallas_call(body, name="lbs_bwd", out_shape=jax.ShapeDtypeStruct(lower_bounds.shape, F32))(lower_bounds, dlbs)


N_LEVEL = 6


def _hgrn_consts():
    L = H_CHUNK
    t = np.arange(L)
    blocks = [(t[:, None] >= t[None, :]).astype(np.float32)]
    masks = []
    m = L // 2
    while m >= 1:
        blk, pos = t // (2 * m), t % (2 * m)
        start = blk * 2 * m
        mat = np.zeros((L, L), np.float32)
        for r in range(L):
            if pos[r] >= m:
                mat[r, start[r] + m:r + 1] = 1.0
            else:
                mat[r, r + 1:start[r] + m] = -1.0
        blocks.append(mat)
        masks.append(((blk[:, None] == blk[None, :]) & (pos[:, None] >= m) & (pos[None, :] < m)).astype(np.float32))
        m //= 2
    blocks.append(np.ones((L, L), np.float32))
    return jnp.asarray(np.concatenate(blocks, 0), BF16), jnp.asarray(np.stack(masks), F32)


def _hgrn_core(qraw, fp, lb, sum_mat, mask_ref):
    L = H_CHUNK
    sq = jax.nn.sigmoid(qraw)
    q = qraw * sq
    sneg = jax.nn.sigmoid(-fp)
    log_sig = jnp.minimum(fp, 0.0) - jnp.log1p(jnp.exp(-jnp.abs(fp)))
    a1 = jnp.log(jnp.maximum(lb, LB_FLOOR))
    a2 = jnp.log1p(-lb) + log_sig
    logf = jnp.maximum(a1, a2) + jnp.log1p(jnp.exp(-jnp.abs(a1 - a2)))
    w1 = jnp.exp(a1 - logf)
    w2 = jnp.exp(a2 - logf)
    k = (1.0 - lb) * sneg
    hi = logf.astype(BF16)
    r1 = logf - hi.astype(F32)
    mid = r1.astype(BF16)
    lo = (r1 - mid.astype(F32)).astype(BF16)
    sums = lax.dot_general(sum_mat, jnp.concatenate([hi, mid, lo], axis=1), NN, preferred_element_type=F32)
    sums = sums[:, 0:HEAD] + sums[:, HEAD:2 * HEAD] + sums[:, 2 * HEAD:3 * HEAD]
    b = sums[0:L]
    b_last = sums[(N_LEVEL + 1) * L:(N_LEVEL + 2) * L]
    eye = lax.broadcasted_iota(jnp.int32, (L, L), 0) == lax.broadcasted_iota(jnp.int32, (L, L), 1)
    attn = jnp.where(eye, jnp.sum(q * k, axis=1, keepdims=True), 0.0)
    fa, fb, ea, eb = [], [], [], []
    for l in range(N_LEVEL):
        d = sums[(l + 1) * L:(l + 2) * L]
        e_a = jnp.exp(jnp.minimum(d, 0.0))
        e_b = jnp.exp(jnp.minimum(-d, 0.0))
        a_l, b_l = q * e_a, k * e_b
        attn = attn + mask_ref[l] * _dot(a_l, b_l, NT)
        fa.append(a_l), fb.append(b_l), ea.append(e_a), eb.append(e_b)
    return dict(sq=sq, q=q, sneg=sneg, logf=logf, w1=w1, w2=w2, k=k, b=b, b_last=b_last, attn=attn,
                fa=fa, fb=fb, ea=ea, eb=eb)


def _hgrn_fwd(p, lbrow, gout):
    T = p.shape[0]
    nch = T // H_CHUNK
    sum_mat, masks = _hgrn_consts()

    def body(p_ref, lb_ref, g_ref, m_ref, mask_ref, o_ref, z_ref, st_ref, state):
        @pl.when(pl.program_id(0) == 0)
        def _():
            state[...] = jnp.zeros_like(state)

        sum_m = m_ref[...]
        for h in range(N_HEAD):
            col = lambda part: pl.ds(part * WIDTH + h * HEAD, HEAD)
            hs = pl.ds(h * HEAD, HEAD)
            v = p_ref[:, col(2)]
            c = _hgrn_core(p_ref[:, col(0)], p_ref[:, col(1)], lb_ref[:, hs], sum_m, mask_ref)
            s0 = state[h]
            st_ref[h] = s0
            o = _dot(c["attn"], v) + _dot(c["q"] * jnp.exp(c["b"]), s0, NT)
            k_dec = c["k"] * jnp.exp(c["b_last"] - c["b"])
            decay = jnp.exp(jnp.max(c["b_last"], axis=0, keepdims=True))
            state[h] = s0 * decay + _dot(v, k_dec, TN)
            o_ref[:, hs] = o
            r = lax.rsqrt(jnp.mean(o * o, axis=-1, keepdims=True) + NORM_EPS)
            z_ref[:, hs] = (o * r * g_ref[:, hs] * jax.nn.sigmoid(p_ref[:, col(3)])).astype(BF16)

    full = lambda shape: pl.BlockSpec(shape, lambda c: (0,) * len(shape))
    return pl.pallas_call(
        body, name="hgrn_fwd", grid=(nch,),
        in_specs=[pl.BlockSpec((H_CHUNK, 4 * WIDTH), lambda c: (c, 0)), full((1, WIDTH)), full((1, WIDTH)),
                  full(sum_mat.shape), full(masks.shape)],
        out_specs=[pl.BlockSpec((H_CHUNK, WIDTH), lambda c: (c, 0)),
                   pl.BlockSpec((None, H_CHUNK, WIDTH), lambda c: (0, c, 0)),
                   pl.BlockSpec((None, N_HEAD, HEAD, HEAD), lambda c: (c, 0, 0, 0))],
        out_shape=[jax.ShapeDtypeStruct((T, WIDTH), F32), jax.ShapeDtypeStruct((N_BRANCH, T, WIDTH), BF16),
                   jax.ShapeDtypeStruct((nch, N_HEAD, HEAD, HEAD), F32)],
        scratch_shapes=[pltpu.VMEM((N_HEAD, HEAD, HEAD), F32)], compiler_params=_params(1),
    )(p, lbrow, gout, sum_mat, masks)


def _hgrn_bwd(p, o_saved, dz, states, lbrow, gout, dp):
    T = p.shape[0]
    nch = T // H_CHUNK
    L = H_CHUNK
    sum_mat, masks = _hgrn_consts()

    def body(p_ref, o_ref, dz_ref, st_ref, lb_ref, g_ref, m_ref, mask_ref, dp_in, dp_ref, dlb_ref, dg_ref, dstate):
        del dp_in

        @pl.when(pl.program_id(0) == 0)
        def _():
            dstate[...] = jnp.zeros_like(dstate)
            dlb_ref[...] = jnp.zeros_like(dlb_ref)
            dg_ref[...] = jnp.zeros_like(dg_ref)

        sum_m = m_ref[...]
        for h in range(N_HEAD):
            col = lambda part: pl.ds(part * WIDTH + h * HEAD, HEAD)
            hs = pl.ds(h * HEAD, HEAD)
            qraw, fp, v, go = p_ref[:, col(0)], p_ref[:, col(1)], p_ref[:, col(2)], p_ref[:, col(3)]
            lb, g = lb_ref[:, hs], g_ref[:, hs]
            c = _hgrn_core(qraw, fp, lb, sum_m, mask_ref)
            q, k, b, b_last = c["q"], c["k"], c["b"], c["b_last"]
            s0, ds1 = st_ref[h], dstate[h]
            e_b = jnp.exp(b)
            q_dec = q * e_b
            e_bl = jnp.exp(b_last - b)
            k_dec = k * e_bl
            decay = jnp.exp(jnp.max(b_last, axis=0, keepdims=True))
            o = o_ref[:, hs]
            r = lax.rsqrt(jnp.mean(o * o, axis=-1, keepdims=True) + NORM_EPS)
            n = o * r
            sgo = jax.nn.sigmoid(go)
            dza = dz_ref[:, hs]
            dgo = dza * n * g * sgo * (1.0 - sgo)
            dg_ref[:, hs] += _row0(jnp.sum(dza * n * sgo, axis=0, keepdims=True))
            dn = dza * g * sgo
            do = r * (dn - n * jnp.mean(dn * n, axis=-1, keepdims=True))
            dattn = _dot(do, v, NT)
            dv = _dot(c["attn"], do, TN) + _dot(k_dec, ds1, NT)
            dq_dec = _dot(do, s0)
            dk_dec = _dot(v, ds1)
            ddiag = jnp.sum(do * v, axis=1, keepdims=True)
            dq = dq_dec * e_b + ddiag * k
            dk = dk_dec * e_bl + ddiag * q
            dsums = [dq_dec * q_dec - dk_dec * k_dec]
            for l in range(N_LEVEL):
                dm = mask_ref[l] * dattn
                da = _dot(dm, c["fb"][l])
                db = _dot(dm, c["fa"][l], TN)
                dq = dq + da * c["ea"][l]
                dk = dk + db * c["eb"][l]
                dsums.append(da * c["fa"][l] - db * c["fb"][l])
            dlast = jnp.sum(ds1 * s0, axis=0, keepdims=True) * decay
            dsums.append(dk_dec * k_dec + _row0(dlast, L))
            dlogf = _dot(sum_m, jnp.concatenate(dsums, axis=0), TN)
            dstate[h] = ds1 * decay + _dot(do, q_dec, TN)
            sq, sneg = c["sq"], c["sneg"]
            dqraw = dq * sq * (1.0 + qraw * (1.0 - sq))
            dfp = dlogf * c["w2"] * sneg - dk * (1.0 - lb) * sneg * (1.0 - sneg)
            inv_lb = jnp.where(lb > LB_FLOOR, 1.0 / jnp.maximum(lb, LB_FLOOR), 0.0)
            dlb_tok = dlogf * (c["w1"] * inv_lb - c["w2"] / (1.0 - lb)) - dk * sneg
            dlb_ref[:, hs] += _row0(jnp.sum(dlb_tok, axis=0, keepdims=True))
            dp_ref[:, col(0)] = dqraw.astype(BF16)
            dp_ref[:, col(1)] = dfp.astype(BF16)
            dp_ref[:, col(2)] = dv.astype(BF16)
            dp_ref[:, col(3)] = dgo.astype(BF16)

    full = lambda shape: pl.BlockSpec(shape, lambda c: (0,) * len(shape))
    rev = lambda c: nch - 1 - c
    return pl.pallas_call(
        body, name="hgrn_bwd", grid=(nch,),
        in_specs=[pl.BlockSpec((L, 4 * WIDTH), lambda c: (rev(c), 0)), pl.BlockSpec((L, WIDTH), lambda c: (rev(c), 0)),
                  pl.BlockSpec((None, L, WIDTH), lambda c: (0, rev(c), 0)),
                  pl.BlockSpec((None, N_HEAD, HEAD, HEAD), lambda c: (rev(c), 0, 0, 0)),
                  full((1, WIDTH)), full((1, WIDTH)), full(sum_mat.shape), full(masks.shape), ANY],
        out_specs=[pl.BlockSpec((L, 4 * WIDTH), lambda c: (rev(c), 0)), full((SUBLANES, WIDTH)), full((SUBLANES, WIDTH))],
        out_shape=[jax.ShapeDtypeStruct(dp.shape, dp.dtype), jax.ShapeDtypeStruct((SUBLANES, WIDTH), F32),
                   jax.ShapeDtypeStruct((SUBLANES, WIDTH), F32)],
        scratch_shapes=[pltpu.VMEM((N_HEAD, HEAD, HEAD), F32)], input_output_aliases={8: 0},
        compiler_params=_params(1),
    )(p, o_saved, dz, states, lbrow, gout, sum_mat, masks, dp)


def _shift_down(tile, halo, s):
    tm = tile.shape[0]
    rows = lax.broadcasted_iota(jnp.int32, tile.shape, 0)
    head = jnp.concatenate([pltpu.roll(halo, s, 0), jnp.zeros((tm - SUBLANES, tile.shape[1]), tile.dtype)], axis=0)
    return jnp.where(rows < s, head, pltpu.roll(tile, s, 0))


def _shift_up(tile, halo, s):
    tm = tile.shape[0]
    rows = lax.broadcasted_iota(jnp.int32, tile.shape, 0)
    tail = jnp.concatenate([jnp.zeros((tm - SUBLANES, tile.shape[1]), tile.dtype), pltpu.roll(halo, SUBLANES - s, 0)], axis=0)
    return jnp.where(rows >= tm - s, tail, pltpu.roll(tile, tm - s, 0))


def _conv_fwd(p, w, z):
    T = p.shape[0]
    tm = _token_tile(T)
    per = tm // SUBLANES

    def body(bg_ref, cg_ref, xc_ref, hcg_ref, hxc_ref, w_ref, z_in, z_ref):
        del z_in
        zc = cg_ref[...] * xc_ref[...]
        hz = jnp.where(pl.program_id(0) > 0, hcg_ref[...] * hxc_ref[...], 0.0)
        y = (w_ref[pl.ds(0, 1), :] * _shift_down(zc, hz, 2) + w_ref[pl.ds(1, 1), :] * _shift_down(zc, hz, 1)
             + w_ref[pl.ds(2, 1), :] * zc)
        z_ref[...] = (bg_ref[...] * y).astype(BF16)

    tile = lambda cb: pl.BlockSpec((tm, WIDTH), lambda i: (i, cb))
    prev = lambda cb: pl.BlockSpec((SUBLANES, WIDTH), lambda i: (jnp.maximum(i * per - 1, 0), cb))
    return pl.pallas_call(
        body, name="conv_fwd", grid=(T // tm,),
        in_specs=[tile(4), tile(5), tile(6), prev(5), prev(6), pl.BlockSpec((CONV_K, WIDTH), lambda i: (0, 0)), ANY],
        out_specs=pl.BlockSpec((None, tm, WIDTH), lambda i: (1, i, 0)),
        out_shape=jax.ShapeDtypeStruct(z.shape, z.dtype), input_output_aliases={6: 0}, compiler_params=_params(1),
    )(p, p, p, p, p, w, z)


def _conv_bwd(p, w, dz, dp):
    T = p.shape[0]
    tm = _token_tile(T)
    per = tm // SUBLANES
    last = T // SUBLANES - 1

    def body(bg_ref, cg_ref, xc_ref, hcg_ref, hxc_ref, nbg_ref, dzb_ref, ndzb_ref, w_ref, dp_in, dp_ref, dw_ref, stash):
        del dp_in
        i, jj = pl.program_id(0), pl.program_id(1)

        @pl.when(jnp.logical_and(i == 0, jj == 0))
        def _():
            dw_ref[...] = jnp.zeros_like(dw_ref)

        @pl.when(jj == 0)
        def _():
            cg, xc, bg = cg_ref[...], xc_ref[...], bg_ref[...]
            w0, w1, w2 = w_ref[pl.ds(0, 1), :], w_ref[pl.ds(1, 1), :], w_ref[pl.ds(2, 1), :]
            zc = cg * xc
            hz = jnp.where(i > 0, hcg_ref[...] * hxc_ref[...], 0.0)
            z2, z1 = _shift_down(zc, hz, 2), _shift_down(zc, hz, 1)
            y = w0 * z2 + w1 * z1 + w2 * zc
            dzb = dzb_ref[...]
            dy = dzb * bg
            hdy = jnp.where(i < pl.num_programs(0) - 1, ndzb_ref[...] * nbg_ref[...], 0.0)
            dzc = w2 * dy + w1 * _shift_up(dy, hdy, 1) + w0 * _shift_up(dy, hdy, 2)
            rows = lax.broadcasted_iota(jnp.int32, (SUBLANES, WIDTH), 0)
            colsum = lambda t: jnp.sum(t, axis=0, keepdims=True)
            dw_ref[...] += (jnp.where(rows == 0, colsum(dy * z2), 0.0) + jnp.where(rows == 1, colsum(dy * z1), 0.0)
                            + jnp.where(rows == 2, colsum(dy * zc), 0.0))
            dp_ref[...] = (dzb * y).astype(BF16)
            stash[0] = dzc * xc
            stash[1] = dzc * cg

        @pl.when(jj > 0)
        def _():
            dp_ref[...] = stash[jj - 1].astype(BF16)

    n_tiles = T // tm
    tile = lambda cb: pl.BlockSpec((tm, WIDTH), lambda i, jj: (i, cb))
    prev = lambda cb: pl.BlockSpec((SUBLANES, WIDTH), lambda i, jj: (jnp.maximum(i * per - 1, 0), cb))
    nxt = lambda i: jnp.minimum((i + 1) * per, last)
    return pl.pallas_call(
        body, name="conv_bwd", grid=(n_tiles, 3),
        in_specs=[tile(4), tile(5), tile(6), prev(5), prev(6),
                  pl.BlockSpec((SUBLANES, WIDTH), lambda i, jj: (nxt(i), 4)),
                  pl.BlockSpec((None, tm, WIDTH), lambda i, jj: (1, i, 0)),
                  pl.BlockSpec((None, SUBLANES, WIDTH), lambda i, jj: (1, nxt(i), 0)),
                  pl.BlockSpec((CONV_K, WIDTH), lambda i, jj: (0, 0)), ANY],
        out_specs=[pl.BlockSpec((tm, WIDTH), lambda i, jj: (i, 4 + jj)),
                   pl.BlockSpec((SUBLANES, WIDTH), lambda i, jj: (0, 0))],
        out_shape=[jax.ShapeDtypeStruct(dp.shape, dp.dtype), jax.ShapeDtypeStruct((SUBLANES, WIDTH), F32)],
        scratch_shapes=[pltpu.VMEM((2, tm, WIDTH), F32)], input_output_aliases={9: 0}, compiler_params=_params(2),
    )(p, p, p, p, p, p, dz, dz, w, dp)


GELU_C = float(np.sqrt(2.0 / np.pi))
GELU_A = 0.044715


def _gelu(x):
    th = jnp.tanh(GELU_C * (x + GELU_A * x * x * x))
    return 0.5 * x * (1.0 + th), th


def _gelu_grad(x, th):
    return 0.5 * (1.0 + th) + 0.5 * x * (1.0 - th * th) * GELU_C * (1.0 + 3.0 * GELU_A * x * x)


def _sg_core(u, v, lng, lnb, ws_ref, bs_ref):
    gu, thu = _gelu(u)
    gv, thv = _gelu(v)
    xc = gv - jnp.mean(gv, axis=-1, keepdims=True)
    rs = lax.rsqrt(jnp.mean(xc * xc, axis=-1, keepdims=True) + LN_EPS)
    xh = xc * rs
    vp = xh * lng + lnb
    tril = (lax.broadcasted_iota(jnp.int32, (SG_CHUNK, SG_CHUNK), 0)
            >= lax.broadcasted_iota(jnp.int32, (SG_CHUNK, SG_CHUNK), 1))
    wm = [jnp.where(tril, ws_ref[g], 0.0).astype(BF16) for g in range(SG_GROUPS)]
    gs = lambda t, g: t[:, g * LANES:(g + 1) * LANES]
    sv = jnp.concatenate([_dot(wm[g], gs(vp, g)) + bs_ref[g] for g in range(SG_GROUPS)], axis=1)
    return dict(gu=gu, thu=thu, thv=thv, rs=rs, xh=xh, vp=vp, tril=tril, wm=wm, sv=sv)


def _sg_fwd(p, lng, lnb, ws, bs, z):
    T = p.shape[0]

    def body(u_ref, v_ref, lng_ref, lnb_ref, ws_ref, bs_ref, z_in, z_ref):
        del z_in
        c = _sg_core(u_ref[...], v_ref[...], lng_ref[...], lnb_ref[...], ws_ref, bs_ref)
        z_ref[...] = (c["gu"] * c["sv"]).astype(BF16)

    full = lambda shape: pl.BlockSpec(shape, lambda c: (0,) * len(shape))
    return pl.pallas_call(
        body, name="sg_fwd", grid=(T // SG_CHUNK,),
        in_specs=[pl.BlockSpec((SG_CHUNK, WIDTH), lambda c: (c, 7)), pl.BlockSpec((SG_CHUNK, WIDTH), lambda c: (c, 8)),
                  full((1, WIDTH)), full((1, WIDTH)), full(ws.shape), full(bs.shape), ANY],
        out_specs=pl.BlockSpec((None, SG_CHUNK, WIDTH), lambda c: (2, c, 0)),
        out_shape=jax.ShapeDtypeStruct(z.shape, z.dtype), input_output_aliases={6: 0}, compiler_params=_params(1),
    )(p, p, lng, lnb, ws, bs, z)


def _sg_bwd(p, lng, lnb, ws, bs, dz, dp):
    T = p.shape[0]

    def body(u_ref, v_ref, lng_ref, lnb_ref, ws_ref, bs_ref, dz_ref, dp_in, dp_ref, dws_ref, dbs_ref, dlng_ref, dlnb_ref,
             stash):
        del dp_in
        cidx, jj = pl.program_id(0), pl.program_id(1)

        @pl.when(jnp.logical_and(cidx == 0, jj == 0))
        def _():
            dws_ref[...] = jnp.zeros_like(dws_ref)
            dbs_ref[...] = jnp.zeros_like(dbs_ref)
            dlng_ref[...] = jnp.zeros_like(dlng_ref)
            dlnb_ref[...] = jnp.zeros_like(dlnb_ref)

        @pl.when(jj == 0)
        def _():
            u, v, lng = u_ref[...], v_ref[...], lng_ref[...]
            c = _sg_core(u, v, lng, lnb_ref[...], ws_ref, bs_ref)
            dzc = dz_ref[...]
            gs = lambda t, g: t[:, g * LANES:(g + 1) * LANES]
            dsv = dzc * c["gu"]
            dvp = []
            for g in range(SG_GROUPS):
                dsv_g = gs(dsv, g)
                dws_ref[g] += jnp.where(c["tril"], _dot(dsv_g, gs(c["vp"], g), NT), 0.0)
                dbs_ref[g] += jnp.sum(dsv_g, axis=1, keepdims=True)
                dvp.append(_dot(c["wm"][g], dsv_g, TN))
            dvp = jnp.concatenate(dvp, axis=1)
            xh = c["xh"]
            dlng_ref[...] += _row0(jnp.sum(dvp * xh, axis=0, keepdims=True))
            dlnb_ref[...] += _row0(jnp.sum(dvp, axis=0, keepdims=True))
            dxh = dvp * lng
            dgv = c["rs"] * (dxh - jnp.mean(dxh, axis=-1, keepdims=True) - xh * jnp.mean(dxh * xh, axis=-1, keepdims=True))
            dp_ref[...] = (dzc * c["sv"] * _gelu_grad(u, c["thu"])).astype(BF16)
            stash[...] = dgv * _gelu_grad(v, c["thv"])

        @pl.when(jj == 1)
        def _():
            dp_ref[...] = stash[...].astype(BF16)

    full = lambda shape: pl.BlockSpec(shape, lambda c, jj: (0,) * len(shape))
    return pl.pallas_call(
        body, name="sg_bwd", grid=(T // SG_CHUNK, 2),
        in_specs=[pl.BlockSpec((SG_CHUNK, WIDTH), lambda c, jj: (c, 7)), pl.BlockSpec((SG_CHUNK, WIDTH), lambda c, jj: (c, 8)),
                  full((1, WIDTH)), full((1, WIDTH)), full(ws.shape), full(bs.shape),
                  pl.BlockSpec((None, SG_CHUNK, WIDTH), lambda c, jj: (2, c, 0)), ANY],
        out_specs=[pl.BlockSpec((SG_CHUNK, WIDTH), lambda c, jj: (c, 7 + jj)), full(ws.shape), full(bs.shape),
                   full((SUBLANES, WIDTH)), full((SUBLANES, WIDTH))],
        out_shape=[jax.ShapeDtypeStruct(dp.shape, dp.dtype), jax.ShapeDtypeStruct(ws.shape, F32),
                   jax.ShapeDtypeStruct(bs.shape, F32), jax.ShapeDtypeStruct((SUBLANES, WIDTH), F32),
                   jax.ShapeDtypeStruct((SUBLANES, WIDTH), F32)],
        scratch_shapes=[pltpu.VMEM((SG_CHUNK, WIDTH), F32)], input_output_aliases={7: 0}, compiler_params=_params(2),
    )(p, p, lng, lnb, ws, bs, dz, dp)


BRANCH_COLS = D_MODEL // N_CHIP
GATE_UNIT0 = GATE_COL0 // WIDTH
UNITS = D_MODEL // WIDTH


def _unit_specs(order):
    def spec(which):
        def index(*g):
            _, n, u = order(*g)
            return (2 * u + which, n, 0, 0)
        return pl.BlockSpec((None, None, WIDTH, BRANCH_COLS), index)
    return [spec(0), spec(1)]


def _merge_fwd(z, p, wb):
    T = z.shape[1]
    tm = _token_tile(T)
    order = lambda i, u, n: (i, n, u)

    def body(z_ref, wa_ref, wb_ref, gt_ref, out_ref, acc):
        n = pl.program_id(2)
        zv = z_ref[...]
        y = jnp.concatenate([_dot(zv, wa_ref[...]), _dot(zv, wb_ref[...])], axis=1)
        part = jax.nn.sigmoid(gt_ref[...]) * y

        @pl.when(n == 0)
        def _():
            acc[...] = part

        @pl.when(n > 0)
        def _():
            acc[...] += part

        @pl.when(n == N_BRANCH - 1)
        def _():
            out_ref[...] = acc[...].astype(BF16)

    return pl.pallas_call(
        body, name="merge_fwd", grid=(T // tm, UNITS, N_BRANCH),
        in_specs=[pl.BlockSpec((None, tm, WIDTH), lambda i, u, n: (n, i, 0)), *_unit_specs(order),
                  pl.BlockSpec((tm, WIDTH), lambda i, u, n: (i, GATE_UNIT0 + UNITS * n + u))],
        out_specs=pl.BlockSpec((tm, WIDTH), lambda i, u, n: (i, u)),
        out_shape=jax.ShapeDtypeStruct((T, D_MODEL), BF16),
        scratch_shapes=[pltpu.VMEM((tm, WIDTH), F32)], compiler_params=_params(3))(z, wb, wb, p)


def _merge_bwd(z, p, wb, dmerged):
    T = z.shape[1]
    tm = _token_tile(T)
    order = lambda i, n, u: (i, n, u)

    def body(z_ref, wa_ref, wb_ref, gt_ref, dm_ref, dp_ref, dy_ref, dz_ref):
        u = pl.program_id(2)
        zv, wa, wbv = z_ref[...], wa_ref[...], wb_ref[...]
        y = jnp.concatenate([_dot(zv, wa), _dot(zv, wbv)], axis=1)
        gate = jax.nn.sigmoid(gt_ref[...])
        dm = dm_ref[...]
        dp_ref[...] = (dm * y * gate * (1.0 - gate)).astype(BF16)
        dyv = (dm * gate).astype(BF16)
        dy_ref[...] = dyv
        part = _dot(dyv[:, :BRANCH_COLS], wa, NT) + _dot(dyv[:, BRANCH_COLS:], wbv, NT)

        @pl.when(u == 0)
        def _():
            dz_ref[...] = part

        @pl.when(u > 0)
        def _():
            dz_ref[...] += part

    unit = lambda i, n, u: (i, GATE_UNIT0 + UNITS * n + u)
    return pl.pallas_call(
        body, name="merge_bwd", grid=(T // tm, N_BRANCH, UNITS),
        in_specs=[pl.BlockSpec((None, tm, WIDTH), lambda i, n, u: (n, i, 0)), *_unit_specs(order),
                  pl.BlockSpec((tm, WIDTH), unit), pl.BlockSpec((tm, WIDTH), lambda i, n, u: (i, u))],
        out_specs=[pl.BlockSpec((tm, WIDTH), unit), pl.BlockSpec((None, tm, WIDTH), lambda i, n, u: (n, i, u)),
                   pl.BlockSpec((None, tm, WIDTH), lambda i, n, u: (n, i, 0))],
        out_shape=[jax.ShapeDtypeStruct((T, IN_COLS), BF16), jax.ShapeDtypeStruct((N_BRANCH, T, D_MODEL), BF16),
                   jax.ShapeDtypeStruct((N_BRANCH, T, WIDTH), F32)],
        compiler_params=_params(3))(z, wb, wb, p, dmerged)


def _branch_wgrad(z, dy):
    T = z.shape[1]
    tt = _token_tile(T)
    nk = T // tt

    def body(z_ref, dy_ref, out_ref, acc):
        kk = pl.program_id(1)
        part = _dot(z_ref[...], dy_ref[...], TN)

        @pl.when(kk == 0)
        def _():
            acc[...] = part

        @pl.when(kk > 0)
        def _():
            acc[...] += part

        @pl.when(kk == nk - 1)
        def _():
            for k in range(N_CHIP):
                out_ref[k] = acc[:, k * BRANCH_COLS:(k + 1) * BRANCH_COLS]

    return pl.pallas_call(
        body, name="branch_wgrad", grid=(N_BRANCH, nk),
        in_specs=[pl.BlockSpec((None, tt, WIDTH), lambda n, kk: (n, kk, 0)),
                  pl.BlockSpec((None, tt, D_MODEL), lambda n, kk: (n, kk, 0))],
        out_specs=pl.BlockSpec((N_CHIP, None, WIDTH, BRANCH_COLS), lambda n, kk: (0, n, 0, 0)),
        out_shape=jax.ShapeDtypeStruct((N_CHIP, N_BRANCH, WIDTH, BRANCH_COLS), F32),
        scratch_shapes=[pltpu.VMEM((WIDTH, D_MODEL), F32)], compiler_params=_params(2))(z, dy)


def _layer_fwd(x, wts, small):
    win, wb, wo, w1, w2 = wts
    h = _rms_fwd("rms_mix", x, small["g_mix"])
    p = _mm_cols("in_proj", h, win, [F32])[0]
    o_hgrn, z, states = _hgrn_fwd(p, small["lbs"], small["g_hgrn_out"])
    z = _conv_fwd(p, small["w_conv"], z)
    z = _sg_fwd(p, small["sg_ln_g"], small["sg_ln_b"], small["w_sg"], small["b_sg"], z)
    merged = _merge_fwd(z, p, wb)
    x_mid = _mm_rows("out_proj", merged, wo, x)
    h2 = _rms_fwd("rms_ffn", x_mid, small["g_ffn"])
    a, s = _mm_cols("ff1", h2, w1, [F32, BF16], epilogue=lambda acc: (acc, jnp.square(jnp.maximum(acc, 0.0))))
    x_out = _mm_rows("ff2", s, w2, x_mid)
    saved = dict(x=x, h=h, p=p, o_hgrn=o_hgrn, z=z, states=states, merged=merged, x_mid=x_mid, h2=h2, a=a, s=s)
    return x_out, saved


def _layer_bwd(dx_out, sv, wts, small, after=()):
    win, wb, wo, w1, w2 = wts
    g = {}
    da = _mm_cols_t("ff2_dgrad", dx_out, w2, BF16, extra=(sv["a"],), after=after,
                    epilogue=lambda acc, a: (acc * 2.0 * jnp.maximum(a, 0.0),))
    g["w_ff2"] = _mm_wgrad("ff2_wgrad", sv["s"], dx_out, w2.shape[1], D_MODEL, True, False)
    g["w_ff1"] = _mm_wgrad("ff1_wgrad", sv["h2"], da, D_MODEL, w1.shape[2], False, True)
    dh2 = _mm_rows_t("ff1_dgrad", da, w1)
    dx_mid, g["g_ffn"] = _rms_bwd("rms_ffn_bwd", sv["x_mid"], small["g_ffn"], dh2, dx_out)
    dmerged = _mm_cols_t("out_proj_dgrad", dx_mid, wo, F32)
    g["w_o"] = _mm_wgrad("out_proj_wgrad", sv["merged"], dx_mid, wo.shape[1], D_MODEL, True, False)
    dp, dy, dz = _merge_bwd(sv["z"], sv["p"], wb, dmerged)
    g["w_branch"] = _branch_wgrad(sv["z"], dy)
    dp, g["lbs"], g["g_hgrn_out"] = _hgrn_bwd(sv["p"], sv["o_hgrn"], dz, sv["states"], small["lbs"],
                                              small["g_hgrn_out"], dp)
    dp, g["w_conv"] = _conv_bwd(sv["p"], small["w_conv"], dz, dp)
    dp, g["w_sg"], g["b_sg"], g["sg_ln_g"], g["sg_ln_b"] = _sg_bwd(
        sv["p"], small["sg_ln_g"], small["sg_ln_b"], small["w_sg"], small["b_sg"], dz, dp)
    g["w_in"] = _mm_wgrad("in_proj_wgrad", sv["h"], dp, D_MODEL, win.shape[2], False, True)
    dh = _mm_rows_t("in_proj_dgrad", dp, win)
    dx, g["g_mix"] = _rms_bwd("rms_mix_bwd", sv["x"], small["g_mix"], dh, dx_mid)
    return dx, g


def _mesh_pos():
    return lax.axis_index("x"), lax.axis_index("y"), lax.axis_index("c")


def _other_chips(x, y):
    return [(1 - x, y), (x, 1 - y), (1 - x, 1 - y)]


def _remote(src, dst, send_sems, recv_sems, k, to):
    return pltpu.make_async_remote_copy(src_ref=src, dst_ref=dst, send_sem=send_sems.at[k], recv_sem=recv_sems.at[k],
                                        device_id=to, device_id_type=MESH)


def _comm_call(name, body, ins, out_shapes, n_remote, in_place=False):
    scratch = [pltpu.SemaphoreType.DMA((n_remote,)), pltpu.SemaphoreType.DMA((n_remote,))]
    aliases = {t: t for t in range(len(ins))} if in_place else {}
    return pl.pallas_call(
        body, name=name, in_specs=[ANY] * len(ins), out_specs=[ANY] * len(out_shapes), out_shape=list(out_shapes),
        scratch_shapes=scratch, input_output_aliases=aliases)(*ins)


HBM = pl.BlockSpec(memory_space=pltpu.HBM)
SEM = pl.BlockSpec(memory_space=pltpu.SEMAPHORE)
DATAFLOW = pltpu.SideEffectType.DATAFLOW_SIDE_EFFECTING


def _split_start(name, bufs, copies, n_copies, after=()):
    n = len(bufs)

    def body(*refs):
        send_sems, recv_sems = refs[n + len(after)], refs[n + len(after) + 1]
        for cp in copies(refs[:n], send_sems, recv_sems):
            cp.start()
        refs[-1][...] = jnp.zeros_like(refs[-1])

    outs = pl.pallas_call(
        body, name=name,
        out_shape=(pltpu.SemaphoreType.DMA((n_copies,)), pltpu.SemaphoreType.DMA((n_copies,)),
                   *[pltpu.HBM(b.shape, b.dtype) for b in bufs], jax.ShapeDtypeStruct((SUBLANES, LANES), F32)),
        in_specs=[HBM] * n + [ANY] * len(after),
        out_specs=(SEM, SEM, *[HBM] * n, pl.BlockSpec(memory_space=pltpu.VMEM)),
        input_output_aliases={t: 2 + t for t in range(n)},
        compiler_params=pltpu.CompilerParams(has_side_effects=DATAFLOW),
    )(*[pltpu.with_memory_space_constraint(b, pltpu.HBM) for b in bufs], *after)
    return outs[0], outs[1], list(outs[2:2 + n]), outs[-1]


def _split_wait(name, started, copies, after):
    send_sems, recv_sems, bufs, _ = started
    n = len(bufs)

    def body(*refs):
        for cp in copies(refs[:n], refs[n], refs[n + 1]):
            cp.wait_send()
            cp.wait_recv()

    return list(pl.pallas_call(
        body, name=name, out_shape=tuple(pltpu.HBM(b.shape, b.dtype) for b in bufs),
        in_specs=[HBM] * n + [SEM, SEM] + [ANY] * len(after), out_specs=tuple([HBM] * n),
        input_output_aliases={t: t for t in range(n)},
        compiler_params=pltpu.CompilerParams(has_side_effects=DATAFLOW),
    )(*bufs, send_sems, recv_sems, *after))


def _weight_ici_copies(refs, send_sems, recv_sems):
    x, y, c = _mesh_pos()
    out = []
    for t, ref in enumerate(refs):
        rh = ref.shape[1] // 2
        mine = ref.at[2 * x + y, pl.ds(c * rh, rh), :]
        out += [_remote(mine, mine, send_sems, recv_sems, 3 * t + j, (*chip, c)) for j, chip in enumerate(_other_chips(x, y))]
    return out


def _weight_d2d_copies(refs, send_sems, recv_sems):
    x, y, c = _mesh_pos()
    out = []
    for t, ref in enumerate(refs):
        rh = ref.shape[1] // 2
        for j, chip in enumerate(_other_chips(x, y)):
            blk = ref.at[2 * chip[0] + chip[1], pl.ds(c * rh, rh), :]
            out.append(_remote(blk, blk, send_sems, recv_sems, 3 * t + j, (x, y, 1 - c)))
    return out


def _grad_exchange_copies(refs, send_sems, recv_sems):
    x, y, c = _mesh_pos()
    n = len(refs) // 2
    out = []
    for t in range(n):
        for j, chip in enumerate(_other_chips(x, y)):
            out.append(_remote(refs[t].at[2 * chip[0] + chip[1]], refs[n + t].at[j], send_sems, recv_sems, 3 * t + j,
                               (*chip, c)))
    return out


def _sibling_swap_halves(name, grads):
    n = len(grads)

    def body(*refs):
        ins, outs = refs[:n], refs[n:2 * n]
        send_sems, recv_sems = refs[2 * n:]
        x, y, c = _mesh_pos()
        copies = []
        for t in range(n):
            rh = grads[t].shape[1] // 2
            cp = _remote(ins[t].at[:, pl.ds((1 - c) * rh, rh), :], outs[t], send_sems, recv_sems, t, (x, y, 1 - c))
            cp.start()
            copies.append(cp)
        for cp in copies:
            cp.wait_send()
            cp.wait_recv()

    out_shapes = [jax.ShapeDtypeStruct((N_CHIP, g.shape[1] // 2, g.shape[2]), g.dtype) for g in grads]
    return _comm_call(name, body, grads, out_shapes, n)


def _sibling_gather(name, bufs):
    n = len(bufs)

    def body(*refs):
        outs = refs[n:2 * n]
        send_sems, recv_sems = refs[2 * n:]
        x, y, c = _mesh_pos()
        copies = []
        for t in range(n):
            cp = _remote(outs[t].at[c], outs[t].at[c], send_sems, recv_sems, t, (x, y, 1 - c))
            cp.start()
            copies.append(cp)
        for cp in copies:
            cp.wait_send()
            cp.wait_recv()

    out_shapes = [jax.ShapeDtypeStruct(b.shape, b.dtype) for b in bufs]
    return _comm_call(name, body, bufs, out_shapes, n, in_place=True)


def _gather_all(name, block, slot):
    buf = lax.dynamic_update_slice(jnp.zeros((8, *block.shape), block.dtype), block[None], (slot, 0, 0))

    def body(in_ref, out_ref, send_sems, recv_sems):
        del in_ref
        x, y, c = _mesh_pos()
        chips = _other_chips(x, y)
        sibling = (x, y, 1 - c)
        slot_of = lambda px, py, pc: out_ref.at[4 * px + 2 * py + pc]
        started = [_remote(slot_of(x, y, c), slot_of(x, y, c), send_sems, recv_sems, 0, sibling)]
        started += [_remote(slot_of(x, y, c), slot_of(x, y, c), send_sems, recv_sems, 1 + j, (*chip, c))
                    for j, chip in enumerate(chips)]
        for cp in started:
            cp.start()
        for j, chip in enumerate(chips):
            _remote(slot_of(*chip, c), slot_of(*chip, c), send_sems, recv_sems, 1 + j, (*chip, c)).wait_recv()
            fw = _remote(slot_of(*chip, c), slot_of(*chip, c), send_sems, recv_sems, 4 + j, sibling)
            fw.start()
            started.append(fw)
        _remote(slot_of(x, y, 1 - c), slot_of(x, y, 1 - c), send_sems, recv_sems, 0, sibling).wait_recv()
        for j, chip in enumerate(chips):
            _remote(slot_of(*chip, 1 - c), slot_of(*chip, 1 - c), send_sems, recv_sems, 4 + j, sibling).wait_recv()
        for cp in started:
            cp.wait_send()

    return _comm_call(name, body, [buf], [jax.ShapeDtypeStruct(buf.shape, buf.dtype)], 7, in_place=True)[0]


def _row_tile(rows, cols):
    cap = max(SUBLANES, ELEMWISE_BLOCK_BYTES // (4 * cols))
    tr = rows
    while tr > cap and tr % 2 == 0:
        tr //= 2
    return tr


def _pair_sum(name, grad, recv, core):
    _, rh, cols = recv.shape
    tr = _row_tile(rh, cols)
    per = rh // tr

    def body(core_ref, g_ref, r_ref, out32_ref, out16_ref):
        del core_ref
        s = g_ref[...] + r_ref[...]
        out32_ref[...] = s
        out16_ref[...] = s.astype(BF16)

    blk = pl.BlockSpec((None, tr, cols), lambda k, i, core_ref: (k, i, 0))
    return pl.pallas_call(
        body, name=name,
        grid_spec=pltpu.PrefetchScalarGridSpec(
            num_scalar_prefetch=1, grid=(N_CHIP, per),
            in_specs=[pl.BlockSpec((None, tr, cols), lambda k, i, core_ref: (k, core_ref[0] * per + i, 0)), blk],
            out_specs=[blk, blk]),
        out_shape=[jax.ShapeDtypeStruct(recv.shape, F32), jax.ShapeDtypeStruct(recv.shape, BF16)],
        compiler_params=_params(2))(core, grad, recv)


def _chip_sum(name, part32, recv, pos):
    _, rh, cols = part32.shape
    tr = _row_tile(rh, cols)

    def body(pos_ref, own_ref, r_ref, out_ref):
        del pos_ref
        out_ref[...] = ((own_ref[...] + r_ref[0].astype(F32)) + r_ref[1].astype(F32)) + r_ref[2].astype(F32)

    return pl.pallas_call(
        body, name=name,
        grid_spec=pltpu.PrefetchScalarGridSpec(
            num_scalar_prefetch=1, grid=(rh // tr,),
            in_specs=[pl.BlockSpec((None, tr, cols), lambda i, pos_ref: (pos_ref[0], i, 0)),
                      pl.BlockSpec((3, tr, cols), lambda i, pos_ref: (0, i, 0))],
            out_specs=pl.BlockSpec((None, tr, cols), lambda i, pos_ref: (pos_ref[1], i, 0))),
        out_shape=jax.ShapeDtypeStruct((2, rh, cols), F32), compiler_params=_params(1))(pos, part32, recv)


def _cast_into_slot(name, w, layer, pos):
    _, rows, cols = w.shape
    tr = _row_tile(rows, cols)

    def body(pos_ref, w_ref, out_ref):
        del pos_ref
        out_ref[...] = w_ref[...].astype(BF16)

    return pl.pallas_call(
        body, name=name,
        grid_spec=pltpu.PrefetchScalarGridSpec(
            num_scalar_prefetch=1, grid=(rows // tr,),
            in_specs=[pl.BlockSpec((None, tr, cols), lambda i, pos_ref: (layer, i, 0))],
            out_specs=pl.BlockSpec((None, tr, cols), lambda i, pos_ref: (pos_ref[0], i, 0))),
        out_shape=jax.ShapeDtypeStruct((N_CHIP, rows, cols), BF16), compiler_params=_params(1))(pos, w)


def _adamw_math(w, g, m, v):
    m = ADAM_B1 * m + (1.0 - ADAM_B1) * g
    v = ADAM_B2 * v + (1.0 - ADAM_B2) * jnp.square(g)
    m_hat = m / (1.0 - ADAM_B1 ** ADAM_STEP)
    v_hat = v / (1.0 - ADAM_B2 ** ADAM_STEP)
    delta = -ADAM_LR * (m_hat / (jnp.sqrt(v_hat) + ADAM_EPS) + ADAM_WD * w)
    return delta, m, v


def _adamw_layers(name, w, m, v, grads):
    _, rows, cols = w.shape
    tr = _row_tile(rows, cols)

    def body(w_ref, m_ref, v_ref, *rest):
        g_refs, (grad_ref, d_ref, nm_ref, nv_ref) = rest[:DEPTH], rest[DEPTH:]
        layer = pl.program_id(0)
        g = g_refs[0][...]
        for l in range(1, DEPTH):
            g = jnp.where(layer == l, g_refs[l][...], g)
        grad_ref[...] = g
        d_ref[...], nm_ref[...], nv_ref[...] = _adamw_math(w_ref[...], g, m_ref[...], v_ref[...])

    blk = pl.BlockSpec((None, tr, cols), lambda l, i: (l, i, 0))
    g_spec = lambda k: pl.BlockSpec((tr, cols), lambda l, i: (jnp.where(l == k, i, 0), 0))
    return pl.pallas_call(
        body, name=name, grid=(DEPTH, rows // tr),
        in_specs=[blk, blk, blk] + [g_spec(k) for k in range(DEPTH)], out_specs=[blk] * 4,
        out_shape=[jax.ShapeDtypeStruct(w.shape, F32)] * 4, compiler_params=_params(2))(w, m, v, *grads)


def _sum_devices(gathered):
    _, rows, cols = gathered.shape

    def body(g_ref, out_ref):
        s = g_ref[0]
        for d in range(1, 8):
            s = s + g_ref[d]
        out_ref[...] = s

    return pl.pallas_call(body, name="sum_devices", out_shape=jax.ShapeDtypeStruct((rows, cols), F32),
                          compiler_params=pltpu.CompilerParams(vmem_limit_bytes=VMEM_LIMIT_BYTES))(gathered)


def _adamw_flat(w, g, m, v):
    def body(w_ref, g_ref, m_ref, v_ref, d_ref, nm_ref, nv_ref):
        d_ref[...], nm_ref[...], nv_ref[...] = _adamw_math(w_ref[...], g_ref[...], m_ref[...], v_ref[...])

    return pl.pallas_call(body, name="adamw_small", out_shape=[jax.ShapeDtypeStruct(w.shape, F32)] * 3,
                          compiler_params=pltpu.CompilerParams(vmem_limit_bytes=VMEM_LIMIT_BYTES))(w, g, m, v)


SMALL_NAMES = ["g_mix", "lower_bounds", "g_hgrn_out", "w_conv", "sg_ln_g", "sg_ln_b", "w_sg", "b_sg", "g_ffn", "g_final"]
BIG_NAMES = ["w_in", "w_branch", "w_o", "w_ff1", "w_ff2"]
WEIGHT_ORDER = ["w_in", "g_mix", "lower_bounds", "g_hgrn_out", "w_conv", "sg_ln_g", "sg_ln_b", "w_sg", "b_sg", "w_branch",
                "w_o", "g_ffn", "w_ff1", "w_ff2", "g_final"]


def _padded_rows(n):
    return -(-n // SUBLANES) * SUBLANES


def _pack(arrays):
    parts = []
    for a in arrays:
        a = a.reshape(-1, LANES)
        parts.append(jnp.pad(a, ((0, _padded_rows(a.shape[0]) - a.shape[0]), (0, 0))))
    return jnp.concatenate(parts, axis=0)


def _unpack(flat, shapes):
    out, row = [], 0
    for s in shapes:
        n = int(np.prod(s)) // LANES
        out.append(flat[row:row + n].reshape(s))
        row += _padded_rows(n)
    return out


def _as_2d(name, a):
    return a.reshape(DEPTH, N_BRANCH * WIDTH, BRANCH_COLS) if name == "w_branch" else a


def kernel(x, w_in, g_mix, lower_bounds, g_hgrn_out, w_conv, sg_ln_g, sg_ln_b, w_sg, b_sg, w_branch, w_o, g_ffn, w_ff1, w_ff2, g_final, loss_target, m_w_in, m_g_mix, m_lower_bounds, m_g_hgrn_out, m_w_conv, m_sg_ln_g, m_sg_ln_b, m_w_sg, m_b_sg, m_w_branch, m_w_o, m_g_ffn, m_w_ff1, m_w_ff2, m_g_final, v_w_in, v_g_mix, v_lower_bounds, v_g_hgrn_out, v_w_conv, v_sg_ln_g, v_sg_ln_b, v_w_sg, v_b_sg, v_w_branch, v_w_o, v_g_ffn, v_w_ff1, v_w_ff2, v_g_final):
    weights = dict(w_in=w_in, g_mix=g_mix, lower_bounds=lower_bounds, g_hgrn_out=g_hgrn_out, w_conv=w_conv,
                   sg_ln_g=sg_ln_g, sg_ln_b=sg_ln_b, w_sg=w_sg, b_sg=b_sg, w_branch=w_branch, w_o=w_o, g_ffn=g_ffn,
                   w_ff1=w_ff1, w_ff2=w_ff2, g_final=g_final)
    mom1 = dict(w_in=m_w_in, g_mix=m_g_mix, lower_bounds=m_lower_bounds, g_hgrn_out=m_g_hgrn_out, w_conv=m_w_conv,
                sg_ln_g=m_sg_ln_g, sg_ln_b=m_sg_ln_b, w_sg=m_w_sg, b_sg=m_b_sg, w_branch=m_w_branch, w_o=m_w_o,
                g_ffn=m_g_ffn, w_ff1=m_w_ff1, w_ff2=m_w_ff2, g_final=m_g_final)
    mom2 = dict(w_in=v_w_in, g_mix=v_g_mix, lower_bounds=v_lower_bounds, g_hgrn_out=v_g_hgrn_out, w_conv=v_w_conv,
                sg_ln_g=v_sg_ln_g, sg_ln_b=v_sg_ln_b, w_sg=v_w_sg, b_sg=v_b_sg, w_branch=v_w_branch, w_o=v_w_o,
                g_ffn=v_g_ffn, w_ff1=v_w_ff1, w_ff2=v_w_ff2, g_final=v_g_final)
    xi, yi, ci = _mesh_pos()
    core = jnp.reshape(ci, (1,)).astype(jnp.int32)
    pos = jnp.stack([2 * xi + yi, ci]).astype(jnp.int32)
    device = 4 * xi + 2 * yi + ci
    conv_cols = w_conv.shape[2]

    conv_all = _gather_all("gather_w_conv", w_conv.reshape(DEPTH * CONV_K, conv_cols), device)
    conv_full = conv_all.reshape(N_CHIP, 2, DEPTH, CONV_K, conv_cols)[:, 0].transpose(1, 2, 0, 3).reshape(DEPTH, CONV_K, WIDTH)
    lbs = _lbs_fwd(lower_bounds)

    act = x[0]
    layers = []
    n_big = len(BIG_NAMES)
    cast = lambda l: [_cast_into_slot("cast_" + n, _as_2d(n, weights[n]), l, pos) for n in BIG_NAMES]
    ici = _split_start("weights_ici_start_0", cast(0), _weight_ici_copies, 3 * n_big)
    for l in range(DEPTH):
        landed = _split_wait("weights_ici_wait_%d" % l, ici, _weight_ici_copies, after=[act])
        token = []
        if l + 1 < DEPTH:
            ici = _split_start("weights_ici_start_%d" % (l + 1), cast(l + 1), _weight_ici_copies, 3 * n_big, after=[landed[0]])
            token = [ici[3]]
        d2d = _split_start("weights_d2d_start_%d" % l, landed, _weight_d2d_copies, 3 * n_big, after=token)
        gathered = _split_wait("weights_d2d_wait_%d" % l, d2d, _weight_d2d_copies, after=[])
        wts = [gathered[0], gathered[1].reshape(N_CHIP, N_BRANCH, WIDTH, BRANCH_COLS), *gathered[2:]]
        small = dict(g_mix=g_mix[l:l + 1], lbs=lbs[l:l + 1], g_hgrn_out=g_hgrn_out[l:l + 1], w_conv=conv_full[l],
                     sg_ln_g=sg_ln_g[l:l + 1], sg_ln_b=sg_ln_b[l:l + 1], w_sg=w_sg[l],
                     b_sg=b_sg[l].reshape(SG_GROUPS, SG_CHUNK, 1), g_ffn=g_ffn[l:l + 1])
        act, saved = _layer_fwd(act, wts, small)
        layers.append((wts, small, saved))
    loss_blk, dact, dg_final = _loss_head(act, g_final.reshape(1, D_MODEL), loss_target[0])

    reduced = {n: [None] * DEPTH for n in BIG_NAMES}
    small_grads = [None] * DEPTH

    def finish_exchange(l, own32, started, after):
        bufs = _split_wait("grad_exchange_wait_%d" % l, started, _grad_exchange_copies, after=after)
        halves = [_chip_sum("grad_chip_sum_" + n, p32, r, pos) for n, p32, r in zip(BIG_NAMES, own32, bufs[n_big:])]
        for n, b in zip(BIG_NAMES, _sibling_gather("grad_half_gather", halves)):
            reduced[n][l] = b.reshape(-1, b.shape[-1])

    pending, token = None, ()
    for l in reversed(range(DEPTH)):
        wts, small, saved = layers[l]
        dact, g = _layer_bwd(dact, saved, wts, small, after=token)
        small_grads[l] = g
        if pending is not None:
            finish_exchange(l + 1, *pending, after=[dact])
        full = [g[n].reshape(N_CHIP, -1, g[n].shape[-1]) for n in BIG_NAMES]
        from_sibling = _sibling_swap_halves("grad_pair_swap", full)
        pair = [_pair_sum("grad_pair_sum_" + n, f, r, core) for n, f, r in zip(BIG_NAMES, full, from_sibling)]
        landing = [lax.empty((3, *p16.shape[1:]), BF16) for _, p16 in pair]
        started = _split_start("grad_exchange_start_%d" % l, [p16 for _, p16 in pair] + landing, _grad_exchange_copies,
                               3 * n_big)
        pending, token = ([p32 for p32, _ in pair], started), (started[3],)
    finish_exchange(0, *pending, after=[dact])
    grad_x = dact[None]

    stack = lambda key, rows=None: jnp.stack([small_grads[l][key][0] if rows is None else small_grads[l][key][:rows]
                                              for l in range(DEPTH)])
    local_small = dict(
        g_mix=stack("g_mix"), lower_bounds=stack("lbs"), g_hgrn_out=stack("g_hgrn_out"), w_conv=stack("w_conv", CONV_K),
        sg_ln_g=stack("sg_ln_g"), sg_ln_b=stack("sg_ln_b"), w_sg=jnp.stack([small_grads[l]["w_sg"] for l in range(DEPTH)]),
        b_sg=jnp.stack([small_grads[l]["b_sg"].reshape(SG_GROUPS, SG_CHUNK) for l in range(DEPTH)]),
        g_ffn=stack("g_ffn"), g_final=dg_final[0])
    shapes = [local_small[n].shape for n in SMALL_NAMES] + [(SUBLANES, LANES)]
    summed = _sum_devices(_gather_all("gather_small_grads", _pack([local_small[n] for n in SMALL_NAMES] + [loss_blk]), device))
    parts = _unpack(summed, shapes)
    loss = parts[-1][0, 0]
    small_grad = dict(zip(SMALL_NAMES, parts[:-1]))
    small_grad["lower_bounds"] = _lbs_bwd(lower_bounds, small_grad["lower_bounds"])
    small_grad["w_conv"] = lax.dynamic_slice_in_dim(small_grad["w_conv"], pos[0] * conv_cols, conv_cols, axis=2)
    g_flat = _pack([small_grad[n] for n in SMALL_NAMES])
    d_flat, m_flat, v_flat = _adamw_flat(_pack([weights[n] for n in SMALL_NAMES]), g_flat,
                                         _pack([mom1[n] for n in SMALL_NAMES]), _pack([mom2[n] for n in SMALL_NAMES]))
    small_shapes = [weights[n].shape for n in SMALL_NAMES]
    grads = dict(small_grad)
    delta = dict(zip(SMALL_NAMES, _unpack(d_flat, small_shapes)))
    new_m = dict(zip(SMALL_NAMES, _unpack(m_flat, small_shapes)))
    new_v = dict(zip(SMALL_NAMES, _unpack(v_flat, small_shapes)))

    for n in BIG_NAMES:
        outs = _adamw_layers("adamw_" + n, _as_2d(n, weights[n]), _as_2d(n, mom1[n]), _as_2d(n, mom2[n]), reduced[n])
        grads[n], delta[n], new_m[n], new_v[n] = [o.reshape(weights[n].shape) for o in outs]

    return (loss, grad_x, *[grads[n] for n in WEIGHT_ORDER], *[delta[n] for n in WEIGHT_ORDER],
            *[new_m[n] for n in WEIGHT_ORDER], *[new_v[n] for n in WEIGHT_ORDER])
```

```python
import numpy as np
import jax
import jax.numpy as jnp
from jax import lax
from jax.experimental import pallas as pl
from jax.experimental.pallas import tpu as pltpu

F32, BF16 = jnp.float32, jnp.bfloat16

D_MODEL = 1024
WIDTH = 512
N_BRANCH = 3
N_HEAD = 4
HEAD = 128
H_CHUNK = 64
CONV_K = 3
SG_CHUNK = 128
SG_GROUPS = 4
D_FF = 4096
DEPTH = 4
N_CHIP = 4
IN_COLS = 9 * WIDTH + N_BRANCH * D_MODEL
GATE_COL0 = 9 * WIDTH
LB_FLOOR = 1e-30
NORM_EPS = 1e-6
LN_EPS = 1e-5
ADAM_LR, ADAM_B1, ADAM_B2, ADAM_EPS, ADAM_WD, ADAM_STEP = 0.001, 0.9, 0.999, 1e-08, 0.01, 10

VMEM_LIMIT_BYTES = 48 * 1024 * 1024
SUBLANES, LANES = 8, 128
ELEMWISE_BLOCK_BYTES = 2 * 1024 * 1024

NN = (((1,), (0,)), ((), ()))
NT = (((1,), (1,)), ((), ()))
TN = (((0,), (0,)), ((), ()))
MESH = pl.DeviceIdType.MESH
ANY = pl.BlockSpec(memory_space=pl.ANY)


def _dot(a, b, dims=NN):
    return lax.dot_general(a.astype(BF16), b.astype(BF16), dims, preferred_element_type=F32)


def _params(n_axes):
    return pltpu.CompilerParams(dimension_semantics=("arbitrary",) * n_axes, vmem_limit_bytes=VMEM_LIMIT_BYTES)


def _row0(part, rows=SUBLANES):
    r = lax.broadcasted_iota(jnp.int32, (rows, part.shape[1]), 0)
    return jnp.where(r == 0, part, 0.0)


def _token_tile(T):
    return min(512, T)


def _matmul(name, a, b, *, dims, grid, a_spec, b_spec, out_specs, out_shapes, acc_shape,
            extra=(), extra_specs=(), epilogue=None, after=()):
    nk = grid[2]
    n_extra, n_out, n_in = len(extra), len(out_shapes), 2 + len(extra) + len(after)

    def body(*refs):
        a_ref, b_ref = refs[0], refs[1]
        ex = refs[2:2 + n_extra]
        outs = refs[n_in:n_in + n_out]
        acc = refs[-1]
        kk = pl.program_id(2)
        part = _dot(a_ref[...], b_ref[...], dims)

        @pl.when(kk == 0)
        def _():
            acc[...] = part

        @pl.when(kk > 0)
        def _():
            acc[...] += part

        @pl.when(kk == nk - 1)
        def _():
            res = epilogue(acc[...], *[e[...] for e in ex]) if epilogue else (acc[...],)
            for o, r in zip(outs, res):
                o[...] = r.astype(o.dtype)

    return pl.pallas_call(
        body, name=name, grid=grid,
        in_specs=[a_spec, b_spec, *extra_specs, *[ANY] * len(after)], out_specs=list(out_specs),
        out_shape=list(out_shapes), scratch_shapes=[pltpu.VMEM(acc_shape, F32)], compiler_params=_params(3),
    )(a, b, *extra, *after)


def _mm_cols(name, a, w, out_dtypes, epilogue=None, extra=()):
    T, K = a.shape
    N = w.shape[2]
    tm = _token_tile(T)
    blk = pl.BlockSpec((tm, N), lambda j, i, kk: (i, j))
    return _matmul(
        name, a, w, dims=NN, grid=(N_CHIP, T // tm, 1),
        a_spec=pl.BlockSpec((tm, K), lambda j, i, kk: (i, 0)),
        b_spec=pl.BlockSpec((None, K, N), lambda j, i, kk: (j, 0, 0)),
        out_specs=[blk] * len(out_dtypes),
        out_shapes=[jax.ShapeDtypeStruct((T, N_CHIP * N), dt) for dt in out_dtypes],
        acc_shape=(tm, N), extra=extra, extra_specs=[blk] * len(extra), epilogue=epilogue)


def _mm_rows(name, a, w, res, after=()):
    T = a.shape[0]
    K, N = N_CHIP * w.shape[1], w.shape[2]
    tm = _token_tile(T)
    blk = pl.BlockSpec((tm, N), lambda i, j, kk: (i, 0))
    return _matmul(
        name, a, w.reshape(K, N), dims=NN, grid=(T // tm, 1, 1),
        a_spec=pl.BlockSpec((tm, K), lambda i, j, kk: (i, 0)),
        b_spec=pl.BlockSpec((K, N), lambda i, j, kk: (0, 0)),
        out_specs=[blk], out_shapes=[jax.ShapeDtypeStruct((T, N), F32)], acc_shape=(tm, N),
        extra=(res,), extra_specs=[blk], epilogue=lambda acc, r: (acc + r,), after=after)[0]


def _mm_cols_t(name, g, w, out_dtype, epilogue=None, extra=(), after=()):
    T, N = g.shape
    K = w.shape[1]
    tm = _token_tile(T)
    blk = pl.BlockSpec((tm, K), lambda j, i, kk: (i, j))
    return _matmul(
        name, g, w, dims=NT, grid=(N_CHIP, T // tm, 1),
        a_spec=pl.BlockSpec((tm, N), lambda j, i, kk: (i, 0)),
        b_spec=pl.BlockSpec((None, K, N), lambda j, i, kk: (j, 0, 0)),
        out_specs=[blk], out_shapes=[jax.ShapeDtypeStruct((T, N_CHIP * K), out_dtype)], acc_shape=(tm, K),
        extra=extra, extra_specs=[blk] * len(extra), epilogue=epilogue, after=after)[0]


def _mm_rows_t(name, g, w):
    T = g.shape[0]
    K, N = w.shape[1], w.shape[2]
    tm = _token_tile(T)
    blk = pl.BlockSpec((tm, K), lambda i, j, kk: (i, 0))
    return _matmul(
        name, g, w, dims=NT, grid=(T // tm, 1, N_CHIP),
        a_spec=pl.BlockSpec((tm, N), lambda i, j, kk: (i, kk)),
        b_spec=pl.BlockSpec((None, K, N), lambda i, j, kk: (kk, 0, 0)),
        out_specs=[blk], out_shapes=[jax.ShapeDtypeStruct((T, K), F32)], acc_shape=(tm, K))[0]


def _mm_wgrad(name, a, g, a_cols, g_cols, a_blocked, g_blocked, after=()):
    T = a.shape[0]
    tt = _token_tile(T)
    return _matmul(
        name, a, g, dims=TN, grid=(N_CHIP, 1, T // tt),
        a_spec=pl.BlockSpec((tt, a_cols), (lambda j, i, kk: (kk, j)) if a_blocked else (lambda j, i, kk: (kk, 0))),
        b_spec=pl.BlockSpec((tt, g_cols), (lambda j, i, kk: (kk, j)) if g_blocked else (lambda j, i, kk: (kk, 0))),
        out_specs=[pl.BlockSpec((None, a_cols, g_cols), lambda j, i, kk: (j, 0, 0))],
        out_shapes=[jax.ShapeDtypeStruct((N_CHIP, a_cols, g_cols), F32)], acc_shape=(a_cols, g_cols), after=after)[0]


def _rms_fwd(name, x, g):
    T, Dm = x.shape
    tm = min(256, T)

    def body(x_ref, g_ref, h_ref):
        xv = x_ref[...]
        r = lax.rsqrt(jnp.mean(xv * xv, axis=-1, keepdims=True) + NORM_EPS)
        h_ref[...] = (xv * r * g_ref[...]).astype(BF16)

    return pl.pallas_call(
        body, name=name, grid=(T // tm,),
        in_specs=[pl.BlockSpec((tm, Dm), lambda i: (i, 0)), pl.BlockSpec((1, Dm), lambda i: (0, 0))],
        out_specs=pl.BlockSpec((tm, Dm), lambda i: (i, 0)),
        out_shape=jax.ShapeDtypeStruct((T, Dm), BF16), compiler_params=_params(1))(x, g)


def _rms_bwd(name, x, g, dh, dres):
    T, Dm = x.shape
    tm = min(256, T)

    def body(x_ref, g_ref, dh_ref, dres_ref, dx_ref, dg_ref):
        xv = x_ref[...]
        r = lax.rsqrt(jnp.mean(xv * xv, axis=-1, keepdims=True) + NORM_EPS)
        xn = xv * r
        dhv = dh_ref[...]
        dxn = dhv * g_ref[...]
        dx_ref[...] = dres_ref[...] + r * (dxn - xn * jnp.mean(dxn * xn, axis=-1, keepdims=True))

        @pl.when(pl.program_id(0) == 0)
        def _():
            dg_ref[...] = jnp.zeros_like(dg_ref)

        dg_ref[...] += _row0(jnp.sum(dhv * xn, axis=0, keepdims=True))

    tile = pl.BlockSpec((tm, Dm), lambda i: (i, 0))
    return pl.pallas_call(
        body, name=name, grid=(T // tm,),
        in_specs=[tile, pl.BlockSpec((1, Dm), lambda i: (0, 0)), tile, tile],
        out_specs=[tile, pl.BlockSpec((SUBLANES, Dm), lambda i: (0, 0))],
        out_shape=[jax.ShapeDtypeStruct((T, Dm), F32), jax.ShapeDtypeStruct((SUBLANES, Dm), F32)],
        compiler_params=_params(1))(x, g, dh, dres)


def _loss_head(x, g, tgt):
    T, Dm = x.shape
    tm = min(256, T)

    def body(x_ref, g_ref, t_ref, loss_ref, dx_ref, dg_ref):
        xv = x_ref[...]
        gv = g_ref[...]
        r = lax.rsqrt(jnp.mean(xv * xv, axis=-1, keepdims=True) + NORM_EPS)
        xn = xv * r
        err = xn * gv - t_ref[...]
        dy = err * (1.0 / Dm)
        dxn = dy * gv
        dx_ref[...] = r * (dxn - xn * jnp.mean(dxn * xn, axis=-1, keepdims=True))

        @pl.when(pl.program_id(0) == 0)
        def _():
            dg_ref[...] = jnp.zeros_like(dg_ref)
            loss_ref[...] = jnp.zeros_like(loss_ref)

        dg_ref[...] += _row0(jnp.sum(dy * xn, axis=0, keepdims=True))
        part = jnp.sum(jnp.sum(err * err, axis=-1, keepdims=True), axis=0, keepdims=True) * (0.5 / Dm)
        loss_ref[...] += jnp.broadcast_to(part, loss_ref.shape)

    tile = pl.BlockSpec((tm, Dm), lambda i: (i, 0))
    return pl.pallas_call(
        body, name="loss_head", grid=(T // tm,),
        in_specs=[tile, pl.BlockSpec((1, Dm), lambda i: (0, 0)), tile],
        out_specs=[pl.BlockSpec((SUBLANES, LANES), lambda i: (0, 0)), tile,
                   pl.BlockSpec((SUBLANES, Dm), lambda i: (0, 0))],
        out_shape=[jax.ShapeDtypeStruct((SUBLANES, LANES), F32), jax.ShapeDtypeStruct((T, Dm), F32),
                   jax.ShapeDtypeStruct((SUBLANES, Dm), F32)],
        compiler_params=_params(1))(x, g, tgt)


def _softmax_rows(lb_ref):
    rows = [lb_ref[pl.ds(i, 1), :] for i in range(DEPTH)]
    mx = rows[0]
    for r in rows[1:]:
        mx = jnp.maximum(mx, r)
    es = [jnp.exp(r - mx) for r in rows]
    tot = es[0]
    for e in es[1:]:
        tot = tot + e
    return [e / tot for e in es]


def _lbs_fwd(lower_bounds):
    def body(lb_ref, out_ref):
        sm = _softmax_rows(lb_ref)
        run = jnp.zeros_like(sm[0])
        out_ref[pl.ds(0, 1), :] = run
        for i in range(1, DEPTH):
            run = run + sm[i]
            out_ref[pl.ds(i, 1), :] = run

    return pl.pallas_call(body, name="lbs_fwd", out_shape=jax.ShapeDtypeStruct(lower_bounds.shape, F32))(lower_bounds)


def _lbs_bwd(lower_bounds, dlbs):
    def body(lb_ref, d_ref, out_ref):
        sm = _softmax_rows(lb_ref)
        dsm = [jnp.zeros_like(sm[0])]
        for i in range(1, DEPTH):
            acc = d_ref[pl.ds(i, 1), :]
            for l in range(i + 1, DEPTH):
                acc = acc + d_ref[pl.ds(l, 1), :]
            dsm.append(acc)
        inner = dsm[0] * sm[0]
        for i in range(1, DEPTH):
            inner = inner + dsm[i] * sm[i]
        for i in range(DEPTH):
            out_ref[pl.ds(i, 1), :] = sm[i] * (dsm[i] - inner)

    return pl.pallas_call(body, name="lbs_bwd", out_shape=jax.ShapeDtypeStruct(lower_bounds.shape, F32))(lower_bounds, dlbs)


N_LEVEL = 6


def _hgrn_consts():
    L = H_CHUNK
    t = np.arange(L)
    blocks = [(t[:, None] >= t[None, :]).astype(np.float32)]
    masks = []
    m = L // 2
    while m >= 1:
        blk, pos = t // (2 * m), t % (2 * m)
        start = blk * 2 * m
        mat = np.zeros((L, L), np.float32)
        for r in range(L):
            if pos[r] >= m:
                mat[r, start[r] + m:r + 1] = 1.0
            else:
                mat[r, r + 1:start[r] + m] = -1.0
        blocks.append(mat)
        masks.append(((blk[:, None] == blk[None, :]) & (pos[:, None] >= m) & (pos[None, :] < m)).astype(np.float32))
        m //= 2
    blocks.append(np.ones((L, L), np.float32))
    return jnp.asarray(np.concatenate(blocks, 0), BF16), jnp.asarray(np.stack(masks), F32)


def _hgrn_core(qraw, fp, lb, sum_mat, mask_ref):
    L = H_CHUNK
    sq = jax.nn.sigmoid(qraw)
    q = qraw * sq
    sneg = jax.nn.sigmoid(-fp)
    log_sig = jnp.minimum(fp, 0.0) - jnp.log1p(jnp.exp(-jnp.abs(fp)))
    a1 = jnp.log(jnp.maximum(lb, LB_FLOOR))
    a2 = jnp.log1p(-lb) + log_sig
    logf = jnp.maximum(a1, a2) + jnp.log1p(jnp.exp(-jnp.abs(a1 - a2)))
    w1 = jnp.exp(a1 - logf)
    w2 = jnp.exp(a2 - logf)
    k = (1.0 - lb) * sneg
    hi = logf.astype(BF16)
    r1 = logf - hi.astype(F32)
    mid = r1.astype(BF16)
    lo = (r1 - mid.astype(F32)).astype(BF16)
    sums = lax.dot_general(sum_mat, jnp.concatenate([hi, mid, lo], axis=1), NN, preferred_element_type=F32)
    sums = sums[:, 0:HEAD] + sums[:, HEAD:2 * HEAD] + sums[:, 2 * HEAD:3 * HEAD]
    b = sums[0:L]
    b_last = sums[(N_LEVEL + 1) * L:(N_LEVEL + 2) * L]
    eye = lax.broadcasted_iota(jnp.int32, (L, L), 0) == lax.broadcasted_iota(jnp.int32, (L, L), 1)
    attn = jnp.where(eye, jnp.sum(q * k, axis=1, keepdims=True), 0.0)
    fa, fb, ea, eb = [], [], [], []
    for l in range(N_LEVEL):
        d = sums[(l + 1) * L:(l + 2) * L]
        e_a = jnp.exp(jnp.minimum(d, 0.0))
        e_b = jnp.exp(jnp.minimum(-d, 0.0))
        a_l, b_l = q * e_a, k * e_b
        attn = attn + mask_ref[l] * _dot(a_l, b_l, NT)
        fa.append(a_l), fb.append(b_l), ea.append(e_a), eb.append(e_b)
    return dict(sq=sq, q=q, sneg=sneg, logf=logf, w1=w1, w2=w2, k=k, b=b, b_last=b_last, attn=attn,
                fa=fa, fb=fb, ea=ea, eb=eb)


def _hgrn_fwd(p, lbrow, gout):
    T = p.shape[0]
    nch = T // H_CHUNK
    sum_mat, masks = _hgrn_consts()

    def body(p_ref, lb_ref, g_ref, m_ref, mask_ref, o_ref, z_ref, st_ref, state):
        @pl.when(pl.program_id(0) == 0)
        def _():
            state[...] = jnp.zeros_like(state)

        sum_m = m_ref[...]
        for h in range(N_HEAD):
            col = lambda part: pl.ds(part * WIDTH + h * HEAD, HEAD)
            hs = pl.ds(h * HEAD, HEAD)
            v = p_ref[:, col(2)]
            c = _hgrn_core(p_ref[:, col(0)], p_ref[:, col(1)], lb_ref[:, hs], sum_m, mask_ref)
            s0 = state[h]
            st_ref[h] = s0
            o = _dot(c["attn"], v) + _dot(c["q"] * jnp.exp(c["b"]), s0, NT)
            k_dec = c["k"] * jnp.exp(c["b_last"] - c["b"])
            decay = jnp.exp(jnp.max(c["b_last"], axis=0, keepdims=True))
            state[h] = s0 * decay + _dot(v, k_dec, TN)
            o_ref[:, hs] = o
            r = lax.rsqrt(jnp.mean(o * o, axis=-1, keepdims=True) + NORM_EPS)
            z_ref[:, hs] = (o * r * g_ref[:, hs] * jax.nn.sigmoid(p_ref[:, col(3)])).astype(BF16)

    full = lambda shape: pl.BlockSpec(shape, lambda c: (0,) * len(shape))
    return pl.pallas_call(
        body, name="hgrn_fwd", grid=(nch,),
        in_specs=[pl.BlockSpec((H_CHUNK, 4 * WIDTH), lambda c: (c, 0)), full((1, WIDTH)), full((1, WIDTH)),
                  full(sum_mat.shape), full(masks.shape)],
        out_specs=[pl.BlockSpec((H_CHUNK, WIDTH), lambda c: (c, 0)),
                   pl.BlockSpec((None, H_CHUNK, WIDTH), lambda c: (0, c, 0)),
                   pl.BlockSpec((None, N_HEAD, HEAD, HEAD), lambda c: (c, 0, 0, 0))],
        out_shape=[jax.ShapeDtypeStruct((T, WIDTH), F32), jax.ShapeDtypeStruct((N_BRANCH, T, WIDTH), BF16),
                   jax.ShapeDtypeStruct((nch, N_HEAD, HEAD, HEAD), F32)],
        scratch_shapes=[pltpu.VMEM((N_HEAD, HEAD, HEAD), F32)], compiler_params=_params(1),
    )(p, lbrow, gout, sum_mat, masks)


def _hgrn_bwd(p, o_saved, dz, states, lbrow, gout, dp, after=()):
    T = p.shape[0]
    nch = T // H_CHUNK
    L = H_CHUNK
    sum_mat, masks = _hgrn_consts()

    def body(p_ref, o_ref, dz_ref, st_ref, lb_ref, g_ref, m_ref, mask_ref, dp_in, *rest):
        del dp_in
        dp_ref, dlb_ref, dg_ref, dstate = rest[len(after):]

        @pl.when(pl.program_id(0) == 0)
        def _():
            dstate[...] = jnp.zeros_like(dstate)
            dlb_ref[...] = jnp.zeros_like(dlb_ref)
            dg_ref[...] = jnp.zeros_like(dg_ref)

        sum_m = m_ref[...]
        for h in range(N_HEAD):
            col = lambda part: pl.ds(part * WIDTH + h * HEAD, HEAD)
            hs = pl.ds(h * HEAD, HEAD)
            qraw, fp, v, go = p_ref[:, col(0)], p_ref[:, col(1)], p_ref[:, col(2)], p_ref[:, col(3)]
            lb, g = lb_ref[:, hs], g_ref[:, hs]
            c = _hgrn_core(qraw, fp, lb, sum_m, mask_ref)
            q, k, b, b_last = c["q"], c["k"], c["b"], c["b_last"]
            s0, ds1 = st_ref[h], dstate[h]
            e_b = jnp.exp(b)
            q_dec = q * e_b
            e_bl = jnp.exp(b_last - b)
            k_dec = k * e_bl
            decay = jnp.exp(jnp.max(b_last, axis=0, keepdims=True))
            o = o_ref[:, hs]
            r = lax.rsqrt(jnp.mean(o * o, axis=-1, keepdims=True) + NORM_EPS)
            n = o * r
            sgo = jax.nn.sigmoid(go)
            dza = dz_ref[:, hs]
            dgo = dza * n * g * sgo * (1.0 - sgo)
            dg_ref[:, hs] += _row0(jnp.sum(dza * n * sgo, axis=0, keepdims=True))
            dn = dza * g * sgo
            do = r * (dn - n * jnp.mean(dn * n, axis=-1, keepdims=True))
            dattn = _dot(do, v, NT)
            dv = _dot(c["attn"], do, TN) + _dot(k_dec, ds1, NT)
            dq_dec = _dot(do, s0)
            dk_dec = _dot(v, ds1)
            ddiag = jnp.sum(do * v, axis=1, keepdims=True)
            dq = dq_dec * e_b + ddiag * k
            dk = dk_dec * e_bl + ddiag * q
            dsums = [dq_dec * q_dec - dk_dec * k_dec]
            for l in range(N_LEVEL):
                dm = mask_ref[l] * dattn
                da = _dot(dm, c["fb"][l])
                db = _dot(dm, c["fa"][l], TN)
                dq = dq + da * c["ea"][l]
                dk = dk + db * c["eb"][l]
                dsums.append(da * c["fa"][l] - db * c["fb"][l])
            dlast = jnp.sum(ds1 * s0, axis=0, keepdims=True) * decay
            dsums.append(dk_dec * k_dec + _row0(dlast, L))
            dlogf = _dot(sum_m, jnp.concatenate(dsums, axis=0), TN)
            dstate[h] = ds1 * decay + _dot(do, q_dec, TN)
            sq, sneg = c["sq"], c["sneg"]
            dqraw = dq * sq * (1.0 + qraw * (1.0 - sq))
            dfp = dlogf * c["w2"] * sneg - dk * (1.0 - lb) * sneg * (1.0 - sneg)
            inv_lb = jnp.where(lb > LB_FLOOR, 1.0 / jnp.maximum(lb, LB_FLOOR), 0.0)
            dlb_tok = dlogf * (c["w1"] * inv_lb - c["w2"] / (1.0 - lb)) - dk * sneg
            dlb_ref[:, hs] += _row0(jnp.sum(dlb_tok, axis=0, keepdims=True))
            dp_ref[:, col(0)] = dqraw.astype(BF16)
            dp_ref[:, col(1)] = dfp.astype(BF16)
            dp_ref[:, col(2)] = dv.astype(BF16)
            dp_ref[:, col(3)] = dgo.astype(BF16)

    full = lambda shape: pl.BlockSpec(shape, lambda c: (0,) * len(shape))
    rev = lambda c: nch - 1 - c
    return pl.pallas_call(
        body, name="hgrn_bwd", grid=(nch,),
        in_specs=[pl.BlockSpec((L, 4 * WIDTH), lambda c: (rev(c), 0)), pl.BlockSpec((L, WIDTH), lambda c: (rev(c), 0)),
                  pl.BlockSpec((None, L, WIDTH), lambda c: (0, rev(c), 0)),
                  pl.BlockSpec((None, N_HEAD, HEAD, HEAD), lambda c: (rev(c), 0, 0, 0)),
                  full((1, WIDTH)), full((1, WIDTH)), full(sum_mat.shape), full(masks.shape), ANY, *[ANY] * len(after)],
        out_specs=[pl.BlockSpec((L, 4 * WIDTH), lambda c: (rev(c), 0)), full((SUBLANES, WIDTH)), full((SUBLANES, WIDTH))],
        out_shape=[jax.ShapeDtypeStruct(dp.shape, dp.dtype), jax.ShapeDtypeStruct((SUBLANES, WIDTH), F32),
                   jax.ShapeDtypeStruct((SUBLANES, WIDTH), F32)],
        scratch_shapes=[pltpu.VMEM((N_HEAD, HEAD, HEAD), F32)], input_output_aliases={8: 0},
        compiler_params=_params(1),
    )(p, o_saved, dz, states, lbrow, gout, sum_mat, masks, dp, *after)


def _shift_down(tile, halo, s):
    tm = tile.shape[0]
    rows = lax.broadcasted_iota(jnp.int32, tile.shape, 0)
    head = jnp.concatenate([pltpu.roll(halo, s, 0), jnp.zeros((tm - SUBLANES, tile.shape[1]), tile.dtype)], axis=0)
    return jnp.where(rows < s, head, pltpu.roll(tile, s, 0))


def _shift_up(tile, halo, s):
    tm = tile.shape[0]
    rows = lax.broadcasted_iota(jnp.int32, tile.shape, 0)
    tail = jnp.concatenate([jnp.zeros((tm - SUBLANES, tile.shape[1]), tile.dtype), pltpu.roll(halo, SUBLANES - s, 0)], axis=0)
    return jnp.where(rows >= tm - s, tail, pltpu.roll(tile, tm - s, 0))


def _conv_fwd(p, w, z):
    T = p.shape[0]
    tm = _token_tile(T)
    per = tm // SUBLANES

    def body(bg_ref, cg_ref, xc_ref, hcg_ref, hxc_ref, w_ref, z_in, z_ref):
        del z_in
        zc = cg_ref[...] * xc_ref[...]
        hz = jnp.where(pl.program_id(0) > 0, hcg_ref[...] * hxc_ref[...], 0.0)
        y = (w_ref[pl.ds(0, 1), :] * _shift_down(zc, hz, 2) + w_ref[pl.ds(1, 1), :] * _shift_down(zc, hz, 1)
             + w_ref[pl.ds(2, 1), :] * zc)
        z_ref[...] = (bg_ref[...] * y).astype(BF16)

    tile = lambda cb: pl.BlockSpec((tm, WIDTH), lambda i: (i, cb))
    prev = lambda cb: pl.BlockSpec((SUBLANES, WIDTH), lambda i: (jnp.maximum(i * per - 1, 0), cb))
    return pl.pallas_call(
        body, name="conv_fwd", grid=(T // tm,),
        in_specs=[tile(4), tile(5), tile(6), prev(5), prev(6), pl.BlockSpec((CONV_K, WIDTH), lambda i: (0, 0)), ANY],
        out_specs=pl.BlockSpec((None, tm, WIDTH), lambda i: (1, i, 0)),
        out_shape=jax.ShapeDtypeStruct(z.shape, z.dtype), input_output_aliases={6: 0}, compiler_params=_params(1),
    )(p, p, p, p, p, w, z)


def _conv_bwd(p, w, dz, dp):
    T = p.shape[0]
    tm = _token_tile(T)
    per = tm // SUBLANES
    last = T // SUBLANES - 1

    def body(bg_ref, cg_ref, xc_ref, hcg_ref, hxc_ref, nbg_ref, dzb_ref, ndzb_ref, w_ref, dp_in, dp_ref, dw_ref, stash):
        del dp_in
        i, jj = pl.program_id(0), pl.program_id(1)

        @pl.when(jnp.logical_and(i == 0, jj == 0))
        def _():
            dw_ref[...] = jnp.zeros_like(dw_ref)

        @pl.when(jj == 0)
        def _():
            cg, xc, bg = cg_ref[...], xc_ref[...], bg_ref[...]
            w0, w1, w2 = w_ref[pl.ds(0, 1), :], w_ref[pl.ds(1, 1), :], w_ref[pl.ds(2, 1), :]
            zc = cg * xc
            hz = jnp.where(i > 0, hcg_ref[...] * hxc_ref[...], 0.0)
            z2, z1 = _shift_down(zc, hz, 2), _shift_down(zc, hz, 1)
            y = w0 * z2 + w1 * z1 + w2 * zc
            dzb = dzb_ref[...]
            dy = dzb * bg
            hdy = jnp.where(i < pl.num_programs(0) - 1, ndzb_ref[...] * nbg_ref[...], 0.0)
            dzc = w2 * dy + w1 * _shift_up(dy, hdy, 1) + w0 * _shift_up(dy, hdy, 2)
            rows = lax.broadcasted_iota(jnp.int32, (SUBLANES, WIDTH), 0)
            colsum = lambda t: jnp.sum(t, axis=0, keepdims=True)
            dw_ref[...] += (jnp.where(rows == 0, colsum(dy * z2), 0.0) + jnp.where(rows == 1, colsum(dy * z1), 0.0)
                            + jnp.where(rows == 2, colsum(dy * zc), 0.0))
            dp_ref[...] = (dzb * y).astype(BF16)
            stash[0] = dzc * xc
            stash[1] = dzc * cg

        @pl.when(jj > 0)
        def _():
            dp_ref[...] = stash[jj - 1].astype(BF16)

    n_tiles = T // tm
    tile = lambda cb: pl.BlockSpec((tm, WIDTH), lambda i, jj: (i, cb))
    prev = lambda cb: pl.BlockSpec((SUBLANES, WIDTH), lambda i, jj: (jnp.maximum(i * per - 1, 0), cb))
    nxt = lambda i: jnp.minimum((i + 1) * per, last)
    return pl.pallas_call(
        body, name="conv_bwd", grid=(n_tiles, 3),
        in_specs=[tile(4), tile(5), tile(6), prev(5), prev(6),
                  pl.BlockSpec((SUBLANES, WIDTH), lambda i, jj: (nxt(i), 4)),
                  pl.BlockSpec((None, tm, WIDTH), lambda i, jj: (1, i, 0)),
                  pl.BlockSpec((None, SUBLANES, WIDTH), lambda i, jj: (1, nxt(i), 0)),
                  pl.BlockSpec((CONV_K, WIDTH), lambda i, jj: (0, 0)), ANY],
        out_specs=[pl.BlockSpec((tm, WIDTH), lambda i, jj: (i, 4 + jj)),
                   pl.BlockSpec((SUBLANES, WIDTH), lambda i, jj: (0, 0))],
        out_shape=[jax.ShapeDtypeStruct(dp.shape, dp.dtype), jax.ShapeDtypeStruct((SUBLANES, WIDTH), F32)],
        scratch_shapes=[pltpu.VMEM((2, tm, WIDTH), F32)], input_output_aliases={9: 0}, compiler_params=_params(2),
    )(p, p, p, p, p, p, dz, dz, w, dp)


GELU_C = float(np.sqrt(2.0 / np.pi))
GELU_A = 0.044715


def _gelu(x):
    th = jnp.tanh(GELU_C * (x + GELU_A * x * x * x))
    return 0.5 * x * (1.0 + th), th


def _gelu_grad(x, th):
    return 0.5 * (1.0 + th) + 0.5 * x * (1.0 - th * th) * GELU_C * (1.0 + 3.0 * GELU_A * x * x)


def _sg_core(u, v, lng, lnb, ws_ref, bs_ref):
    gu, thu = _gelu(u)
    gv, thv = _gelu(v)
    xc = gv - jnp.mean(gv, axis=-1, keepdims=True)
    rs = lax.rsqrt(jnp.mean(xc * xc, axis=-1, keepdims=True) + LN_EPS)
    xh = xc * rs
    vp = xh * lng + lnb
    tril = (lax.broadcasted_iota(jnp.int32, (SG_CHUNK, SG_CHUNK), 0)
            >= lax.broadcasted_iota(jnp.int32, (SG_CHUNK, SG_CHUNK), 1))
    wm = [jnp.where(tril, ws_ref[g], 0.0).astype(BF16) for g in range(SG_GROUPS)]
    gs = lambda t, g: t[:, g * LANES:(g + 1) * LANES]
    sv = jnp.concatenate([_dot(wm[g], gs(vp, g)) + bs_ref[g] for g in range(SG_GROUPS)], axis=1)
    return dict(gu=gu, thu=thu, thv=thv, rs=rs, xh=xh, vp=vp, tril=tril, wm=wm, sv=sv)


def _sg_fwd(p, lng, lnb, ws, bs, z):
    T = p.shape[0]

    def body(u_ref, v_ref, lng_ref, lnb_ref, ws_ref, bs_ref, z_in, z_ref):
        del z_in
        c = _sg_core(u_ref[...], v_ref[...], lng_ref[...], lnb_ref[...], ws_ref, bs_ref)
        z_ref[...] = (c["gu"] * c["sv"]).astype(BF16)

    full = lambda shape: pl.BlockSpec(shape, lambda c: (0,) * len(shape))
    return pl.pallas_call(
        body, name="sg_fwd", grid=(T // SG_CHUNK,),
        in_specs=[pl.BlockSpec((SG_CHUNK, WIDTH), lambda c: (c, 7)), pl.BlockSpec((SG_CHUNK, WIDTH), lambda c: (c, 8)),
                  full((1, WIDTH)), full((1, WIDTH)), full(ws.shape), full(bs.shape), ANY],
        out_specs=pl.BlockSpec((None, SG_CHUNK, WIDTH), lambda c: (2, c, 0)),
        out_shape=jax.ShapeDtypeStruct(z.shape, z.dtype), input_output_aliases={6: 0}, compiler_params=_params(1),
    )(p, p, lng, lnb, ws, bs, z)


def _sg_bwd(p, lng, lnb, ws, bs, dz, dp):
    T = p.shape[0]

    def body(u_ref, v_ref, lng_ref, lnb_ref, ws_ref, bs_ref, dz_ref, dp_in, dp_ref, dws_ref, dbs_ref, dlng_ref, dlnb_ref,
             stash):
        del dp_in
        cidx, jj = pl.program_id(0), pl.program_id(1)

        @pl.when(jnp.logical_and(cidx == 0, jj == 0))
        def _():
            dws_ref[...] = jnp.zeros_like(dws_ref)
            dbs_ref[...] = jnp.zeros_like(dbs_ref)
            dlng_ref[...] = jnp.zeros_like(dlng_ref)
            dlnb_ref[...] = jnp.zeros_like(dlnb_ref)

        @pl.when(jj == 0)
        def _():
            u, v, lng = u_ref[...], v_ref[...], lng_ref[...]
            c = _sg_core(u, v, lng, lnb_ref[...], ws_ref, bs_ref)
            dzc = dz_ref[...]
            gs = lambda t, g: t[:, g * LANES:(g + 1) * LANES]
            dsv = dzc * c["gu"]
            dvp = []
            for g in range(SG_GROUPS):
                dsv_g = gs(dsv, g)
                dws_ref[g] += jnp.where(c["tril"], _dot(dsv_g, gs(c["vp"], g), NT), 0.0)
                dbs_ref[g] += jnp.sum(dsv_g, axis=1, keepdims=True)
                dvp.append(_dot(c["wm"][g], dsv_g, TN))
            dvp = jnp.concatenate(dvp, axis=1)
            xh = c["xh"]
            dlng_ref[...] += _row0(jnp.sum(dvp * xh, axis=0, keepdims=True))
            dlnb_ref[...] += _row0(jnp.sum(dvp, axis=0, keepdims=True))
            dxh = dvp * lng
            dgv = c["rs"] * (dxh - jnp.mean(dxh, axis=-1, keepdims=True) - xh * jnp.mean(dxh * xh, axis=-1, keepdims=True))
            dp_ref[...] = (dzc * c["sv"] * _gelu_grad(u, c["thu"])).astype(BF16)
            stash[...] = dgv * _gelu_grad(v, c["thv"])

        @pl.when(jj == 1)
        def _():
            dp_ref[...] = stash[...].astype(BF16)

    full = lambda shape: pl.BlockSpec(shape, lambda c, jj: (0,) * len(shape))
    return pl.pallas_call(
        body, name="sg_bwd", grid=(T // SG_CHUNK, 2),
        in_specs=[pl.BlockSpec((SG_CHUNK, WIDTH), lambda c, jj: (c, 7)), pl.BlockSpec((SG_CHUNK, WIDTH), lambda c, jj: (c, 8)),
                  full((1, WIDTH)), full((1, WIDTH)), full(ws.shape), full(bs.shape),
                  pl.BlockSpec((None, SG_CHUNK, WIDTH), lambda c, jj: (2, c, 0)), ANY],
        out_specs=[pl.BlockSpec((SG_CHUNK, WIDTH), lambda c, jj: (c, 7 + jj)), full(ws.shape), full(bs.shape),
                   full((SUBLANES, WIDTH)), full((SUBLANES, WIDTH))],
        out_shape=[jax.ShapeDtypeStruct(dp.shape, dp.dtype), jax.ShapeDtypeStruct(ws.shape, F32),
                   jax.ShapeDtypeStruct(bs.shape, F32), jax.ShapeDtypeStruct((SUBLANES, WIDTH), F32),
                   jax.ShapeDtypeStruct((SUBLANES, WIDTH), F32)],
        scratch_shapes=[pltpu.VMEM((SG_CHUNK, WIDTH), F32)], input_output_aliases={7: 0}, compiler_params=_params(2),
    )(p, p, lng, lnb, ws, bs, dz, dp)


BRANCH_COLS = D_MODEL // N_CHIP
GATE_UNIT0 = GATE_COL0 // WIDTH
UNITS = D_MODEL // WIDTH


def _unit_specs(order):
    def spec(which):
        def index(*g):
            _, n, u = order(*g)
            return (2 * u + which, n, 0, 0)
        return pl.BlockSpec((None, None, WIDTH, BRANCH_COLS), index)
    return [spec(0), spec(1)]


def _merge_fwd(z, p, wb):
    T = z.shape[1]
    tm = _token_tile(T)
    order = lambda i, u, n: (i, n, u)

    def body(z_ref, wa_ref, wb_ref, gt_ref, out_ref, acc):
        n = pl.program_id(2)
        zv = z_ref[...]
        y = jnp.concatenate([_dot(zv, wa_ref[...]), _dot(zv, wb_ref[...])], axis=1)
        part = jax.nn.sigmoid(gt_ref[...]) * y

        @pl.when(n == 0)
        def _():
            acc[...] = part

        @pl.when(n > 0)
        def _():
            acc[...] += part

        @pl.when(n == N_BRANCH - 1)
        def _():
            out_ref[...] = acc[...].astype(BF16)

    return pl.pallas_call(
        body, name="merge_fwd", grid=(T // tm, UNITS, N_BRANCH),
        in_specs=[pl.BlockSpec((None, tm, WIDTH), lambda i, u, n: (n, i, 0)), *_unit_specs(order),
                  pl.BlockSpec((tm, WIDTH), lambda i, u, n: (i, GATE_UNIT0 + UNITS * n + u))],
        out_specs=pl.BlockSpec((tm, WIDTH), lambda i, u, n: (i, u)),
        out_shape=jax.ShapeDtypeStruct((T, D_MODEL), BF16),
        scratch_shapes=[pltpu.VMEM((tm, WIDTH), F32)], compiler_params=_params(3))(z, wb, wb, p)


def _merge_bwd(z, p, wb, dmerged):
    T = z.shape[1]
    tm = _token_tile(T)
    order = lambda i, n, u: (i, n, u)

    def body(z_ref, wa_ref, wb_ref, gt_ref, dm_ref, dp_ref, dy_ref, dz_ref):
        u = pl.program_id(2)
        zv, wa, wbv = z_ref[...], wa_ref[...], wb_ref[...]
        y = jnp.concatenate([_dot(zv, wa), _dot(zv, wbv)], axis=1)
        gate = jax.nn.sigmoid(gt_ref[...])
        dm = dm_ref[...]
        dp_ref[...] = (dm * y * gate * (1.0 - gate)).astype(BF16)
        dyv = (dm * gate).astype(BF16)
        dy_ref[...] = dyv
        part = _dot(dyv[:, :BRANCH_COLS], wa, NT) + _dot(dyv[:, BRANCH_COLS:], wbv, NT)

        @pl.when(u == 0)
        def _():
            dz_ref[...] = part

        @pl.when(u > 0)
        def _():
            dz_ref[...] += part

    unit = lambda i, n, u: (i, GATE_UNIT0 + UNITS * n + u)
    return pl.pallas_call(
        body, name="merge_bwd", grid=(T // tm, N_BRANCH, UNITS),
        in_specs=[pl.BlockSpec((None, tm, WIDTH), lambda i, n, u: (n, i, 0)), *_unit_specs(order),
                  pl.BlockSpec((tm, WIDTH), unit), pl.BlockSpec((tm, WIDTH), lambda i, n, u: (i, u))],
        out_specs=[pl.BlockSpec((tm, WIDTH), unit), pl.BlockSpec((None, tm, WIDTH), lambda i, n, u: (n, i, u)),
                   pl.BlockSpec((None, tm, WIDTH), lambda i, n, u: (n, i, 0))],
        out_shape=[jax.ShapeDtypeStruct((T, IN_COLS), BF16), jax.ShapeDtypeStruct((N_BRANCH, T, D_MODEL), BF16),
                   jax.ShapeDtypeStruct((N_BRANCH, T, WIDTH), F32)],
        compiler_params=_params(3))(z, wb, wb, p, dmerged)


def _branch_wgrad(z, dy):
    T = z.shape[1]
    tt = _token_tile(T)
    nk = T // tt

    def body(z_ref, dy_ref, out_ref, acc):
        kk = pl.program_id(1)
        part = _dot(z_ref[...], dy_ref[...], TN)

        @pl.when(kk == 0)
        def _():
            acc[...] = part

        @pl.when(kk > 0)
        def _():
            acc[...] += part

        @pl.when(kk == nk - 1)
        def _():
            for k in range(N_CHIP):
                out_ref[k] = acc[:, k * BRANCH_COLS:(k + 1) * BRANCH_COLS]

    return pl.pallas_call(
        body, name="branch_wgrad", grid=(N_BRANCH, nk),
        in_specs=[pl.BlockSpec((None, tt, WIDTH), lambda n, kk: (n, kk, 0)),
                  pl.BlockSpec((None, tt, D_MODEL), lambda n, kk: (n, kk, 0))],
        out_specs=pl.BlockSpec((N_CHIP, None, WIDTH, BRANCH_COLS), lambda n, kk: (0, n, 0, 0)),
        out_shape=jax.ShapeDtypeStruct((N_CHIP, N_BRANCH, WIDTH, BRANCH_COLS), F32),
        scratch_shapes=[pltpu.VMEM((WIDTH, D_MODEL), F32)], compiler_params=_params(2))(z, dy)


def _layer_fwd(x, wts, small, hook):
    win, wb, wo, w1, w2 = wts
    h = _rms_fwd("rms_mix", x, small["g_mix"])
    p = _mm_cols("in_proj", h, win, [F32])[0]
    o_hgrn, z, states = _hgrn_fwd(p, small["lbs"], small["g_hgrn_out"])
    z = _conv_fwd(p, small["w_conv"], z)
    z = _sg_fwd(p, small["sg_ln_g"], small["sg_ln_b"], small["w_sg"], small["b_sg"], z)
    merged = _merge_fwd(z, p, wb)
    x_mid = _mm_rows("out_proj", merged, wo, x)
    h2 = _rms_fwd("rms_ffn", x_mid, small["g_ffn"])
    a, s = _mm_cols("ff1", h2, w1, [F32, BF16], epilogue=lambda acc: (acc, jnp.square(jnp.maximum(acc, 0.0))))
    x_out = _mm_rows("ff2", s, w2, x_mid, after=hook([s]))
    saved = dict(x=x, h=h, p=p, o_hgrn=o_hgrn, z=z, states=states, merged=merged, x_mid=x_mid, h2=h2, a=a, s=s)
    return x_out, saved


def _layer_bwd(dx_out, sv, wts, small, tick, after):
    win, wb, wo, w1, w2 = wts
    g = {}
    da = _mm_cols_t("ff2_dgrad", dx_out, w2, BF16, extra=(sv["a"],), after=after,
                    epilogue=lambda acc, a: (acc * 2.0 * jnp.maximum(a, 0.0),))
    d_ff2 = _mm_wgrad("ff2_wgrad", sv["s"], dx_out, w2.shape[1], D_MODEL, True, False)
    d_ff1 = _mm_wgrad("ff1_wgrad", sv["h2"], da, D_MODEL, w1.shape[2], False, True)
    dh2 = _mm_rows_t("ff1_dgrad", da, w1)
    dx_mid, g["g_ffn"] = _rms_bwd("rms_ffn_bwd", sv["x_mid"], small["g_ffn"], dh2, dx_out)
    after = tick([dx_mid], [("w_ff1", d_ff1), ("w_ff2", d_ff2)])
    dmerged = _mm_cols_t("out_proj_dgrad", dx_mid, wo, F32, after=after)
    d_o = _mm_wgrad("out_proj_wgrad", sv["merged"], dx_mid, wo.shape[1], D_MODEL, True, False)
    dp, dy, dz = _merge_bwd(sv["z"], sv["p"], wb, dmerged)
    d_branch = _branch_wgrad(sv["z"], dy).reshape(N_CHIP, N_BRANCH * WIDTH, BRANCH_COLS)
    after = tick([dp], [("w_branch", d_branch), ("w_o", d_o)])
    dp, g["lbs"], g["g_hgrn_out"] = _hgrn_bwd(sv["p"], sv["o_hgrn"], dz, sv["states"], small["lbs"],
                                              small["g_hgrn_out"], dp, after=after)
    dp, g["w_conv"] = _conv_bwd(sv["p"], small["w_conv"], dz, dp)
    dp, g["w_sg"], g["b_sg"], g["sg_ln_g"], g["sg_ln_b"] = _sg_bwd(
        sv["p"], small["sg_ln_g"], small["sg_ln_b"], small["w_sg"], small["b_sg"], dz, dp)
    after = tick([dp], [])
    d_in = _mm_wgrad("in_proj_wgrad", sv["h"], dp, D_MODEL, win.shape[2], False, True, after=after)
    dh = _mm_rows_t("in_proj_dgrad", dp, win)
    dx, g["g_mix"] = _rms_bwd("rms_mix_bwd", sv["x"], small["g_mix"], dh, dx_mid)
    return dx, g, tick([dx], [("w_in", d_in)])


def _mesh_pos():
    return lax.axis_index("x"), lax.axis_index("y"), lax.axis_index("c")


def _other_chips(x, y):
    return [(1 - x, y), (x, 1 - y), (1 - x, 1 - y)]


def _remote(src, dst, send_sems, recv_sems, k, to):
    return pltpu.make_async_remote_copy(src_ref=src, dst_ref=dst, send_sem=send_sems.at[k], recv_sem=recv_sems.at[k],
                                        device_id=to, device_id_type=MESH)


def _gather_call(name, body, buf, after):
    scratch = [pltpu.SemaphoreType.DMA((7,)), pltpu.SemaphoreType.DMA((7,))]
    return pl.pallas_call(
        body, name=name, in_specs=[ANY] * (1 + len(after)), out_specs=ANY,
        out_shape=jax.ShapeDtypeStruct(buf.shape, buf.dtype), scratch_shapes=scratch, input_output_aliases={0: 0})(buf, *after)


HBM = pl.BlockSpec(memory_space=pltpu.HBM)
SEM = pl.BlockSpec(memory_space=pltpu.SEMAPHORE)
DATAFLOW = pltpu.SideEffectType.DATAFLOW_SIDE_EFFECTING


def _split_start(name, bufs, copies, n_copies, after=()):
    n = len(bufs)

    def body(*refs):
        send_sems, recv_sems = refs[n + len(after)], refs[n + len(after) + 1]
        for cp in copies(refs[:n], send_sems, recv_sems):
            cp.start()
        refs[-1][...] = jnp.zeros_like(refs[-1])

    outs = pl.pallas_call(
        body, name=name,
        out_shape=(pltpu.SemaphoreType.DMA((n_copies,)), pltpu.SemaphoreType.DMA((n_copies,)),
                   *[pltpu.HBM(b.shape, b.dtype) for b in bufs], jax.ShapeDtypeStruct((SUBLANES, LANES), F32)),
        in_specs=[HBM] * n + [ANY] * len(after),
        out_specs=(SEM, SEM, *[HBM] * n, pl.BlockSpec(memory_space=pltpu.VMEM)),
        input_output_aliases={t: 2 + t for t in range(n)},
        compiler_params=pltpu.CompilerParams(has_side_effects=DATAFLOW),
    )(*[pltpu.with_memory_space_constraint(b, pltpu.HBM) for b in bufs], *after)
    return outs[0], outs[1], list(outs[2:2 + n]), outs[-1]


def _split_wait(name, started, copies, after):
    send_sems, recv_sems, bufs, _ = started
    n = len(bufs)

    def body(*refs):
        for cp in copies(refs[:n], refs[n], refs[n + 1]):
            cp.wait_send()
            cp.wait_recv()

    return list(pl.pallas_call(
        body, name=name, out_shape=tuple(pltpu.HBM(b.shape, b.dtype) for b in bufs),
        in_specs=[HBM] * n + [SEM, SEM] + [ANY] * len(after), out_specs=tuple([HBM] * n),
        input_output_aliases={t: t for t in range(n)},
        compiler_params=pltpu.CompilerParams(has_side_effects=DATAFLOW),
    )(*bufs, send_sems, recv_sems, *after))


def _weight_ici_copies(refs, send_sems, recv_sems):
    x, y, c = _mesh_pos()
    out = []
    for t, ref in enumerate(refs):
        rh = ref.shape[1] // 2
        mine = ref.at[2 * x + y, pl.ds(c * rh, rh), :]
        out += [_remote(mine, mine, send_sems, recv_sems, 3 * t + j, (*chip, c)) for j, chip in enumerate(_other_chips(x, y))]
    return out


def _weight_d2d_copies(refs, send_sems, recv_sems):
    x, y, c = _mesh_pos()
    out = []
    for t, ref in enumerate(refs):
        rh = ref.shape[1] // 2
        for j, chip in enumerate(_other_chips(x, y)):
            blk = ref.at[2 * chip[0] + chip[1], pl.ds(c * rh, rh), :]
            out.append(_remote(blk, blk, send_sems, recv_sems, 3 * t + j, (x, y, 1 - c)))
    return out


def _swap_part(refs, send_sems, recv_sems, s0):
    x, y, c = _mesh_pos()
    n = len(refs) // 2
    out = []
    for t in range(n):
        rh = refs[t].shape[1] // 2
        out.append(_remote(refs[t].at[:, pl.ds((1 - c) * rh, rh), :], refs[n + t], send_sems, recv_sems, s0 + t, (x, y, 1 - c)))
    return out


def _exchange_part(refs, send_sems, recv_sems, s0):
    x, y, c = _mesh_pos()
    n = len(refs) // 2
    out = []
    for t in range(n):
        for j, chip in enumerate(_other_chips(x, y)):
            out.append(_remote(refs[t].at[2 * chip[0] + chip[1]], refs[n + t].at[j], send_sems, recv_sems, s0 + 3 * t + j,
                               (*chip, c)))
    return out


def _gather_part(refs, send_sems, recv_sems, s0):
    x, y, c = _mesh_pos()
    return [_remote(ref.at[c], ref.at[c], send_sems, recv_sems, s0 + t, (x, y, 1 - c)) for t, ref in enumerate(refs)]


class _GradPipeline:
    def __init__(self, core, pos):
        self.core, self.pos = core, pos
        self.groups, self.pending, self.count = [], None, 0
        self.reduced = {n: [None] * DEPTH for n in BIG_NAMES}

    def busy(self):
        return bool(self.groups) or self.pending is not None

    def tick(self, deps, new):
        if self.pending is not None:
            started, copies, owners = self.pending
            bufs = _split_wait("grad_pipe_wait_%d" % self.count, started, copies, after=list(deps))
            for grp, lo, hi in owners:
                grp["bufs"] = bufs[lo:hi]
            self.pending = None
        parts = []
        for grp in list(self.groups):
            n, names = len(grp["names"]), grp["names"]
            if grp["stage"] == "swap":
                pair = [_pair_sum("grad_pair_sum_" + nm, f, r, self.core)
                        for nm, f, r in zip(names, grp["bufs"][:n], grp["bufs"][n:])]
                grp["own32"] = [p32 for p32, _ in pair]
                landing = [lax.empty((3, *p16.shape[1:]), BF16) for _, p16 in pair]
                grp["stage"] = "exchange"
                parts.append((grp, [p16 for _, p16 in pair] + landing, _exchange_part, 3 * n))
            elif grp["stage"] == "exchange":
                halves = [_chip_sum("grad_chip_sum_" + nm, p32, r, self.pos)
                          for nm, p32, r in zip(names, grp["own32"], grp["bufs"][n:])]
                grp["stage"] = "gather"
                parts.append((grp, halves, _gather_part, n))
            else:
                for nm, b in zip(names, grp["bufs"]):
                    self.reduced[nm][grp["layer"]] = b.reshape(-1, b.shape[-1])
                self.groups.remove(grp)
        if new:
            grp = dict(names=[nm for nm, _, _ in new], layer=new[0][1], stage="swap")
            self.groups.append(grp)
            fulls = [g for _, _, g in new]
            landing = [lax.empty((N_CHIP, g.shape[1] // 2, g.shape[2]), F32) for g in fulls]
            parts.append((grp, fulls + landing, _swap_part, len(fulls)))
        if not parts:
            return ()
        bufs, layout, owners, sems = [], [], [], 0
        for grp, part_bufs, fn, n_sems in parts:
            layout.append((len(bufs), len(bufs) + len(part_bufs), fn, sems))
            owners.append((grp, len(bufs), len(bufs) + len(part_bufs)))
            bufs += part_bufs
            sems += n_sems

        def copies(refs, send_sems, recv_sems):
            out = []
            for lo, hi, fn, s0 in layout:
                out += fn(refs[lo:hi], send_sems, recv_sems, s0)
            return out

        started = _split_start("grad_pipe_start_%d" % self.count, bufs, copies, sems)
        self.pending = (started, copies, owners)
        self.count += 1
        return (started[3],)


def _gather_all(name, block, slot, after=()):
    buf = lax.dynamic_update_slice(jnp.zeros((8, *block.shape), block.dtype), block[None], (slot, 0, 0))

    def body(*refs):
        out_ref, send_sems, recv_sems = refs[1 + len(after):]
        x, y, c = _mesh_pos()
        chips = _other_chips(x, y)
        sibling = (x, y, 1 - c)
        slot_of = lambda px, py, pc: out_ref.at[4 * px + 2 * py + pc]
        started = [_remote(slot_of(x, y, c), slot_of(x, y, c), send_sems, recv_sems, 0, sibling)]
        started += [_remote(slot_of(x, y, c), slot_of(x, y, c), send_sems, recv_sems, 1 + j, (*chip, c))
                    for j, chip in enumerate(chips)]
        for cp in started:
            cp.start()
        for j, chip in enumerate(chips):
            _remote(slot_of(*chip, c), slot_of(*chip, c), send_sems, recv_sems, 1 + j, (*chip, c)).wait_recv()
            fw = _remote(slot_of(*chip, c), slot_of(*chip, c), send_sems, recv_sems, 4 + j, sibling)
            fw.start()
            started.append(fw)
        _remote(slot_of(x, y, 1 - c), slot_of(x, y, 1 - c), send_sems, recv_sems, 0, sibling).wait_recv()
        for j, chip in enumerate(chips):
            _remote(slot_of(*chip, 1 - c), slot_of(*chip, 1 - c), send_sems, recv_sems, 4 + j, sibling).wait_recv()
        for cp in started:
            cp.wait_send()

    return _gather_call(name, body, buf, after)


def _row_tile(rows, cols):
    cap = max(SUBLANES, ELEMWISE_BLOCK_BYTES // (4 * cols))
    tr = rows
    while tr > cap and tr % 2 == 0:
        tr //= 2
    return tr


def _pair_sum(name, grad, recv, core):
    _, rh, cols = recv.shape
    tr = _row_tile(rh, cols)
    per = rh // tr

    def body(core_ref, g_ref, r_ref, out32_ref, out16_ref):
        del core_ref
        s = g_ref[...] + r_ref[...]
        out32_ref[...] = s
        out16_ref[...] = s.astype(BF16)

    blk = pl.BlockSpec((None, tr, cols), lambda k, i, core_ref: (k, i, 0))
    return pl.pallas_call(
        body, name=name,
        grid_spec=pltpu.PrefetchScalarGridSpec(
            num_scalar_prefetch=1, grid=(N_CHIP, per),
            in_specs=[pl.BlockSpec((None, tr, cols), lambda k, i, core_ref: (k, core_ref[0] * per + i, 0)), blk],
            out_specs=[blk, blk]),
        out_shape=[jax.ShapeDtypeStruct(recv.shape, F32), jax.ShapeDtypeStruct(recv.shape, BF16)],
        compiler_params=_params(2))(core, grad, recv)


def _chip_sum(name, part32, recv, pos):
    _, rh, cols = part32.shape
    tr = _row_tile(rh, cols)

    def body(pos_ref, own_ref, r_ref, out_ref):
        del pos_ref
        out_ref[...] = ((own_ref[...] + r_ref[0].astype(F32)) + r_ref[1].astype(F32)) + r_ref[2].astype(F32)

    return pl.pallas_call(
        body, name=name,
        grid_spec=pltpu.PrefetchScalarGridSpec(
            num_scalar_prefetch=1, grid=(rh // tr,),
            in_specs=[pl.BlockSpec((None, tr, cols), lambda i, pos_ref: (pos_ref[0], i, 0)),
                      pl.BlockSpec((3, tr, cols), lambda i, pos_ref: (0, i, 0))],
            out_specs=pl.BlockSpec((None, tr, cols), lambda i, pos_ref: (pos_ref[1], i, 0))),
        out_shape=jax.ShapeDtypeStruct((2, rh, cols), F32), compiler_params=_params(1))(pos, part32, recv)


def _cast_into_slot(name, w, layer, pos):
    _, rows, cols = w.shape
    tr = _row_tile(rows, cols)

    def body(pos_ref, w_ref, out_ref):
        del pos_ref
        out_ref[...] = w_ref[...].astype(BF16)

    return pl.pallas_call(
        body, name=name,
        grid_spec=pltpu.PrefetchScalarGridSpec(
            num_scalar_prefetch=1, grid=(rows // tr,),
            in_specs=[pl.BlockSpec((None, tr, cols), lambda i, pos_ref: (layer, i, 0))],
            out_specs=pl.BlockSpec((None, tr, cols), lambda i, pos_ref: (pos_ref[0], i, 0))),
        out_shape=jax.ShapeDtypeStruct((N_CHIP, rows, cols), BF16), compiler_params=_params(1))(pos, w)


def _adamw_math(w, g, m, v):
    m = ADAM_B1 * m + (1.0 - ADAM_B1) * g
    v = ADAM_B2 * v + (1.0 - ADAM_B2) * jnp.square(g)
    m_hat = m / (1.0 - ADAM_B1 ** ADAM_STEP)
    v_hat = v / (1.0 - ADAM_B2 ** ADAM_STEP)
    delta = -ADAM_LR * (m_hat / (jnp.sqrt(v_hat) + ADAM_EPS) + ADAM_WD * w)
    return delta, m, v


def _adamw_layers(name, w, m, v, grads, first, into=None, after=()):
    _, rows, cols = w.shape
    tr = _row_tile(rows, cols)
    n_layers = len(grads)

    def body(w_ref, m_ref, v_ref, *rest):
        g_refs, (grad_ref, d_ref, nm_ref, nv_ref) = rest[:n_layers], rest[len(rest) - 4:]
        layer = pl.program_id(0)
        g = g_refs[0][...]
        for l in range(1, n_layers):
            g = jnp.where(layer == l, g_refs[l][...], g)
        grad_ref[...] = g
        d_ref[...], nm_ref[...], nv_ref[...] = _adamw_math(w_ref[...], g, m_ref[...], v_ref[...])

    blk = pl.BlockSpec((None, tr, cols), lambda l, i: (first + l, i, 0))
    g_spec = lambda k: pl.BlockSpec((tr, cols), lambda l, i: (jnp.where(l == k, i, 0), 0))
    passed = list(into or []) + list(after)
    return pl.pallas_call(
        body, name=name, grid=(n_layers, rows // tr),
        in_specs=[blk, blk, blk] + [g_spec(k) for k in range(n_layers)] + [ANY] * len(passed), out_specs=[blk] * 4,
        out_shape=[jax.ShapeDtypeStruct(w.shape, F32)] * 4,
        input_output_aliases={3 + n_layers + t: t for t in range(4)} if into else {},
        compiler_params=_params(2))(w, m, v, *grads, *passed)


def _sum_devices(gathered):
    _, rows, cols = gathered.shape

    def body(g_ref, out_ref):
        s = g_ref[0]
        for d in range(1, 8):
            s = s + g_ref[d]
        out_ref[...] = s

    return pl.pallas_call(body, name="sum_devices", out_shape=jax.ShapeDtypeStruct((rows, cols), F32),
                          compiler_params=pltpu.CompilerParams(vmem_limit_bytes=VMEM_LIMIT_BYTES))(gathered)


def _adamw_flat(w, g, m, v):
    def body(w_ref, g_ref, m_ref, v_ref, d_ref, nm_ref, nv_ref):
        d_ref[...], nm_ref[...], nv_ref[...] = _adamw_math(w_ref[...], g_ref[...], m_ref[...], v_ref[...])

    return pl.pallas_call(body, name="adamw_small", out_shape=[jax.ShapeDtypeStruct(w.shape, F32)] * 3,
                          compiler_params=pltpu.CompilerParams(vmem_limit_bytes=VMEM_LIMIT_BYTES))(w, g, m, v)


SMALL_NAMES = ["g_mix", "lower_bounds", "g_hgrn_out", "w_conv", "sg_ln_g", "sg_ln_b", "w_sg", "b_sg", "g_ffn", "g_final"]
BIG_NAMES = ["w_in", "w_branch", "w_o", "w_ff1", "w_ff2"]
WEIGHT_ORDER = ["w_in", "g_mix", "lower_bounds", "g_hgrn_out", "w_conv", "sg_ln_g", "sg_ln_b", "w_sg", "b_sg", "w_branch",
                "w_o", "g_ffn", "w_ff1", "w_ff2", "g_final"]


def _padded_rows(n):
    return -(-n // SUBLANES) * SUBLANES


def _pack(arrays):
    parts = []
    for a in arrays:
        a = a.reshape(-1, LANES)
        parts.append(jnp.pad(a, ((0, _padded_rows(a.shape[0]) - a.shape[0]), (0, 0))))
    return jnp.concatenate(parts, axis=0)


def _unpack(flat, shapes):
    out, row = [], 0
    for s in shapes:
        n = int(np.prod(s)) // LANES
        out.append(flat[row:row + n].reshape(s))
        row += _padded_rows(n)
    return out


def _as_2d(name, a):
    return a.reshape(DEPTH, N_BRANCH * WIDTH, BRANCH_COLS) if name == "w_branch" else a


def kernel(x, w_in, g_mix, lower_bounds, g_hgrn_out, w_conv, sg_ln_g, sg_ln_b, w_sg, b_sg, w_branch, w_o, g_ffn, w_ff1, w_ff2, g_final, loss_target, m_w_in, m_g_mix, m_lower_bounds, m_g_hgrn_out, m_w_conv, m_sg_ln_g, m_sg_ln_b, m_w_sg, m_b_sg, m_w_branch, m_w_o, m_g_ffn, m_w_ff1, m_w_ff2, m_g_final, v_w_in, v_g_mix, v_lower_bounds, v_g_hgrn_out, v_w_conv, v_sg_ln_g, v_sg_ln_b, v_w_sg, v_b_sg, v_w_branch, v_w_o, v_g_ffn, v_w_ff1, v_w_ff2, v_g_final):
    weights = dict(w_in=w_in, g_mix=g_mix, lower_bounds=lower_bounds, g_hgrn_out=g_hgrn_out, w_conv=w_conv,
                   sg_ln_g=sg_ln_g, sg_ln_b=sg_ln_b, w_sg=w_sg, b_sg=b_sg, w_branch=w_branch, w_o=w_o, g_ffn=g_ffn,
                   w_ff1=w_ff1, w_ff2=w_ff2, g_final=g_final)
    mom1 = dict(w_in=m_w_in, g_mix=m_g_mix, lower_bounds=m_lower_bounds, g_hgrn_out=m_g_hgrn_out, w_conv=m_w_conv,
                sg_ln_g=m_sg_ln_g, sg_ln_b=m_sg_ln_b, w_sg=m_w_sg, b_sg=m_b_sg, w_branch=m_w_branch, w_o=m_w_o,
                g_ffn=m_g_ffn, w_ff1=m_w_ff1, w_ff2=m_w_ff2, g_final=m_g_final)
    mom2 = dict(w_in=v_w_in, g_mix=v_g_mix, lower_bounds=v_lower_bounds, g_hgrn_out=v_g_hgrn_out, w_conv=v_w_conv,
                sg_ln_g=v_sg_ln_g, sg_ln_b=v_sg_ln_b, w_sg=v_w_sg, b_sg=v_b_sg, w_branch=v_w_branch, w_o=v_w_o,
                g_ffn=v_g_ffn, w_ff1=v_w_ff1, w_ff2=v_w_ff2, g_final=v_g_final)
    xi, yi, ci = _mesh_pos()
    core = jnp.reshape(ci, (1,)).astype(jnp.int32)
    pos = jnp.stack([2 * xi + yi, ci]).astype(jnp.int32)
    device = 4 * xi + 2 * yi + ci
    conv_cols = w_conv.shape[2]

    conv_all = _gather_all("gather_w_conv", w_conv.reshape(DEPTH * CONV_K, conv_cols), device)
    conv_full = conv_all.reshape(N_CHIP, 2, DEPTH, CONV_K, conv_cols)[:, 0].transpose(1, 2, 0, 3).reshape(DEPTH, CONV_K, WIDTH)
    lbs = _lbs_fwd(lower_bounds)

    act = x[0]
    layers = []
    n_big = len(BIG_NAMES)
    cast = lambda l: [_cast_into_slot("cast_" + n, _as_2d(n, weights[n]), l, pos) for n in BIG_NAMES]
    flight = {}

    def fetch(l, deps):
        landed = _split_wait("weights_ici_wait_%d" % l, flight.pop("ici"), _weight_ici_copies, after=deps)
        token = []
        if l + 1 < DEPTH:
            flight["ici"] = _split_start("weights_ici_start_%d" % (l + 1), cast(l + 1), _weight_ici_copies, 3 * n_big,
                                         after=[landed[0]])
            token = [flight["ici"][3]]
        flight["d2d"] = _split_start("weights_d2d_start_%d" % l, landed, _weight_d2d_copies, 3 * n_big, after=token)
        return (flight["d2d"][3],)

    flight["ici"] = _split_start("weights_ici_start_0", cast(0), _weight_ici_copies, 3 * n_big)
    fetch(0, [act])
    for l in range(DEPTH):
        gathered = _split_wait("weights_d2d_wait_%d" % l, flight.pop("d2d"), _weight_d2d_copies, after=[act])
        wts = [gathered[0], gathered[1].reshape(N_CHIP, N_BRANCH, WIDTH, BRANCH_COLS), *gathered[2:]]
        small = dict(g_mix=g_mix[l:l + 1], lbs=lbs[l:l + 1], g_hgrn_out=g_hgrn_out[l:l + 1], w_conv=conv_full[l],
                     sg_ln_g=sg_ln_g[l:l + 1], sg_ln_b=sg_ln_b[l:l + 1], w_sg=w_sg[l],
                     b_sg=b_sg[l].reshape(SG_GROUPS, SG_CHUNK, 1), g_ffn=g_ffn[l:l + 1])
        hook = (lambda deps, l=l: fetch(l + 1, deps)) if l + 1 < DEPTH else (lambda deps: ())
        act, saved = _layer_fwd(act, wts, small, hook)
        layers.append((wts, small, saved))
    loss_blk, dact, dg_final = _loss_head(act, g_final.reshape(1, D_MODEL), loss_target[0])

    pipe = _GradPipeline(core, pos)
    small_grads = [None] * DEPTH
    after = ()
    for l in reversed(range(DEPTH)):
        wts, small, saved = layers[l]
        tick = lambda deps, new, l=l: pipe.tick(deps, [(nm, l, g) for nm, g in new])
        dact, small_grads[l], after = _layer_bwd(dact, saved, wts, small, tick, after)
    grad_x = dact[None]

    stack = lambda key, rows=None: jnp.stack([small_grads[l][key][0] if rows is None else small_grads[l][key][:rows]
                                              for l in range(DEPTH)])
    local_small = dict(
        g_mix=stack("g_mix"), lower_bounds=stack("lbs"), g_hgrn_out=stack("g_hgrn_out"), w_conv=stack("w_conv", CONV_K),
        sg_ln_g=stack("sg_ln_g"), sg_ln_b=stack("sg_ln_b"), w_sg=jnp.stack([small_grads[l]["w_sg"] for l in range(DEPTH)]),
        b_sg=jnp.stack([small_grads[l]["b_sg"].reshape(SG_GROUPS, SG_CHUNK) for l in range(DEPTH)]),
        g_ffn=stack("g_ffn"), g_final=dg_final[0])
    shapes = [local_small[n].shape for n in SMALL_NAMES] + [(SUBLANES, LANES)]
    summed = _sum_devices(_gather_all("gather_small_grads", _pack([local_small[n] for n in SMALL_NAMES] + [loss_blk]), device,
                                       after=after))
    parts = _unpack(summed, shapes)
    loss = parts[-1][0, 0]
    small_grad = dict(zip(SMALL_NAMES, parts[:-1]))
    small_grad["lower_bounds"] = _lbs_bwd(lower_bounds, small_grad["lower_bounds"])
    small_grad["w_conv"] = lax.dynamic_slice_in_dim(small_grad["w_conv"], pos[0] * conv_cols, conv_cols, axis=2)
    g_flat = _pack([small_grad[n] for n in SMALL_NAMES])
    d_flat, m_flat, v_flat = _adamw_flat(_pack([weights[n] for n in SMALL_NAMES]), g_flat,
                                         _pack([mom1[n] for n in SMALL_NAMES]), _pack([mom2[n] for n in SMALL_NAMES]))
    small_shapes = [weights[n].shape for n in SMALL_NAMES]
    grads = dict(small_grad)
    delta = dict(zip(SMALL_NAMES, _unpack(d_flat, small_shapes)))
    new_m = dict(zip(SMALL_NAMES, _unpack(m_flat, small_shapes)))
    new_v = dict(zip(SMALL_NAMES, _unpack(v_flat, small_shapes)))

    def adam(n, first, layer_grads, into=None, after=()):
        return _adamw_layers("adamw_%s_%d" % (n, first), _as_2d(n, weights[n]), _as_2d(n, mom1[n]), _as_2d(n, mom2[n]),
                             layer_grads, first, into, after)

    done = {}
    token = pipe.tick([summed], [])
    for n in ("w_ff1", "w_ff2"):
        done[n] = adam(n, 0, pipe.reduced[n], after=token)
    token = pipe.tick([done["w_ff2"][1]], [])
    for n in ("w_o", "w_branch"):
        done[n] = adam(n, 0, pipe.reduced[n], after=token)
    rest = adam("w_in", 1, pipe.reduced["w_in"][1:], after=token)
    pipe.tick([rest[1]], [])
    assert not pipe.busy()
    done["w_in"] = adam("w_in", 0, pipe.reduced["w_in"][:1], into=rest)
    for n in BIG_NAMES:
        grads[n], delta[n], new_m[n], new_v[n] = [o.reshape(weights[n].shape) for o in done[n]]

    return (loss, grad_x, *[grads[n] for n in WEIGHT_ORDER], *[delta[n] for n in WEIGHT_ORDER],
            *[new_m[n] for n in WEIGHT_ORDER], *[new_v[n] for n in WEIGHT_ORDER])
```

```python
import numpy as np
import jax
import jax.numpy as jnp
from jax import lax
from jax.experimental import pallas as pl
from jax.experimental.pallas import tpu as pltpu

F32, BF16 = jnp.float32, jnp.bfloat16

D_MODEL = 1024
WIDTH = 512
N_BRANCH = 3
N_HEAD = 4
HEAD = 128
H_CHUNK = 64
CONV_K = 3
SG_CHUNK = 128
SG_GROUPS = 4
D_FF = 4096
DEPTH = 4
N_CHIP = 4
IN_COLS = 9 * WIDTH + N_BRANCH * D_MODEL
GATE_COL0 = 9 * WIDTH
LB_FLOOR = 1e-30
NORM_EPS = 1e-6
LN_EPS = 1e-5
ADAM_LR, ADAM_B1, ADAM_B2, ADAM_EPS, ADAM_WD, ADAM_STEP = 0.001, 0.9, 0.999, 1e-08, 0.01, 10

VMEM_LIMIT_BYTES = 56 * 1024 * 1024
VMEM_BLOCK_BUDGET = 44 * 1024 * 1024
SUBLANES, LANES = 8, 128
ELEMWISE_BLOCK_BYTES = 2 * 1024 * 1024

NN = (((1,), (0,)), ((), ()))
NT = (((1,), (1,)), ((), ()))
TN = (((0,), (0,)), ((), ()))
MESH = pl.DeviceIdType.MESH
ANY = pl.BlockSpec(memory_space=pl.ANY)


def _dot(a, b, dims=NN):
    return lax.dot_general(a.astype(BF16), b.astype(BF16), dims, preferred_element_type=F32)


def _params(n_axes):
    return pltpu.CompilerParams(dimension_semantics=("arbitrary",) * n_axes, vmem_limit_bytes=VMEM_LIMIT_BYTES)


def _row0(part, rows=SUBLANES):
    r = lax.broadcasted_iota(jnp.int32, (rows, part.shape[1]), 0)
    return jnp.where(r == 0, part, 0.0)


def _token_tile(T):
    return min(512, T)


def _matmul(name, a, b, *, dims, grid, a_spec, b_spec, out_specs, out_shapes, acc_shape,
            extra=(), extra_specs=(), epilogue=None, after=()):
    nk = grid[2]
    n_extra, n_out, n_in = len(extra), len(out_shapes), 2 + len(extra) + len(after)
    one_step = nk == 1

    def body(*refs):
        a_ref, b_ref = refs[0], refs[1]
        ex = refs[2:2 + n_extra]
        outs = refs[n_in:n_in + n_out]
        part = _dot(a_ref[...], b_ref[...], dims)

        def finish(total):
            res = epilogue(total, *[e[...] for e in ex]) if epilogue else (total,)
            for o, r in zip(outs, res):
                o[...] = r.astype(o.dtype)

        if one_step:
            finish(part)
            return
        acc = refs[-1]
        kk = pl.program_id(2)

        @pl.when(kk == 0)
        def _():
            acc[...] = part

        @pl.when(kk > 0)
        def _():
            acc[...] += part

        @pl.when(kk == nk - 1)
        def _():
            finish(acc[...])

    return pl.pallas_call(
        body, name=name, grid=grid,
        in_specs=[a_spec, b_spec, *extra_specs, *[ANY] * len(after)], out_specs=list(out_specs),
        out_shape=list(out_shapes), scratch_shapes=[] if one_step else [pltpu.VMEM(acc_shape, F32)],
        compiler_params=_params(3),
    )(a, b, *extra, *after)


def _mm_cols(name, a, w, out_dtypes, epilogue=None, extra=()):
    T, K = a.shape
    N = w.shape[2]
    tm = _token_tile(T)
    blk = pl.BlockSpec((tm, N), lambda j, i, kk: (i, j))
    return _matmul(
        name, a, w, dims=NN, grid=(N_CHIP, T // tm, 1),
        a_spec=pl.BlockSpec((tm, K), lambda j, i, kk: (i, 0)),
        b_spec=pl.BlockSpec((None, K, N), lambda j, i, kk: (j, 0, 0)),
        out_specs=[blk] * len(out_dtypes),
        out_shapes=[jax.ShapeDtypeStruct((T, N_CHIP * N), dt) for dt in out_dtypes],
        acc_shape=(tm, N), extra=extra, extra_specs=[blk] * len(extra), epilogue=epilogue)


def _mm_rows(name, a, w, res, after=()):
    T = a.shape[0]
    K, N = N_CHIP * w.shape[1], w.shape[2]
    tm = _token_tile(T)
    blk = pl.BlockSpec((tm, N), lambda i, j, kk: (i, 0))
    return _matmul(
        name, a, w.reshape(K, N), dims=NN, grid=(T // tm, 1, 1),
        a_spec=pl.BlockSpec((tm, K), lambda i, j, kk: (i, 0)),
        b_spec=pl.BlockSpec((K, N), lambda i, j, kk: (0, 0)),
        out_specs=[blk], out_shapes=[jax.ShapeDtypeStruct((T, N), F32)], acc_shape=(tm, N),
        extra=(res,), extra_specs=[blk], epilogue=lambda acc, r: (acc + r,), after=after)[0]


def _mm_cols_t(name, g, w, out_dtype, epilogue=None, extra=(), after=()):
    T, N = g.shape
    K = w.shape[1]
    tm = _token_tile(T)
    blk = pl.BlockSpec((tm, K), lambda j, i, kk: (i, j))
    return _matmul(
        name, g, w, dims=NT, grid=(N_CHIP, T // tm, 1),
        a_spec=pl.BlockSpec((tm, N), lambda j, i, kk: (i, 0)),
        b_spec=pl.BlockSpec((None, K, N), lambda j, i, kk: (j, 0, 0)),
        out_specs=[blk], out_shapes=[jax.ShapeDtypeStruct((T, N_CHIP * K), out_dtype)], acc_shape=(tm, K),
        extra=extra, extra_specs=[blk] * len(extra), epilogue=epilogue, after=after)[0]


def _mm_rows_t(name, g, w):
    T = g.shape[0]
    K, N = w.shape[1], w.shape[2]
    tm = _token_tile(T)
    blk = pl.BlockSpec((tm, K), lambda i, j, kk: (i, 0))
    return _matmul(
        name, g, w, dims=NT, grid=(T // tm, 1, N_CHIP),
        a_spec=pl.BlockSpec((tm, N), lambda i, j, kk: (i, kk)),
        b_spec=pl.BlockSpec((None, K, N), lambda i, j, kk: (kk, 0, 0)),
        out_specs=[blk], out_shapes=[jax.ShapeDtypeStruct((T, K), F32)], acc_shape=(tm, K))[0]


def _mm_wgrad(name, a, g, a_cols, g_cols, a_blocked, g_blocked, after=()):
    T = a.shape[0]
    tt = T
    while tt > LANES and 2 * 2 * tt * (a_cols + g_cols) + (2 if tt == T else 3) * 4 * a_cols * g_cols > VMEM_BLOCK_BUDGET:
        tt //= 2
    return _matmul(
        name, a, g, dims=TN, grid=(N_CHIP, 1, T // tt),
        a_spec=pl.BlockSpec((tt, a_cols), (lambda j, i, kk: (kk, j)) if a_blocked else (lambda j, i, kk: (kk, 0))),
        b_spec=pl.BlockSpec((tt, g_cols), (lambda j, i, kk: (kk, j)) if g_blocked else (lambda j, i, kk: (kk, 0))),
        out_specs=[pl.BlockSpec((None, a_cols, g_cols), lambda j, i, kk: (j, 0, 0))],
        out_shapes=[jax.ShapeDtypeStruct((N_CHIP, a_cols, g_cols), F32)], acc_shape=(a_cols, g_cols), after=after)[0]


def _rms_fwd(name, x, g):
    T, Dm = x.shape
    tm = min(256, T)

    def body(x_ref, g_ref, h_ref):
        xv = x_ref[...]
        r = lax.rsqrt(jnp.mean(xv * xv, axis=-1, keepdims=True) + NORM_EPS)
        h_ref[...] = (xv * r * g_ref[...]).astype(BF16)

    return pl.pallas_call(
        body, name=name, grid=(T // tm,),
        in_specs=[pl.BlockSpec((tm, Dm), lambda i: (i, 0)), pl.BlockSpec((1, Dm), lambda i: (0, 0))],
        out_specs=pl.BlockSpec((tm, Dm), lambda i: (i, 0)),
        out_shape=jax.ShapeDtypeStruct((T, Dm), BF16), compiler_params=_params(1))(x, g)


def _rms_bwd(name, x, g, dh, dres):
    T, Dm = x.shape
    tm = min(256, T)

    def body(x_ref, g_ref, dh_ref, dres_ref, dx_ref, dg_ref):
        xv = x_ref[...]
        r = lax.rsqrt(jnp.mean(xv * xv, axis=-1, keepdims=True) + NORM_EPS)
        xn = xv * r
        dhv = dh_ref[...]
        dxn = dhv * g_ref[...]
        dx_ref[...] = dres_ref[...] + r * (dxn - xn * jnp.mean(dxn * xn, axis=-1, keepdims=True))

        @pl.when(pl.program_id(0) == 0)
        def _():
            dg_ref[...] = jnp.zeros_like(dg_ref)

        dg_ref[...] += _row0(jnp.sum(dhv * xn, axis=0, keepdims=True))

    tile = pl.BlockSpec((tm, Dm), lambda i: (i, 0))
    return pl.pallas_call(
        body, name=name, grid=(T // tm,),
        in_specs=[tile, pl.BlockSpec((1, Dm), lambda i: (0, 0)), tile, tile],
        out_specs=[tile, pl.BlockSpec((SUBLANES, Dm), lambda i: (0, 0))],
        out_shape=[jax.ShapeDtypeStruct((T, Dm), F32), jax.ShapeDtypeStruct((SUBLANES, Dm), F32)],
        compiler_params=_params(1))(x, g, dh, dres)


def _loss_head(x, g, tgt):
    T, Dm = x.shape
    tm = min(256, T)

    def body(x_ref, g_ref, t_ref, loss_ref, dx_ref, dg_ref):
        xv = x_ref[...]
        gv = g_ref[...]
        r = lax.rsqrt(jnp.mean(xv * xv, axis=-1, keepdims=True) + NORM_EPS)
        xn = xv * r
        err = xn * gv - t_ref[...]
        dy = err * (1.0 / Dm)
        dxn = dy * gv
        dx_ref[...] = r * (dxn - xn * jnp.mean(dxn * xn, axis=-1, keepdims=True))

        @pl.when(pl.program_id(0) == 0)
        def _():
            dg_ref[...] = jnp.zeros_like(dg_ref)
            loss_ref[...] = jnp.zeros_like(loss_ref)

        dg_ref[...] += _row0(jnp.sum(dy * xn, axis=0, keepdims=True))
        part = jnp.sum(jnp.sum(err * err, axis=-1, keepdims=True), axis=0, keepdims=True) * (0.5 / Dm)
        loss_ref[...] += jnp.broadcast_to(part, loss_ref.shape)

    tile = pl.BlockSpec((tm, Dm), lambda i: (i, 0))
    return pl.pallas_call(
        body, name="loss_head", grid=(T // tm,),
        in_specs=[tile, pl.BlockSpec((1, Dm), lambda i: (0, 0)), tile],
        out_specs=[pl.BlockSpec((SUBLANES, LANES), lambda i: (0, 0)), tile,
                   pl.BlockSpec((SUBLANES, Dm), lambda i: (0, 0))],
        out_shape=[jax.ShapeDtypeStruct((SUBLANES, LANES), F32), jax.ShapeDtypeStruct((T, Dm), F32),
                   jax.ShapeDtypeStruct((SUBLANES, Dm), F32)],
        compiler_params=_params(1))(x, g, tgt)


def _softmax_rows(lb_ref):
    rows = [lb_ref[pl.ds(i, 1), :] for i in range(DEPTH)]
    mx = rows[0]
    for r in rows[1:]:
        mx = jnp.maximum(mx, r)
    es = [jnp.exp(r - mx) for r in rows]
    tot = es[0]
    for e in es[1:]:
        tot = tot + e
    return [e / tot for e in es]


def _lbs_fwd(lower_bounds):
    def body(lb_ref, out_ref):
        sm = _softmax_rows(lb_ref)
        run = jnp.zeros_like(sm[0])
        out_ref[pl.ds(0, 1), :] = run
        for i in range(1, DEPTH):
            run = run + sm[i]
            out_ref[pl.ds(i, 1), :] = run

    return pl.pallas_call(body, name="lbs_fwd", out_shape=jax.ShapeDtypeStruct(lower_bounds.shape, F32))(lower_bounds)


def _lbs_bwd(lower_bounds, dlbs):
    def body(lb_ref, d_ref, out_ref):
        sm = _softmax_rows(lb_ref)
        dsm = [jnp.zeros_like(sm[0])]
        for i in range(1, DEPTH):
            acc = d_ref[pl.ds(i, 1), :]
            for l in range(i + 1, DEPTH):
                acc = acc + d_ref[pl.ds(l, 1), :]
            dsm.append(acc)
        inner = dsm[0] * sm[0]
        for i in range(1, DEPTH):
            inner = inner + dsm[i] * sm[i]
        for i in range(DEPTH):
            out_ref[pl.ds(i, 1), :] = sm[i] * (dsm[i] - inner)

    return pl.pallas_call(body, name="lbs_bwd", out_shape=jax.ShapeDtypeStruct(lower_bounds.shape, F32))(lower_bounds, dlbs)


N_LEVEL = 6


def _hgrn_consts():
    L = H_CHUNK
    t = np.arange(L)
    blocks = [(t[:, None] >= t[None, :]).astype(np.float32)]
    masks = []
    m = L // 2
    while m >= 1:
        blk, pos = t // (2 * m), t % (2 * m)
        start = blk * 2 * m
        mat = np.zeros((L, L), np.float32)
        for r in range(L):
            if pos[r] >= m:
                mat[r, start[r] + m:r + 1] = 1.0
            else:
                mat[r, r + 1:start[r] + m] = -1.0
        blocks.append(mat)
        masks.append(((blk[:, None] == blk[None, :]) & (pos[:, None] >= m) & (pos[None, :] < m)).astype(np.float32))
        m //= 2
    blocks.append(np.ones((L, L), np.float32))
    return jnp.asarray(np.concatenate(blocks, 0), BF16), jnp.asarray(np.stack(masks), F32)


def _hgrn_core(qraw, fp, lb, sum_mat, mask_ref):
    L = H_CHUNK
    sq = jax.nn.sigmoid(qraw)
    q = qraw * sq
    sneg = jax.nn.sigmoid(-fp)
    log_sig = jnp.minimum(fp, 0.0) - jnp.log1p(jnp.exp(-jnp.abs(fp)))
    a1 = jnp.log(jnp.maximum(lb, LB_FLOOR))
    a2 = jnp.log1p(-lb) + log_sig
    logf = jnp.maximum(a1, a2) + jnp.log1p(jnp.exp(-jnp.abs(a1 - a2)))
    w1 = jnp.exp(a1 - logf)
    w2 = jnp.exp(a2 - logf)
    k = (1.0 - lb) * sneg
    hi = logf.astype(BF16)
    r1 = logf - hi.astype(F32)
    mid = r1.astype(BF16)
    lo = (r1 - mid.astype(F32)).astype(BF16)
    sums = lax.dot_general(sum_mat, jnp.concatenate([hi, mid, lo], axis=1), NN, preferred_element_type=F32)
    sums = sums[:, 0:HEAD] + sums[:, HEAD:2 * HEAD] + sums[:, 2 * HEAD:3 * HEAD]
    b = sums[0:L]
    b_last = sums[(N_LEVEL + 1) * L:(N_LEVEL + 2) * L]
    eye = lax.broadcasted_iota(jnp.int32, (L, L), 0) == lax.broadcasted_iota(jnp.int32, (L, L), 1)
    attn = jnp.where(eye, jnp.sum(q * k, axis=1, keepdims=True), 0.0)
    fa, fb, ea, eb = [], [], [], []
    for l in range(N_LEVEL):
        d = sums[(l + 1) * L:(l + 2) * L]
        e_a = jnp.exp(jnp.minimum(d, 0.0))
        e_b = jnp.exp(jnp.minimum(-d, 0.0))
        a_l, b_l = q * e_a, k * e_b
        attn = attn + mask_ref[l] * _dot(a_l, b_l, NT)
        fa.append(a_l), fb.append(b_l), ea.append(e_a), eb.append(e_b)
    return dict(sq=sq, q=q, sneg=sneg, logf=logf, w1=w1, w2=w2, k=k, b=b, b_last=b_last, attn=attn,
                fa=fa, fb=fb, ea=ea, eb=eb)


def _hgrn_fwd(p, lbrow, gout):
    T = p.shape[0]
    nch = T // H_CHUNK
    sum_mat, masks = _hgrn_consts()

    def body(p_ref, lb_ref, g_ref, m_ref, mask_ref, o_ref, z_ref, st_ref, state):
        @pl.when(pl.program_id(0) == 0)
        def _():
            state[...] = jnp.zeros_like(state)

        sum_m = m_ref[...]
        for h in range(N_HEAD):
            col = lambda part: pl.ds(part * WIDTH + h * HEAD, HEAD)
            hs = pl.ds(h * HEAD, HEAD)
            v = p_ref[:, col(2)]
            c = _hgrn_core(p_ref[:, col(0)], p_ref[:, col(1)], lb_ref[:, hs], sum_m, mask_ref)
            s0 = state[h]
            st_ref[h] = s0
            o = _dot(c["attn"], v) + _dot(c["q"] * jnp.exp(c["b"]), s0, NT)
            k_dec = c["k"] * jnp.exp(c["b_last"] - c["b"])
            decay = jnp.exp(jnp.max(c["b_last"], axis=0, keepdims=True))
            state[h] = s0 * decay + _dot(v, k_dec, TN)
            o_ref[:, hs] = o
            r = lax.rsqrt(jnp.mean(o * o, axis=-1, keepdims=True) + NORM_EPS)
            z_ref[:, hs] = (o * r * g_ref[:, hs] * jax.nn.sigmoid(p_ref[:, col(3)])).astype(BF16)

    full = lambda shape: pl.BlockSpec(shape, lambda c: (0,) * len(shape))
    return pl.pallas_call(
        body, name="hgrn_fwd", grid=(nch,),
        in_specs=[pl.BlockSpec((H_CHUNK, 4 * WIDTH), lambda c: (c, 0)), full((1, WIDTH)), full((1, WIDTH)),
                  full(sum_mat.shape), full(masks.shape)],
        out_specs=[pl.BlockSpec((H_CHUNK, WIDTH), lambda c: (c, 0)),
                   pl.BlockSpec((None, H_CHUNK, WIDTH), lambda c: (0, c, 0)),
                   pl.BlockSpec((None, N_HEAD, HEAD, HEAD), lambda c: (c, 0, 0, 0))],
        out_shape=[jax.ShapeDtypeStruct((T, WIDTH), F32), jax.ShapeDtypeStruct((N_BRANCH, T, WIDTH), BF16),
                   jax.ShapeDtypeStruct((nch, N_HEAD, HEAD, HEAD), F32)],
        scratch_shapes=[pltpu.VMEM((N_HEAD, HEAD, HEAD), F32)], compiler_params=_params(1),
    )(p, lbrow, gout, sum_mat, masks)


def _hgrn_bwd(p, o_saved, dz, states, lbrow, gout, dp, after=()):
    T = p.shape[0]
    nch = T // H_CHUNK
    L = H_CHUNK
    sum_mat, masks = _hgrn_consts()

    def body(p_ref, o_ref, dz_ref, st_ref, lb_ref, g_ref, m_ref, mask_ref, dp_in, *rest):
        del dp_in
        dp_ref, dlb_ref, dg_ref, dstate = rest[len(after):]

        @pl.when(pl.program_id(0) == 0)
        def _():
            dstate[...] = jnp.zeros_like(dstate)
            dlb_ref[...] = jnp.zeros_like(dlb_ref)
            dg_ref[...] = jnp.zeros_like(dg_ref)

        sum_m = m_ref[...]
        for h in range(N_HEAD):
            col = lambda part: pl.ds(part * WIDTH + h * HEAD, HEAD)
            hs = pl.ds(h * HEAD, HEAD)
            qraw, fp, v, go = p_ref[:, col(0)], p_ref[:, col(1)], p_ref[:, col(2)], p_ref[:, col(3)]
            lb, g = lb_ref[:, hs], g_ref[:, hs]
            c = _hgrn_core(qraw, fp, lb, sum_m, mask_ref)
            q, k, b, b_last = c["q"], c["k"], c["b"], c["b_last"]
            s0, ds1 = st_ref[h], dstate[h]
            e_b = jnp.exp(b)
            q_dec = q * e_b
            e_bl = jnp.exp(b_last - b)
            k_dec = k * e_bl
            decay = jnp.exp(jnp.max(b_last, axis=0, keepdims=True))
            o = o_ref[:, hs]
            r = lax.rsqrt(jnp.mean(o * o, axis=-1, keepdims=True) + NORM_EPS)
            n = o * r
            sgo = jax.nn.sigmoid(go)
            dza = dz_ref[:, hs]
            dgo = dza * n * g * sgo * (1.0 - sgo)
            dg_ref[:, hs] += _row0(jnp.sum(dza * n * sgo, axis=0, keepdims=True))
            dn = dza * g * sgo
            do = r * (dn - n * jnp.mean(dn * n, axis=-1, keepdims=True))
            dattn = _dot(do, v, NT)
            dv = _dot(c["attn"], do, TN) + _dot(k_dec, ds1, NT)
            dq_dec = _dot(do, s0)
            dk_dec = _dot(v, ds1)
            ddiag = jnp.sum(do * v, axis=1, keepdims=True)
            dq = dq_dec * e_b + ddiag * k
            dk = dk_dec * e_bl + ddiag * q
            dsums = [dq_dec * q_dec - dk_dec * k_dec]
            for l in range(N_LEVEL):
                dm = mask_ref[l] * dattn
                da = _dot(dm, c["fb"][l])
                db = _dot(dm, c["fa"][l], TN)
                dq = dq + da * c["ea"][l]
                dk = dk + db * c["eb"][l]
                dsums.append(da * c["fa"][l] - db * c["fb"][l])
            dlast = jnp.sum(ds1 * s0, axis=0, keepdims=True) * decay
            dsums.append(dk_dec * k_dec + _row0(dlast, L))
            dlogf = _dot(sum_m, jnp.concatenate(dsums, axis=0), TN)
            dstate[h] = ds1 * decay + _dot(do, q_dec, TN)
            sq, sneg = c["sq"], c["sneg"]
            dqraw = dq * sq * (1.0 + qraw * (1.0 - sq))
            dfp = dlogf * c["w2"] * sneg - dk * (1.0 - lb) * sneg * (1.0 - sneg)
            inv_lb = jnp.where(lb > LB_FLOOR, 1.0 / jnp.maximum(lb, LB_FLOOR), 0.0)
            dlb_tok = dlogf * (c["w1"] * inv_lb - c["w2"] / (1.0 - lb)) - dk * sneg
            dlb_ref[:, hs] += _row0(jnp.sum(dlb_tok, axis=0, keepdims=True))
            dp_ref[:, col(0)] = dqraw.astype(BF16)
            dp_ref[:, col(1)] = dfp.astype(BF16)
            dp_ref[:, col(2)] = dv.astype(BF16)
            dp_ref[:, col(3)] = dgo.astype(BF16)

    full = lambda shape: pl.BlockSpec(shape, lambda c: (0,) * len(shape))
    rev = lambda c: nch - 1 - c
    return pl.pallas_call(
        body, name="hgrn_bwd", grid=(nch,),
        in_specs=[pl.BlockSpec((L, 4 * WIDTH), lambda c: (rev(c), 0)), pl.BlockSpec((L, WIDTH), lambda c: (rev(c), 0)),
                  pl.BlockSpec((None, L, WIDTH), lambda c: (0, rev(c), 0)),
                  pl.BlockSpec((None, N_HEAD, HEAD, HEAD), lambda c: (rev(c), 0, 0, 0)),
                  full((1, WIDTH)), full((1, WIDTH)), full(sum_mat.shape), full(masks.shape), ANY, *[ANY] * len(after)],
        out_specs=[pl.BlockSpec((L, 4 * WIDTH), lambda c: (rev(c), 0)), full((SUBLANES, WIDTH)), full((SUBLANES, WIDTH))],
        out_shape=[jax.ShapeDtypeStruct(dp.shape, dp.dtype), jax.ShapeDtypeStruct((SUBLANES, WIDTH), F32),
                   jax.ShapeDtypeStruct((SUBLANES, WIDTH), F32)],
        scratch_shapes=[pltpu.VMEM((N_HEAD, HEAD, HEAD), F32)], input_output_aliases={8: 0},
        compiler_params=_params(1),
    )(p, o_saved, dz, states, lbrow, gout, sum_mat, masks, dp, *after)


def _shift_down(tile, halo, s):
    tm = tile.shape[0]
    rows = lax.broadcasted_iota(jnp.int32, tile.shape, 0)
    head = jnp.concatenate([pltpu.roll(halo, s, 0), jnp.zeros((tm - SUBLANES, tile.shape[1]), tile.dtype)], axis=0)
    return jnp.where(rows < s, head, pltpu.roll(tile, s, 0))


def _shift_up(tile, halo, s):
    tm = tile.shape[0]
    rows = lax.broadcasted_iota(jnp.int32, tile.shape, 0)
    tail = jnp.concatenate([jnp.zeros((tm - SUBLANES, tile.shape[1]), tile.dtype), pltpu.roll(halo, SUBLANES - s, 0)], axis=0)
    return jnp.where(rows >= tm - s, tail, pltpu.roll(tile, tm - s, 0))


def _conv_fwd(p, w, z):
    T = p.shape[0]
    tm = _token_tile(T)
    per = tm // SUBLANES

    def body(bg_ref, cg_ref, xc_ref, hcg_ref, hxc_ref, w_ref, z_in, z_ref):
        del z_in
        zc = cg_ref[...] * xc_ref[...]
        hz = jnp.where(pl.program_id(0) > 0, hcg_ref[...] * hxc_ref[...], 0.0)
        y = (w_ref[pl.ds(0, 1), :] * _shift_down(zc, hz, 2) + w_ref[pl.ds(1, 1), :] * _shift_down(zc, hz, 1)
             + w_ref[pl.ds(2, 1), :] * zc)
        z_ref[...] = (bg_ref[...] * y).astype(BF16)

    tile = lambda cb: pl.BlockSpec((tm, WIDTH), lambda i: (i, cb))
    prev = lambda cb: pl.BlockSpec((SUBLANES, WIDTH), lambda i: (jnp.maximum(i * per - 1, 0), cb))
    return pl.pallas_call(
        body, name="conv_fwd", grid=(T // tm,),
        in_specs=[tile(4), tile(5), tile(6), prev(5), prev(6), pl.BlockSpec((CONV_K, WIDTH), lambda i: (0, 0)), ANY],
        out_specs=pl.BlockSpec((None, tm, WIDTH), lambda i: (1, i, 0)),
        out_shape=jax.ShapeDtypeStruct(z.shape, z.dtype), input_output_aliases={6: 0}, compiler_params=_params(1),
    )(p, p, p, p, p, w, z)


def _conv_bwd(p, w, dz, dp):
    T = p.shape[0]
    tm = _token_tile(T)
    per = tm // SUBLANES
    last = T // SUBLANES - 1

    def body(bg_ref, cg_ref, xc_ref, hcg_ref, hxc_ref, nbg_ref, dzb_ref, ndzb_ref, w_ref, dp_in, dp_ref, dw_ref, stash):
        del dp_in
        i, jj = pl.program_id(0), pl.program_id(1)

        @pl.when(jnp.logical_and(i == 0, jj == 0))
        def _():
            dw_ref[...] = jnp.zeros_like(dw_ref)

        @pl.when(jj == 0)
        def _():
            cg, xc, bg = cg_ref[...], xc_ref[...], bg_ref[...]
            w0, w1, w2 = w_ref[pl.ds(0, 1), :], w_ref[pl.ds(1, 1), :], w_ref[pl.ds(2, 1), :]
            zc = cg * xc
            hz = jnp.where(i > 0, hcg_ref[...] * hxc_ref[...], 0.0)
            z2, z1 = _shift_down(zc, hz, 2), _shift_down(zc, hz, 1)
            y = w0 * z2 + w1 * z1 + w2 * zc
            dzb = dzb_ref[...]
            dy = dzb * bg
            hdy = jnp.where(i < pl.num_programs(0) - 1, ndzb_ref[...] * nbg_ref[...], 0.0)
            dzc = w2 * dy + w1 * _shift_up(dy, hdy, 1) + w0 * _shift_up(dy, hdy, 2)
            rows = lax.broadcasted_iota(jnp.int32, (SUBLANES, WIDTH), 0)
            colsum = lambda t: jnp.sum(t, axis=0, keepdims=True)
            dw_ref[...] += (jnp.where(rows == 0, colsum(dy * z2), 0.0) + jnp.where(rows == 1, colsum(dy * z1), 0.0)
                            + jnp.where(rows == 2, colsum(dy * zc), 0.0))
            dp_ref[...] = (dzb * y).astype(BF16)
            stash[0] = dzc * xc
            stash[1] = dzc * cg

        @pl.when(jj > 0)
        def _():
            dp_ref[...] = stash[jj - 1].astype(BF16)

    n_tiles = T // tm
    tile = lambda cb: pl.BlockSpec((tm, WIDTH), lambda i, jj: (i, cb))
    prev = lambda cb: pl.BlockSpec((SUBLANES, WIDTH), lambda i, jj: (jnp.maximum(i * per - 1, 0), cb))
    nxt = lambda i: jnp.minimum((i + 1) * per, last)
    return pl.pallas_call(
        body, name="conv_bwd", grid=(n_tiles, 3),
        in_specs=[tile(4), tile(5), tile(6), prev(5), prev(6),
                  pl.BlockSpec((SUBLANES, WIDTH), lambda i, jj: (nxt(i), 4)),
                  pl.BlockSpec((None, tm, WIDTH), lambda i, jj: (1, i, 0)),
                  pl.BlockSpec((None, SUBLANES, WIDTH), lambda i, jj: (1, nxt(i), 0)),
                  pl.BlockSpec((CONV_K, WIDTH), lambda i, jj: (0, 0)), ANY],
        out_specs=[pl.BlockSpec((tm, WIDTH), lambda i, jj: (i, 4 + jj)),
                   pl.BlockSpec((SUBLANES, WIDTH), lambda i, jj: (0, 0))],
        out_shape=[jax.ShapeDtypeStruct(dp.shape, dp.dtype), jax.ShapeDtypeStruct((SUBLANES, WIDTH), F32)],
        scratch_shapes=[pltpu.VMEM((2, tm, WIDTH), F32)], input_output_aliases={9: 0}, compiler_params=_params(2),
    )(p, p, p, p, p, p, dz, dz, w, dp)


GELU_C = float(np.sqrt(2.0 / np.pi))
GELU_A = 0.044715


def _gelu(x):
    th = jnp.tanh(GELU_C * (x + GELU_A * x * x * x))
    return 0.5 * x * (1.0 + th), th


def _gelu_grad(x, th):
    return 0.5 * (1.0 + th) + 0.5 * x * (1.0 - th * th) * GELU_C * (1.0 + 3.0 * GELU_A * x * x)


def _sg_core(u, v, lng, lnb, ws_ref, bs_ref):
    gu, thu = _gelu(u)
    gv, thv = _gelu(v)
    xc = gv - jnp.mean(gv, axis=-1, keepdims=True)
    rs = lax.rsqrt(jnp.mean(xc * xc, axis=-1, keepdims=True) + LN_EPS)
    xh = xc * rs
    vp = xh * lng + lnb
    tril = (lax.broadcasted_iota(jnp.int32, (SG_CHUNK, SG_CHUNK), 0)
            >= lax.broadcasted_iota(jnp.int32, (SG_CHUNK, SG_CHUNK), 1))
    wm = [jnp.where(tril, ws_ref[g], 0.0).astype(BF16) for g in range(SG_GROUPS)]
    gs = lambda t, g: t[:, g * LANES:(g + 1) * LANES]
    sv = jnp.concatenate([_dot(wm[g], gs(vp, g)) + bs_ref[g] for g in range(SG_GROUPS)], axis=1)
    return dict(gu=gu, thu=thu, thv=thv, rs=rs, xh=xh, vp=vp, tril=tril, wm=wm, sv=sv)


def _sg_fwd(p, lng, lnb, ws, bs, z):
    T = p.shape[0]

    def body(u_ref, v_ref, lng_ref, lnb_ref, ws_ref, bs_ref, z_in, z_ref):
        del z_in
        c = _sg_core(u_ref[...], v_ref[...], lng_ref[...], lnb_ref[...], ws_ref, bs_ref)
        z_ref[...] = (c["gu"] * c["sv"]).astype(BF16)

    full = lambda shape: pl.BlockSpec(shape, lambda c: (0,) * len(shape))
    return pl.pallas_call(
        body, name="sg_fwd", grid=(T // SG_CHUNK,),
        in_specs=[pl.BlockSpec((SG_CHUNK, WIDTH), lambda c: (c, 7)), pl.BlockSpec((SG_CHUNK, WIDTH), lambda c: (c, 8)),
                  full((1, WIDTH)), full((1, WIDTH)), full(ws.shape), full(bs.shape), ANY],
        out_specs=pl.BlockSpec((None, SG_CHUNK, WIDTH), lambda c: (2, c, 0)),
        out_shape=jax.ShapeDtypeStruct(z.shape, z.dtype), input_output_aliases={6: 0}, compiler_params=_params(1),
    )(p, p, lng, lnb, ws, bs, z)


def _sg_bwd(p, lng, lnb, ws, bs, dz, dp):
    T = p.shape[0]

    def body(u_ref, v_ref, lng_ref, lnb_ref, ws_ref, bs_ref, dz_ref, dp_in, dp_ref, dws_ref, dbs_ref, dlng_ref, dlnb_ref,
             stash):
        del dp_in
        cidx, jj = pl.program_id(0), pl.program_id(1)

        @pl.when(jnp.logical_and(cidx == 0, jj == 0))
        def _():
            dws_ref[...] = jnp.zeros_like(dws_ref)
            dbs_ref[...] = jnp.zeros_like(dbs_ref)
            dlng_ref[...] = jnp.zeros_like(dlng_ref)
            dlnb_ref[...] = jnp.zeros_like(dlnb_ref)

        @pl.when(jj == 0)
        def _():
            u, v, lng = u_ref[...], v_ref[...], lng_ref[...]
            c = _sg_core(u, v, lng, lnb_ref[...], ws_ref, bs_ref)
            dzc = dz_ref[...]
            gs = lambda t, g: t[:, g * LANES:(g + 1) * LANES]
            dsv = dzc * c["gu"]
            dvp = []
            for g in range(SG_GROUPS):
                dsv_g = gs(dsv, g)
                dws_ref[g] += jnp.where(c["tril"], _dot(dsv_g, gs(c["vp"], g), NT), 0.0)
                dbs_ref[g] += jnp.sum(dsv_g, axis=1, keepdims=True)
                dvp.append(_dot(c["wm"][g], dsv_g, TN))
            dvp = jnp.concatenate(dvp, axis=1)
            xh = c["xh"]
            dlng_ref[...] += _row0(jnp.sum(dvp * xh, axis=0, keepdims=True))
            dlnb_ref[...] += _row0(jnp.sum(dvp, axis=0, keepdims=True))
            dxh = dvp * lng
            dgv = c["rs"] * (dxh - jnp.mean(dxh, axis=-1, keepdims=True) - xh * jnp.mean(dxh * xh, axis=-1, keepdims=True))
            dp_ref[...] = (dzc * c["sv"] * _gelu_grad(u, c["thu"])).astype(BF16)
            stash[...] = dgv * _gelu_grad(v, c["thv"])

        @pl.when(jj == 1)
        def _():
            dp_ref[...] = stash[...].astype(BF16)

    full = lambda shape: pl.BlockSpec(shape, lambda c, jj: (0,) * len(shape))
    return pl.pallas_call(
        body, name="sg_bwd", grid=(T // SG_CHUNK, 2),
        in_specs=[pl.BlockSpec((SG_CHUNK, WIDTH), lambda c, jj: (c, 7)), pl.BlockSpec((SG_CHUNK, WIDTH), lambda c, jj: (c, 8)),
                  full((1, WIDTH)), full((1, WIDTH)), full(ws.shape), full(bs.shape),
                  pl.BlockSpec((None, SG_CHUNK, WIDTH), lambda c, jj: (2, c, 0)), ANY],
        out_specs=[pl.BlockSpec((SG_CHUNK, WIDTH), lambda c, jj: (c, 7 + jj)), full(ws.shape), full(bs.shape),
                   full((SUBLANES, WIDTH)), full((SUBLANES, WIDTH))],
        out_shape=[jax.ShapeDtypeStruct(dp.shape, dp.dtype), jax.ShapeDtypeStruct(ws.shape, F32),
                   jax.ShapeDtypeStruct(bs.shape, F32), jax.ShapeDtypeStruct((SUBLANES, WIDTH), F32),
                   jax.ShapeDtypeStruct((SUBLANES, WIDTH), F32)],
        scratch_shapes=[pltpu.VMEM((SG_CHUNK, WIDTH), F32)], input_output_aliases={7: 0}, compiler_params=_params(2),
    )(p, p, lng, lnb, ws, bs, dz, dp)


BRANCH_COLS = D_MODEL // N_CHIP
GATE_UNIT0 = GATE_COL0 // WIDTH
UNITS = D_MODEL // WIDTH


def _unit_specs(order):
    def spec(which):
        def index(*g):
            _, n, u = order(*g)
            return (2 * u + which, n, 0, 0)
        return pl.BlockSpec((None, None, WIDTH, BRANCH_COLS), index)
    return [spec(0), spec(1)]


def _merge_fwd(z, p, wb):
    T = z.shape[1]
    tm = _token_tile(T)
    order = lambda i, u, n: (i, n, u)

    def body(z_ref, wa_ref, wb_ref, gt_ref, out_ref, acc):
        n = pl.program_id(2)
        zv = z_ref[...]
        y = jnp.concatenate([_dot(zv, wa_ref[...]), _dot(zv, wb_ref[...])], axis=1)
        part = jax.nn.sigmoid(gt_ref[...]) * y

        @pl.when(n == 0)
        def _():
            acc[...] = part

        @pl.when(n > 0)
        def _():
            acc[...] += part

        @pl.when(n == N_BRANCH - 1)
        def _():
            out_ref[...] = acc[...].astype(BF16)

    return pl.pallas_call(
        body, name="merge_fwd", grid=(T // tm, UNITS, N_BRANCH),
        in_specs=[pl.BlockSpec((None, tm, WIDTH), lambda i, u, n: (n, i, 0)), *_unit_specs(order),
                  pl.BlockSpec((tm, WIDTH), lambda i, u, n: (i, GATE_UNIT0 + UNITS * n + u))],
        out_specs=pl.BlockSpec((tm, WIDTH), lambda i, u, n: (i, u)),
        out_shape=jax.ShapeDtypeStruct((T, D_MODEL), BF16),
        scratch_shapes=[pltpu.VMEM((tm, WIDTH), F32)], compiler_params=_params(3))(z, wb, wb, p)


def _merge_bwd(z, p, wb, dmerged):
    T = z.shape[1]
    tm = _token_tile(T)
    order = lambda i, n, u: (i, n, u)

    def body(z_ref, wa_ref, wb_ref, gt_ref, dm_ref, dp_ref, dy_ref, dz_ref):
        u = pl.program_id(2)
        zv, wa, wbv = z_ref[...], wa_ref[...], wb_ref[...]
        y = jnp.concatenate([_dot(zv, wa), _dot(zv, wbv)], axis=1)
        gate = jax.nn.sigmoid(gt_ref[...])
        dm = dm_ref[...]
        dp_ref[...] = (dm * y * gate * (1.0 - gate)).astype(BF16)
        dyv = (dm * gate).astype(BF16)
        dy_ref[...] = dyv
        part = _dot(dyv[:, :BRANCH_COLS], wa, NT) + _dot(dyv[:, BRANCH_COLS:], wbv, NT)

        @pl.when(u == 0)
        def _():
            dz_ref[...] = part

        @pl.when(u > 0)
        def _():
            dz_ref[...] += part

    unit = lambda i, n, u: (i, GATE_UNIT0 + UNITS * n + u)
    return pl.pallas_call(
        body, name="merge_bwd", grid=(T // tm, N_BRANCH, UNITS),
        in_specs=[pl.BlockSpec((None, tm, WIDTH), lambda i, n, u: (n, i, 0)), *_unit_specs(order),
                  pl.BlockSpec((tm, WIDTH), unit), pl.BlockSpec((tm, WIDTH), lambda i, n, u: (i, u))],
        out_specs=[pl.BlockSpec((tm, WIDTH), unit), pl.BlockSpec((None, tm, WIDTH), lambda i, n, u: (n, i, u)),
                   pl.BlockSpec((None, tm, WIDTH), lambda i, n, u: (n, i, 0))],
        out_shape=[jax.ShapeDtypeStruct((T, IN_COLS), BF16), jax.ShapeDtypeStruct((N_BRANCH, T, D_MODEL), BF16),
                   jax.ShapeDtypeStruct((N_BRANCH, T, WIDTH), F32)],
        compiler_params=_params(3))(z, wb, wb, p, dmerged)


def _branch_wgrad(z, dy):
    T = z.shape[1]
    tt = T
    nk = T // tt

    def body(z_ref, dy_ref, out_ref, acc):
        kk = pl.program_id(1)
        part = _dot(z_ref[...], dy_ref[...], TN)

        @pl.when(kk == 0)
        def _():
            acc[...] = part

        @pl.when(kk > 0)
        def _():
            acc[...] += part

        @pl.when(kk == nk - 1)
        def _():
            for k in range(N_CHIP):
                out_ref[k] = acc[:, k * BRANCH_COLS:(k + 1) * BRANCH_COLS]

    return pl.pallas_call(
        body, name="branch_wgrad", grid=(N_BRANCH, nk),
        in_specs=[pl.BlockSpec((None, tt, WIDTH), lambda n, kk: (n, kk, 0)),
                  pl.BlockSpec((None, tt, D_MODEL), lambda n, kk: (n, kk, 0))],
        out_specs=pl.BlockSpec((N_CHIP, None, WIDTH, BRANCH_COLS), lambda n, kk: (0, n, 0, 0)),
        out_shape=jax.ShapeDtypeStruct((N_CHIP, N_BRANCH, WIDTH, BRANCH_COLS), F32),
        scratch_shapes=[pltpu.VMEM((WIDTH, D_MODEL), F32)], compiler_params=_params(2))(z, dy)


def _layer_fwd(x, wts, small, hook):
    win, wb, wo, w1, w2 = wts
    h = _rms_fwd("rms_mix", x, small["g_mix"])
    p = _mm_cols("in_proj", h, win, [F32])[0]
    o_hgrn, z, states = _hgrn_fwd(p, small["lbs"], small["g_hgrn_out"])
    z = _conv_fwd(p, small["w_conv"], z)
    z = _sg_fwd(p, small["sg_ln_g"], small["sg_ln_b"], small["w_sg"], small["b_sg"], z)
    merged = _merge_fwd(z, p, wb)
    x_mid = _mm_rows("out_proj", merged, wo, x)
    h2 = _rms_fwd("rms_ffn", x_mid, small["g_ffn"])
    a, s = _mm_cols("ff1", h2, w1, [F32, BF16], epilogue=lambda acc: (acc, jnp.square(jnp.maximum(acc, 0.0))))
    x_out = _mm_rows("ff2", s, w2, x_mid, after=hook([s]))
    saved = dict(x=x, h=h, p=p, o_hgrn=o_hgrn, z=z, states=states, merged=merged, x_mid=x_mid, h2=h2, a=a, s=s)
    return x_out, saved


def _layer_bwd(dx_out, sv, wts, small, tick, after):
    win, wb, wo, w1, w2 = wts
    g = {}
    da = _mm_cols_t("ff2_dgrad", dx_out, w2, BF16, extra=(sv["a"],), after=after,
                    epilogue=lambda acc, a: (acc * 2.0 * jnp.maximum(a, 0.0),))
    d_ff2 = _mm_wgrad("ff2_wgrad", sv["s"], dx_out, w2.shape[1], D_MODEL, True, False)
    d_ff1 = _mm_wgrad("ff1_wgrad", sv["h2"], da, D_MODEL, w1.shape[2], False, True)
    dh2 = _mm_rows_t("ff1_dgrad", da, w1)
    dx_mid, g["g_ffn"] = _rms_bwd("rms_ffn_bwd", sv["x_mid"], small["g_ffn"], dh2, dx_out)
    after = tick([dx_mid], [("w_ff1", d_ff1), ("w_ff2", d_ff2)])
    dmerged = _mm_cols_t("out_proj_dgrad", dx_mid, wo, F32, after=after)
    d_o = _mm_wgrad("out_proj_wgrad", sv["merged"], dx_mid, wo.shape[1], D_MODEL, True, False)
    dp, dy, dz = _merge_bwd(sv["z"], sv["p"], wb, dmerged)
    d_branch = _branch_wgrad(sv["z"], dy).reshape(N_CHIP, N_BRANCH * WIDTH, BRANCH_COLS)
    after = tick([dp], [("w_branch", d_branch), ("w_o", d_o)])
    dp, g["lbs"], g["g_hgrn_out"] = _hgrn_bwd(sv["p"], sv["o_hgrn"], dz, sv["states"], small["lbs"],
                                              small["g_hgrn_out"], dp, after=after)
    dp, g["w_conv"] = _conv_bwd(sv["p"], small["w_conv"], dz, dp)
    dp, g["w_sg"], g["b_sg"], g["sg_ln_g"], g["sg_ln_b"] = _sg_bwd(
        sv["p"], small["sg_ln_g"], small["sg_ln_b"], small["w_sg"], small["b_sg"], dz, dp)
    after = tick([dp], [])
    d_in = _mm_wgrad("in_proj_wgrad", sv["h"], dp, D_MODEL, win.shape[2], False, True, after=after)
    dh = _mm_rows_t("in_proj_dgrad", dp, win)
    dx, g["g_mix"] = _rms_bwd("rms_mix_bwd", sv["x"], small["g_mix"], dh, dx_mid)
    return dx, g, tick([dx], [("w_in", d_in)])


def _mesh_pos():
    return lax.axis_index("x"), lax.axis_index("y"), lax.axis_index("c")


def _other_chips(x, y):
    return [(1 - x, y), (x, 1 - y), (1 - x, 1 - y)]


def _remote(src, dst, send_sems, recv_sems, k, to):
    return pltpu.make_async_remote_copy(src_ref=src, dst_ref=dst, send_sem=send_sems.at[k], recv_sem=recv_sems.at[k],
                                        device_id=to, device_id_type=MESH)


def _gather_call(name, body, buf, after):
    scratch = [pltpu.SemaphoreType.DMA((7,)), pltpu.SemaphoreType.DMA((7,))]
    return pl.pallas_call(
        body, name=name, in_specs=[ANY] * (1 + len(after)), out_specs=ANY,
        out_shape=jax.ShapeDtypeStruct(buf.shape, buf.dtype), scratch_shapes=scratch, input_output_aliases={0: 0})(buf, *after)


HBM = pl.BlockSpec(memory_space=pltpu.HBM)
SEM = pl.BlockSpec(memory_space=pltpu.SEMAPHORE)
DATAFLOW = pltpu.SideEffectType.DATAFLOW_SIDE_EFFECTING


def _split_start(name, bufs, copies, n_copies, after=()):
    n = len(bufs)

    def body(*refs):
        send_sems, recv_sems = refs[n + len(after)], refs[n + len(after) + 1]
        for cp in copies(refs[:n], send_sems, recv_sems):
            cp.start()
        refs[-1][...] = jnp.zeros_like(refs[-1])

    outs = pl.pallas_call(
        body, name=name,
        out_shape=(pltpu.SemaphoreType.DMA((n_copies,)), pltpu.SemaphoreType.DMA((n_copies,)),
                   *[pltpu.HBM(b.shape, b.dtype) for b in bufs], jax.ShapeDtypeStruct((SUBLANES, LANES), F32)),
        in_specs=[HBM] * n + [ANY] * len(after),
        out_specs=(SEM, SEM, *[HBM] * n, pl.BlockSpec(memory_space=pltpu.VMEM)),
        input_output_aliases={t: 2 + t for t in range(n)},
        compiler_params=pltpu.CompilerParams(has_side_effects=DATAFLOW),
    )(*[pltpu.with_memory_space_constraint(b, pltpu.HBM) for b in bufs], *after)
    return outs[0], outs[1], list(outs[2:2 + n]), outs[-1]


def _split_wait(name, started, copies, after):
    send_sems, recv_sems, bufs, _ = started
    n = len(bufs)

    def body(*refs):
        for cp in copies(refs[:n], refs[n], refs[n + 1]):
            cp.wait_send()
            cp.wait_recv()

    return list(pl.pallas_call(
        body, name=name, out_shape=tuple(pltpu.HBM(b.shape, b.dtype) for b in bufs),
        in_specs=[HBM] * n + [SEM, SEM] + [ANY] * len(after), out_specs=tuple([HBM] * n),
        input_output_aliases={t: t for t in range(n)},
        compiler_params=pltpu.CompilerParams(has_side_effects=DATAFLOW),
    )(*bufs, send_sems, recv_sems, *after))


def _weight_ici_copies(refs, send_sems, recv_sems):
    x, y, c = _mesh_pos()
    out = []
    for t, ref in enumerate(refs):
        rh = ref.shape[1] // 2
        mine = ref.at[2 * x + y, pl.ds(c * rh, rh), :]
        out += [_remote(mine, mine, send_sems, recv_sems, 3 * t + j, (*chip, c)) for j, chip in enumerate(_other_chips(x, y))]
    return out


def _weight_d2d_copies(refs, send_sems, recv_sems):
    x, y, c = _mesh_pos()
    out = []
    for t, ref in enumerate(refs):
        rh = ref.shape[1] // 2
        for j, chip in enumerate(_other_chips(x, y)):
            blk = ref.at[2 * chip[0] + chip[1], pl.ds(c * rh, rh), :]
            out.append(_remote(blk, blk, send_sems, recv_sems, 3 * t + j, (x, y, 1 - c)))
    return out


def _swap_part(refs, send_sems, recv_sems, s0):
    x, y, c = _mesh_pos()
    n = len(refs) // 2
    out = []
    for t in range(n):
        rh = refs[t].shape[1] // 2
        out.append(_remote(refs[t].at[:, pl.ds((1 - c) * rh, rh), :], refs[n + t], send_sems, recv_sems, s0 + t, (x, y, 1 - c)))
    return out


def _exchange_part(refs, send_sems, recv_sems, s0):
    x, y, c = _mesh_pos()
    n = len(refs) // 2
    out = []
    for t in range(n):
        for j, chip in enumerate(_other_chips(x, y)):
            out.append(_remote(refs[t].at[2 * chip[0] + chip[1]], refs[n + t].at[j], send_sems, recv_sems, s0 + 3 * t + j,
                               (*chip, c)))
    return out


def _gather_part(refs, send_sems, recv_sems, s0):
    x, y, c = _mesh_pos()
    return [_remote(ref.at[c], ref.at[c], send_sems, recv_sems, s0 + t, (x, y, 1 - c)) for t, ref in enumerate(refs)]


class _GradPipeline:
    def __init__(self, pos):
        self.pos = pos
        self.groups, self.pending, self.count = [], None, 0
        self.reduced = {n: [None] * DEPTH for n in BIG_NAMES}

    def busy(self):
        return bool(self.groups) or self.pending is not None

    def tick(self, deps, new):
        if self.pending is not None:
            started, copies, owners = self.pending
            bufs = _split_wait("grad_pipe_wait_%d" % self.count, started, copies, after=list(deps))
            for grp, lo, hi in owners:
                grp["bufs"] = bufs[lo:hi]
            self.pending = None
        parts = []
        for grp in list(self.groups):
            n, names = len(grp["names"]), grp["names"]
            if grp["stage"] == "swap":
                pair = [_pair_sum("grad_pair_sum_" + nm, f, r, self.pos)
                        for nm, f, r in zip(names, grp["bufs"][:n], grp["bufs"][n:])]
                grp["own32"] = [p32 for p32, _ in pair]
                landing = [lax.empty((3, *p16.shape[1:]), BF16) for _, p16 in pair]
                grp["stage"] = "exchange"
                parts.append((grp, [p16 for _, p16 in pair] + landing, _exchange_part, 3 * n))
            elif grp["stage"] == "exchange":
                halves = [_chip_sum("grad_chip_sum_" + nm, p32, r, self.pos)
                          for nm, p32, r in zip(names, grp["own32"], grp["bufs"][n:])]
                grp["stage"] = "gather"
                parts.append((grp, halves, _gather_part, n))
            else:
                for nm, b in zip(names, grp["bufs"]):
                    self.reduced[nm][grp["layer"]] = b.reshape(-1, b.shape[-1])
                self.groups.remove(grp)
        if new:
            grp = dict(names=[nm for nm, _, _ in new], layer=new[0][1], stage="swap")
            self.groups.append(grp)
            fulls = [g for _, _, g in new]
            landing = [lax.empty((N_CHIP, g.shape[1] // 2, g.shape[2]), F32) for g in fulls]
            parts.append((grp, fulls + landing, _swap_part, len(fulls)))
        if not parts:
            return ()
        bufs, layout, owners, sems = [], [], [], 0
        for grp, part_bufs, fn, n_sems in parts:
            layout.append((len(bufs), len(bufs) + len(part_bufs), fn, sems))
            owners.append((grp, len(bufs), len(bufs) + len(part_bufs)))
            bufs += part_bufs
            sems += n_sems

        def copies(refs, send_sems, recv_sems):
            out = []
            for lo, hi, fn, s0 in layout:
                out += fn(refs[lo:hi], send_sems, recv_sems, s0)
            return out

        started = _split_start("grad_pipe_start_%d" % self.count, bufs, copies, sems)
        self.pending = (started, copies, owners)
        self.count += 1
        return (started[3],)


def _gather_all(name, block, slot, after=()):
    buf = lax.dynamic_update_slice(jnp.zeros((8, *block.shape), block.dtype), block[None], (slot, 0, 0))

    def body(*refs):
        out_ref, send_sems, recv_sems = refs[1 + len(after):]
        x, y, c = _mesh_pos()
        chips = _other_chips(x, y)
        sibling = (x, y, 1 - c)
        slot_of = lambda px, py, pc: out_ref.at[4 * px + 2 * py + pc]
        started = [_remote(slot_of(x, y, c), slot_of(x, y, c), send_sems, recv_sems, 0, sibling)]
        started += [_remote(slot_of(x, y, c), slot_of(x, y, c), send_sems, recv_sems, 1 + j, (*chip, c))
                    for j, chip in enumerate(chips)]
        for cp in started:
            cp.start()
        for j, chip in enumerate(chips):
            _remote(slot_of(*chip, c), slot_of(*chip, c), send_sems, recv_sems, 1 + j, (*chip, c)).wait_recv()
            fw = _remote(slot_of(*chip, c), slot_of(*chip, c), send_sems, recv_sems, 4 + j, sibling)
            fw.start()
            started.append(fw)
        _remote(slot_of(x, y, 1 - c), slot_of(x, y, 1 - c), send_sems, recv_sems, 0, sibling).wait_recv()
        for j, chip in enumerate(chips):
            _remote(slot_of(*chip, 1 - c), slot_of(*chip, 1 - c), send_sems, recv_sems, 4 + j, sibling).wait_recv()
        for cp in started:
            cp.wait_send()

    return _gather_call(name, body, buf, after)


def _row_tile(rows, cols):
    cap = max(SUBLANES, ELEMWISE_BLOCK_BYTES // (4 * cols))
    tr = rows
    while tr > cap and tr % 2 == 0:
        tr //= 2
    return tr


def _pair_sum(name, grad, recv, pos):
    _, rh, cols = recv.shape
    tr = _row_tile(rh, cols)
    per = rh // tr

    def body(pos_ref, g_ref, r_ref, own_ref, out16_ref):
        s = g_ref[...] + r_ref[...]
        out16_ref[...] = s.astype(BF16)

        @pl.when(pl.program_id(1) == pos_ref[0])
        def _():
            own_ref[...] = s

    blk = pl.BlockSpec((None, tr, cols), lambda i, k, pos_ref: (k, i, 0))
    return pl.pallas_call(
        body, name=name,
        grid_spec=pltpu.PrefetchScalarGridSpec(
            num_scalar_prefetch=1, grid=(per, N_CHIP),
            in_specs=[pl.BlockSpec((None, tr, cols), lambda i, k, pos_ref: (k, pos_ref[1] * per + i, 0)), blk],
            out_specs=[pl.BlockSpec((tr, cols), lambda i, k, pos_ref: (i, 0)), blk]),
        out_shape=[jax.ShapeDtypeStruct((rh, cols), F32), jax.ShapeDtypeStruct(recv.shape, BF16)],
        compiler_params=_params(2))(pos, grad, recv)


def _chip_sum(name, own32, recv, pos):
    rh, cols = own32.shape
    tr = _row_tile(rh, cols)

    def body(pos_ref, own_ref, r_ref, out_ref):
        del pos_ref
        out_ref[...] = ((own_ref[...] + r_ref[0].astype(F32)) + r_ref[1].astype(F32)) + r_ref[2].astype(F32)

    return pl.pallas_call(
        body, name=name,
        grid_spec=pltpu.PrefetchScalarGridSpec(
            num_scalar_prefetch=1, grid=(rh // tr,),
            in_specs=[pl.BlockSpec((tr, cols), lambda i, pos_ref: (i, 0)),
                      pl.BlockSpec((3, tr, cols), lambda i, pos_ref: (0, i, 0))],
            out_specs=pl.BlockSpec((None, tr, cols), lambda i, pos_ref: (pos_ref[1], i, 0))),
        out_shape=jax.ShapeDtypeStruct((2, rh, cols), F32), compiler_params=_params(1))(pos, own32, recv)


def _cast_into_slot(name, w, layer, pos, after=()):
    _, rows, cols = w.shape
    tr = _row_tile(rows, cols)

    def body(pos_ref, w_ref, *rest):
        del pos_ref
        rest[-1][...] = w_ref[...].astype(BF16)

    return pl.pallas_call(
        body, name=name,
        grid_spec=pltpu.PrefetchScalarGridSpec(
            num_scalar_prefetch=1, grid=(rows // tr,),
            in_specs=[pl.BlockSpec((None, tr, cols), lambda i, pos_ref: (layer, i, 0))] + [ANY] * len(after),
            out_specs=pl.BlockSpec((None, tr, cols), lambda i, pos_ref: (pos_ref[0], i, 0))),
        out_shape=jax.ShapeDtypeStruct((N_CHIP, rows, cols), BF16), compiler_params=_params(1))(pos, w, *after)


def _adamw_math(w, g, m, v):
    m = ADAM_B1 * m + (1.0 - ADAM_B1) * g
    v = ADAM_B2 * v + (1.0 - ADAM_B2) * jnp.square(g)
    m_hat = m / (1.0 - ADAM_B1 ** ADAM_STEP)
    v_hat = v / (1.0 - ADAM_B2 ** ADAM_STEP)
    delta = -ADAM_LR * (m_hat / (jnp.sqrt(v_hat) + ADAM_EPS) + ADAM_WD * w)
    return delta, m, v


def _adamw_layers(name, w, m, v, grads, first, into=None, after=()):
    _, rows, cols = w.shape
    tr = _row_tile(rows, cols)
    n_layers = len(grads)

    def body(w_ref, m_ref, v_ref, *rest):
        g_refs, (grad_ref, d_ref, nm_ref, nv_ref) = rest[:n_layers], rest[len(rest) - 4:]
        layer = pl.program_id(0)
        g = g_refs[0][...]
        for l in range(1, n_layers):
            g = jnp.where(layer == l, g_refs[l][...], g)
        grad_ref[...] = g
        d_ref[...], nm_ref[...], nv_ref[...] = _adamw_math(w_ref[...], g, m_ref[...], v_ref[...])

    blk = pl.BlockSpec((None, tr, cols), lambda l, i: (first + l, i, 0))
    g_spec = lambda k: pl.BlockSpec((tr, cols), lambda l, i: (jnp.where(l == k, i, 0), 0))
    passed = list(into or []) + list(after)
    return pl.pallas_call(
        body, name=name, grid=(n_layers, rows // tr),
        in_specs=[blk, blk, blk] + [g_spec(k) for k in range(n_layers)] + [ANY] * len(passed), out_specs=[blk] * 4,
        out_shape=[jax.ShapeDtypeStruct(w.shape, F32)] * 4,
        input_output_aliases={3 + n_layers + t: t for t in range(4)} if into else {},
        compiler_params=_params(2))(w, m, v, *grads, *passed)


def _sum_devices(gathered):
    _, rows, cols = gathered.shape

    def body(g_ref, out_ref):
        s = g_ref[0]
        for d in range(1, 8):
            s = s + g_ref[d]
        out_ref[...] = s

    return pl.pallas_call(body, name="sum_devices", out_shape=jax.ShapeDtypeStruct((rows, cols), F32),
                          compiler_params=pltpu.CompilerParams(vmem_limit_bytes=VMEM_LIMIT_BYTES))(gathered)


def _adamw_flat(w, g, m, v):
    def body(w_ref, g_ref, m_ref, v_ref, d_ref, nm_ref, nv_ref):
        d_ref[...], nm_ref[...], nv_ref[...] = _adamw_math(w_ref[...], g_ref[...], m_ref[...], v_ref[...])

    return pl.pallas_call(body, name="adamw_small", out_shape=[jax.ShapeDtypeStruct(w.shape, F32)] * 3,
                          compiler_params=pltpu.CompilerParams(vmem_limit_bytes=VMEM_LIMIT_BYTES))(w, g, m, v)


SMALL_NAMES = ["g_mix", "lower_bounds", "g_hgrn_out", "w_conv", "sg_ln_g", "sg_ln_b", "w_sg", "b_sg", "g_ffn", "g_final"]
BIG_NAMES = ["w_in", "w_branch", "w_o", "w_ff1", "w_ff2"]
WEIGHT_ORDER = ["w_in", "g_mix", "lower_bounds", "g_hgrn_out", "w_conv", "sg_ln_g", "sg_ln_b", "w_sg", "b_sg", "w_branch",
                "w_o", "g_ffn", "w_ff1", "w_ff2", "g_final"]


def _padded_rows(n):
    return -(-n // SUBLANES) * SUBLANES


def _pack(arrays):
    parts = []
    for a in arrays:
        a = a.reshape(-1, LANES)
        parts.append(jnp.pad(a, ((0, _padded_rows(a.shape[0]) - a.shape[0]), (0, 0))))
    return jnp.concatenate(parts, axis=0)


def _unpack(flat, shapes):
    out, row = [], 0
    for s in shapes:
        n = int(np.prod(s)) // LANES
        out.append(flat[row:row + n].reshape(s))
        row += _padded_rows(n)
    return out


def _as_2d(name, a):
    return a.reshape(DEPTH, N_BRANCH * WIDTH, BRANCH_COLS) if name == "w_branch" else a


def kernel(x, w_in, g_mix, lower_bounds, g_hgrn_out, w_conv, sg_ln_g, sg_ln_b, w_sg, b_sg, w_branch, w_o, g_ffn, w_ff1, w_ff2, g_final, loss_target, m_w_in, m_g_mix, m_lower_bounds, m_g_hgrn_out, m_w_conv, m_sg_ln_g, m_sg_ln_b, m_w_sg, m_b_sg, m_w_branch, m_w_o, m_g_ffn, m_w_ff1, m_w_ff2, m_g_final, v_w_in, v_g_mix, v_lower_bounds, v_g_hgrn_out, v_w_conv, v_sg_ln_g, v_sg_ln_b, v_w_sg, v_b_sg, v_w_branch, v_w_o, v_g_ffn, v_w_ff1, v_w_ff2, v_g_final):
    weights = dict(w_in=w_in, g_mix=g_mix, lower_bounds=lower_bounds, g_hgrn_out=g_hgrn_out, w_conv=w_conv,
                   sg_ln_g=sg_ln_g, sg_ln_b=sg_ln_b, w_sg=w_sg, b_sg=b_sg, w_branch=w_branch, w_o=w_o, g_ffn=g_ffn,
                   w_ff1=w_ff1, w_ff2=w_ff2, g_final=g_final)
    mom1 = dict(w_in=m_w_in, g_mix=m_g_mix, lower_bounds=m_lower_bounds, g_hgrn_out=m_g_hgrn_out, w_conv=m_w_conv,
                sg_ln_g=m_sg_ln_g, sg_ln_b=m_sg_ln_b, w_sg=m_w_sg, b_sg=m_b_sg, w_branch=m_w_branch, w_o=m_w_o,
                g_ffn=m_g_ffn, w_ff1=m_w_ff1, w_ff2=m_w_ff2, g_final=m_g_final)
    mom2 = dict(w_in=v_w_in, g_mix=v_g_mix, lower_bounds=v_lower_bounds, g_hgrn_out=v_g_hgrn_out, w_conv=v_w_conv,
                sg_ln_g=v_sg_ln_g, sg_ln_b=v_sg_ln_b, w_sg=v_w_sg, b_sg=v_b_sg, w_branch=v_w_branch, w_o=v_w_o,
                g_ffn=v_g_ffn, w_ff1=v_w_ff1, w_ff2=v_w_ff2, g_final=v_g_final)
    xi, yi, ci = _mesh_pos()
    pos = jnp.stack([2 * xi + yi, ci]).astype(jnp.int32)
    device = 4 * xi + 2 * yi + ci
    conv_cols = w_conv.shape[2]

    n_big = len(BIG_NAMES)
    flight = {}
    casts = {0: [_cast_into_slot("cast_" + n, _as_2d(n, weights[n]), 0, pos) for n in BIG_NAMES]}
    flight["ici"] = _split_start("weights_ici_start_0", casts[0], _weight_ici_copies, 3 * n_big)
    first = (flight["ici"][3],)
    for l in range(1, DEPTH):
        casts[l] = [_cast_into_slot("cast_" + n, _as_2d(n, weights[n]), l, pos, after=first) for n in BIG_NAMES]
    conv_all = _gather_all("gather_w_conv", w_conv.reshape(DEPTH * CONV_K, conv_cols), device, after=first)
    conv_full = conv_all.reshape(N_CHIP, 2, DEPTH, CONV_K, conv_cols)[:, 0].transpose(1, 2, 0, 3).reshape(DEPTH, CONV_K, WIDTH)
    lbs = _lbs_fwd(lower_bounds)

    act = x[0]
    layers = []

    def fetch(l, deps):
        landed = _split_wait("weights_ici_wait_%d" % l, flight.pop("ici"), _weight_ici_copies, after=deps)
        token = []
        if l + 1 < DEPTH:
            flight["ici"] = _split_start("weights_ici_start_%d" % (l + 1), casts[l + 1], _weight_ici_copies, 3 * n_big,
                                         after=[landed[0]])
            token = [flight["ici"][3]]
        flight["d2d"] = _split_start("weights_d2d_start_%d" % l, landed, _weight_d2d_copies, 3 * n_big, after=token)
        return (flight["d2d"][3],)

    fetch(0, [act, lbs, conv_full] + [c for l in range(1, DEPTH) for c in casts[l]])
    for l in range(DEPTH):
        gathered = _split_wait("weights_d2d_wait_%d" % l, flight.pop("d2d"), _weight_d2d_copies, after=[act])
        wts = [gathered[0], gathered[1].reshape(N_CHIP, N_BRANCH, WIDTH, BRANCH_COLS), *gathered[2:]]
        small = dict(g_mix=g_mix[l:l + 1], lbs=lbs[l:l + 1], g_hgrn_out=g_hgrn_out[l:l + 1], w_conv=conv_full[l],
                     sg_ln_g=sg_ln_g[l:l + 1], sg_ln_b=sg_ln_b[l:l + 1], w_sg=w_sg[l],
                     b_sg=b_sg[l].reshape(SG_GROUPS, SG_CHUNK, 1), g_ffn=g_ffn[l:l + 1])
        hook = (lambda deps, l=l: fetch(l + 1, deps)) if l + 1 < DEPTH else (lambda deps: ())
        act, saved = _layer_fwd(act, wts, small, hook)
        layers.append((wts, small, saved))
    loss_blk, dact, dg_final = _loss_head(act, g_final.reshape(1, D_MODEL), loss_target[0])

    pipe = _GradPipeline(pos)
    small_grads = [None] * DEPTH
    after = ()
    for l in reversed(range(DEPTH)):
        wts, small, saved = layers[l]
        tick = lambda deps, new, l=l: pipe.tick(deps, [(nm, l, g) for nm, g in new])
        dact, small_grads[l], after = _layer_bwd(dact, saved, wts, small, tick, after)
    grad_x = dact[None]

    stack = lambda key, rows=None: jnp.stack([small_grads[l][key][0] if rows is None else small_grads[l][key][:rows]
                                              for l in range(DEPTH)])
    local_small = dict(
        g_mix=stack("g_mix"), lower_bounds=stack("lbs"), g_hgrn_out=stack("g_hgrn_out"), w_conv=stack("w_conv", CONV_K),
        sg_ln_g=stack("sg_ln_g"), sg_ln_b=stack("sg_ln_b"), w_sg=jnp.stack([small_grads[l]["w_sg"] for l in range(DEPTH)]),
        b_sg=jnp.stack([small_grads[l]["b_sg"].reshape(SG_GROUPS, SG_CHUNK) for l in range(DEPTH)]),
        g_ffn=stack("g_ffn"), g_final=dg_final[0])
    shapes = [local_small[n].shape for n in SMALL_NAMES] + [(SUBLANES, LANES)]
    summed = _sum_devices(_gather_all("gather_small_grads", _pack([local_small[n] for n in SMALL_NAMES] + [loss_blk]), device,
                                       after=after))
    parts = _unpack(summed, shapes)
    loss = parts[-1][0, 0]
    small_grad = dict(zip(SMALL_NAMES, parts[:-1]))
    small_grad["lower_bounds"] = _lbs_bwd(lower_bounds, small_grad["lower_bounds"])
    small_grad["w_conv"] = lax.dynamic_slice_in_dim(small_grad["w_conv"], pos[0] * conv_cols, conv_cols, axis=2)
    g_flat = _pack([small_grad[n] for n in SMALL_NAMES])
    d_flat, m_flat, v_flat = _adamw_flat(_pack([weights[n] for n in SMALL_NAMES]), g_flat,
                                         _pack([mom1[n] for n in SMALL_NAMES]), _pack([mom2[n] for n in SMALL_NAMES]))
    small_shapes = [weights[n].shape for n in SMALL_NAMES]
    grads = dict(small_grad)
    delta = dict(zip(SMALL_NAMES, _unpack(d_flat, small_shapes)))
    new_m = dict(zip(SMALL_NAMES, _unpack(m_flat, small_shapes)))
    new_v = dict(zip(SMALL_NAMES, _unpack(v_flat, small_shapes)))

    def adam(n, first, layer_grads, into=None, after=()):
        return _adamw_layers("adamw_%s_%d" % (n, first), _as_2d(n, weights[n]), _as_2d(n, mom1[n]), _as_2d(n, mom2[n]),
                             layer_grads, first, into, after)

    done = {}
    token = pipe.tick([summed], [])
    for n in ("w_ff1", "w_ff2"):
        done[n] = adam(n, 0, pipe.reduced[n], after=token)
    token = pipe.tick([done["w_ff2"][1]], [])
    for n in ("w_o", "w_branch"):
        done[n] = adam(n, 0, pipe.reduced[n], after=token)
    rest = adam("w_in", 1, pipe.reduced["w_in"][1:], after=token)
    pipe.tick([rest[1]], [])
    assert not pipe.busy()
    done["w_in"] = adam("w_in", 0, pipe.reduced["w_in"][:1], into=rest)
    for n in BIG_NAMES:
        grads[n], delta[n], new_m[n], new_v[n] = [o.reshape(weights[n].shape) for o in done[n]]

    return (loss, grad_x, *[grads[n] for n in WEIGHT_ORDER], *[delta[n] for n in WEIGHT_ORDER],
            *[new_m[n] for n in WEIGHT_ORDER], *[new_v[n] for n in WEIGHT_ORDER])
```

```python
import numpy as np
import jax
import jax.numpy as jnp
from jax import lax
from jax.experimental import pallas as pl
from jax.experimental.pallas import tpu as pltpu

F32, BF16 = jnp.float32, jnp.bfloat16

D_MODEL = 1024
WIDTH = 512
N_BRANCH = 3
N_HEAD = 4
HEAD = 128
H_CHUNK = 64
CONV_K = 3
SG_CHUNK = 128
SG_GROUPS = 4
D_FF = 4096
DEPTH = 4
N_CHIP = 4
IN_COLS = 9 * WIDTH + N_BRANCH * D_MODEL
GATE_COL0 = 9 * WIDTH
LB_FLOOR = 1e-30
NORM_EPS = 1e-6
LN_EPS = 1e-5
ADAM_LR, ADAM_B1, ADAM_B2, ADAM_EPS, ADAM_WD, ADAM_STEP = 0.001, 0.9, 0.999, 1e-08, 0.01, 10

VMEM_LIMIT_BYTES = 56 * 1024 * 1024
VMEM_BLOCK_BUDGET = 44 * 1024 * 1024
SUBLANES, LANES = 8, 128
ELEMWISE_BLOCK_BYTES = 2 * 1024 * 1024

NN = (((1,), (0,)), ((), ()))
NT = (((1,), (1,)), ((), ()))
TN = (((0,), (0,)), ((), ()))
MESH = pl.DeviceIdType.MESH
ANY = pl.BlockSpec(memory_space=pl.ANY)


def _dot(a, b, dims=NN):
    return lax.dot_general(a.astype(BF16), b.astype(BF16), dims, preferred_element_type=F32)


def _params(n_axes):
    return pltpu.CompilerParams(dimension_semantics=("arbitrary",) * n_axes, vmem_limit_bytes=VMEM_LIMIT_BYTES)


def _row0(part, rows=SUBLANES):
    r = lax.broadcasted_iota(jnp.int32, (rows, part.shape[1]), 0)
    return jnp.where(r == 0, part, 0.0)


def _token_tile(T):
    return min(512, T)


def _matmul(name, a, b, *, dims, grid, a_spec, b_spec, out_specs, out_shapes, acc_shape,
            extra=(), extra_specs=(), epilogue=None, after=()):
    nk = grid[2]
    n_extra, n_out, n_in = len(extra), len(out_shapes), 2 + len(extra) + len(after)
    one_step = nk == 1

    def body(*refs):
        a_ref, b_ref = refs[0], refs[1]
        ex = refs[2:2 + n_extra]
        outs = refs[n_in:n_in + n_out]
        part = _dot(a_ref[...], b_ref[...], dims)

        def finish(total):
            res = epilogue(total, *[e[...] for e in ex]) if epilogue else (total,)
            for o, r in zip(outs, res):
                o[...] = r.astype(o.dtype)

        if one_step:
            finish(part)
            return
        acc = refs[-1]
        kk = pl.program_id(2)

        @pl.when(kk == 0)
        def _():
            acc[...] = part

        @pl.when(kk > 0)
        def _():
            acc[...] += part

        @pl.when(kk == nk - 1)
        def _():
            finish(acc[...])

    return pl.pallas_call(
        body, name=name, grid=grid,
        in_specs=[a_spec, b_spec, *extra_specs, *[ANY] * len(after)], out_specs=list(out_specs),
        out_shape=list(out_shapes), scratch_shapes=[] if one_step else [pltpu.VMEM(acc_shape, F32)],
        compiler_params=_params(3),
    )(a, b, *extra, *after)


def _mm_cols(name, a, w, out_dtypes, epilogue=None, extra=()):
    T, K = a.shape
    N = w.shape[2]
    tm = _token_tile(T)
    blk = pl.BlockSpec((tm, N), lambda j, i, kk: (i, j))
    return _matmul(
        name, a, w, dims=NN, grid=(N_CHIP, T // tm, 1),
        a_spec=pl.BlockSpec((tm, K), lambda j, i, kk: (i, 0)),
        b_spec=pl.BlockSpec((None, K, N), lambda j, i, kk: (j, 0, 0)),
        out_specs=[blk] * len(out_dtypes),
        out_shapes=[jax.ShapeDtypeStruct((T, N_CHIP * N), dt) for dt in out_dtypes],
        acc_shape=(tm, N), extra=extra, extra_specs=[blk] * len(extra), epilogue=epilogue)


def _mm_rows(name, a, w, res, norm_gain=None, after=()):
    T = a.shape[0]
    K, N = N_CHIP * w.shape[1], w.shape[2]
    tm = _token_tile(T)
    blk = pl.BlockSpec((tm, N), lambda i, j, kk: (i, 0))

    def with_norm(acc, r, gain):
        xv = acc + r
        return xv, xv * lax.rsqrt(jnp.mean(xv * xv, axis=-1, keepdims=True) + NORM_EPS) * gain

    normed = norm_gain is not None
    outs = _matmul(
        name, a, w.reshape(K, N), dims=NN, grid=(T // tm, 1, 1),
        a_spec=pl.BlockSpec((tm, K), lambda i, j, kk: (i, 0)),
        b_spec=pl.BlockSpec((K, N), lambda i, j, kk: (0, 0)),
        out_specs=[blk] * (2 if normed else 1),
        out_shapes=[jax.ShapeDtypeStruct((T, N), F32)] + ([jax.ShapeDtypeStruct((T, N), BF16)] if normed else []),
        acc_shape=(tm, N), extra=(res, norm_gain) if normed else (res,),
        extra_specs=[blk] + ([pl.BlockSpec((1, N), lambda i, j, kk: (0, 0))] if normed else []),
        epilogue=with_norm if normed else (lambda acc, r: (acc + r,)), after=after)
    return outs if normed else outs[0]


def _mm_cols_t(name, g, w, out_dtype, epilogue=None, extra=(), after=()):
    T, N = g.shape
    K = N_CHIP * w.shape[1]
    tm = _token_tile(T) if K <= 2 * D_MODEL else _token_tile(T) // 2
    blk = pl.BlockSpec((tm, K), lambda i, j, kk: (i, 0))
    return _matmul(
        name, g, w.reshape(K, N), dims=NT, grid=(T // tm, 1, 1),
        a_spec=pl.BlockSpec((tm, N), lambda i, j, kk: (i, 0)),
        b_spec=pl.BlockSpec((K, N), lambda i, j, kk: (0, 0)),
        out_specs=[blk], out_shapes=[jax.ShapeDtypeStruct((T, K), out_dtype)], acc_shape=(tm, K),
        extra=extra, extra_specs=[blk] * len(extra), epilogue=epilogue, after=after)[0]


def _mm_rows_t(name, g, w):
    T = g.shape[0]
    K, N = w.shape[1], w.shape[2]
    tm = _token_tile(T)
    blk = pl.BlockSpec((tm, K), lambda i, j, kk: (i, 0))
    return _matmul(
        name, g, w, dims=NT, grid=(T // tm, 1, N_CHIP),
        a_spec=pl.BlockSpec((tm, N), lambda i, j, kk: (i, kk)),
        b_spec=pl.BlockSpec((None, K, N), lambda i, j, kk: (kk, 0, 0)),
        out_specs=[blk], out_shapes=[jax.ShapeDtypeStruct((T, K), F32)], acc_shape=(tm, K))[0]


def _mm_wgrad(name, a, g, a_cols, g_cols, a_blocked, g_blocked, after=()):
    T = a.shape[0]
    tt = T
    while tt > LANES and 2 * 2 * tt * (a_cols + g_cols) + (2 if tt == T else 3) * 4 * a_cols * g_cols > VMEM_BLOCK_BUDGET:
        tt //= 2
    return _matmul(
        name, a, g, dims=TN, grid=(N_CHIP, 1, T // tt),
        a_spec=pl.BlockSpec((tt, a_cols), (lambda j, i, kk: (kk, j)) if a_blocked else (lambda j, i, kk: (kk, 0))),
        b_spec=pl.BlockSpec((tt, g_cols), (lambda j, i, kk: (kk, j)) if g_blocked else (lambda j, i, kk: (kk, 0))),
        out_specs=[pl.BlockSpec((None, a_cols, g_cols), lambda j, i, kk: (j, 0, 0))],
        out_shapes=[jax.ShapeDtypeStruct((N_CHIP, a_cols, g_cols), F32)], acc_shape=(a_cols, g_cols), after=after)[0]


def _rms_fwd(name, x, g, after=()):
    T, Dm = x.shape
    tm = min(256, T)

    def body(x_ref, g_ref, *rest):
        xv = x_ref[...]
        r = lax.rsqrt(jnp.mean(xv * xv, axis=-1, keepdims=True) + NORM_EPS)
        rest[-1][...] = (xv * r * g_ref[...]).astype(BF16)

    return pl.pallas_call(
        body, name=name, grid=(T // tm,),
        in_specs=[pl.BlockSpec((tm, Dm), lambda i: (i, 0)), pl.BlockSpec((1, Dm), lambda i: (0, 0))] + [ANY] * len(after),
        out_specs=pl.BlockSpec((tm, Dm), lambda i: (i, 0)),
        out_shape=jax.ShapeDtypeStruct((T, Dm), BF16), compiler_params=_params(1))(x, g, *after)


def _rms_bwd(name, x, g, dh, dres):
    T, Dm = x.shape
    tm = min(256, T)

    def body(x_ref, g_ref, dh_ref, dres_ref, dx_ref, dg_ref):
        xv = x_ref[...]
        r = lax.rsqrt(jnp.mean(xv * xv, axis=-1, keepdims=True) + NORM_EPS)
        xn = xv * r
        dhv = dh_ref[...]
        dxn = dhv * g_ref[...]
        dx_ref[...] = dres_ref[...] + r * (dxn - xn * jnp.mean(dxn * xn, axis=-1, keepdims=True))

        @pl.when(pl.program_id(0) == 0)
        def _():
            dg_ref[...] = jnp.zeros_like(dg_ref)

        dg_ref[...] += _row0(jnp.sum(dhv * xn, axis=0, keepdims=True))

    tile = pl.BlockSpec((tm, Dm), lambda i: (i, 0))
    return pl.pallas_call(
        body, name=name, grid=(T // tm,),
        in_specs=[tile, pl.BlockSpec((1, Dm), lambda i: (0, 0)), tile, tile],
        out_specs=[tile, pl.BlockSpec((SUBLANES, Dm), lambda i: (0, 0))],
        out_shape=[jax.ShapeDtypeStruct((T, Dm), F32), jax.ShapeDtypeStruct((SUBLANES, Dm), F32)],
        compiler_params=_params(1))(x, g, dh, dres)


def _loss_head(x, g, tgt):
    T, Dm = x.shape
    tm = min(256, T)

    def body(x_ref, g_ref, t_ref, loss_ref, dx_ref, dg_ref):
        xv = x_ref[...]
        gv = g_ref[...]
        r = lax.rsqrt(jnp.mean(xv * xv, axis=-1, keepdims=True) + NORM_EPS)
        xn = xv * r
        err = xn * gv - t_ref[...]
        dy = err * (1.0 / Dm)
        dxn = dy * gv
        dx_ref[...] = r * (dxn - xn * jnp.mean(dxn * xn, axis=-1, keepdims=True))

        @pl.when(pl.program_id(0) == 0)
        def _():
            dg_ref[...] = jnp.zeros_like(dg_ref)
            loss_ref[...] = jnp.zeros_like(loss_ref)

        dg_ref[...] += _row0(jnp.sum(dy * xn, axis=0, keepdims=True))
        part = jnp.sum(jnp.sum(err * err, axis=-1, keepdims=True), axis=0, keepdims=True) * (0.5 / Dm)
        loss_ref[...] += jnp.broadcast_to(part, loss_ref.shape)

    tile = pl.BlockSpec((tm, Dm), lambda i: (i, 0))
    return pl.pallas_call(
        body, name="loss_head", grid=(T // tm,),
        in_specs=[tile, pl.BlockSpec((1, Dm), lambda i: (0, 0)), tile],
        out_specs=[pl.BlockSpec((SUBLANES, LANES), lambda i: (0, 0)), tile,
                   pl.BlockSpec((SUBLANES, Dm), lambda i: (0, 0))],
        out_shape=[jax.ShapeDtypeStruct((SUBLANES, LANES), F32), jax.ShapeDtypeStruct((T, Dm), F32),
                   jax.ShapeDtypeStruct((SUBLANES, Dm), F32)],
        compiler_params=_params(1))(x, g, tgt)


def _softmax_rows(lb_ref):
    rows = [lb_ref[pl.ds(i, 1), :] for i in range(DEPTH)]
    mx = rows[0]
    for r in rows[1:]:
        mx = jnp.maximum(mx, r)
    es = [jnp.exp(r - mx) for r in rows]
    tot = es[0]
    for e in es[1:]:
        tot = tot + e
    return [e / tot for e in es]


def _lbs_fwd(lower_bounds):
    def body(lb_ref, out_ref):
        sm = _softmax_rows(lb_ref)
        run = jnp.zeros_like(sm[0])
        out_ref[pl.ds(0, 1), :] = run
        for i in range(1, DEPTH):
            run = run + sm[i]
            out_ref[pl.ds(i, 1), :] = run

    return pl.pallas_call(body, name="lbs_fwd", out_shape=jax.ShapeDtypeStruct(lower_bounds.shape, F32))(lower_bounds)


def _lbs_bwd(lower_bounds, dlbs):
    def body(lb_ref, d_ref, out_ref):
        sm = _softmax_rows(lb_ref)
        dsm = [jnp.zeros_like(sm[0])]
        for i in range(1, DEPTH):
            acc = d_ref[pl.ds(i, 1), :]
            for l in range(i + 1, DEPTH):
                acc = acc + d_ref[pl.ds(l, 1), :]
            dsm.append(acc)
        inner = dsm[0] * sm[0]
        for i in range(1, DEPTH):
            inner = inner + dsm[i] * sm[i]
        for i in range(DEPTH):
            out_ref[pl.ds(i, 1), :] = sm[i] * (dsm[i] - inner)

    return pl.pallas_call(body, name="lbs_bwd", out_shape=jax.ShapeDtypeStruct(lower_bounds.shape, F32))(lower_bounds, dlbs)


N_LEVEL = 6


def _hgrn_consts():
    L = H_CHUNK
    t = np.arange(L)
    blocks = [(t[:, None] >= t[None, :]).astype(np.float32)]
    masks = []
    m = L // 2
    while m >= 1:
        blk, pos = t // (2 * m), t % (2 * m)
        start = blk * 2 * m
        mat = np.zeros((L, L), np.float32)
        for r in range(L):
            if pos[r] >= m:
                mat[r, start[r] + m:r + 1] = 1.0
            else:
                mat[r, r + 1:start[r] + m] = -1.0
        blocks.append(mat)
        masks.append(((blk[:, None] == blk[None, :]) & (pos[:, None] >= m) & (pos[None, :] < m)).astype(np.float32))
        m //= 2
    blocks.append(np.ones((L, L), np.float32))
    return jnp.asarray(np.concatenate(blocks, 0), BF16), jnp.asarray(np.stack(masks), F32)


def _hgrn_core(qraw, fp, lb, sum_mat, mask_ref):
    L = H_CHUNK
    sq = jax.nn.sigmoid(qraw)
    q = qraw * sq
    sneg = jax.nn.sigmoid(-fp)
    log_sig = jnp.minimum(fp, 0.0) - jnp.log1p(jnp.exp(-jnp.abs(fp)))
    a1 = jnp.log(jnp.maximum(lb, LB_FLOOR))
    a2 = jnp.log1p(-lb) + log_sig
    logf = jnp.maximum(a1, a2) + jnp.log1p(jnp.exp(-jnp.abs(a1 - a2)))
    w1 = jnp.exp(a1 - logf)
    w2 = jnp.exp(a2 - logf)
    k = (1.0 - lb) * sneg
    hi = logf.astype(BF16)
    r1 = logf - hi.astype(F32)
    mid = r1.astype(BF16)
    lo = (r1 - mid.astype(F32)).astype(BF16)
    sums = lax.dot_general(sum_mat, jnp.concatenate([hi, mid, lo], axis=1), NN, preferred_element_type=F32)
    sums = sums[:, 0:HEAD] + sums[:, HEAD:2 * HEAD] + sums[:, 2 * HEAD:3 * HEAD]
    b = sums[0:L]
    b_last = sums[(N_LEVEL + 1) * L:(N_LEVEL + 2) * L]
    eye = lax.broadcasted_iota(jnp.int32, (L, L), 0) == lax.broadcasted_iota(jnp.int32, (L, L), 1)
    attn = jnp.where(eye, jnp.sum(q * k, axis=1, keepdims=True), 0.0)
    fa, fb, ea, eb = [], [], [], []
    for l in range(N_LEVEL):
        d = sums[(l + 1) * L:(l + 2) * L]
        e_a = jnp.exp(jnp.minimum(d, 0.0))
        e_b = jnp.exp(jnp.minimum(-d, 0.0))
        a_l, b_l = q * e_a, k * e_b
        attn = attn + mask_ref[l] * _dot(a_l, b_l, NT)
        fa.append(a_l), fb.append(b_l), ea.append(e_a), eb.append(e_b)
    return dict(sq=sq, q=q, sneg=sneg, logf=logf, w1=w1, w2=w2, k=k, b=b, b_last=b_last, attn=attn,
                fa=fa, fb=fb, ea=ea, eb=eb)


def _hgrn_fwd(p, lbrow, gout):
    T = p.shape[0]
    nch = T // H_CHUNK
    sum_mat, masks = _hgrn_consts()

    def body(p_ref, lb_ref, g_ref, m_ref, mask_ref, o_ref, z_ref, st_ref, state):
        @pl.when(pl.program_id(0) == 0)
        def _():
            state[...] = jnp.zeros_like(state)

        sum_m = m_ref[...]
        for h in range(N_HEAD):
            col = lambda part: pl.ds(part * WIDTH + h * HEAD, HEAD)
            hs = pl.ds(h * HEAD, HEAD)
            v = p_ref[:, col(2)]
            c = _hgrn_core(p_ref[:, col(0)], p_ref[:, col(1)], lb_ref[:, hs], sum_m, mask_ref)
            s0 = state[h]
            st_ref[h] = s0
            o = _dot(c["attn"], v) + _dot(c["q"] * jnp.exp(c["b"]), s0, NT)
            k_dec = c["k"] * jnp.exp(c["b_last"] - c["b"])
            decay = jnp.exp(jnp.max(c["b_last"], axis=0, keepdims=True))
            state[h] = s0 * decay + _dot(v, k_dec, TN)
            o_ref[:, hs] = o
            r = lax.rsqrt(jnp.mean(o * o, axis=-1, keepdims=True) + NORM_EPS)
            z_ref[:, hs] = (o * r * g_ref[:, hs] * jax.nn.sigmoid(p_ref[:, col(3)])).astype(BF16)

    full = lambda shape: pl.BlockSpec(shape, lambda c: (0,) * len(shape))
    return pl.pallas_call(
        body, name="hgrn_fwd", grid=(nch,),
        in_specs=[pl.BlockSpec((H_CHUNK, 4 * WIDTH), lambda c: (c, 0)), full((1, WIDTH)), full((1, WIDTH)),
                  full(sum_mat.shape), full(masks.shape)],
        out_specs=[pl.BlockSpec((H_CHUNK, WIDTH), lambda c: (c, 0)),
                   pl.BlockSpec((None, H_CHUNK, WIDTH), lambda c: (0, c, 0)),
                   pl.BlockSpec((None, N_HEAD, HEAD, HEAD), lambda c: (c, 0, 0, 0))],
        out_shape=[jax.ShapeDtypeStruct((T, WIDTH), F32), jax.ShapeDtypeStruct((N_BRANCH, T, WIDTH), BF16),
                   jax.ShapeDtypeStruct((nch, N_HEAD, HEAD, HEAD), F32)],
        scratch_shapes=[pltpu.VMEM((N_HEAD, HEAD, HEAD), F32)], compiler_params=_params(1),
    )(p, lbrow, gout, sum_mat, masks)


def _hgrn_bwd(p, o_saved, dz, states, lbrow, gout, dp, after=()):
    T = p.shape[0]
    nch = T // H_CHUNK
    L = H_CHUNK
    sum_mat, masks = _hgrn_consts()

    def body(p_ref, o_ref, dz_ref, st_ref, lb_ref, g_ref, m_ref, mask_ref, dp_in, *rest):
        del dp_in
        dp_ref, dlb_ref, dg_ref, dstate = rest[len(after):]

        @pl.when(pl.program_id(0) == 0)
        def _():
            dstate[...] = jnp.zeros_like(dstate)
            dlb_ref[...] = jnp.zeros_like(dlb_ref)
            dg_ref[...] = jnp.zeros_like(dg_ref)

        sum_m = m_ref[...]
        for h in range(N_HEAD):
            col = lambda part: pl.ds(part * WIDTH + h * HEAD, HEAD)
            hs = pl.ds(h * HEAD, HEAD)
            qraw, fp, v, go = p_ref[:, col(0)], p_ref[:, col(1)], p_ref[:, col(2)], p_ref[:, col(3)]
            lb, g = lb_ref[:, hs], g_ref[:, hs]
            c = _hgrn_core(qraw, fp, lb, sum_m, mask_ref)
            q, k, b, b_last = c["q"], c["k"], c["b"], c["b_last"]
            s0, ds1 = st_ref[h], dstate[h]
            e_b = jnp.exp(b)
            q_dec = q * e_b
            e_bl = jnp.exp(b_last - b)
            k_dec = k * e_bl
            decay = jnp.exp(jnp.max(b_last, axis=0, keepdims=True))
            o = o_ref[:, hs]
            r = lax.rsqrt(jnp.mean(o * o, axis=-1, keepdims=True) + NORM_EPS)
            n = o * r
            sgo = jax.nn.sigmoid(go)
            dza = dz_ref[:, hs]
            dgo = dza * n * g * sgo * (1.0 - sgo)
            dg_ref[:, hs] += _row0(jnp.sum(dza * n * sgo, axis=0, keepdims=True))
            dn = dza * g * sgo
            do = r * (dn - n * jnp.mean(dn * n, axis=-1, keepdims=True))
            dattn = _dot(do, v, NT)
            dv = _dot(c["attn"], do, TN) + _dot(k_dec, ds1, NT)
            dq_dec = _dot(do, s0)
            dk_dec = _dot(v, ds1)
            ddiag = jnp.sum(do * v, axis=1, keepdims=True)
            dq = dq_dec * e_b + ddiag * k
            dk = dk_dec * e_bl + ddiag * q
            dsums = [dq_dec * q_dec - dk_dec * k_dec]
            for l in range(N_LEVEL):
                dm = mask_ref[l] * dattn
                da = _dot(dm, c["fb"][l])
                db = _dot(dm, c["fa"][l], TN)
                dq = dq + da * c["ea"][l]
                dk = dk + db * c["eb"][l]
                dsums.append(da * c["fa"][l] - db * c["fb"][l])
            dlast = jnp.sum(ds1 * s0, axis=0, keepdims=True) * decay
            dsums.append(dk_dec * k_dec + _row0(dlast, L))
            dlogf = _dot(sum_m, jnp.concatenate(dsums, axis=0), TN)
            dstate[h] = ds1 * decay + _dot(do, q_dec, TN)
            sq, sneg = c["sq"], c["sneg"]
            dqraw = dq * sq * (1.0 + qraw * (1.0 - sq))
            dfp = dlogf * c["w2"] * sneg - dk * (1.0 - lb) * sneg * (1.0 - sneg)
            inv_lb = jnp.where(lb > LB_FLOOR, 1.0 / jnp.maximum(lb, LB_FLOOR), 0.0)
            dlb_tok = dlogf * (c["w1"] * inv_lb - c["w2"] / (1.0 - lb)) - dk * sneg
            dlb_ref[:, hs] += _row0(jnp.sum(dlb_tok, axis=0, keepdims=True))
            dp_ref[:, col(0)] = dqraw.astype(BF16)
            dp_ref[:, col(1)] = dfp.astype(BF16)
            dp_ref[:, col(2)] = dv.astype(BF16)
            dp_ref[:, col(3)] = dgo.astype(BF16)

    full = lambda shape: pl.BlockSpec(shape, lambda c: (0,) * len(shape))
    rev = lambda c: nch - 1 - c
    return pl.pallas_call(
        body, name="hgrn_bwd", grid=(nch,),
        in_specs=[pl.BlockSpec((L, 4 * WIDTH), lambda c: (rev(c), 0)), pl.BlockSpec((L, WIDTH), lambda c: (rev(c), 0)),
                  pl.BlockSpec((None, L, WIDTH), lambda c: (0, rev(c), 0)),
                  pl.BlockSpec((None, N_HEAD, HEAD, HEAD), lambda c: (rev(c), 0, 0, 0)),
                  full((1, WIDTH)), full((1, WIDTH)), full(sum_mat.shape), full(masks.shape), ANY, *[ANY] * len(after)],
        out_specs=[pl.BlockSpec((L, 4 * WIDTH), lambda c: (rev(c), 0)), full((SUBLANES, WIDTH)), full((SUBLANES, WIDTH))],
        out_shape=[jax.ShapeDtypeStruct(dp.shape, dp.dtype), jax.ShapeDtypeStruct((SUBLANES, WIDTH), F32),
                   jax.ShapeDtypeStruct((SUBLANES, WIDTH), F32)],
        scratch_shapes=[pltpu.VMEM((N_HEAD, HEAD, HEAD), F32)], input_output_aliases={8: 0},
        compiler_params=_params(1),
    )(p, o_saved, dz, states, lbrow, gout, sum_mat, masks, dp, *after)


def _shift_down(tile, halo, s):
    tm = tile.shape[0]
    rows = lax.broadcasted_iota(jnp.int32, tile.shape, 0)
    head = jnp.concatenate([pltpu.roll(halo, s, 0), jnp.zeros((tm - SUBLANES, tile.shape[1]), tile.dtype)], axis=0)
    return jnp.where(rows < s, head, pltpu.roll(tile, s, 0))


def _shift_up(tile, halo, s):
    tm = tile.shape[0]
    rows = lax.broadcasted_iota(jnp.int32, tile.shape, 0)
    tail = jnp.concatenate([jnp.zeros((tm - SUBLANES, tile.shape[1]), tile.dtype), pltpu.roll(halo, SUBLANES - s, 0)], axis=0)
    return jnp.where(rows >= tm - s, tail, pltpu.roll(tile, tm - s, 0))


def _conv_fwd(p, w, z):
    T = p.shape[0]
    tm = _token_tile(T)
    per = tm // SUBLANES

    def body(bg_ref, cg_ref, xc_ref, hcg_ref, hxc_ref, w_ref, z_in, z_ref):
        del z_in
        zc = cg_ref[...] * xc_ref[...]
        hz = jnp.where(pl.program_id(0) > 0, hcg_ref[...] * hxc_ref[...], 0.0)
        y = (w_ref[pl.ds(0, 1), :] * _shift_down(zc, hz, 2) + w_ref[pl.ds(1, 1), :] * _shift_down(zc, hz, 1)
             + w_ref[pl.ds(2, 1), :] * zc)
        z_ref[...] = (bg_ref[...] * y).astype(BF16)

    tile = lambda cb: pl.BlockSpec((tm, WIDTH), lambda i: (i, cb))
    prev = lambda cb: pl.BlockSpec((SUBLANES, WIDTH), lambda i: (jnp.maximum(i * per - 1, 0), cb))
    return pl.pallas_call(
        body, name="conv_fwd", grid=(T // tm,),
        in_specs=[tile(4), tile(5), tile(6), prev(5), prev(6), pl.BlockSpec((CONV_K, WIDTH), lambda i: (0, 0)), ANY],
        out_specs=pl.BlockSpec((None, tm, WIDTH), lambda i: (1, i, 0)),
        out_shape=jax.ShapeDtypeStruct(z.shape, z.dtype), input_output_aliases={6: 0}, compiler_params=_params(1),
    )(p, p, p, p, p, w, z)


def _conv_bwd(p, w, dz, dp):
    T = p.shape[0]
    tm = _token_tile(T)
    per = tm // SUBLANES
    last = T // SUBLANES - 1

    def body(bg_ref, cg_ref, xc_ref, hcg_ref, hxc_ref, nbg_ref, dzb_ref, ndzb_ref, w_ref, dp_in, dp_ref, dw_ref, stash):
        del dp_in
        i, jj = pl.program_id(0), pl.program_id(1)

        @pl.when(jnp.logical_and(i == 0, jj == 0))
        def _():
            dw_ref[...] = jnp.zeros_like(dw_ref)

        @pl.when(jj == 0)
        def _():
            cg, xc, bg = cg_ref[...], xc_ref[...], bg_ref[...]
            w0, w1, w2 = w_ref[pl.ds(0, 1), :], w_ref[pl.ds(1, 1), :], w_ref[pl.ds(2, 1), :]
            zc = cg * xc
            hz = jnp.where(i > 0, hcg_ref[...] * hxc_ref[...], 0.0)
            z2, z1 = _shift_down(zc, hz, 2), _shift_down(zc, hz, 1)
            y = w0 * z2 + w1 * z1 + w2 * zc
            dzb = dzb_ref[...]
            dy = dzb * bg
            hdy = jnp.where(i < pl.num_programs(0) - 1, ndzb_ref[...] * nbg_ref[...], 0.0)
            dzc = w2 * dy + w1 * _shift_up(dy, hdy, 1) + w0 * _shift_up(dy, hdy, 2)
            rows = lax.broadcasted_iota(jnp.int32, (SUBLANES, WIDTH), 0)
            colsum = lambda t: jnp.sum(t, axis=0, keepdims=True)
            dw_ref[...] += (jnp.where(rows == 0, colsum(dy * z2), 0.0) + jnp.where(rows == 1, colsum(dy * z1), 0.0)
                            + jnp.where(rows == 2, colsum(dy * zc), 0.0))
            dp_ref[...] = (dzb * y).astype(BF16)
            stash[0] = dzc * xc
            stash[1] = dzc * cg

        @pl.when(jj > 0)
        def _():
            dp_ref[...] = stash[jj - 1].astype(BF16)

    n_tiles = T // tm
    tile = lambda cb: pl.BlockSpec((tm, WIDTH), lambda i, jj: (i, cb))
    prev = lambda cb: pl.BlockSpec((SUBLANES, WIDTH), lambda i, jj: (jnp.maximum(i * per - 1, 0), cb))
    nxt = lambda i: jnp.minimum((i + 1) * per, last)
    return pl.pallas_call(
        body, name="conv_bwd", grid=(n_tiles, 3),
        in_specs=[tile(4), tile(5), tile(6), prev(5), prev(6),
                  pl.BlockSpec((SUBLANES, WIDTH), lambda i, jj: (nxt(i), 4)),
                  pl.BlockSpec((None, tm, WIDTH), lambda i, jj: (1, i, 0)),
                  pl.BlockSpec((None, SUBLANES, WIDTH), lambda i, jj: (1, nxt(i), 0)),
                  pl.BlockSpec((CONV_K, WIDTH), lambda i, jj: (0, 0)), ANY],
        out_specs=[pl.BlockSpec((tm, WIDTH), lambda i, jj: (i, 4 + jj)),
                   pl.BlockSpec((SUBLANES, WIDTH), lambda i, jj: (0, 0))],
        out_shape=[jax.ShapeDtypeStruct(dp.shape, dp.dtype), jax.ShapeDtypeStruct((SUBLANES, WIDTH), F32)],
        scratch_shapes=[pltpu.VMEM((2, tm, WIDTH), F32)], input_output_aliases={9: 0}, compiler_params=_params(2),
    )(p, p, p, p, p, p, dz, dz, w, dp)


GELU_C = float(np.sqrt(2.0 / np.pi))
GELU_A = 0.044715


def _gelu(x):
    th = jnp.tanh(GELU_C * (x + GELU_A * x * x * x))
    return 0.5 * x * (1.0 + th), th


def _gelu_grad(x, th):
    return 0.5 * (1.0 + th) + 0.5 * x * (1.0 - th * th) * GELU_C * (1.0 + 3.0 * GELU_A * x * x)


def _sg_core(u, v, lng, lnb, ws_ref, bs_ref):
    gu, thu = _gelu(u)
    gv, thv = _gelu(v)
    xc = gv - jnp.mean(gv, axis=-1, keepdims=True)
    rs = lax.rsqrt(jnp.mean(xc * xc, axis=-1, keepdims=True) + LN_EPS)
    xh = xc * rs
    vp = xh * lng + lnb
    tril = (lax.broadcasted_iota(jnp.int32, (SG_CHUNK, SG_CHUNK), 0)
            >= lax.broadcasted_iota(jnp.int32, (SG_CHUNK, SG_CHUNK), 1))
    wm = [jnp.where(tril, ws_ref[g], 0.0).astype(BF16) for g in range(SG_GROUPS)]
    gs = lambda t, g: t[:, g * LANES:(g + 1) * LANES]
    sv = jnp.concatenate([_dot(wm[g], gs(vp, g)) + bs_ref[g] for g in range(SG_GROUPS)], axis=1)
    return dict(gu=gu, thu=thu, thv=thv, rs=rs, xh=xh, vp=vp, tril=tril, wm=wm, sv=sv)


def _sg_fwd(p, lng, lnb, ws, bs, z):
    T = p.shape[0]

    def body(u_ref, v_ref, lng_ref, lnb_ref, ws_ref, bs_ref, z_in, z_ref):
        del z_in
        c = _sg_core(u_ref[...], v_ref[...], lng_ref[...], lnb_ref[...], ws_ref, bs_ref)
        z_ref[...] = (c["gu"] * c["sv"]).astype(BF16)

    full = lambda shape: pl.BlockSpec(shape, lambda c: (0,) * len(shape))
    return pl.pallas_call(
        body, name="sg_fwd", grid=(T // SG_CHUNK,),
        in_specs=[pl.BlockSpec((SG_CHUNK, WIDTH), lambda c: (c, 7)), pl.BlockSpec((SG_CHUNK, WIDTH), lambda c: (c, 8)),
                  full((1, WIDTH)), full((1, WIDTH)), full(ws.shape), full(bs.shape), ANY],
        out_specs=pl.BlockSpec((None, SG_CHUNK, WIDTH), lambda c: (2, c, 0)),
        out_shape=jax.ShapeDtypeStruct(z.shape, z.dtype), input_output_aliases={6: 0}, compiler_params=_params(1),
    )(p, p, lng, lnb, ws, bs, z)


def _sg_bwd(p, lng, lnb, ws, bs, dz, dp):
    T = p.shape[0]

    def body(u_ref, v_ref, lng_ref, lnb_ref, ws_ref, bs_ref, dz_ref, dp_in, dp_ref, dws_ref, dbs_ref, dlng_ref, dlnb_ref,
             stash):
        del dp_in
        cidx, jj = pl.program_id(0), pl.program_id(1)

        @pl.when(jnp.logical_and(cidx == 0, jj == 0))
        def _():
            dws_ref[...] = jnp.zeros_like(dws_ref)
            dbs_ref[...] = jnp.zeros_like(dbs_ref)
            dlng_ref[...] = jnp.zeros_like(dlng_ref)
            dlnb_ref[...] = jnp.zeros_like(dlnb_ref)

        @pl.when(jj == 0)
        def _():
            u, v, lng = u_ref[...], v_ref[...], lng_ref[...]
            c = _sg_core(u, v, lng, lnb_ref[...], ws_ref, bs_ref)
            dzc = dz_ref[...]
            gs = lambda t, g: t[:, g * LANES:(g + 1) * LANES]
            dsv = dzc * c["gu"]
            dvp = []
            for g in range(SG_GROUPS):
                dsv_g = gs(dsv, g)
                dws_ref[g] += jnp.where(c["tril"], _dot(dsv_g, gs(c["vp"], g), NT), 0.0)
                dbs_ref[g] += jnp.sum(dsv_g, axis=1, keepdims=True)
                dvp.append(_dot(c["wm"][g], dsv_g, TN))
            dvp = jnp.concatenate(dvp, axis=1)
            xh = c["xh"]
            dlng_ref[...] += _row0(jnp.sum(dvp * xh, axis=0, keepdims=True))
            dlnb_ref[...] += _row0(jnp.sum(dvp, axis=0, keepdims=True))
            dxh = dvp * lng
            dgv = c["rs"] * (dxh - jnp.mean(dxh, axis=-1, keepdims=True) - xh * jnp.mean(dxh * xh, axis=-1, keepdims=True))
            dp_ref[...] = (dzc * c["sv"] * _gelu_grad(u, c["thu"])).astype(BF16)
            stash[...] = dgv * _gelu_grad(v, c["thv"])

        @pl.when(jj == 1)
        def _():
            dp_ref[...] = stash[...].astype(BF16)

    full = lambda shape: pl.BlockSpec(shape, lambda c, jj: (0,) * len(shape))
    return pl.pallas_call(
        body, name="sg_bwd", grid=(T // SG_CHUNK, 2),
        in_specs=[pl.BlockSpec((SG_CHUNK, WIDTH), lambda c, jj: (c, 7)), pl.BlockSpec((SG_CHUNK, WIDTH), lambda c, jj: (c, 8)),
                  full((1, WIDTH)), full((1, WIDTH)), full(ws.shape), full(bs.shape),
                  pl.BlockSpec((None, SG_CHUNK, WIDTH), lambda c, jj: (2, c, 0)), ANY],
        out_specs=[pl.BlockSpec((SG_CHUNK, WIDTH), lambda c, jj: (c, 7 + jj)), full(ws.shape), full(bs.shape),
                   full((SUBLANES, WIDTH)), full((SUBLANES, WIDTH))],
        out_shape=[jax.ShapeDtypeStruct(dp.shape, dp.dtype), jax.ShapeDtypeStruct(ws.shape, F32),
                   jax.ShapeDtypeStruct(bs.shape, F32), jax.ShapeDtypeStruct((SUBLANES, WIDTH), F32),
                   jax.ShapeDtypeStruct((SUBLANES, WIDTH), F32)],
        scratch_shapes=[pltpu.VMEM((SG_CHUNK, WIDTH), F32)], input_output_aliases={7: 0}, compiler_params=_params(2),
    )(p, p, lng, lnb, ws, bs, dz, dp)


BRANCH_COLS = D_MODEL // N_CHIP
GATE_UNIT0 = GATE_COL0 // WIDTH
UNITS = D_MODEL // WIDTH


def _unit_specs(order):
    def spec(which):
        def index(*g):
            _, n, u = order(*g)
            return (2 * u + which, n, 0, 0)
        return pl.BlockSpec((None, None, WIDTH, BRANCH_COLS), index)
    return [spec(0), spec(1)]


def _merge_fwd(z, p, wb):
    T = z.shape[1]
    tm = _token_tile(T)
    order = lambda i, u, n: (i, n, u)

    def body(z_ref, wa_ref, wb_ref, gt_ref, out_ref, acc):
        n = pl.program_id(2)
        zv = z_ref[...]
        y = jnp.concatenate([_dot(zv, wa_ref[...]), _dot(zv, wb_ref[...])], axis=1)
        part = jax.nn.sigmoid(gt_ref[...]) * y

        @pl.when(n == 0)
        def _():
            acc[...] = part

        @pl.when(n > 0)
        def _():
            acc[...] += part

        @pl.when(n == N_BRANCH - 1)
        def _():
            out_ref[...] = acc[...].astype(BF16)

    return pl.pallas_call(
        body, name="merge_fwd", grid=(T // tm, UNITS, N_BRANCH),
        in_specs=[pl.BlockSpec((None, tm, WIDTH), lambda i, u, n: (n, i, 0)), *_unit_specs(order),
                  pl.BlockSpec((tm, WIDTH), lambda i, u, n: (i, GATE_UNIT0 + UNITS * n + u))],
        out_specs=pl.BlockSpec((tm, WIDTH), lambda i, u, n: (i, u)),
        out_shape=jax.ShapeDtypeStruct((T, D_MODEL), BF16),
        scratch_shapes=[pltpu.VMEM((tm, WIDTH), F32)], compiler_params=_params(3))(z, wb, wb, p)


def _merge_bwd(z, p, wb, dmerged):
    T = z.shape[1]
    tm = _token_tile(T)
    order = lambda i, n, u: (i, n, u)

    def body(z_ref, wa_ref, wb_ref, gt_ref, dm_ref, dp_ref, dy_ref, dz_ref):
        u = pl.program_id(2)
        zv, wa, wbv = z_ref[...], wa_ref[...], wb_ref[...]
        y = jnp.concatenate([_dot(zv, wa), _dot(zv, wbv)], axis=1)
        gate = jax.nn.sigmoid(gt_ref[...])
        dm = dm_ref[...]
        dp_ref[...] = (dm * y * gate * (1.0 - gate)).astype(BF16)
        dyv = (dm * gate).astype(BF16)
        dy_ref[...] = dyv
        part = _dot(dyv[:, :BRANCH_COLS], wa, NT) + _dot(dyv[:, BRANCH_COLS:], wbv, NT)

        @pl.when(u == 0)
        def _():
            dz_ref[...] = part

        @pl.when(u > 0)
        def _():
            dz_ref[...] += part

    unit = lambda i, n, u: (i, GATE_UNIT0 + UNITS * n + u)
    return pl.pallas_call(
        body, name="merge_bwd", grid=(T // tm, N_BRANCH, UNITS),
        in_specs=[pl.BlockSpec((None, tm, WIDTH), lambda i, n, u: (n, i, 0)), *_unit_specs(order),
                  pl.BlockSpec((tm, WIDTH), unit), pl.BlockSpec((tm, WIDTH), lambda i, n, u: (i, u))],
        out_specs=[pl.BlockSpec((tm, WIDTH), unit), pl.BlockSpec((None, tm, WIDTH), lambda i, n, u: (n, i, u)),
                   pl.BlockSpec((None, tm, WIDTH), lambda i, n, u: (n, i, 0))],
        out_shape=[jax.ShapeDtypeStruct((T, IN_COLS), BF16), jax.ShapeDtypeStruct((N_BRANCH, T, D_MODEL), BF16),
                   jax.ShapeDtypeStruct((N_BRANCH, T, WIDTH), F32)],
        compiler_params=_params(3))(z, wb, wb, p, dmerged)


def _branch_wgrad(z, dy):
    T = z.shape[1]
    tt = T
    nk = T // tt

    def body(z_ref, dy_ref, out_ref, acc):
        kk = pl.program_id(1)
        part = _dot(z_ref[...], dy_ref[...], TN)

        @pl.when(kk == 0)
        def _():
            acc[...] = part

        @pl.when(kk > 0)
        def _():
            acc[...] += part

        @pl.when(kk == nk - 1)
        def _():
            for k in range(N_CHIP):
                out_ref[k] = acc[:, k * BRANCH_COLS:(k + 1) * BRANCH_COLS]

    return pl.pallas_call(
        body, name="branch_wgrad", grid=(N_BRANCH, nk),
        in_specs=[pl.BlockSpec((None, tt, WIDTH), lambda n, kk: (n, kk, 0)),
                  pl.BlockSpec((None, tt, D_MODEL), lambda n, kk: (n, kk, 0))],
        out_specs=pl.BlockSpec((N_CHIP, None, WIDTH, BRANCH_COLS), lambda n, kk: (0, n, 0, 0)),
        out_shape=jax.ShapeDtypeStruct((N_CHIP, N_BRANCH, WIDTH, BRANCH_COLS), F32),
        scratch_shapes=[pltpu.VMEM((WIDTH, D_MODEL), F32)], compiler_params=_params(2))(z, dy)


def _layer_fwd(x, h, wts, small, next_gain, hook):
    win, wb, wo, w1, w2 = wts
    p = _mm_cols("in_proj", h, win, [F32])[0]
    o_hgrn, z, states = _hgrn_fwd(p, small["lbs"], small["g_hgrn_out"])
    z = _conv_fwd(p, small["w_conv"], z)
    z = _sg_fwd(p, small["sg_ln_g"], small["sg_ln_b"], small["w_sg"], small["b_sg"], z)
    merged = _merge_fwd(z, p, wb)
    x_mid, h2 = _mm_rows("out_proj", merged, wo, x, small["g_ffn"])
    s = _mm_cols("ff1", h2, w1, [BF16], epilogue=lambda acc: (jnp.square(jnp.maximum(acc, 0.0)),))[0]
    if next_gain is None:
        x_out, h_next = _mm_rows("ff2_last", s, w2, x_mid, after=hook([s])), None
    else:
        x_out, h_next = _mm_rows("ff2", s, w2, x_mid, next_gain, after=hook([s]))
    saved = dict(x=x, h=h, p=p, o_hgrn=o_hgrn, z=z, states=states, merged=merged, x_mid=x_mid, h2=h2, s=s)
    return x_out, h_next, saved


def _layer_bwd(dx_out, sv, wts, small, tick, after):
    win, wb, wo, w1, w2 = wts
    g = {}
    da = _mm_cols_t("ff2_dgrad", dx_out, w2, BF16, extra=(sv["s"],), after=after,
                    epilogue=lambda acc, s: (acc * 2.0 * jnp.sqrt(s.astype(F32)),))
    d_ff2 = _mm_wgrad("ff2_wgrad", sv["s"], dx_out, w2.shape[1], D_MODEL, True, False)
    d_ff1 = _mm_wgrad("ff1_wgrad", sv["h2"], da, D_MODEL, w1.shape[2], False, True)
    dh2 = _mm_rows_t("ff1_dgrad", da, w1)
    dx_mid, g["g_ffn"] = _rms_bwd("rms_ffn_bwd", sv["x_mid"], small["g_ffn"], dh2, dx_out)
    after = tick([dx_mid], [("w_ff1", d_ff1), ("w_ff2", d_ff2)])
    dmerged = _mm_cols_t("out_proj_dgrad", dx_mid, wo, F32, after=after)
    d_o = _mm_wgrad("out_proj_wgrad", sv["merged"], dx_mid, wo.shape[1], D_MODEL, True, False)
    dp, dy, dz = _merge_bwd(sv["z"], sv["p"], wb, dmerged)
    d_branch = _branch_wgrad(sv["z"], dy).reshape(N_CHIP, N_BRANCH * WIDTH, BRANCH_COLS)
    after = tick([dp], [("w_branch", d_branch), ("w_o", d_o)])
    dp, g["lbs"], g["g_hgrn_out"] = _hgrn_bwd(sv["p"], sv["o_hgrn"], dz, sv["states"], small["lbs"],
                                              small["g_hgrn_out"], dp, after=after)
    dp, g["w_conv"] = _conv_bwd(sv["p"], small["w_conv"], dz, dp)
    dp, g["w_sg"], g["b_sg"], g["sg_ln_g"], g["sg_ln_b"] = _sg_bwd(
        sv["p"], small["sg_ln_g"], small["sg_ln_b"], small["w_sg"], small["b_sg"], dz, dp)
    after = tick([dp], [])
    d_in = _mm_wgrad("in_proj_wgrad", sv["h"], dp, D_MODEL, win.shape[2], False, True, after=after)
    dh = _mm_rows_t("in_proj_dgrad", dp, win)
    dx, g["g_mix"] = _rms_bwd("rms_mix_bwd", sv["x"], small["g_mix"], dh, dx_mid)
    return dx, g, tick([dx], [("w_in", d_in)])


def _mesh_pos():
    return lax.axis_index("x"), lax.axis_index("y"), lax.axis_index("c")


def _other_chips(x, y):
    return [(1 - x, y), (x, 1 - y), (1 - x, 1 - y)]


def _remote(src, dst, send_sems, recv_sems, k, to):
    return pltpu.make_async_remote_copy(src_ref=src, dst_ref=dst, send_sem=send_sems.at[k], recv_sem=recv_sems.at[k],
                                        device_id=to, device_id_type=MESH)


def _gather_call(name, body, buf, after):
    scratch = [pltpu.SemaphoreType.DMA((7,)), pltpu.SemaphoreType.DMA((7,))]
    return pl.pallas_call(
        body, name=name, in_specs=[ANY] * (1 + len(after)), out_specs=ANY,
        out_shape=jax.ShapeDtypeStruct(buf.shape, buf.dtype), scratch_shapes=scratch, input_output_aliases={0: 0})(buf, *after)


HBM = pl.BlockSpec(memory_space=pltpu.HBM)
SEM = pl.BlockSpec(memory_space=pltpu.SEMAPHORE)
DATAFLOW = pltpu.SideEffectType.DATAFLOW_SIDE_EFFECTING


def _split_start(name, bufs, copies, n_copies, after=()):
    n = len(bufs)

    def body(*refs):
        send_sems, recv_sems = refs[n + len(after)], refs[n + len(after) + 1]
        for cp in copies(refs[:n], send_sems, recv_sems):
            cp.start()
        refs[-1][...] = jnp.zeros_like(refs[-1])

    outs = pl.pallas_call(
        body, name=name,
        out_shape=(pltpu.SemaphoreType.DMA((n_copies,)), pltpu.SemaphoreType.DMA((n_copies,)),
                   *[pltpu.HBM(b.shape, b.dtype) for b in bufs], jax.ShapeDtypeStruct((SUBLANES, LANES), F32)),
        in_specs=[HBM] * n + [ANY] * len(after),
        out_specs=(SEM, SEM, *[HBM] * n, pl.BlockSpec(memory_space=pltpu.VMEM)),
        input_output_aliases={t: 2 + t for t in range(n)},
        compiler_params=pltpu.CompilerParams(has_side_effects=DATAFLOW),
    )(*[pltpu.with_memory_space_constraint(b, pltpu.HBM) for b in bufs], *after)
    return outs[0], outs[1], list(outs[2:2 + n]), outs[-1]


def _split_wait(name, started, copies, after):
    send_sems, recv_sems, bufs, _ = started
    n = len(bufs)

    def body(*refs):
        for cp in copies(refs[:n], refs[n], refs[n + 1]):
            cp.wait_send()
            cp.wait_recv()

    return list(pl.pallas_call(
        body, name=name, out_shape=tuple(pltpu.HBM(b.shape, b.dtype) for b in bufs),
        in_specs=[HBM] * n + [SEM, SEM] + [ANY] * len(after), out_specs=tuple([HBM] * n),
        input_output_aliases={t: t for t in range(n)},
        compiler_params=pltpu.CompilerParams(has_side_effects=DATAFLOW),
    )(*bufs, send_sems, recv_sems, *after))


def _weight_ici_copies(refs, send_sems, recv_sems):
    x, y, c = _mesh_pos()
    out = []
    for t, ref in enumerate(refs):
        rh = ref.shape[1] // 2
        mine = ref.at[2 * x + y, pl.ds(c * rh, rh), :]
        out += [_remote(mine, mine, send_sems, recv_sems, 3 * t + j, (*chip, c)) for j, chip in enumerate(_other_chips(x, y))]
    return out


def _weight_d2d_copies(refs, send_sems, recv_sems):
    x, y, c = _mesh_pos()
    out = []
    for t, ref in enumerate(refs):
        rh = ref.shape[1] // 2
        for j, chip in enumerate(_other_chips(x, y)):
            blk = ref.at[2 * chip[0] + chip[1], pl.ds(c * rh, rh), :]
            out.append(_remote(blk, blk, send_sems, recv_sems, 3 * t + j, (x, y, 1 - c)))
    return out


def _swap_part(refs, send_sems, recv_sems, s0):
    x, y, c = _mesh_pos()
    n = len(refs) // 2
    out = []
    for t in range(n):
        rh = refs[t].shape[1] // 2
        out.append(_remote(refs[t].at[:, pl.ds((1 - c) * rh, rh), :], refs[n + t], send_sems, recv_sems, s0 + t, (x, y, 1 - c)))
    return out


def _exchange_part(refs, send_sems, recv_sems, s0):
    x, y, c = _mesh_pos()
    n = len(refs) // 2
    out = []
    for t in range(n):
        for j, chip in enumerate(_other_chips(x, y)):
            out.append(_remote(refs[t].at[2 * chip[0] + chip[1]], refs[n + t].at[j], send_sems, recv_sems, s0 + 3 * t + j,
                               (*chip, c)))
    return out


def _gather_part(refs, send_sems, recv_sems, s0):
    x, y, c = _mesh_pos()
    return [_remote(ref.at[c], ref.at[c], send_sems, recv_sems, s0 + t, (x, y, 1 - c)) for t, ref in enumerate(refs)]


class _GradPipeline:
    def __init__(self, pos):
        self.pos = pos
        self.groups, self.pending, self.count = [], None, 0
        self.reduced = {n: [None] * DEPTH for n in BIG_NAMES}

    def busy(self):
        return bool(self.groups) or self.pending is not None

    def tick(self, deps, new):
        if self.pending is not None:
            started, copies, owners = self.pending
            bufs = _split_wait("grad_pipe_wait_%d" % self.count, started, copies, after=list(deps))
            for grp, lo, hi in owners:
                grp["bufs"] = bufs[lo:hi]
            self.pending = None
        parts = []
        for grp in list(self.groups):
            n, names = len(grp["names"]), grp["names"]
            if grp["stage"] == "swap":
                pair = [_pair_sum("grad_pair_sum_" + nm, f, r, self.pos)
                        for nm, f, r in zip(names, grp["bufs"][:n], grp["bufs"][n:])]
                grp["own32"] = [p32 for p32, _ in pair]
                landing = [lax.empty((3, *p16.shape[1:]), BF16) for _, p16 in pair]
                grp["stage"] = "exchange"
                parts.append((grp, [p16 for _, p16 in pair] + landing, _exchange_part, 3 * n))
            elif grp["stage"] == "exchange":
                halves = [_chip_sum("grad_chip_sum_" + nm, p32, r, self.pos)
                          for nm, p32, r in zip(names, grp["own32"], grp["bufs"][n:])]
                grp["stage"] = "gather"
                parts.append((grp, halves, _gather_part, n))
            else:
                for nm, b in zip(names, grp["bufs"]):
                    self.reduced[nm][grp["layer"]] = b.reshape(-1, b.shape[-1])
                self.groups.remove(grp)
        if new:
            grp = dict(names=[nm for nm, _, _ in new], layer=new[0][1], stage="swap")
            self.groups.append(grp)
            fulls = [g for _, _, g in new]
            landing = [lax.empty((N_CHIP, g.shape[1] // 2, g.shape[2]), F32) for g in fulls]
            parts.append((grp, fulls + landing, _swap_part, len(fulls)))
        if not parts:
            return ()
        bufs, layout, owners, sems = [], [], [], 0
        for grp, part_bufs, fn, n_sems in parts:
            layout.append((len(bufs), len(bufs) + len(part_bufs), fn, sems))
            owners.append((grp, len(bufs), len(bufs) + len(part_bufs)))
            bufs += part_bufs
            sems += n_sems

        def copies(refs, send_sems, recv_sems):
            out = []
            for lo, hi, fn, s0 in layout:
                out += fn(refs[lo:hi], send_sems, recv_sems, s0)
            return out

        started = _split_start("grad_pipe_start_%d" % self.count, bufs, copies, sems)
        self.pending = (started, copies, owners)
        self.count += 1
        return (started[3],)


def _gather_all(name, block, slot, after=()):
    buf = lax.dynamic_update_slice(jnp.zeros((8, *block.shape), block.dtype), block[None], (slot, 0, 0))

    def body(*refs):
        out_ref, send_sems, recv_sems = refs[1 + len(after):]
        x, y, c = _mesh_pos()
        chips = _other_chips(x, y)
        sibling = (x, y, 1 - c)
        slot_of = lambda px, py, pc: out_ref.at[4 * px + 2 * py + pc]
        started = [_remote(slot_of(x, y, c), slot_of(x, y, c), send_sems, recv_sems, 0, sibling)]
        started += [_remote(slot_of(x, y, c), slot_of(x, y, c), send_sems, recv_sems, 1 + j, (*chip, c))
                    for j, chip in enumerate(chips)]
        for cp in started:
            cp.start()
        for j, chip in enumerate(chips):
            _remote(slot_of(*chip, c), slot_of(*chip, c), send_sems, recv_sems, 1 + j, (*chip, c)).wait_recv()
            fw = _remote(slot_of(*chip, c), slot_of(*chip, c), send_sems, recv_sems, 4 + j, sibling)
            fw.start()
            started.append(fw)
        _remote(slot_of(x, y, 1 - c), slot_of(x, y, 1 - c), send_sems, recv_sems, 0, sibling).wait_recv()
        for j, chip in enumerate(chips):
            _remote(slot_of(*chip, 1 - c), slot_of(*chip, 1 - c), send_sems, recv_sems, 4 + j, sibling).wait_recv()
        for cp in started:
            cp.wait_send()

    return _gather_call(name, body, buf, after)


def _row_tile(rows, cols):
    cap = max(SUBLANES, ELEMWISE_BLOCK_BYTES // (4 * cols))
    tr = rows
    while tr > cap and tr % 2 == 0:
        tr //= 2
    return tr


def _pair_sum(name, grad, recv, pos):
    _, rh, cols = recv.shape
    tr = _row_tile(rh, cols)
    per = rh // tr

    def body(pos_ref, g_ref, r_ref, own_ref, out16_ref):
        s = g_ref[...] + r_ref[...]
        out16_ref[...] = s.astype(BF16)

        @pl.when(pl.program_id(1) == pos_ref[0])
        def _():
            own_ref[...] = s

    blk = pl.BlockSpec((None, tr, cols), lambda i, k, pos_ref: (k, i, 0))
    return pl.pallas_call(
        body, name=name,
        grid_spec=pltpu.PrefetchScalarGridSpec(
            num_scalar_prefetch=1, grid=(per, N_CHIP),
            in_specs=[pl.BlockSpec((None, tr, cols), lambda i, k, pos_ref: (k, pos_ref[1] * per + i, 0)), blk],
            out_specs=[pl.BlockSpec((tr, cols), lambda i, k, pos_ref: (i, 0)), blk]),
        out_shape=[jax.ShapeDtypeStruct((rh, cols), F32), jax.ShapeDtypeStruct(recv.shape, BF16)],
        compiler_params=_params(2))(pos, grad, recv)


def _chip_sum(name, own32, recv, pos):
    rh, cols = own32.shape
    tr = _row_tile(rh, cols)

    def body(pos_ref, own_ref, r_ref, out_ref):
        del pos_ref
        out_ref[...] = ((own_ref[...] + r_ref[0].astype(F32)) + r_ref[1].astype(F32)) + r_ref[2].astype(F32)

    return pl.pallas_call(
        body, name=name,
        grid_spec=pltpu.PrefetchScalarGridSpec(
            num_scalar_prefetch=1, grid=(rh // tr,),
            in_specs=[pl.BlockSpec((tr, cols), lambda i, pos_ref: (i, 0)),
                      pl.BlockSpec((3, tr, cols), lambda i, pos_ref: (0, i, 0))],
            out_specs=pl.BlockSpec((None, tr, cols), lambda i, pos_ref: (pos_ref[1], i, 0))),
        out_shape=jax.ShapeDtypeStruct((2, rh, cols), F32), compiler_params=_params(1))(pos, own32, recv)


def _cast_into_slot(name, w, layer, pos, after=()):
    _, rows, cols = w.shape
    tr = _row_tile(rows, cols)

    def body(pos_ref, w_ref, *rest):
        del pos_ref
        rest[-1][...] = w_ref[...].astype(BF16)

    return pl.pallas_call(
        body, name=name,
        grid_spec=pltpu.PrefetchScalarGridSpec(
            num_scalar_prefetch=1, grid=(rows // tr,),
            in_specs=[pl.BlockSpec((None, tr, cols), lambda i, pos_ref: (layer, i, 0))] + [ANY] * len(after),
            out_specs=pl.BlockSpec((None, tr, cols), lambda i, pos_ref: (pos_ref[0], i, 0))),
        out_shape=jax.ShapeDtypeStruct((N_CHIP, rows, cols), BF16), compiler_params=_params(1))(pos, w, *after)


def _adamw_math(w, g, m, v):
    m = ADAM_B1 * m + (1.0 - ADAM_B1) * g
    v = ADAM_B2 * v + (1.0 - ADAM_B2) * jnp.square(g)
    m_hat = m / (1.0 - ADAM_B1 ** ADAM_STEP)
    v_hat = v / (1.0 - ADAM_B2 ** ADAM_STEP)
    delta = -ADAM_LR * (m_hat / (jnp.sqrt(v_hat) + ADAM_EPS) + ADAM_WD * w)
    return delta, m, v


def _adamw_layers(name, w, m, v, grads, first, into=None, after=()):
    _, rows, cols = w.shape
    tr = _row_tile(rows, cols)
    n_layers = len(grads)

    def body(w_ref, m_ref, v_ref, *rest):
        g_refs, (grad_ref, d_ref, nm_ref, nv_ref) = rest[:n_layers], rest[len(rest) - 4:]
        layer = pl.program_id(0)
        g = g_refs[0][...]
        for l in range(1, n_layers):
            g = jnp.where(layer == l, g_refs[l][...], g)
        grad_ref[...] = g
        d_ref[...], nm_ref[...], nv_ref[...] = _adamw_math(w_ref[...], g, m_ref[...], v_ref[...])

    blk = pl.BlockSpec((None, tr, cols), lambda l, i: (first + l, i, 0))
    g_spec = lambda k: pl.BlockSpec((tr, cols), lambda l, i: (jnp.where(l == k, i, 0), 0))
    passed = list(into or []) + list(after)
    return pl.pallas_call(
        body, name=name, grid=(n_layers, rows // tr),
        in_specs=[blk, blk, blk] + [g_spec(k) for k in range(n_layers)] + [ANY] * len(passed), out_specs=[blk] * 4,
        out_shape=[jax.ShapeDtypeStruct(w.shape, F32)] * 4,
        input_output_aliases={3 + n_layers + t: t for t in range(4)} if into else {},
        compiler_params=_params(2))(w, m, v, *grads, *passed)


def _sum_devices(gathered):
    _, rows, cols = gathered.shape

    def body(g_ref, out_ref):
        s = g_ref[0]
        for d in range(1, 8):
            s = s + g_ref[d]
        out_ref[...] = s

    return pl.pallas_call(body, name="sum_devices", out_shape=jax.ShapeDtypeStruct((rows, cols), F32),
                          compiler_params=pltpu.CompilerParams(vmem_limit_bytes=VMEM_LIMIT_BYTES))(gathered)


def _adamw_flat(w, g, m, v):
    def body(w_ref, g_ref, m_ref, v_ref, d_ref, nm_ref, nv_ref):
        d_ref[...], nm_ref[...], nv_ref[...] = _adamw_math(w_ref[...], g_ref[...], m_ref[...], v_ref[...])

    return pl.pallas_call(body, name="adamw_small", out_shape=[jax.ShapeDtypeStruct(w.shape, F32)] * 3,
                          compiler_params=pltpu.CompilerParams(vmem_limit_bytes=VMEM_LIMIT_BYTES))(w, g, m, v)


SMALL_NAMES = ["g_mix", "lower_bounds", "g_hgrn_out", "w_conv", "sg_ln_g", "sg_ln_b", "w_sg", "b_sg", "g_ffn", "g_final"]
BIG_NAMES = ["w_in", "w_branch", "w_o", "w_ff1", "w_ff2"]
WEIGHT_ORDER = ["w_in", "g_mix", "lower_bounds", "g_hgrn_out", "w_conv", "sg_ln_g", "sg_ln_b", "w_sg", "b_sg", "w_branch",
                "w_o", "g_ffn", "w_ff1", "w_ff2", "g_final"]


def _padded_rows(n):
    return -(-n // SUBLANES) * SUBLANES


def _pack(arrays):
    parts = []
    for a in arrays:
        a = a.reshape(-1, LANES)
        parts.append(jnp.pad(a, ((0, _padded_rows(a.shape[0]) - a.shape[0]), (0, 0))))
    return jnp.concatenate(parts, axis=0)


def _unpack(flat, shapes):
    out, row = [], 0
    for s in shapes:
        n = int(np.prod(s)) // LANES
        out.append(flat[row:row + n].reshape(s))
        row += _padded_rows(n)
    return out


def _as_2d(name, a):
    return a.reshape(DEPTH, N_BRANCH * WIDTH, BRANCH_COLS) if name == "w_branch" else a


def kernel(x, w_in, g_mix, lower_bounds, g_hgrn_out, w_conv, sg_ln_g, sg_ln_b, w_sg, b_sg, w_branch, w_o, g_ffn, w_ff1, w_ff2, g_final, loss_target, m_w_in, m_g_mix, m_lower_bounds, m_g_hgrn_out, m_w_conv, m_sg_ln_g, m_sg_ln_b, m_w_sg, m_b_sg, m_w_branch, m_w_o, m_g_ffn, m_w_ff1, m_w_ff2, m_g_final, v_w_in, v_g_mix, v_lower_bounds, v_g_hgrn_out, v_w_conv, v_sg_ln_g, v_sg_ln_b, v_w_sg, v_b_sg, v_w_branch, v_w_o, v_g_ffn, v_w_ff1, v_w_ff2, v_g_final):
    weights = dict(w_in=w_in, g_mix=g_mix, lower_bounds=lower_bounds, g_hgrn_out=g_hgrn_out, w_conv=w_conv,
                   sg_ln_g=sg_ln_g, sg_ln_b=sg_ln_b, w_sg=w_sg, b_sg=b_sg, w_branch=w_branch, w_o=w_o, g_ffn=g_ffn,
                   w_ff1=w_ff1, w_ff2=w_ff2, g_final=g_final)
    mom1 = dict(w_in=m_w_in, g_mix=m_g_mix, lower_bounds=m_lower_bounds, g_hgrn_out=m_g_hgrn_out, w_conv=m_w_conv,
                sg_ln_g=m_sg_ln_g, sg_ln_b=m_sg_ln_b, w_sg=m_w_sg, b_sg=m_b_sg, w_branch=m_w_branch, w_o=m_w_o,
                g_ffn=m_g_ffn, w_ff1=m_w_ff1, w_ff2=m_w_ff2, g_final=m_g_final)
    mom2 = dict(w_in=v_w_in, g_mix=v_g_mix, lower_bounds=v_lower_bounds, g_hgrn_out=v_g_hgrn_out, w_conv=v_w_conv,
                sg_ln_g=v_sg_ln_g, sg_ln_b=v_sg_ln_b, w_sg=v_w_sg, b_sg=v_b_sg, w_branch=v_w_branch, w_o=v_w_o,
                g_ffn=v_g_ffn, w_ff1=v_w_ff1, w_ff2=v_w_ff2, g_final=v_g_final)
    xi, yi, ci = _mesh_pos()
    pos = jnp.stack([2 * xi + yi, ci]).astype(jnp.int32)
    device = 4 * xi + 2 * yi + ci
    conv_cols = w_conv.shape[2]

    n_big = len(BIG_NAMES)
    flight = {}
    casts = {0: [_cast_into_slot("cast_" + n, _as_2d(n, weights[n]), 0, pos) for n in BIG_NAMES]}
    flight["ici"] = _split_start("weights_ici_start_0", casts[0], _weight_ici_copies, 3 * n_big)
    first = (flight["ici"][3],)
    for l in range(1, DEPTH):
        casts[l] = [_cast_into_slot("cast_" + n, _as_2d(n, weights[n]), l, pos, after=first) for n in BIG_NAMES]
    conv_all = _gather_all("gather_w_conv", w_conv.reshape(DEPTH * CONV_K, conv_cols), device, after=first)
    conv_full = conv_all.reshape(N_CHIP, 2, DEPTH, CONV_K, conv_cols)[:, 0].transpose(1, 2, 0, 3).reshape(DEPTH, CONV_K, WIDTH)
    lbs = _lbs_fwd(lower_bounds)

    act = x[0]
    normed = _rms_fwd("rms_mix", act, g_mix[0:1], after=first)
    layers = []

    def fetch(l, deps):
        landed = _split_wait("weights_ici_wait_%d" % l, flight.pop("ici"), _weight_ici_copies, after=deps)
        token = []
        if l + 1 < DEPTH:
            flight["ici"] = _split_start("weights_ici_start_%d" % (l + 1), casts[l + 1], _weight_ici_copies, 3 * n_big,
                                         after=[landed[0]])
            token = [flight["ici"][3]]
        flight["d2d"] = _split_start("weights_d2d_start_%d" % l, landed, _weight_d2d_copies, 3 * n_big, after=token)
        return (flight["d2d"][3],)

    fetch(0, [normed, lbs, conv_full] + [c for l in range(1, DEPTH) for c in casts[l]])
    for l in range(DEPTH):
        gathered = _split_wait("weights_d2d_wait_%d" % l, flight.pop("d2d"), _weight_d2d_copies, after=[act])
        wts = [gathered[0], gathered[1].reshape(N_CHIP, N_BRANCH, WIDTH, BRANCH_COLS), *gathered[2:]]
        small = dict(g_mix=g_mix[l:l + 1], lbs=lbs[l:l + 1], g_hgrn_out=g_hgrn_out[l:l + 1], w_conv=conv_full[l],
                     sg_ln_g=sg_ln_g[l:l + 1], sg_ln_b=sg_ln_b[l:l + 1], w_sg=w_sg[l],
                     b_sg=b_sg[l].reshape(SG_GROUPS, SG_CHUNK, 1), g_ffn=g_ffn[l:l + 1])
        hook = (lambda deps, l=l: fetch(l + 1, deps)) if l + 1 < DEPTH else (lambda deps: ())
        act, normed, saved = _layer_fwd(act, normed, wts, small, g_mix[l + 1:l + 2] if l + 1 < DEPTH else None, hook)
        layers.append((wts, small, saved))
    loss_blk, dact, dg_final = _loss_head(act, g_final.reshape(1, D_MODEL), loss_target[0])

    pipe = _GradPipeline(pos)
    small_grads = [None] * DEPTH
    after = ()
    for l in reversed(range(DEPTH)):
        wts, small, saved = layers[l]
        tick = lambda deps, new, l=l: pipe.tick(deps, [(nm, l, g) for nm, g in new])
        dact, small_grads[l], after = _layer_bwd(dact, saved, wts, small, tick, after)
    grad_x = dact[None]

    stack = lambda key, rows=None: jnp.stack([small_grads[l][key][0] if rows is None else small_grads[l][key][:rows]
                                              for l in range(DEPTH)])
    local_small = dict(
        g_mix=stack("g_mix"), lower_bounds=stack("lbs"), g_hgrn_out=stack("g_hgrn_out"), w_conv=stack("w_conv", CONV_K),
        sg_ln_g=stack("sg_ln_g"), sg_ln_b=stack("sg_ln_b"), w_sg=jnp.stack([small_grads[l]["w_sg"] for l in range(DEPTH)]),
        b_sg=jnp.stack([small_grads[l]["b_sg"].reshape(SG_GROUPS, SG_CHUNK) for l in range(DEPTH)]),
        g_ffn=stack("g_ffn"), g_final=dg_final[0])
    shapes = [local_small[n].shape for n in SMALL_NAMES] + [(SUBLANES, LANES)]
    summed = _sum_devices(_gather_all("gather_small_grads", _pack([local_small[n] for n in SMALL_NAMES] + [loss_blk]), device,
                                       after=after))
    parts = _unpack(summed, shapes)
    loss = parts[-1][0, 0]
    small_grad = dict(zip(SMALL_NAMES, parts[:-1]))
    small_grad["lower_bounds"] = _lbs_bwd(lower_bounds, small_grad["lower_bounds"])
    small_grad["w_conv"] = lax.dynamic_slice_in_dim(small_grad["w_conv"], pos[0] * conv_cols, conv_cols, axis=2)
    g_flat = _pack([small_grad[n] for n in SMALL_NAMES])
    d_flat, m_flat, v_flat = _adamw_flat(_pack([weights[n] for n in SMALL_NAMES]), g_flat,
                                         _pack([mom1[n] for n in SMALL_NAMES]), _pack([mom2[n] for n in SMALL_NAMES]))
    small_shapes = [weights[n].shape for n in SMALL_NAMES]
    grads = dict(small_grad)
    delta = dict(zip(SMALL_NAMES, _unpack(d_flat, small_shapes)))
    new_m = dict(zip(SMALL_NAMES, _unpack(m_flat, small_shapes)))
    new_v = dict(zip(SMALL_NAMES, _unpack(v_flat, small_shapes)))

    def adam(n, first, layer_grads, into=None, after=()):
        return _adamw_layers("adamw_%s_%d" % (n, first), _as_2d(n, weights[n]), _as_2d(n, mom1[n]), _as_2d(n, mom2[n]),
                             layer_grads, first, into, after)

    done = {}
    token = pipe.tick([summed], [])
    for n in ("w_ff1", "w_ff2"):
        done[n] = adam(n, 0, pipe.reduced[n], after=token)
    token = pipe.tick([done["w_ff2"][1]], [])
    for n in ("w_o", "w_branch"):
        done[n] = adam(n, 0, pipe.reduced[n], after=token)
    rest = adam("w_in", 1, pipe.reduced["w_in"][1:], after=token)
    pipe.tick([rest[1]], [])
    assert not pipe.busy()
    done["w_in"] = adam("w_in", 0, pipe.reduced["w_in"][:1], into=rest)
    for n in BIG_NAMES:
        grads[n], delta[n], new_m[n], new_v[n] = [o.reshape(weights[n].shape) for o in done[n]]

    return (loss, grad_x, *[grads[n] for n in WEIGHT_ORDER], *[delta[n] for n in WEIGHT_ORDER],
            *[new_m[n] for n in WEIGHT_ORDER], *[new_v[n] for n in WEIGHT_ORDER])
```

```python
import numpy as np
import jax
import jax.numpy as jnp
from jax import lax
from jax.experimental import pallas as pl
from jax.experimental.pallas import tpu as pltpu

F32, BF16 = jnp.float32, jnp.bfloat16

D_MODEL = 1024
WIDTH = 512
N_BRANCH = 3
N_HEAD = 4
HEAD = 128
H_CHUNK = 64
CONV_K = 3
SG_CHUNK = 128
SG_GROUPS = 4
D_FF = 4096
DEPTH = 4
N_CHIP = 4
IN_COLS = 9 * WIDTH + N_BRANCH * D_MODEL
GATE_COL0 = 9 * WIDTH
LB_FLOOR = 1e-30
NORM_EPS = 1e-6
LN_EPS = 1e-5
ADAM_LR, ADAM_B1, ADAM_B2, ADAM_EPS, ADAM_WD, ADAM_STEP = 0.001, 0.9, 0.999, 1e-08, 0.01, 10

VMEM_LIMIT_BYTES = 56 * 1024 * 1024
VMEM_BLOCK_BUDGET = 44 * 1024 * 1024
SUBLANES, LANES = 8, 128
ELEMWISE_BLOCK_BYTES = 2 * 1024 * 1024

NN = (((1,), (0,)), ((), ()))
NT = (((1,), (1,)), ((), ()))
TN = (((0,), (0,)), ((), ()))
MESH = pl.DeviceIdType.MESH
ANY = pl.BlockSpec(memory_space=pl.ANY)


def _dot(a, b, dims=NN):
    return lax.dot_general(a.astype(BF16), b.astype(BF16), dims, preferred_element_type=F32)


def _params(n_axes):
    return pltpu.CompilerParams(dimension_semantics=("arbitrary",) * n_axes, vmem_limit_bytes=VMEM_LIMIT_BYTES)


def _row0(part, rows=SUBLANES):
    r = lax.broadcasted_iota(jnp.int32, (rows, part.shape[1]), 0)
    return jnp.where(r == 0, part, 0.0)


def _token_tile(T):
    return min(512, T)


def _matmul(name, a, b, *, dims, grid, a_spec, b_spec, out_specs, out_shapes, acc_shape,
            extra=(), extra_specs=(), epilogue=None, after=()):
    nk = grid[2]
    n_extra, n_out, n_in = len(extra), len(out_shapes), 2 + len(extra) + len(after)
    one_step = nk == 1

    def body(*refs):
        a_ref, b_ref = refs[0], refs[1]
        ex = refs[2:2 + n_extra]
        outs = refs[n_in:n_in + n_out]
        part = _dot(a_ref[...], b_ref[...], dims)

        def finish(total):
            res = epilogue(total, *[e[...] for e in ex]) if epilogue else (total,)
            for o, r in zip(outs, res):
                o[...] = r.astype(o.dtype)

        if one_step:
            finish(part)
            return
        acc = refs[-1]
        kk = pl.program_id(2)

        @pl.when(kk == 0)
        def _():
            acc[...] = part

        @pl.when(kk > 0)
        def _():
            acc[...] += part

        @pl.when(kk == nk - 1)
        def _():
            finish(acc[...])

    return pl.pallas_call(
        body, name=name, grid=grid,
        in_specs=[a_spec, b_spec, *extra_specs, *[ANY] * len(after)], out_specs=list(out_specs),
        out_shape=list(out_shapes), scratch_shapes=[] if one_step else [pltpu.VMEM(acc_shape, F32)],
        compiler_params=_params(3),
    )(a, b, *extra, *after)


def _mm_cols(name, a, w, out_dtypes, epilogue=None, extra=()):
    T, K = a.shape
    N = w.shape[2]
    tm = _token_tile(T)
    blk = pl.BlockSpec((tm, N), lambda j, i, kk: (i, j))
    return _matmul(
        name, a, w, dims=NN, grid=(N_CHIP, T // tm, 1),
        a_spec=pl.BlockSpec((tm, K), lambda j, i, kk: (i, 0)),
        b_spec=pl.BlockSpec((None, K, N), lambda j, i, kk: (j, 0, 0)),
        out_specs=[blk] * len(out_dtypes),
        out_shapes=[jax.ShapeDtypeStruct((T, N_CHIP * N), dt) for dt in out_dtypes],
        acc_shape=(tm, N), extra=extra, extra_specs=[blk] * len(extra), epilogue=epilogue)


def _mm_rows(name, a, w, res, norm_gain=None, after=()):
    T = a.shape[0]
    K, N = N_CHIP * w.shape[1], w.shape[2]
    tm = _token_tile(T)
    blk = pl.BlockSpec((tm, N), lambda i, j, kk: (i, 0))

    def with_norm(acc, r, gain):
        xv = acc + r
        return xv, xv * lax.rsqrt(jnp.mean(xv * xv, axis=-1, keepdims=True) + NORM_EPS) * gain

    normed = norm_gain is not None
    outs = _matmul(
        name, a, w.reshape(K, N), dims=NN, grid=(T // tm, 1, 1),
        a_spec=pl.BlockSpec((tm, K), lambda i, j, kk: (i, 0)),
        b_spec=pl.BlockSpec((K, N), lambda i, j, kk: (0, 0)),
        out_specs=[blk] * (2 if normed else 1),
        out_shapes=[jax.ShapeDtypeStruct((T, N), F32)] + ([jax.ShapeDtypeStruct((T, N), BF16)] if normed else []),
        acc_shape=(tm, N), extra=(res, norm_gain) if normed else (res,),
        extra_specs=[blk] + ([pl.BlockSpec((1, N), lambda i, j, kk: (0, 0))] if normed else []),
        epilogue=with_norm if normed else (lambda acc, r: (acc + r,)), after=after)
    return outs if normed else outs[0]


def _mm_cols_t(name, g, w, out_dtype, epilogue=None, extra=(), after=()):
    T, N = g.shape
    K = N_CHIP * w.shape[1]
    tm = _token_tile(T) if K <= 2 * D_MODEL else _token_tile(T) // 2
    blk = pl.BlockSpec((tm, K), lambda i, j, kk: (i, 0))
    return _matmul(
        name, g, w.reshape(K, N), dims=NT, grid=(T // tm, 1, 1),
        a_spec=pl.BlockSpec((tm, N), lambda i, j, kk: (i, 0)),
        b_spec=pl.BlockSpec((K, N), lambda i, j, kk: (0, 0)),
        out_specs=[blk], out_shapes=[jax.ShapeDtypeStruct((T, K), out_dtype)], acc_shape=(tm, K),
        extra=extra, extra_specs=[blk] * len(extra), epilogue=epilogue, after=after)[0]


def _dgrad_norm_bwd(name, g, w, x, gain, dres):
    T = g.shape[0]
    K, N = w.shape[1], w.shape[2]
    tm = _token_tile(T)

    def body(g_ref, w_ref, x_ref, gain_ref, dres_ref, dx_ref, dgain_ref, acc):
        i, kk = pl.program_id(0), pl.program_id(1)
        part = _dot(g_ref[...], w_ref[...], NT)

        @pl.when(kk == 0)
        def _():
            acc[...] = part

        @pl.when(kk > 0)
        def _():
            acc[...] += part

        @pl.when(kk == N_CHIP - 1)
        def _():
            dhv, xv = acc[...], x_ref[...]
            r = lax.rsqrt(jnp.mean(xv * xv, axis=-1, keepdims=True) + NORM_EPS)
            xn = xv * r
            dxn = dhv * gain_ref[...]
            dx_ref[...] = dres_ref[...] + r * (dxn - xn * jnp.mean(dxn * xn, axis=-1, keepdims=True))

            @pl.when(i == 0)
            def _():
                dgain_ref[...] = jnp.zeros_like(dgain_ref)

            dgain_ref[...] += _row0(jnp.sum(dhv * xn, axis=0, keepdims=True))

    tile = pl.BlockSpec((tm, K), lambda i, kk: (i, 0))
    return pl.pallas_call(
        body, name=name, grid=(T // tm, N_CHIP),
        in_specs=[pl.BlockSpec((tm, N), lambda i, kk: (i, kk)), pl.BlockSpec((None, K, N), lambda i, kk: (kk, 0, 0)),
                  tile, pl.BlockSpec((1, K), lambda i, kk: (0, 0)), tile],
        out_specs=[tile, pl.BlockSpec((SUBLANES, K), lambda i, kk: (0, 0))],
        out_shape=[jax.ShapeDtypeStruct((T, K), F32), jax.ShapeDtypeStruct((SUBLANES, K), F32)],
        scratch_shapes=[pltpu.VMEM((tm, K), F32)], compiler_params=_params(2))(g, w, x, gain, dres)


def _mm_wgrad(name, a, g, a_cols, g_cols, a_blocked, g_blocked, after=()):
    T = a.shape[0]
    tt = T
    while tt > LANES and 2 * 2 * tt * (a_cols + g_cols) + (2 if tt == T else 3) * 4 * a_cols * g_cols > VMEM_BLOCK_BUDGET:
        tt //= 2
    return _matmul(
        name, a, g, dims=TN, grid=(N_CHIP, 1, T // tt),
        a_spec=pl.BlockSpec((tt, a_cols), (lambda j, i, kk: (kk, j)) if a_blocked else (lambda j, i, kk: (kk, 0))),
        b_spec=pl.BlockSpec((tt, g_cols), (lambda j, i, kk: (kk, j)) if g_blocked else (lambda j, i, kk: (kk, 0))),
        out_specs=[pl.BlockSpec((None, a_cols, g_cols), lambda j, i, kk: (j, 0, 0))],
        out_shapes=[jax.ShapeDtypeStruct((N_CHIP, a_cols, g_cols), F32)], acc_shape=(a_cols, g_cols), after=after)[0]


def _rms_fwd(name, x, g, after=()):
    T, Dm = x.shape
    tm = min(256, T)

    def body(x_ref, g_ref, *rest):
        xv = x_ref[...]
        r = lax.rsqrt(jnp.mean(xv * xv, axis=-1, keepdims=True) + NORM_EPS)
        rest[-1][...] = (xv * r * g_ref[...]).astype(BF16)

    return pl.pallas_call(
        body, name=name, grid=(T // tm,),
        in_specs=[pl.BlockSpec((tm, Dm), lambda i: (i, 0)), pl.BlockSpec((1, Dm), lambda i: (0, 0))] + [ANY] * len(after),
        out_specs=pl.BlockSpec((tm, Dm), lambda i: (i, 0)),
        out_shape=jax.ShapeDtypeStruct((T, Dm), BF16), compiler_params=_params(1))(x, g, *after)


def _loss_head(x, g, tgt):
    T, Dm = x.shape
    tm = min(256, T)

    def body(x_ref, g_ref, t_ref, loss_ref, dx_ref, dg_ref):
        xv = x_ref[...]
        gv = g_ref[...]
        r = lax.rsqrt(jnp.mean(xv * xv, axis=-1, keepdims=True) + NORM_EPS)
        xn = xv * r
        err = xn * gv - t_ref[...]
        dy = err * (1.0 / Dm)
        dxn = dy * gv
        dx_ref[...] = r * (dxn - xn * jnp.mean(dxn * xn, axis=-1, keepdims=True))

        @pl.when(pl.program_id(0) == 0)
        def _():
            dg_ref[...] = jnp.zeros_like(dg_ref)
            loss_ref[...] = jnp.zeros_like(loss_ref)

        dg_ref[...] += _row0(jnp.sum(dy * xn, axis=0, keepdims=True))
        part = jnp.sum(jnp.sum(err * err, axis=-1, keepdims=True), axis=0, keepdims=True) * (0.5 / Dm)
        loss_ref[...] += jnp.broadcast_to(part, loss_ref.shape)

    tile = pl.BlockSpec((tm, Dm), lambda i: (i, 0))
    return pl.pallas_call(
        body, name="loss_head", grid=(T // tm,),
        in_specs=[tile, pl.BlockSpec((1, Dm), lambda i: (0, 0)), tile],
        out_specs=[pl.BlockSpec((SUBLANES, LANES), lambda i: (0, 0)), tile,
                   pl.BlockSpec((SUBLANES, Dm), lambda i: (0, 0))],
        out_shape=[jax.ShapeDtypeStruct((SUBLANES, LANES), F32), jax.ShapeDtypeStruct((T, Dm), F32),
                   jax.ShapeDtypeStruct((SUBLANES, Dm), F32)],
        compiler_params=_params(1))(x, g, tgt)


def _softmax_rows(lb_ref):
    rows = [lb_ref[pl.ds(i, 1), :] for i in range(DEPTH)]
    mx = rows[0]
    for r in rows[1:]:
        mx = jnp.maximum(mx, r)
    es = [jnp.exp(r - mx) for r in rows]
    tot = es[0]
    for e in es[1:]:
        tot = tot + e
    return [e / tot for e in es]


def _lbs_fwd(lower_bounds):
    def body(lb_ref, out_ref):
        sm = _softmax_rows(lb_ref)
        run = jnp.zeros_like(sm[0])
        out_ref[pl.ds(0, 1), :] = run
        for i in range(1, DEPTH):
            run = run + sm[i]
            out_ref[pl.ds(i, 1), :] = run

    return pl.pallas_call(body, name="lbs_fwd", out_shape=jax.ShapeDtypeStruct(lower_bounds.shape, F32))(lower_bounds)


def _lbs_bwd(lower_bounds, dlbs):
    def body(lb_ref, d_ref, out_ref):
        sm = _softmax_rows(lb_ref)
        dsm = [jnp.zeros_like(sm[0])]
        for i in range(1, DEPTH):
            acc = d_ref[pl.ds(i, 1), :]
            for l in range(i + 1, DEPTH):
                acc = acc + d_ref[pl.ds(l, 1), :]
            dsm.append(acc)
        inner = dsm[0] * sm[0]
        for i in range(1, DEPTH):
            inner = inner + dsm[i] * sm[i]
        for i in range(DEPTH):
            out_ref[pl.ds(i, 1), :] = sm[i] * (dsm[i] - inner)

    return pl.pallas_call(body, name="lbs_bwd", out_shape=jax.ShapeDtypeStruct(lower_bounds.shape, F32))(lower_bounds, dlbs)


N_LEVEL = 6


def _hgrn_consts():
    L = H_CHUNK
    t = np.arange(L)
    blocks = [(t[:, None] >= t[None, :]).astype(np.float32)]
    masks = []
    m = L // 2
    while m >= 1:
        blk, pos = t // (2 * m), t % (2 * m)
        start = blk * 2 * m
        mat = np.zeros((L, L), np.float32)
        for r in range(L):
            if pos[r] >= m:
                mat[r, start[r] + m:r + 1] = 1.0
            else:
                mat[r, r + 1:start[r] + m] = -1.0
        blocks.append(mat)
        masks.append(((blk[:, None] == blk[None, :]) & (pos[:, None] >= m) & (pos[None, :] < m)).astype(np.float32))
        m //= 2
    blocks.append(np.ones((L, L), np.float32))
    return jnp.asarray(np.concatenate(blocks, 0), BF16), jnp.asarray(np.stack(masks), F32)


def _hgrn_core(qraw, fp, lb, sum_mat, mask_ref):
    L = H_CHUNK
    sq = jax.nn.sigmoid(qraw)
    q = qraw * sq
    sneg = jax.nn.sigmoid(-fp)
    log_sig = jnp.minimum(fp, 0.0) - jnp.log1p(jnp.exp(-jnp.abs(fp)))
    a1 = jnp.log(jnp.maximum(lb, LB_FLOOR))
    a2 = jnp.log1p(-lb) + log_sig
    logf = jnp.maximum(a1, a2) + jnp.log1p(jnp.exp(-jnp.abs(a1 - a2)))
    w1 = jnp.exp(a1 - logf)
    w2 = jnp.exp(a2 - logf)
    k = (1.0 - lb) * sneg
    hi = logf.astype(BF16)
    r1 = logf - hi.astype(F32)
    mid = r1.astype(BF16)
    lo = (r1 - mid.astype(F32)).astype(BF16)
    sums = lax.dot_general(sum_mat, jnp.concatenate([hi, mid, lo], axis=1), NN, preferred_element_type=F32)
    sums = sums[:, 0:HEAD] + sums[:, HEAD:2 * HEAD] + sums[:, 2 * HEAD:3 * HEAD]
    b = sums[0:L]
    b_last = sums[(N_LEVEL + 1) * L:(N_LEVEL + 2) * L]
    eye = lax.broadcasted_iota(jnp.int32, (L, L), 0) == lax.broadcasted_iota(jnp.int32, (L, L), 1)
    attn = jnp.where(eye, jnp.sum(q * k, axis=1, keepdims=True), 0.0)
    fa, fb, ea, eb = [], [], [], []
    for l in range(N_LEVEL):
        d = sums[(l + 1) * L:(l + 2) * L]
        e_a = jnp.exp(jnp.minimum(d, 0.0))
        e_b = jnp.exp(jnp.minimum(-d, 0.0))
        a_l, b_l = q * e_a, k * e_b
        attn = attn + mask_ref[l] * _dot(a_l, b_l, NT)
        fa.append(a_l), fb.append(b_l), ea.append(e_a), eb.append(e_b)
    return dict(sq=sq, q=q, sneg=sneg, logf=logf, w1=w1, w2=w2, k=k, b=b, b_last=b_last, attn=attn,
                fa=fa, fb=fb, ea=ea, eb=eb)


def _hgrn_fwd(p, lbrow, gout):
    T = p.shape[0]
    nch = T // H_CHUNK
    sum_mat, masks = _hgrn_consts()

    def body(p_ref, lb_ref, g_ref, m_ref, mask_ref, o_ref, z_ref, st_ref, state):
        @pl.when(pl.program_id(0) == 0)
        def _():
            state[...] = jnp.zeros_like(state)

        sum_m = m_ref[...]
        for h in range(N_HEAD):
            col = lambda part: pl.ds(part * WIDTH + h * HEAD, HEAD)
            hs = pl.ds(h * HEAD, HEAD)
            v = p_ref[:, col(2)]
            c = _hgrn_core(p_ref[:, col(0)], p_ref[:, col(1)], lb_ref[:, hs], sum_m, mask_ref)
            s0 = state[h]
            st_ref[h] = s0
            o = _dot(c["attn"], v) + _dot(c["q"] * jnp.exp(c["b"]), s0, NT)
            k_dec = c["k"] * jnp.exp(c["b_last"] - c["b"])
            decay = jnp.exp(jnp.max(c["b_last"], axis=0, keepdims=True))
            state[h] = s0 * decay + _dot(v, k_dec, TN)
            o_ref[:, hs] = o
            r = lax.rsqrt(jnp.mean(o * o, axis=-1, keepdims=True) + NORM_EPS)
            z_ref[:, hs] = (o * r * g_ref[:, hs] * jax.nn.sigmoid(p_ref[:, col(3)])).astype(BF16)

    full = lambda shape: pl.BlockSpec(shape, lambda c: (0,) * len(shape))
    return pl.pallas_call(
        body, name="hgrn_fwd", grid=(nch,),
        in_specs=[pl.BlockSpec((H_CHUNK, 4 * WIDTH), lambda c: (c, 0)), full((1, WIDTH)), full((1, WIDTH)),
                  full(sum_mat.shape), full(masks.shape)],
        out_specs=[pl.BlockSpec((H_CHUNK, WIDTH), lambda c: (c, 0)),
                   pl.BlockSpec((None, H_CHUNK, WIDTH), lambda c: (0, c, 0)),
                   pl.BlockSpec((None, N_HEAD, HEAD, HEAD), lambda c: (c, 0, 0, 0))],
        out_shape=[jax.ShapeDtypeStruct((T, WIDTH), F32), jax.ShapeDtypeStruct((N_BRANCH, T, WIDTH), BF16),
                   jax.ShapeDtypeStruct((nch, N_HEAD, HEAD, HEAD), F32)],
        scratch_shapes=[pltpu.VMEM((N_HEAD, HEAD, HEAD), F32)], compiler_params=_params(1),
    )(p, lbrow, gout, sum_mat, masks)


def _hgrn_bwd(p, o_saved, dz, states, lbrow, gout, dp, after=()):
    T = p.shape[0]
    nch = T // H_CHUNK
    L = H_CHUNK
    sum_mat, masks = _hgrn_consts()

    def body(p_ref, o_ref, dz_ref, st_ref, lb_ref, g_ref, m_ref, mask_ref, dp_in, *rest):
        del dp_in
        dp_ref, dlb_ref, dg_ref, dstate = rest[len(after):]

        @pl.when(pl.program_id(0) == 0)
        def _():
            dstate[...] = jnp.zeros_like(dstate)
            dlb_ref[...] = jnp.zeros_like(dlb_ref)
            dg_ref[...] = jnp.zeros_like(dg_ref)

        sum_m = m_ref[...]
        for h in range(N_HEAD):
            col = lambda part: pl.ds(part * WIDTH + h * HEAD, HEAD)
            hs = pl.ds(h * HEAD, HEAD)
            qraw, fp, v, go = p_ref[:, col(0)], p_ref[:, col(1)], p_ref[:, col(2)], p_ref[:, col(3)]
            lb, g = lb_ref[:, hs], g_ref[:, hs]
            c = _hgrn_core(qraw, fp, lb, sum_m, mask_ref)
            q, k, b, b_last = c["q"], c["k"], c["b"], c["b_last"]
            s0, ds1 = st_ref[h], dstate[h]
            e_b = jnp.exp(b)
            q_dec = q * e_b
            e_bl = jnp.exp(b_last - b)
            k_dec = k * e_bl
            decay = jnp.exp(jnp.max(b_last, axis=0, keepdims=True))
            o = o_ref[:, hs]
            r = lax.rsqrt(jnp.mean(o * o, axis=-1, keepdims=True) + NORM_EPS)
            n = o * r
            sgo = jax.nn.sigmoid(go)
            dza = dz_ref[:, hs]
            dgo = dza * n * g * sgo * (1.0 - sgo)
            dg_ref[:, hs] += _row0(jnp.sum(dza * n * sgo, axis=0, keepdims=True))
            dn = dza * g * sgo
            do = r * (dn - n * jnp.mean(dn * n, axis=-1, keepdims=True))
            dattn = _dot(do, v, NT)
            dv = _dot(c["attn"], do, TN) + _dot(k_dec, ds1, NT)
            dq_dec = _dot(do, s0)
            dk_dec = _dot(v, ds1)
            ddiag = jnp.sum(do * v, axis=1, keepdims=True)
            dq = dq_dec * e_b + ddiag * k
            dk = dk_dec * e_bl + ddiag * q
            dsums = [dq_dec * q_dec - dk_dec * k_dec]
            for l in range(N_LEVEL):
                dm = mask_ref[l] * dattn
                da = _dot(dm, c["fb"][l])
                db = _dot(dm, c["fa"][l], TN)
                dq = dq + da * c["ea"][l]
                dk = dk + db * c["eb"][l]
                dsums.append(da * c["fa"][l] - db * c["fb"][l])
            dlast = jnp.sum(ds1 * s0, axis=0, keepdims=True) * decay
            dsums.append(dk_dec * k_dec + _row0(dlast, L))
            dlogf = _dot(sum_m, jnp.concatenate(dsums, axis=0), TN)
            dstate[h] = ds1 * decay + _dot(do, q_dec, TN)
            sq, sneg = c["sq"], c["sneg"]
            dqraw = dq * sq * (1.0 + qraw * (1.0 - sq))
            dfp = dlogf * c["w2"] * sneg - dk * (1.0 - lb) * sneg * (1.0 - sneg)
            inv_lb = jnp.where(lb > LB_FLOOR, 1.0 / jnp.maximum(lb, LB_FLOOR), 0.0)
            dlb_tok = dlogf * (c["w1"] * inv_lb - c["w2"] / (1.0 - lb)) - dk * sneg
            dlb_ref[:, hs] += _row0(jnp.sum(dlb_tok, axis=0, keepdims=True))
            dp_ref[:, col(0)] = dqraw.astype(BF16)
            dp_ref[:, col(1)] = dfp.astype(BF16)
            dp_ref[:, col(2)] = dv.astype(BF16)
            dp_ref[:, col(3)] = dgo.astype(BF16)

    full = lambda shape: pl.BlockSpec(shape, lambda c: (0,) * len(shape))
    rev = lambda c: nch - 1 - c
    return pl.pallas_call(
        body, name="hgrn_bwd", grid=(nch,),
        in_specs=[pl.BlockSpec((L, 4 * WIDTH), lambda c: (rev(c), 0)), pl.BlockSpec((L, WIDTH), lambda c: (rev(c), 0)),
                  pl.BlockSpec((None, L, WIDTH), lambda c: (0, rev(c), 0)),
                  pl.BlockSpec((None, N_HEAD, HEAD, HEAD), lambda c: (rev(c), 0, 0, 0)),
                  full((1, WIDTH)), full((1, WIDTH)), full(sum_mat.shape), full(masks.shape), ANY, *[ANY] * len(after)],
        out_specs=[pl.BlockSpec((L, 4 * WIDTH), lambda c: (rev(c), 0)), full((SUBLANES, WIDTH)), full((SUBLANES, WIDTH))],
        out_shape=[jax.ShapeDtypeStruct(dp.shape, dp.dtype), jax.ShapeDtypeStruct((SUBLANES, WIDTH), F32),
                   jax.ShapeDtypeStruct((SUBLANES, WIDTH), F32)],
        scratch_shapes=[pltpu.VMEM((N_HEAD, HEAD, HEAD), F32)], input_output_aliases={8: 0},
        compiler_params=_params(1),
    )(p, o_saved, dz, states, lbrow, gout, sum_mat, masks, dp, *after)


def _shift_down(tile, halo, s):
    tm = tile.shape[0]
    rows = lax.broadcasted_iota(jnp.int32, tile.shape, 0)
    head = jnp.concatenate([pltpu.roll(halo, s, 0), jnp.zeros((tm - SUBLANES, tile.shape[1]), tile.dtype)], axis=0)
    return jnp.where(rows < s, head, pltpu.roll(tile, s, 0))


def _shift_up(tile, halo, s):
    tm = tile.shape[0]
    rows = lax.broadcasted_iota(jnp.int32, tile.shape, 0)
    tail = jnp.concatenate([jnp.zeros((tm - SUBLANES, tile.shape[1]), tile.dtype), pltpu.roll(halo, SUBLANES - s, 0)], axis=0)
    return jnp.where(rows >= tm - s, tail, pltpu.roll(tile, tm - s, 0))


def _conv_fwd(p, w, z):
    T = p.shape[0]
    tm = _token_tile(T)
    per = tm // SUBLANES

    def body(bg_ref, cg_ref, xc_ref, hcg_ref, hxc_ref, w_ref, z_in, z_ref):
        del z_in
        zc = cg_ref[...] * xc_ref[...]
        hz = jnp.where(pl.program_id(0) > 0, hcg_ref[...] * hxc_ref[...], 0.0)
        y = (w_ref[pl.ds(0, 1), :] * _shift_down(zc, hz, 2) + w_ref[pl.ds(1, 1), :] * _shift_down(zc, hz, 1)
             + w_ref[pl.ds(2, 1), :] * zc)
        z_ref[...] = (bg_ref[...] * y).astype(BF16)

    tile = lambda cb: pl.BlockSpec((tm, WIDTH), lambda i: (i, cb))
    prev = lambda cb: pl.BlockSpec((SUBLANES, WIDTH), lambda i: (jnp.maximum(i * per - 1, 0), cb))
    return pl.pallas_call(
        body, name="conv_fwd", grid=(T // tm,),
        in_specs=[tile(4), tile(5), tile(6), prev(5), prev(6), pl.BlockSpec((CONV_K, WIDTH), lambda i: (0, 0)), ANY],
        out_specs=pl.BlockSpec((None, tm, WIDTH), lambda i: (1, i, 0)),
        out_shape=jax.ShapeDtypeStruct(z.shape, z.dtype), input_output_aliases={6: 0}, compiler_params=_params(1),
    )(p, p, p, p, p, w, z)


def _conv_bwd(p, w, dz, dp):
    T = p.shape[0]
    tm = _token_tile(T)
    per = tm // SUBLANES
    last = T // SUBLANES - 1

    def body(bg_ref, cg_ref, xc_ref, hcg_ref, hxc_ref, nbg_ref, dzb_ref, ndzb_ref, w_ref, dp_in, dp_ref, dw_ref, stash):
        del dp_in
        i, jj = pl.program_id(0), pl.program_id(1)

        @pl.when(jnp.logical_and(i == 0, jj == 0))
        def _():
            dw_ref[...] = jnp.zeros_like(dw_ref)

        @pl.when(jj == 0)
        def _():
            cg, xc, bg = cg_ref[...], xc_ref[...], bg_ref[...]
            w0, w1, w2 = w_ref[pl.ds(0, 1), :], w_ref[pl.ds(1, 1), :], w_ref[pl.ds(2, 1), :]
            zc = cg * xc
            hz = jnp.where(i > 0, hcg_ref[...] * hxc_ref[...], 0.0)
            z2, z1 = _shift_down(zc, hz, 2), _shift_down(zc, hz, 1)
            y = w0 * z2 + w1 * z1 + w2 * zc
            dzb = dzb_ref[...]
            dy = dzb * bg
            hdy = jnp.where(i < pl.num_programs(0) - 1, ndzb_ref[...] * nbg_ref[...], 0.0)
            dzc = w2 * dy + w1 * _shift_up(dy, hdy, 1) + w0 * _shift_up(dy, hdy, 2)
            rows = lax.broadcasted_iota(jnp.int32, (SUBLANES, WIDTH), 0)
            colsum = lambda t: jnp.sum(t, axis=0, keepdims=True)
            dw_ref[...] += (jnp.where(rows == 0, colsum(dy * z2), 0.0) + jnp.where(rows == 1, colsum(dy * z1), 0.0)
                            + jnp.where(rows == 2, colsum(dy * zc), 0.0))
            dp_ref[...] = (dzb * y).astype(BF16)
            stash[0] = dzc * xc
            stash[1] = dzc * cg

        @pl.when(jj > 0)
        def _():
            dp_ref[...] = stash[jj - 1].astype(BF16)

    n_tiles = T // tm
    tile = lambda cb: pl.BlockSpec((tm, WIDTH), lambda i, jj: (i, cb))
    prev = lambda cb: pl.BlockSpec((SUBLANES, WIDTH), lambda i, jj: (jnp.maximum(i * per - 1, 0), cb))
    nxt = lambda i: jnp.minimum((i + 1) * per, last)
    return pl.pallas_call(
        body, name="conv_bwd", grid=(n_tiles, 3),
        in_specs=[tile(4), tile(5), tile(6), prev(5), prev(6),
                  pl.BlockSpec((SUBLANES, WIDTH), lambda i, jj: (nxt(i), 4)),
                  pl.BlockSpec((None, tm, WIDTH), lambda i, jj: (1, i, 0)),
                  pl.BlockSpec((None, SUBLANES, WIDTH), lambda i, jj: (1, nxt(i), 0)),
                  pl.BlockSpec((CONV_K, WIDTH), lambda i, jj: (0, 0)), ANY],
        out_specs=[pl.BlockSpec((tm, WIDTH), lambda i, jj: (i, 4 + jj)),
                   pl.BlockSpec((SUBLANES, WIDTH), lambda i, jj: (0, 0))],
        out_shape=[jax.ShapeDtypeStruct(dp.shape, dp.dtype), jax.ShapeDtypeStruct((SUBLANES, WIDTH), F32)],
        scratch_shapes=[pltpu.VMEM((2, tm, WIDTH), F32)], input_output_aliases={9: 0}, compiler_params=_params(2),
    )(p, p, p, p, p, p, dz, dz, w, dp)


GELU_C = float(np.sqrt(2.0 / np.pi))
GELU_A = 0.044715


def _gelu(x):
    th = jnp.tanh(GELU_C * (x + GELU_A * x * x * x))
    return 0.5 * x * (1.0 + th), th


def _gelu_grad(x, th):
    return 0.5 * (1.0 + th) + 0.5 * x * (1.0 - th * th) * GELU_C * (1.0 + 3.0 * GELU_A * x * x)


def _sg_core(u, v, lng, lnb, ws_ref, bs_ref):
    gu, thu = _gelu(u)
    gv, thv = _gelu(v)
    xc = gv - jnp.mean(gv, axis=-1, keepdims=True)
    rs = lax.rsqrt(jnp.mean(xc * xc, axis=-1, keepdims=True) + LN_EPS)
    xh = xc * rs
    vp = xh * lng + lnb
    tril = (lax.broadcasted_iota(jnp.int32, (SG_CHUNK, SG_CHUNK), 0)
            >= lax.broadcasted_iota(jnp.int32, (SG_CHUNK, SG_CHUNK), 1))
    wm = [jnp.where(tril, ws_ref[g], 0.0).astype(BF16) for g in range(SG_GROUPS)]
    gs = lambda t, g: t[:, g * LANES:(g + 1) * LANES]
    sv = jnp.concatenate([_dot(wm[g], gs(vp, g)) + bs_ref[g] for g in range(SG_GROUPS)], axis=1)
    return dict(gu=gu, thu=thu, thv=thv, rs=rs, xh=xh, vp=vp, tril=tril, wm=wm, sv=sv)


def _sg_fwd(p, lng, lnb, ws, bs, z):
    T = p.shape[0]

    def body(u_ref, v_ref, lng_ref, lnb_ref, ws_ref, bs_ref, z_in, z_ref):
        del z_in
        c = _sg_core(u_ref[...], v_ref[...], lng_ref[...], lnb_ref[...], ws_ref, bs_ref)
        z_ref[...] = (c["gu"] * c["sv"]).astype(BF16)

    full = lambda shape: pl.BlockSpec(shape, lambda c: (0,) * len(shape))
    return pl.pallas_call(
        body, name="sg_fwd", grid=(T // SG_CHUNK,),
        in_specs=[pl.BlockSpec((SG_CHUNK, WIDTH), lambda c: (c, 7)), pl.BlockSpec((SG_CHUNK, WIDTH), lambda c: (c, 8)),
                  full((1, WIDTH)), full((1, WIDTH)), full(ws.shape), full(bs.shape), ANY],
        out_specs=pl.BlockSpec((None, SG_CHUNK, WIDTH), lambda c: (2, c, 0)),
        out_shape=jax.ShapeDtypeStruct(z.shape, z.dtype), input_output_aliases={6: 0}, compiler_params=_params(1),
    )(p, p, lng, lnb, ws, bs, z)


def _sg_bwd(p, lng, lnb, ws, bs, dz, dp):
    T = p.shape[0]

    def body(u_ref, v_ref, lng_ref, lnb_ref, ws_ref, bs_ref, dz_ref, dp_in, dp_ref, dws_ref, dbs_ref, dlng_ref, dlnb_ref,
             stash):
        del dp_in
        cidx, jj = pl.program_id(0), pl.program_id(1)

        @pl.when(jnp.logical_and(cidx == 0, jj == 0))
        def _():
            dws_ref[...] = jnp.zeros_like(dws_ref)
            dbs_ref[...] = jnp.zeros_like(dbs_ref)
            dlng_ref[...] = jnp.zeros_like(dlng_ref)
            dlnb_ref[...] = jnp.zeros_like(dlnb_ref)

        @pl.when(jj == 0)
        def _():
            u, v, lng = u_ref[...], v_ref[...], lng_ref[...]
            c = _sg_core(u, v, lng, lnb_ref[...], ws_ref, bs_ref)
            dzc = dz_ref[...]
            gs = lambda t, g: t[:, g * LANES:(g + 1) * LANES]
            dsv = dzc * c["gu"]
            dvp = []
            for g in range(SG_GROUPS):
                dsv_g = gs(dsv, g)
                dws_ref[g] += jnp.where(c["tril"], _dot(dsv_g, gs(c["vp"], g), NT), 0.0)
                dbs_ref[g] += jnp.sum(dsv_g, axis=1, keepdims=True)
                dvp.append(_dot(c["wm"][g], dsv_g, TN))
            dvp = jnp.concatenate(dvp, axis=1)
            xh = c["xh"]
            dlng_ref[...] += _row0(jnp.sum(dvp * xh, axis=0, keepdims=True))
            dlnb_ref[...] += _row0(jnp.sum(dvp, axis=0, keepdims=True))
            dxh = dvp * lng
            dgv = c["rs"] * (dxh - jnp.mean(dxh, axis=-1, keepdims=True) - xh * jnp.mean(dxh * xh, axis=-1, keepdims=True))
            dp_ref[...] = (dzc * c["sv"] * _gelu_grad(u, c["thu"])).astype(BF16)
            stash[...] = dgv * _gelu_grad(v, c["thv"])

        @pl.when(jj == 1)
        def _():
            dp_ref[...] = stash[...].astype(BF16)

    full = lambda shape: pl.BlockSpec(shape, lambda c, jj: (0,) * len(shape))
    return pl.pallas_call(
        body, name="sg_bwd", grid=(T // SG_CHUNK, 2),
        in_specs=[pl.BlockSpec((SG_CHUNK, WIDTH), lambda c, jj: (c, 7)), pl.BlockSpec((SG_CHUNK, WIDTH), lambda c, jj: (c, 8)),
                  full((1, WIDTH)), full((1, WIDTH)), full(ws.shape), full(bs.shape),
                  pl.BlockSpec((None, SG_CHUNK, WIDTH), lambda c, jj: (2, c, 0)), ANY],
        out_specs=[pl.BlockSpec((SG_CHUNK, WIDTH), lambda c, jj: (c, 7 + jj)), full(ws.shape), full(bs.shape),
                   full((SUBLANES, WIDTH)), full((SUBLANES, WIDTH))],
        out_shape=[jax.ShapeDtypeStruct(dp.shape, dp.dtype), jax.ShapeDtypeStruct(ws.shape, F32),
                   jax.ShapeDtypeStruct(bs.shape, F32), jax.ShapeDtypeStruct((SUBLANES, WIDTH), F32),
                   jax.ShapeDtypeStruct((SUBLANES, WIDTH), F32)],
        scratch_shapes=[pltpu.VMEM((SG_CHUNK, WIDTH), F32)], input_output_aliases={7: 0}, compiler_params=_params(2),
    )(p, p, lng, lnb, ws, bs, dz, dp)


BRANCH_COLS = D_MODEL // N_CHIP
GATE_UNIT0 = GATE_COL0 // WIDTH
UNITS = D_MODEL // WIDTH


def _unit_specs(order):
    def spec(which):
        def index(*g):
            _, n, u = order(*g)
            return (2 * u + which, n, 0, 0)
        return pl.BlockSpec((None, None, WIDTH, BRANCH_COLS), index)
    return [spec(0), spec(1)]


def _merge_fwd(z, p, wb):
    T = z.shape[1]
    tm = _token_tile(T)
    order = lambda i, u, n: (i, n, u)

    def body(z_ref, wa_ref, wb_ref, gt_ref, out_ref, acc):
        n = pl.program_id(2)
        zv = z_ref[...]
        y = jnp.concatenate([_dot(zv, wa_ref[...]), _dot(zv, wb_ref[...])], axis=1)
        part = jax.nn.sigmoid(gt_ref[...]) * y

        @pl.when(n == 0)
        def _():
            acc[...] = part

        @pl.when(n > 0)
        def _():
            acc[...] += part

        @pl.when(n == N_BRANCH - 1)
        def _():
            out_ref[...] = acc[...].astype(BF16)

    return pl.pallas_call(
        body, name="merge_fwd", grid=(T // tm, UNITS, N_BRANCH),
        in_specs=[pl.BlockSpec((None, tm, WIDTH), lambda i, u, n: (n, i, 0)), *_unit_specs(order),
                  pl.BlockSpec((tm, WIDTH), lambda i, u, n: (i, GATE_UNIT0 + UNITS * n + u))],
        out_specs=pl.BlockSpec((tm, WIDTH), lambda i, u, n: (i, u)),
        out_shape=jax.ShapeDtypeStruct((T, D_MODEL), BF16),
        scratch_shapes=[pltpu.VMEM((tm, WIDTH), F32)], compiler_params=_params(3))(z, wb, wb, p)


def _merge_bwd(z, p, wb, dmerged):
    T = z.shape[1]
    tm = _token_tile(T)
    order = lambda i, n, u: (i, n, u)

    def body(z_ref, wa_ref, wb_ref, gt_ref, dm_ref, dp_ref, dy_ref, dz_ref):
        u = pl.program_id(2)
        zv, wa, wbv = z_ref[...], wa_ref[...], wb_ref[...]
        y = jnp.concatenate([_dot(zv, wa), _dot(zv, wbv)], axis=1)
        gate = jax.nn.sigmoid(gt_ref[...])
        dm = dm_ref[...]
        dp_ref[...] = (dm * y * gate * (1.0 - gate)).astype(BF16)
        dyv = (dm * gate).astype(BF16)
        dy_ref[...] = dyv
        part = _dot(dyv[:, :BRANCH_COLS], wa, NT) + _dot(dyv[:, BRANCH_COLS:], wbv, NT)

        @pl.when(u == 0)
        def _():
            dz_ref[...] = part

        @pl.when(u > 0)
        def _():
            dz_ref[...] += part

    unit = lambda i, n, u: (i, GATE_UNIT0 + UNITS * n + u)
    return pl.pallas_call(
        body, name="merge_bwd", grid=(T // tm, N_BRANCH, UNITS),
        in_specs=[pl.BlockSpec((None, tm, WIDTH), lambda i, n, u: (n, i, 0)), *_unit_specs(order),
                  pl.BlockSpec((tm, WIDTH), unit), pl.BlockSpec((tm, WIDTH), lambda i, n, u: (i, u))],
        out_specs=[pl.BlockSpec((tm, WIDTH), unit), pl.BlockSpec((None, tm, WIDTH), lambda i, n, u: (n, i, u)),
                   pl.BlockSpec((None, tm, WIDTH), lambda i, n, u: (n, i, 0))],
        out_shape=[jax.ShapeDtypeStruct((T, IN_COLS), BF16), jax.ShapeDtypeStruct((N_BRANCH, T, D_MODEL), BF16),
                   jax.ShapeDtypeStruct((N_BRANCH, T, WIDTH), F32)],
        compiler_params=_params(3))(z, wb, wb, p, dmerged)


def _branch_wgrad(z, dy):
    T = z.shape[1]
    tt = T
    nk = T // tt

    def body(z_ref, dy_ref, out_ref, acc):
        kk = pl.program_id(1)
        part = _dot(z_ref[...], dy_ref[...], TN)

        @pl.when(kk == 0)
        def _():
            acc[...] = part

        @pl.when(kk > 0)
        def _():
            acc[...] += part

        @pl.when(kk == nk - 1)
        def _():
            for k in range(N_CHIP):
                out_ref[k] = acc[:, k * BRANCH_COLS:(k + 1) * BRANCH_COLS]

    return pl.pallas_call(
        body, name="branch_wgrad", grid=(N_BRANCH, nk),
        in_specs=[pl.BlockSpec((None, tt, WIDTH), lambda n, kk: (n, kk, 0)),
                  pl.BlockSpec((None, tt, D_MODEL), lambda n, kk: (n, kk, 0))],
        out_specs=pl.BlockSpec((N_CHIP, None, WIDTH, BRANCH_COLS), lambda n, kk: (0, n, 0, 0)),
        out_shape=jax.ShapeDtypeStruct((N_CHIP, N_BRANCH, WIDTH, BRANCH_COLS), F32),
        scratch_shapes=[pltpu.VMEM((WIDTH, D_MODEL), F32)], compiler_params=_params(2))(z, dy)


def _layer_fwd(x, h, win, late_weights, small, next_gain, hook):
    p = _mm_cols("in_proj", h, win, [F32])[0]
    o_hgrn, z, states = _hgrn_fwd(p, small["lbs"], small["g_hgrn_out"])
    z = _conv_fwd(p, small["w_conv"], z)
    z = _sg_fwd(p, small["sg_ln_g"], small["sg_ln_b"], small["w_sg"], small["b_sg"], z)
    wb, wo, w1, w2 = late_weights([z])
    wb = wb.reshape(N_CHIP, N_BRANCH, WIDTH, BRANCH_COLS)
    merged = _merge_fwd(z, p, wb)
    x_mid, h2 = _mm_rows("out_proj", merged, wo, x, small["g_ffn"])
    s = _mm_cols("ff1", h2, w1, [BF16], epilogue=lambda acc: (jnp.square(jnp.maximum(acc, 0.0)),))[0]
    if next_gain is None:
        x_out, h_next = _mm_rows("ff2_last", s, w2, x_mid, after=hook([s])), None
    else:
        x_out, h_next = _mm_rows("ff2", s, w2, x_mid, next_gain, after=hook([s]))
    saved = dict(x=x, h=h, p=p, o_hgrn=o_hgrn, z=z, states=states, merged=merged, x_mid=x_mid, h2=h2, s=s)
    return x_out, h_next, saved, [win, wb, wo, w1, w2]


def _layer_bwd(dx_out, sv, wts, small, tick, after):
    win, wb, wo, w1, w2 = wts
    g = {}
    da = _mm_cols_t("ff2_dgrad", dx_out, w2, BF16, extra=(sv["s"],), after=after,
                    epilogue=lambda acc, s: (acc * 2.0 * jnp.sqrt(s.astype(F32)),))
    d_ff2 = _mm_wgrad("ff2_wgrad", sv["s"], dx_out, w2.shape[1], D_MODEL, True, False)
    d_ff1 = _mm_wgrad("ff1_wgrad", sv["h2"], da, D_MODEL, w1.shape[2], False, True)
    dx_mid, g["g_ffn"] = _dgrad_norm_bwd("ff1_dgrad", da, w1, sv["x_mid"], small["g_ffn"], dx_out)
    after = tick([dx_mid], [("w_ff1", d_ff1), ("w_ff2", d_ff2)])
    dmerged = _mm_cols_t("out_proj_dgrad", dx_mid, wo, F32, after=after)
    d_o = _mm_wgrad("out_proj_wgrad", sv["merged"], dx_mid, wo.shape[1], D_MODEL, True, False)
    dp, dy, dz = _merge_bwd(sv["z"], sv["p"], wb, dmerged)
    d_branch = _branch_wgrad(sv["z"], dy).reshape(N_CHIP, N_BRANCH * WIDTH, BRANCH_COLS)
    after = tick([dp], [("w_branch", d_branch), ("w_o", d_o)])
    dp, g["lbs"], g["g_hgrn_out"] = _hgrn_bwd(sv["p"], sv["o_hgrn"], dz, sv["states"], small["lbs"],
                                              small["g_hgrn_out"], dp, after=after)
    dp, g["w_conv"] = _conv_bwd(sv["p"], small["w_conv"], dz, dp)
    dp, g["w_sg"], g["b_sg"], g["sg_ln_g"], g["sg_ln_b"] = _sg_bwd(
        sv["p"], small["sg_ln_g"], small["sg_ln_b"], small["w_sg"], small["b_sg"], dz, dp)
    after = tick([dp], [])
    d_in = _mm_wgrad("in_proj_wgrad", sv["h"], dp, D_MODEL, win.shape[2], False, True, after=after)
    dx, g["g_mix"] = _dgrad_norm_bwd("in_proj_dgrad", dp, win, sv["x"], small["g_mix"], dx_mid)
    return dx, g, tick([dx], [("w_in", d_in)])


def _mesh_pos():
    return lax.axis_index("x"), lax.axis_index("y"), lax.axis_index("c")


def _other_chips(x, y):
    return [(1 - x, y), (x, 1 - y), (1 - x, 1 - y)]


def _remote(src, dst, send_sems, recv_sems, k, to):
    return pltpu.make_async_remote_copy(src_ref=src, dst_ref=dst, send_sem=send_sems.at[k], recv_sem=recv_sems.at[k],
                                        device_id=to, device_id_type=MESH)


def _gather_call(name, body, buf, after):
    scratch = [pltpu.SemaphoreType.DMA((7,)), pltpu.SemaphoreType.DMA((7,))]
    return pl.pallas_call(
        body, name=name, in_specs=[ANY] * (1 + len(after)), out_specs=ANY,
        out_shape=jax.ShapeDtypeStruct(buf.shape, buf.dtype), scratch_shapes=scratch, input_output_aliases={0: 0})(buf, *after)


HBM = pl.BlockSpec(memory_space=pltpu.HBM)
SEM = pl.BlockSpec(memory_space=pltpu.SEMAPHORE)
DATAFLOW = pltpu.SideEffectType.DATAFLOW_SIDE_EFFECTING


def _split_start(name, bufs, copies, n_copies, after=()):
    n = len(bufs)

    def body(*refs):
        send_sems, recv_sems = refs[n + len(after)], refs[n + len(after) + 1]
        for cp in copies(refs[:n], send_sems, recv_sems):
            cp.start()
        refs[-1][...] = jnp.zeros_like(refs[-1])

    outs = pl.pallas_call(
        body, name=name,
        out_shape=(pltpu.SemaphoreType.DMA((n_copies,)), pltpu.SemaphoreType.DMA((n_copies,)),
                   *[pltpu.HBM(b.shape, b.dtype) for b in bufs], jax.ShapeDtypeStruct((SUBLANES, LANES), F32)),
        in_specs=[HBM] * n + [ANY] * len(after),
        out_specs=(SEM, SEM, *[HBM] * n, pl.BlockSpec(memory_space=pltpu.VMEM)),
        input_output_aliases={t: 2 + t for t in range(n)},
        compiler_params=pltpu.CompilerParams(has_side_effects=DATAFLOW),
    )(*[pltpu.with_memory_space_constraint(b, pltpu.HBM) for b in bufs], *after)
    return outs[0], outs[1], list(outs[2:2 + n]), outs[-1]


def _split_wait(name, started, copies, after):
    send_sems, recv_sems, bufs, _ = started
    n = len(bufs)

    def body(*refs):
        for cp in copies(refs[:n], refs[n], refs[n + 1]):
            cp.wait_send()
            cp.wait_recv()

    return list(pl.pallas_call(
        body, name=name, out_shape=tuple(pltpu.HBM(b.shape, b.dtype) for b in bufs),
        in_specs=[HBM] * n + [SEM, SEM] + [ANY] * len(after), out_specs=tuple([HBM] * n),
        input_output_aliases={t: t for t in range(n)},
        compiler_params=pltpu.CompilerParams(has_side_effects=DATAFLOW),
    )(*bufs, send_sems, recv_sems, *after))


def _weight_ici_copies(refs, send_sems, recv_sems):
    x, y, c = _mesh_pos()
    out = []
    for t, ref in enumerate(refs):
        rh = ref.shape[1] // 2
        mine = ref.at[2 * x + y, pl.ds(c * rh, rh), :]
        out += [_remote(mine, mine, send_sems, recv_sems, 3 * t + j, (*chip, c)) for j, chip in enumerate(_other_chips(x, y))]
    return out


def _weight_d2d_copies(refs, send_sems, recv_sems):
    x, y, c = _mesh_pos()
    out = []
    for t, ref in enumerate(refs):
        rh = ref.shape[1] // 2
        for j, chip in enumerate(_other_chips(x, y)):
            blk = ref.at[2 * chip[0] + chip[1], pl.ds(c * rh, rh), :]
            out.append(_remote(blk, blk, send_sems, recv_sems, 3 * t + j, (x, y, 1 - c)))
    return out


def _swap_part(refs, send_sems, recv_sems, s0):
    x, y, c = _mesh_pos()
    n = len(refs) // 2
    out = []
    for t in range(n):
        rh = refs[t].shape[1] // 2
        out.append(_remote(refs[t].at[:, pl.ds((1 - c) * rh, rh), :], refs[n + t], send_sems, recv_sems, s0 + t, (x, y, 1 - c)))
    return out


def _exchange_part(refs, send_sems, recv_sems, s0):
    x, y, c = _mesh_pos()
    n = len(refs) // 2
    out = []
    for t in range(n):
        for j, chip in enumerate(_other_chips(x, y)):
            out.append(_remote(refs[t].at[2 * chip[0] + chip[1]], refs[n + t].at[j], send_sems, recv_sems, s0 + 3 * t + j,
                               (*chip, c)))
    return out


def _gather_part(refs, send_sems, recv_sems, s0):
    x, y, c = _mesh_pos()
    return [_remote(ref.at[c], ref.at[c], send_sems, recv_sems, s0 + t, (x, y, 1 - c)) for t, ref in enumerate(refs)]


class _GradPipeline:
    def __init__(self, pos):
        self.pos = pos
        self.groups, self.pending, self.count = [], None, 0
        self.reduced = {n: [None] * DEPTH for n in BIG_NAMES}

    def busy(self):
        return bool(self.groups) or self.pending is not None

    def tick(self, deps, new):
        if self.pending is not None:
            started, copies, owners = self.pending
            bufs = _split_wait("grad_pipe_wait_%d" % self.count, started, copies, after=list(deps))
            for grp, lo, hi in owners:
                grp["bufs"] = bufs[lo:hi]
            self.pending = None
        parts = []
        for grp in list(self.groups):
            n, names = len(grp["names"]), grp["names"]
            if grp["stage"] == "swap":
                pair = [_pair_sum("grad_pair_sum_" + nm, f, r, self.pos)
                        for nm, f, r in zip(names, grp["bufs"][:n], grp["bufs"][n:])]
                grp["own32"] = [p32 for p32, _ in pair]
                landing = [lax.empty((3, *p16.shape[1:]), BF16) for _, p16 in pair]
                grp["stage"] = "exchange"
                parts.append((grp, [p16 for _, p16 in pair] + landing, _exchange_part, 3 * n))
            elif grp["stage"] == "exchange":
                halves = [_chip_sum("grad_chip_sum_" + nm, p32, r, self.pos)
                          for nm, p32, r in zip(names, grp["own32"], grp["bufs"][n:])]
                grp["stage"] = "gather"
                parts.append((grp, halves, _gather_part, n))
            else:
                for nm, b in zip(names, grp["bufs"]):
                    self.reduced[nm][grp["layer"]] = b.reshape(-1, b.shape[-1])
                self.groups.remove(grp)
        if new:
            grp = dict(names=[nm for nm, _, _ in new], layer=new[0][1], stage="swap")
            self.groups.append(grp)
            fulls = [g for _, _, g in new]
            landing = [lax.empty((N_CHIP, g.shape[1] // 2, g.shape[2]), F32) for g in fulls]
            parts.append((grp, fulls + landing, _swap_part, len(fulls)))
        if not parts:
            return ()
        bufs, layout, owners, sems = [], [], [], 0
        for grp, part_bufs, fn, n_sems in parts:
            layout.append((len(bufs), len(bufs) + len(part_bufs), fn, sems))
            owners.append((grp, len(bufs), len(bufs) + len(part_bufs)))
            bufs += part_bufs
            sems += n_sems

        def copies(refs, send_sems, recv_sems):
            out = []
            for lo, hi, fn, s0 in layout:
                out += fn(refs[lo:hi], send_sems, recv_sems, s0)
            return out

        started = _split_start("grad_pipe_start_%d" % self.count, bufs, copies, sems)
        self.pending = (started, copies, owners)
        self.count += 1
        return (started[3],)


def _gather_all(name, block, slot, after=()):
    buf = lax.dynamic_update_slice(jnp.zeros((8, *block.shape), block.dtype), block[None], (slot, 0, 0))

    def body(*refs):
        out_ref, send_sems, recv_sems = refs[1 + len(after):]
        x, y, c = _mesh_pos()
        chips = _other_chips(x, y)
        sibling = (x, y, 1 - c)
        slot_of = lambda px, py, pc: out_ref.at[4 * px + 2 * py + pc]
        started = [_remote(slot_of(x, y, c), slot_of(x, y, c), send_sems, recv_sems, 0, sibling)]
        started += [_remote(slot_of(x, y, c), slot_of(x, y, c), send_sems, recv_sems, 1 + j, (*chip, c))
                    for j, chip in enumerate(chips)]
        for cp in started:
            cp.start()
        for j, chip in enumerate(chips):
            _remote(slot_of(*chip, c), slot_of(*chip, c), send_sems, recv_sems, 1 + j, (*chip, c)).wait_recv()
            fw = _remote(slot_of(*chip, c), slot_of(*chip, c), send_sems, recv_sems, 4 + j, sibling)
            fw.start()
            started.append(fw)
        _remote(slot_of(x, y, 1 - c), slot_of(x, y, 1 - c), send_sems, recv_sems, 0, sibling).wait_recv()
        for j, chip in enumerate(chips):
            _remote(slot_of(*chip, 1 - c), slot_of(*chip, 1 - c), send_sems, recv_sems, 4 + j, sibling).wait_recv()
        for cp in started:
            cp.wait_send()

    return _gather_call(name, body, buf, after)


def _row_tile(rows, cols):
    cap = max(SUBLANES, ELEMWISE_BLOCK_BYTES // (4 * cols))
    tr = rows
    while tr > cap and tr % 2 == 0:
        tr //= 2
    return tr


def _pair_sum(name, grad, recv, pos):
    _, rh, cols = recv.shape
    tr = _row_tile(rh, cols)
    per = rh // tr

    def body(pos_ref, g_ref, r_ref, own_ref, out16_ref):
        s = g_ref[...] + r_ref[...]
        out16_ref[...] = s.astype(BF16)

        @pl.when(pl.program_id(1) == pos_ref[0])
        def _():
            own_ref[...] = s

    blk = pl.BlockSpec((None, tr, cols), lambda i, k, pos_ref: (k, i, 0))
    return pl.pallas_call(
        body, name=name,
        grid_spec=pltpu.PrefetchScalarGridSpec(
            num_scalar_prefetch=1, grid=(per, N_CHIP),
            in_specs=[pl.BlockSpec((None, tr, cols), lambda i, k, pos_ref: (k, pos_ref[1] * per + i, 0)), blk],
            out_specs=[pl.BlockSpec((tr, cols), lambda i, k, pos_ref: (i, 0)), blk]),
        out_shape=[jax.ShapeDtypeStruct((rh, cols), F32), jax.ShapeDtypeStruct(recv.shape, BF16)],
        compiler_params=_params(2))(pos, grad, recv)


def _chip_sum(name, own32, recv, pos):
    rh, cols = own32.shape
    tr = _row_tile(rh, cols)

    def body(pos_ref, own_ref, r_ref, out_ref):
        del pos_ref
        out_ref[...] = ((own_ref[...] + r_ref[0].astype(F32)) + r_ref[1].astype(F32)) + r_ref[2].astype(F32)

    return pl.pallas_call(
        body, name=name,
        grid_spec=pltpu.PrefetchScalarGridSpec(
            num_scalar_prefetch=1, grid=(rh // tr,),
            in_specs=[pl.BlockSpec((tr, cols), lambda i, pos_ref: (i, 0)),
                      pl.BlockSpec((3, tr, cols), lambda i, pos_ref: (0, i, 0))],
            out_specs=pl.BlockSpec((None, tr, cols), lambda i, pos_ref: (pos_ref[1], i, 0))),
        out_shape=jax.ShapeDtypeStruct((2, rh, cols), F32), compiler_params=_params(1))(pos, own32, recv)


def _cast_into_slot(name, w, layer, pos, after=()):
    _, rows, cols = w.shape
    tr = _row_tile(rows, cols)

    def body(pos_ref, w_ref, *rest):
        del pos_ref
        rest[-1][...] = w_ref[...].astype(BF16)

    return pl.pallas_call(
        body, name=name,
        grid_spec=pltpu.PrefetchScalarGridSpec(
            num_scalar_prefetch=1, grid=(rows // tr,),
            in_specs=[pl.BlockSpec((None, tr, cols), lambda i, pos_ref: (layer, i, 0))] + [ANY] * len(after),
            out_specs=pl.BlockSpec((None, tr, cols), lambda i, pos_ref: (pos_ref[0], i, 0))),
        out_shape=jax.ShapeDtypeStruct((N_CHIP, rows, cols), BF16), compiler_params=_params(1))(pos, w, *after)


def _adamw_math(w, g, m, v):
    m = ADAM_B1 * m + (1.0 - ADAM_B1) * g
    v = ADAM_B2 * v + (1.0 - ADAM_B2) * jnp.square(g)
    m_hat = m / (1.0 - ADAM_B1 ** ADAM_STEP)
    v_hat = v / (1.0 - ADAM_B2 ** ADAM_STEP)
    delta = -ADAM_LR * (m_hat / (jnp.sqrt(v_hat) + ADAM_EPS) + ADAM_WD * w)
    return delta, m, v


def _adamw_layers(name, w, m, v, grads, first, into=None, after=()):
    _, rows, cols = w.shape
    tr = _row_tile(rows, cols)
    n_layers = len(grads)

    def body(w_ref, m_ref, v_ref, *rest):
        g_refs, (grad_ref, d_ref, nm_ref, nv_ref) = rest[:n_layers], rest[len(rest) - 4:]
        layer = pl.program_id(0)
        g = g_refs[0][...]
        for l in range(1, n_layers):
            g = jnp.where(layer == l, g_refs[l][...], g)
        grad_ref[...] = g
        d_ref[...], nm_ref[...], nv_ref[...] = _adamw_math(w_ref[...], g, m_ref[...], v_ref[...])

    blk = pl.BlockSpec((None, tr, cols), lambda l, i: (first + l, i, 0))
    g_spec = lambda k: pl.BlockSpec((tr, cols), lambda l, i: (jnp.where(l == k, i, 0), 0))
    passed = list(into or []) + list(after)
    return pl.pallas_call(
        body, name=name, grid=(n_layers, rows // tr),
        in_specs=[blk, blk, blk] + [g_spec(k) for k in range(n_layers)] + [ANY] * len(passed), out_specs=[blk] * 4,
        out_shape=[jax.ShapeDtypeStruct(w.shape, F32)] * 4,
        input_output_aliases={3 + n_layers + t: t for t in range(4)} if into else {},
        compiler_params=_params(2))(w, m, v, *grads, *passed)


def _sum_devices(gathered):
    _, rows, cols = gathered.shape

    def body(g_ref, out_ref):
        s = g_ref[0]
        for d in range(1, 8):
            s = s + g_ref[d]
        out_ref[...] = s

    return pl.pallas_call(body, name="sum_devices", out_shape=jax.ShapeDtypeStruct((rows, cols), F32),
                          compiler_params=pltpu.CompilerParams(vmem_limit_bytes=VMEM_LIMIT_BYTES))(gathered)


def _adamw_flat(w, g, m, v):
    def body(w_ref, g_ref, m_ref, v_ref, d_ref, nm_ref, nv_ref):
        d_ref[...], nm_ref[...], nv_ref[...] = _adamw_math(w_ref[...], g_ref[...], m_ref[...], v_ref[...])

    return pl.pallas_call(body, name="adamw_small", out_shape=[jax.ShapeDtypeStruct(w.shape, F32)] * 3,
                          compiler_params=pltpu.CompilerParams(vmem_limit_bytes=VMEM_LIMIT_BYTES))(w, g, m, v)


SMALL_NAMES = ["g_mix", "lower_bounds", "g_hgrn_out", "w_conv", "sg_ln_g", "sg_ln_b", "w_sg", "b_sg", "g_ffn", "g_final"]
BIG_NAMES = ["w_in", "w_branch", "w_o", "w_ff1", "w_ff2"]
WEIGHT_ORDER = ["w_in", "g_mix", "lower_bounds", "g_hgrn_out", "w_conv", "sg_ln_g", "sg_ln_b", "w_sg", "b_sg", "w_branch",
                "w_o", "g_ffn", "w_ff1", "w_ff2", "g_final"]


def _padded_rows(n):
    return -(-n // SUBLANES) * SUBLANES


def _pack(arrays):
    parts = []
    for a in arrays:
        a = a.reshape(-1, LANES)
        parts.append(jnp.pad(a, ((0, _padded_rows(a.shape[0]) - a.shape[0]), (0, 0))))
    return jnp.concatenate(parts, axis=0)


def _unpack(flat, shapes):
    out, row = [], 0
    for s in shapes:
        n = int(np.prod(s)) // LANES
        out.append(flat[row:row + n].reshape(s))
        row += _padded_rows(n)
    return out


def _as_2d(name, a):
    return a.reshape(DEPTH, N_BRANCH * WIDTH, BRANCH_COLS) if name == "w_branch" else a


def kernel(x, w_in, g_mix, lower_bounds, g_hgrn_out, w_conv, sg_ln_g, sg_ln_b, w_sg, b_sg, w_branch, w_o, g_ffn, w_ff1, w_ff2, g_final, loss_target, m_w_in, m_g_mix, m_lower_bounds, m_g_hgrn_out, m_w_conv, m_sg_ln_g, m_sg_ln_b, m_w_sg, m_b_sg, m_w_branch, m_w_o, m_g_ffn, m_w_ff1, m_w_ff2, m_g_final, v_w_in, v_g_mix, v_lower_bounds, v_g_hgrn_out, v_w_conv, v_sg_ln_g, v_sg_ln_b, v_w_sg, v_b_sg, v_w_branch, v_w_o, v_g_ffn, v_w_ff1, v_w_ff2, v_g_final):
    weights = dict(w_in=w_in, g_mix=g_mix, lower_bounds=lower_bounds, g_hgrn_out=g_hgrn_out, w_conv=w_conv,
                   sg_ln_g=sg_ln_g, sg_ln_b=sg_ln_b, w_sg=w_sg, b_sg=b_sg, w_branch=w_branch, w_o=w_o, g_ffn=g_ffn,
                   w_ff1=w_ff1, w_ff2=w_ff2, g_final=g_final)
    mom1 = dict(w_in=m_w_in, g_mix=m_g_mix, lower_bounds=m_lower_bounds, g_hgrn_out=m_g_hgrn_out, w_conv=m_w_conv,
                sg_ln_g=m_sg_ln_g, sg_ln_b=m_sg_ln_b, w_sg=m_w_sg, b_sg=m_b_sg, w_branch=m_w_branch, w_o=m_w_o,
                g_ffn=m_g_ffn, w_ff1=m_w_ff1, w_ff2=m_w_ff2, g_final=m_g_final)
    mom2 = dict(w_in=v_w_in, g_mix=v_g_mix, lower_bounds=v_lower_bounds, g_hgrn_out=v_g_hgrn_out, w_conv=v_w_conv,
                sg_ln_g=v_sg_ln_g, sg_ln_b=v_sg_ln_b, w_sg=v_w_sg, b_sg=v_b_sg, w_branch=v_w_branch, w_o=v_w_o,
                g_ffn=v_g_ffn, w_ff1=v_w_ff1, w_ff2=v_w_ff2, g_final=v_g_final)
    xi, yi, ci = _mesh_pos()
    pos = jnp.stack([2 * xi + yi, ci]).astype(jnp.int32)
    device = 4 * xi + 2 * yi + ci
    conv_cols = w_conv.shape[2]

    n_big = len(BIG_NAMES)
    flight = {}
    casts = {0: [_cast_into_slot("cast_" + n, _as_2d(n, weights[n]), 0, pos) for n in BIG_NAMES]}
    early = _split_start("weights_ici_start_0_w_in", casts[0][:1], _weight_ici_copies, 3)
    flight["ici"] = _split_start("weights_ici_start_0", casts[0][1:], _weight_ici_copies, 3 * (n_big - 1), after=[early[3]])
    first = (flight["ici"][3],)
    for l in range(1, DEPTH):
        casts[l] = [_cast_into_slot("cast_" + n, _as_2d(n, weights[n]), l, pos, after=first) for n in BIG_NAMES]
    conv_all = _gather_all("gather_w_conv", w_conv.reshape(DEPTH * CONV_K, conv_cols), device, after=first)
    conv_full = conv_all.reshape(N_CHIP, 2, DEPTH, CONV_K, conv_cols)[:, 0].transpose(1, 2, 0, 3).reshape(DEPTH, CONV_K, WIDTH)
    lbs = _lbs_fwd(lower_bounds)

    act = x[0]
    normed = _rms_fwd("rms_mix", act, g_mix[0:1], after=first)
    layers = []

    def fetch(l, deps):
        landed = _split_wait("weights_ici_wait_%d" % l, flight.pop("ici"), _weight_ici_copies, after=deps)
        token = []
        if l + 1 < DEPTH:
            flight["ici"] = _split_start("weights_ici_start_%d" % (l + 1), casts[l + 1], _weight_ici_copies, 3 * n_big,
                                         after=[landed[0]])
            token = [flight["ici"][3]]
        flight["d2d"] = _split_start("weights_d2d_start_%d" % l, landed, _weight_d2d_copies, 3 * len(landed), after=token)
        return (flight["d2d"][3],)

    landed = _split_wait("weights_ici_wait_0_w_in", early, _weight_ici_copies,
                         after=[normed, lbs, conv_full] + [c for l in range(1, DEPTH) for c in casts[l]])
    early = _split_start("weights_d2d_start_0_w_in", landed, _weight_d2d_copies, 3)
    gathered = _split_wait("weights_d2d_wait_0_w_in", early, _weight_d2d_copies, after=[])

    def rest_of_layer_0(deps):
        fetch(0, deps)
        return _split_wait("weights_d2d_wait_0", flight.pop("d2d"), _weight_d2d_copies, after=[])

    for l in range(DEPTH):
        if l > 0:
            gathered = _split_wait("weights_d2d_wait_%d" % l, flight.pop("d2d"), _weight_d2d_copies, after=[act])
        late_weights = rest_of_layer_0 if l == 0 else (lambda deps, rest=gathered[1:]: rest)
        small = dict(g_mix=g_mix[l:l + 1], lbs=lbs[l:l + 1], g_hgrn_out=g_hgrn_out[l:l + 1], w_conv=conv_full[l],
                     sg_ln_g=sg_ln_g[l:l + 1], sg_ln_b=sg_ln_b[l:l + 1], w_sg=w_sg[l],
                     b_sg=b_sg[l].reshape(SG_GROUPS, SG_CHUNK, 1), g_ffn=g_ffn[l:l + 1])
        hook = (lambda deps, l=l: fetch(l + 1, deps)) if l + 1 < DEPTH else (lambda deps: ())
        act, normed, saved, wts = _layer_fwd(act, normed, gathered[0], late_weights, small,
                                             g_mix[l + 1:l + 2] if l + 1 < DEPTH else None, hook)
        layers.append((wts, small, saved))
    loss_blk, dact, dg_final = _loss_head(act, g_final.reshape(1, D_MODEL), loss_target[0])

    pipe = _GradPipeline(pos)
    small_grads = [None] * DEPTH
    after = ()
    for l in reversed(range(DEPTH)):
        wts, small, saved = layers[l]
        tick = lambda deps, new, l=l: pipe.tick(deps, [(nm, l, g) for nm, g in new])
        dact, small_grads[l], after = _layer_bwd(dact, saved, wts, small, tick, after)
    grad_x = dact[None]

    stack = lambda key, rows=None: jnp.stack([small_grads[l][key][0] if rows is None else small_grads[l][key][:rows]
                                              for l in range(DEPTH)])
    local_small = dict(
        g_mix=stack("g_mix"), lower_bounds=stack("lbs"), g_hgrn_out=stack("g_hgrn_out"), w_conv=stack("w_conv", CONV_K),
        sg_ln_g=stack("sg_ln_g"), sg_ln_b=stack("sg_ln_b"), w_sg=jnp.stack([small_grads[l]["w_sg"] for l in range(DEPTH)]),
        b_sg=jnp.stack([small_grads[l]["b_sg"].reshape(SG_GROUPS, SG_CHUNK) for l in range(DEPTH)]),
        g_ffn=stack("g_ffn"), g_final=dg_final[0])
    shapes = [local_small[n].shape for n in SMALL_NAMES] + [(SUBLANES, LANES)]
    summed = _sum_devices(_gather_all("gather_small_grads", _pack([local_small[n] for n in SMALL_NAMES] + [loss_blk]), device,
                                       after=after))
    parts = _unpack(summed, shapes)
    loss = parts[-1][0, 0]
    small_grad = dict(zip(SMALL_NAMES, parts[:-1]))
    small_grad["lower_bounds"] = _lbs_bwd(lower_bounds, small_grad["lower_bounds"])
    small_grad["w_conv"] = lax.dynamic_slice_in_dim(small_grad["w_conv"], pos[0] * conv_cols, conv_cols, axis=2)
    g_flat = _pack([small_grad[n] for n in SMALL_NAMES])
    d_flat, m_flat, v_flat = _adamw_flat(_pack([weights[n] for n in SMALL_NAMES]), g_flat,
                                         _pack([mom1[n] for n in SMALL_NAMES]), _pack([mom2[n] for n in SMALL_NAMES]))
    small_shapes = [weights[n].shape for n in SMALL_NAMES]
    grads = dict(small_grad)
    delta = dict(zip(SMALL_NAMES, _unpack(d_flat, small_shapes)))
    new_m = dict(zip(SMALL_NAMES, _unpack(m_flat, small_shapes)))
    new_v = dict(zip(SMALL_NAMES, _unpack(v_flat, small_shapes)))

    def adam(n, first, layer_grads, into=None, after=()):
        return _adamw_layers("adamw_%s_%d" % (n, first), _as_2d(n, weights[n]), _as_2d(n, mom1[n]), _as_2d(n, mom2[n]),
                             layer_grads, first, into, after)

    done = {}
    token = pipe.tick([summed], [])
    for n in ("w_ff1", "w_ff2"):
        done[n] = adam(n, 0, pipe.reduced[n], after=token)
    token = pipe.tick([done["w_ff2"][1]], [])
    for n in ("w_o", "w_branch"):
        done[n] = adam(n, 0, pipe.reduced[n], after=token)
    rest = adam("w_in", 1, pipe.reduced["w_in"][1:], after=token)
    pipe.tick([rest[1]], [])
    assert not pipe.busy()
    done["w_in"] = adam("w_in", 0, pipe.reduced["w_in"][:1], into=rest)
    for n in BIG_NAMES:
        grads[n], delta[n], new_m[n], new_v[n] = [o.reshape(weights[n].shape) for o in done[n]]

    return (loss, grad_x, *[grads[n] for n in WEIGHT_ORDER], *[delta[n] for n in WEIGHT_ORDER],
            *[new_m[n] for n in WEIGHT_ORDER], *[new_v[n] for n in WEIGHT_ORDER])
```

```python
import numpy as np
import jax
import jax.numpy as jnp
from jax import lax
from jax.experimental import pallas as pl
from jax.experimental.pallas import tpu as pltpu

F32, BF16 = jnp.float32, jnp.bfloat16

D_MODEL = 1024
WIDTH = 512
N_BRANCH = 3
N_HEAD = 4
HEAD = 128
H_CHUNK = 64
CONV_K = 3
SG_CHUNK = 128
SG_GROUPS = 4
D_FF = 4096
DEPTH = 4
N_CHIP = 4
IN_COLS = 9 * WIDTH + N_BRANCH * D_MODEL
GATE_COL0 = 9 * WIDTH
LB_FLOOR = 1e-30
NORM_EPS = 1e-6
LN_EPS = 1e-5
ADAM_LR, ADAM_B1, ADAM_B2, ADAM_EPS, ADAM_WD, ADAM_STEP = 0.001, 0.9, 0.999, 1e-08, 0.01, 10

VMEM_LIMIT_BYTES = 56 * 1024 * 1024
VMEM_BLOCK_BUDGET = 44 * 1024 * 1024
SUBLANES, LANES = 8, 128
ELEMWISE_BLOCK_BYTES = 2 * 1024 * 1024

NN = (((1,), (0,)), ((), ()))
NT = (((1,), (1,)), ((), ()))
TN = (((0,), (0,)), ((), ()))
MESH = pl.DeviceIdType.MESH
ANY = pl.BlockSpec(memory_space=pl.ANY)


def _dot(a, b, dims=NN):
    return lax.dot_general(a.astype(BF16), b.astype(BF16), dims, preferred_element_type=F32)


def _params(n_axes):
    return pltpu.CompilerParams(dimension_semantics=("arbitrary",) * n_axes, vmem_limit_bytes=VMEM_LIMIT_BYTES)


def _row0(part, rows=SUBLANES):
    r = lax.broadcasted_iota(jnp.int32, (rows, part.shape[1]), 0)
    return jnp.where(r == 0, part, 0.0)


def _token_tile(T):
    return min(512, T)


def _matmul(name, a, b, *, dims, grid, a_spec, b_spec, out_specs, out_shapes, acc_shape,
            extra=(), extra_specs=(), epilogue=None, after=()):
    nk = grid[2]
    n_extra, n_out, n_in = len(extra), len(out_shapes), 2 + len(extra) + len(after)
    one_step = nk == 1

    def body(*refs):
        a_ref, b_ref = refs[0], refs[1]
        ex = refs[2:2 + n_extra]
        outs = refs[n_in:n_in + n_out]
        part = _dot(a_ref[...], b_ref[...], dims)

        def finish(total):
            res = epilogue(total, *[e[...] for e in ex]) if epilogue else (total,)
            for o, r in zip(outs, res):
                o[...] = r.astype(o.dtype)

        if one_step:
            finish(part)
            return
        acc = refs[-1]
        kk = pl.program_id(2)

        @pl.when(kk == 0)
        def _():
            acc[...] = part

        @pl.when(kk > 0)
        def _():
            acc[...] += part

        @pl.when(kk == nk - 1)
        def _():
            finish(acc[...])

    return pl.pallas_call(
        body, name=name, grid=grid,
        in_specs=[a_spec, b_spec, *extra_specs, *[ANY] * len(after)], out_specs=list(out_specs),
        out_shape=list(out_shapes), scratch_shapes=[] if one_step else [pltpu.VMEM(acc_shape, F32)],
        compiler_params=_params(3),
    )(a, b, *extra, *after)


def _mm_cols(name, a, w, out_dtypes, epilogue=None, extra=()):
    T, K = a.shape
    N = w.shape[2]
    tm = _token_tile(T)
    blk = pl.BlockSpec((tm, N), lambda j, i, kk: (i, j))
    return _matmul(
        name, a, w, dims=NN, grid=(N_CHIP, T // tm, 1),
        a_spec=pl.BlockSpec((tm, K), lambda j, i, kk: (i, 0)),
        b_spec=pl.BlockSpec((None, K, N), lambda j, i, kk: (j, 0, 0)),
        out_specs=[blk] * len(out_dtypes),
        out_shapes=[jax.ShapeDtypeStruct((T, N_CHIP * N), dt) for dt in out_dtypes],
        acc_shape=(tm, N), extra=extra, extra_specs=[blk] * len(extra), epilogue=epilogue)


def _mm_rows(name, a, w, res, norm_gain=None, after=()):
    T = a.shape[0]
    K, N = N_CHIP * w.shape[1], w.shape[2]
    tm = _token_tile(T)
    blk = pl.BlockSpec((tm, N), lambda i, j, kk: (i, 0))

    def with_norm(acc, r, gain):
        xv = acc + r
        return xv, xv * lax.rsqrt(jnp.mean(xv * xv, axis=-1, keepdims=True) + NORM_EPS) * gain

    normed = norm_gain is not None
    outs = _matmul(
        name, a, w.reshape(K, N), dims=NN, grid=(T // tm, 1, 1),
        a_spec=pl.BlockSpec((tm, K), lambda i, j, kk: (i, 0)),
        b_spec=pl.BlockSpec((K, N), lambda i, j, kk: (0, 0)),
        out_specs=[blk] * (2 if normed else 1),
        out_shapes=[jax.ShapeDtypeStruct((T, N), F32)] + ([jax.ShapeDtypeStruct((T, N), BF16)] if normed else []),
        acc_shape=(tm, N), extra=(res, norm_gain) if normed else (res,),
        extra_specs=[blk] + ([pl.BlockSpec((1, N), lambda i, j, kk: (0, 0))] if normed else []),
        epilogue=with_norm if normed else (lambda acc, r: (acc + r,)), after=after)
    return outs if normed else outs[0]


def _mm_cols_t(name, g, w, out_dtype, epilogue=None, extra=(), after=()):
    T, N = g.shape
    K = N_CHIP * w.shape[1]
    tm = _token_tile(T) if K <= 2 * D_MODEL else _token_tile(T) // 2
    blk = pl.BlockSpec((tm, K), lambda i, j, kk: (i, 0))
    return _matmul(
        name, g, w.reshape(K, N), dims=NT, grid=(T // tm, 1, 1),
        a_spec=pl.BlockSpec((tm, N), lambda i, j, kk: (i, 0)),
        b_spec=pl.BlockSpec((K, N), lambda i, j, kk: (0, 0)),
        out_specs=[blk], out_shapes=[jax.ShapeDtypeStruct((T, K), out_dtype)], acc_shape=(tm, K),
        extra=extra, extra_specs=[blk] * len(extra), epilogue=epilogue, after=after)[0]


def _dgrad_norm_bwd(name, g, w, x, gain, dres):
    T = g.shape[0]
    K, N = w.shape[1], w.shape[2]
    tm = _token_tile(T)

    def body(g_ref, w_ref, x_ref, gain_ref, dres_ref, dx_ref, dgain_ref, acc):
        i, kk = pl.program_id(0), pl.program_id(1)
        part = _dot(g_ref[...], w_ref[...], NT)

        @pl.when(kk == 0)
        def _():
            acc[...] = part

        @pl.when(kk > 0)
        def _():
            acc[...] += part

        @pl.when(kk == N_CHIP - 1)
        def _():
            dhv, xv = acc[...], x_ref[...]
            r = lax.rsqrt(jnp.mean(xv * xv, axis=-1, keepdims=True) + NORM_EPS)
            xn = xv * r
            dxn = dhv * gain_ref[...]
            dx_ref[...] = dres_ref[...] + r * (dxn - xn * jnp.mean(dxn * xn, axis=-1, keepdims=True))

            @pl.when(i == 0)
            def _():
                dgain_ref[...] = jnp.zeros_like(dgain_ref)

            dgain_ref[...] += _row0(jnp.sum(dhv * xn, axis=0, keepdims=True))

    tile = pl.BlockSpec((tm, K), lambda i, kk: (i, 0))
    return pl.pallas_call(
        body, name=name, grid=(T // tm, N_CHIP),
        in_specs=[pl.BlockSpec((tm, N), lambda i, kk: (i, kk)), pl.BlockSpec((None, K, N), lambda i, kk: (kk, 0, 0)),
                  tile, pl.BlockSpec((1, K), lambda i, kk: (0, 0)), tile],
        out_specs=[tile, pl.BlockSpec((SUBLANES, K), lambda i, kk: (0, 0))],
        out_shape=[jax.ShapeDtypeStruct((T, K), F32), jax.ShapeDtypeStruct((SUBLANES, K), F32)],
        scratch_shapes=[pltpu.VMEM((tm, K), F32)], compiler_params=_params(2))(g, w, x, gain, dres)


def _mm_wgrad(name, a, g, a_cols, g_cols, a_blocked, g_blocked, after=()):
    T = a.shape[0]
    tt = T
    while tt > LANES and 2 * 2 * tt * (a_cols + g_cols) + (2 if tt == T else 3) * 4 * a_cols * g_cols > VMEM_BLOCK_BUDGET:
        tt //= 2
    return _matmul(
        name, a, g, dims=TN, grid=(N_CHIP, 1, T // tt),
        a_spec=pl.BlockSpec((tt, a_cols), (lambda j, i, kk: (kk, j)) if a_blocked else (lambda j, i, kk: (kk, 0))),
        b_spec=pl.BlockSpec((tt, g_cols), (lambda j, i, kk: (kk, j)) if g_blocked else (lambda j, i, kk: (kk, 0))),
        out_specs=[pl.BlockSpec((None, a_cols, g_cols), lambda j, i, kk: (j, 0, 0))],
        out_shapes=[jax.ShapeDtypeStruct((N_CHIP, a_cols, g_cols), F32)], acc_shape=(a_cols, g_cols), after=after)[0]


def _rms_fwd(name, x, g, after=()):
    T, Dm = x.shape
    tm = min(256, T)

    def body(x_ref, g_ref, *rest):
        xv = x_ref[...]
        r = lax.rsqrt(jnp.mean(xv * xv, axis=-1, keepdims=True) + NORM_EPS)
        rest[-1][...] = (xv * r * g_ref[...]).astype(BF16)

    return pl.pallas_call(
        body, name=name, grid=(T // tm,),
        in_specs=[pl.BlockSpec((tm, Dm), lambda i: (i, 0)), pl.BlockSpec((1, Dm), lambda i: (0, 0))] + [ANY] * len(after),
        out_specs=pl.BlockSpec((tm, Dm), lambda i: (i, 0)),
        out_shape=jax.ShapeDtypeStruct((T, Dm), BF16), compiler_params=_params(1))(x, g, *after)


def _loss_head(x, g, tgt):
    T, Dm = x.shape
    tm = min(256, T)

    def body(x_ref, g_ref, t_ref, loss_ref, dx_ref, dg_ref):
        xv = x_ref[...]
        gv = g_ref[...]
        r = lax.rsqrt(jnp.mean(xv * xv, axis=-1, keepdims=True) + NORM_EPS)
        xn = xv * r
        err = xn * gv - t_ref[...]
        dy = err * (1.0 / Dm)
        dxn = dy * gv
        dx_ref[...] = r * (dxn - xn * jnp.mean(dxn * xn, axis=-1, keepdims=True))

        @pl.when(pl.program_id(0) == 0)
        def _():
            dg_ref[...] = jnp.zeros_like(dg_ref)
            loss_ref[...] = jnp.zeros_like(loss_ref)

        dg_ref[...] += _row0(jnp.sum(dy * xn, axis=0, keepdims=True))
        part = jnp.sum(jnp.sum(err * err, axis=-1, keepdims=True), axis=0, keepdims=True) * (0.5 / Dm)
        loss_ref[...] += jnp.broadcast_to(part, loss_ref.shape)

    tile = pl.BlockSpec((tm, Dm), lambda i: (i, 0))
    return pl.pallas_call(
        body, name="loss_head", grid=(T // tm,),
        in_specs=[tile, pl.BlockSpec((1, Dm), lambda i: (0, 0)), tile],
        out_specs=[pl.BlockSpec((SUBLANES, LANES), lambda i: (0, 0)), tile,
                   pl.BlockSpec((SUBLANES, Dm), lambda i: (0, 0))],
        out_shape=[jax.ShapeDtypeStruct((SUBLANES, LANES), F32), jax.ShapeDtypeStruct((T, Dm), F32),
                   jax.ShapeDtypeStruct((SUBLANES, Dm), F32)],
        compiler_params=_params(1))(x, g, tgt)


def _softmax_rows(lb_ref):
    rows = [lb_ref[pl.ds(i, 1), :] for i in range(DEPTH)]
    mx = rows[0]
    for r in rows[1:]:
        mx = jnp.maximum(mx, r)
    es = [jnp.exp(r - mx) for r in rows]
    tot = es[0]
    for e in es[1:]:
        tot = tot + e
    return [e / tot for e in es]


def _lbs_fwd(lower_bounds):
    def body(lb_ref, out_ref):
        sm = _softmax_rows(lb_ref)
        run = jnp.zeros_like(sm[0])
        out_ref[pl.ds(0, 1), :] = run
        for i in range(1, DEPTH):
            run = run + sm[i]
            out_ref[pl.ds(i, 1), :] = run

    return pl.pallas_call(body, name="lbs_fwd", out_shape=jax.ShapeDtypeStruct(lower_bounds.shape, F32))(lower_bounds)


def _lbs_bwd(lower_bounds, dlbs):
    def body(lb_ref, d_ref, out_ref):
        sm = _softmax_rows(lb_ref)
        dsm = [jnp.zeros_like(sm[0])]
        for i in range(1, DEPTH):
            acc = d_ref[pl.ds(i, 1), :]
            for l in range(i + 1, DEPTH):
                acc = acc + d_ref[pl.ds(l, 1), :]
            dsm.append(acc)
        inner = dsm[0] * sm[0]
        for i in range(1, DEPTH):
            inner = inner + dsm[i] * sm[i]
        for i in range(DEPTH):
            out_ref[pl.ds(i, 1), :] = sm[i] * (dsm[i] - inner)

    return pl.pallas_call(body, name="lbs_bwd", out_shape=jax.ShapeDtypeStruct(lower_bounds.shape, F32))(lower_bounds, dlbs)


N_LEVEL = 6


def _hgrn_consts():
    L = H_CHUNK
    t = np.arange(L)
    blocks = [(t[:, None] >= t[None, :]).astype(np.float32)]
    masks = []
    m = L // 2
    while m >= 1:
        blk, pos = t // (2 * m), t % (2 * m)
        start = blk * 2 * m
        mat = np.zeros((L, L), np.float32)
        for r in range(L):
            if pos[r] >= m:
                mat[r, start[r] + m:r + 1] = 1.0
            else:
                mat[r, r + 1:start[r] + m] = -1.0
        blocks.append(mat)
        masks.append(((blk[:, None] == blk[None, :]) & (pos[:, None] >= m) & (pos[None, :] < m)).astype(np.float32))
        m //= 2
    blocks.append(np.ones((L, L), np.float32))
    return jnp.asarray(np.concatenate(blocks, 0), BF16), jnp.asarray(np.stack(masks), F32)


def _hgrn_core(qraw, fp, lb, sum_mat, mask_ref):
    L = H_CHUNK
    sq = jax.nn.sigmoid(qraw)
    q = qraw * sq
    sneg = jax.nn.sigmoid(-fp)
    log_sig = jnp.minimum(fp, 0.0) - jnp.log1p(jnp.exp(-jnp.abs(fp)))
    a1 = jnp.log(jnp.maximum(lb, LB_FLOOR))
    a2 = jnp.log1p(-lb) + log_sig
    logf = jnp.maximum(a1, a2) + jnp.log1p(jnp.exp(-jnp.abs(a1 - a2)))
    w1 = jnp.exp(a1 - logf)
    w2 = jnp.exp(a2 - logf)
    k = (1.0 - lb) * sneg
    hi = logf.astype(BF16)
    r1 = logf - hi.astype(F32)
    mid = r1.astype(BF16)
    lo = (r1 - mid.astype(F32)).astype(BF16)
    sums = lax.dot_general(sum_mat, jnp.concatenate([hi, mid, lo], axis=1), NN, preferred_element_type=F32)
    sums = sums[:, 0:HEAD] + sums[:, HEAD:2 * HEAD] + sums[:, 2 * HEAD:3 * HEAD]
    b = sums[0:L]
    b_last = sums[(N_LEVEL + 1) * L:(N_LEVEL + 2) * L]
    eye = lax.broadcasted_iota(jnp.int32, (L, L), 0) == lax.broadcasted_iota(jnp.int32, (L, L), 1)
    attn = jnp.where(eye, jnp.sum(q * k, axis=1, keepdims=True), 0.0)
    fa, fb, ea, eb = [], [], [], []
    for l in range(N_LEVEL):
        d = sums[(l + 1) * L:(l + 2) * L]
        e_a = jnp.exp(jnp.minimum(d, 0.0))
        e_b = jnp.exp(jnp.minimum(-d, 0.0))
        a_l, b_l = q * e_a, k * e_b
        attn = attn + mask_ref[l] * _dot(a_l, b_l, NT)
        fa.append(a_l), fb.append(b_l), ea.append(e_a), eb.append(e_b)
    return dict(sq=sq, q=q, sneg=sneg, logf=logf, w1=w1, w2=w2, k=k, b=b, b_last=b_last, attn=attn,
                fa=fa, fb=fb, ea=ea, eb=eb)


def _hgrn_fwd(p, lbrow, gout):
    T = p.shape[0]
    nch = T // H_CHUNK
    sum_mat, masks = _hgrn_consts()

    def body(p_ref, lb_ref, g_ref, m_ref, mask_ref, o_ref, z_ref, st_ref, state):
        @pl.when(pl.program_id(0) == 0)
        def _():
            state[...] = jnp.zeros_like(state)

        sum_m = m_ref[...]
        for h in range(N_HEAD):
            col = lambda part: pl.ds(part * WIDTH + h * HEAD, HEAD)
            hs = pl.ds(h * HEAD, HEAD)
            v = p_ref[:, col(2)]
            c = _hgrn_core(p_ref[:, col(0)], p_ref[:, col(1)], lb_ref[:, hs], sum_m, mask_ref)
            s0 = state[h]
            st_ref[h] = s0
            o = _dot(c["attn"], v) + _dot(c["q"] * jnp.exp(c["b"]), s0, NT)
            k_dec = c["k"] * jnp.exp(c["b_last"] - c["b"])
            decay = jnp.exp(jnp.max(c["b_last"], axis=0, keepdims=True))
            state[h] = s0 * decay + _dot(v, k_dec, TN)
            o_ref[:, hs] = o
            r = lax.rsqrt(jnp.mean(o * o, axis=-1, keepdims=True) + NORM_EPS)
            z_ref[:, hs] = (o * r * g_ref[:, hs] * jax.nn.sigmoid(p_ref[:, col(3)])).astype(BF16)

    full = lambda shape: pl.BlockSpec(shape, lambda c: (0,) * len(shape))
    return pl.pallas_call(
        body, name="hgrn_fwd", grid=(nch,),
        in_specs=[pl.BlockSpec((H_CHUNK, 4 * WIDTH), lambda c: (c, 0)), full((1, WIDTH)), full((1, WIDTH)),
                  full(sum_mat.shape), full(masks.shape)],
        out_specs=[pl.BlockSpec((H_CHUNK, WIDTH), lambda c: (c, 0)),
                   pl.BlockSpec((None, H_CHUNK, WIDTH), lambda c: (0, c, 0)),
                   pl.BlockSpec((None, N_HEAD, HEAD, HEAD), lambda c: (c, 0, 0, 0))],
        out_shape=[jax.ShapeDtypeStruct((T, WIDTH), F32), jax.ShapeDtypeStruct((N_BRANCH, T, WIDTH), BF16),
                   jax.ShapeDtypeStruct((nch, N_HEAD, HEAD, HEAD), F32)],
        scratch_shapes=[pltpu.VMEM((N_HEAD, HEAD, HEAD), F32)], compiler_params=_params(1),
    )(p, lbrow, gout, sum_mat, masks)


def _hgrn_bwd(p, o_saved, dz, states, lbrow, gout, dp, after=()):
    T = p.shape[0]
    nch = T // H_CHUNK
    L = H_CHUNK
    sum_mat, masks = _hgrn_consts()

    def body(p_ref, o_ref, dz_ref, st_ref, lb_ref, g_ref, m_ref, mask_ref, dp_in, *rest):
        del dp_in
        dp_ref, dlb_ref, dg_ref, dstate = rest[len(after):]

        @pl.when(pl.program_id(0) == 0)
        def _():
            dstate[...] = jnp.zeros_like(dstate)
            dlb_ref[...] = jnp.zeros_like(dlb_ref)
            dg_ref[...] = jnp.zeros_like(dg_ref)

        sum_m = m_ref[...]
        for h in range(N_HEAD):
            col = lambda part: pl.ds(part * WIDTH + h * HEAD, HEAD)
            hs = pl.ds(h * HEAD, HEAD)
            qraw, fp, v, go = p_ref[:, col(0)], p_ref[:, col(1)], p_ref[:, col(2)], p_ref[:, col(3)]
            lb, g = lb_ref[:, hs], g_ref[:, hs]
            c = _hgrn_core(qraw, fp, lb, sum_m, mask_ref)
            q, k, b, b_last = c["q"], c["k"], c["b"], c["b_last"]
            s0, ds1 = st_ref[h], dstate[h]
            e_b = jnp.exp(b)
            q_dec = q * e_b
            e_bl = jnp.exp(b_last - b)
            k_dec = k * e_bl
            decay = jnp.exp(jnp.max(b_last, axis=0, keepdims=True))
            o = o_ref[:, hs]
            r = lax.rsqrt(jnp.mean(o * o, axis=-1, keepdims=True) + NORM_EPS)
            n = o * r
            sgo = jax.nn.sigmoid(go)
            dza = dz_ref[:, hs]
            dgo = dza * n * g * sgo * (1.0 - sgo)
            dg_ref[:, hs] += _row0(jnp.sum(dza * n * sgo, axis=0, keepdims=True))
            dn = dza * g * sgo
            do = r * (dn - n * jnp.mean(dn * n, axis=-1, keepdims=True))
            dattn = _dot(do, v, NT)
            dv = _dot(c["attn"], do, TN) + _dot(k_dec, ds1, NT)
            dq_dec = _dot(do, s0)
            dk_dec = _dot(v, ds1)
            ddiag = jnp.sum(do * v, axis=1, keepdims=True)
            dq = dq_dec * e_b + ddiag * k
            dk = dk_dec * e_bl + ddiag * q
            dsums = [dq_dec * q_dec - dk_dec * k_dec]
            for l in range(N_LEVEL):
                dm = mask_ref[l] * dattn
                da = _dot(dm, c["fb"][l])
                db = _dot(dm, c["fa"][l], TN)
                dq = dq + da * c["ea"][l]
                dk = dk + db * c["eb"][l]
                dsums.append(da * c["fa"][l] - db * c["fb"][l])
            dlast = jnp.sum(ds1 * s0, axis=0, keepdims=True) * decay
            dsums.append(dk_dec * k_dec + _row0(dlast, L))
            dlogf = _dot(sum_m, jnp.concatenate(dsums, axis=0), TN)
            dstate[h] = ds1 * decay + _dot(do, q_dec, TN)
            sq, sneg = c["sq"], c["sneg"]
            dqraw = dq * sq * (1.0 + qraw * (1.0 - sq))
            dfp = dlogf * c["w2"] * sneg - dk * (1.0 - lb) * sneg * (1.0 - sneg)
            inv_lb = jnp.where(lb > LB_FLOOR, 1.0 / jnp.maximum(lb, LB_FLOOR), 0.0)
            dlb_tok = dlogf * (c["w1"] * inv_lb - c["w2"] / (1.0 - lb)) - dk * sneg
            dlb_ref[:, hs] += _row0(jnp.sum(dlb_tok, axis=0, keepdims=True))
            dp_ref[:, col(0)] = dqraw.astype(BF16)
            dp_ref[:, col(1)] = dfp.astype(BF16)
            dp_ref[:, col(2)] = dv.astype(BF16)
            dp_ref[:, col(3)] = dgo.astype(BF16)

    full = lambda shape: pl.BlockSpec(shape, lambda c: (0,) * len(shape))
    rev = lambda c: nch - 1 - c
    return pl.pallas_call(
        body, name="hgrn_bwd", grid=(nch,),
        in_specs=[pl.BlockSpec((L, 4 * WIDTH), lambda c: (rev(c), 0)), pl.BlockSpec((L, WIDTH), lambda c: (rev(c), 0)),
                  pl.BlockSpec((None, L, WIDTH), lambda c: (0, rev(c), 0)),
                  pl.BlockSpec((None, N_HEAD, HEAD, HEAD), lambda c: (rev(c), 0, 0, 0)),
                  full((1, WIDTH)), full((1, WIDTH)), full(sum_mat.shape), full(masks.shape), ANY, *[ANY] * len(after)],
        out_specs=[pl.BlockSpec((L, 4 * WIDTH), lambda c: (rev(c), 0)), full((SUBLANES, WIDTH)), full((SUBLANES, WIDTH))],
        out_shape=[jax.ShapeDtypeStruct(dp.shape, dp.dtype), jax.ShapeDtypeStruct((SUBLANES, WIDTH), F32),
                   jax.ShapeDtypeStruct((SUBLANES, WIDTH), F32)],
        scratch_shapes=[pltpu.VMEM((N_HEAD, HEAD, HEAD), F32)], input_output_aliases={8: 0},
        compiler_params=_params(1),
    )(p, o_saved, dz, states, lbrow, gout, sum_mat, masks, dp, *after)


def _shift_down(tile, halo, s):
    tm = tile.shape[0]
    rows = lax.broadcasted_iota(jnp.int32, tile.shape, 0)
    head = jnp.concatenate([pltpu.roll(halo, s, 0), jnp.zeros((tm - SUBLANES, tile.shape[1]), tile.dtype)], axis=0)
    return jnp.where(rows < s, head, pltpu.roll(tile, s, 0))


def _shift_up(tile, halo, s):
    tm = tile.shape[0]
    rows = lax.broadcasted_iota(jnp.int32, tile.shape, 0)
    tail = jnp.concatenate([jnp.zeros((tm - SUBLANES, tile.shape[1]), tile.dtype), pltpu.roll(halo, SUBLANES - s, 0)], axis=0)
    return jnp.where(rows >= tm - s, tail, pltpu.roll(tile, tm - s, 0))


def _conv_fwd(p, w, z):
    T = p.shape[0]
    tm = _token_tile(T)
    per = tm // SUBLANES

    def body(bg_ref, cg_ref, xc_ref, hcg_ref, hxc_ref, w_ref, z_in, z_ref):
        del z_in
        zc = cg_ref[...] * xc_ref[...]
        hz = jnp.where(pl.program_id(0) > 0, hcg_ref[...] * hxc_ref[...], 0.0)
        y = (w_ref[pl.ds(0, 1), :] * _shift_down(zc, hz, 2) + w_ref[pl.ds(1, 1), :] * _shift_down(zc, hz, 1)
             + w_ref[pl.ds(2, 1), :] * zc)
        z_ref[...] = (bg_ref[...] * y).astype(BF16)

    tile = lambda cb: pl.BlockSpec((tm, WIDTH), lambda i: (i, cb))
    prev = lambda cb: pl.BlockSpec((SUBLANES, WIDTH), lambda i: (jnp.maximum(i * per - 1, 0), cb))
    return pl.pallas_call(
        body, name="conv_fwd", grid=(T // tm,),
        in_specs=[tile(4), tile(5), tile(6), prev(5), prev(6), pl.BlockSpec((CONV_K, WIDTH), lambda i: (0, 0)), ANY],
        out_specs=pl.BlockSpec((None, tm, WIDTH), lambda i: (1, i, 0)),
        out_shape=jax.ShapeDtypeStruct(z.shape, z.dtype), input_output_aliases={6: 0}, compiler_params=_params(1),
    )(p, p, p, p, p, w, z)


def _conv_bwd(p, w, dz, dp):
    T = p.shape[0]
    tm = _token_tile(T)
    per = tm // SUBLANES
    last = T // SUBLANES - 1

    def body(bg_ref, cg_ref, xc_ref, hcg_ref, hxc_ref, nbg_ref, dzb_ref, ndzb_ref, w_ref, dp_in, dp_ref, dw_ref, stash):
        del dp_in
        i, jj = pl.program_id(0), pl.program_id(1)

        @pl.when(jnp.logical_and(i == 0, jj == 0))
        def _():
            dw_ref[...] = jnp.zeros_like(dw_ref)

        @pl.when(jj == 0)
        def _():
            cg, xc, bg = cg_ref[...], xc_ref[...], bg_ref[...]
            w0, w1, w2 = w_ref[pl.ds(0, 1), :], w_ref[pl.ds(1, 1), :], w_ref[pl.ds(2, 1), :]
            zc = cg * xc
            hz = jnp.where(i > 0, hcg_ref[...] * hxc_ref[...], 0.0)
            z2, z1 = _shift_down(zc, hz, 2), _shift_down(zc, hz, 1)
            y = w0 * z2 + w1 * z1 + w2 * zc
            dzb = dzb_ref[...]
            dy = dzb * bg
            hdy = jnp.where(i < pl.num_programs(0) - 1, ndzb_ref[...] * nbg_ref[...], 0.0)
            dzc = w2 * dy + w1 * _shift_up(dy, hdy, 1) + w0 * _shift_up(dy, hdy, 2)
            rows = lax.broadcasted_iota(jnp.int32, (SUBLANES, WIDTH), 0)
            colsum = lambda t: jnp.sum(t, axis=0, keepdims=True)
            dw_ref[...] += (jnp.where(rows == 0, colsum(dy * z2), 0.0) + jnp.where(rows == 1, colsum(dy * z1), 0.0)
                            + jnp.where(rows == 2, colsum(dy * zc), 0.0))
            dp_ref[...] = (dzb * y).astype(BF16)
            stash[0] = dzc * xc
            stash[1] = dzc * cg

        @pl.when(jj > 0)
        def _():
            dp_ref[...] = stash[jj - 1].astype(BF16)

    n_tiles = T // tm
    tile = lambda cb: pl.BlockSpec((tm, WIDTH), lambda i, jj: (i, cb))
    prev = lambda cb: pl.BlockSpec((SUBLANES, WIDTH), lambda i, jj: (jnp.maximum(i * per - 1, 0), cb))
    nxt = lambda i: jnp.minimum((i + 1) * per, last)
    return pl.pallas_call(
        body, name="conv_bwd", grid=(n_tiles, 3),
        in_specs=[tile(4), tile(5), tile(6), prev(5), prev(6),
                  pl.BlockSpec((SUBLANES, WIDTH), lambda i, jj: (nxt(i), 4)),
                  pl.BlockSpec((None, tm, WIDTH), lambda i, jj: (1, i, 0)),
                  pl.BlockSpec((None, SUBLANES, WIDTH), lambda i, jj: (1, nxt(i), 0)),
                  pl.BlockSpec((CONV_K, WIDTH), lambda i, jj: (0, 0)), ANY],
        out_specs=[pl.BlockSpec((tm, WIDTH), lambda i, jj: (i, 4 + jj)),
                   pl.BlockSpec((SUBLANES, WIDTH), lambda i, jj: (0, 0))],
        out_shape=[jax.ShapeDtypeStruct(dp.shape, dp.dtype), jax.ShapeDtypeStruct((SUBLANES, WIDTH), F32)],
        scratch_shapes=[pltpu.VMEM((2, tm, WIDTH), F32)], input_output_aliases={9: 0}, compiler_params=_params(2),
    )(p, p, p, p, p, p, dz, dz, w, dp)


GELU_C = float(np.sqrt(2.0 / np.pi))
GELU_A = 0.044715


def _gelu(x):
    th = jnp.tanh(GELU_C * (x + GELU_A * x * x * x))
    return 0.5 * x * (1.0 + th), th


def _gelu_grad(x, th):
    return 0.5 * (1.0 + th) + 0.5 * x * (1.0 - th * th) * GELU_C * (1.0 + 3.0 * GELU_A * x * x)


def _sg_core(u, v, lng, lnb, ws_ref, bs_ref):
    gu, thu = _gelu(u)
    gv, thv = _gelu(v)
    xc = gv - jnp.mean(gv, axis=-1, keepdims=True)
    rs = lax.rsqrt(jnp.mean(xc * xc, axis=-1, keepdims=True) + LN_EPS)
    xh = xc * rs
    vp = xh * lng + lnb
    tril = (lax.broadcasted_iota(jnp.int32, (SG_CHUNK, SG_CHUNK), 0)
            >= lax.broadcasted_iota(jnp.int32, (SG_CHUNK, SG_CHUNK), 1))
    wm = [jnp.where(tril, ws_ref[g], 0.0).astype(BF16) for g in range(SG_GROUPS)]
    gs = lambda t, g: t[:, g * LANES:(g + 1) * LANES]
    sv = jnp.concatenate([_dot(wm[g], gs(vp, g)) + bs_ref[g] for g in range(SG_GROUPS)], axis=1)
    return dict(gu=gu, thu=thu, thv=thv, rs=rs, xh=xh, vp=vp, tril=tril, wm=wm, sv=sv)


def _sg_fwd(p, lng, lnb, ws, bs, z):
    T = p.shape[0]

    def body(u_ref, v_ref, lng_ref, lnb_ref, ws_ref, bs_ref, z_in, z_ref):
        del z_in
        c = _sg_core(u_ref[...], v_ref[...], lng_ref[...], lnb_ref[...], ws_ref, bs_ref)
        z_ref[...] = (c["gu"] * c["sv"]).astype(BF16)

    full = lambda shape: pl.BlockSpec(shape, lambda c: (0,) * len(shape))
    return pl.pallas_call(
        body, name="sg_fwd", grid=(T // SG_CHUNK,),
        in_specs=[pl.BlockSpec((SG_CHUNK, WIDTH), lambda c: (c, 7)), pl.BlockSpec((SG_CHUNK, WIDTH), lambda c: (c, 8)),
                  full((1, WIDTH)), full((1, WIDTH)), full(ws.shape), full(bs.shape), ANY],
        out_specs=pl.BlockSpec((None, SG_CHUNK, WIDTH), lambda c: (2, c, 0)),
        out_shape=jax.ShapeDtypeStruct(z.shape, z.dtype), input_output_aliases={6: 0}, compiler_params=_params(1),
    )(p, p, lng, lnb, ws, bs, z)


def _sg_bwd(p, lng, lnb, ws, bs, dz, dp):
    T = p.shape[0]

    def body(u_ref, v_ref, lng_ref, lnb_ref, ws_ref, bs_ref, dz_ref, dp_in, dp_ref, dws_ref, dbs_ref, dlng_ref, dlnb_ref,
             stash):
        del dp_in
        cidx, jj = pl.program_id(0), pl.program_id(1)

        @pl.when(jnp.logical_and(cidx == 0, jj == 0))
        def _():
            dws_ref[...] = jnp.zeros_like(dws_ref)
            dbs_ref[...] = jnp.zeros_like(dbs_ref)
            dlng_ref[...] = jnp.zeros_like(dlng_ref)
            dlnb_ref[...] = jnp.zeros_like(dlnb_ref)

        @pl.when(jj == 0)
        def _():
            u, v, lng = u_ref[...], v_ref[...], lng_ref[...]
            c = _sg_core(u, v, lng, lnb_ref[...], ws_ref, bs_ref)
            dzc = dz_ref[...]
            gs = lambda t, g: t[:, g * LANES:(g + 1) * LANES]
            dsv = dzc * c["gu"]
            dvp = []
            for g in range(SG_GROUPS):
                dsv_g = gs(dsv, g)
                dws_ref[g] += jnp.where(c["tril"], _dot(dsv_g, gs(c["vp"], g), NT), 0.0)
                dbs_ref[g] += jnp.sum(dsv_g, axis=1, keepdims=True)
                dvp.append(_dot(c["wm"][g], dsv_g, TN))
            dvp = jnp.concatenate(dvp, axis=1)
            xh = c["xh"]
            dlng_ref[...] += _row0(jnp.sum(dvp * xh, axis=0, keepdims=True))
            dlnb_ref[...] += _row0(jnp.sum(dvp, axis=0, keepdims=True))
            dxh = dvp * lng
            dgv = c["rs"] * (dxh - jnp.mean(dxh, axis=-1, keepdims=True) - xh * jnp.mean(dxh * xh, axis=-1, keepdims=True))
            dp_ref[...] = (dzc * c["sv"] * _gelu_grad(u, c["thu"])).astype(BF16)
            stash[...] = dgv * _gelu_grad(v, c["thv"])

        @pl.when(jj == 1)
        def _():
            dp_ref[...] = stash[...].astype(BF16)

    full = lambda shape: pl.BlockSpec(shape, lambda c, jj: (0,) * len(shape))
    return pl.pallas_call(
        body, name="sg_bwd", grid=(T // SG_CHUNK, 2),
        in_specs=[pl.BlockSpec((SG_CHUNK, WIDTH), lambda c, jj: (c, 7)), pl.BlockSpec((SG_CHUNK, WIDTH), lambda c, jj: (c, 8)),
                  full((1, WIDTH)), full((1, WIDTH)), full(ws.shape), full(bs.shape),
                  pl.BlockSpec((None, SG_CHUNK, WIDTH), lambda c, jj: (2, c, 0)), ANY],
        out_specs=[pl.BlockSpec((SG_CHUNK, WIDTH), lambda c, jj: (c, 7 + jj)), full(ws.shape), full(bs.shape),
                   full((SUBLANES, WIDTH)), full((SUBLANES, WIDTH))],
        out_shape=[jax.ShapeDtypeStruct(dp.shape, dp.dtype), jax.ShapeDtypeStruct(ws.shape, F32),
                   jax.ShapeDtypeStruct(bs.shape, F32), jax.ShapeDtypeStruct((SUBLANES, WIDTH), F32),
                   jax.ShapeDtypeStruct((SUBLANES, WIDTH), F32)],
        scratch_shapes=[pltpu.VMEM((SG_CHUNK, WIDTH), F32)], input_output_aliases={7: 0}, compiler_params=_params(2),
    )(p, p, lng, lnb, ws, bs, dz, dp)


BRANCH_COLS = D_MODEL // N_CHIP
GATE_UNIT0 = GATE_COL0 // WIDTH
UNITS = D_MODEL // WIDTH


def _unit_specs(order):
    def spec(which):
        def index(*g):
            _, n, u = order(*g)
            return (2 * u + which, n, 0, 0)
        return pl.BlockSpec((None, None, WIDTH, BRANCH_COLS), index)
    return [spec(0), spec(1)]


def _merge_fwd(z, p, wb):
    T = z.shape[1]
    tm = _token_tile(T)
    order = lambda i, u, n: (i, n, u)

    def body(z_ref, wa_ref, wb_ref, gt_ref, out_ref, acc):
        n = pl.program_id(2)
        zv = z_ref[...]
        y = jnp.concatenate([_dot(zv, wa_ref[...]), _dot(zv, wb_ref[...])], axis=1)
        part = jax.nn.sigmoid(gt_ref[...]) * y

        @pl.when(n == 0)
        def _():
            acc[...] = part

        @pl.when(n > 0)
        def _():
            acc[...] += part

        @pl.when(n == N_BRANCH - 1)
        def _():
            out_ref[...] = acc[...].astype(BF16)

    return pl.pallas_call(
        body, name="merge_fwd", grid=(T // tm, UNITS, N_BRANCH),
        in_specs=[pl.BlockSpec((None, tm, WIDTH), lambda i, u, n: (n, i, 0)), *_unit_specs(order),
                  pl.BlockSpec((tm, WIDTH), lambda i, u, n: (i, GATE_UNIT0 + UNITS * n + u))],
        out_specs=pl.BlockSpec((tm, WIDTH), lambda i, u, n: (i, u)),
        out_shape=jax.ShapeDtypeStruct((T, D_MODEL), BF16),
        scratch_shapes=[pltpu.VMEM((tm, WIDTH), F32)], compiler_params=_params(3))(z, wb, wb, p)


def _merge_bwd(z, p, wb, dmerged):
    T = z.shape[1]
    tm = _token_tile(T)
    order = lambda i, n, u: (i, n, u)

    def body(z_ref, wa_ref, wb_ref, gt_ref, dm_ref, dp_ref, dy_ref, dz_ref):
        u = pl.program_id(2)
        zv, wa, wbv = z_ref[...], wa_ref[...], wb_ref[...]
        y = jnp.concatenate([_dot(zv, wa), _dot(zv, wbv)], axis=1)
        gate = jax.nn.sigmoid(gt_ref[...])
        dm = dm_ref[...]
        dp_ref[...] = (dm * y * gate * (1.0 - gate)).astype(BF16)
        dyv = (dm * gate).astype(BF16)
        dy_ref[...] = dyv
        part = _dot(dyv[:, :BRANCH_COLS], wa, NT) + _dot(dyv[:, BRANCH_COLS:], wbv, NT)

        @pl.when(u == 0)
        def _():
            dz_ref[...] = part

        @pl.when(u > 0)
        def _():
            dz_ref[...] += part

    unit = lambda i, n, u: (i, GATE_UNIT0 + UNITS * n + u)
    return pl.pallas_call(
        body, name="merge_bwd", grid=(T // tm, N_BRANCH, UNITS),
        in_specs=[pl.BlockSpec((None, tm, WIDTH), lambda i, n, u: (n, i, 0)), *_unit_specs(order),
                  pl.BlockSpec((tm, WIDTH), unit), pl.BlockSpec((tm, WIDTH), lambda i, n, u: (i, u))],
        out_specs=[pl.BlockSpec((tm, WIDTH), unit), pl.BlockSpec((None, tm, WIDTH), lambda i, n, u: (n, i, u)),
                   pl.BlockSpec((None, tm, WIDTH), lambda i, n, u: (n, i, 0))],
        out_shape=[jax.ShapeDtypeStruct((T, IN_COLS), BF16), jax.ShapeDtypeStruct((N_BRANCH, T, D_MODEL), BF16),
                   jax.ShapeDtypeStruct((N_BRANCH, T, WIDTH), F32)],
        compiler_params=_params(3))(z, wb, wb, p, dmerged)


def _branch_wgrad(z, dy):
    T = z.shape[1]
    tt = T
    nk = T // tt

    def body(z_ref, dy_ref, out_ref, acc):
        kk = pl.program_id(1)
        part = _dot(z_ref[...], dy_ref[...], TN)

        @pl.when(kk == 0)
        def _():
            acc[...] = part

        @pl.when(kk > 0)
        def _():
            acc[...] += part

        @pl.when(kk == nk - 1)
        def _():
            for k in range(N_CHIP):
                out_ref[k] = acc[:, k * BRANCH_COLS:(k + 1) * BRANCH_COLS]

    return pl.pallas_call(
        body, name="branch_wgrad", grid=(N_BRANCH, nk),
        in_specs=[pl.BlockSpec((None, tt, WIDTH), lambda n, kk: (n, kk, 0)),
                  pl.BlockSpec((None, tt, D_MODEL), lambda n, kk: (n, kk, 0))],
        out_specs=pl.BlockSpec((N_CHIP, None, WIDTH, BRANCH_COLS), lambda n, kk: (0, n, 0, 0)),
        out_shape=jax.ShapeDtypeStruct((N_CHIP, N_BRANCH, WIDTH, BRANCH_COLS), F32),
        scratch_shapes=[pltpu.VMEM((WIDTH, D_MODEL), F32)], compiler_params=_params(2))(z, dy)


def _layer_fwd(x, h, win, late_weights, small, next_gain, hook):
    p = _mm_cols("in_proj", h, win, [F32])[0]
    o_hgrn, z, states = _hgrn_fwd(p, small["lbs"], small["g_hgrn_out"])
    z = _conv_fwd(p, small["w_conv"], z)
    z = _sg_fwd(p, small["sg_ln_g"], small["sg_ln_b"], small["w_sg"], small["b_sg"], z)
    wb, wo, w1, w2 = late_weights([z])
    wb = wb.reshape(N_CHIP, N_BRANCH, WIDTH, BRANCH_COLS)
    merged = _merge_fwd(z, p, wb)
    x_mid, h2 = _mm_rows("out_proj", merged, wo, x, small["g_ffn"])
    s = _mm_cols("ff1", h2, w1, [BF16], epilogue=lambda acc: (jnp.square(jnp.maximum(acc, 0.0)),))[0]
    if next_gain is None:
        x_out, h_next = _mm_rows("ff2_last", s, w2, x_mid, after=hook([s])), None
    else:
        x_out, h_next = _mm_rows("ff2", s, w2, x_mid, next_gain, after=hook([s]))
    saved = dict(x=x, h=h, p=p, o_hgrn=o_hgrn, z=z, states=states, merged=merged, x_mid=x_mid, h2=h2, s=s)
    return x_out, h_next, saved, [win, wb, wo, w1, w2]


def _layer_bwd(dx_out, sv, wts, small, tick, after):
    win, wb, wo, w1, w2 = wts
    g = {}
    da = _mm_cols_t("ff2_dgrad", dx_out, w2, BF16, extra=(sv["s"],), after=after,
                    epilogue=lambda acc, s: (acc * 2.0 * jnp.sqrt(s.astype(F32)),))
    d_ff2 = _mm_wgrad("ff2_wgrad", sv["s"], dx_out, w2.shape[1], D_MODEL, True, False)
    d_ff1 = _mm_wgrad("ff1_wgrad", sv["h2"], da, D_MODEL, w1.shape[2], False, True)
    dx_mid, g["g_ffn"] = _dgrad_norm_bwd("ff1_dgrad", da, w1, sv["x_mid"], small["g_ffn"], dx_out)
    after = tick([dx_mid], [("w_ff1", d_ff1), ("w_ff2", d_ff2)])
    dmerged = _mm_cols_t("out_proj_dgrad", dx_mid, wo, F32, after=after)
    d_o = _mm_wgrad("out_proj_wgrad", sv["merged"], dx_mid, wo.shape[1], D_MODEL, True, False)
    dp, dy, dz = _merge_bwd(sv["z"], sv["p"], wb, dmerged)
    d_branch = _branch_wgrad(sv["z"], dy).reshape(N_CHIP, N_BRANCH * WIDTH, BRANCH_COLS)
    after = tick([dp], [("w_branch", d_branch), ("w_o", d_o)])
    dp, g["lbs"], g["g_hgrn_out"] = _hgrn_bwd(sv["p"], sv["o_hgrn"], dz, sv["states"], small["lbs"],
                                              small["g_hgrn_out"], dp, after=after)
    dp, g["w_conv"] = _conv_bwd(sv["p"], small["w_conv"], dz, dp)
    dp, g["w_sg"], g["b_sg"], g["sg_ln_g"], g["sg_ln_b"] = _sg_bwd(
        sv["p"], small["sg_ln_g"], small["sg_ln_b"], small["w_sg"], small["b_sg"], dz, dp)
    after = tick([dp], [])
    d_in = _mm_wgrad("in_proj_wgrad", sv["h"], dp, D_MODEL, win.shape[2], False, True, after=after)
    dx, g["g_mix"] = _dgrad_norm_bwd("in_proj_dgrad", dp, win, sv["x"], small["g_mix"], dx_mid)
    return dx, g, tick([dx], [("w_in", d_in)])


def _mesh_pos():
    return lax.axis_index("x"), lax.axis_index("y"), lax.axis_index("c")


def _other_chips(x, y):
    return [(1 - x, y), (x, 1 - y), (1 - x, 1 - y)]


def _remote(src, dst, send_sems, recv_sems, k, to):
    return pltpu.make_async_remote_copy(src_ref=src, dst_ref=dst, send_sem=send_sems.at[k], recv_sem=recv_sems.at[k],
                                        device_id=to, device_id_type=MESH)


def _gather_call(name, body, buf, after):
    scratch = [pltpu.SemaphoreType.DMA((7,)), pltpu.SemaphoreType.DMA((7,))]
    return pl.pallas_call(
        body, name=name, in_specs=[ANY] * (1 + len(after)), out_specs=ANY,
        out_shape=jax.ShapeDtypeStruct(buf.shape, buf.dtype), scratch_shapes=scratch, input_output_aliases={0: 0})(buf, *after)


HBM = pl.BlockSpec(memory_space=pltpu.HBM)
SEM = pl.BlockSpec(memory_space=pltpu.SEMAPHORE)
DATAFLOW = pltpu.SideEffectType.DATAFLOW_SIDE_EFFECTING


def _split_start(name, bufs, copies, n_copies, after=()):
    n = len(bufs)

    def body(*refs):
        send_sems, recv_sems = refs[n + len(after)], refs[n + len(after) + 1]
        for cp in copies(refs[:n], send_sems, recv_sems):
            cp.start()
        refs[-1][...] = jnp.zeros_like(refs[-1])

    outs = pl.pallas_call(
        body, name=name,
        out_shape=(pltpu.SemaphoreType.DMA((n_copies,)), pltpu.SemaphoreType.DMA((n_copies,)),
                   *[pltpu.HBM(b.shape, b.dtype) for b in bufs], jax.ShapeDtypeStruct((SUBLANES, LANES), F32)),
        in_specs=[HBM] * n + [ANY] * len(after),
        out_specs=(SEM, SEM, *[HBM] * n, pl.BlockSpec(memory_space=pltpu.VMEM)),
        input_output_aliases={t: 2 + t for t in range(n)},
        compiler_params=pltpu.CompilerParams(has_side_effects=DATAFLOW),
    )(*[pltpu.with_memory_space_constraint(b, pltpu.HBM) for b in bufs], *after)
    return outs[0], outs[1], list(outs[2:2 + n]), outs[-1]


def _split_wait(name, started, copies, after):
    send_sems, recv_sems, bufs, _ = started
    n = len(bufs)

    def body(*refs):
        for cp in copies(refs[:n], refs[n], refs[n + 1]):
            cp.wait_send()
            cp.wait_recv()

    return list(pl.pallas_call(
        body, name=name, out_shape=tuple(pltpu.HBM(b.shape, b.dtype) for b in bufs),
        in_specs=[HBM] * n + [SEM, SEM] + [ANY] * len(after), out_specs=tuple([HBM] * n),
        input_output_aliases={t: t for t in range(n)},
        compiler_params=pltpu.CompilerParams(has_side_effects=DATAFLOW),
    )(*bufs, send_sems, recv_sems, *after))


def _weight_ici_copies(refs, send_sems, recv_sems):
    x, y, c = _mesh_pos()
    out = []
    for t, ref in enumerate(refs):
        rh = ref.shape[1] // 2
        mine = ref.at[2 * x + y, pl.ds(c * rh, rh), :]
        out += [_remote(mine, mine, send_sems, recv_sems, 3 * t + j, (*chip, c)) for j, chip in enumerate(_other_chips(x, y))]
    return out


def _weight_d2d_copies(refs, send_sems, recv_sems):
    x, y, c = _mesh_pos()
    out = []
    for t, ref in enumerate(refs):
        rh = ref.shape[1] // 2
        for j, chip in enumerate(_other_chips(x, y)):
            blk = ref.at[2 * chip[0] + chip[1], pl.ds(c * rh, rh), :]
            out.append(_remote(blk, blk, send_sems, recv_sems, 3 * t + j, (x, y, 1 - c)))
    return out


def _swap_part(refs, send_sems, recv_sems, s0):
    x, y, c = _mesh_pos()
    n = len(refs) // 2
    out = []
    for t in range(n):
        rh = refs[t].shape[1] // 2
        out.append(_remote(refs[t].at[:, pl.ds((1 - c) * rh, rh), :], refs[n + t], send_sems, recv_sems, s0 + t, (x, y, 1 - c)))
    return out


def _exchange_part(refs, send_sems, recv_sems, s0):
    x, y, c = _mesh_pos()
    n = len(refs) // 2
    out = []
    for t in range(n):
        for j, chip in enumerate(_other_chips(x, y)):
            out.append(_remote(refs[t].at[2 * chip[0] + chip[1]], refs[n + t].at[j], send_sems, recv_sems, s0 + 3 * t + j,
                               (*chip, c)))
    return out


def _gather_part(refs, send_sems, recv_sems, s0):
    x, y, c = _mesh_pos()
    return [_remote(ref.at[c], ref.at[c], send_sems, recv_sems, s0 + t, (x, y, 1 - c)) for t, ref in enumerate(refs)]


class _GradPipeline:
    def __init__(self, pos):
        self.pos = pos
        self.groups, self.pending, self.count = [], None, 0
        self.reduced = {n: [None] * DEPTH for n in BIG_NAMES}

    def busy(self):
        return bool(self.groups) or self.pending is not None

    def tick(self, deps, new):
        if self.pending is not None:
            started, copies, owners = self.pending
            bufs = _split_wait("grad_pipe_wait_%d" % self.count, started, copies, after=list(deps))
            for grp, lo, hi in owners:
                grp["bufs"] = bufs[lo:hi]
            self.pending = None
        parts = []
        for grp in list(self.groups):
            n, names = len(grp["names"]), grp["names"]
            if grp["stage"] == "swap":
                pair = [_pair_sum("grad_pair_sum_" + nm, f, r, self.pos)
                        for nm, f, r in zip(names, grp["bufs"][:n], grp["bufs"][n:])]
                grp["own32"] = [p32 for p32, _ in pair]
                landing = [lax.empty((3, *p16.shape[1:]), BF16) for _, p16 in pair]
                grp["stage"] = "exchange"
                parts.append((grp, [p16 for _, p16 in pair] + landing, _exchange_part, 3 * n))
            elif grp["stage"] == "exchange":
                halves = [_chip_sum("grad_chip_sum_" + nm, p32, r, self.pos)
                          for nm, p32, r in zip(names, grp["own32"], grp["bufs"][n:])]
                grp["stage"] = "gather"
                parts.append((grp, halves, _gather_part, n))
            else:
                for nm, b in zip(names, grp["bufs"]):
                    self.reduced[nm][grp["layer"]] = b.reshape(-1, b.shape[-1])
                self.groups.remove(grp)
        if new:
            grp = dict(names=[nm for nm, _, _ in new], layer=new[0][1], stage="swap")
            self.groups.append(grp)
            fulls = [g for _, _, g in new]
            landing = [lax.empty((N_CHIP, g.shape[1] // 2, g.shape[2]), F32) for g in fulls]
            parts.append((grp, fulls + landing, _swap_part, len(fulls)))
        if not parts:
            return ()
        bufs, layout, owners, sems = [], [], [], 0
        for grp, part_bufs, fn, n_sems in parts:
            layout.append((len(bufs), len(bufs) + len(part_bufs), fn, sems))
            owners.append((grp, len(bufs), len(bufs) + len(part_bufs)))
            bufs += part_bufs
            sems += n_sems

        def copies(refs, send_sems, recv_sems):
            out = []
            for lo, hi, fn, s0 in layout:
                out += fn(refs[lo:hi], send_sems, recv_sems, s0)
            return out

        started = _split_start("grad_pipe_start_%d" % self.count, bufs, copies, sems)
        self.pending = (started, copies, owners)
        self.count += 1
        return (started[3],)


def _gather_all(name, block, slot, after=()):
    buf = lax.dynamic_update_slice(jnp.zeros((8, *block.shape), block.dtype), block[None], (slot, 0, 0))

    def body(*refs):
        out_ref, send_sems, recv_sems = refs[1 + len(after):]
        x, y, c = _mesh_pos()
        chips = _other_chips(x, y)
        sibling = (x, y, 1 - c)
        slot_of = lambda px, py, pc: out_ref.at[4 * px + 2 * py + pc]
        started = [_remote(slot_of(x, y, c), slot_of(x, y, c), send_sems, recv_sems, 0, sibling)]
        started += [_remote(slot_of(x, y, c), slot_of(x, y, c), send_sems, recv_sems, 1 + j, (*chip, c))
                    for j, chip in enumerate(chips)]
        for cp in started:
            cp.start()
        for j, chip in enumerate(chips):
            _remote(slot_of(*chip, c), slot_of(*chip, c), send_sems, recv_sems, 1 + j, (*chip, c)).wait_recv()
            fw = _remote(slot_of(*chip, c), slot_of(*chip, c), send_sems, recv_sems, 4 + j, sibling)
            fw.start()
            started.append(fw)
        _remote(slot_of(x, y, 1 - c), slot_of(x, y, 1 - c), send_sems, recv_sems, 0, sibling).wait_recv()
        for j, chip in enumerate(chips):
            _remote(slot_of(*chip, 1 - c), slot_of(*chip, 1 - c), send_sems, recv_sems, 4 + j, sibling).wait_recv()
        for cp in started:
            cp.wait_send()

    return _gather_call(name, body, buf, after)


def _row_tile(rows, cols):
    cap = max(SUBLANES, ELEMWISE_BLOCK_BYTES // (4 * cols))
    tr = rows
    while tr > cap and tr % 2 == 0:
        tr //= 2
    return tr


def _pair_sum(name, grad, recv, pos):
    _, rh, cols = recv.shape
    tr = _row_tile(rh, cols)
    per = rh // tr

    def body(pos_ref, g_ref, r_ref, own_ref, out16_ref):
        s = g_ref[...] + r_ref[...]
        out16_ref[...] = s.astype(BF16)

        @pl.when(pl.program_id(1) == pos_ref[0])
        def _():
            own_ref[...] = s

    blk = pl.BlockSpec((None, tr, cols), lambda i, k, pos_ref: (k, i, 0))
    return pl.pallas_call(
        body, name=name,
        grid_spec=pltpu.PrefetchScalarGridSpec(
            num_scalar_prefetch=1, grid=(per, N_CHIP),
            in_specs=[pl.BlockSpec((None, tr, cols), lambda i, k, pos_ref: (k, pos_ref[1] * per + i, 0)), blk],
            out_specs=[pl.BlockSpec((tr, cols), lambda i, k, pos_ref: (i, 0)), blk]),
        out_shape=[jax.ShapeDtypeStruct((rh, cols), F32), jax.ShapeDtypeStruct(recv.shape, BF16)],
        compiler_params=_params(2))(pos, grad, recv)


def _chip_sum(name, own32, recv, pos):
    rh, cols = own32.shape
    tr = _row_tile(rh, cols)

    def body(pos_ref, own_ref, r_ref, out_ref):
        del pos_ref
        out_ref[...] = ((own_ref[...] + r_ref[0].astype(F32)) + r_ref[1].astype(F32)) + r_ref[2].astype(F32)

    return pl.pallas_call(
        body, name=name,
        grid_spec=pltpu.PrefetchScalarGridSpec(
            num_scalar_prefetch=1, grid=(rh // tr,),
            in_specs=[pl.BlockSpec((tr, cols), lambda i, pos_ref: (i, 0)),
                      pl.BlockSpec((3, tr, cols), lambda i, pos_ref: (0, i, 0))],
            out_specs=pl.BlockSpec((None, tr, cols), lambda i, pos_ref: (pos_ref[1], i, 0))),
        out_shape=jax.ShapeDtypeStruct((2, rh, cols), F32), compiler_params=_params(1))(pos, own32, recv)


def _cast_into_slot(name, w, layer, pos, after=()):
    _, rows, cols = w.shape
    tr = _row_tile(rows, cols)

    def body(pos_ref, w_ref, *rest):
        del pos_ref
        rest[-1][...] = w_ref[...].astype(BF16)

    return pl.pallas_call(
        body, name=name,
        grid_spec=pltpu.PrefetchScalarGridSpec(
            num_scalar_prefetch=1, grid=(rows // tr,),
            in_specs=[pl.BlockSpec((None, tr, cols), lambda i, pos_ref: (layer, i, 0))] + [ANY] * len(after),
            out_specs=pl.BlockSpec((None, tr, cols), lambda i, pos_ref: (pos_ref[0], i, 0))),
        out_shape=jax.ShapeDtypeStruct((N_CHIP, rows, cols), BF16), compiler_params=_params(1))(pos, w, *after)


def _adamw_math(w, g, m, v):
    m = ADAM_B1 * m + (1.0 - ADAM_B1) * g
    v = ADAM_B2 * v + (1.0 - ADAM_B2) * jnp.square(g)
    m_hat = m / (1.0 - ADAM_B1 ** ADAM_STEP)
    v_hat = v / (1.0 - ADAM_B2 ** ADAM_STEP)
    delta = -ADAM_LR * (m_hat / (jnp.sqrt(v_hat) + ADAM_EPS) + ADAM_WD * w)
    return delta, m, v


def _adamw_layers(name, w, m, v, grads, first, into=None, after=()):
    _, rows, cols = w.shape
    tr = _row_tile(rows, cols)
    n_layers = len(grads)

    def body(w_ref, m_ref, v_ref, *rest):
        g_refs, (grad_ref, d_ref, nm_ref, nv_ref) = rest[:n_layers], rest[len(rest) - 4:]
        layer = pl.program_id(0)
        g = g_refs[0][...]
        for l in range(1, n_layers):
            g = jnp.where(layer == l, g_refs[l][...], g)
        grad_ref[...] = g
        d_ref[...], nm_ref[...], nv_ref[...] = _adamw_math(w_ref[...], g, m_ref[...], v_ref[...])

    blk = pl.BlockSpec((None, tr, cols), lambda l, i: (first + l, i, 0))
    g_spec = lambda k: pl.BlockSpec((tr, cols), lambda l, i: (jnp.where(l == k, i, 0), 0))
    passed = list(into or []) + list(after)
    return pl.pallas_call(
        body, name=name, grid=(n_layers, rows // tr),
        in_specs=[blk, blk, blk] + [g_spec(k) for k in range(n_layers)] + [ANY] * len(passed), out_specs=[blk] * 4,
        out_shape=[jax.ShapeDtypeStruct(w.shape, F32)] * 4,
        input_output_aliases={3 + n_layers + t: t for t in range(4)} if into else {},
        compiler_params=_params(2))(w, m, v, *grads, *passed)


def _sum_devices(gathered):
    _, rows, cols = gathered.shape

    def body(g_ref, out_ref):
        s = g_ref[0]
        for d in range(1, 8):
            s = s + g_ref[d]
        out_ref[...] = s

    return pl.pallas_call(body, name="sum_devices", out_shape=jax.ShapeDtypeStruct((rows, cols), F32),
                          compiler_params=pltpu.CompilerParams(vmem_limit_bytes=VMEM_LIMIT_BYTES))(gathered)


def _adamw_flat(w, g, m, v):
    def body(w_ref, g_ref, m_ref, v_ref, d_ref, nm_ref, nv_ref):
        d_ref[...], nm_ref[...], nv_ref[...] = _adamw_math(w_ref[...], g_ref[...], m_ref[...], v_ref[...])

    return pl.pallas_call(body, name="adamw_small", out_shape=[jax.ShapeDtypeStruct(w.shape, F32)] * 3,
                          compiler_params=pltpu.CompilerParams(vmem_limit_bytes=VMEM_LIMIT_BYTES))(w, g, m, v)


SMALL_NAMES = ["g_mix", "lower_bounds", "g_hgrn_out", "w_conv", "sg_ln_g", "sg_ln_b", "w_sg", "b_sg", "g_ffn", "g_final"]
BIG_NAMES = ["w_in", "w_branch", "w_o", "w_ff1", "w_ff2"]
WEIGHT_ORDER = ["w_in", "g_mix", "lower_bounds", "g_hgrn_out", "w_conv", "sg_ln_g", "sg_ln_b", "w_sg", "b_sg", "w_branch",
                "w_o", "g_ffn", "w_ff1", "w_ff2", "g_final"]


def _padded_rows(n):
    return -(-n // SUBLANES) * SUBLANES


def _pack(arrays):
    parts = []
    for a in arrays:
        a = a.reshape(-1, LANES)
        parts.append(jnp.pad(a, ((0, _padded_rows(a.shape[0]) - a.shape[0]), (0, 0))))
    return jnp.concatenate(parts, axis=0)


def _unpack(flat, shapes):
    out, row = [], 0
    for s in shapes:
        n = int(np.prod(s)) // LANES
        out.append(flat[row:row + n].reshape(s))
        row += _padded_rows(n)
    return out


def _as_2d(name, a):
    return a.reshape(DEPTH, N_BRANCH * WIDTH, BRANCH_COLS) if name == "w_branch" else a


def kernel(x, w_in, g_mix, lower_bounds, g_hgrn_out, w_conv, sg_ln_g, sg_ln_b, w_sg, b_sg, w_branch, w_o, g_ffn, w_ff1, w_ff2, g_final, loss_target, m_w_in, m_g_mix, m_lower_bounds, m_g_hgrn_out, m_w_conv, m_sg_ln_g, m_sg_ln_b, m_w_sg, m_b_sg, m_w_branch, m_w_o, m_g_ffn, m_w_ff1, m_w_ff2, m_g_final, v_w_in, v_g_mix, v_lower_bounds, v_g_hgrn_out, v_w_conv, v_sg_ln_g, v_sg_ln_b, v_w_sg, v_b_sg, v_w_branch, v_w_o, v_g_ffn, v_w_ff1, v_w_ff2, v_g_final):
    weights = dict(w_in=w_in, g_mix=g_mix, lower_bounds=lower_bounds, g_hgrn_out=g_hgrn_out, w_conv=w_conv,
                   sg_ln_g=sg_ln_g, sg_ln_b=sg_ln_b, w_sg=w_sg, b_sg=b_sg, w_branch=w_branch, w_o=w_o, g_ffn=g_ffn,
                   w_ff1=w_ff1, w_ff2=w_ff2, g_final=g_final)
    mom1 = dict(w_in=m_w_in, g_mix=m_g_mix, lower_bounds=m_lower_bounds, g_hgrn_out=m_g_hgrn_out, w_conv=m_w_conv,
                sg_ln_g=m_sg_ln_g, sg_ln_b=m_sg_ln_b, w_sg=m_w_sg, b_sg=m_b_sg, w_branch=m_w_branch, w_o=m_w_o,
                g_ffn=m_g_ffn, w_ff1=m_w_ff1, w_ff2=m_w_ff2, g_final=m_g_final)
    mom2 = dict(w_in=v_w_in, g_mix=v_g_mix, lower_bounds=v_lower_bounds, g_hgrn_out=v_g_hgrn_out, w_conv=v_w_conv,
                sg_ln_g=v_sg_ln_g, sg_ln_b=v_sg_ln_b, w_sg=v_w_sg, b_sg=v_b_sg, w_branch=v_w_branch, w_o=v_w_o,
                g_ffn=v_g_ffn, w_ff1=v_w_ff1, w_ff2=v_w_ff2, g_final=v_g_final)
    xi, yi, ci = _mesh_pos()
    pos = jnp.stack([2 * xi + yi, ci]).astype(jnp.int32)
    device = 4 * xi + 2 * yi + ci
    conv_cols = w_conv.shape[2]

    conv_all = _gather_all("gather_w_conv", w_conv.reshape(DEPTH * CONV_K, conv_cols), device)
    conv_full = conv_all.reshape(N_CHIP, 2, DEPTH, CONV_K, conv_cols)[:, 0].transpose(1, 2, 0, 3).reshape(DEPTH, CONV_K, WIDTH)

    n_big = len(BIG_NAMES)
    flight, ici = {}, {}
    cast = lambda l, names, after: [_cast_into_slot("cast_" + n, _as_2d(n, weights[n]), l, pos, after=after) for n in names]
    early = _split_start("weights_ici_start_0_w_in", cast(0, BIG_NAMES[:1], (conv_all,)), _weight_ici_copies, 3)
    token = (early[3],)
    for l in range(DEPTH):
        names = BIG_NAMES[1:] if l == 0 else BIG_NAMES
        ici[l] = _split_start("weights_ici_start_%d" % l, cast(l, names, token), _weight_ici_copies, 3 * len(names), after=token)
        token = (ici[l][3],)
    lbs = _lbs_fwd(lower_bounds)

    act = x[0]
    normed = _rms_fwd("rms_mix", act, g_mix[0:1], after=token)
    layers = []

    def fetch(l, deps):
        landed = _split_wait("weights_ici_wait_%d" % l, ici.pop(l), _weight_ici_copies, after=deps)
        flight["d2d"] = _split_start("weights_d2d_start_%d" % l, landed, _weight_d2d_copies, 3 * len(landed))
        return (flight["d2d"][3],)

    landed = _split_wait("weights_ici_wait_0_w_in", early, _weight_ici_copies, after=[normed, lbs])
    early = _split_start("weights_d2d_start_0_w_in", landed, _weight_d2d_copies, 3)
    gathered = _split_wait("weights_d2d_wait_0_w_in", early, _weight_d2d_copies, after=[])

    def rest_of_layer_0(deps):
        fetch(0, deps)
        return _split_wait("weights_d2d_wait_0", flight.pop("d2d"), _weight_d2d_copies, after=[])

    for l in range(DEPTH):
        if l > 0:
            gathered = _split_wait("weights_d2d_wait_%d" % l, flight.pop("d2d"), _weight_d2d_copies, after=[act])
        late_weights = rest_of_layer_0 if l == 0 else (lambda deps, rest=gathered[1:]: rest)
        small = dict(g_mix=g_mix[l:l + 1], lbs=lbs[l:l + 1], g_hgrn_out=g_hgrn_out[l:l + 1], w_conv=conv_full[l],
                     sg_ln_g=sg_ln_g[l:l + 1], sg_ln_b=sg_ln_b[l:l + 1], w_sg=w_sg[l],
                     b_sg=b_sg[l].reshape(SG_GROUPS, SG_CHUNK, 1), g_ffn=g_ffn[l:l + 1])
        hook = (lambda deps, l=l: fetch(l + 1, deps)) if l + 1 < DEPTH else (lambda deps: ())
        act, normed, saved, wts = _layer_fwd(act, normed, gathered[0], late_weights, small,
                                             g_mix[l + 1:l + 2] if l + 1 < DEPTH else None, hook)
        layers.append((wts, small, saved))
    loss_blk, dact, dg_final = _loss_head(act, g_final.reshape(1, D_MODEL), loss_target[0])

    pipe = _GradPipeline(pos)
    small_grads = [None] * DEPTH
    after = ()
    for l in reversed(range(DEPTH)):
        wts, small, saved = layers[l]
        tick = lambda deps, new, l=l: pipe.tick(deps, [(nm, l, g) for nm, g in new])
        dact, small_grads[l], after = _layer_bwd(dact, saved, wts, small, tick, after)
    grad_x = dact[None]

    stack = lambda key, rows=None: jnp.stack([small_grads[l][key][0] if rows is None else small_grads[l][key][:rows]
                                              for l in range(DEPTH)])
    local_small = dict(
        g_mix=stack("g_mix"), lower_bounds=stack("lbs"), g_hgrn_out=stack("g_hgrn_out"), w_conv=stack("w_conv", CONV_K),
        sg_ln_g=stack("sg_ln_g"), sg_ln_b=stack("sg_ln_b"), w_sg=jnp.stack([small_grads[l]["w_sg"] for l in range(DEPTH)]),
        b_sg=jnp.stack([small_grads[l]["b_sg"].reshape(SG_GROUPS, SG_CHUNK) for l in range(DEPTH)]),
        g_ffn=stack("g_ffn"), g_final=dg_final[0])
    shapes = [local_small[n].shape for n in SMALL_NAMES] + [(SUBLANES, LANES)]
    summed = _sum_devices(_gather_all("gather_small_grads", _pack([local_small[n] for n in SMALL_NAMES] + [loss_blk]), device,
                                       after=after))
    parts = _unpack(summed, shapes)
    loss = parts[-1][0, 0]
    small_grad = dict(zip(SMALL_NAMES, parts[:-1]))
    small_grad["lower_bounds"] = _lbs_bwd(lower_bounds, small_grad["lower_bounds"])
    small_grad["w_conv"] = lax.dynamic_slice_in_dim(small_grad["w_conv"], pos[0] * conv_cols, conv_cols, axis=2)
    g_flat = _pack([small_grad[n] for n in SMALL_NAMES])
    d_flat, m_flat, v_flat = _adamw_flat(_pack([weights[n] for n in SMALL_NAMES]), g_flat,
                                         _pack([mom1[n] for n in SMALL_NAMES]), _pack([mom2[n] for n in SMALL_NAMES]))
    small_shapes = [weights[n].shape for n in SMALL_NAMES]
    grads = dict(small_grad)
    delta = dict(zip(SMALL_NAMES, _unpack(d_flat, small_shapes)))
    new_m = dict(zip(SMALL_NAMES, _unpack(m_flat, small_shapes)))
    new_v = dict(zip(SMALL_NAMES, _unpack(v_flat, small_shapes)))

    def adam(n, first, layer_grads, into=None, after=()):
        return _adamw_layers("adamw_%s_%d" % (n, first), _as_2d(n, weights[n]), _as_2d(n, mom1[n]), _as_2d(n, mom2[n]),
                             layer_grads, first, into, after)

    done = {}
    token = pipe.tick([summed], [])
    for n in ("w_ff1", "w_ff2"):
        done[n] = adam(n, 0, pipe.reduced[n], after=token)
    token = pipe.tick([done["w_ff2"][1]], [])
    for n in ("w_o", "w_branch"):
        done[n] = adam(n, 0, pipe.reduced[n], after=token)
    rest = adam("w_in", 1, pipe.reduced["w_in"][1:], after=token)
    pipe.tick([rest[1]], [])
    assert not pipe.busy()
    done["w_in"] = adam("w_in", 0, pipe.reduced["w_in"][:1], into=rest)
    for n in BIG_NAMES:
        grads[n], delta[n], new_m[n], new_v[n] = [o.reshape(weights[n].shape) for o in done[n]]

    return (loss, grad_x, *[grads[n] for n in WEIGHT_ORDER], *[delta[n] for n in WEIGHT_ORDER],
            *[new_m[n] for n in WEIGHT_ORDER], *[new_v[n] for n in WEIGHT_ORDER])
```

```python
import numpy as np
import jax
import jax.numpy as jnp
from jax import lax
from jax.experimental import pallas as pl
from jax.experimental.pallas import tpu as pltpu

F32, BF16 = jnp.float32, jnp.bfloat16

D_MODEL = 1024
WIDTH = 512
N_BRANCH = 3
N_HEAD = 4
HEAD = 128
H_CHUNK = 64
CONV_K = 3
SG_CHUNK = 128
SG_GROUPS = 4
D_FF = 4096
DEPTH = 4
N_CHIP = 4
IN_COLS = 9 * WIDTH + N_BRANCH * D_MODEL
GATE_COL0 = 9 * WIDTH
LB_FLOOR = 1e-30
NORM_EPS = 1e-6
LN_EPS = 1e-5
ADAM_LR, ADAM_B1, ADAM_B2, ADAM_EPS, ADAM_WD, ADAM_STEP = 0.001, 0.9, 0.999, 1e-08, 0.01, 10

VMEM_LIMIT_BYTES = 56 * 1024 * 1024
VMEM_BLOCK_BUDGET = 44 * 1024 * 1024
SUBLANES, LANES = 8, 128
ELEMWISE_BLOCK_BYTES = 2 * 1024 * 1024

NN = (((1,), (0,)), ((), ()))
NT = (((1,), (1,)), ((), ()))
TN = (((0,), (0,)), ((), ()))
MESH = pl.DeviceIdType.MESH
ANY = pl.BlockSpec(memory_space=pl.ANY)


def _dot(a, b, dims=NN):
    return lax.dot_general(a.astype(BF16), b.astype(BF16), dims, preferred_element_type=F32)


def _params(n_axes):
    return pltpu.CompilerParams(dimension_semantics=("arbitrary",) * n_axes, vmem_limit_bytes=VMEM_LIMIT_BYTES)


def _row0(part, rows=SUBLANES):
    r = lax.broadcasted_iota(jnp.int32, (rows, part.shape[1]), 0)
    return jnp.where(r == 0, part, 0.0)


def _token_tile(T):
    return min(512, T)


def _matmul(name, a, b, *, dims, grid, a_spec, b_spec, out_specs, out_shapes, acc_shape,
            extra=(), extra_specs=(), epilogue=None, after=()):
    nk = grid[2]
    n_extra, n_out, n_in = len(extra), len(out_shapes), 2 + len(extra) + len(after)
    one_step = nk == 1

    def body(*refs):
        a_ref, b_ref = refs[0], refs[1]
        ex = refs[2:2 + n_extra]
        outs = refs[n_in:n_in + n_out]
        part = _dot(a_ref[...], b_ref[...], dims)

        def finish(total):
            res = epilogue(total, *[e[...] for e in ex]) if epilogue else (total,)
            for o, r in zip(outs, res):
                o[...] = r.astype(o.dtype)

        if one_step:
            finish(part)
            return
        acc = refs[-1]
        kk = pl.program_id(2)

        @pl.when(kk == 0)
        def _():
            acc[...] = part

        @pl.when(kk > 0)
        def _():
            acc[...] += part

        @pl.when(kk == nk - 1)
        def _():
            finish(acc[...])

    return pl.pallas_call(
        body, name=name, grid=grid,
        in_specs=[a_spec, b_spec, *extra_specs, *[ANY] * len(after)], out_specs=list(out_specs),
        out_shape=list(out_shapes), scratch_shapes=[] if one_step else [pltpu.VMEM(acc_shape, F32)],
        compiler_params=_params(3),
    )(a, b, *extra, *after)


def _mm_cols(name, a, w, out_dtypes, epilogue=None, extra=()):
    T, K = a.shape
    N = w.shape[2]
    tm = min(2 * _token_tile(T), T)
    blk = pl.BlockSpec((tm, N), lambda j, i, kk: (i, j))
    return _matmul(
        name, a, w, dims=NN, grid=(N_CHIP, T // tm, 1),
        a_spec=pl.BlockSpec((tm, K), lambda j, i, kk: (i, 0)),
        b_spec=pl.BlockSpec((None, K, N), lambda j, i, kk: (j, 0, 0)),
        out_specs=[blk] * len(out_dtypes),
        out_shapes=[jax.ShapeDtypeStruct((T, N_CHIP * N), dt) for dt in out_dtypes],
        acc_shape=(tm, N), extra=extra, extra_specs=[blk] * len(extra), epilogue=epilogue)


def _mm_rows(name, a, w, res, norm_gain=None, after=()):
    T = a.shape[0]
    K, N = N_CHIP * w.shape[1], w.shape[2]
    tm = _token_tile(T)
    blk = pl.BlockSpec((tm, N), lambda i, j, kk: (i, 0))

    def with_norm(acc, r, gain):
        xv = acc + r
        return xv, xv * lax.rsqrt(jnp.mean(xv * xv, axis=-1, keepdims=True) + NORM_EPS) * gain

    normed = norm_gain is not None
    outs = _matmul(
        name, a, w.reshape(K, N), dims=NN, grid=(T // tm, 1, 1),
        a_spec=pl.BlockSpec((tm, K), lambda i, j, kk: (i, 0)),
        b_spec=pl.BlockSpec((K, N), lambda i, j, kk: (0, 0)),
        out_specs=[blk] * (2 if normed else 1),
        out_shapes=[jax.ShapeDtypeStruct((T, N), F32)] + ([jax.ShapeDtypeStruct((T, N), BF16)] if normed else []),
        acc_shape=(tm, N), extra=(res, norm_gain) if normed else (res,),
        extra_specs=[blk] + ([pl.BlockSpec((1, N), lambda i, j, kk: (0, 0))] if normed else []),
        epilogue=with_norm if normed else (lambda acc, r: (acc + r,)), after=after)
    return outs if normed else outs[0]


def _mm_cols_t(name, g, w, out_dtype, epilogue=None, extra=(), after=()):
    T, N = g.shape
    K = N_CHIP * w.shape[1]
    tm = _token_tile(T) if K <= 2 * D_MODEL else _token_tile(T) // 2
    blk = pl.BlockSpec((tm, K), lambda i, j, kk: (i, 0))
    return _matmul(
        name, g, w.reshape(K, N), dims=NT, grid=(T // tm, 1, 1),
        a_spec=pl.BlockSpec((tm, N), lambda i, j, kk: (i, 0)),
        b_spec=pl.BlockSpec((K, N), lambda i, j, kk: (0, 0)),
        out_specs=[blk], out_shapes=[jax.ShapeDtypeStruct((T, K), out_dtype)], acc_shape=(tm, K),
        extra=extra, extra_specs=[blk] * len(extra), epilogue=epilogue, after=after)[0]


def _dgrad_norm_bwd(name, g, w, x, gain, dres):
    T = g.shape[0]
    K, N = w.shape[1], w.shape[2]
    tm = _token_tile(T)

    def body(g_ref, w_ref, x_ref, gain_ref, dres_ref, dx_ref, dgain_ref, acc):
        i, kk = pl.program_id(0), pl.program_id(1)
        part = _dot(g_ref[...], w_ref[...], NT)

        @pl.when(kk == 0)
        def _():
            acc[...] = part

        @pl.when(kk > 0)
        def _():
            acc[...] += part

        @pl.when(kk == N_CHIP - 1)
        def _():
            dhv, xv = acc[...], x_ref[...]
            r = lax.rsqrt(jnp.mean(xv * xv, axis=-1, keepdims=True) + NORM_EPS)
            xn = xv * r
            dxn = dhv * gain_ref[...]
            dx_ref[...] = dres_ref[...] + r * (dxn - xn * jnp.mean(dxn * xn, axis=-1, keepdims=True))

            @pl.when(i == 0)
            def _():
                dgain_ref[...] = jnp.zeros_like(dgain_ref)

            dgain_ref[...] += _row0(jnp.sum(dhv * xn, axis=0, keepdims=True))

    tile = pl.BlockSpec((tm, K), lambda i, kk: (i, 0))
    return pl.pallas_call(
        body, name=name, grid=(T // tm, N_CHIP),
        in_specs=[pl.BlockSpec((tm, N), lambda i, kk: (i, kk)), pl.BlockSpec((None, K, N), lambda i, kk: (kk, 0, 0)),
                  tile, pl.BlockSpec((1, K), lambda i, kk: (0, 0)), tile],
        out_specs=[tile, pl.BlockSpec((SUBLANES, K), lambda i, kk: (0, 0))],
        out_shape=[jax.ShapeDtypeStruct((T, K), F32), jax.ShapeDtypeStruct((SUBLANES, K), F32)],
        scratch_shapes=[pltpu.VMEM((tm, K), F32)], compiler_params=_params(2))(g, w, x, gain, dres)


def _mm_wgrad(name, a, g, a_cols, g_cols, a_blocked, g_blocked, after=()):
    T = a.shape[0]
    tt = T
    while tt > LANES and 2 * 2 * tt * (a_cols + g_cols) + (2 if tt == T else 3) * 4 * a_cols * g_cols > VMEM_BLOCK_BUDGET:
        tt //= 2
    return _matmul(
        name, a, g, dims=TN, grid=(N_CHIP, 1, T // tt),
        a_spec=pl.BlockSpec((tt, a_cols), (lambda j, i, kk: (kk, j)) if a_blocked else (lambda j, i, kk: (kk, 0))),
        b_spec=pl.BlockSpec((tt, g_cols), (lambda j, i, kk: (kk, j)) if g_blocked else (lambda j, i, kk: (kk, 0))),
        out_specs=[pl.BlockSpec((None, a_cols, g_cols), lambda j, i, kk: (j, 0, 0))],
        out_shapes=[jax.ShapeDtypeStruct((N_CHIP, a_cols, g_cols), F32)], acc_shape=(a_cols, g_cols), after=after)[0]


def _rms_fwd(name, x, g, after=()):
    T, Dm = x.shape
    tm = min(256, T)

    def body(x_ref, g_ref, *rest):
        xv = x_ref[...]
        r = lax.rsqrt(jnp.mean(xv * xv, axis=-1, keepdims=True) + NORM_EPS)
        rest[-1][...] = (xv * r * g_ref[...]).astype(BF16)

    return pl.pallas_call(
        body, name=name, grid=(T // tm,),
        in_specs=[pl.BlockSpec((tm, Dm), lambda i: (i, 0)), pl.BlockSpec((1, Dm), lambda i: (0, 0))] + [ANY] * len(after),
        out_specs=pl.BlockSpec((tm, Dm), lambda i: (i, 0)),
        out_shape=jax.ShapeDtypeStruct((T, Dm), BF16), compiler_params=_params(1))(x, g, *after)


def _loss_head(x, g, tgt):
    T, Dm = x.shape
    tm = min(256, T)

    def body(x_ref, g_ref, t_ref, loss_ref, dx_ref, dg_ref):
        xv = x_ref[...]
        gv = g_ref[...]
        r = lax.rsqrt(jnp.mean(xv * xv, axis=-1, keepdims=True) + NORM_EPS)
        xn = xv * r
        err = xn * gv - t_ref[...]
        dy = err * (1.0 / Dm)
        dxn = dy * gv
        dx_ref[...] = r * (dxn - xn * jnp.mean(dxn * xn, axis=-1, keepdims=True))

        @pl.when(pl.program_id(0) == 0)
        def _():
            dg_ref[...] = jnp.zeros_like(dg_ref)
            loss_ref[...] = jnp.zeros_like(loss_ref)

        dg_ref[...] += _row0(jnp.sum(dy * xn, axis=0, keepdims=True))
        part = jnp.sum(jnp.sum(err * err, axis=-1, keepdims=True), axis=0, keepdims=True) * (0.5 / Dm)
        loss_ref[...] += jnp.broadcast_to(part, loss_ref.shape)

    tile = pl.BlockSpec((tm, Dm), lambda i: (i, 0))
    return pl.pallas_call(
        body, name="loss_head", grid=(T // tm,),
        in_specs=[tile, pl.BlockSpec((1, Dm), lambda i: (0, 0)), tile],
        out_specs=[pl.BlockSpec((SUBLANES, LANES), lambda i: (0, 0)), tile,
                   pl.BlockSpec((SUBLANES, Dm), lambda i: (0, 0))],
        out_shape=[jax.ShapeDtypeStruct((SUBLANES, LANES), F32), jax.ShapeDtypeStruct((T, Dm), F32),
                   jax.ShapeDtypeStruct((SUBLANES, Dm), F32)],
        compiler_params=_params(1))(x, g, tgt)


def _softmax_rows(lb_ref):
    rows = [lb_ref[pl.ds(i, 1), :] for i in range(DEPTH)]
    mx = rows[0]
    for r in rows[1:]:
        mx = jnp.maximum(mx, r)
    es = [jnp.exp(r - mx) for r in rows]
    tot = es[0]
    for e in es[1:]:
        tot = tot + e
    return [e / tot for e in es]


def _lbs_fwd(lower_bounds):
    def body(lb_ref, out_ref):
        sm = _softmax_rows(lb_ref)
        run = jnp.zeros_like(sm[0])
        out_ref[pl.ds(0, 1), :] = run
        for i in range(1, DEPTH):
            run = run + sm[i]
            out_ref[pl.ds(i, 1), :] = run

    return pl.pallas_call(body, name="lbs_fwd", out_shape=jax.ShapeDtypeStruct(lower_bounds.shape, F32))(lower_bounds)


def _lbs_bwd(lower_bounds, dlbs):
    def body(lb_ref, d_ref, out_ref):
        sm = _softmax_rows(lb_ref)
        dsm = [jnp.zeros_like(sm[0])]
        for i in range(1, DEPTH):
            acc = d_ref[pl.ds(i, 1), :]
            for l in range(i + 1, DEPTH):
                acc = acc + d_ref[pl.ds(l, 1), :]
            dsm.append(acc)
        inner = dsm[0] * sm[0]
        for i in range(1, DEPTH):
            inner = inner + dsm[i] * sm[i]
        for i in range(DEPTH):
            out_ref[pl.ds(i, 1), :] = sm[i] * (dsm[i] - inner)

    return pl.pallas_call(body, name="lbs_bwd", out_shape=jax.ShapeDtypeStruct(lower_bounds.shape, F32))(lower_bounds, dlbs)


N_LEVEL = 6


def _hgrn_consts():
    L = H_CHUNK
    t = np.arange(L)
    blocks = [(t[:, None] >= t[None, :]).astype(np.float32)]
    masks = []
    m = L // 2
    while m >= 1:
        blk, pos = t // (2 * m), t % (2 * m)
        start = blk * 2 * m
        mat = np.zeros((L, L), np.float32)
        for r in range(L):
            if pos[r] >= m:
                mat[r, start[r] + m:r + 1] = 1.0
            else:
                mat[r, r + 1:start[r] + m] = -1.0
        blocks.append(mat)
        masks.append(((blk[:, None] == blk[None, :]) & (pos[:, None] >= m) & (pos[None, :] < m)).astype(np.float32))
        m //= 2
    blocks.append(np.ones((L, L), np.float32))
    return jnp.asarray(np.concatenate(blocks, 0), BF16), jnp.asarray(np.stack(masks), F32)


def _hgrn_core(qraw, fp, lb, sum_mat, mask_ref):
    L = H_CHUNK
    sq = jax.nn.sigmoid(qraw)
    q = qraw * sq
    sneg = jax.nn.sigmoid(-fp)
    log_sig = jnp.minimum(fp, 0.0) - jnp.log1p(jnp.exp(-jnp.abs(fp)))
    a1 = jnp.log(jnp.maximum(lb, LB_FLOOR))
    a2 = jnp.log1p(-lb) + log_sig
    logf = jnp.maximum(a1, a2) + jnp.log1p(jnp.exp(-jnp.abs(a1 - a2)))
    w1 = jnp.exp(a1 - logf)
    w2 = jnp.exp(a2 - logf)
    k = (1.0 - lb) * sneg
    hi = logf.astype(BF16)
    r1 = logf - hi.astype(F32)
    mid = r1.astype(BF16)
    lo = (r1 - mid.astype(F32)).astype(BF16)
    sums = lax.dot_general(sum_mat, jnp.concatenate([hi, mid, lo], axis=1), NN, preferred_element_type=F32)
    sums = sums[:, 0:HEAD] + sums[:, HEAD:2 * HEAD] + sums[:, 2 * HEAD:3 * HEAD]
    b = sums[0:L]
    b_last = sums[(N_LEVEL + 1) * L:(N_LEVEL + 2) * L]
    eye = lax.broadcasted_iota(jnp.int32, (L, L), 0) == lax.broadcasted_iota(jnp.int32, (L, L), 1)
    attn = jnp.where(eye, jnp.sum(q * k, axis=1, keepdims=True), 0.0)
    fa, fb, ea, eb = [], [], [], []
    for l in range(N_LEVEL):
        d = sums[(l + 1) * L:(l + 2) * L]
        e_a = jnp.exp(jnp.minimum(d, 0.0))
        e_b = jnp.exp(jnp.minimum(-d, 0.0))
        a_l, b_l = q * e_a, k * e_b
        attn = attn + mask_ref[l] * _dot(a_l, b_l, NT)
        fa.append(a_l), fb.append(b_l), ea.append(e_a), eb.append(e_b)
    return dict(sq=sq, q=q, sneg=sneg, logf=logf, w1=w1, w2=w2, k=k, b=b, b_last=b_last, attn=attn,
                fa=fa, fb=fb, ea=ea, eb=eb)


def _hgrn_fwd(p, lbrow, gout):
    T = p.shape[0]
    nch = T // H_CHUNK
    sum_mat, masks = _hgrn_consts()

    def body(p_ref, lb_ref, g_ref, m_ref, mask_ref, o_ref, z_ref, st_ref, state):
        @pl.when(pl.program_id(0) == 0)
        def _():
            state[...] = jnp.zeros_like(state)

        sum_m = m_ref[...]
        for h in range(N_HEAD):
            col = lambda part: pl.ds(part * WIDTH + h * HEAD, HEAD)
            hs = pl.ds(h * HEAD, HEAD)
            v = p_ref[:, col(2)]
            c = _hgrn_core(p_ref[:, col(0)], p_ref[:, col(1)], lb_ref[:, hs], sum_m, mask_ref)
            s0 = state[h]
            st_ref[h] = s0
            o = _dot(c["attn"], v) + _dot(c["q"] * jnp.exp(c["b"]), s0, NT)
            k_dec = c["k"] * jnp.exp(c["b_last"] - c["b"])
            decay = jnp.exp(jnp.max(c["b_last"], axis=0, keepdims=True))
            state[h] = s0 * decay + _dot(v, k_dec, TN)
            o_ref[:, hs] = o
            r = lax.rsqrt(jnp.mean(o * o, axis=-1, keepdims=True) + NORM_EPS)
            z_ref[:, hs] = (o * r * g_ref[:, hs] * jax.nn.sigmoid(p_ref[:, col(3)])).astype(BF16)

    full = lambda shape: pl.BlockSpec(shape, lambda c: (0,) * len(shape))
    return pl.pallas_call(
        body, name="hgrn_fwd", grid=(nch,),
        in_specs=[pl.BlockSpec((H_CHUNK, 4 * WIDTH), lambda c: (c, 0)), full((1, WIDTH)), full((1, WIDTH)),
                  full(sum_mat.shape), full(masks.shape)],
        out_specs=[pl.BlockSpec((H_CHUNK, WIDTH), lambda c: (c, 0)),
                   pl.BlockSpec((None, H_CHUNK, WIDTH), lambda c: (0, c, 0)),
                   pl.BlockSpec((None, N_HEAD, HEAD, HEAD), lambda c: (c, 0, 0, 0))],
        out_shape=[jax.ShapeDtypeStruct((T, WIDTH), F32), jax.ShapeDtypeStruct((N_BRANCH, T, WIDTH), BF16),
                   jax.ShapeDtypeStruct((nch, N_HEAD, HEAD, HEAD), F32)],
        scratch_shapes=[pltpu.VMEM((N_HEAD, HEAD, HEAD), F32)], compiler_params=_params(1),
    )(p, lbrow, gout, sum_mat, masks)


def _hgrn_bwd(p, o_saved, dz, states, lbrow, gout, dp, after=()):
    T = p.shape[0]
    nch = T // H_CHUNK
    L = H_CHUNK
    sum_mat, masks = _hgrn_consts()

    def body(p_ref, o_ref, dz_ref, st_ref, lb_ref, g_ref, m_ref, mask_ref, dp_in, *rest):
        del dp_in
        dp_ref, dlb_ref, dg_ref, dstate = rest[len(after):]

        @pl.when(pl.program_id(0) == 0)
        def _():
            dstate[...] = jnp.zeros_like(dstate)
            dlb_ref[...] = jnp.zeros_like(dlb_ref)
            dg_ref[...] = jnp.zeros_like(dg_ref)

        sum_m = m_ref[...]
        for h in range(N_HEAD):
            col = lambda part: pl.ds(part * WIDTH + h * HEAD, HEAD)
            hs = pl.ds(h * HEAD, HEAD)
            qraw, fp, v, go = p_ref[:, col(0)], p_ref[:, col(1)], p_ref[:, col(2)], p_ref[:, col(3)]
            lb, g = lb_ref[:, hs], g_ref[:, hs]
            c = _hgrn_core(qraw, fp, lb, sum_m, mask_ref)
            q, k, b, b_last = c["q"], c["k"], c["b"], c["b_last"]
            s0, ds1 = st_ref[h], dstate[h]
            e_b = jnp.exp(b)
            q_dec = q * e_b
            e_bl = jnp.exp(b_last - b)
            k_dec = k * e_bl
            decay = jnp.exp(jnp.max(b_last, axis=0, keepdims=True))
            o = o_ref[:, hs]
            r = lax.rsqrt(jnp.mean(o * o, axis=-1, keepdims=True) + NORM_EPS)
            n = o * r
            sgo = jax.nn.sigmoid(go)
            dza = dz_ref[:, hs]
            dgo = dza * n * g * sgo * (1.0 - sgo)
            dg_ref[:, hs] += _row0(jnp.sum(dza * n * sgo, axis=0, keepdims=True))
            dn = dza * g * sgo
            do = r * (dn - n * jnp.mean(dn * n, axis=-1, keepdims=True))
            dattn = _dot(do, v, NT)
            dv = _dot(c["attn"], do, TN) + _dot(k_dec, ds1, NT)
            dq_dec = _dot(do, s0)
            dk_dec = _dot(v, ds1)
            ddiag = jnp.sum(do * v, axis=1, keepdims=True)
            dq = dq_dec * e_b + ddiag * k
            dk = dk_dec * e_bl + ddiag * q
            dsums = [dq_dec * q_dec - dk_dec * k_dec]
            for l in range(N_LEVEL):
                dm = mask_ref[l] * dattn
                da = _dot(dm, c["fb"][l])
                db = _dot(dm, c["fa"][l], TN)
                dq = dq + da * c["ea"][l]
                dk = dk + db * c["eb"][l]
                dsums.append(da * c["fa"][l] - db * c["fb"][l])
            dlast = jnp.sum(ds1 * s0, axis=0, keepdims=True) * decay
            dsums.append(dk_dec * k_dec + _row0(dlast, L))
            dlogf = _dot(sum_m, jnp.concatenate(dsums, axis=0), TN)
            dstate[h] = ds1 * decay + _dot(do, q_dec, TN)
            sq, sneg = c["sq"], c["sneg"]
            dqraw = dq * sq * (1.0 + qraw * (1.0 - sq))
            dfp = dlogf * c["w2"] * sneg - dk * (1.0 - lb) * sneg * (1.0 - sneg)
            inv_lb = jnp.where(lb > LB_FLOOR, 1.0 / jnp.maximum(lb, LB_FLOOR), 0.0)
            dlb_tok = dlogf * (c["w1"] * inv_lb - c["w2"] / (1.0 - lb)) - dk * sneg
            dlb_ref[:, hs] += _row0(jnp.sum(dlb_tok, axis=0, keepdims=True))
            dp_ref[:, col(0)] = dqraw.astype(BF16)
            dp_ref[:, col(1)] = dfp.astype(BF16)
            dp_ref[:, col(2)] = dv.astype(BF16)
            dp_ref[:, col(3)] = dgo.astype(BF16)

    full = lambda shape: pl.BlockSpec(shape, lambda c: (0,) * len(shape))
    rev = lambda c: nch - 1 - c
    return pl.pallas_call(
        body, name="hgrn_bwd", grid=(nch,),
        in_specs=[pl.BlockSpec((L, 4 * WIDTH), lambda c: (rev(c), 0)), pl.BlockSpec((L, WIDTH), lambda c: (rev(c), 0)),
                  pl.BlockSpec((None, L, WIDTH), lambda c: (0, rev(c), 0)),
                  pl.BlockSpec((None, N_HEAD, HEAD, HEAD), lambda c: (rev(c), 0, 0, 0)),
                  full((1, WIDTH)), full((1, WIDTH)), full(sum_mat.shape), full(masks.shape), ANY, *[ANY] * len(after)],
        out_specs=[pl.BlockSpec((L, 4 * WIDTH), lambda c: (rev(c), 0)), full((SUBLANES, WIDTH)), full((SUBLANES, WIDTH))],
        out_shape=[jax.ShapeDtypeStruct(dp.shape, dp.dtype), jax.ShapeDtypeStruct((SUBLANES, WIDTH), F32),
                   jax.ShapeDtypeStruct((SUBLANES, WIDTH), F32)],
        scratch_shapes=[pltpu.VMEM((N_HEAD, HEAD, HEAD), F32)], input_output_aliases={8: 0},
        compiler_params=_params(1),
    )(p, o_saved, dz, states, lbrow, gout, sum_mat, masks, dp, *after)


def _shift_down(tile, halo, s):
    tm = tile.shape[0]
    rows = lax.broadcasted_iota(jnp.int32, tile.shape, 0)
    head = jnp.concatenate([pltpu.roll(halo, s, 0), jnp.zeros((tm - SUBLANES, tile.shape[1]), tile.dtype)], axis=0)
    return jnp.where(rows < s, head, pltpu.roll(tile, s, 0))


def _shift_up(tile, halo, s):
    tm = tile.shape[0]
    rows = lax.broadcasted_iota(jnp.int32, tile.shape, 0)
    tail = jnp.concatenate([jnp.zeros((tm - SUBLANES, tile.shape[1]), tile.dtype), pltpu.roll(halo, SUBLANES - s, 0)], axis=0)
    return jnp.where(rows >= tm - s, tail, pltpu.roll(tile, tm - s, 0))


def _conv_fwd(p, w, z, after=()):
    T = p.shape[0]
    tm = _token_tile(T)
    per = tm // SUBLANES

    def body(bg_ref, cg_ref, xc_ref, hcg_ref, hxc_ref, w_ref, *rest):
        z_ref = rest[-1]
        zc = cg_ref[...] * xc_ref[...]
        hz = jnp.where(pl.program_id(0) > 0, hcg_ref[...] * hxc_ref[...], 0.0)
        y = (w_ref[pl.ds(0, 1), :] * _shift_down(zc, hz, 2) + w_ref[pl.ds(1, 1), :] * _shift_down(zc, hz, 1)
             + w_ref[pl.ds(2, 1), :] * zc)
        z_ref[...] = (bg_ref[...] * y).astype(BF16)

    tile = lambda cb: pl.BlockSpec((tm, WIDTH), lambda i: (i, cb))
    prev = lambda cb: pl.BlockSpec((SUBLANES, WIDTH), lambda i: (jnp.maximum(i * per - 1, 0), cb))
    return pl.pallas_call(
        body, name="conv_fwd", grid=(T // tm,),
        in_specs=[tile(4), tile(5), tile(6), prev(5), prev(6), pl.BlockSpec((CONV_K, WIDTH), lambda i: (0, 0)), ANY,
                  *[ANY] * len(after)],
        out_specs=pl.BlockSpec((None, tm, WIDTH), lambda i: (1, i, 0)),
        out_shape=jax.ShapeDtypeStruct(z.shape, z.dtype), input_output_aliases={6: 0}, compiler_params=_params(1),
    )(p, p, p, p, p, w, z, *after)


def _conv_bwd(p, w, dz, dp):
    T = p.shape[0]
    tm = _token_tile(T)
    per = tm // SUBLANES
    last = T // SUBLANES - 1

    def body(bg_ref, cg_ref, xc_ref, hcg_ref, hxc_ref, nbg_ref, dzb_ref, ndzb_ref, w_ref, dp_in, dp_ref, dw_ref, stash):
        del dp_in
        i, jj = pl.program_id(0), pl.program_id(1)

        @pl.when(jnp.logical_and(i == 0, jj == 0))
        def _():
            dw_ref[...] = jnp.zeros_like(dw_ref)

        @pl.when(jj == 0)
        def _():
            cg, xc, bg = cg_ref[...], xc_ref[...], bg_ref[...]
            w0, w1, w2 = w_ref[pl.ds(0, 1), :], w_ref[pl.ds(1, 1), :], w_ref[pl.ds(2, 1), :]
            zc = cg * xc
            hz = jnp.where(i > 0, hcg_ref[...] * hxc_ref[...], 0.0)
            z2, z1 = _shift_down(zc, hz, 2), _shift_down(zc, hz, 1)
            y = w0 * z2 + w1 * z1 + w2 * zc
            dzb = dzb_ref[...]
            dy = dzb * bg
            hdy = jnp.where(i < pl.num_programs(0) - 1, ndzb_ref[...] * nbg_ref[...], 0.0)
            dzc = w2 * dy + w1 * _shift_up(dy, hdy, 1) + w0 * _shift_up(dy, hdy, 2)
            rows = lax.broadcasted_iota(jnp.int32, (SUBLANES, WIDTH), 0)
            colsum = lambda t: jnp.sum(t, axis=0, keepdims=True)
            dw_ref[...] += (jnp.where(rows == 0, colsum(dy * z2), 0.0) + jnp.where(rows == 1, colsum(dy * z1), 0.0)
                            + jnp.where(rows == 2, colsum(dy * zc), 0.0))
            dp_ref[...] = (dzb * y).astype(BF16)
            stash[0] = dzc * xc
            stash[1] = dzc * cg

        @pl.when(jj > 0)
        def _():
            dp_ref[...] = stash[jj - 1].astype(BF16)

    n_tiles = T // tm
    tile = lambda cb: pl.BlockSpec((tm, WIDTH), lambda i, jj: (i, cb))
    prev = lambda cb: pl.BlockSpec((SUBLANES, WIDTH), lambda i, jj: (jnp.maximum(i * per - 1, 0), cb))
    nxt = lambda i: jnp.minimum((i + 1) * per, last)
    return pl.pallas_call(
        body, name="conv_bwd", grid=(n_tiles, 3),
        in_specs=[tile(4), tile(5), tile(6), prev(5), prev(6),
                  pl.BlockSpec((SUBLANES, WIDTH), lambda i, jj: (nxt(i), 4)),
                  pl.BlockSpec((None, tm, WIDTH), lambda i, jj: (1, i, 0)),
                  pl.BlockSpec((None, SUBLANES, WIDTH), lambda i, jj: (1, nxt(i), 0)),
                  pl.BlockSpec((CONV_K, WIDTH), lambda i, jj: (0, 0)), ANY],
        out_specs=[pl.BlockSpec((tm, WIDTH), lambda i, jj: (i, 4 + jj)),
                   pl.BlockSpec((SUBLANES, WIDTH), lambda i, jj: (0, 0))],
        out_shape=[jax.ShapeDtypeStruct(dp.shape, dp.dtype), jax.ShapeDtypeStruct((SUBLANES, WIDTH), F32)],
        scratch_shapes=[pltpu.VMEM((2, tm, WIDTH), F32)], input_output_aliases={9: 0}, compiler_params=_params(2),
    )(p, p, p, p, p, p, dz, dz, w, dp)


GELU_C = float(np.sqrt(2.0 / np.pi))
GELU_A = 0.044715


def _gelu(x):
    th = jnp.tanh(GELU_C * (x + GELU_A * x * x * x))
    return 0.5 * x * (1.0 + th), th


def _gelu_grad(x, th):
    return 0.5 * (1.0 + th) + 0.5 * x * (1.0 - th * th) * GELU_C * (1.0 + 3.0 * GELU_A * x * x)


def _sg_core(u, v, lng, lnb, ws_ref, bs_ref):
    gu, thu = _gelu(u)
    gv, thv = _gelu(v)
    xc = gv - jnp.mean(gv, axis=-1, keepdims=True)
    rs = lax.rsqrt(jnp.mean(xc * xc, axis=-1, keepdims=True) + LN_EPS)
    xh = xc * rs
    vp = xh * lng + lnb
    tril = (lax.broadcasted_iota(jnp.int32, (SG_CHUNK, SG_CHUNK), 0)
            >= lax.broadcasted_iota(jnp.int32, (SG_CHUNK, SG_CHUNK), 1))
    wm = [jnp.where(tril, ws_ref[g], 0.0).astype(BF16) for g in range(SG_GROUPS)]
    gs = lambda t, g: t[:, g * LANES:(g + 1) * LANES]
    sv = jnp.concatenate([_dot(wm[g], gs(vp, g)) + bs_ref[g] for g in range(SG_GROUPS)], axis=1)
    return dict(gu=gu, thu=thu, thv=thv, rs=rs, xh=xh, vp=vp, tril=tril, wm=wm, sv=sv)


def _sg_fwd(p, lng, lnb, ws, bs, z):
    T = p.shape[0]

    def body(u_ref, v_ref, lng_ref, lnb_ref, ws_ref, bs_ref, z_in, z_ref):
        del z_in
        c = _sg_core(u_ref[...], v_ref[...], lng_ref[...], lnb_ref[...], ws_ref, bs_ref)
        z_ref[...] = (c["gu"] * c["sv"]).astype(BF16)

    full = lambda shape: pl.BlockSpec(shape, lambda c: (0,) * len(shape))
    return pl.pallas_call(
        body, name="sg_fwd", grid=(T // SG_CHUNK,),
        in_specs=[pl.BlockSpec((SG_CHUNK, WIDTH), lambda c: (c, 7)), pl.BlockSpec((SG_CHUNK, WIDTH), lambda c: (c, 8)),
                  full((1, WIDTH)), full((1, WIDTH)), full(ws.shape), full(bs.shape), ANY],
        out_specs=pl.BlockSpec((None, SG_CHUNK, WIDTH), lambda c: (2, c, 0)),
        out_shape=jax.ShapeDtypeStruct(z.shape, z.dtype), input_output_aliases={6: 0}, compiler_params=_params(1),
    )(p, p, lng, lnb, ws, bs, z)


def _sg_bwd(p, lng, lnb, ws, bs, dz, dp):
    T = p.shape[0]

    def body(u_ref, v_ref, lng_ref, lnb_ref, ws_ref, bs_ref, dz_ref, dp_in, dp_ref, dws_ref, dbs_ref, dlng_ref, dlnb_ref,
             stash):
        del dp_in
        cidx, jj = pl.program_id(0), pl.program_id(1)

        @pl.when(jnp.logical_and(cidx == 0, jj == 0))
        def _():
            dws_ref[...] = jnp.zeros_like(dws_ref)
            dbs_ref[...] = jnp.zeros_like(dbs_ref)
            dlng_ref[...] = jnp.zeros_like(dlng_ref)
            dlnb_ref[...] = jnp.zeros_like(dlnb_ref)

        @pl.when(jj == 0)
        def _():
            u, v, lng = u_ref[...], v_ref[...], lng_ref[...]
            c = _sg_core(u, v, lng, lnb_ref[...], ws_ref, bs_ref)
            dzc = dz_ref[...]
            gs = lambda t, g: t[:, g * LANES:(g + 1) * LANES]
            dsv = dzc * c["gu"]
            dvp = []
            for g in range(SG_GROUPS):
                dsv_g = gs(dsv, g)
                dws_ref[g] += jnp.where(c["tril"], _dot(dsv_g, gs(c["vp"], g), NT), 0.0)
                dbs_ref[g] += jnp.sum(dsv_g, axis=1, keepdims=True)
                dvp.append(_dot(c["wm"][g], dsv_g, TN))
            dvp = jnp.concatenate(dvp, axis=1)
            xh = c["xh"]
            dlng_ref[...] += _row0(jnp.sum(dvp * xh, axis=0, keepdims=True))
            dlnb_ref[...] += _row0(jnp.sum(dvp, axis=0, keepdims=True))
            dxh = dvp * lng
            dgv = c["rs"] * (dxh - jnp.mean(dxh, axis=-1, keepdims=True) - xh * jnp.mean(dxh * xh, axis=-1, keepdims=True))
            dp_ref[...] = (dzc * c["sv"] * _gelu_grad(u, c["thu"])).astype(BF16)
            stash[...] = dgv * _gelu_grad(v, c["thv"])

        @pl.when(jj == 1)
        def _():
            dp_ref[...] = stash[...].astype(BF16)

    full = lambda shape: pl.BlockSpec(shape, lambda c, jj: (0,) * len(shape))
    return pl.pallas_call(
        body, name="sg_bwd", grid=(T // SG_CHUNK, 2),
        in_specs=[pl.BlockSpec((SG_CHUNK, WIDTH), lambda c, jj: (c, 7)), pl.BlockSpec((SG_CHUNK, WIDTH), lambda c, jj: (c, 8)),
                  full((1, WIDTH)), full((1, WIDTH)), full(ws.shape), full(bs.shape),
                  pl.BlockSpec((None, SG_CHUNK, WIDTH), lambda c, jj: (2, c, 0)), ANY],
        out_specs=[pl.BlockSpec((SG_CHUNK, WIDTH), lambda c, jj: (c, 7 + jj)), full(ws.shape), full(bs.shape),
                   full((SUBLANES, WIDTH)), full((SUBLANES, WIDTH))],
        out_shape=[jax.ShapeDtypeStruct(dp.shape, dp.dtype), jax.ShapeDtypeStruct(ws.shape, F32),
                   jax.ShapeDtypeStruct(bs.shape, F32), jax.ShapeDtypeStruct((SUBLANES, WIDTH), F32),
                   jax.ShapeDtypeStruct((SUBLANES, WIDTH), F32)],
        scratch_shapes=[pltpu.VMEM((SG_CHUNK, WIDTH), F32)], input_output_aliases={7: 0}, compiler_params=_params(2),
    )(p, p, lng, lnb, ws, bs, dz, dp)


BRANCH_COLS = D_MODEL // N_CHIP
GATE_UNIT0 = GATE_COL0 // WIDTH
UNITS = D_MODEL // WIDTH


def _unit_specs(order):
    def spec(which):
        def index(*g):
            _, n, u = order(*g)
            return (2 * u + which, n, 0, 0)
        return pl.BlockSpec((None, None, WIDTH, BRANCH_COLS), index)
    return [spec(0), spec(1)]


def _merge_fwd(z, p, wb):
    T = z.shape[1]
    tm = _token_tile(T)
    order = lambda i, u, n: (i, n, u)

    def body(z_ref, wa_ref, wb_ref, gt_ref, out_ref, acc):
        n = pl.program_id(2)
        zv = z_ref[...]
        y = jnp.concatenate([_dot(zv, wa_ref[...]), _dot(zv, wb_ref[...])], axis=1)
        part = jax.nn.sigmoid(gt_ref[...]) * y

        @pl.when(n == 0)
        def _():
            acc[...] = part

        @pl.when(n > 0)
        def _():
            acc[...] += part

        @pl.when(n == N_BRANCH - 1)
        def _():
            out_ref[...] = acc[...].astype(BF16)

    return pl.pallas_call(
        body, name="merge_fwd", grid=(T // tm, UNITS, N_BRANCH),
        in_specs=[pl.BlockSpec((None, tm, WIDTH), lambda i, u, n: (n, i, 0)), *_unit_specs(order),
                  pl.BlockSpec((tm, WIDTH), lambda i, u, n: (i, GATE_UNIT0 + UNITS * n + u))],
        out_specs=pl.BlockSpec((tm, WIDTH), lambda i, u, n: (i, u)),
        out_shape=jax.ShapeDtypeStruct((T, D_MODEL), BF16),
        scratch_shapes=[pltpu.VMEM((tm, WIDTH), F32)], compiler_params=_params(3))(z, wb, wb, p)


def _merge_bwd(z, p, wb, dmerged):
    T = z.shape[1]
    tm = _token_tile(T)
    order = lambda i, n, u: (i, n, u)

    def body(z_ref, wa_ref, wb_ref, gt_ref, dm_ref, dp_ref, dy_ref, dz_ref):
        u = pl.program_id(2)
        zv, wa, wbv = z_ref[...], wa_ref[...], wb_ref[...]
        y = jnp.concatenate([_dot(zv, wa), _dot(zv, wbv)], axis=1)
        gate = jax.nn.sigmoid(gt_ref[...])
        dm = dm_ref[...]
        dp_ref[...] = (dm * y * gate * (1.0 - gate)).astype(BF16)
        dyv = (dm * gate).astype(BF16)
        dy_ref[...] = dyv
        part = _dot(dyv[:, :BRANCH_COLS], wa, NT) + _dot(dyv[:, BRANCH_COLS:], wbv, NT)

        @pl.when(u == 0)
        def _():
            dz_ref[...] = part

        @pl.when(u > 0)
        def _():
            dz_ref[...] += part

    unit = lambda i, n, u: (i, GATE_UNIT0 + UNITS * n + u)
    return pl.pallas_call(
        body, name="merge_bwd", grid=(T // tm, N_BRANCH, UNITS),
        in_specs=[pl.BlockSpec((None, tm, WIDTH), lambda i, n, u: (n, i, 0)), *_unit_specs(order),
                  pl.BlockSpec((tm, WIDTH), unit), pl.BlockSpec((tm, WIDTH), lambda i, n, u: (i, u))],
        out_specs=[pl.BlockSpec((tm, WIDTH), unit), pl.BlockSpec((None, tm, WIDTH), lambda i, n, u: (n, i, u)),
                   pl.BlockSpec((None, tm, WIDTH), lambda i, n, u: (n, i, 0))],
        out_shape=[jax.ShapeDtypeStruct((T, IN_COLS), BF16), jax.ShapeDtypeStruct((N_BRANCH, T, D_MODEL), BF16),
                   jax.ShapeDtypeStruct((N_BRANCH, T, WIDTH), F32)],
        compiler_params=_params(3))(z, wb, wb, p, dmerged)


def _branch_wgrad(z, dy):
    T = z.shape[1]
    tt = T
    nk = T // tt

    def body(z_ref, dy_ref, out_ref, acc):
        kk = pl.program_id(1)
        part = _dot(z_ref[...], dy_ref[...], TN)

        @pl.when(kk == 0)
        def _():
            acc[...] = part

        @pl.when(kk > 0)
        def _():
            acc[...] += part

        @pl.when(kk == nk - 1)
        def _():
            for k in range(N_CHIP):
                out_ref[k] = acc[:, k * BRANCH_COLS:(k + 1) * BRANCH_COLS]

    return pl.pallas_call(
        body, name="branch_wgrad", grid=(N_BRANCH, nk),
        in_specs=[pl.BlockSpec((None, tt, WIDTH), lambda n, kk: (n, kk, 0)),
                  pl.BlockSpec((None, tt, D_MODEL), lambda n, kk: (n, kk, 0))],
        out_specs=pl.BlockSpec((N_CHIP, None, WIDTH, BRANCH_COLS), lambda n, kk: (0, n, 0, 0)),
        out_shape=jax.ShapeDtypeStruct((N_CHIP, N_BRANCH, WIDTH, BRANCH_COLS), F32),
        scratch_shapes=[pltpu.VMEM((WIDTH, D_MODEL), F32)], compiler_params=_params(2))(z, dy)


def _layer_fwd(x, h, win, small, next_gain, hooks):
    p = _mm_cols("in_proj", h, win, [F32])[0]
    o_hgrn, z, states = _hgrn_fwd(p, small["lbs"], small["g_hgrn_out"])
    z = _conv_fwd(p, small["w_conv"], z, after=hooks["after_hgrn"]([o_hgrn]))
    z = _sg_fwd(p, small["sg_ln_g"], small["sg_ln_b"], small["w_sg"], small["b_sg"], z)
    wb, wo, w1, w2 = hooks["late_weights"]([z])
    wb = wb.reshape(N_CHIP, N_BRANCH, WIDTH, BRANCH_COLS)
    merged = _merge_fwd(z, p, wb)
    x_mid, h2 = _mm_rows("out_proj", merged, wo, x, small["g_ffn"])
    s = _mm_cols("ff1", h2, w1, [BF16], epilogue=lambda acc: (jnp.square(jnp.maximum(acc, 0.0)),))[0]
    if next_gain is None:
        x_out, h_next = _mm_rows("ff2_last", s, w2, x_mid, after=hooks["before_last"]([s])), None
    else:
        x_out, h_next = _mm_rows("ff2", s, w2, x_mid, next_gain, after=hooks["before_last"]([s]))
    saved = dict(x=x, h=h, p=p, o_hgrn=o_hgrn, z=z, states=states, merged=merged, x_mid=x_mid, h2=h2, s=s)
    return x_out, h_next, saved, [win, wb, wo, w1, w2]


def _layer_bwd(dx_out, sv, wts, small, tick, after):
    win, wb, wo, w1, w2 = wts
    g = {}
    da = _mm_cols_t("ff2_dgrad", dx_out, w2, BF16, extra=(sv["s"],), after=after,
                    epilogue=lambda acc, s: (acc * 2.0 * jnp.sqrt(s.astype(F32)),))
    d_ff2 = _mm_wgrad("ff2_wgrad", sv["s"], dx_out, w2.shape[1], D_MODEL, True, False)
    d_ff1 = _mm_wgrad("ff1_wgrad", sv["h2"], da, D_MODEL, w1.shape[2], False, True)
    dx_mid, g["g_ffn"] = _dgrad_norm_bwd("ff1_dgrad", da, w1, sv["x_mid"], small["g_ffn"], dx_out)
    after = tick([dx_mid], [("w_ff1", d_ff1), ("w_ff2", d_ff2)])
    dmerged = _mm_cols_t("out_proj_dgrad", dx_mid, wo, F32, after=after)
    d_o = _mm_wgrad("out_proj_wgrad", sv["merged"], dx_mid, wo.shape[1], D_MODEL, True, False)
    dp, dy, dz = _merge_bwd(sv["z"], sv["p"], wb, dmerged)
    d_branch = _branch_wgrad(sv["z"], dy).reshape(N_CHIP, N_BRANCH * WIDTH, BRANCH_COLS)
    after = tick([dp], [("w_branch", d_branch), ("w_o", d_o)])
    dp, g["lbs"], g["g_hgrn_out"] = _hgrn_bwd(sv["p"], sv["o_hgrn"], dz, sv["states"], small["lbs"],
                                              small["g_hgrn_out"], dp, after=after)
    dp, g["w_conv"] = _conv_bwd(sv["p"], small["w_conv"], dz, dp)
    dp, g["w_sg"], g["b_sg"], g["sg_ln_g"], g["sg_ln_b"] = _sg_bwd(
        sv["p"], small["sg_ln_g"], small["sg_ln_b"], small["w_sg"], small["b_sg"], dz, dp)
    after = tick([dp], [])
    d_in = _mm_wgrad("in_proj_wgrad", sv["h"], dp, D_MODEL, win.shape[2], False, True, after=after)
    dx, g["g_mix"] = _dgrad_norm_bwd("in_proj_dgrad", dp, win, sv["x"], small["g_mix"], dx_mid)
    return dx, g, tick([dx], [("w_in", d_in)])


def _mesh_pos():
    return lax.axis_index("x"), lax.axis_index("y"), lax.axis_index("c")


def _other_chips(x, y):
    return [(1 - x, y), (x, 1 - y), (1 - x, 1 - y)]


def _remote(src, dst, send_sems, recv_sems, k, to):
    return pltpu.make_async_remote_copy(src_ref=src, dst_ref=dst, send_sem=send_sems.at[k], recv_sem=recv_sems.at[k],
                                        device_id=to, device_id_type=MESH)


def _gather_call(name, body, buf, after):
    scratch = [pltpu.SemaphoreType.DMA((7,)), pltpu.SemaphoreType.DMA((7,))]
    return pl.pallas_call(
        body, name=name, in_specs=[ANY] * (1 + len(after)), out_specs=ANY,
        out_shape=jax.ShapeDtypeStruct(buf.shape, buf.dtype), scratch_shapes=scratch, input_output_aliases={0: 0})(buf, *after)


HBM = pl.BlockSpec(memory_space=pltpu.HBM)
SEM = pl.BlockSpec(memory_space=pltpu.SEMAPHORE)
DATAFLOW = pltpu.SideEffectType.DATAFLOW_SIDE_EFFECTING


def _split_start(name, bufs, copies, n_copies, after=()):
    n = len(bufs)

    def body(*refs):
        send_sems, recv_sems = refs[n + len(after)], refs[n + len(after) + 1]
        for cp in copies(refs[:n], send_sems, recv_sems):
            cp.start()
        refs[-1][...] = jnp.zeros_like(refs[-1])

    outs = pl.pallas_call(
        body, name=name,
        out_shape=(pltpu.SemaphoreType.DMA((n_copies,)), pltpu.SemaphoreType.DMA((n_copies,)),
                   *[pltpu.HBM(b.shape, b.dtype) for b in bufs], jax.ShapeDtypeStruct((SUBLANES, LANES), F32)),
        in_specs=[HBM] * n + [ANY] * len(after),
        out_specs=(SEM, SEM, *[HBM] * n, pl.BlockSpec(memory_space=pltpu.VMEM)),
        input_output_aliases={t: 2 + t for t in range(n)},
        compiler_params=pltpu.CompilerParams(has_side_effects=DATAFLOW),
    )(*[pltpu.with_memory_space_constraint(b, pltpu.HBM) for b in bufs], *after)
    return outs[0], outs[1], list(outs[2:2 + n]), outs[-1]


def _split_wait(name, started, copies, after):
    send_sems, recv_sems, bufs, _ = started
    n = len(bufs)

    def body(*refs):
        for cp in copies(refs[:n], refs[n], refs[n + 1]):
            cp.wait_send()
            cp.wait_recv()

    return list(pl.pallas_call(
        body, name=name, out_shape=tuple(pltpu.HBM(b.shape, b.dtype) for b in bufs),
        in_specs=[HBM] * n + [SEM, SEM] + [ANY] * len(after), out_specs=tuple([HBM] * n),
        input_output_aliases={t: t for t in range(n)},
        compiler_params=pltpu.CompilerParams(has_side_effects=DATAFLOW),
    )(*bufs, send_sems, recv_sems, *after))


def _weight_ici_copies(refs, send_sems, recv_sems):
    x, y, c = _mesh_pos()
    out = []
    for t, ref in enumerate(refs):
        rh = ref.shape[1] // 2
        mine = ref.at[2 * x + y, pl.ds(c * rh, rh), :]
        out += [_remote(mine, mine, send_sems, recv_sems, 3 * t + j, (*chip, c)) for j, chip in enumerate(_other_chips(x, y))]
    return out


def _weight_d2d_copies(refs, send_sems, recv_sems):
    x, y, c = _mesh_pos()
    out = []
    for t, ref in enumerate(refs):
        rh = ref.shape[1] // 2
        for j, chip in enumerate(_other_chips(x, y)):
            blk = ref.at[2 * chip[0] + chip[1], pl.ds(c * rh, rh), :]
            out.append(_remote(blk, blk, send_sems, recv_sems, 3 * t + j, (x, y, 1 - c)))
    return out


def _swap_part(refs, send_sems, recv_sems, s0):
    x, y, c = _mesh_pos()
    n = len(refs) // 2
    out = []
    for t in range(n):
        rh = refs[t].shape[1] // 2
        out.append(_remote(refs[t].at[:, pl.ds((1 - c) * rh, rh), :], refs[n + t], send_sems, recv_sems, s0 + t, (x, y, 1 - c)))
    return out


def _exchange_part(refs, send_sems, recv_sems, s0):
    x, y, c = _mesh_pos()
    n = len(refs) // 2
    out = []
    for t in range(n):
        for j, chip in enumerate(_other_chips(x, y)):
            out.append(_remote(refs[t].at[2 * chip[0] + chip[1]], refs[n + t].at[j], send_sems, recv_sems, s0 + 3 * t + j,
                               (*chip, c)))
    return out


def _gather_part(refs, send_sems, recv_sems, s0):
    x, y, c = _mesh_pos()
    return [_remote(ref.at[c], ref.at[c], send_sems, recv_sems, s0 + t, (x, y, 1 - c)) for t, ref in enumerate(refs)]


class _GradPipeline:
    def __init__(self, pos):
        self.pos = pos
        self.groups, self.pending, self.count = [], None, 0
        self.reduced = {n: [None] * DEPTH for n in BIG_NAMES}

    def busy(self):
        return bool(self.groups) or self.pending is not None

    def tick(self, deps, new):
        if self.pending is not None:
            started, copies, owners = self.pending
            bufs = _split_wait("grad_pipe_wait_%d" % self.count, started, copies, after=list(deps))
            for grp, lo, hi in owners:
                grp["bufs"] = bufs[lo:hi]
            self.pending = None
        parts = []
        for grp in list(self.groups):
            n, names = len(grp["names"]), grp["names"]
            if grp["stage"] == "swap":
                pair = [_pair_sum("grad_pair_sum_" + nm, f, r, self.pos)
                        for nm, f, r in zip(names, grp["bufs"][:n], grp["bufs"][n:])]
                grp["own32"] = [p32 for p32, _ in pair]
                landing = [lax.empty((3, *p16.shape[1:]), BF16) for _, p16 in pair]
                grp["stage"] = "exchange"
                parts.append((grp, [p16 for _, p16 in pair] + landing, _exchange_part, 3 * n))
            elif grp["stage"] == "exchange":
                halves = [_chip_sum("grad_chip_sum_" + nm, p32, r, self.pos)
                          for nm, p32, r in zip(names, grp["own32"], grp["bufs"][n:])]
                grp["stage"] = "gather"
                parts.append((grp, halves, _gather_part, n))
            else:
                for nm, b in zip(names, grp["bufs"]):
                    self.reduced[nm][grp["layer"]] = b.reshape(-1, b.shape[-1])
                self.groups.remove(grp)
        if new:
            grp = dict(names=[nm for nm, _, _ in new], layer=new[0][1], stage="swap")
            self.groups.append(grp)
            fulls = [g for _, _, g in new]
            landing = [lax.empty((N_CHIP, g.shape[1] // 2, g.shape[2]), F32) for g in fulls]
            parts.append((grp, fulls + landing, _swap_part, len(fulls)))
        if not parts:
            return ()
        bufs, layout, owners, sems = [], [], [], 0
        for grp, part_bufs, fn, n_sems in parts:
            layout.append((len(bufs), len(bufs) + len(part_bufs), fn, sems))
            owners.append((grp, len(bufs), len(bufs) + len(part_bufs)))
            bufs += part_bufs
            sems += n_sems

        def copies(refs, send_sems, recv_sems):
            out = []
            for lo, hi, fn, s0 in layout:
                out += fn(refs[lo:hi], send_sems, recv_sems, s0)
            return out

        started = _split_start("grad_pipe_start_%d" % self.count, bufs, copies, sems)
        self.pending = (started, copies, owners)
        self.count += 1
        return (started[3],)


def _gather_all(name, block, slot, after=()):
    buf = lax.dynamic_update_slice(jnp.zeros((8, *block.shape), block.dtype), block[None], (slot, 0, 0))

    def body(*refs):
        out_ref, send_sems, recv_sems = refs[1 + len(after):]
        x, y, c = _mesh_pos()
        chips = _other_chips(x, y)
        sibling = (x, y, 1 - c)
        slot_of = lambda px, py, pc: out_ref.at[4 * px + 2 * py + pc]
        started = [_remote(slot_of(x, y, c), slot_of(x, y, c), send_sems, recv_sems, 0, sibling)]
        started += [_remote(slot_of(x, y, c), slot_of(x, y, c), send_sems, recv_sems, 1 + j, (*chip, c))
                    for j, chip in enumerate(chips)]
        for cp in started:
            cp.start()
        for j, chip in enumerate(chips):
            _remote(slot_of(*chip, c), slot_of(*chip, c), send_sems, recv_sems, 1 + j, (*chip, c)).wait_recv()
            fw = _remote(slot_of(*chip, c), slot_of(*chip, c), send_sems, recv_sems, 4 + j, sibling)
            fw.start()
            started.append(fw)
        _remote(slot_of(x, y, 1 - c), slot_of(x, y, 1 - c), send_sems, recv_sems, 0, sibling).wait_recv()
        for j, chip in enumerate(chips):
            _remote(slot_of(*chip, 1 - c), slot_of(*chip, 1 - c), send_sems, recv_sems, 4 + j, sibling).wait_recv()
        for cp in started:
            cp.wait_send()

    return _gather_call(name, body, buf, after)


def _row_tile(rows, cols):
    cap = max(SUBLANES, ELEMWISE_BLOCK_BYTES // (4 * cols))
    tr = rows
    while tr > cap and tr % 2 == 0:
        tr //= 2
    return tr


def _pair_sum(name, grad, recv, pos):
    _, rh, cols = recv.shape
    tr = _row_tile(rh, cols)
    per = rh // tr

    def body(pos_ref, g_ref, r_ref, own_ref, out16_ref):
        s = g_ref[...] + r_ref[...]
        out16_ref[...] = s.astype(BF16)

        @pl.when(pl.program_id(1) == pos_ref[0])
        def _():
            own_ref[...] = s

    blk = pl.BlockSpec((None, tr, cols), lambda i, k, pos_ref: (k, i, 0))
    return pl.pallas_call(
        body, name=name,
        grid_spec=pltpu.PrefetchScalarGridSpec(
            num_scalar_prefetch=1, grid=(per, N_CHIP),
            in_specs=[pl.BlockSpec((None, tr, cols), lambda i, k, pos_ref: (k, pos_ref[1] * per + i, 0)), blk],
            out_specs=[pl.BlockSpec((tr, cols), lambda i, k, pos_ref: (i, 0)), blk]),
        out_shape=[jax.ShapeDtypeStruct((rh, cols), F32), jax.ShapeDtypeStruct(recv.shape, BF16)],
        compiler_params=_params(2))(pos, grad, recv)


def _chip_sum(name, own32, recv, pos):
    rh, cols = own32.shape
    tr = _row_tile(rh, cols)

    def body(pos_ref, own_ref, r_ref, out_ref):
        del pos_ref
        out_ref[...] = ((own_ref[...] + r_ref[0].astype(F32)) + r_ref[1].astype(F32)) + r_ref[2].astype(F32)

    return pl.pallas_call(
        body, name=name,
        grid_spec=pltpu.PrefetchScalarGridSpec(
            num_scalar_prefetch=1, grid=(rh // tr,),
            in_specs=[pl.BlockSpec((tr, cols), lambda i, pos_ref: (i, 0)),
                      pl.BlockSpec((3, tr, cols), lambda i, pos_ref: (0, i, 0))],
            out_specs=pl.BlockSpec((None, tr, cols), lambda i, pos_ref: (pos_ref[1], i, 0))),
        out_shape=jax.ShapeDtypeStruct((2, rh, cols), F32), compiler_params=_params(1))(pos, own32, recv)


def _cast_into_slot(name, w, layer, pos, after=()):
    _, rows, cols = w.shape
    tr = _row_tile(rows, cols)

    def body(pos_ref, w_ref, *rest):
        del pos_ref
        rest[-1][...] = w_ref[...].astype(BF16)

    return pl.pallas_call(
        body, name=name,
        grid_spec=pltpu.PrefetchScalarGridSpec(
            num_scalar_prefetch=1, grid=(rows // tr,),
            in_specs=[pl.BlockSpec((None, tr, cols), lambda i, pos_ref: (layer, i, 0))] + [ANY] * len(after),
            out_specs=pl.BlockSpec((None, tr, cols), lambda i, pos_ref: (pos_ref[0], i, 0))),
        out_shape=jax.ShapeDtypeStruct((N_CHIP, rows, cols), BF16), compiler_params=_params(1))(pos, w, *after)


def _adamw_math(w, g, m, v):
    m = ADAM_B1 * m + (1.0 - ADAM_B1) * g
    v = ADAM_B2 * v + (1.0 - ADAM_B2) * jnp.square(g)
    m_hat = m / (1.0 - ADAM_B1 ** ADAM_STEP)
    v_hat = v / (1.0 - ADAM_B2 ** ADAM_STEP)
    delta = -ADAM_LR * (m_hat / (jnp.sqrt(v_hat) + ADAM_EPS) + ADAM_WD * w)
    return delta, m, v


def _adamw_layers(name, w, m, v, grads, first, into=None, after=()):
    _, rows, cols = w.shape
    tr = _row_tile(rows, cols)
    n_layers = len(grads)

    def body(w_ref, m_ref, v_ref, *rest):
        g_refs, (grad_ref, d_ref, nm_ref, nv_ref) = rest[:n_layers], rest[len(rest) - 4:]
        layer = pl.program_id(0)
        g = g_refs[0][...]
        for l in range(1, n_layers):
            g = jnp.where(layer == l, g_refs[l][...], g)
        grad_ref[...] = g
        d_ref[...], nm_ref[...], nv_ref[...] = _adamw_math(w_ref[...], g, m_ref[...], v_ref[...])

    blk = pl.BlockSpec((None, tr, cols), lambda l, i: (first + l, i, 0))
    g_spec = lambda k: pl.BlockSpec((tr, cols), lambda l, i: (jnp.where(l == k, i, 0), 0))
    passed = list(into or []) + list(after)
    return pl.pallas_call(
        body, name=name, grid=(n_layers, rows // tr),
        in_specs=[blk, blk, blk] + [g_spec(k) for k in range(n_layers)] + [ANY] * len(passed), out_specs=[blk] * 4,
        out_shape=[jax.ShapeDtypeStruct(w.shape, F32)] * 4,
        input_output_aliases={3 + n_layers + t: t for t in range(4)} if into else {},
        compiler_params=_params(2))(w, m, v, *grads, *passed)


def _sum_devices(gathered):
    _, rows, cols = gathered.shape

    def body(g_ref, out_ref):
        s = g_ref[0]
        for d in range(1, 8):
            s = s + g_ref[d]
        out_ref[...] = s

    return pl.pallas_call(body, name="sum_devices", out_shape=jax.ShapeDtypeStruct((rows, cols), F32),
                          compiler_params=pltpu.CompilerParams(vmem_limit_bytes=VMEM_LIMIT_BYTES))(gathered)


def _adamw_flat(w, g, m, v):
    def body(w_ref, g_ref, m_ref, v_ref, d_ref, nm_ref, nv_ref):
        d_ref[...], nm_ref[...], nv_ref[...] = _adamw_math(w_ref[...], g_ref[...], m_ref[...], v_ref[...])

    return pl.pallas_call(body, name="adamw_small", out_shape=[jax.ShapeDtypeStruct(w.shape, F32)] * 3,
                          compiler_params=pltpu.CompilerParams(vmem_limit_bytes=VMEM_LIMIT_BYTES))(w, g, m, v)


SMALL_NAMES = ["g_mix", "lower_bounds", "g_hgrn_out", "w_conv", "sg_ln_g", "sg_ln_b", "w_sg", "b_sg", "g_ffn", "g_final"]
BIG_NAMES = ["w_in", "w_branch", "w_o", "w_ff1", "w_ff2"]
WEIGHT_ORDER = ["w_in", "g_mix", "lower_bounds", "g_hgrn_out", "w_conv", "sg_ln_g", "sg_ln_b", "w_sg", "b_sg", "w_branch",
                "w_o", "g_ffn", "w_ff1", "w_ff2", "g_final"]


def _padded_rows(n):
    return -(-n // SUBLANES) * SUBLANES


def _pack(arrays):
    parts = []
    for a in arrays:
        a = a.reshape(-1, LANES)
        parts.append(jnp.pad(a, ((0, _padded_rows(a.shape[0]) - a.shape[0]), (0, 0))))
    return jnp.concatenate(parts, axis=0)


def _unpack(flat, shapes):
    out, row = [], 0
    for s in shapes:
        n = int(np.prod(s)) // LANES
        out.append(flat[row:row + n].reshape(s))
        row += _padded_rows(n)
    return out


def _as_2d(name, a):
    return a.reshape(DEPTH, N_BRANCH * WIDTH, BRANCH_COLS) if name == "w_branch" else a


def kernel(x, w_in, g_mix, lower_bounds, g_hgrn_out, w_conv, sg_ln_g, sg_ln_b, w_sg, b_sg, w_branch, w_o, g_ffn, w_ff1, w_ff2, g_final, loss_target, m_w_in, m_g_mix, m_lower_bounds, m_g_hgrn_out, m_w_conv, m_sg_ln_g, m_sg_ln_b, m_w_sg, m_b_sg, m_w_branch, m_w_o, m_g_ffn, m_w_ff1, m_w_ff2, m_g_final, v_w_in, v_g_mix, v_lower_bounds, v_g_hgrn_out, v_w_conv, v_sg_ln_g, v_sg_ln_b, v_w_sg, v_b_sg, v_w_branch, v_w_o, v_g_ffn, v_w_ff1, v_w_ff2, v_g_final):
    weights = dict(w_in=w_in, g_mix=g_mix, lower_bounds=lower_bounds, g_hgrn_out=g_hgrn_out, w_conv=w_conv,
                   sg_ln_g=sg_ln_g, sg_ln_b=sg_ln_b, w_sg=w_sg, b_sg=b_sg, w_branch=w_branch, w_o=w_o, g_ffn=g_ffn,
                   w_ff1=w_ff1, w_ff2=w_ff2, g_final=g_final)
    mom1 = dict(w_in=m_w_in, g_mix=m_g_mix, lower_bounds=m_lower_bounds, g_hgrn_out=m_g_hgrn_out, w_conv=m_w_conv,
                sg_ln_g=m_sg_ln_g, sg_ln_b=m_sg_ln_b, w_sg=m_w_sg, b_sg=m_b_sg, w_branch=m_w_branch, w_o=m_w_o,
                g_ffn=m_g_ffn, w_ff1=m_w_ff1, w_ff2=m_w_ff2, g_final=m_g_final)
    mom2 = dict(w_in=v_w_in, g_mix=v_g_mix, lower_bounds=v_lower_bounds, g_hgrn_out=v_g_hgrn_out, w_conv=v_w_conv,
                sg_ln_g=v_sg_ln_g, sg_ln_b=v_sg_ln_b, w_sg=v_w_sg, b_sg=v_b_sg, w_branch=v_w_branch, w_o=v_w_o,
                g_ffn=v_g_ffn, w_ff1=v_w_ff1, w_ff2=v_w_ff2, g_final=v_g_final)
    xi, yi, ci = _mesh_pos()
    pos = jnp.stack([2 * xi + yi, ci]).astype(jnp.int32)
    device = 4 * xi + 2 * yi + ci
    conv_cols = w_conv.shape[2]

    conv_all = _gather_all("gather_w_conv", w_conv.reshape(DEPTH * CONV_K, conv_cols), device)
    conv_full = conv_all.reshape(N_CHIP, 2, DEPTH, CONV_K, conv_cols)[:, 0].transpose(1, 2, 0, 3).reshape(DEPTH, CONV_K, WIDTH)

    ici, d2d = {}, {}
    cast = lambda l, names, after: [_cast_into_slot("cast_" + n, _as_2d(n, weights[n]), l, pos, after=after) for n in names]
    token = (conv_all,)
    for l in range(DEPTH):
        for part, names in (("w_in", BIG_NAMES[:1]), ("rest", BIG_NAMES[1:])):
            ici[l, part] = _split_start("weights_ici_start_%d_%s" % (l, part), cast(l, names, token), _weight_ici_copies,
                                        3 * len(names), after=token)
            token = (ici[l, part][3],)
    lbs = _lbs_fwd(lower_bounds)

    def forward_to_sibling(l, part, deps):
        landed = _split_wait("weights_ici_wait_%d_%s" % (l, part), ici.pop((l, part)), _weight_ici_copies, after=deps)
        d2d[l, part] = _split_start("weights_d2d_start_%d_%s" % (l, part), landed, _weight_d2d_copies, 3 * len(landed))
        return (d2d[l, part][3],)

    def gathered(l, part, deps):
        return _split_wait("weights_d2d_wait_%d_%s" % (l, part), d2d.pop((l, part)), _weight_d2d_copies, after=deps)

    act = x[0]
    normed = _rms_fwd("rms_mix", act, g_mix[0:1], after=token)
    layers = []
    forward_to_sibling(0, "w_in", [normed, lbs])
    for l in range(DEPTH):
        small = dict(g_mix=g_mix[l:l + 1], lbs=lbs[l:l + 1], g_hgrn_out=g_hgrn_out[l:l + 1], w_conv=conv_full[l],
                     sg_ln_g=sg_ln_g[l:l + 1], sg_ln_b=sg_ln_b[l:l + 1], w_sg=w_sg[l],
                     b_sg=b_sg[l].reshape(SG_GROUPS, SG_CHUNK, 1), g_ffn=g_ffn[l:l + 1])
        hooks = dict(after_hgrn=lambda deps, l=l: forward_to_sibling(l, "rest", deps),
                     late_weights=lambda deps, l=l: gathered(l, "rest", deps),
                     before_last=(lambda deps, l=l: forward_to_sibling(l + 1, "w_in", deps)) if l + 1 < DEPTH
                     else (lambda deps: ()))
        act, normed, saved, wts = _layer_fwd(act, normed, gathered(l, "w_in", [act])[0], small,
                                             g_mix[l + 1:l + 2] if l + 1 < DEPTH else None, hooks)
        layers.append((wts, small, saved))
    loss_blk, dact, dg_final = _loss_head(act, g_final.reshape(1, D_MODEL), loss_target[0])

    pipe = _GradPipeline(pos)
    small_grads = [None] * DEPTH
    after = ()
    for l in reversed(range(DEPTH)):
        wts, small, saved = layers[l]
        tick = lambda deps, new, l=l: pipe.tick(deps, [(nm, l, g) for nm, g in new])
        dact, small_grads[l], after = _layer_bwd(dact, saved, wts, small, tick, after)
    grad_x = dact[None]

    stack = lambda key, rows=None: jnp.stack([small_grads[l][key][0] if rows is None else small_grads[l][key][:rows]
                                              for l in range(DEPTH)])
    local_small = dict(
        g_mix=stack("g_mix"), lower_bounds=stack("lbs"), g_hgrn_out=stack("g_hgrn_out"), w_conv=stack("w_conv", CONV_K),
        sg_ln_g=stack("sg_ln_g"), sg_ln_b=stack("sg_ln_b"), w_sg=jnp.stack([small_grads[l]["w_sg"] for l in range(DEPTH)]),
        b_sg=jnp.stack([small_grads[l]["b_sg"].reshape(SG_GROUPS, SG_CHUNK) for l in range(DEPTH)]),
        g_ffn=stack("g_ffn"), g_final=dg_final[0])
    shapes = [local_small[n].shape for n in SMALL_NAMES] + [(SUBLANES, LANES)]
    summed = _sum_devices(_gather_all("gather_small_grads", _pack([local_small[n] for n in SMALL_NAMES] + [loss_blk]), device,
                                       after=after))
    parts = _unpack(summed, shapes)
    loss = parts[-1][0, 0]
    small_grad = dict(zip(SMALL_NAMES, parts[:-1]))
    small_grad["lower_bounds"] = _lbs_bwd(lower_bounds, small_grad["lower_bounds"])
    small_grad["w_conv"] = lax.dynamic_slice_in_dim(small_grad["w_conv"], pos[0] * conv_cols, conv_cols, axis=2)
    g_flat = _pack([small_grad[n] for n in SMALL_NAMES])
    d_flat, m_flat, v_flat = _adamw_flat(_pack([weights[n] for n in SMALL_NAMES]), g_flat,
                                         _pack([mom1[n] for n in SMALL_NAMES]), _pack([mom2[n] for n in SMALL_NAMES]))
    small_shapes = [weights[n].shape for n in SMALL_NAMES]
    grads = dict(small_grad)
    delta = dict(zip(SMALL_NAMES, _unpack(d_flat, small_shapes)))
    new_m = dict(zip(SMALL_NAMES, _unpack(m_flat, small_shapes)))
    new_v = dict(zip(SMALL_NAMES, _unpack(v_flat, small_shapes)))

    def adam(n, first, layer_grads, into=None, after=()):
        return _adamw_layers("adamw_%s_%d" % (n, first), _as_2d(n, weights[n]), _as_2d(n, mom1[n]), _as_2d(n, mom2[n]),
                             layer_grads, first, into, after)

    done = {}
    token = pipe.tick([summed], [])
    for n in ("w_ff1", "w_ff2"):
        done[n] = adam(n, 0, pipe.reduced[n], after=token)
    token = pipe.tick([done["w_ff2"][1]], [])
    for n in ("w_o", "w_branch"):
        done[n] = adam(n, 0, pipe.reduced[n], after=token)
    rest = adam("w_in", 1, pipe.reduced["w_in"][1:], after=token)
    pipe.tick([rest[1]], [])
    assert not pipe.busy()
    done["w_in"] = adam("w_in", 0, pipe.reduced["w_in"][:1], into=rest)
    for n in BIG_NAMES:
        grads[n], delta[n], new_m[n], new_v[n] = [o.reshape(weights[n].shape) for o in done[n]]

    return (loss, grad_x, *[grads[n] for n in WEIGHT_ORDER], *[delta[n] for n in WEIGHT_ORDER],
            *[new_m[n] for n in WEIGHT_ORDER], *[new_v[n] for n in WEIGHT_ORDER])
```

```python
import numpy as np
import jax
import jax.numpy as jnp
from jax import lax
from jax.experimental import pallas as pl
from jax.experimental.pallas import tpu as pltpu

F32, BF16 = jnp.float32, jnp.bfloat16

D_MODEL = 1024
WIDTH = 512
N_BRANCH = 3
N_HEAD = 4
HEAD = 128
H_CHUNK = 64
CONV_K = 3
SG_CHUNK = 128
SG_GROUPS = 4
D_FF = 4096
DEPTH = 4
N_CHIP = 4
IN_COLS = 9 * WIDTH + N_BRANCH * D_MODEL
GATE_COL0 = 9 * WIDTH
LB_FLOOR = 1e-30
NORM_EPS = 1e-6
LN_EPS = 1e-5
ADAM_LR, ADAM_B1, ADAM_B2, ADAM_EPS, ADAM_WD, ADAM_STEP = 0.001, 0.9, 0.999, 1e-08, 0.01, 10

VMEM_LIMIT_BYTES = 56 * 1024 * 1024
VMEM_BLOCK_BUDGET = 44 * 1024 * 1024
SUBLANES, LANES = 8, 128
ELEMWISE_BLOCK_BYTES = 2 * 1024 * 1024

NN = (((1,), (0,)), ((), ()))
NT = (((1,), (1,)), ((), ()))
TN = (((0,), (0,)), ((), ()))
MESH = pl.DeviceIdType.MESH
ANY = pl.BlockSpec(memory_space=pl.ANY)


def _dot(a, b, dims=NN):
    return lax.dot_general(a.astype(BF16), b.astype(BF16), dims, preferred_element_type=F32)


def _params(n_axes):
    return pltpu.CompilerParams(dimension_semantics=("arbitrary",) * n_axes, vmem_limit_bytes=VMEM_LIMIT_BYTES)


def _row0(part, rows=SUBLANES):
    r = lax.broadcasted_iota(jnp.int32, (rows, part.shape[1]), 0)
    return jnp.where(r == 0, part, 0.0)


def _token_tile(T):
    return min(512, T)


def _matmul(name, a, b, *, dims, grid, a_spec, b_spec, out_specs, out_shapes, acc_shape,
            extra=(), extra_specs=(), epilogue=None, after=()):
    nk = grid[2]
    n_extra, n_out, n_in = len(extra), len(out_shapes), 2 + len(extra) + len(after)
    one_step = nk == 1

    def body(*refs):
        a_ref, b_ref = refs[0], refs[1]
        ex = refs[2:2 + n_extra]
        outs = refs[n_in:n_in + n_out]
        part = _dot(a_ref[...], b_ref[...], dims)

        def finish(total):
            res = epilogue(total, *[e[...] for e in ex]) if epilogue else (total,)
            for o, r in zip(outs, res):
                o[...] = r.astype(o.dtype)

        if one_step:
            finish(part)
            return
        acc = refs[-1]
        kk = pl.program_id(2)

        @pl.when(kk == 0)
        def _():
            acc[...] = part

        @pl.when(kk > 0)
        def _():
            acc[...] += part

        @pl.when(kk == nk - 1)
        def _():
            finish(acc[...])

    return pl.pallas_call(
        body, name=name, grid=grid,
        in_specs=[a_spec, b_spec, *extra_specs, *[ANY] * len(after)], out_specs=list(out_specs),
        out_shape=list(out_shapes), scratch_shapes=[] if one_step else [pltpu.VMEM(acc_shape, F32)],
        compiler_params=_params(3),
    )(a, b, *extra, *after)


def _mm_cols(name, a, w, out_dtypes, epilogue=None, extra=()):
    T, K = a.shape
    N = w.shape[2]
    tm = min(2 * _token_tile(T), T)
    blk = pl.BlockSpec((tm, N), lambda j, i, kk: (i, j))
    return _matmul(
        name, a, w, dims=NN, grid=(N_CHIP, T // tm, 1),
        a_spec=pl.BlockSpec((tm, K), lambda j, i, kk: (i, 0)),
        b_spec=pl.BlockSpec((None, K, N), lambda j, i, kk: (j, 0, 0)),
        out_specs=[blk] * len(out_dtypes),
        out_shapes=[jax.ShapeDtypeStruct((T, N_CHIP * N), dt) for dt in out_dtypes],
        acc_shape=(tm, N), extra=extra, extra_specs=[blk] * len(extra), epilogue=epilogue)


def _mm_rows(name, a, w, res, norm_gain=None, after=()):
    T = a.shape[0]
    K, N = N_CHIP * w.shape[1], w.shape[2]
    tm = _token_tile(T)
    blk = pl.BlockSpec((tm, N), lambda i, j, kk: (i, 0))

    def with_norm(acc, r, gain):
        xv = acc + r
        return xv, xv * lax.rsqrt(jnp.mean(xv * xv, axis=-1, keepdims=True) + NORM_EPS) * gain

    normed = norm_gain is not None
    outs = _matmul(
        name, a, w.reshape(K, N), dims=NN, grid=(T // tm, 1, 1),
        a_spec=pl.BlockSpec((tm, K), lambda i, j, kk: (i, 0)),
        b_spec=pl.BlockSpec((K, N), lambda i, j, kk: (0, 0)),
        out_specs=[blk] * (2 if normed else 1),
        out_shapes=[jax.ShapeDtypeStruct((T, N), F32)] + ([jax.ShapeDtypeStruct((T, N), BF16)] if normed else []),
        acc_shape=(tm, N), extra=(res, norm_gain) if normed else (res,),
        extra_specs=[blk] + ([pl.BlockSpec((1, N), lambda i, j, kk: (0, 0))] if normed else []),
        epilogue=with_norm if normed else (lambda acc, r: (acc + r,)), after=after)
    return outs if normed else outs[0]


def _mm_cols_t(name, g, w, out_dtype, epilogue=None, extra=(), after=()):
    T, N = g.shape
    K = N_CHIP * w.shape[1]
    tm = _token_tile(T) if K <= 2 * D_MODEL else _token_tile(T) // 2
    blk = pl.BlockSpec((tm, K), lambda i, j, kk: (i, 0))
    return _matmul(
        name, g, w.reshape(K, N), dims=NT, grid=(T // tm, 1, 1),
        a_spec=pl.BlockSpec((tm, N), lambda i, j, kk: (i, 0)),
        b_spec=pl.BlockSpec((K, N), lambda i, j, kk: (0, 0)),
        out_specs=[blk], out_shapes=[jax.ShapeDtypeStruct((T, K), out_dtype)], acc_shape=(tm, K),
        extra=extra, extra_specs=[blk] * len(extra), epilogue=epilogue, after=after)[0]


def _dgrad_norm_bwd(name, g, w, x, gain, dres):
    T = g.shape[0]
    K, N = w.shape[1], w.shape[2]
    tm = _token_tile(T)
    whole = 2 * w.size * w.dtype.itemsize <= VMEM_BLOCK_BUDGET // 2

    def norm_bwd(i, dhv, x_ref, gain_ref, dres_ref, dx_ref, dgain_ref):
        xv = x_ref[...]
        r = lax.rsqrt(jnp.mean(xv * xv, axis=-1, keepdims=True) + NORM_EPS)
        xn = xv * r
        dxn = dhv * gain_ref[...]
        dx_ref[...] = dres_ref[...] + r * (dxn - xn * jnp.mean(dxn * xn, axis=-1, keepdims=True))

        @pl.when(i == 0)
        def _():
            dgain_ref[...] = jnp.zeros_like(dgain_ref)

        dgain_ref[...] += _row0(jnp.sum(dhv * xn, axis=0, keepdims=True))

    def body_whole(g_ref, w_ref, *rest):
        dhv = _dot(g_ref[:, pl.ds(0, N)], w_ref[0], NT)
        for k in range(1, N_CHIP):
            dhv = dhv + _dot(g_ref[:, pl.ds(k * N, N)], w_ref[k], NT)
        norm_bwd(pl.program_id(0), dhv, *rest)

    def body_steps(g_ref, w_ref, x_ref, gain_ref, dres_ref, dx_ref, dgain_ref, acc):
        kk = pl.program_id(1)
        part = _dot(g_ref[...], w_ref[...], NT)

        @pl.when(kk == 0)
        def _():
            acc[...] = part

        @pl.when(kk > 0)
        def _():
            acc[...] += part

        @pl.when(kk == N_CHIP - 1)
        def _():
            norm_bwd(pl.program_id(0), acc[...], x_ref, gain_ref, dres_ref, dx_ref, dgain_ref)

    tile = pl.BlockSpec((tm, K), lambda i, *kk: (i, 0))
    row = lambda cols: pl.BlockSpec((SUBLANES if cols is None else 1, K), lambda i, *kk: (0, 0))
    if whole:
        g_spec, w_spec = pl.BlockSpec((tm, N_CHIP * N), lambda i: (i, 0)), pl.BlockSpec(w.shape, lambda i: (0, 0, 0))
    else:
        g_spec = pl.BlockSpec((tm, N), lambda i, kk: (i, kk))
        w_spec = pl.BlockSpec((None, K, N), lambda i, kk: (kk, 0, 0))
    return pl.pallas_call(
        body_whole if whole else body_steps, name=name, grid=(T // tm,) if whole else (T // tm, N_CHIP),
        in_specs=[g_spec, w_spec, tile, row(1), tile], out_specs=[tile, row(None)],
        out_shape=[jax.ShapeDtypeStruct((T, K), F32), jax.ShapeDtypeStruct((SUBLANES, K), F32)],
        scratch_shapes=[] if whole else [pltpu.VMEM((tm, K), F32)],
        compiler_params=_params(1 if whole else 2))(g, w, x, gain, dres)


def _mm_wgrad(name, a, g, a_cols, g_cols, a_blocked, g_blocked, after=()):
    T = a.shape[0]
    tt = T
    while tt > LANES and 2 * 2 * tt * (a_cols + g_cols) + (2 if tt == T else 3) * 4 * a_cols * g_cols > VMEM_BLOCK_BUDGET:
        tt //= 2
    return _matmul(
        name, a, g, dims=TN, grid=(N_CHIP, 1, T // tt),
        a_spec=pl.BlockSpec((tt, a_cols), (lambda j, i, kk: (kk, j)) if a_blocked else (lambda j, i, kk: (kk, 0))),
        b_spec=pl.BlockSpec((tt, g_cols), (lambda j, i, kk: (kk, j)) if g_blocked else (lambda j, i, kk: (kk, 0))),
        out_specs=[pl.BlockSpec((None, a_cols, g_cols), lambda j, i, kk: (j, 0, 0))],
        out_shapes=[jax.ShapeDtypeStruct((N_CHIP, a_cols, g_cols), F32)], acc_shape=(a_cols, g_cols), after=after)[0]


def _rms_fwd(name, x, g, after=()):
    T, Dm = x.shape
    tm = min(256, T)

    def body(x_ref, g_ref, *rest):
        xv = x_ref[...]
        r = lax.rsqrt(jnp.mean(xv * xv, axis=-1, keepdims=True) + NORM_EPS)
        rest[-1][...] = (xv * r * g_ref[...]).astype(BF16)

    return pl.pallas_call(
        body, name=name, grid=(T // tm,),
        in_specs=[pl.BlockSpec((tm, Dm), lambda i: (i, 0)), pl.BlockSpec((1, Dm), lambda i: (0, 0))] + [ANY] * len(after),
        out_specs=pl.BlockSpec((tm, Dm), lambda i: (i, 0)),
        out_shape=jax.ShapeDtypeStruct((T, Dm), BF16), compiler_params=_params(1))(x, g, *after)


def _loss_head(x, g, tgt):
    T, Dm = x.shape
    tm = min(256, T)

    def body(x_ref, g_ref, t_ref, loss_ref, dx_ref, dg_ref):
        xv = x_ref[...]
        gv = g_ref[...]
        r = lax.rsqrt(jnp.mean(xv * xv, axis=-1, keepdims=True) + NORM_EPS)
        xn = xv * r
        err = xn * gv - t_ref[...]
        dy = err * (1.0 / Dm)
        dxn = dy * gv
        dx_ref[...] = r * (dxn - xn * jnp.mean(dxn * xn, axis=-1, keepdims=True))

        @pl.when(pl.program_id(0) == 0)
        def _():
            dg_ref[...] = jnp.zeros_like(dg_ref)
            loss_ref[...] = jnp.zeros_like(loss_ref)

        dg_ref[...] += _row0(jnp.sum(dy * xn, axis=0, keepdims=True))
        part = jnp.sum(jnp.sum(err * err, axis=-1, keepdims=True), axis=0, keepdims=True) * (0.5 / Dm)
        loss_ref[...] += jnp.broadcast_to(part, loss_ref.shape)

    tile = pl.BlockSpec((tm, Dm), lambda i: (i, 0))
    return pl.pallas_call(
        body, name="loss_head", grid=(T // tm,),
        in_specs=[tile, pl.BlockSpec((1, Dm), lambda i: (0, 0)), tile],
        out_specs=[pl.BlockSpec((SUBLANES, LANES), lambda i: (0, 0)), tile,
                   pl.BlockSpec((SUBLANES, Dm), lambda i: (0, 0))],
        out_shape=[jax.ShapeDtypeStruct((SUBLANES, LANES), F32), jax.ShapeDtypeStruct((T, Dm), F32),
                   jax.ShapeDtypeStruct((SUBLANES, Dm), F32)],
        compiler_params=_params(1))(x, g, tgt)


def _softmax_rows(lb_ref):
    rows = [lb_ref[pl.ds(i, 1), :] for i in range(DEPTH)]
    mx = rows[0]
    for r in rows[1:]:
        mx = jnp.maximum(mx, r)
    es = [jnp.exp(r - mx) for r in rows]
    tot = es[0]
    for e in es[1:]:
        tot = tot + e
    return [e / tot for e in es]


def _lbs_fwd(lower_bounds):
    def body(lb_ref, out_ref):
        sm = _softmax_rows(lb_ref)
        run = jnp.zeros_like(sm[0])
        out_ref[pl.ds(0, 1), :] = run
        for i in range(1, DEPTH):
            run = run + sm[i]
            out_ref[pl.ds(i, 1), :] = run

    return pl.pallas_call(body, name="lbs_fwd", out_shape=jax.ShapeDtypeStruct(lower_bounds.shape, F32))(lower_bounds)


def _lbs_bwd(lower_bounds, dlbs):
    def body(lb_ref, d_ref, out_ref):
        sm = _softmax_rows(lb_ref)
        dsm = [jnp.zeros_like(sm[0])]
        for i in range(1, DEPTH):
            acc = d_ref[pl.ds(i, 1), :]
            for l in range(i + 1, DEPTH):
                acc = acc + d_ref[pl.ds(l, 1), :]
            dsm.append(acc)
        inner = dsm[0] * sm[0]
        for i in range(1, DEPTH):
            inner = inner + dsm[i] * sm[i]
        for i in range(DEPTH):
            out_ref[pl.ds(i, 1), :] = sm[i] * (dsm[i] - inner)

    return pl.pallas_call(body, name="lbs_bwd", out_shape=jax.ShapeDtypeStruct(lower_bounds.shape, F32))(lower_bounds, dlbs)


N_LEVEL = 6


def _hgrn_consts():
    L = H_CHUNK
    t = np.arange(L)
    blocks = [(t[:, None] >= t[None, :]).astype(np.float32)]
    masks = []
    m = L // 2
    while m >= 1:
        blk, pos = t // (2 * m), t % (2 * m)
        start = blk * 2 * m
        mat = np.zeros((L, L), np.float32)
        for r in range(L):
            if pos[r] >= m:
                mat[r, start[r] + m:r + 1] = 1.0
            else:
                mat[r, r + 1:start[r] + m] = -1.0
        blocks.append(mat)
        masks.append(((blk[:, None] == blk[None, :]) & (pos[:, None] >= m) & (pos[None, :] < m)).astype(np.float32))
        m //= 2
    blocks.append(np.ones((L, L), np.float32))
    return jnp.asarray(np.concatenate(blocks, 0), BF16), jnp.asarray(np.stack(masks), F32)


def _hgrn_core(qraw, fp, lb, sum_mat, mask_ref):
    L = H_CHUNK
    sq = jax.nn.sigmoid(qraw)
    q = qraw * sq
    sneg = jax.nn.sigmoid(-fp)
    log_sig = jnp.minimum(fp, 0.0) - jnp.log1p(jnp.exp(-jnp.abs(fp)))
    a1 = jnp.log(jnp.maximum(lb, LB_FLOOR))
    a2 = jnp.log1p(-lb) + log_sig
    logf = jnp.maximum(a1, a2) + jnp.log1p(jnp.exp(-jnp.abs(a1 - a2)))
    w1 = jnp.exp(a1 - logf)
    w2 = jnp.exp(a2 - logf)
    k = (1.0 - lb) * sneg
    hi = logf.astype(BF16)
    r1 = logf - hi.astype(F32)
    mid = r1.astype(BF16)
    lo = (r1 - mid.astype(F32)).astype(BF16)
    sums = lax.dot_general(sum_mat, jnp.concatenate([hi, mid, lo], axis=1), NN, preferred_element_type=F32)
    sums = sums[:, 0:HEAD] + sums[:, HEAD:2 * HEAD] + sums[:, 2 * HEAD:3 * HEAD]
    b = sums[0:L]
    b_last = sums[(N_LEVEL + 1) * L:(N_LEVEL + 2) * L]
    eye = lax.broadcasted_iota(jnp.int32, (L, L), 0) == lax.broadcasted_iota(jnp.int32, (L, L), 1)
    attn = jnp.where(eye, jnp.sum(q * k, axis=1, keepdims=True), 0.0)
    fa, fb, ea, eb = [], [], [], []
    for l in range(N_LEVEL):
        d = sums[(l + 1) * L:(l + 2) * L]
        e_a = jnp.exp(jnp.minimum(d, 0.0))
        e_b = jnp.exp(jnp.minimum(-d, 0.0))
        a_l, b_l = q * e_a, k * e_b
        attn = attn + mask_ref[l] * _dot(a_l, b_l, NT)
        fa.append(a_l), fb.append(b_l), ea.append(e_a), eb.append(e_b)
    return dict(sq=sq, q=q, sneg=sneg, logf=logf, w1=w1, w2=w2, k=k, b=b, b_last=b_last, attn=attn,
                fa=fa, fb=fb, ea=ea, eb=eb)


def _hgrn_fwd(p, lbrow, gout):
    T = p.shape[0]
    nch = T // H_CHUNK
    sum_mat, masks = _hgrn_consts()

    def body(p_ref, lb_ref, g_ref, m_ref, mask_ref, o_ref, z_ref, st_ref, state):
        @pl.when(pl.program_id(0) == 0)
        def _():
            state[...] = jnp.zeros_like(state)

        sum_m = m_ref[...]
        for h in range(N_HEAD):
            col = lambda part: pl.ds(part * WIDTH + h * HEAD, HEAD)
            hs = pl.ds(h * HEAD, HEAD)
            v = p_ref[:, col(2)]
            c = _hgrn_core(p_ref[:, col(0)], p_ref[:, col(1)], lb_ref[:, hs], sum_m, mask_ref)
            s0 = state[h]
            st_ref[h] = s0
            o = _dot(c["attn"], v) + _dot(c["q"] * jnp.exp(c["b"]), s0, NT)
            k_dec = c["k"] * jnp.exp(c["b_last"] - c["b"])
            decay = jnp.exp(jnp.max(c["b_last"], axis=0, keepdims=True))
            state[h] = s0 * decay + _dot(v, k_dec, TN)
            o_ref[:, hs] = o
            r = lax.rsqrt(jnp.mean(o * o, axis=-1, keepdims=True) + NORM_EPS)
            z_ref[:, hs] = (o * r * g_ref[:, hs] * jax.nn.sigmoid(p_ref[:, col(3)])).astype(BF16)

    full = lambda shape: pl.BlockSpec(shape, lambda c: (0,) * len(shape))
    return pl.pallas_call(
        body, name="hgrn_fwd", grid=(nch,),
        in_specs=[pl.BlockSpec((H_CHUNK, 4 * WIDTH), lambda c: (c, 0)), full((1, WIDTH)), full((1, WIDTH)),
                  full(sum_mat.shape), full(masks.shape)],
        out_specs=[pl.BlockSpec((H_CHUNK, WIDTH), lambda c: (c, 0)),
                   pl.BlockSpec((None, H_CHUNK, WIDTH), lambda c: (0, c, 0)),
                   pl.BlockSpec((None, N_HEAD, HEAD, HEAD), lambda c: (c, 0, 0, 0))],
        out_shape=[jax.ShapeDtypeStruct((T, WIDTH), F32), jax.ShapeDtypeStruct((N_BRANCH, T, WIDTH), BF16),
                   jax.ShapeDtypeStruct((nch, N_HEAD, HEAD, HEAD), F32)],
        scratch_shapes=[pltpu.VMEM((N_HEAD, HEAD, HEAD), F32)], compiler_params=_params(1),
    )(p, lbrow, gout, sum_mat, masks)


def _hgrn_bwd(p, o_saved, dz, states, lbrow, gout, dp, after=()):
    T = p.shape[0]
    nch = T // H_CHUNK
    L = H_CHUNK
    sum_mat, masks = _hgrn_consts()

    def body(p_ref, o_ref, dz_ref, st_ref, lb_ref, g_ref, m_ref, mask_ref, dp_in, *rest):
        del dp_in
        dp_ref, dlb_ref, dg_ref, dstate = rest[len(after):]

        @pl.when(pl.program_id(0) == 0)
        def _():
            dstate[...] = jnp.zeros_like(dstate)
            dlb_ref[...] = jnp.zeros_like(dlb_ref)
            dg_ref[...] = jnp.zeros_like(dg_ref)

        sum_m = m_ref[...]
        for h in range(N_HEAD):
            col = lambda part: pl.ds(part * WIDTH + h * HEAD, HEAD)
            hs = pl.ds(h * HEAD, HEAD)
            qraw, fp, v, go = p_ref[:, col(0)], p_ref[:, col(1)], p_ref[:, col(2)], p_ref[:, col(3)]
            lb, g = lb_ref[:, hs], g_ref[:, hs]
            c = _hgrn_core(qraw, fp, lb, sum_m, mask_ref)
            q, k, b, b_last = c["q"], c["k"], c["b"], c["b_last"]
            s0, ds1 = st_ref[h], dstate[h]
            e_b = jnp.exp(b)
            q_dec = q * e_b
            e_bl = jnp.exp(b_last - b)
            k_dec = k * e_bl
            decay = jnp.exp(jnp.max(b_last, axis=0, keepdims=True))
            o = o_ref[:, hs]
            r = lax.rsqrt(jnp.mean(o * o, axis=-1, keepdims=True) + NORM_EPS)
            n = o * r
            sgo = jax.nn.sigmoid(go)
            dza = dz_ref[:, hs]
            dgo = dza * n * g * sgo * (1.0 - sgo)
            dg_ref[:, hs] += _row0(jnp.sum(dza * n * sgo, axis=0, keepdims=True))
            dn = dza * g * sgo
            do = r * (dn - n * jnp.mean(dn * n, axis=-1, keepdims=True))
            dattn = _dot(do, v, NT)
            dv = _dot(c["attn"], do, TN) + _dot(k_dec, ds1, NT)
            dq_dec = _dot(do, s0)
            dk_dec = _dot(v, ds1)
            ddiag = jnp.sum(do * v, axis=1, keepdims=True)
            dq = dq_dec * e_b + ddiag * k
            dk = dk_dec * e_bl + ddiag * q
            dsums = [dq_dec * q_dec - dk_dec * k_dec]
            for l in range(N_LEVEL):
                dm = mask_ref[l] * dattn
                da = _dot(dm, c["fb"][l])
                db = _dot(dm, c["fa"][l], TN)
                dq = dq + da * c["ea"][l]
                dk = dk + db * c["eb"][l]
                dsums.append(da * c["fa"][l] - db * c["fb"][l])
            dlast = jnp.sum(ds1 * s0, axis=0, keepdims=True) * decay
            dsums.append(dk_dec * k_dec + _row0(dlast, L))
            dlogf = _dot(sum_m, jnp.concatenate(dsums, axis=0), TN)
            dstate[h] = ds1 * decay + _dot(do, q_dec, TN)
            sq, sneg = c["sq"], c["sneg"]
            dqraw = dq * sq * (1.0 + qraw * (1.0 - sq))
            dfp = dlogf * c["w2"] * sneg - dk * (1.0 - lb) * sneg * (1.0 - sneg)
            inv_lb = jnp.where(lb > LB_FLOOR, 1.0 / jnp.maximum(lb, LB_FLOOR), 0.0)
            dlb_tok = dlogf * (c["w1"] * inv_lb - c["w2"] / (1.0 - lb)) - dk * sneg
            dlb_ref[:, hs] += _row0(jnp.sum(dlb_tok, axis=0, keepdims=True))
            dp_ref[:, col(0)] = dqraw.astype(BF16)
            dp_ref[:, col(1)] = dfp.astype(BF16)
            dp_ref[:, col(2)] = dv.astype(BF16)
            dp_ref[:, col(3)] = dgo.astype(BF16)

    full = lambda shape: pl.BlockSpec(shape, lambda c: (0,) * len(shape))
    rev = lambda c: nch - 1 - c
    return pl.pallas_call(
        body, name="hgrn_bwd", grid=(nch,),
        in_specs=[pl.BlockSpec((L, 4 * WIDTH), lambda c: (rev(c), 0)), pl.BlockSpec((L, WIDTH), lambda c: (rev(c), 0)),
                  pl.BlockSpec((None, L, WIDTH), lambda c: (0, rev(c), 0)),
                  pl.BlockSpec((None, N_HEAD, HEAD, HEAD), lambda c: (rev(c), 0, 0, 0)),
                  full((1, WIDTH)), full((1, WIDTH)), full(sum_mat.shape), full(masks.shape), ANY, *[ANY] * len(after)],
        out_specs=[pl.BlockSpec((L, 4 * WIDTH), lambda c: (rev(c), 0)), full((SUBLANES, WIDTH)), full((SUBLANES, WIDTH))],
        out_shape=[jax.ShapeDtypeStruct(dp.shape, dp.dtype), jax.ShapeDtypeStruct((SUBLANES, WIDTH), F32),
                   jax.ShapeDtypeStruct((SUBLANES, WIDTH), F32)],
        scratch_shapes=[pltpu.VMEM((N_HEAD, HEAD, HEAD), F32)], input_output_aliases={8: 0},
        compiler_params=_params(1),
    )(p, o_saved, dz, states, lbrow, gout, sum_mat, masks, dp, *after)


def _shift_down(tile, halo, s):
    tm = tile.shape[0]
    rows = lax.broadcasted_iota(jnp.int32, tile.shape, 0)
    head = jnp.concatenate([pltpu.roll(halo, s, 0), jnp.zeros((tm - SUBLANES, tile.shape[1]), tile.dtype)], axis=0)
    return jnp.where(rows < s, head, pltpu.roll(tile, s, 0))


def _shift_up(tile, halo, s):
    tm = tile.shape[0]
    rows = lax.broadcasted_iota(jnp.int32, tile.shape, 0)
    tail = jnp.concatenate([jnp.zeros((tm - SUBLANES, tile.shape[1]), tile.dtype), pltpu.roll(halo, SUBLANES - s, 0)], axis=0)
    return jnp.where(rows >= tm - s, tail, pltpu.roll(tile, tm - s, 0))


def _conv_fwd(p, w, z, after=()):
    T = p.shape[0]
    tm = _token_tile(T)
    per = tm // SUBLANES

    def body(bg_ref, cg_ref, xc_ref, hcg_ref, hxc_ref, w_ref, *rest):
        z_ref = rest[-1]
        zc = cg_ref[...] * xc_ref[...]
        hz = jnp.where(pl.program_id(0) > 0, hcg_ref[...] * hxc_ref[...], 0.0)
        y = (w_ref[pl.ds(0, 1), :] * _shift_down(zc, hz, 2) + w_ref[pl.ds(1, 1), :] * _shift_down(zc, hz, 1)
             + w_ref[pl.ds(2, 1), :] * zc)
        z_ref[...] = (bg_ref[...] * y).astype(BF16)

    tile = lambda cb: pl.BlockSpec((tm, WIDTH), lambda i: (i, cb))
    prev = lambda cb: pl.BlockSpec((SUBLANES, WIDTH), lambda i: (jnp.maximum(i * per - 1, 0), cb))
    return pl.pallas_call(
        body, name="conv_fwd", grid=(T // tm,),
        in_specs=[tile(4), tile(5), tile(6), prev(5), prev(6), pl.BlockSpec((CONV_K, WIDTH), lambda i: (0, 0)), ANY,
                  *[ANY] * len(after)],
        out_specs=pl.BlockSpec((None, tm, WIDTH), lambda i: (1, i, 0)),
        out_shape=jax.ShapeDtypeStruct(z.shape, z.dtype), input_output_aliases={6: 0}, compiler_params=_params(1),
    )(p, p, p, p, p, w, z, *after)


def _conv_bwd(p, w, dz, dp):
    T = p.shape[0]
    tm = _token_tile(T)
    per = tm // SUBLANES
    last = T // SUBLANES - 1

    def body(bg_ref, cg_ref, xc_ref, hcg_ref, hxc_ref, nbg_ref, dzb_ref, ndzb_ref, w_ref, dp_in, dp_ref, dw_ref, stash):
        del dp_in
        i, jj = pl.program_id(0), pl.program_id(1)

        @pl.when(jnp.logical_and(i == 0, jj == 0))
        def _():
            dw_ref[...] = jnp.zeros_like(dw_ref)

        @pl.when(jj == 0)
        def _():
            cg, xc, bg = cg_ref[...], xc_ref[...], bg_ref[...]
            w0, w1, w2 = w_ref[pl.ds(0, 1), :], w_ref[pl.ds(1, 1), :], w_ref[pl.ds(2, 1), :]
            zc = cg * xc
            hz = jnp.where(i > 0, hcg_ref[...] * hxc_ref[...], 0.0)
            z2, z1 = _shift_down(zc, hz, 2), _shift_down(zc, hz, 1)
            y = w0 * z2 + w1 * z1 + w2 * zc
            dzb = dzb_ref[...]
            dy = dzb * bg
            hdy = jnp.where(i < pl.num_programs(0) - 1, ndzb_ref[...] * nbg_ref[...], 0.0)
            dzc = w2 * dy + w1 * _shift_up(dy, hdy, 1) + w0 * _shift_up(dy, hdy, 2)
            rows = lax.broadcasted_iota(jnp.int32, (SUBLANES, WIDTH), 0)
            colsum = lambda t: jnp.sum(t, axis=0, keepdims=True)
            dw_ref[...] += (jnp.where(rows == 0, colsum(dy * z2), 0.0) + jnp.where(rows == 1, colsum(dy * z1), 0.0)
                            + jnp.where(rows == 2, colsum(dy * zc), 0.0))
            dp_ref[...] = (dzb * y).astype(BF16)
            stash[0] = dzc * xc
            stash[1] = dzc * cg

        @pl.when(jj > 0)
        def _():
            dp_ref[...] = stash[jj - 1].astype(BF16)

    n_tiles = T // tm
    tile = lambda cb: pl.BlockSpec((tm, WIDTH), lambda i, jj: (i, cb))
    prev = lambda cb: pl.BlockSpec((SUBLANES, WIDTH), lambda i, jj: (jnp.maximum(i * per - 1, 0), cb))
    nxt = lambda i: jnp.minimum((i + 1) * per, last)
    return pl.pallas_call(
        body, name="conv_bwd", grid=(n_tiles, 3),
        in_specs=[tile(4), tile(5), tile(6), prev(5), prev(6),
                  pl.BlockSpec((SUBLANES, WIDTH), lambda i, jj: (nxt(i), 4)),
                  pl.BlockSpec((None, tm, WIDTH), lambda i, jj: (1, i, 0)),
                  pl.BlockSpec((None, SUBLANES, WIDTH), lambda i, jj: (1, nxt(i), 0)),
                  pl.BlockSpec((CONV_K, WIDTH), lambda i, jj: (0, 0)), ANY],
        out_specs=[pl.BlockSpec((tm, WIDTH), lambda i, jj: (i, 4 + jj)),
                   pl.BlockSpec((SUBLANES, WIDTH), lambda i, jj: (0, 0))],
        out_shape=[jax.ShapeDtypeStruct(dp.shape, dp.dtype), jax.ShapeDtypeStruct((SUBLANES, WIDTH), F32)],
        scratch_shapes=[pltpu.VMEM((2, tm, WIDTH), F32)], input_output_aliases={9: 0}, compiler_params=_params(2),
    )(p, p, p, p, p, p, dz, dz, w, dp)


GELU_C = float(np.sqrt(2.0 / np.pi))
GELU_A = 0.044715


def _gelu(x):
    th = jnp.tanh(GELU_C * (x + GELU_A * x * x * x))
    return 0.5 * x * (1.0 + th), th


def _gelu_grad(x, th):
    return 0.5 * (1.0 + th) + 0.5 * x * (1.0 - th * th) * GELU_C * (1.0 + 3.0 * GELU_A * x * x)


def _sg_core(u, v, lng, lnb, ws_ref, bs_ref):
    gu, thu = _gelu(u)
    gv, thv = _gelu(v)
    xc = gv - jnp.mean(gv, axis=-1, keepdims=True)
    rs = lax.rsqrt(jnp.mean(xc * xc, axis=-1, keepdims=True) + LN_EPS)
    xh = xc * rs
    vp = xh * lng + lnb
    tril = (lax.broadcasted_iota(jnp.int32, (SG_CHUNK, SG_CHUNK), 0)
            >= lax.broadcasted_iota(jnp.int32, (SG_CHUNK, SG_CHUNK), 1))
    wm = [jnp.where(tril, ws_ref[g], 0.0).astype(BF16) for g in range(SG_GROUPS)]
    gs = lambda t, g: t[:, g * LANES:(g + 1) * LANES]
    sv = jnp.concatenate([_dot(wm[g], gs(vp, g)) + bs_ref[g] for g in range(SG_GROUPS)], axis=1)
    return dict(gu=gu, thu=thu, thv=thv, rs=rs, xh=xh, vp=vp, tril=tril, wm=wm, sv=sv)


def _sg_fwd(p, lng, lnb, ws, bs, z):
    T = p.shape[0]

    def body(u_ref, v_ref, lng_ref, lnb_ref, ws_ref, bs_ref, z_in, z_ref):
        del z_in
        c = _sg_core(u_ref[...], v_ref[...], lng_ref[...], lnb_ref[...], ws_ref, bs_ref)
        z_ref[...] = (c["gu"] * c["sv"]).astype(BF16)

    full = lambda shape: pl.BlockSpec(shape, lambda c: (0,) * len(shape))
    return pl.pallas_call(
        body, name="sg_fwd", grid=(T // SG_CHUNK,),
        in_specs=[pl.BlockSpec((SG_CHUNK, WIDTH), lambda c: (c, 7)), pl.BlockSpec((SG_CHUNK, WIDTH), lambda c: (c, 8)),
                  full((1, WIDTH)), full((1, WIDTH)), full(ws.shape), full(bs.shape), ANY],
        out_specs=pl.BlockSpec((None, SG_CHUNK, WIDTH), lambda c: (2, c, 0)),
        out_shape=jax.ShapeDtypeStruct(z.shape, z.dtype), input_output_aliases={6: 0}, compiler_params=_params(1),
    )(p, p, lng, lnb, ws, bs, z)


def _sg_bwd(p, lng, lnb, ws, bs, dz, dp):
    T = p.shape[0]

    def body(u_ref, v_ref, lng_ref, lnb_ref, ws_ref, bs_ref, dz_ref, dp_in, dp_ref, dws_ref, dbs_ref, dlng_ref, dlnb_ref,
             stash):
        del dp_in
        cidx, jj = pl.program_id(0), pl.program_id(1)

        @pl.when(jnp.logical_and(cidx == 0, jj == 0))
        def _():
            dws_ref[...] = jnp.zeros_like(dws_ref)
            dbs_ref[...] = jnp.zeros_like(dbs_ref)
            dlng_ref[...] = jnp.zeros_like(dlng_ref)
            dlnb_ref[...] = jnp.zeros_like(dlnb_ref)

        @pl.when(jj == 0)
        def _():
            u, v, lng = u_ref[...], v_ref[...], lng_ref[...]
            c = _sg_core(u, v, lng, lnb_ref[...], ws_ref, bs_ref)
            dzc = dz_ref[...]
            gs = lambda t, g: t[:, g * LANES:(g + 1) * LANES]
            dsv = dzc * c["gu"]
            dvp = []
            for g in range(SG_GROUPS):
                dsv_g = gs(dsv, g)
                dws_ref[g] += jnp.where(c["tril"], _dot(dsv_g, gs(c["vp"], g), NT), 0.0)
                dbs_ref[g] += jnp.sum(dsv_g, axis=1, keepdims=True)
                dvp.append(_dot(c["wm"][g], dsv_g, TN))
            dvp = jnp.concatenate(dvp, axis=1)
            xh = c["xh"]
            dlng_ref[...] += _row0(jnp.sum(dvp * xh, axis=0, keepdims=True))
            dlnb_ref[...] += _row0(jnp.sum(dvp, axis=0, keepdims=True))
            dxh = dvp * lng
            dgv = c["rs"] * (dxh - jnp.mean(dxh, axis=-1, keepdims=True) - xh * jnp.mean(dxh * xh, axis=-1, keepdims=True))
            dp_ref[...] = (dzc * c["sv"] * _gelu_grad(u, c["thu"])).astype(BF16)
            stash[...] = dgv * _gelu_grad(v, c["thv"])

        @pl.when(jj == 1)
        def _():
            dp_ref[...] = stash[...].astype(BF16)

    full = lambda shape: pl.BlockSpec(shape, lambda c, jj: (0,) * len(shape))
    return pl.pallas_call(
        body, name="sg_bwd", grid=(T // SG_CHUNK, 2),
        in_specs=[pl.BlockSpec((SG_CHUNK, WIDTH), lambda c, jj: (c, 7)), pl.BlockSpec((SG_CHUNK, WIDTH), lambda c, jj: (c, 8)),
                  full((1, WIDTH)), full((1, WIDTH)), full(ws.shape), full(bs.shape),
                  pl.BlockSpec((None, SG_CHUNK, WIDTH), lambda c, jj: (2, c, 0)), ANY],
        out_specs=[pl.BlockSpec((SG_CHUNK, WIDTH), lambda c, jj: (c, 7 + jj)), full(ws.shape), full(bs.shape),
                   full((SUBLANES, WIDTH)), full((SUBLANES, WIDTH))],
        out_shape=[jax.ShapeDtypeStruct(dp.shape, dp.dtype), jax.ShapeDtypeStruct(ws.shape, F32),
                   jax.ShapeDtypeStruct(bs.shape, F32), jax.ShapeDtypeStruct((SUBLANES, WIDTH), F32),
                   jax.ShapeDtypeStruct((SUBLANES, WIDTH), F32)],
        scratch_shapes=[pltpu.VMEM((SG_CHUNK, WIDTH), F32)], input_output_aliases={7: 0}, compiler_params=_params(2),
    )(p, p, lng, lnb, ws, bs, dz, dp)


BRANCH_COLS = D_MODEL // N_CHIP
GATE_UNIT0 = GATE_COL0 // WIDTH
UNITS = D_MODEL // WIDTH


def _unit_specs(order):
    def spec(which):
        def index(*g):
            _, n, u = order(*g)
            return (2 * u + which, n, 0, 0)
        return pl.BlockSpec((None, None, WIDTH, BRANCH_COLS), index)
    return [spec(0), spec(1)]


def _merge_fwd(z, p, wb):
    T = z.shape[1]
    tm = _token_tile(T)
    order = lambda i, u, n: (i, n, u)

    def body(z_ref, wa_ref, wb_ref, gt_ref, out_ref, acc):
        n = pl.program_id(2)
        zv = z_ref[...]
        y = jnp.concatenate([_dot(zv, wa_ref[...]), _dot(zv, wb_ref[...])], axis=1)
        part = jax.nn.sigmoid(gt_ref[...]) * y

        @pl.when(n == 0)
        def _():
            acc[...] = part

        @pl.when(n > 0)
        def _():
            acc[...] += part

        @pl.when(n == N_BRANCH - 1)
        def _():
            out_ref[...] = acc[...].astype(BF16)

    return pl.pallas_call(
        body, name="merge_fwd", grid=(T // tm, UNITS, N_BRANCH),
        in_specs=[pl.BlockSpec((None, tm, WIDTH), lambda i, u, n: (n, i, 0)), *_unit_specs(order),
                  pl.BlockSpec((tm, WIDTH), lambda i, u, n: (i, GATE_UNIT0 + UNITS * n + u))],
        out_specs=pl.BlockSpec((tm, WIDTH), lambda i, u, n: (i, u)),
        out_shape=jax.ShapeDtypeStruct((T, D_MODEL), BF16),
        scratch_shapes=[pltpu.VMEM((tm, WIDTH), F32)], compiler_params=_params(3))(z, wb, wb, p)


def _merge_bwd(z, p, wb, dmerged):
    T = z.shape[1]
    tm = _token_tile(T)
    order = lambda i, n, u: (i, n, u)

    def body(z_ref, wa_ref, wb_ref, gt_ref, dm_ref, dp_ref, dy_ref, dz_ref):
        u = pl.program_id(2)
        zv, wa, wbv = z_ref[...], wa_ref[...], wb_ref[...]
        y = jnp.concatenate([_dot(zv, wa), _dot(zv, wbv)], axis=1)
        gate = jax.nn.sigmoid(gt_ref[...])
        dm = dm_ref[...]
        dp_ref[...] = (dm * y * gate * (1.0 - gate)).astype(BF16)
        dyv = (dm * gate).astype(BF16)
        dy_ref[...] = dyv
        part = _dot(dyv[:, :BRANCH_COLS], wa, NT) + _dot(dyv[:, BRANCH_COLS:], wbv, NT)

        @pl.when(u == 0)
        def _():
            dz_ref[...] = part

        @pl.when(u > 0)
        def _():
            dz_ref[...] += part

    unit = lambda i, n, u: (i, GATE_UNIT0 + UNITS * n + u)
    return pl.pallas_call(
        body, name="merge_bwd", grid=(T // tm, N_BRANCH, UNITS),
        in_specs=[pl.BlockSpec((None, tm, WIDTH), lambda i, n, u: (n, i, 0)), *_unit_specs(order),
                  pl.BlockSpec((tm, WIDTH), unit), pl.BlockSpec((tm, WIDTH), lambda i, n, u: (i, u))],
        out_specs=[pl.BlockSpec((tm, WIDTH), unit), pl.BlockSpec((None, tm, WIDTH), lambda i, n, u: (n, i, u)),
                   pl.BlockSpec((None, tm, WIDTH), lambda i, n, u: (n, i, 0))],
        out_shape=[jax.ShapeDtypeStruct((T, IN_COLS), BF16), jax.ShapeDtypeStruct((N_BRANCH, T, D_MODEL), BF16),
                   jax.ShapeDtypeStruct((N_BRANCH, T, WIDTH), F32)],
        compiler_params=_params(3))(z, wb, wb, p, dmerged)


def _branch_wgrad(z, dy):
    T = z.shape[1]
    tt = T
    nk = T // tt

    def body(z_ref, dy_ref, out_ref, acc):
        kk = pl.program_id(1)
        part = _dot(z_ref[...], dy_ref[...], TN)

        @pl.when(kk == 0)
        def _():
            acc[...] = part

        @pl.when(kk > 0)
        def _():
            acc[...] += part

        @pl.when(kk == nk - 1)
        def _():
            for k in range(N_CHIP):
                out_ref[k] = acc[:, k * BRANCH_COLS:(k + 1) * BRANCH_COLS]

    return pl.pallas_call(
        body, name="branch_wgrad", grid=(N_BRANCH, nk),
        in_specs=[pl.BlockSpec((None, tt, WIDTH), lambda n, kk: (n, kk, 0)),
                  pl.BlockSpec((None, tt, D_MODEL), lambda n, kk: (n, kk, 0))],
        out_specs=pl.BlockSpec((N_CHIP, None, WIDTH, BRANCH_COLS), lambda n, kk: (0, n, 0, 0)),
        out_shape=jax.ShapeDtypeStruct((N_CHIP, N_BRANCH, WIDTH, BRANCH_COLS), F32),
        scratch_shapes=[pltpu.VMEM((WIDTH, D_MODEL), F32)], compiler_params=_params(2))(z, dy)


def _layer_fwd(x, h, win, small, next_gain, hooks):
    p = _mm_cols("in_proj", h, win, [F32])[0]
    o_hgrn, z, states = _hgrn_fwd(p, small["lbs"], small["g_hgrn_out"])
    z = _conv_fwd(p, small["w_conv"], z, after=hooks["after_hgrn"]([o_hgrn]))
    z = _sg_fwd(p, small["sg_ln_g"], small["sg_ln_b"], small["w_sg"], small["b_sg"], z)
    wb, wo, w1, w2 = hooks["late_weights"]([z])
    wb = wb.reshape(N_CHIP, N_BRANCH, WIDTH, BRANCH_COLS)
    merged = _merge_fwd(z, p, wb)
    x_mid, h2 = _mm_rows("out_proj", merged, wo, x, small["g_ffn"])
    s = _mm_cols("ff1", h2, w1, [BF16], epilogue=lambda acc: (jnp.square(jnp.maximum(acc, 0.0)),))[0]
    if next_gain is None:
        x_out, h_next = _mm_rows("ff2_last", s, w2, x_mid, after=hooks["before_last"]([s])), None
    else:
        x_out, h_next = _mm_rows("ff2", s, w2, x_mid, next_gain, after=hooks["before_last"]([s]))
    saved = dict(x=x, h=h, p=p, o_hgrn=o_hgrn, z=z, states=states, merged=merged, x_mid=x_mid, h2=h2, s=s)
    return x_out, h_next, saved, [win, wb, wo, w1, w2]


def _layer_bwd(dx_out, sv, wts, small, tick, after):
    win, wb, wo, w1, w2 = wts
    g = {}
    da = _mm_cols_t("ff2_dgrad", dx_out, w2, BF16, extra=(sv["s"],), after=after,
                    epilogue=lambda acc, s: (acc * 2.0 * jnp.sqrt(s.astype(F32)),))
    d_ff2 = _mm_wgrad("ff2_wgrad", sv["s"], dx_out, w2.shape[1], D_MODEL, True, False)
    d_ff1 = _mm_wgrad("ff1_wgrad", sv["h2"], da, D_MODEL, w1.shape[2], False, True)
    dx_mid, g["g_ffn"] = _dgrad_norm_bwd("ff1_dgrad", da, w1, sv["x_mid"], small["g_ffn"], dx_out)
    after = tick([dx_mid], [("w_ff1", d_ff1), ("w_ff2", d_ff2)])
    dmerged = _mm_cols_t("out_proj_dgrad", dx_mid, wo, F32, after=after)
    d_o = _mm_wgrad("out_proj_wgrad", sv["merged"], dx_mid, wo.shape[1], D_MODEL, True, False)
    dp, dy, dz = _merge_bwd(sv["z"], sv["p"], wb, dmerged)
    d_branch = _branch_wgrad(sv["z"], dy).reshape(N_CHIP, N_BRANCH * WIDTH, BRANCH_COLS)
    after = tick([dp], [("w_branch", d_branch), ("w_o", d_o)])
    dp, g["lbs"], g["g_hgrn_out"] = _hgrn_bwd(sv["p"], sv["o_hgrn"], dz, sv["states"], small["lbs"],
                                              small["g_hgrn_out"], dp, after=after)
    dp, g["w_conv"] = _conv_bwd(sv["p"], small["w_conv"], dz, dp)
    dp, g["w_sg"], g["b_sg"], g["sg_ln_g"], g["sg_ln_b"] = _sg_bwd(
        sv["p"], small["sg_ln_g"], small["sg_ln_b"], small["w_sg"], small["b_sg"], dz, dp)
    after = tick([dp], [])
    d_in = _mm_wgrad("in_proj_wgrad", sv["h"], dp, D_MODEL, win.shape[2], False, True, after=after)
    dx, g["g_mix"] = _dgrad_norm_bwd("in_proj_dgrad", dp, win, sv["x"], small["g_mix"], dx_mid)
    return dx, g, tick([dx], [("w_in", d_in)])


def _mesh_pos():
    return lax.axis_index("x"), lax.axis_index("y"), lax.axis_index("c")


def _other_chips(x, y):
    return [(1 - x, y), (x, 1 - y), (1 - x, 1 - y)]


def _remote(src, dst, send_sems, recv_sems, k, to):
    return pltpu.make_async_remote_copy(src_ref=src, dst_ref=dst, send_sem=send_sems.at[k], recv_sem=recv_sems.at[k],
                                        device_id=to, device_id_type=MESH)


def _gather_call(name, body, buf, after):
    scratch = [pltpu.SemaphoreType.DMA((7,)), pltpu.SemaphoreType.DMA((7,))]
    return pl.pallas_call(
        body, name=name, in_specs=[ANY] * (1 + len(after)), out_specs=ANY,
        out_shape=jax.ShapeDtypeStruct(buf.shape, buf.dtype), scratch_shapes=scratch, input_output_aliases={0: 0})(buf, *after)


HBM = pl.BlockSpec(memory_space=pltpu.HBM)
SEM = pl.BlockSpec(memory_space=pltpu.SEMAPHORE)
DATAFLOW = pltpu.SideEffectType.DATAFLOW_SIDE_EFFECTING


def _split_start(name, bufs, copies, n_copies, after=()):
    n = len(bufs)

    def body(*refs):
        send_sems, recv_sems = refs[n + len(after)], refs[n + len(after) + 1]
        for cp in copies(refs[:n], send_sems, recv_sems):
            cp.start()
        refs[-1][...] = jnp.zeros_like(refs[-1])

    outs = pl.pallas_call(
        body, name=name,
        out_shape=(pltpu.SemaphoreType.DMA((n_copies,)), pltpu.SemaphoreType.DMA((n_copies,)),
                   *[pltpu.HBM(b.shape, b.dtype) for b in bufs], jax.ShapeDtypeStruct((SUBLANES, LANES), F32)),
        in_specs=[HBM] * n + [ANY] * len(after),
        out_specs=(SEM, SEM, *[HBM] * n, pl.BlockSpec(memory_space=pltpu.VMEM)),
        input_output_aliases={t: 2 + t for t in range(n)},
        compiler_params=pltpu.CompilerParams(has_side_effects=DATAFLOW),
    )(*[pltpu.with_memory_space_constraint(b, pltpu.HBM) for b in bufs], *after)
    return outs[0], outs[1], list(outs[2:2 + n]), outs[-1]


def _split_wait(name, started, copies, after):
    send_sems, recv_sems, bufs, _ = started
    n = len(bufs)

    def body(*refs):
        for cp in copies(refs[:n], refs[n], refs[n + 1]):
            cp.wait_send()
            cp.wait_recv()

    return list(pl.pallas_call(
        body, name=name, out_shape=tuple(pltpu.HBM(b.shape, b.dtype) for b in bufs),
        in_specs=[HBM] * n + [SEM, SEM] + [ANY] * len(after), out_specs=tuple([HBM] * n),
        input_output_aliases={t: t for t in range(n)},
        compiler_params=pltpu.CompilerParams(has_side_effects=DATAFLOW),
    )(*bufs, send_sems, recv_sems, *after))


def _weight_ici_copies(refs, send_sems, recv_sems):
    x, y, c = _mesh_pos()
    out = []
    for t, ref in enumerate(refs):
        rh = ref.shape[1] // 2
        mine = ref.at[2 * x + y, pl.ds(c * rh, rh), :]
        out += [_remote(mine, mine, send_sems, recv_sems, 3 * t + j, (*chip, c)) for j, chip in enumerate(_other_chips(x, y))]
    return out


def _weight_d2d_copies(refs, send_sems, recv_sems):
    x, y, c = _mesh_pos()
    out = []
    for t, ref in enumerate(refs):
        rh = ref.shape[1] // 2
        for j, chip in enumerate(_other_chips(x, y)):
            blk = ref.at[2 * chip[0] + chip[1], pl.ds(c * rh, rh), :]
            out.append(_remote(blk, blk, send_sems, recv_sems, 3 * t + j, (x, y, 1 - c)))
    return out


def _swap_part(refs, send_sems, recv_sems, s0):
    x, y, c = _mesh_pos()
    n = len(refs) // 2
    out = []
    for t in range(n):
        rh = refs[t].shape[1] // 2
        out.append(_remote(refs[t].at[:, pl.ds((1 - c) * rh, rh), :], refs[n + t], send_sems, recv_sems, s0 + t, (x, y, 1 - c)))
    return out


def _exchange_part(refs, send_sems, recv_sems, s0):
    x, y, c = _mesh_pos()
    n = len(refs) // 2
    out = []
    for t in range(n):
        for j, chip in enumerate(_other_chips(x, y)):
            out.append(_remote(refs[t].at[2 * chip[0] + chip[1]], refs[n + t].at[j], send_sems, recv_sems, s0 + 3 * t + j,
                               (*chip, c)))
    return out


def _gather_part(refs, send_sems, recv_sems, s0):
    x, y, c = _mesh_pos()
    return [_remote(ref.at[c], ref.at[c], send_sems, recv_sems, s0 + t, (x, y, 1 - c)) for t, ref in enumerate(refs)]


def _all_to_all_copies(refs, send_sems, recv_sems):
    x, y, c = _mesh_pos()
    blk = refs[0].at[4 * x + 2 * y + c]
    peers = [(x, y, 1 - c)] + [(*chip, cc) for chip in _other_chips(x, y) for cc in (c, 1 - c)]
    return [_remote(blk, blk, send_sems, recv_sems, k, peer) for k, peer in enumerate(peers)]


class _GradPipeline:
    def __init__(self, pos):
        self.pos = pos
        self.groups, self.pending, self.count = [], None, 0
        self.reduced = {n: [None] * DEPTH for n in BIG_NAMES}

    def busy(self):
        return bool(self.groups) or self.pending is not None

    def tick(self, deps, new):
        if self.pending is not None:
            started, copies, owners = self.pending
            bufs = _split_wait("grad_pipe_wait_%d" % self.count, started, copies, after=list(deps))
            for grp, lo, hi in owners:
                grp["bufs"] = bufs[lo:hi]
            self.pending = None
        parts = []
        for grp in list(self.groups):
            n, names = len(grp["names"]), grp["names"]
            if grp["stage"] == "swap":
                pair = [_pair_sum("grad_pair_sum_" + nm, f, r, self.pos)
                        for nm, f, r in zip(names, grp["bufs"][:n], grp["bufs"][n:])]
                grp["own32"] = [p32 for p32, _ in pair]
                landing = [lax.empty((3, *p16.shape[1:]), BF16) for _, p16 in pair]
                grp["stage"] = "exchange"
                parts.append((grp, [p16 for _, p16 in pair] + landing, _exchange_part, 3 * n))
            elif grp["stage"] == "exchange":
                halves = [_chip_sum("grad_chip_sum_" + nm, p32, r, self.pos)
                          for nm, p32, r in zip(names, grp["own32"], grp["bufs"][n:])]
                grp["stage"] = "gather"
                parts.append((grp, halves, _gather_part, n))
            else:
                for nm, b in zip(names, grp["bufs"]):
                    self.reduced[nm][grp["layer"]] = b.reshape(-1, b.shape[-1])
                self.groups.remove(grp)
        if new:
            grp = dict(names=[nm for nm, _, _ in new], layer=new[0][1], stage="swap")
            self.groups.append(grp)
            fulls = [g for _, _, g in new]
            landing = [lax.empty((N_CHIP, g.shape[1] // 2, g.shape[2]), F32) for g in fulls]
            parts.append((grp, fulls + landing, _swap_part, len(fulls)))
        if not parts:
            return ()
        bufs, layout, owners, sems = [], [], [], 0
        for grp, part_bufs, fn, n_sems in parts:
            layout.append((len(bufs), len(bufs) + len(part_bufs), fn, sems))
            owners.append((grp, len(bufs), len(bufs) + len(part_bufs)))
            bufs += part_bufs
            sems += n_sems

        def copies(refs, send_sems, recv_sems):
            out = []
            for lo, hi, fn, s0 in layout:
                out += fn(refs[lo:hi], send_sems, recv_sems, s0)
            return out

        started = _split_start("grad_pipe_start_%d" % self.count, bufs, copies, sems)
        self.pending = (started, copies, owners)
        self.count += 1
        return (started[3],)


def _gather_all(name, block, slot, after=()):
    buf = lax.dynamic_update_slice(jnp.zeros((8, *block.shape), block.dtype), block[None], (slot, 0, 0))

    def body(*refs):
        out_ref, send_sems, recv_sems = refs[1 + len(after):]
        x, y, c = _mesh_pos()
        chips = _other_chips(x, y)
        sibling = (x, y, 1 - c)
        slot_of = lambda px, py, pc: out_ref.at[4 * px + 2 * py + pc]
        started = [_remote(slot_of(x, y, c), slot_of(x, y, c), send_sems, recv_sems, 0, sibling)]
        started += [_remote(slot_of(x, y, c), slot_of(x, y, c), send_sems, recv_sems, 1 + j, (*chip, c))
                    for j, chip in enumerate(chips)]
        for cp in started:
            cp.start()
        for j, chip in enumerate(chips):
            _remote(slot_of(*chip, c), slot_of(*chip, c), send_sems, recv_sems, 1 + j, (*chip, c)).wait_recv()
            fw = _remote(slot_of(*chip, c), slot_of(*chip, c), send_sems, recv_sems, 4 + j, sibling)
            fw.start()
            started.append(fw)
        _remote(slot_of(x, y, 1 - c), slot_of(x, y, 1 - c), send_sems, recv_sems, 0, sibling).wait_recv()
        for j, chip in enumerate(chips):
            _remote(slot_of(*chip, 1 - c), slot_of(*chip, 1 - c), send_sems, recv_sems, 4 + j, sibling).wait_recv()
        for cp in started:
            cp.wait_send()

    return _gather_call(name, body, buf, after)


def _row_tile(rows, cols):
    cap = max(SUBLANES, ELEMWISE_BLOCK_BYTES // (4 * cols))
    tr = rows
    while tr > cap and tr % 2 == 0:
        tr //= 2
    return tr


def _pair_sum(name, grad, recv, pos):
    _, rh, cols = recv.shape
    tr = _row_tile(rh, cols)
    per = rh // tr

    def body(pos_ref, g_ref, r_ref, own_ref, out16_ref):
        s = g_ref[...] + r_ref[...]
        out16_ref[...] = s.astype(BF16)

        @pl.when(pl.program_id(1) == pos_ref[0])
        def _():
            own_ref[...] = s

    blk = pl.BlockSpec((None, tr, cols), lambda i, k, pos_ref: (k, i, 0))
    return pl.pallas_call(
        body, name=name,
        grid_spec=pltpu.PrefetchScalarGridSpec(
            num_scalar_prefetch=1, grid=(per, N_CHIP),
            in_specs=[pl.BlockSpec((None, tr, cols), lambda i, k, pos_ref: (k, pos_ref[1] * per + i, 0)), blk],
            out_specs=[pl.BlockSpec((tr, cols), lambda i, k, pos_ref: (i, 0)), blk]),
        out_shape=[jax.ShapeDtypeStruct((rh, cols), F32), jax.ShapeDtypeStruct(recv.shape, BF16)],
        compiler_params=_params(2))(pos, grad, recv)


def _chip_sum(name, own32, recv, pos):
    rh, cols = own32.shape
    tr = _row_tile(rh, cols)

    def body(pos_ref, own_ref, r_ref, out_ref):
        del pos_ref
        out_ref[...] = ((own_ref[...] + r_ref[0].astype(F32)) + r_ref[1].astype(F32)) + r_ref[2].astype(F32)

    return pl.pallas_call(
        body, name=name,
        grid_spec=pltpu.PrefetchScalarGridSpec(
            num_scalar_prefetch=1, grid=(rh // tr,),
            in_specs=[pl.BlockSpec((tr, cols), lambda i, pos_ref: (i, 0)),
                      pl.BlockSpec((3, tr, cols), lambda i, pos_ref: (0, i, 0))],
            out_specs=pl.BlockSpec((None, tr, cols), lambda i, pos_ref: (pos_ref[1], i, 0))),
        out_shape=jax.ShapeDtypeStruct((2, rh, cols), F32), compiler_params=_params(1))(pos, own32, recv)


def _cast_into_slot(name, w, layer, pos, after=()):
    _, rows, cols = w.shape
    tr = _row_tile(rows, cols)

    def body(pos_ref, w_ref, *rest):
        del pos_ref
        rest[-1][...] = w_ref[...].astype(BF16)

    return pl.pallas_call(
        body, name=name,
        grid_spec=pltpu.PrefetchScalarGridSpec(
            num_scalar_prefetch=1, grid=(rows // tr,),
            in_specs=[pl.BlockSpec((None, tr, cols), lambda i, pos_ref: (layer, i, 0))] + [ANY] * len(after),
            out_specs=pl.BlockSpec((None, tr, cols), lambda i, pos_ref: (pos_ref[0], i, 0))),
        out_shape=jax.ShapeDtypeStruct((N_CHIP, rows, cols), BF16), compiler_params=_params(1))(pos, w, *after)


def _adamw_math(w, g, m, v):
    m = ADAM_B1 * m + (1.0 - ADAM_B1) * g
    v = ADAM_B2 * v + (1.0 - ADAM_B2) * jnp.square(g)
    m_hat = m / (1.0 - ADAM_B1 ** ADAM_STEP)
    v_hat = v / (1.0 - ADAM_B2 ** ADAM_STEP)
    delta = -ADAM_LR * (m_hat / (jnp.sqrt(v_hat) + ADAM_EPS) + ADAM_WD * w)
    return delta, m, v


def _adamw_layers(name, w, m, v, grads, first, into=None, after=()):
    _, rows, cols = w.shape
    tr = _row_tile(rows, cols)
    n_layers = len(grads)

    def body(w_ref, m_ref, v_ref, *rest):
        g_refs, (grad_ref, d_ref, nm_ref, nv_ref) = rest[:n_layers], rest[len(rest) - 4:]
        layer = pl.program_id(0)
        g = g_refs[0][...]
        for l in range(1, n_layers):
            g = jnp.where(layer == l, g_refs[l][...], g)
        grad_ref[...] = g
        d_ref[...], nm_ref[...], nv_ref[...] = _adamw_math(w_ref[...], g, m_ref[...], v_ref[...])

    blk = pl.BlockSpec((None, tr, cols), lambda l, i: (first + l, i, 0))
    g_spec = lambda k: pl.BlockSpec((tr, cols), lambda l, i: (jnp.where(l == k, i, 0), 0))
    passed = list(into or []) + list(after)
    return pl.pallas_call(
        body, name=name, grid=(n_layers, rows // tr),
        in_specs=[blk, blk, blk] + [g_spec(k) for k in range(n_layers)] + [ANY] * len(passed), out_specs=[blk] * 4,
        out_shape=[jax.ShapeDtypeStruct(w.shape, F32)] * 4,
        input_output_aliases={3 + n_layers + t: t for t in range(4)} if into else {},
        compiler_params=_params(2))(w, m, v, *grads, *passed)


def _sum_devices(gathered):
    _, rows, cols = gathered.shape

    def body(g_ref, out_ref):
        s = g_ref[0]
        for d in range(1, 8):
            s = s + g_ref[d]
        out_ref[...] = s

    return pl.pallas_call(body, name="sum_devices", out_shape=jax.ShapeDtypeStruct((rows, cols), F32),
                          compiler_params=pltpu.CompilerParams(vmem_limit_bytes=VMEM_LIMIT_BYTES))(gathered)


def _adamw_flat(w, g, m, v):
    def body(w_ref, g_ref, m_ref, v_ref, d_ref, nm_ref, nv_ref):
        d_ref[...], nm_ref[...], nv_ref[...] = _adamw_math(w_ref[...], g_ref[...], m_ref[...], v_ref[...])

    return pl.pallas_call(body, name="adamw_small", out_shape=[jax.ShapeDtypeStruct(w.shape, F32)] * 3,
                          compiler_params=pltpu.CompilerParams(vmem_limit_bytes=VMEM_LIMIT_BYTES))(w, g, m, v)


SMALL_NAMES = ["g_mix", "lower_bounds", "g_hgrn_out", "w_conv", "sg_ln_g", "sg_ln_b", "w_sg", "b_sg", "g_ffn", "g_final"]
BIG_NAMES = ["w_in", "w_branch", "w_o", "w_ff1", "w_ff2"]
WEIGHT_ORDER = ["w_in", "g_mix", "lower_bounds", "g_hgrn_out", "w_conv", "sg_ln_g", "sg_ln_b", "w_sg", "b_sg", "w_branch",
                "w_o", "g_ffn", "w_ff1", "w_ff2", "g_final"]


def _padded_rows(n):
    return -(-n // SUBLANES) * SUBLANES


def _pack(arrays):
    parts = []
    for a in arrays:
        a = a.reshape(-1, LANES)
        parts.append(jnp.pad(a, ((0, _padded_rows(a.shape[0]) - a.shape[0]), (0, 0))))
    return jnp.concatenate(parts, axis=0)


def _unpack(flat, shapes):
    out, row = [], 0
    for s in shapes:
        n = int(np.prod(s)) // LANES
        out.append(flat[row:row + n].reshape(s))
        row += _padded_rows(n)
    return out


def _as_2d(name, a):
    return a.reshape(DEPTH, N_BRANCH * WIDTH, BRANCH_COLS) if name == "w_branch" else a


def kernel(x, w_in, g_mix, lower_bounds, g_hgrn_out, w_conv, sg_ln_g, sg_ln_b, w_sg, b_sg, w_branch, w_o, g_ffn, w_ff1, w_ff2, g_final, loss_target, m_w_in, m_g_mix, m_lower_bounds, m_g_hgrn_out, m_w_conv, m_sg_ln_g, m_sg_ln_b, m_w_sg, m_b_sg, m_w_branch, m_w_o, m_g_ffn, m_w_ff1, m_w_ff2, m_g_final, v_w_in, v_g_mix, v_lower_bounds, v_g_hgrn_out, v_w_conv, v_sg_ln_g, v_sg_ln_b, v_w_sg, v_b_sg, v_w_branch, v_w_o, v_g_ffn, v_w_ff1, v_w_ff2, v_g_final):
    weights = dict(w_in=w_in, g_mix=g_mix, lower_bounds=lower_bounds, g_hgrn_out=g_hgrn_out, w_conv=w_conv,
                   sg_ln_g=sg_ln_g, sg_ln_b=sg_ln_b, w_sg=w_sg, b_sg=b_sg, w_branch=w_branch, w_o=w_o, g_ffn=g_ffn,
                   w_ff1=w_ff1, w_ff2=w_ff2, g_final=g_final)
    mom1 = dict(w_in=m_w_in, g_mix=m_g_mix, lower_bounds=m_lower_bounds, g_hgrn_out=m_g_hgrn_out, w_conv=m_w_conv,
                sg_ln_g=m_sg_ln_g, sg_ln_b=m_sg_ln_b, w_sg=m_w_sg, b_sg=m_b_sg, w_branch=m_w_branch, w_o=m_w_o,
                g_ffn=m_g_ffn, w_ff1=m_w_ff1, w_ff2=m_w_ff2, g_final=m_g_final)
    mom2 = dict(w_in=v_w_in, g_mix=v_g_mix, lower_bounds=v_lower_bounds, g_hgrn_out=v_g_hgrn_out, w_conv=v_w_conv,
                sg_ln_g=v_sg_ln_g, sg_ln_b=v_sg_ln_b, w_sg=v_w_sg, b_sg=v_b_sg, w_branch=v_w_branch, w_o=v_w_o,
                g_ffn=v_g_ffn, w_ff1=v_w_ff1, w_ff2=v_w_ff2, g_final=v_g_final)
    xi, yi, ci = _mesh_pos()
    pos = jnp.stack([2 * xi + yi, ci]).astype(jnp.int32)
    device = 4 * xi + 2 * yi + ci
    conv_cols = w_conv.shape[2]

    conv_all = _gather_all("gather_w_conv", w_conv.reshape(DEPTH * CONV_K, conv_cols), device)
    conv_full = conv_all.reshape(N_CHIP, 2, DEPTH, CONV_K, conv_cols)[:, 0].transpose(1, 2, 0, 3).reshape(DEPTH, CONV_K, WIDTH)

    ici, d2d = {}, {}
    cast = lambda l, names, after: [_cast_into_slot("cast_" + n, _as_2d(n, weights[n]), l, pos, after=after) for n in names]
    token = (conv_all,)
    for l in range(DEPTH):
        for part, names in (("w_in", BIG_NAMES[:1]), ("rest", BIG_NAMES[1:])):
            ici[l, part] = _split_start("weights_ici_start_%d_%s" % (l, part), cast(l, names, token), _weight_ici_copies,
                                        3 * len(names), after=token)
            token = (ici[l, part][3],)
    lbs = _lbs_fwd(lower_bounds)

    def forward_to_sibling(l, part, deps):
        landed = _split_wait("weights_ici_wait_%d_%s" % (l, part), ici.pop((l, part)), _weight_ici_copies, after=deps)
        d2d[l, part] = _split_start("weights_d2d_start_%d_%s" % (l, part), landed, _weight_d2d_copies, 3 * len(landed))
        return (d2d[l, part][3],)

    def gathered(l, part, deps):
        return _split_wait("weights_d2d_wait_%d_%s" % (l, part), d2d.pop((l, part)), _weight_d2d_copies, after=deps)

    act = x[0]
    normed = _rms_fwd("rms_mix", act, g_mix[0:1], after=token)
    layers = []
    forward_to_sibling(0, "w_in", [normed, lbs])
    for l in range(DEPTH):
        small = dict(g_mix=g_mix[l:l + 1], lbs=lbs[l:l + 1], g_hgrn_out=g_hgrn_out[l:l + 1], w_conv=conv_full[l],
                     sg_ln_g=sg_ln_g[l:l + 1], sg_ln_b=sg_ln_b[l:l + 1], w_sg=w_sg[l],
                     b_sg=b_sg[l].reshape(SG_GROUPS, SG_CHUNK, 1), g_ffn=g_ffn[l:l + 1])
        hooks = dict(after_hgrn=lambda deps, l=l: forward_to_sibling(l, "rest", deps),
                     late_weights=lambda deps, l=l: gathered(l, "rest", deps),
                     before_last=(lambda deps, l=l: forward_to_sibling(l + 1, "w_in", deps)) if l + 1 < DEPTH
                     else (lambda deps: ()))
        act, normed, saved, wts = _layer_fwd(act, normed, gathered(l, "w_in", [act])[0], small,
                                             g_mix[l + 1:l + 2] if l + 1 < DEPTH else None, hooks)
        layers.append((wts, small, saved))
    loss_blk, dact, dg_final = _loss_head(act, g_final.reshape(1, D_MODEL), loss_target[0])

    pipe = _GradPipeline(pos)
    small_grads = [None] * DEPTH
    after = ()
    for l in reversed(range(DEPTH)):
        wts, small, saved = layers[l]
        tick = lambda deps, new, l=l: pipe.tick(deps, [(nm, l, g) for nm, g in new])
        dact, small_grads[l], after = _layer_bwd(dact, saved, wts, small, tick, after)
    grad_x = dact[None]

    stack = lambda key, rows=None: jnp.stack([small_grads[l][key][0] if rows is None else small_grads[l][key][:rows]
                                              for l in range(DEPTH)])
    local_small = dict(
        g_mix=stack("g_mix"), lower_bounds=stack("lbs"), g_hgrn_out=stack("g_hgrn_out"), w_conv=stack("w_conv", CONV_K),
        sg_ln_g=stack("sg_ln_g"), sg_ln_b=stack("sg_ln_b"), w_sg=jnp.stack([small_grads[l]["w_sg"] for l in range(DEPTH)]),
        b_sg=jnp.stack([small_grads[l]["b_sg"].reshape(SG_GROUPS, SG_CHUNK) for l in range(DEPTH)]),
        g_ffn=stack("g_ffn"), g_final=dg_final[0])
    shapes = [local_small[n].shape for n in SMALL_NAMES] + [(SUBLANES, LANES)]
    packed = _pack([local_small[n] for n in SMALL_NAMES] + [loss_blk])
    packed = lax.dynamic_update_slice(jnp.zeros((8, *packed.shape), F32), packed[None], (device, 0, 0))
    small_flight = _split_start("small_grads_start", [packed], _all_to_all_copies, 7, after=after)

    def adam(n, first, layer_grads, into=None, after=()):
        return _adamw_layers("adamw_%s_%d" % (n, first), _as_2d(n, weights[n]), _as_2d(n, mom1[n]), _as_2d(n, mom2[n]),
                             layer_grads, first, into, after)

    done = {"w_ff1": adam("w_ff1", 0, pipe.reduced["w_ff1"], after=(small_flight[3],))}
    token = pipe.tick([done["w_ff1"][1]], [])
    done["w_ff2"] = adam("w_ff2", 0, pipe.reduced["w_ff2"], after=token)
    summed = _sum_devices(_split_wait("small_grads_wait", small_flight, _all_to_all_copies, after=[done["w_ff2"][1]])[0])
    parts = _unpack(summed, shapes)
    loss = parts[-1][0, 0]
    small_grad = dict(zip(SMALL_NAMES, parts[:-1]))
    small_grad["lower_bounds"] = _lbs_bwd(lower_bounds, small_grad["lower_bounds"])
    small_grad["w_conv"] = lax.dynamic_slice_in_dim(small_grad["w_conv"], pos[0] * conv_cols, conv_cols, axis=2)
    g_flat = _pack([small_grad[n] for n in SMALL_NAMES])
    d_flat, m_flat, v_flat = _adamw_flat(_pack([weights[n] for n in SMALL_NAMES]), g_flat,
                                         _pack([mom1[n] for n in SMALL_NAMES]), _pack([mom2[n] for n in SMALL_NAMES]))
    small_shapes = [weights[n].shape for n in SMALL_NAMES]
    grads = dict(small_grad)
    delta = dict(zip(SMALL_NAMES, _unpack(d_flat, small_shapes)))
    new_m = dict(zip(SMALL_NAMES, _unpack(m_flat, small_shapes)))
    new_v = dict(zip(SMALL_NAMES, _unpack(v_flat, small_shapes)))

    for n in ("w_o", "w_branch"):
        done[n] = adam(n, 0, pipe.reduced[n], after=(d_flat,))
    token = pipe.tick([done["w_branch"][1]], [])
    rest = adam("w_in", 1, pipe.reduced["w_in"][1:], after=token)
    pipe.tick([rest[1]], [])
    assert not pipe.busy()
    done["w_in"] = adam("w_in", 0, pipe.reduced["w_in"][:1], into=rest)
    for n in BIG_NAMES:
        grads[n], delta[n], new_m[n], new_v[n] = [o.reshape(weights[n].shape) for o in done[n]]

    return (loss, grad_x, *[grads[n] for n in WEIGHT_ORDER], *[delta[n] for n in WEIGHT_ORDER],
            *[new_m[n] for n in WEIGHT_ORDER], *[new_v[n] for n in WEIGHT_ORDER])
```

```python
import numpy as np
import jax
import jax.numpy as jnp
from jax import lax
from jax.experimental import pallas as pl
from jax.experimental.pallas import tpu as pltpu

F32, BF16 = jnp.float32, jnp.bfloat16

D_MODEL = 1024
WIDTH = 512
N_BRANCH = 3
N_HEAD = 4
HEAD = 128
H_CHUNK = 64
CONV_K = 3
SG_CHUNK = 128
SG_GROUPS = 4
D_FF = 4096
DEPTH = 4
N_CHIP = 4
IN_COLS = 9 * WIDTH + N_BRANCH * D_MODEL
GATE_COL0 = 9 * WIDTH
LB_FLOOR = 1e-30
NORM_EPS = 1e-6
LN_EPS = 1e-5
ADAM_LR, ADAM_B1, ADAM_B2, ADAM_EPS, ADAM_WD, ADAM_STEP = 0.001, 0.9, 0.999, 1e-08, 0.01, 10

VMEM_LIMIT_BYTES = 56 * 1024 * 1024
VMEM_BLOCK_BUDGET = 44 * 1024 * 1024
SUBLANES, LANES = 8, 128
ELEMWISE_BLOCK_BYTES = 2 * 1024 * 1024

NN = (((1,), (0,)), ((), ()))
NT = (((1,), (1,)), ((), ()))
TN = (((0,), (0,)), ((), ()))
MESH = pl.DeviceIdType.MESH
ANY = pl.BlockSpec(memory_space=pl.ANY)


def _dot(a, b, dims=NN):
    return lax.dot_general(a.astype(BF16), b.astype(BF16), dims, preferred_element_type=F32)


def _params(n_axes):
    return pltpu.CompilerParams(dimension_semantics=("arbitrary",) * n_axes, vmem_limit_bytes=VMEM_LIMIT_BYTES)


def _row0(part, rows=SUBLANES):
    r = lax.broadcasted_iota(jnp.int32, (rows, part.shape[1]), 0)
    return jnp.where(r == 0, part, 0.0)


def _token_tile(T):
    return min(512, T)


def _matmul(name, a, b, *, dims, grid, a_spec, b_spec, out_specs, out_shapes, acc_shape,
            extra=(), extra_specs=(), epilogue=None, after=()):
    nk = grid[2]
    n_extra, n_out, n_in = len(extra), len(out_shapes), 2 + len(extra) + len(after)
    one_step = nk == 1

    def body(*refs):
        a_ref, b_ref = refs[0], refs[1]
        ex = refs[2:2 + n_extra]
        outs = refs[n_in:n_in + n_out]
        part = _dot(a_ref[...], b_ref[...], dims)

        def finish(total):
            res = epilogue(total, *[e[...] for e in ex]) if epilogue else (total,)
            for o, r in zip(outs, res):
                o[...] = r.astype(o.dtype)

        if one_step:
            finish(part)
            return
        acc = refs[-1]
        kk = pl.program_id(2)

        @pl.when(kk == 0)
        def _():
            acc[...] = part

        @pl.when(kk > 0)
        def _():
            acc[...] += part

        @pl.when(kk == nk - 1)
        def _():
            finish(acc[...])

    return pl.pallas_call(
        body, name=name, grid=grid,
        in_specs=[a_spec, b_spec, *extra_specs, *[ANY] * len(after)], out_specs=list(out_specs),
        out_shape=list(out_shapes), scratch_shapes=[] if one_step else [pltpu.VMEM(acc_shape, F32)],
        compiler_params=_params(3),
    )(a, b, *extra, *after)


def _mm_cols(name, a, w, out_dtypes, epilogue=None, extra=()):
    T, K = a.shape
    N = w.shape[2]
    tm = min(2 * _token_tile(T), T)
    blk = pl.BlockSpec((tm, N), lambda j, i, kk: (i, j))
    return _matmul(
        name, a, w, dims=NN, grid=(N_CHIP, T // tm, 1),
        a_spec=pl.BlockSpec((tm, K), lambda j, i, kk: (i, 0)),
        b_spec=pl.BlockSpec((None, K, N), lambda j, i, kk: (j, 0, 0)),
        out_specs=[blk] * len(out_dtypes),
        out_shapes=[jax.ShapeDtypeStruct((T, N_CHIP * N), dt) for dt in out_dtypes],
        acc_shape=(tm, N), extra=extra, extra_specs=[blk] * len(extra), epilogue=epilogue)


def _mm_rows(name, a, w, res, norm_gain=None, after=()):
    T = a.shape[0]
    K, N = N_CHIP * w.shape[1], w.shape[2]
    tm = _token_tile(T)
    blk = pl.BlockSpec((tm, N), lambda i, j, kk: (i, 0))

    def with_norm(acc, r, gain):
        xv = acc + r
        return xv, xv * lax.rsqrt(jnp.mean(xv * xv, axis=-1, keepdims=True) + NORM_EPS) * gain

    normed = norm_gain is not None
    outs = _matmul(
        name, a, w.reshape(K, N), dims=NN, grid=(T // tm, 1, 1),
        a_spec=pl.BlockSpec((tm, K), lambda i, j, kk: (i, 0)),
        b_spec=pl.BlockSpec((K, N), lambda i, j, kk: (0, 0)),
        out_specs=[blk] * (2 if normed else 1),
        out_shapes=[jax.ShapeDtypeStruct((T, N), F32)] + ([jax.ShapeDtypeStruct((T, N), BF16)] if normed else []),
        acc_shape=(tm, N), extra=(res, norm_gain) if normed else (res,),
        extra_specs=[blk] + ([pl.BlockSpec((1, N), lambda i, j, kk: (0, 0))] if normed else []),
        epilogue=with_norm if normed else (lambda acc, r: (acc + r,)), after=after)
    return outs if normed else outs[0]


def _mm_cols_t(name, g, w, out_dtype, epilogue=None, extra=(), after=()):
    T, N = g.shape
    K = N_CHIP * w.shape[1]
    tm = _token_tile(T) if K <= 2 * D_MODEL else _token_tile(T) // 2
    blk = pl.BlockSpec((tm, K), lambda i, j, kk: (i, 0))
    return _matmul(
        name, g, w.reshape(K, N), dims=NT, grid=(T // tm, 1, 1),
        a_spec=pl.BlockSpec((tm, N), lambda i, j, kk: (i, 0)),
        b_spec=pl.BlockSpec((K, N), lambda i, j, kk: (0, 0)),
        out_specs=[blk], out_shapes=[jax.ShapeDtypeStruct((T, K), out_dtype)], acc_shape=(tm, K),
        extra=extra, extra_specs=[blk] * len(extra), epilogue=epilogue, after=after)[0]


def _dgrad_norm_bwd(name, g, w, x, gain, dres):
    T = g.shape[0]
    K, N = w.shape[1], w.shape[2]
    tm = _token_tile(T)
    whole = w.size * w.dtype.itemsize <= VMEM_BLOCK_BUDGET // 2

    def norm_bwd(i, dhv, x_ref, gain_ref, dres_ref, dx_ref, dgain_ref):
        xv = x_ref[...]
        r = lax.rsqrt(jnp.mean(xv * xv, axis=-1, keepdims=True) + NORM_EPS)
        xn = xv * r
        dxn = dhv * gain_ref[...]
        dx_ref[...] = dres_ref[...] + r * (dxn - xn * jnp.mean(dxn * xn, axis=-1, keepdims=True))

        @pl.when(i == 0)
        def _():
            dgain_ref[...] = jnp.zeros_like(dgain_ref)

        dgain_ref[...] += _row0(jnp.sum(dhv * xn, axis=0, keepdims=True))

    def body_whole(g_ref, w_ref, *rest):
        dhv = _dot(g_ref[:, pl.ds(0, N)], w_ref[0], NT)
        for k in range(1, N_CHIP):
            dhv = dhv + _dot(g_ref[:, pl.ds(k * N, N)], w_ref[k], NT)
        norm_bwd(pl.program_id(0), dhv, *rest)

    def body_steps(g_ref, w_ref, x_ref, gain_ref, dres_ref, dx_ref, dgain_ref, acc):
        kk = pl.program_id(1)
        part = _dot(g_ref[...], w_ref[...], NT)

        @pl.when(kk == 0)
        def _():
            acc[...] = part

        @pl.when(kk > 0)
        def _():
            acc[...] += part

        @pl.when(kk == N_CHIP - 1)
        def _():
            norm_bwd(pl.program_id(0), acc[...], x_ref, gain_ref, dres_ref, dx_ref, dgain_ref)

    tile = pl.BlockSpec((tm, K), lambda i, *kk: (i, 0))
    row = lambda cols: pl.BlockSpec((SUBLANES if cols is None else 1, K), lambda i, *kk: (0, 0))
    if whole:
        g_spec = pl.BlockSpec((tm, N_CHIP * N), lambda i: (i, 0))
        w_spec = pl.BlockSpec(w.shape, lambda i: (0, 0, 0), pipeline_mode=pl.Buffered(1))
    else:
        g_spec = pl.BlockSpec((tm, N), lambda i, kk: (i, kk))
        w_spec = pl.BlockSpec((None, K, N), lambda i, kk: (kk, 0, 0))
    return pl.pallas_call(
        body_whole if whole else body_steps, name=name, grid=(T // tm,) if whole else (T // tm, N_CHIP),
        in_specs=[g_spec, w_spec, tile, row(1), tile], out_specs=[tile, row(None)],
        out_shape=[jax.ShapeDtypeStruct((T, K), F32), jax.ShapeDtypeStruct((SUBLANES, K), F32)],
        scratch_shapes=[] if whole else [pltpu.VMEM((tm, K), F32)],
        compiler_params=_params(1 if whole else 2))(g, w, x, gain, dres)


def _mm_wgrad(name, a, g, a_cols, g_cols, a_blocked, g_blocked, after=()):
    T = a.shape[0]
    tt = T
    while tt > LANES and 2 * 2 * tt * (a_cols + g_cols) + (2 if tt == T else 3) * 4 * a_cols * g_cols > VMEM_BLOCK_BUDGET:
        tt //= 2
    return _matmul(
        name, a, g, dims=TN, grid=(N_CHIP, 1, T // tt),
        a_spec=pl.BlockSpec((tt, a_cols), (lambda j, i, kk: (kk, j)) if a_blocked else (lambda j, i, kk: (kk, 0))),
        b_spec=pl.BlockSpec((tt, g_cols), (lambda j, i, kk: (kk, j)) if g_blocked else (lambda j, i, kk: (kk, 0))),
        out_specs=[pl.BlockSpec((None, a_cols, g_cols), lambda j, i, kk: (j, 0, 0))],
        out_shapes=[jax.ShapeDtypeStruct((N_CHIP, a_cols, g_cols), F32)], acc_shape=(a_cols, g_cols), after=after)[0]


def _rms_fwd(name, x, g, after=()):
    T, Dm = x.shape
    tm = min(256, T)

    def body(x_ref, g_ref, *rest):
        xv = x_ref[...]
        r = lax.rsqrt(jnp.mean(xv * xv, axis=-1, keepdims=True) + NORM_EPS)
        rest[-1][...] = (xv * r * g_ref[...]).astype(BF16)

    return pl.pallas_call(
        body, name=name, grid=(T // tm,),
        in_specs=[pl.BlockSpec((tm, Dm), lambda i: (i, 0)), pl.BlockSpec((1, Dm), lambda i: (0, 0))] + [ANY] * len(after),
        out_specs=pl.BlockSpec((tm, Dm), lambda i: (i, 0)),
        out_shape=jax.ShapeDtypeStruct((T, Dm), BF16), compiler_params=_params(1))(x, g, *after)


def _loss_head(x, g, tgt):
    T, Dm = x.shape
    tm = min(256, T)

    def body(x_ref, g_ref, t_ref, loss_ref, dx_ref, dg_ref):
        xv = x_ref[...]
        gv = g_ref[...]
        r = lax.rsqrt(jnp.mean(xv * xv, axis=-1, keepdims=True) + NORM_EPS)
        xn = xv * r
        err = xn * gv - t_ref[...]
        dy = err * (1.0 / Dm)
        dxn = dy * gv
        dx_ref[...] = r * (dxn - xn * jnp.mean(dxn * xn, axis=-1, keepdims=True))

        @pl.when(pl.program_id(0) == 0)
        def _():
            dg_ref[...] = jnp.zeros_like(dg_ref)
            loss_ref[...] = jnp.zeros_like(loss_ref)

        dg_ref[...] += _row0(jnp.sum(dy * xn, axis=0, keepdims=True))
        part = jnp.sum(jnp.sum(err * err, axis=-1, keepdims=True), axis=0, keepdims=True) * (0.5 / Dm)
        loss_ref[...] += jnp.broadcast_to(part, loss_ref.shape)

    tile = pl.BlockSpec((tm, Dm), lambda i: (i, 0))
    return pl.pallas_call(
        body, name="loss_head", grid=(T // tm,),
        in_specs=[tile, pl.BlockSpec((1, Dm), lambda i: (0, 0)), tile],
        out_specs=[pl.BlockSpec((SUBLANES, LANES), lambda i: (0, 0)), tile,
                   pl.BlockSpec((SUBLANES, Dm), lambda i: (0, 0))],
        out_shape=[jax.ShapeDtypeStruct((SUBLANES, LANES), F32), jax.ShapeDtypeStruct((T, Dm), F32),
                   jax.ShapeDtypeStruct((SUBLANES, Dm), F32)],
        compiler_params=_params(1))(x, g, tgt)


def _softmax_rows(lb_ref):
    rows = [lb_ref[pl.ds(i, 1), :] for i in range(DEPTH)]
    mx = rows[0]
    for r in rows[1:]:
        mx = jnp.maximum(mx, r)
    es = [jnp.exp(r - mx) for r in rows]
    tot = es[0]
    for e in es[1:]:
        tot = tot + e
    return [e / tot for e in es]


def _lbs_fwd(lower_bounds):
    def body(lb_ref, out_ref):
        sm = _softmax_rows(lb_ref)
        run = jnp.zeros_like(sm[0])
        out_ref[pl.ds(0, 1), :] = run
        for i in range(1, DEPTH):
            run = run + sm[i]
            out_ref[pl.ds(i, 1), :] = run

    return pl.pallas_call(body, name="lbs_fwd", out_shape=jax.ShapeDtypeStruct(lower_bounds.shape, F32))(lower_bounds)


def _lbs_bwd(lower_bounds, dlbs):
    def body(lb_ref, d_ref, out_ref):
        sm = _softmax_rows(lb_ref)
        dsm = [jnp.zeros_like(sm[0])]
        for i in range(1, DEPTH):
            acc = d_ref[pl.ds(i, 1), :]
            for l in range(i + 1, DEPTH):
                acc = acc + d_ref[pl.ds(l, 1), :]
            dsm.append(acc)
        inner = dsm[0] * sm[0]
        for i in range(1, DEPTH):
            inner = inner + dsm[i] * sm[i]
        for i in range(DEPTH):
            out_ref[pl.ds(i, 1), :] = sm[i] * (dsm[i] - inner)

    return pl.pallas_call(body, name="lbs_bwd", out_shape=jax.ShapeDtypeStruct(lower_bounds.shape, F32))(lower_bounds, dlbs)


N_LEVEL = 6


def _hgrn_consts():
    L = H_CHUNK
    t = np.arange(L)
    blocks = [(t[:, None] >= t[None, :]).astype(np.float32)]
    masks = []
    m = L // 2
    while m >= 1:
        blk, pos = t // (2 * m), t % (2 * m)
        start = blk * 2 * m
        mat = np.zeros((L, L), np.float32)
        for r in range(L):
            if pos[r] >= m:
                mat[r, start[r] + m:r + 1] = 1.0
            else:
                mat[r, r + 1:start[r] + m] = -1.0
        blocks.append(mat)
        masks.append(((blk[:, None] == blk[None, :]) & (pos[:, None] >= m) & (pos[None, :] < m)).astype(np.float32))
        m //= 2
    blocks.append(np.ones((L, L), np.float32))
    return jnp.asarray(np.concatenate(blocks, 0), BF16), jnp.asarray(np.stack(masks), F32)


def _hgrn_core(qraw, fp, lb, sum_mat, mask_ref):
    L = H_CHUNK
    sq = jax.nn.sigmoid(qraw)
    q = qraw * sq
    sneg = jax.nn.sigmoid(-fp)
    log_sig = jnp.minimum(fp, 0.0) - jnp.log1p(jnp.exp(-jnp.abs(fp)))
    a1 = jnp.log(jnp.maximum(lb, LB_FLOOR))
    a2 = jnp.log1p(-lb) + log_sig
    logf = jnp.maximum(a1, a2) + jnp.log1p(jnp.exp(-jnp.abs(a1 - a2)))
    w1 = jnp.exp(a1 - logf)
    w2 = jnp.exp(a2 - logf)
    k = (1.0 - lb) * sneg
    hi = logf.astype(BF16)
    r1 = logf - hi.astype(F32)
    mid = r1.astype(BF16)
    lo = (r1 - mid.astype(F32)).astype(BF16)
    sums = lax.dot_general(sum_mat, jnp.concatenate([hi, mid, lo], axis=1), NN, preferred_element_type=F32)
    sums = sums[:, 0:HEAD] + sums[:, HEAD:2 * HEAD] + sums[:, 2 * HEAD:3 * HEAD]
    b = sums[0:L]
    b_last = sums[(N_LEVEL + 1) * L:(N_LEVEL + 2) * L]
    eye = lax.broadcasted_iota(jnp.int32, (L, L), 0) == lax.broadcasted_iota(jnp.int32, (L, L), 1)
    attn = jnp.where(eye, jnp.sum(q * k, axis=1, keepdims=True), 0.0)
    fa, fb, ea, eb = [], [], [], []
    for l in range(N_LEVEL):
        d = sums[(l + 1) * L:(l + 2) * L]
        e_a = jnp.exp(jnp.minimum(d, 0.0))
        e_b = jnp.exp(jnp.minimum(-d, 0.0))
        a_l, b_l = q * e_a, k * e_b
        attn = attn + mask_ref[l] * _dot(a_l, b_l, NT)
        fa.append(a_l), fb.append(b_l), ea.append(e_a), eb.append(e_b)
    return dict(sq=sq, q=q, sneg=sneg, logf=logf, w1=w1, w2=w2, k=k, b=b, b_last=b_last, attn=attn,
                fa=fa, fb=fb, ea=ea, eb=eb)


def _hgrn_fwd(p, lbrow, gout):
    T = p.shape[0]
    nch = T // H_CHUNK
    sum_mat, masks = _hgrn_consts()

    def body(p_ref, lb_ref, g_ref, m_ref, mask_ref, o_ref, z_ref, st_ref, state):
        @pl.when(pl.program_id(0) == 0)
        def _():
            state[...] = jnp.zeros_like(state)

        sum_m = m_ref[...]
        for h in range(N_HEAD):
            col = lambda part: pl.ds(part * WIDTH + h * HEAD, HEAD)
            hs = pl.ds(h * HEAD, HEAD)
            v = p_ref[:, col(2)]
            c = _hgrn_core(p_ref[:, col(0)], p_ref[:, col(1)], lb_ref[:, hs], sum_m, mask_ref)
            s0 = state[h]
            st_ref[h] = s0
            o = _dot(c["attn"], v) + _dot(c["q"] * jnp.exp(c["b"]), s0, NT)
            k_dec = c["k"] * jnp.exp(c["b_last"] - c["b"])
            decay = jnp.exp(jnp.max(c["b_last"], axis=0, keepdims=True))
            state[h] = s0 * decay + _dot(v, k_dec, TN)
            o_ref[:, hs] = o
            r = lax.rsqrt(jnp.mean(o * o, axis=-1, keepdims=True) + NORM_EPS)
            z_ref[:, hs] = (o * r * g_ref[:, hs] * jax.nn.sigmoid(p_ref[:, col(3)])).astype(BF16)

    full = lambda shape: pl.BlockSpec(shape, lambda c: (0,) * len(shape))
    return pl.pallas_call(
        body, name="hgrn_fwd", grid=(nch,),
        in_specs=[pl.BlockSpec((H_CHUNK, 4 * WIDTH), lambda c: (c, 0)), full((1, WIDTH)), full((1, WIDTH)),
                  full(sum_mat.shape), full(masks.shape)],
        out_specs=[pl.BlockSpec((H_CHUNK, WIDTH), lambda c: (c, 0)),
                   pl.BlockSpec((None, H_CHUNK, WIDTH), lambda c: (0, c, 0)),
                   pl.BlockSpec((None, N_HEAD, HEAD, HEAD), lambda c: (c, 0, 0, 0))],
        out_shape=[jax.ShapeDtypeStruct((T, WIDTH), F32), jax.ShapeDtypeStruct((N_BRANCH, T, WIDTH), BF16),
                   jax.ShapeDtypeStruct((nch, N_HEAD, HEAD, HEAD), F32)],
        scratch_shapes=[pltpu.VMEM((N_HEAD, HEAD, HEAD), F32)], compiler_params=_params(1),
    )(p, lbrow, gout, sum_mat, masks)


def _hgrn_bwd(p, o_saved, dz, states, lbrow, gout, dp, after=()):
    T = p.shape[0]
    nch = T // H_CHUNK
    L = H_CHUNK
    sum_mat, masks = _hgrn_consts()

    def body(p_ref, o_ref, dz_ref, st_ref, lb_ref, g_ref, m_ref, mask_ref, dp_in, *rest):
        del dp_in
        dp_ref, dlb_ref, dg_ref, dstate = rest[len(after):]

        @pl.when(pl.program_id(0) == 0)
        def _():
            dstate[...] = jnp.zeros_like(dstate)
            dlb_ref[...] = jnp.zeros_like(dlb_ref)
            dg_ref[...] = jnp.zeros_like(dg_ref)

        sum_m = m_ref[...]
        for h in range(N_HEAD):
            col = lambda part: pl.ds(part * WIDTH + h * HEAD, HEAD)
            hs = pl.ds(h * HEAD, HEAD)
            qraw, fp, v, go = p_ref[:, col(0)], p_ref[:, col(1)], p_ref[:, col(2)], p_ref[:, col(3)]
            lb, g = lb_ref[:, hs], g_ref[:, hs]
            c = _hgrn_core(qraw, fp, lb, sum_m, mask_ref)
            q, k, b, b_last = c["q"], c["k"], c["b"], c["b_last"]
            s0, ds1 = st_ref[h], dstate[h]
            e_b = jnp.exp(b)
            q_dec = q * e_b
            e_bl = jnp.exp(b_last - b)
            k_dec = k * e_bl
            decay = jnp.exp(jnp.max(b_last, axis=0, keepdims=True))
            o = o_ref[:, hs]
            r = lax.rsqrt(jnp.mean(o * o, axis=-1, keepdims=True) + NORM_EPS)
            n = o * r
            sgo = jax.nn.sigmoid(go)
            dza = dz_ref[:, hs]
            dgo = dza * n * g * sgo * (1.0 - sgo)
            dg_ref[:, hs] += _row0(jnp.sum(dza * n * sgo, axis=0, keepdims=True))
            dn = dza * g * sgo
            do = r * (dn - n * jnp.mean(dn * n, axis=-1, keepdims=True))
            dattn = _dot(do, v, NT)
            dv = _dot(c["attn"], do, TN) + _dot(k_dec, ds1, NT)
            dq_dec = _dot(do, s0)
            dk_dec = _dot(v, ds1)
            ddiag = jnp.sum(do * v, axis=1, keepdims=True)
            dq = dq_dec * e_b + ddiag * k
            dk = dk_dec * e_bl + ddiag * q
            dsums = [dq_dec * q_dec - dk_dec * k_dec]
            for l in range(N_LEVEL):
                dm = mask_ref[l] * dattn
                da = _dot(dm, c["fb"][l])
                db = _dot(dm, c["fa"][l], TN)
                dq = dq + da * c["ea"][l]
                dk = dk + db * c["eb"][l]
                dsums.append(da * c["fa"][l] - db * c["fb"][l])
            dlast = jnp.sum(ds1 * s0, axis=0, keepdims=True) * decay
            dsums.append(dk_dec * k_dec + _row0(dlast, L))
            dlogf = _dot(sum_m, jnp.concatenate(dsums, axis=0), TN)
            dstate[h] = ds1 * decay + _dot(do, q_dec, TN)
            sq, sneg = c["sq"], c["sneg"]
            dqraw = dq * sq * (1.0 + qraw * (1.0 - sq))
            dfp = dlogf * c["w2"] * sneg - dk * (1.0 - lb) * sneg * (1.0 - sneg)
            inv_lb = jnp.where(lb > LB_FLOOR, 1.0 / jnp.maximum(lb, LB_FLOOR), 0.0)
            dlb_tok = dlogf * (c["w1"] * inv_lb - c["w2"] / (1.0 - lb)) - dk * sneg
            dlb_ref[:, hs] += _row0(jnp.sum(dlb_tok, axis=0, keepdims=True))
            dp_ref[:, col(0)] = dqraw.astype(BF16)
            dp_ref[:, col(1)] = dfp.astype(BF16)
            dp_ref[:, col(2)] = dv.astype(BF16)
            dp_ref[:, col(3)] = dgo.astype(BF16)

    full = lambda shape: pl.BlockSpec(shape, lambda c: (0,) * len(shape))
    rev = lambda c: nch - 1 - c
    return pl.pallas_call(
        body, name="hgrn_bwd", grid=(nch,),
        in_specs=[pl.BlockSpec((L, 4 * WIDTH), lambda c: (rev(c), 0)), pl.BlockSpec((L, WIDTH), lambda c: (rev(c), 0)),
                  pl.BlockSpec((None, L, WIDTH), lambda c: (0, rev(c), 0)),
                  pl.BlockSpec((None, N_HEAD, HEAD, HEAD), lambda c: (rev(c), 0, 0, 0)),
                  full((1, WIDTH)), full((1, WIDTH)), full(sum_mat.shape), full(masks.shape), ANY, *[ANY] * len(after)],
        out_specs=[pl.BlockSpec((L, 4 * WIDTH), lambda c: (rev(c), 0)), full((SUBLANES, WIDTH)), full((SUBLANES, WIDTH))],
        out_shape=[jax.ShapeDtypeStruct(dp.shape, dp.dtype), jax.ShapeDtypeStruct((SUBLANES, WIDTH), F32),
                   jax.ShapeDtypeStruct((SUBLANES, WIDTH), F32)],
        scratch_shapes=[pltpu.VMEM((N_HEAD, HEAD, HEAD), F32)], input_output_aliases={8: 0},
        compiler_params=_params(1),
    )(p, o_saved, dz, states, lbrow, gout, sum_mat, masks, dp, *after)


def _shift_down(tile, halo, s):
    tm = tile.shape[0]
    rows = lax.broadcasted_iota(jnp.int32, tile.shape, 0)
    head = jnp.concatenate([pltpu.roll(halo, s, 0), jnp.zeros((tm - SUBLANES, tile.shape[1]), tile.dtype)], axis=0)
    return jnp.where(rows < s, head, pltpu.roll(tile, s, 0))


def _shift_up(tile, halo, s):
    tm = tile.shape[0]
    rows = lax.broadcasted_iota(jnp.int32, tile.shape, 0)
    tail = jnp.concatenate([jnp.zeros((tm - SUBLANES, tile.shape[1]), tile.dtype), pltpu.roll(halo, SUBLANES - s, 0)], axis=0)
    return jnp.where(rows >= tm - s, tail, pltpu.roll(tile, tm - s, 0))


def _conv_fwd(p, w, z, after=()):
    T = p.shape[0]
    tm = _token_tile(T)
    per = tm // SUBLANES

    def body(bg_ref, cg_ref, xc_ref, hcg_ref, hxc_ref, w_ref, *rest):
        z_ref = rest[-1]
        zc = cg_ref[...] * xc_ref[...]
        hz = jnp.where(pl.program_id(0) > 0, hcg_ref[...] * hxc_ref[...], 0.0)
        y = (w_ref[pl.ds(0, 1), :] * _shift_down(zc, hz, 2) + w_ref[pl.ds(1, 1), :] * _shift_down(zc, hz, 1)
             + w_ref[pl.ds(2, 1), :] * zc)
        z_ref[...] = (bg_ref[...] * y).astype(BF16)

    tile = lambda cb: pl.BlockSpec((tm, WIDTH), lambda i: (i, cb))
    prev = lambda cb: pl.BlockSpec((SUBLANES, WIDTH), lambda i: (jnp.maximum(i * per - 1, 0), cb))
    return pl.pallas_call(
        body, name="conv_fwd", grid=(T // tm,),
        in_specs=[tile(4), tile(5), tile(6), prev(5), prev(6), pl.BlockSpec((CONV_K, WIDTH), lambda i: (0, 0)), ANY,
                  *[ANY] * len(after)],
        out_specs=pl.BlockSpec((None, tm, WIDTH), lambda i: (1, i, 0)),
        out_shape=jax.ShapeDtypeStruct(z.shape, z.dtype), input_output_aliases={6: 0}, compiler_params=_params(1),
    )(p, p, p, p, p, w, z, *after)


def _conv_bwd(p, w, dz, dp):
    T = p.shape[0]
    tm = _token_tile(T)
    per = tm // SUBLANES
    last = T // SUBLANES - 1

    def body(bg_ref, cg_ref, xc_ref, hcg_ref, hxc_ref, nbg_ref, dzb_ref, ndzb_ref, w_ref, dp_in, dp_ref, dw_ref, stash):
        del dp_in
        i, jj = pl.program_id(0), pl.program_id(1)

        @pl.when(jnp.logical_and(i == 0, jj == 0))
        def _():
            dw_ref[...] = jnp.zeros_like(dw_ref)

        @pl.when(jj == 0)
        def _():
            cg, xc, bg = cg_ref[...], xc_ref[...], bg_ref[...]
            w0, w1, w2 = w_ref[pl.ds(0, 1), :], w_ref[pl.ds(1, 1), :], w_ref[pl.ds(2, 1), :]
            zc = cg * xc
            hz = jnp.where(i > 0, hcg_ref[...] * hxc_ref[...], 0.0)
            z2, z1 = _shift_down(zc, hz, 2), _shift_down(zc, hz, 1)
            y = w0 * z2 + w1 * z1 + w2 * zc
            dzb = dzb_ref[...]
            dy = dzb * bg
            hdy = jnp.where(i < pl.num_programs(0) - 1, ndzb_ref[...] * nbg_ref[...], 0.0)
            dzc = w2 * dy + w1 * _shift_up(dy, hdy, 1) + w0 * _shift_up(dy, hdy, 2)
            rows = lax.broadcasted_iota(jnp.int32, (SUBLANES, WIDTH), 0)
            colsum = lambda t: jnp.sum(t, axis=0, keepdims=True)
            dw_ref[...] += (jnp.where(rows == 0, colsum(dy * z2), 0.0) + jnp.where(rows == 1, colsum(dy * z1), 0.0)
                            + jnp.where(rows == 2, colsum(dy * zc), 0.0))
            dp_ref[...] = (dzb * y).astype(BF16)
            stash[0] = dzc * xc
            stash[1] = dzc * cg

        @pl.when(jj > 0)
        def _():
            dp_ref[...] = stash[jj - 1].astype(BF16)

    n_tiles = T // tm
    tile = lambda cb: pl.BlockSpec((tm, WIDTH), lambda i, jj: (i, cb))
    prev = lambda cb: pl.BlockSpec((SUBLANES, WIDTH), lambda i, jj: (jnp.maximum(i * per - 1, 0), cb))
    nxt = lambda i: jnp.minimum((i + 1) * per, last)
    return pl.pallas_call(
        body, name="conv_bwd", grid=(n_tiles, 3),
        in_specs=[tile(4), tile(5), tile(6), prev(5), prev(6),
                  pl.BlockSpec((SUBLANES, WIDTH), lambda i, jj: (nxt(i), 4)),
                  pl.BlockSpec((None, tm, WIDTH), lambda i, jj: (1, i, 0)),
                  pl.BlockSpec((None, SUBLANES, WIDTH), lambda i, jj: (1, nxt(i), 0)),
                  pl.BlockSpec((CONV_K, WIDTH), lambda i, jj: (0, 0)), ANY],
        out_specs=[pl.BlockSpec((tm, WIDTH), lambda i, jj: (i, 4 + jj)),
                   pl.BlockSpec((SUBLANES, WIDTH), lambda i, jj: (0, 0))],
        out_shape=[jax.ShapeDtypeStruct(dp.shape, dp.dtype), jax.ShapeDtypeStruct((SUBLANES, WIDTH), F32)],
        scratch_shapes=[pltpu.VMEM((2, tm, WIDTH), F32)], input_output_aliases={9: 0}, compiler_params=_params(2),
    )(p, p, p, p, p, p, dz, dz, w, dp)


GELU_C = float(np.sqrt(2.0 / np.pi))
GELU_A = 0.044715


def _gelu(x):
    th = jnp.tanh(GELU_C * (x + GELU_A * x * x * x))
    return 0.5 * x * (1.0 + th), th


def _gelu_grad(x, th):
    return 0.5 * (1.0 + th) + 0.5 * x * (1.0 - th * th) * GELU_C * (1.0 + 3.0 * GELU_A * x * x)


def _sg_core(u, v, lng, lnb, ws_ref, bs_ref):
    gu, thu = _gelu(u)
    gv, thv = _gelu(v)
    xc = gv - jnp.mean(gv, axis=-1, keepdims=True)
    rs = lax.rsqrt(jnp.mean(xc * xc, axis=-1, keepdims=True) + LN_EPS)
    xh = xc * rs
    vp = xh * lng + lnb
    tril = (lax.broadcasted_iota(jnp.int32, (SG_CHUNK, SG_CHUNK), 0)
            >= lax.broadcasted_iota(jnp.int32, (SG_CHUNK, SG_CHUNK), 1))
    wm = [jnp.where(tril, ws_ref[g], 0.0).astype(BF16) for g in range(SG_GROUPS)]
    gs = lambda t, g: t[:, g * LANES:(g + 1) * LANES]
    sv = jnp.concatenate([_dot(wm[g], gs(vp, g)) + bs_ref[g] for g in range(SG_GROUPS)], axis=1)
    return dict(gu=gu, thu=thu, thv=thv, rs=rs, xh=xh, vp=vp, tril=tril, wm=wm, sv=sv)


def _sg_fwd(p, lng, lnb, ws, bs, z):
    T = p.shape[0]

    def body(u_ref, v_ref, lng_ref, lnb_ref, ws_ref, bs_ref, z_in, z_ref):
        del z_in
        c = _sg_core(u_ref[...], v_ref[...], lng_ref[...], lnb_ref[...], ws_ref, bs_ref)
        z_ref[...] = (c["gu"] * c["sv"]).astype(BF16)

    full = lambda shape: pl.BlockSpec(shape, lambda c: (0,) * len(shape))
    return pl.pallas_call(
        body, name="sg_fwd", grid=(T // SG_CHUNK,),
        in_specs=[pl.BlockSpec((SG_CHUNK, WIDTH), lambda c: (c, 7)), pl.BlockSpec((SG_CHUNK, WIDTH), lambda c: (c, 8)),
                  full((1, WIDTH)), full((1, WIDTH)), full(ws.shape), full(bs.shape), ANY],
        out_specs=pl.BlockSpec((None, SG_CHUNK, WIDTH), lambda c: (2, c, 0)),
        out_shape=jax.ShapeDtypeStruct(z.shape, z.dtype), input_output_aliases={6: 0}, compiler_params=_params(1),
    )(p, p, lng, lnb, ws, bs, z)


def _sg_bwd(p, lng, lnb, ws, bs, dz, dp):
    T = p.shape[0]

    def body(u_ref, v_ref, lng_ref, lnb_ref, ws_ref, bs_ref, dz_ref, dp_in, dp_ref, dws_ref, dbs_ref, dlng_ref, dlnb_ref,
             stash):
        del dp_in
        cidx, jj = pl.program_id(0), pl.program_id(1)

        @pl.when(jnp.logical_and(cidx == 0, jj == 0))
        def _():
            dws_ref[...] = jnp.zeros_like(dws_ref)
            dbs_ref[...] = jnp.zeros_like(dbs_ref)
            dlng_ref[...] = jnp.zeros_like(dlng_ref)
            dlnb_ref[...] = jnp.zeros_like(dlnb_ref)

        @pl.when(jj == 0)
        def _():
            u, v, lng = u_ref[...], v_ref[...], lng_ref[...]
            c = _sg_core(u, v, lng, lnb_ref[...], ws_ref, bs_ref)
            dzc = dz_ref[...]
            gs = lambda t, g: t[:, g * LANES:(g + 1) * LANES]
            dsv = dzc * c["gu"]
            dvp = []
            for g in range(SG_GROUPS):
                dsv_g = gs(dsv, g)
                dws_ref[g] += jnp.where(c["tril"], _dot(dsv_g, gs(c["vp"], g), NT), 0.0)
                dbs_ref[g] += jnp.sum(dsv_g, axis=1, keepdims=True)
                dvp.append(_dot(c["wm"][g], dsv_g, TN))
            dvp = jnp.concatenate(dvp, axis=1)
            xh = c["xh"]
            dlng_ref[...] += _row0(jnp.sum(dvp * xh, axis=0, keepdims=True))
            dlnb_ref[...] += _row0(jnp.sum(dvp, axis=0, keepdims=True))
            dxh = dvp * lng
            dgv = c["rs"] * (dxh - jnp.mean(dxh, axis=-1, keepdims=True) - xh * jnp.mean(dxh * xh, axis=-1, keepdims=True))
            dp_ref[...] = (dzc * c["sv"] * _gelu_grad(u, c["thu"])).astype(BF16)
            stash[...] = dgv * _gelu_grad(v, c["thv"])

        @pl.when(jj == 1)
        def _():
            dp_ref[...] = stash[...].astype(BF16)

    full = lambda shape: pl.BlockSpec(shape, lambda c, jj: (0,) * len(shape))
    return pl.pallas_call(
        body, name="sg_bwd", grid=(T // SG_CHUNK, 2),
        in_specs=[pl.BlockSpec((SG_CHUNK, WIDTH), lambda c, jj: (c, 7)), pl.BlockSpec((SG_CHUNK, WIDTH), lambda c, jj: (c, 8)),
                  full((1, WIDTH)), full((1, WIDTH)), full(ws.shape), full(bs.shape),
                  pl.BlockSpec((None, SG_CHUNK, WIDTH), lambda c, jj: (2, c, 0)), ANY],
        out_specs=[pl.BlockSpec((SG_CHUNK, WIDTH), lambda c, jj: (c, 7 + jj)), full(ws.shape), full(bs.shape),
                   full((SUBLANES, WIDTH)), full((SUBLANES, WIDTH))],
        out_shape=[jax.ShapeDtypeStruct(dp.shape, dp.dtype), jax.ShapeDtypeStruct(ws.shape, F32),
                   jax.ShapeDtypeStruct(bs.shape, F32), jax.ShapeDtypeStruct((SUBLANES, WIDTH), F32),
                   jax.ShapeDtypeStruct((SUBLANES, WIDTH), F32)],
        scratch_shapes=[pltpu.VMEM((SG_CHUNK, WIDTH), F32)], input_output_aliases={7: 0}, compiler_params=_params(2),
    )(p, p, lng, lnb, ws, bs, dz, dp)


BRANCH_COLS = D_MODEL // N_CHIP
GATE_UNIT0 = GATE_COL0 // WIDTH
UNITS = D_MODEL // WIDTH


def _unit_specs(order):
    def spec(which):
        def index(*g):
            _, n, u = order(*g)
            return (2 * u + which, n, 0, 0)
        return pl.BlockSpec((None, None, WIDTH, BRANCH_COLS), index)
    return [spec(0), spec(1)]


def _merge_fwd(z, p, wb):
    T = z.shape[1]
    tm = _token_tile(T)
    order = lambda i, u, n: (i, n, u)

    def body(z_ref, wa_ref, wb_ref, gt_ref, out_ref, acc):
        n = pl.program_id(2)
        zv = z_ref[...]
        y = jnp.concatenate([_dot(zv, wa_ref[...]), _dot(zv, wb_ref[...])], axis=1)
        part = jax.nn.sigmoid(gt_ref[...]) * y

        @pl.when(n == 0)
        def _():
            acc[...] = part

        @pl.when(n > 0)
        def _():
            acc[...] += part

        @pl.when(n == N_BRANCH - 1)
        def _():
            out_ref[...] = acc[...].astype(BF16)

    return pl.pallas_call(
        body, name="merge_fwd", grid=(T // tm, UNITS, N_BRANCH),
        in_specs=[pl.BlockSpec((None, tm, WIDTH), lambda i, u, n: (n, i, 0)), *_unit_specs(order),
                  pl.BlockSpec((tm, WIDTH), lambda i, u, n: (i, GATE_UNIT0 + UNITS * n + u))],
        out_specs=pl.BlockSpec((tm, WIDTH), lambda i, u, n: (i, u)),
        out_shape=jax.ShapeDtypeStruct((T, D_MODEL), BF16),
        scratch_shapes=[pltpu.VMEM((tm, WIDTH), F32)], compiler_params=_params(3))(z, wb, wb, p)


def _merge_bwd(z, p, wb, dmerged):
    T = z.shape[1]
    tm = _token_tile(T)
    order = lambda n, u, i: (i, n, u)

    def body(z_ref, wa_ref, wb_ref, gt_ref, dm_ref, dp_ref, dw_ref, dz_ref):
        u, i = pl.program_id(1), pl.program_id(2)
        zv, wa, wbv = z_ref[...], wa_ref[...], wb_ref[...]
        y = jnp.concatenate([_dot(zv, wa), _dot(zv, wbv)], axis=1)
        gate = jax.nn.sigmoid(gt_ref[...])
        dm = dm_ref[...]
        dp_ref[...] = (dm * y * gate * (1.0 - gate)).astype(BF16)
        dyv = (dm * gate).astype(BF16)
        dw = _dot(zv, dyv, TN)
        part = _dot(dyv[:, :BRANCH_COLS], wa, NT) + _dot(dyv[:, BRANCH_COLS:], wbv, NT)
        rows = pl.ds(pl.multiple_of(i * tm, tm), tm)

        @pl.when(i == 0)
        def _():
            dw_ref[0] = dw[:, :BRANCH_COLS]
            dw_ref[1] = dw[:, BRANCH_COLS:]

        @pl.when(i > 0)
        def _():
            dw_ref[0] += dw[:, :BRANCH_COLS]
            dw_ref[1] += dw[:, BRANCH_COLS:]

        @pl.when(u == 0)
        def _():
            dz_ref[rows, :] = part

        @pl.when(u > 0)
        def _():
            dz_ref[rows, :] += part

    unit = lambda n, u, i: (i, GATE_UNIT0 + UNITS * n + u)
    return pl.pallas_call(
        body, name="merge_bwd", grid=(N_BRANCH, UNITS, T // tm),
        in_specs=[pl.BlockSpec((None, tm, WIDTH), lambda n, u, i: (n, i, 0)), *_unit_specs(order),
                  pl.BlockSpec((tm, WIDTH), unit), pl.BlockSpec((tm, WIDTH), lambda n, u, i: (i, u))],
        out_specs=[pl.BlockSpec((tm, WIDTH), unit),
                   pl.BlockSpec((2, None, WIDTH, BRANCH_COLS), lambda n, u, i: (u, n, 0, 0)),
                   pl.BlockSpec((None, T, WIDTH), lambda n, u, i: (n, 0, 0))],
        out_shape=[jax.ShapeDtypeStruct((T, IN_COLS), BF16),
                   jax.ShapeDtypeStruct((N_CHIP, N_BRANCH, WIDTH, BRANCH_COLS), F32),
                   jax.ShapeDtypeStruct((N_BRANCH, T, WIDTH), F32)],
        compiler_params=_params(3))(z, wb, wb, p, dmerged)


def _layer_fwd(x, h, win, small, next_gain, hooks):
    p = _mm_cols("in_proj", h, win, [F32])[0]
    o_hgrn, z, states = _hgrn_fwd(p, small["lbs"], small["g_hgrn_out"])
    z = _conv_fwd(p, small["w_conv"], z, after=hooks["after_hgrn"]([o_hgrn]))
    z = _sg_fwd(p, small["sg_ln_g"], small["sg_ln_b"], small["w_sg"], small["b_sg"], z)
    wb, wo, w1, w2 = hooks["late_weights"]([z])
    wb = wb.reshape(N_CHIP, N_BRANCH, WIDTH, BRANCH_COLS)
    merged = _merge_fwd(z, p, wb)
    x_mid, h2 = _mm_rows("out_proj", merged, wo, x, small["g_ffn"])
    s = _mm_cols("ff1", h2, w1, [BF16], epilogue=lambda acc: (jnp.square(jnp.maximum(acc, 0.0)),))[0]
    if next_gain is None:
        x_out, h_next = _mm_rows("ff2_last", s, w2, x_mid, after=hooks["before_last"]([s])), None
    else:
        x_out, h_next = _mm_rows("ff2", s, w2, x_mid, next_gain, after=hooks["before_last"]([s]))
    saved = dict(x=x, h=h, p=p, o_hgrn=o_hgrn, z=z, states=states, merged=merged, x_mid=x_mid, h2=h2, s=s)
    return x_out, h_next, saved, [win, wb, wo, w1, w2]


def _layer_bwd(dx_out, sv, wts, small, tick, after):
    win, wb, wo, w1, w2 = wts
    g = {}
    da = _mm_cols_t("ff2_dgrad", dx_out, w2, BF16, extra=(sv["s"],), after=after,
                    epilogue=lambda acc, s: (acc * 2.0 * jnp.sqrt(s.astype(F32)),))
    d_ff2 = _mm_wgrad("ff2_wgrad", sv["s"], dx_out, w2.shape[1], D_MODEL, True, False)
    d_ff1 = _mm_wgrad("ff1_wgrad", sv["h2"], da, D_MODEL, w1.shape[2], False, True)
    dx_mid, g["g_ffn"] = _dgrad_norm_bwd("ff1_dgrad", da, w1, sv["x_mid"], small["g_ffn"], dx_out)
    after = tick([dx_mid], [("w_ff1", d_ff1), ("w_ff2", d_ff2)])
    dmerged = _mm_cols_t("out_proj_dgrad", dx_mid, wo, F32, after=after)
    d_o = _mm_wgrad("out_proj_wgrad", sv["merged"], dx_mid, wo.shape[1], D_MODEL, True, False)
    dp, d_branch, dz = _merge_bwd(sv["z"], sv["p"], wb, dmerged)
    d_branch = d_branch.reshape(N_CHIP, N_BRANCH * WIDTH, BRANCH_COLS)
    after = tick([dp], [("w_branch", d_branch), ("w_o", d_o)])
    dp, g["lbs"], g["g_hgrn_out"] = _hgrn_bwd(sv["p"], sv["o_hgrn"], dz, sv["states"], small["lbs"],
                                              small["g_hgrn_out"], dp, after=after)
    dp, g["w_conv"] = _conv_bwd(sv["p"], small["w_conv"], dz, dp)
    dp, g["w_sg"], g["b_sg"], g["sg_ln_g"], g["sg_ln_b"] = _sg_bwd(
        sv["p"], small["sg_ln_g"], small["sg_ln_b"], small["w_sg"], small["b_sg"], dz, dp)
    after = tick([dp], [])
    d_in = _mm_wgrad("in_proj_wgrad", sv["h"], dp, D_MODEL, win.shape[2], False, True, after=after)
    dx, g["g_mix"] = _dgrad_norm_bwd("in_proj_dgrad", dp, win, sv["x"], small["g_mix"], dx_mid)
    return dx, g, tick([dx], [("w_in", d_in)])


def _mesh_pos():
    return lax.axis_index("x"), lax.axis_index("y"), lax.axis_index("c")


def _other_chips(x, y):
    return [(1 - x, y), (x, 1 - y), (1 - x, 1 - y)]


def _remote(src, dst, send_sems, recv_sems, k, to):
    return pltpu.make_async_remote_copy(src_ref=src, dst_ref=dst, send_sem=send_sems.at[k], recv_sem=recv_sems.at[k],
                                        device_id=to, device_id_type=MESH)


def _gather_call(name, body, buf, after):
    scratch = [pltpu.SemaphoreType.DMA((7,)), pltpu.SemaphoreType.DMA((7,))]
    return pl.pallas_call(
        body, name=name, in_specs=[ANY] * (1 + len(after)), out_specs=ANY,
        out_shape=jax.ShapeDtypeStruct(buf.shape, buf.dtype), scratch_shapes=scratch, input_output_aliases={0: 0})(buf, *after)


HBM = pl.BlockSpec(memory_space=pltpu.HBM)
SEM = pl.BlockSpec(memory_space=pltpu.SEMAPHORE)
DATAFLOW = pltpu.SideEffectType.DATAFLOW_SIDE_EFFECTING


def _split_start(name, bufs, copies, n_copies, after=()):
    n = len(bufs)

    def body(*refs):
        send_sems, recv_sems = refs[n + len(after)], refs[n + len(after) + 1]
        for cp in copies(refs[:n], send_sems, recv_sems):
            cp.start()
        refs[-1][...] = jnp.zeros_like(refs[-1])

    outs = pl.pallas_call(
        body, name=name,
        out_shape=(pltpu.SemaphoreType.DMA((n_copies,)), pltpu.SemaphoreType.DMA((n_copies,)),
                   *[pltpu.HBM(b.shape, b.dtype) for b in bufs], jax.ShapeDtypeStruct((SUBLANES, LANES), F32)),
        in_specs=[HBM] * n + [ANY] * len(after),
        out_specs=(SEM, SEM, *[HBM] * n, pl.BlockSpec(memory_space=pltpu.VMEM)),
        input_output_aliases={t: 2 + t for t in range(n)},
        compiler_params=pltpu.CompilerParams(has_side_effects=DATAFLOW),
    )(*[pltpu.with_memory_space_constraint(b, pltpu.HBM) for b in bufs], *after)
    return outs[0], outs[1], list(outs[2:2 + n]), outs[-1]


def _split_wait(name, started, copies, after):
    send_sems, recv_sems, bufs, _ = started
    n = len(bufs)

    def body(*refs):
        for cp in copies(refs[:n], refs[n], refs[n + 1]):
            cp.wait_send()
            cp.wait_recv()

    return list(pl.pallas_call(
        body, name=name, out_shape=tuple(pltpu.HBM(b.shape, b.dtype) for b in bufs),
        in_specs=[HBM] * n + [SEM, SEM] + [ANY] * len(after), out_specs=tuple([HBM] * n),
        input_output_aliases={t: t for t in range(n)},
        compiler_params=pltpu.CompilerParams(has_side_effects=DATAFLOW),
    )(*bufs, send_sems, recv_sems, *after))


def _weight_ici_copies(refs, send_sems, recv_sems):
    x, y, c = _mesh_pos()
    out = []
    for t, ref in enumerate(refs):
        rh = ref.shape[1] // 2
        mine = ref.at[2 * x + y, pl.ds(c * rh, rh), :]
        out += [_remote(mine, mine, send_sems, recv_sems, 3 * t + j, (*chip, c)) for j, chip in enumerate(_other_chips(x, y))]
    return out


def _weight_d2d_copies(refs, send_sems, recv_sems):
    x, y, c = _mesh_pos()
    out = []
    for t, ref in enumerate(refs):
        rh = ref.shape[1] // 2
        for j, chip in enumerate(_other_chips(x, y)):
            blk = ref.at[2 * chip[0] + chip[1], pl.ds(c * rh, rh), :]
            out.append(_remote(blk, blk, send_sems, recv_sems, 3 * t + j, (x, y, 1 - c)))
    return out


def _swap_part(refs, send_sems, recv_sems, s0):
    x, y, c = _mesh_pos()
    n = len(refs) // 2
    out = []
    for t in range(n):
        rh = refs[t].shape[1] // 2
        out.append(_remote(refs[t].at[:, pl.ds((1 - c) * rh, rh), :], refs[n + t], send_sems, recv_sems, s0 + t, (x, y, 1 - c)))
    return out


def _exchange_part(refs, send_sems, recv_sems, s0):
    x, y, c = _mesh_pos()
    n = len(refs) // 2
    out = []
    for t in range(n):
        for j, chip in enumerate(_other_chips(x, y)):
            out.append(_remote(refs[t].at[2 * chip[0] + chip[1]], refs[n + t].at[j], send_sems, recv_sems, s0 + 3 * t + j,
                               (*chip, c)))
    return out


def _gather_part(refs, send_sems, recv_sems, s0):
    x, y, c = _mesh_pos()
    return [_remote(ref.at[c], ref.at[c], send_sems, recv_sems, s0 + t, (x, y, 1 - c)) for t, ref in enumerate(refs)]


def _all_to_all_copies(refs, send_sems, recv_sems):
    x, y, c = _mesh_pos()
    blk = refs[0].at[4 * x + 2 * y + c]
    peers = [(x, y, 1 - c)] + [(*chip, cc) for chip in _other_chips(x, y) for cc in (c, 1 - c)]
    return [_remote(blk, blk, send_sems, recv_sems, k, peer) for k, peer in enumerate(peers)]


class _GradPipeline:
    def __init__(self, pos):
        self.pos = pos
        self.groups, self.pending, self.count = [], None, 0
        self.reduced = {n: [None] * DEPTH for n in BIG_NAMES}

    def busy(self):
        return bool(self.groups) or self.pending is not None

    def tick(self, deps, new):
        if self.pending is not None:
            started, copies, owners = self.pending
            bufs = _split_wait("grad_pipe_wait_%d" % self.count, started, copies, after=list(deps))
            for grp, lo, hi in owners:
                grp["bufs"] = bufs[lo:hi]
            self.pending = None
        parts = []
        for grp in list(self.groups):
            n, names = len(grp["names"]), grp["names"]
            if grp["stage"] == "swap":
                pair = [_pair_sum("grad_pair_sum_" + nm, f, r, self.pos)
                        for nm, f, r in zip(names, grp["bufs"][:n], grp["bufs"][n:])]
                grp["own32"] = [p32 for p32, _ in pair]
                landing = [lax.empty((3, *p16.shape[1:]), BF16) for _, p16 in pair]
                grp["stage"] = "exchange"
                parts.append((grp, [p16 for _, p16 in pair] + landing, _exchange_part, 3 * n))
            elif grp["stage"] == "exchange":
                halves = [_chip_sum("grad_chip_sum_" + nm, p32, r, self.pos)
                          for nm, p32, r in zip(names, grp["own32"], grp["bufs"][n:])]
                grp["stage"] = "gather"
                parts.append((grp, halves, _gather_part, n))
            else:
                for nm, b in zip(names, grp["bufs"]):
                    self.reduced[nm][grp["layer"]] = b.reshape(-1, b.shape[-1])
                self.groups.remove(grp)
        if new:
            grp = dict(names=[nm for nm, _, _ in new], layer=new[0][1], stage="swap")
            self.groups.append(grp)
            fulls = [g for _, _, g in new]
            landing = [lax.empty((N_CHIP, g.shape[1] // 2, g.shape[2]), F32) for g in fulls]
            parts.append((grp, fulls + landing, _swap_part, len(fulls)))
        if not parts:
            return ()
        bufs, layout, owners, sems = [], [], [], 0
        for grp, part_bufs, fn, n_sems in parts:
            layout.append((len(bufs), len(bufs) + len(part_bufs), fn, sems))
            owners.append((grp, len(bufs), len(bufs) + len(part_bufs)))
            bufs += part_bufs
            sems += n_sems

        def copies(refs, send_sems, recv_sems):
            out = []
            for lo, hi, fn, s0 in layout:
                out += fn(refs[lo:hi], send_sems, recv_sems, s0)
            return out

        started = _split_start("grad_pipe_start_%d" % self.count, bufs, copies, sems)
        self.pending = (started, copies, owners)
        self.count += 1
        return (started[3],)


def _gather_all(name, block, slot, after=()):
    buf = lax.dynamic_update_slice(jnp.zeros((8, *block.shape), block.dtype), block[None], (slot, 0, 0))

    def body(*refs):
        out_ref, send_sems, recv_sems = refs[1 + len(after):]
        x, y, c = _mesh_pos()
        chips = _other_chips(x, y)
        sibling = (x, y, 1 - c)
        slot_of = lambda px, py, pc: out_ref.at[4 * px + 2 * py + pc]
        started = [_remote(slot_of(x, y, c), slot_of(x, y, c), send_sems, recv_sems, 0, sibling)]
        started += [_remote(slot_of(x, y, c), slot_of(x, y, c), send_sems, recv_sems, 1 + j, (*chip, c))
                    for j, chip in enumerate(chips)]
        for cp in started:
            cp.start()
        for j, chip in enumerate(chips):
            _remote(slot_of(*chip, c), slot_of(*chip, c), send_sems, recv_sems, 1 + j, (*chip, c)).wait_recv()
            fw = _remote(slot_of(*chip, c), slot_of(*chip, c), send_sems, recv_sems, 4 + j, sibling)
            fw.start()
            started.append(fw)
        _remote(slot_of(x, y, 1 - c), slot_of(x, y, 1 - c), send_sems, recv_sems, 0, sibling).wait_recv()
        for j, chip in enumerate(chips):
            _remote(slot_of(*chip, 1 - c), slot_of(*chip, 1 - c), send_sems, recv_sems, 4 + j, sibling).wait_recv()
        for cp in started:
            cp.wait_send()

    return _gather_call(name, body, buf, after)


def _row_tile(rows, cols):
    cap = max(SUBLANES, ELEMWISE_BLOCK_BYTES // (4 * cols))
    tr = rows
    while tr > cap and tr % 2 == 0:
        tr //= 2
    return tr


def _pair_sum(name, grad, recv, pos):
    _, rh, cols = recv.shape
    tr = _row_tile(rh, cols)
    per = rh // tr

    def body(pos_ref, g_ref, r_ref, own_ref, out16_ref):
        s = g_ref[...] + r_ref[...]
        out16_ref[...] = s.astype(BF16)

        @pl.when(pl.program_id(1) == pos_ref[0])
        def _():
            own_ref[...] = s

    blk = pl.BlockSpec((None, tr, cols), lambda i, k, pos_ref: (k, i, 0))
    return pl.pallas_call(
        body, name=name,
        grid_spec=pltpu.PrefetchScalarGridSpec(
            num_scalar_prefetch=1, grid=(per, N_CHIP),
            in_specs=[pl.BlockSpec((None, tr, cols), lambda i, k, pos_ref: (k, pos_ref[1] * per + i, 0)), blk],
            out_specs=[pl.BlockSpec((tr, cols), lambda i, k, pos_ref: (i, 0)), blk]),
        out_shape=[jax.ShapeDtypeStruct((rh, cols), F32), jax.ShapeDtypeStruct(recv.shape, BF16)],
        compiler_params=_params(2))(pos, grad, recv)


def _chip_sum(name, own32, recv, pos):
    rh, cols = own32.shape
    tr = _row_tile(rh, cols)

    def body(pos_ref, own_ref, r_ref, out_ref):
        del pos_ref
        out_ref[...] = ((own_ref[...] + r_ref[0].astype(F32)) + r_ref[1].astype(F32)) + r_ref[2].astype(F32)

    return pl.pallas_call(
        body, name=name,
        grid_spec=pltpu.PrefetchScalarGridSpec(
            num_scalar_prefetch=1, grid=(rh // tr,),
            in_specs=[pl.BlockSpec((tr, cols), lambda i, pos_ref: (i, 0)),
                      pl.BlockSpec((3, tr, cols), lambda i, pos_ref: (0, i, 0))],
            out_specs=pl.BlockSpec((None, tr, cols), lambda i, pos_ref: (pos_ref[1], i, 0))),
        out_shape=jax.ShapeDtypeStruct((2, rh, cols), F32), compiler_params=_params(1))(pos, own32, recv)


def _cast_into_slot(name, w, layer, pos, after=()):
    _, rows, cols = w.shape
    tr = _row_tile(rows, cols)

    def body(pos_ref, w_ref, *rest):
        del pos_ref
        rest[-1][...] = w_ref[...].astype(BF16)

    return pl.pallas_call(
        body, name=name,
        grid_spec=pltpu.PrefetchScalarGridSpec(
            num_scalar_prefetch=1, grid=(rows // tr,),
            in_specs=[pl.BlockSpec((None, tr, cols), lambda i, pos_ref: (layer, i, 0))] + [ANY] * len(after),
            out_specs=pl.BlockSpec((None, tr, cols), lambda i, pos_ref: (pos_ref[0], i, 0))),
        out_shape=jax.ShapeDtypeStruct((N_CHIP, rows, cols), BF16), compiler_params=_params(1))(pos, w, *after)


def _adamw_math(w, g, m, v):
    m = ADAM_B1 * m + (1.0 - ADAM_B1) * g
    v = ADAM_B2 * v + (1.0 - ADAM_B2) * jnp.square(g)
    m_hat = m / (1.0 - ADAM_B1 ** ADAM_STEP)
    v_hat = v / (1.0 - ADAM_B2 ** ADAM_STEP)
    delta = -ADAM_LR * (m_hat / (jnp.sqrt(v_hat) + ADAM_EPS) + ADAM_WD * w)
    return delta, m, v


def _adamw_layers(name, w, m, v, grads, first, into=None, after=()):
    _, rows, cols = w.shape
    tr = _row_tile(rows, cols)
    n_layers = len(grads)

    def body(w_ref, m_ref, v_ref, *rest):
        g_refs, (grad_ref, d_ref, nm_ref, nv_ref) = rest[:n_layers], rest[len(rest) - 4:]
        layer = pl.program_id(0)
        g = g_refs[0][...]
        for l in range(1, n_layers):
            g = jnp.where(layer == l, g_refs[l][...], g)
        grad_ref[...] = g
        d_ref[...], nm_ref[...], nv_ref[...] = _adamw_math(w_ref[...], g, m_ref[...], v_ref[...])

    blk = pl.BlockSpec((None, tr, cols), lambda l, i: (first + l, i, 0))
    g_spec = lambda k: pl.BlockSpec((tr, cols), lambda l, i: (jnp.where(l == k, i, 0), 0))
    passed = list(into or []) + list(after)
    return pl.pallas_call(
        body, name=name, grid=(n_layers, rows // tr),
        in_specs=[blk, blk, blk] + [g_spec(k) for k in range(n_layers)] + [ANY] * len(passed), out_specs=[blk] * 4,
        out_shape=[jax.ShapeDtypeStruct(w.shape, F32)] * 4,
        input_output_aliases={3 + n_layers + t: t for t in range(4)} if into else {},
        compiler_params=_params(2))(w, m, v, *grads, *passed)


def _sum_devices(gathered):
    _, rows, cols = gathered.shape

    def body(g_ref, out_ref):
        s = g_ref[0]
        for d in range(1, 8):
            s = s + g_ref[d]
        out_ref[...] = s

    return pl.pallas_call(body, name="sum_devices", out_shape=jax.ShapeDtypeStruct((rows, cols), F32),
                          compiler_params=pltpu.CompilerParams(vmem_limit_bytes=VMEM_LIMIT_BYTES))(gathered)


def _adamw_flat(w, g, m, v):
    def body(w_ref, g_ref, m_ref, v_ref, d_ref, nm_ref, nv_ref):
        d_ref[...], nm_ref[...], nv_ref[...] = _adamw_math(w_ref[...], g_ref[...], m_ref[...], v_ref[...])

    return pl.pallas_call(body, name="adamw_small", out_shape=[jax.ShapeDtypeStruct(w.shape, F32)] * 3,
                          compiler_params=pltpu.CompilerParams(vmem_limit_bytes=VMEM_LIMIT_BYTES))(w, g, m, v)


SMALL_NAMES = ["g_mix", "lower_bounds", "g_hgrn_out", "w_conv", "sg_ln_g", "sg_ln_b", "w_sg", "b_sg", "g_ffn", "g_final"]
BIG_NAMES = ["w_in", "w_branch", "w_o", "w_ff1", "w_ff2"]
WEIGHT_ORDER = ["w_in", "g_mix", "lower_bounds", "g_hgrn_out", "w_conv", "sg_ln_g", "sg_ln_b", "w_sg", "b_sg", "w_branch",
                "w_o", "g_ffn", "w_ff1", "w_ff2", "g_final"]


def _padded_rows(n):
    return -(-n // SUBLANES) * SUBLANES


def _pack(arrays):
    parts = []
    for a in arrays:
        a = a.reshape(-1, LANES)
        parts.append(jnp.pad(a, ((0, _padded_rows(a.shape[0]) - a.shape[0]), (0, 0))))
    return jnp.concatenate(parts, axis=0)


def _unpack(flat, shapes):
    out, row = [], 0
    for s in shapes:
        n = int(np.prod(s)) // LANES
        out.append(flat[row:row + n].reshape(s))
        row += _padded_rows(n)
    return out


def _as_2d(name, a):
    return a.reshape(DEPTH, N_BRANCH * WIDTH, BRANCH_COLS) if name == "w_branch" else a


def kernel(x, w_in, g_mix, lower_bounds, g_hgrn_out, w_conv, sg_ln_g, sg_ln_b, w_sg, b_sg, w_branch, w_o, g_ffn, w_ff1, w_ff2, g_final, loss_target, m_w_in, m_g_mix, m_lower_bounds, m_g_hgrn_out, m_w_conv, m_sg_ln_g, m_sg_ln_b, m_w_sg, m_b_sg, m_w_branch, m_w_o, m_g_ffn, m_w_ff1, m_w_ff2, m_g_final, v_w_in, v_g_mix, v_lower_bounds, v_g_hgrn_out, v_w_conv, v_sg_ln_g, v_sg_ln_b, v_w_sg, v_b_sg, v_w_branch, v_w_o, v_g_ffn, v_w_ff1, v_w_ff2, v_g_final):
    weights = dict(w_in=w_in, g_mix=g_mix, lower_bounds=lower_bounds, g_hgrn_out=g_hgrn_out, w_conv=w_conv,
                   sg_ln_g=sg_ln_g, sg_ln_b=sg_ln_b, w_sg=w_sg, b_sg=b_sg, w_branch=w_branch, w_o=w_o, g_ffn=g_ffn,
                   w_ff1=w_ff1, w_ff2=w_ff2, g_final=g_final)
    mom1 = dict(w_in=m_w_in, g_mix=m_g_mix, lower_bounds=m_lower_bounds, g_hgrn_out=m_g_hgrn_out, w_conv=m_w_conv,
                sg_ln_g=m_sg_ln_g, sg_ln_b=m_sg_ln_b, w_sg=m_w_sg, b_sg=m_b_sg, w_branch=m_w_branch, w_o=m_w_o,
                g_ffn=m_g_ffn, w_ff1=m_w_ff1, w_ff2=m_w_ff2, g_final=m_g_final)
    mom2 = dict(w_in=v_w_in, g_mix=v_g_mix, lower_bounds=v_lower_bounds, g_hgrn_out=v_g_hgrn_out, w_conv=v_w_conv,
                sg_ln_g=v_sg_ln_g, sg_ln_b=v_sg_ln_b, w_sg=v_w_sg, b_sg=v_b_sg, w_branch=v_w_branch, w_o=v_w_o,
                g_ffn=v_g_ffn, w_ff1=v_w_ff1, w_ff2=v_w_ff2, g_final=v_g_final)
    xi, yi, ci = _mesh_pos()
    pos = jnp.stack([2 * xi + yi, ci]).astype(jnp.int32)
    device = 4 * xi + 2 * yi + ci
    conv_cols = w_conv.shape[2]

    conv_all = _gather_all("gather_w_conv", w_conv.reshape(DEPTH * CONV_K, conv_cols), device)
    conv_full = conv_all.reshape(N_CHIP, 2, DEPTH, CONV_K, conv_cols)[:, 0].transpose(1, 2, 0, 3).reshape(DEPTH, CONV_K, WIDTH)

    ici, d2d = {}, {}
    cast = lambda l, names, after: [_cast_into_slot("cast_" + n, _as_2d(n, weights[n]), l, pos, after=after) for n in names]
    token = (conv_all,)
    for l in range(DEPTH):
        for part, names in (("w_in", BIG_NAMES[:1]), ("rest", BIG_NAMES[1:])):
            ici[l, part] = _split_start("weights_ici_start_%d_%s" % (l, part), cast(l, names, token), _weight_ici_copies,
                                        3 * len(names), after=token)
            token = (ici[l, part][3],)
    lbs = _lbs_fwd(lower_bounds)

    def forward_to_sibling(l, part, deps):
        landed = _split_wait("weights_ici_wait_%d_%s" % (l, part), ici.pop((l, part)), _weight_ici_copies, after=deps)
        d2d[l, part] = _split_start("weights_d2d_start_%d_%s" % (l, part), landed, _weight_d2d_copies, 3 * len(landed))
        return (d2d[l, part][3],)

    def gathered(l, part, deps):
        return _split_wait("weights_d2d_wait_%d_%s" % (l, part), d2d.pop((l, part)), _weight_d2d_copies, after=deps)

    act = x[0]
    normed = _rms_fwd("rms_mix", act, g_mix[0:1], after=token)
    layers = []
    forward_to_sibling(0, "w_in", [normed, lbs])
    for l in range(DEPTH):
        small = dict(g_mix=g_mix[l:l + 1], lbs=lbs[l:l + 1], g_hgrn_out=g_hgrn_out[l:l + 1], w_conv=conv_full[l],
                     sg_ln_g=sg_ln_g[l:l + 1], sg_ln_b=sg_ln_b[l:l + 1], w_sg=w_sg[l],
                     b_sg=b_sg[l].reshape(SG_GROUPS, SG_CHUNK, 1), g_ffn=g_ffn[l:l + 1])
        hooks = dict(after_hgrn=lambda deps, l=l: forward_to_sibling(l, "rest", deps),
                     late_weights=lambda deps, l=l: gathered(l, "rest", deps),
                     before_last=(lambda deps, l=l: forward_to_sibling(l + 1, "w_in", deps)) if l + 1 < DEPTH
                     else (lambda deps: ()))
        act, normed, saved, wts = _layer_fwd(act, normed, gathered(l, "w_in", [act])[0], small,
                                             g_mix[l + 1:l + 2] if l + 1 < DEPTH else None, hooks)
        layers.append((wts, small, saved))
    loss_blk, dact, dg_final = _loss_head(act, g_final.reshape(1, D_MODEL), loss_target[0])

    pipe = _GradPipeline(pos)
    small_grads = [None] * DEPTH
    after = ()
    for l in reversed(range(DEPTH)):
        wts, small, saved = layers[l]
        tick = lambda deps, new, l=l: pipe.tick(deps, [(nm, l, g) for nm, g in new])
        dact, small_grads[l], after = _layer_bwd(dact, saved, wts, small, tick, after)
    grad_x = dact[None]

    stack = lambda key, rows=None: jnp.stack([small_grads[l][key][0] if rows is None else small_grads[l][key][:rows]
                                              for l in range(DEPTH)])
    local_small = dict(
        g_mix=stack("g_mix"), lower_bounds=stack("lbs"), g_hgrn_out=stack("g_hgrn_out"), w_conv=stack("w_conv", CONV_K),
        sg_ln_g=stack("sg_ln_g"), sg_ln_b=stack("sg_ln_b"), w_sg=jnp.stack([small_grads[l]["w_sg"] for l in range(DEPTH)]),
        b_sg=jnp.stack([small_grads[l]["b_sg"].reshape(SG_GROUPS, SG_CHUNK) for l in range(DEPTH)]),
        g_ffn=stack("g_ffn"), g_final=dg_final[0])
    shapes = [local_small[n].shape for n in SMALL_NAMES] + [(SUBLANES, LANES)]
    packed = _pack([local_small[n] for n in SMALL_NAMES] + [loss_blk])
    packed = lax.dynamic_update_slice(jnp.zeros((8, *packed.shape), F32), packed[None], (device, 0, 0))
    small_flight = _split_start("small_grads_start", [packed], _all_to_all_copies, 7, after=after)

    def adam(n, first, layer_grads, into=None, after=()):
        return _adamw_layers("adamw_%s_%d" % (n, first), _as_2d(n, weights[n]), _as_2d(n, mom1[n]), _as_2d(n, mom2[n]),
                             layer_grads, first, into, after)

    done = {"w_ff1": adam("w_ff1", 0, pipe.reduced["w_ff1"], after=(small_flight[3],))}
    token = pipe.tick([done["w_ff1"][1]], [])
    done["w_ff2"] = adam("w_ff2", 0, pipe.reduced["w_ff2"], after=token)
    summed = _sum_devices(_split_wait("small_grads_wait", small_flight, _all_to_all_copies, after=[done["w_ff2"][1]])[0])
    parts = _unpack(summed, shapes)
    loss = parts[-1][0, 0]
    small_grad = dict(zip(SMALL_NAMES, parts[:-1]))
    small_grad["lower_bounds"] = _lbs_bwd(lower_bounds, small_grad["lower_bounds"])
    small_grad["w_conv"] = lax.dynamic_slice_in_dim(small_grad["w_conv"], pos[0] * conv_cols, conv_cols, axis=2)
    g_flat = _pack([small_grad[n] for n in SMALL_NAMES])
    d_flat, m_flat, v_flat = _adamw_flat(_pack([weights[n] for n in SMALL_NAMES]), g_flat,
                                         _pack([mom1[n] for n in SMALL_NAMES]), _pack([mom2[n] for n in SMALL_NAMES]))
    small_shapes = [weights[n].shape for n in SMALL_NAMES]
    grads = dict(small_grad)
    delta = dict(zip(SMALL_NAMES, _unpack(d_flat, small_shapes)))
    new_m = dict(zip(SMALL_NAMES, _unpack(m_flat, small_shapes)))
    new_v = dict(zip(SMALL_NAMES, _unpack(v_flat, small_shapes)))

    for n in ("w_o", "w_branch"):
        done[n] = adam(n, 0, pipe.reduced[n], after=(d_flat,))
    token = pipe.tick([done["w_branch"][1]], [])
    rest = adam("w_in", 1, pipe.reduced["w_in"][1:], after=token)
    pipe.tick([rest[1]], [])
    assert not pipe.busy()
    done["w_in"] = adam("w_in", 0, pipe.reduced["w_in"][:1], into=rest)
    for n in BIG_NAMES:
        grads[n], delta[n], new_m[n], new_v[n] = [o.reshape(weights[n].shape) for o in done[n]]

    return (loss, grad_x, *[grads[n] for n in WEIGHT_ORDER], *[delta[n] for n in WEIGHT_ORDER],
            *[new_m[n] for n in WEIGHT_ORDER], *[new_v[n] for n in WEIGHT_ORDER])
```

```python
import numpy as np
import jax
import jax.numpy as jnp
from jax import lax
from jax.experimental import pallas as pl
from jax.experimental.pallas import tpu as pltpu

F32, BF16 = jnp.float32, jnp.bfloat16

D_MODEL = 1024
WIDTH = 512
N_BRANCH = 3
N_HEAD = 4
HEAD = 128
H_CHUNK = 64
CONV_K = 3
SG_CHUNK = 128
SG_GROUPS = 4
D_FF = 4096
DEPTH = 4
N_CHIP = 4
IN_COLS = 9 * WIDTH + N_BRANCH * D_MODEL
GATE_COL0 = 9 * WIDTH
LB_FLOOR = 1e-30
NORM_EPS = 1e-6
LN_EPS = 1e-5
ADAM_LR, ADAM_B1, ADAM_B2, ADAM_EPS, ADAM_WD, ADAM_STEP = 0.001, 0.9, 0.999, 1e-08, 0.01, 10

VMEM_LIMIT_BYTES = 56 * 1024 * 1024
VMEM_BLOCK_BUDGET = 44 * 1024 * 1024
SUBLANES, LANES = 8, 128
ELEMWISE_BLOCK_BYTES = 2 * 1024 * 1024

NN = (((1,), (0,)), ((), ()))
NT = (((1,), (1,)), ((), ()))
TN = (((0,), (0,)), ((), ()))
MESH = pl.DeviceIdType.MESH
ANY = pl.BlockSpec(memory_space=pl.ANY)


def _dot(a, b, dims=NN):
    return lax.dot_general(a.astype(BF16), b.astype(BF16), dims, preferred_element_type=F32)


def _params(n_axes):
    return pltpu.CompilerParams(dimension_semantics=("arbitrary",) * n_axes, vmem_limit_bytes=VMEM_LIMIT_BYTES)


def _row0(part, rows=SUBLANES):
    r = lax.broadcasted_iota(jnp.int32, (rows, part.shape[1]), 0)
    return jnp.where(r == 0, part, 0.0)


def _token_tile(T):
    return min(512, T)


def _matmul(name, a, b, *, dims, grid, a_spec, b_spec, out_specs, out_shapes, acc_shape,
            extra=(), extra_specs=(), epilogue=None, after=()):
    nk = grid[2]
    n_extra, n_out, n_in = len(extra), len(out_shapes), 2 + len(extra) + len(after)
    one_step = nk == 1

    def body(*refs):
        a_ref, b_ref = refs[0], refs[1]
        ex = refs[2:2 + n_extra]
        outs = refs[n_in:n_in + n_out]
        part = _dot(a_ref[...], b_ref[...], dims)

        def finish(total):
            res = epilogue(total, *[e[...] for e in ex]) if epilogue else (total,)
            for o, r in zip(outs, res):
                o[...] = r.astype(o.dtype)

        if one_step:
            finish(part)
            return
        acc = refs[-1]
        kk = pl.program_id(2)

        @pl.when(kk == 0)
        def _():
            acc[...] = part

        @pl.when(kk > 0)
        def _():
            acc[...] += part

        @pl.when(kk == nk - 1)
        def _():
            finish(acc[...])

    return pl.pallas_call(
        body, name=name, grid=grid,
        in_specs=[a_spec, b_spec, *extra_specs, *[ANY] * len(after)], out_specs=list(out_specs),
        out_shape=list(out_shapes), scratch_shapes=[] if one_step else [pltpu.VMEM(acc_shape, F32)],
        compiler_params=_params(3),
    )(a, b, *extra, *after)


def _mm_cols(name, a, w, out_dtypes, epilogue=None, extra=()):
    T, K = a.shape
    N = w.shape[2]
    tm = min(2 * _token_tile(T), T)
    blk = pl.BlockSpec((tm, N), lambda j, i, kk: (i, j))
    return _matmul(
        name, a, w, dims=NN, grid=(N_CHIP, T // tm, 1),
        a_spec=pl.BlockSpec((tm, K), lambda j, i, kk: (i, 0)),
        b_spec=pl.BlockSpec((None, K, N), lambda j, i, kk: (j, 0, 0)),
        out_specs=[blk] * len(out_dtypes),
        out_shapes=[jax.ShapeDtypeStruct((T, N_CHIP * N), dt) for dt in out_dtypes],
        acc_shape=(tm, N), extra=extra, extra_specs=[blk] * len(extra), epilogue=epilogue)


def _mm_rows(name, a, w, res, norm_gain=None, after=()):
    T = a.shape[0]
    K, N = N_CHIP * w.shape[1], w.shape[2]
    tm = _token_tile(T)
    blk = pl.BlockSpec((tm, N), lambda i, j, kk: (i, 0))

    def with_norm(acc, r, gain):
        xv = acc + r
        return xv, xv * lax.rsqrt(jnp.mean(xv * xv, axis=-1, keepdims=True) + NORM_EPS) * gain

    normed = norm_gain is not None
    outs = _matmul(
        name, a, w.reshape(K, N), dims=NN, grid=(T // tm, 1, 1),
        a_spec=pl.BlockSpec((tm, K), lambda i, j, kk: (i, 0)),
        b_spec=pl.BlockSpec((K, N), lambda i, j, kk: (0, 0)),
        out_specs=[blk] * (2 if normed else 1),
        out_shapes=[jax.ShapeDtypeStruct((T, N), F32)] + ([jax.ShapeDtypeStruct((T, N), BF16)] if normed else []),
        acc_shape=(tm, N), extra=(res, norm_gain) if normed else (res,),
        extra_specs=[blk] + ([pl.BlockSpec((1, N), lambda i, j, kk: (0, 0))] if normed else []),
        epilogue=with_norm if normed else (lambda acc, r: (acc + r,)), after=after)
    return outs if normed else outs[0]


def _mm_cols_t(name, g, w, out_dtype, epilogue=None, extra=(), after=()):
    T, N = g.shape
    K = N_CHIP * w.shape[1]
    tm = _token_tile(T)
    blk = pl.BlockSpec((tm, K), lambda i, j, kk: (i, 0))
    return _matmul(
        name, g, w.reshape(K, N), dims=NT, grid=(T // tm, 1, 1),
        a_spec=pl.BlockSpec((tm, N), lambda i, j, kk: (i, 0)),
        b_spec=pl.BlockSpec((K, N), lambda i, j, kk: (0, 0), pipeline_mode=pl.Buffered(1)),
        out_specs=[blk], out_shapes=[jax.ShapeDtypeStruct((T, K), out_dtype)], acc_shape=(tm, K),
        extra=extra, extra_specs=[blk] * len(extra), epilogue=epilogue, after=after)[0]


def _dgrad_norm_bwd(name, g, w, x, gain, dres):
    T = g.shape[0]
    K, N = w.shape[1], w.shape[2]
    tm = _token_tile(T)
    whole = w.size * w.dtype.itemsize <= VMEM_BLOCK_BUDGET // 2

    def norm_bwd(i, dhv, x_ref, gain_ref, dres_ref, dx_ref, dgain_ref):
        xv = x_ref[...]
        r = lax.rsqrt(jnp.mean(xv * xv, axis=-1, keepdims=True) + NORM_EPS)
        xn = xv * r
        dxn = dhv * gain_ref[...]
        dx_ref[...] = dres_ref[...] + r * (dxn - xn * jnp.mean(dxn * xn, axis=-1, keepdims=True))

        @pl.when(i == 0)
        def _():
            dgain_ref[...] = jnp.zeros_like(dgain_ref)

        dgain_ref[...] += _row0(jnp.sum(dhv * xn, axis=0, keepdims=True))

    def body_whole(g_ref, w_ref, *rest):
        dhv = _dot(g_ref[:, pl.ds(0, N)], w_ref[0], NT)
        for k in range(1, N_CHIP):
            dhv = dhv + _dot(g_ref[:, pl.ds(k * N, N)], w_ref[k], NT)
        norm_bwd(pl.program_id(0), dhv, *rest)

    def body_steps(g_ref, w_ref, x_ref, gain_ref, dres_ref, dx_ref, dgain_ref, acc):
        kk = pl.program_id(1)
        part = _dot(g_ref[...], w_ref[...], NT)

        @pl.when(kk == 0)
        def _():
            acc[...] = part

        @pl.when(kk > 0)
        def _():
            acc[...] += part

        @pl.when(kk == N_CHIP - 1)
        def _():
            norm_bwd(pl.program_id(0), acc[...], x_ref, gain_ref, dres_ref, dx_ref, dgain_ref)

    tile = pl.BlockSpec((tm, K), lambda i, *kk: (i, 0))
    row = lambda cols: pl.BlockSpec((SUBLANES if cols is None else 1, K), lambda i, *kk: (0, 0))
    if whole:
        g_spec = pl.BlockSpec((tm, N_CHIP * N), lambda i: (i, 0))
        w_spec = pl.BlockSpec(w.shape, lambda i: (0, 0, 0), pipeline_mode=pl.Buffered(1))
    else:
        g_spec = pl.BlockSpec((tm, N), lambda i, kk: (i, kk))
        w_spec = pl.BlockSpec((None, K, N), lambda i, kk: (kk, 0, 0))
    return pl.pallas_call(
        body_whole if whole else body_steps, name=name, grid=(T // tm,) if whole else (T // tm, N_CHIP),
        in_specs=[g_spec, w_spec, tile, row(1), tile], out_specs=[tile, row(None)],
        out_shape=[jax.ShapeDtypeStruct((T, K), F32), jax.ShapeDtypeStruct((SUBLANES, K), F32)],
        scratch_shapes=[] if whole else [pltpu.VMEM((tm, K), F32)],
        compiler_params=_params(1 if whole else 2))(g, w, x, gain, dres)


def _mm_wgrad(name, a, g, a_cols, g_cols, a_blocked, g_blocked, after=()):
    T = a.shape[0]
    tt = T
    while tt > LANES and 2 * 2 * tt * (a_cols + g_cols) + (2 if tt == T else 3) * 4 * a_cols * g_cols > VMEM_BLOCK_BUDGET:
        tt //= 2
    return _matmul(
        name, a, g, dims=TN, grid=(N_CHIP, 1, T // tt),
        a_spec=pl.BlockSpec((tt, a_cols), (lambda j, i, kk: (kk, j)) if a_blocked else (lambda j, i, kk: (kk, 0))),
        b_spec=pl.BlockSpec((tt, g_cols), (lambda j, i, kk: (kk, j)) if g_blocked else (lambda j, i, kk: (kk, 0))),
        out_specs=[pl.BlockSpec((None, a_cols, g_cols), lambda j, i, kk: (j, 0, 0))],
        out_shapes=[jax.ShapeDtypeStruct((N_CHIP, a_cols, g_cols), F32)], acc_shape=(a_cols, g_cols), after=after)[0]


def _rms_fwd(name, x, g, after=()):
    T, Dm = x.shape
    tm = min(256, T)

    def body(x_ref, g_ref, *rest):
        xv = x_ref[...]
        r = lax.rsqrt(jnp.mean(xv * xv, axis=-1, keepdims=True) + NORM_EPS)
        rest[-1][...] = (xv * r * g_ref[...]).astype(BF16)

    return pl.pallas_call(
        body, name=name, grid=(T // tm,),
        in_specs=[pl.BlockSpec((tm, Dm), lambda i: (i, 0)), pl.BlockSpec((1, Dm), lambda i: (0, 0))] + [ANY] * len(after),
        out_specs=pl.BlockSpec((tm, Dm), lambda i: (i, 0)),
        out_shape=jax.ShapeDtypeStruct((T, Dm), BF16), compiler_params=_params(1))(x, g, *after)


def _loss_head(x, g, tgt):
    T, Dm = x.shape
    tm = min(256, T)

    def body(x_ref, g_ref, t_ref, loss_ref, dx_ref, dg_ref):
        xv = x_ref[...]
        gv = g_ref[...]
        r = lax.rsqrt(jnp.mean(xv * xv, axis=-1, keepdims=True) + NORM_EPS)
        xn = xv * r
        err = xn * gv - t_ref[...]
        dy = err * (1.0 / Dm)
        dxn = dy * gv
        dx_ref[...] = r * (dxn - xn * jnp.mean(dxn * xn, axis=-1, keepdims=True))

        @pl.when(pl.program_id(0) == 0)
        def _():
            dg_ref[...] = jnp.zeros_like(dg_ref)
            loss_ref[...] = jnp.zeros_like(loss_ref)

        dg_ref[...] += _row0(jnp.sum(dy * xn, axis=0, keepdims=True))
        part = jnp.sum(jnp.sum(err * err, axis=-1, keepdims=True), axis=0, keepdims=True) * (0.5 / Dm)
        loss_ref[...] += jnp.broadcast_to(part, loss_ref.shape)

    tile = pl.BlockSpec((tm, Dm), lambda i: (i, 0))
    return pl.pallas_call(
        body, name="loss_head", grid=(T // tm,),
        in_specs=[tile, pl.BlockSpec((1, Dm), lambda i: (0, 0)), tile],
        out_specs=[pl.BlockSpec((SUBLANES, LANES), lambda i: (0, 0)), tile,
                   pl.BlockSpec((SUBLANES, Dm), lambda i: (0, 0))],
        out_shape=[jax.ShapeDtypeStruct((SUBLANES, LANES), F32), jax.ShapeDtypeStruct((T, Dm), F32),
                   jax.ShapeDtypeStruct((SUBLANES, Dm), F32)],
        compiler_params=_params(1))(x, g, tgt)


def _softmax_rows(lb_ref):
    rows = [lb_ref[pl.ds(i, 1), :] for i in range(DEPTH)]
    mx = rows[0]
    for r in rows[1:]:
        mx = jnp.maximum(mx, r)
    es = [jnp.exp(r - mx) for r in rows]
    tot = es[0]
    for e in es[1:]:
        tot = tot + e
    return [e / tot for e in es]


def _lbs_fwd(lower_bounds):
    def body(lb_ref, out_ref):
        sm = _softmax_rows(lb_ref)
        run = jnp.zeros_like(sm[0])
        out_ref[pl.ds(0, 1), :] = run
        for i in range(1, DEPTH):
            run = run + sm[i]
            out_ref[pl.ds(i, 1), :] = run

    return pl.pallas_call(body, name="lbs_fwd", out_shape=jax.ShapeDtypeStruct(lower_bounds.shape, F32))(lower_bounds)


def _lbs_bwd(lower_bounds, dlbs):
    def body(lb_ref, d_ref, out_ref):
        sm = _softmax_rows(lb_ref)
        dsm = [jnp.zeros_like(sm[0])]
        for i in range(1, DEPTH):
            acc = d_ref[pl.ds(i, 1), :]
            for l in range(i + 1, DEPTH):
                acc = acc + d_ref[pl.ds(l, 1), :]
            dsm.append(acc)
        inner = dsm[0] * sm[0]
        for i in range(1, DEPTH):
            inner = inner + dsm[i] * sm[i]
        for i in range(DEPTH):
            out_ref[pl.ds(i, 1), :] = sm[i] * (dsm[i] - inner)

    return pl.pallas_call(body, name="lbs_bwd", out_shape=jax.ShapeDtypeStruct(lower_bounds.shape, F32))(lower_bounds, dlbs)


N_LEVEL = 6


def _hgrn_consts():
    L = H_CHUNK
    t = np.arange(L)
    blocks = [(t[:, None] >= t[None, :]).astype(np.float32)]
    masks = []
    m = L // 2
    while m >= 1:
        blk, pos = t // (2 * m), t % (2 * m)
        start = blk * 2 * m
        mat = np.zeros((L, L), np.float32)
        for r in range(L):
            if pos[r] >= m:
                mat[r, start[r] + m:r + 1] = 1.0
            else:
                mat[r, r + 1:start[r] + m] = -1.0
        blocks.append(mat)
        masks.append(((blk[:, None] == blk[None, :]) & (pos[:, None] >= m) & (pos[None, :] < m)).astype(np.float32))
        m //= 2
    blocks.append(np.ones((L, L), np.float32))
    return jnp.asarray(np.concatenate(blocks, 0), BF16), jnp.asarray(np.stack(masks), F32)


def _hgrn_core(qraw, fp, lb, sum_mat, mask_ref):
    L = H_CHUNK
    sq = jax.nn.sigmoid(qraw)
    q = qraw * sq
    sneg = jax.nn.sigmoid(-fp)
    log_sig = jnp.minimum(fp, 0.0) - jnp.log1p(jnp.exp(-jnp.abs(fp)))
    a1 = jnp.log(jnp.maximum(lb, LB_FLOOR))
    a2 = jnp.log1p(-lb) + log_sig
    logf = jnp.maximum(a1, a2) + jnp.log1p(jnp.exp(-jnp.abs(a1 - a2)))
    w1 = jnp.exp(a1 - logf)
    w2 = jnp.exp(a2 - logf)
    k = (1.0 - lb) * sneg
    hi = logf.astype(BF16)
    r1 = logf - hi.astype(F32)
    mid = r1.astype(BF16)
    lo = (r1 - mid.astype(F32)).astype(BF16)
    sums = lax.dot_general(sum_mat, jnp.concatenate([hi, mid, lo], axis=1), NN, preferred_element_type=F32)
    sums = sums[:, 0:HEAD] + sums[:, HEAD:2 * HEAD] + sums[:, 2 * HEAD:3 * HEAD]
    b = sums[0:L]
    b_last = sums[(N_LEVEL + 1) * L:(N_LEVEL + 2) * L]
    eye = lax.broadcasted_iota(jnp.int32, (L, L), 0) == lax.broadcasted_iota(jnp.int32, (L, L), 1)
    attn = jnp.where(eye, jnp.sum(q * k, axis=1, keepdims=True), 0.0)
    fa, fb, ea, eb = [], [], [], []
    for l in range(N_LEVEL):
        d = sums[(l + 1) * L:(l + 2) * L]
        e_a = jnp.exp(jnp.minimum(d, 0.0))
        e_b = jnp.exp(jnp.minimum(-d, 0.0))
        a_l, b_l = q * e_a, k * e_b
        attn = attn + mask_ref[l] * _dot(a_l, b_l, NT)
        fa.append(a_l), fb.append(b_l), ea.append(e_a), eb.append(e_b)
    return dict(sq=sq, q=q, sneg=sneg, logf=logf, w1=w1, w2=w2, k=k, b=b, b_last=b_last, attn=attn,
                fa=fa, fb=fb, ea=ea, eb=eb)


def _hgrn_fwd(p, lbrow, gout):
    T = p.shape[0]
    nch = T // H_CHUNK
    sum_mat, masks = _hgrn_consts()

    def body(p_ref, lb_ref, g_ref, m_ref, mask_ref, o_ref, z_ref, st_ref, state):
        @pl.when(pl.program_id(0) == 0)
        def _():
            state[...] = jnp.zeros_like(state)

        sum_m = m_ref[...]
        for h in range(N_HEAD):
            col = lambda part: pl.ds(part * WIDTH + h * HEAD, HEAD)
            hs = pl.ds(h * HEAD, HEAD)
            v = p_ref[:, col(2)]
            c = _hgrn_core(p_ref[:, col(0)], p_ref[:, col(1)], lb_ref[:, hs], sum_m, mask_ref)
            s0 = state[h]
            st_ref[h] = s0
            o = _dot(c["attn"], v) + _dot(c["q"] * jnp.exp(c["b"]), s0, NT)
            k_dec = c["k"] * jnp.exp(c["b_last"] - c["b"])
            decay = jnp.exp(jnp.max(c["b_last"], axis=0, keepdims=True))
            state[h] = s0 * decay + _dot(v, k_dec, TN)
            o_ref[:, hs] = o
            r = lax.rsqrt(jnp.mean(o * o, axis=-1, keepdims=True) + NORM_EPS)
            z_ref[:, hs] = (o * r * g_ref[:, hs] * jax.nn.sigmoid(p_ref[:, col(3)])).astype(BF16)

    full = lambda shape: pl.BlockSpec(shape, lambda c: (0,) * len(shape))
    return pl.pallas_call(
        body, name="hgrn_fwd", grid=(nch,),
        in_specs=[pl.BlockSpec((H_CHUNK, 4 * WIDTH), lambda c: (c, 0)), full((1, WIDTH)), full((1, WIDTH)),
                  full(sum_mat.shape), full(masks.shape)],
        out_specs=[pl.BlockSpec((H_CHUNK, WIDTH), lambda c: (c, 0)),
                   pl.BlockSpec((None, H_CHUNK, WIDTH), lambda c: (0, c, 0)),
                   pl.BlockSpec((None, N_HEAD, HEAD, HEAD), lambda c: (c, 0, 0, 0))],
        out_shape=[jax.ShapeDtypeStruct((T, WIDTH), F32), jax.ShapeDtypeStruct((N_BRANCH, T, WIDTH), BF16),
                   jax.ShapeDtypeStruct((nch, N_HEAD, HEAD, HEAD), F32)],
        scratch_shapes=[pltpu.VMEM((N_HEAD, HEAD, HEAD), F32)], compiler_params=_params(1),
    )(p, lbrow, gout, sum_mat, masks)


def _hgrn_bwd(p, o_saved, dz, states, lbrow, gout, dp, after=()):
    T = p.shape[0]
    nch = T // H_CHUNK
    L = H_CHUNK
    sum_mat, masks = _hgrn_consts()

    def body(p_ref, o_ref, dz_ref, st_ref, lb_ref, g_ref, m_ref, mask_ref, dp_in, *rest):
        del dp_in
        dp_ref, dlb_ref, dg_ref, dstate = rest[len(after):]

        @pl.when(pl.program_id(0) == 0)
        def _():
            dstate[...] = jnp.zeros_like(dstate)
            dlb_ref[...] = jnp.zeros_like(dlb_ref)
            dg_ref[...] = jnp.zeros_like(dg_ref)

        sum_m = m_ref[...]
        for h in range(N_HEAD):
            col = lambda part: pl.ds(part * WIDTH + h * HEAD, HEAD)
            hs = pl.ds(h * HEAD, HEAD)
            qraw, fp, v, go = p_ref[:, col(0)], p_ref[:, col(1)], p_ref[:, col(2)], p_ref[:, col(3)]
            lb, g = lb_ref[:, hs], g_ref[:, hs]
            c = _hgrn_core(qraw, fp, lb, sum_m, mask_ref)
            q, k, b, b_last = c["q"], c["k"], c["b"], c["b_last"]
            s0, ds1 = st_ref[h], dstate[h]
            e_b = jnp.exp(b)
            q_dec = q * e_b
            e_bl = jnp.exp(b_last - b)
            k_dec = k * e_bl
            decay = jnp.exp(jnp.max(b_last, axis=0, keepdims=True))
            o = o_ref[:, hs]
            r = lax.rsqrt(jnp.mean(o * o, axis=-1, keepdims=True) + NORM_EPS)
            n = o * r
            sgo = jax.nn.sigmoid(go)
            dza = dz_ref[:, hs]
            dgo = dza * n * g * sgo * (1.0 - sgo)
            dg_ref[:, hs] += _row0(jnp.sum(dza * n * sgo, axis=0, keepdims=True))
            dn = dza * g * sgo
            do = r * (dn - n * jnp.mean(dn * n, axis=-1, keepdims=True))
            dattn = _dot(do, v, NT)
            dv = _dot(c["attn"], do, TN) + _dot(k_dec, ds1, NT)
            dq_dec = _dot(do, s0)
            dk_dec = _dot(v, ds1)
            ddiag = jnp.sum(do * v, axis=1, keepdims=True)
            dq = dq_dec * e_b + ddiag * k
            dk = dk_dec * e_bl + ddiag * q
            dsums = [dq_dec * q_dec - dk_dec * k_dec]
            for l in range(N_LEVEL):
                dm = mask_ref[l] * dattn
                da = _dot(dm, c["fb"][l])
                db = _dot(dm, c["fa"][l], TN)
                dq = dq + da * c["ea"][l]
                dk = dk + db * c["eb"][l]
                dsums.append(da * c["fa"][l] - db * c["fb"][l])
            dlast = jnp.sum(ds1 * s0, axis=0, keepdims=True) * decay
            dsums.append(dk_dec * k_dec + _row0(dlast, L))
            dlogf = _dot(sum_m, jnp.concatenate(dsums, axis=0), TN)
            dstate[h] = ds1 * decay + _dot(do, q_dec, TN)
            sq, sneg = c["sq"], c["sneg"]
            dqraw = dq * sq * (1.0 + qraw * (1.0 - sq))
            dfp = dlogf * c["w2"] * sneg - dk * (1.0 - lb) * sneg * (1.0 - sneg)
            inv_lb = jnp.where(lb > LB_FLOOR, 1.0 / jnp.maximum(lb, LB_FLOOR), 0.0)
            dlb_tok = dlogf * (c["w1"] * inv_lb - c["w2"] / (1.0 - lb)) - dk * sneg
            dlb_ref[:, hs] += _row0(jnp.sum(dlb_tok, axis=0, keepdims=True))
            dp_ref[:, col(0)] = dqraw.astype(BF16)
            dp_ref[:, col(1)] = dfp.astype(BF16)
            dp_ref[:, col(2)] = dv.astype(BF16)
            dp_ref[:, col(3)] = dgo.astype(BF16)

    full = lambda shape: pl.BlockSpec(shape, lambda c: (0,) * len(shape))
    rev = lambda c: nch - 1 - c
    return pl.pallas_call(
        body, name="hgrn_bwd", grid=(nch,),
        in_specs=[pl.BlockSpec((L, 4 * WIDTH), lambda c: (rev(c), 0)), pl.BlockSpec((L, WIDTH), lambda c: (rev(c), 0)),
                  pl.BlockSpec((None, L, WIDTH), lambda c: (0, rev(c), 0)),
                  pl.BlockSpec((None, N_HEAD, HEAD, HEAD), lambda c: (rev(c), 0, 0, 0)),
                  full((1, WIDTH)), full((1, WIDTH)), full(sum_mat.shape), full(masks.shape), ANY, *[ANY] * len(after)],
        out_specs=[pl.BlockSpec((L, 4 * WIDTH), lambda c: (rev(c), 0)), full((SUBLANES, WIDTH)), full((SUBLANES, WIDTH))],
        out_shape=[jax.ShapeDtypeStruct(dp.shape, dp.dtype), jax.ShapeDtypeStruct((SUBLANES, WIDTH), F32),
                   jax.ShapeDtypeStruct((SUBLANES, WIDTH), F32)],
        scratch_shapes=[pltpu.VMEM((N_HEAD, HEAD, HEAD), F32)], input_output_aliases={8: 0},
        compiler_params=_params(1),
    )(p, o_saved, dz, states, lbrow, gout, sum_mat, masks, dp, *after)


def _shift_down(tile, halo, s):
    tm = tile.shape[0]
    rows = lax.broadcasted_iota(jnp.int32, tile.shape, 0)
    head = jnp.concatenate([pltpu.roll(halo, s, 0), jnp.zeros((tm - SUBLANES, tile.shape[1]), tile.dtype)], axis=0)
    return jnp.where(rows < s, head, pltpu.roll(tile, s, 0))


def _shift_up(tile, halo, s):
    tm = tile.shape[0]
    rows = lax.broadcasted_iota(jnp.int32, tile.shape, 0)
    tail = jnp.concatenate([jnp.zeros((tm - SUBLANES, tile.shape[1]), tile.dtype), pltpu.roll(halo, SUBLANES - s, 0)], axis=0)
    return jnp.where(rows >= tm - s, tail, pltpu.roll(tile, tm - s, 0))


def _conv_fwd(p, w, z, after=()):
    T = p.shape[0]
    tm = _token_tile(T)
    per = tm // SUBLANES

    def body(bg_ref, cg_ref, xc_ref, hcg_ref, hxc_ref, w_ref, *rest):
        z_ref = rest[-1]
        zc = cg_ref[...] * xc_ref[...]
        hz = jnp.where(pl.program_id(0) > 0, hcg_ref[...] * hxc_ref[...], 0.0)
        y = (w_ref[pl.ds(0, 1), :] * _shift_down(zc, hz, 2) + w_ref[pl.ds(1, 1), :] * _shift_down(zc, hz, 1)
             + w_ref[pl.ds(2, 1), :] * zc)
        z_ref[...] = (bg_ref[...] * y).astype(BF16)

    tile = lambda cb: pl.BlockSpec((tm, WIDTH), lambda i: (i, cb))
    prev = lambda cb: pl.BlockSpec((SUBLANES, WIDTH), lambda i: (jnp.maximum(i * per - 1, 0), cb))
    return pl.pallas_call(
        body, name="conv_fwd", grid=(T // tm,),
        in_specs=[tile(4), tile(5), tile(6), prev(5), prev(6), pl.BlockSpec((CONV_K, WIDTH), lambda i: (0, 0)), ANY,
                  *[ANY] * len(after)],
        out_specs=pl.BlockSpec((None, tm, WIDTH), lambda i: (1, i, 0)),
        out_shape=jax.ShapeDtypeStruct(z.shape, z.dtype), input_output_aliases={6: 0}, compiler_params=_params(1),
    )(p, p, p, p, p, w, z, *after)


def _conv_bwd(p, w, dz, dp):
    T = p.shape[0]
    tm = _token_tile(T)
    per = tm // SUBLANES
    last = T // SUBLANES - 1

    def body(bg_ref, cg_ref, xc_ref, hcg_ref, hxc_ref, nbg_ref, dzb_ref, ndzb_ref, w_ref, dp_in, dp_ref, dw_ref, stash):
        del dp_in
        i, jj = pl.program_id(0), pl.program_id(1)

        @pl.when(jnp.logical_and(i == 0, jj == 0))
        def _():
            dw_ref[...] = jnp.zeros_like(dw_ref)

        @pl.when(jj == 0)
        def _():
            cg, xc, bg = cg_ref[...], xc_ref[...], bg_ref[...]
            w0, w1, w2 = w_ref[pl.ds(0, 1), :], w_ref[pl.ds(1, 1), :], w_ref[pl.ds(2, 1), :]
            zc = cg * xc
            hz = jnp.where(i > 0, hcg_ref[...] * hxc_ref[...], 0.0)
            z2, z1 = _shift_down(zc, hz, 2), _shift_down(zc, hz, 1)
            y = w0 * z2 + w1 * z1 + w2 * zc
            dzb = dzb_ref[...]
            dy = dzb * bg
            hdy = jnp.where(i < pl.num_programs(0) - 1, ndzb_ref[...] * nbg_ref[...], 0.0)
            dzc = w2 * dy + w1 * _shift_up(dy, hdy, 1) + w0 * _shift_up(dy, hdy, 2)
            rows = lax.broadcasted_iota(jnp.int32, (SUBLANES, WIDTH), 0)
            colsum = lambda t: jnp.sum(t, axis=0, keepdims=True)
            dw_ref[...] += (jnp.where(rows == 0, colsum(dy * z2), 0.0) + jnp.where(rows == 1, colsum(dy * z1), 0.0)
                            + jnp.where(rows == 2, colsum(dy * zc), 0.0))
            dp_ref[...] = (dzb * y).astype(BF16)
            stash[0] = dzc * xc
            stash[1] = dzc * cg

        @pl.when(jj > 0)
        def _():
            dp_ref[...] = stash[jj - 1].astype(BF16)

    n_tiles = T // tm
    tile = lambda cb: pl.BlockSpec((tm, WIDTH), lambda i, jj: (i, cb))
    prev = lambda cb: pl.BlockSpec((SUBLANES, WIDTH), lambda i, jj: (jnp.maximum(i * per - 1, 0), cb))
    nxt = lambda i: jnp.minimum((i + 1) * per, last)
    return pl.pallas_call(
        body, name="conv_bwd", grid=(n_tiles, 3),
        in_specs=[tile(4), tile(5), tile(6), prev(5), prev(6),
                  pl.BlockSpec((SUBLANES, WIDTH), lambda i, jj: (nxt(i), 4)),
                  pl.BlockSpec((None, tm, WIDTH), lambda i, jj: (1, i, 0)),
                  pl.BlockSpec((None, SUBLANES, WIDTH), lambda i, jj: (1, nxt(i), 0)),
                  pl.BlockSpec((CONV_K, WIDTH), lambda i, jj: (0, 0)), ANY],
        out_specs=[pl.BlockSpec((tm, WIDTH), lambda i, jj: (i, 4 + jj)),
                   pl.BlockSpec((SUBLANES, WIDTH), lambda i, jj: (0, 0))],
        out_shape=[jax.ShapeDtypeStruct(dp.shape, dp.dtype), jax.ShapeDtypeStruct((SUBLANES, WIDTH), F32)],
        scratch_shapes=[pltpu.VMEM((2, tm, WIDTH), F32)], input_output_aliases={9: 0}, compiler_params=_params(2),
    )(p, p, p, p, p, p, dz, dz, w, dp)


GELU_C = float(np.sqrt(2.0 / np.pi))
GELU_A = 0.044715


def _gelu(x):
    th = jnp.tanh(GELU_C * (x + GELU_A * x * x * x))
    return 0.5 * x * (1.0 + th), th


def _gelu_grad(x, th):
    return 0.5 * (1.0 + th) + 0.5 * x * (1.0 - th * th) * GELU_C * (1.0 + 3.0 * GELU_A * x * x)


def _sg_core(u, v, lng, lnb, ws_ref, bs_ref):
    gu, thu = _gelu(u)
    gv, thv = _gelu(v)
    xc = gv - jnp.mean(gv, axis=-1, keepdims=True)
    rs = lax.rsqrt(jnp.mean(xc * xc, axis=-1, keepdims=True) + LN_EPS)
    xh = xc * rs
    vp = xh * lng + lnb
    tril = (lax.broadcasted_iota(jnp.int32, (SG_CHUNK, SG_CHUNK), 0)
            >= lax.broadcasted_iota(jnp.int32, (SG_CHUNK, SG_CHUNK), 1))
    wm = [jnp.where(tril, ws_ref[g], 0.0).astype(BF16) for g in range(SG_GROUPS)]
    gs = lambda t, g: t[:, g * LANES:(g + 1) * LANES]
    sv = jnp.concatenate([_dot(wm[g], gs(vp, g)) + bs_ref[g] for g in range(SG_GROUPS)], axis=1)
    return dict(gu=gu, thu=thu, thv=thv, rs=rs, xh=xh, vp=vp, tril=tril, wm=wm, sv=sv)


def _sg_fwd(p, lng, lnb, ws, bs, z):
    T = p.shape[0]

    def body(u_ref, v_ref, lng_ref, lnb_ref, ws_ref, bs_ref, z_in, z_ref):
        del z_in
        c = _sg_core(u_ref[...], v_ref[...], lng_ref[...], lnb_ref[...], ws_ref, bs_ref)
        z_ref[...] = (c["gu"] * c["sv"]).astype(BF16)

    full = lambda shape: pl.BlockSpec(shape, lambda c: (0,) * len(shape))
    return pl.pallas_call(
        body, name="sg_fwd", grid=(T // SG_CHUNK,),
        in_specs=[pl.BlockSpec((SG_CHUNK, WIDTH), lambda c: (c, 7)), pl.BlockSpec((SG_CHUNK, WIDTH), lambda c: (c, 8)),
                  full((1, WIDTH)), full((1, WIDTH)), full(ws.shape), full(bs.shape), ANY],
        out_specs=pl.BlockSpec((None, SG_CHUNK, WIDTH), lambda c: (2, c, 0)),
        out_shape=jax.ShapeDtypeStruct(z.shape, z.dtype), input_output_aliases={6: 0}, compiler_params=_params(1),
    )(p, p, lng, lnb, ws, bs, z)


def _sg_bwd(p, lng, lnb, ws, bs, dz, dp):
    T = p.shape[0]

    def body(u_ref, v_ref, lng_ref, lnb_ref, ws_ref, bs_ref, dz_ref, dp_in, dp_ref, dws_ref, dbs_ref, dlng_ref, dlnb_ref,
             stash):
        del dp_in
        cidx, jj = pl.program_id(0), pl.program_id(1)

        @pl.when(jnp.logical_and(cidx == 0, jj == 0))
        def _():
            dws_ref[...] = jnp.zeros_like(dws_ref)
            dbs_ref[...] = jnp.zeros_like(dbs_ref)
            dlng_ref[...] = jnp.zeros_like(dlng_ref)
            dlnb_ref[...] = jnp.zeros_like(dlnb_ref)

        @pl.when(jj == 0)
        def _():
            u, v, lng = u_ref[...], v_ref[...], lng_ref[...]
            c = _sg_core(u, v, lng, lnb_ref[...], ws_ref, bs_ref)
            dzc = dz_ref[...]
            gs = lambda t, g: t[:, g * LANES:(g + 1) * LANES]
            dsv = dzc * c["gu"]
            dvp = []
            for g in range(SG_GROUPS):
                dsv_g = gs(dsv, g)
                dws_ref[g] += jnp.where(c["tril"], _dot(dsv_g, gs(c["vp"], g), NT), 0.0)
                dbs_ref[g] += jnp.sum(dsv_g, axis=1, keepdims=True)
                dvp.append(_dot(c["wm"][g], dsv_g, TN))
            dvp = jnp.concatenate(dvp, axis=1)
            xh = c["xh"]
            dlng_ref[...] += _row0(jnp.sum(dvp * xh, axis=0, keepdims=True))
            dlnb_ref[...] += _row0(jnp.sum(dvp, axis=0, keepdims=True))
            dxh = dvp * lng
            dgv = c["rs"] * (dxh - jnp.mean(dxh, axis=-1, keepdims=True) - xh * jnp.mean(dxh * xh, axis=-1, keepdims=True))
            dp_ref[...] = (dzc * c["sv"] * _gelu_grad(u, c["thu"])).astype(BF16)
            stash[...] = dgv * _gelu_grad(v, c["thv"])

        @pl.when(jj == 1)
        def _():
            dp_ref[...] = stash[...].astype(BF16)

    full = lambda shape: pl.BlockSpec(shape, lambda c, jj: (0,) * len(shape))
    return pl.pallas_call(
        body, name="sg_bwd", grid=(T // SG_CHUNK, 2),
        in_specs=[pl.BlockSpec((SG_CHUNK, WIDTH), lambda c, jj: (c, 7)), pl.BlockSpec((SG_CHUNK, WIDTH), lambda c, jj: (c, 8)),
                  full((1, WIDTH)), full((1, WIDTH)), full(ws.shape), full(bs.shape),
                  pl.BlockSpec((None, SG_CHUNK, WIDTH), lambda c, jj: (2, c, 0)), ANY],
        out_specs=[pl.BlockSpec((SG_CHUNK, WIDTH), lambda c, jj: (c, 7 + jj)), full(ws.shape), full(bs.shape),
                   full((SUBLANES, WIDTH)), full((SUBLANES, WIDTH))],
        out_shape=[jax.ShapeDtypeStruct(dp.shape, dp.dtype), jax.ShapeDtypeStruct(ws.shape, F32),
                   jax.ShapeDtypeStruct(bs.shape, F32), jax.ShapeDtypeStruct((SUBLANES, WIDTH), F32),
                   jax.ShapeDtypeStruct((SUBLANES, WIDTH), F32)],
        scratch_shapes=[pltpu.VMEM((SG_CHUNK, WIDTH), F32)], input_output_aliases={7: 0}, compiler_params=_params(2),
    )(p, p, lng, lnb, ws, bs, dz, dp)


BRANCH_COLS = D_MODEL // N_CHIP
GATE_UNIT0 = GATE_COL0 // WIDTH
UNITS = D_MODEL // WIDTH


def _unit_specs(order):
    def spec(which):
        def index(*g):
            _, n, u = order(*g)
            return (2 * u + which, n, 0, 0)
        return pl.BlockSpec((None, None, WIDTH, BRANCH_COLS), index)
    return [spec(0), spec(1)]


def _merge_fwd(z, p, wb):
    T = z.shape[1]
    tm = min(2 * _token_tile(T), T)
    order = lambda i, u, n: (i, n, u)

    def body(z_ref, wa_ref, wb_ref, gt_ref, out_ref, acc):
        n = pl.program_id(2)
        zv = z_ref[...]
        y = jnp.concatenate([_dot(zv, wa_ref[...]), _dot(zv, wb_ref[...])], axis=1)
        part = jax.nn.sigmoid(gt_ref[...]) * y

        @pl.when(n == 0)
        def _():
            acc[...] = part

        @pl.when(n > 0)
        def _():
            acc[...] += part

        @pl.when(n == N_BRANCH - 1)
        def _():
            out_ref[...] = acc[...].astype(BF16)

    return pl.pallas_call(
        body, name="merge_fwd", grid=(T // tm, UNITS, N_BRANCH),
        in_specs=[pl.BlockSpec((None, tm, WIDTH), lambda i, u, n: (n, i, 0)), *_unit_specs(order),
                  pl.BlockSpec((tm, WIDTH), lambda i, u, n: (i, GATE_UNIT0 + UNITS * n + u))],
        out_specs=pl.BlockSpec((tm, WIDTH), lambda i, u, n: (i, u)),
        out_shape=jax.ShapeDtypeStruct((T, D_MODEL), BF16),
        scratch_shapes=[pltpu.VMEM((tm, WIDTH), F32)], compiler_params=_params(3))(z, wb, wb, p)


def _merge_bwd(z, p, wb, dmerged):
    T = z.shape[1]
    tm = min(2 * _token_tile(T), T)
    order = lambda n, u, i: (i, n, u)

    def body(z_ref, wa_ref, wb_ref, gt_ref, dm_ref, dp_ref, dw_ref, dz_ref):
        u, i = pl.program_id(1), pl.program_id(2)
        zv, wa, wbv = z_ref[...], wa_ref[...], wb_ref[...]
        y = jnp.concatenate([_dot(zv, wa), _dot(zv, wbv)], axis=1)
        gate = jax.nn.sigmoid(gt_ref[...])
        dm = dm_ref[...]
        dp_ref[...] = (dm * y * gate * (1.0 - gate)).astype(BF16)
        dyv = (dm * gate).astype(BF16)
        dw = _dot(zv, dyv, TN)
        part = _dot(dyv[:, :BRANCH_COLS], wa, NT) + _dot(dyv[:, BRANCH_COLS:], wbv, NT)
        rows = pl.ds(pl.multiple_of(i * tm, tm), tm)

        @pl.when(i == 0)
        def _():
            dw_ref[0] = dw[:, :BRANCH_COLS]
            dw_ref[1] = dw[:, BRANCH_COLS:]

        @pl.when(i > 0)
        def _():
            dw_ref[0] += dw[:, :BRANCH_COLS]
            dw_ref[1] += dw[:, BRANCH_COLS:]

        @pl.when(u == 0)
        def _():
            dz_ref[rows, :] = part

        @pl.when(u > 0)
        def _():
            dz_ref[rows, :] += part

    unit = lambda n, u, i: (i, GATE_UNIT0 + UNITS * n + u)
    return pl.pallas_call(
        body, name="merge_bwd", grid=(N_BRANCH, UNITS, T // tm),
        in_specs=[pl.BlockSpec((None, tm, WIDTH), lambda n, u, i: (n, i, 0)), *_unit_specs(order),
                  pl.BlockSpec((tm, WIDTH), unit), pl.BlockSpec((tm, WIDTH), lambda n, u, i: (i, u))],
        out_specs=[pl.BlockSpec((tm, WIDTH), unit),
                   pl.BlockSpec((2, None, WIDTH, BRANCH_COLS), lambda n, u, i: (u, n, 0, 0)),
                   pl.BlockSpec((None, T, WIDTH), lambda n, u, i: (n, 0, 0))],
        out_shape=[jax.ShapeDtypeStruct((T, IN_COLS), BF16),
                   jax.ShapeDtypeStruct((N_CHIP, N_BRANCH, WIDTH, BRANCH_COLS), F32),
                   jax.ShapeDtypeStruct((N_BRANCH, T, WIDTH), F32)],
        compiler_params=_params(3))(z, wb, wb, p, dmerged)


def _layer_fwd(x, h, win, small, next_gain, hooks):
    p = _mm_cols("in_proj", h, win, [F32])[0]
    o_hgrn, z, states = _hgrn_fwd(p, small["lbs"], small["g_hgrn_out"])
    z = _conv_fwd(p, small["w_conv"], z, after=hooks["after_hgrn"]([o_hgrn]))
    z = _sg_fwd(p, small["sg_ln_g"], small["sg_ln_b"], small["w_sg"], small["b_sg"], z)
    wb, wo, w1, w2 = hooks["late_weights"]([z])
    wb = wb.reshape(N_CHIP, N_BRANCH, WIDTH, BRANCH_COLS)
    merged = _merge_fwd(z, p, wb)
    x_mid, h2 = _mm_rows("out_proj", merged, wo, x, small["g_ffn"])
    s = _mm_cols("ff1", h2, w1, [BF16], epilogue=lambda acc: (jnp.square(jnp.maximum(acc, 0.0)),))[0]
    if next_gain is None:
        x_out, h_next = _mm_rows("ff2_last", s, w2, x_mid, after=hooks["before_last"]([s])), None
    else:
        x_out, h_next = _mm_rows("ff2", s, w2, x_mid, next_gain, after=hooks["before_last"]([s]))
    saved = dict(x=x, h=h, p=p, o_hgrn=o_hgrn, z=z, states=states, merged=merged, x_mid=x_mid, h2=h2, s=s)
    return x_out, h_next, saved, [win, wb, wo, w1, w2]


def _layer_bwd(dx_out, sv, wts, small, tick, after):
    win, wb, wo, w1, w2 = wts
    g = {}
    da = _mm_cols_t("ff2_dgrad", dx_out, w2, BF16, extra=(sv["s"],), after=after,
                    epilogue=lambda acc, s: (acc * 2.0 * jnp.sqrt(s.astype(F32)),))
    d_ff2 = _mm_wgrad("ff2_wgrad", sv["s"], dx_out, w2.shape[1], D_MODEL, True, False)
    d_ff1 = _mm_wgrad("ff1_wgrad", sv["h2"], da, D_MODEL, w1.shape[2], False, True)
    dx_mid, g["g_ffn"] = _dgrad_norm_bwd("ff1_dgrad", da, w1, sv["x_mid"], small["g_ffn"], dx_out)
    after = tick([dx_mid], [("w_ff1", d_ff1), ("w_ff2", d_ff2)])
    dmerged = _mm_cols_t("out_proj_dgrad", dx_mid, wo, F32, after=after)
    d_o = _mm_wgrad("out_proj_wgrad", sv["merged"], dx_mid, wo.shape[1], D_MODEL, True, False)
    dp, d_branch, dz = _merge_bwd(sv["z"], sv["p"], wb, dmerged)
    d_branch = d_branch.reshape(N_CHIP, N_BRANCH * WIDTH, BRANCH_COLS)
    after = tick([dp], [("w_branch", d_branch), ("w_o", d_o)])
    dp, g["lbs"], g["g_hgrn_out"] = _hgrn_bwd(sv["p"], sv["o_hgrn"], dz, sv["states"], small["lbs"],
                                              small["g_hgrn_out"], dp, after=after)
    dp, g["w_conv"] = _conv_bwd(sv["p"], small["w_conv"], dz, dp)
    dp, g["w_sg"], g["b_sg"], g["sg_ln_g"], g["sg_ln_b"] = _sg_bwd(
        sv["p"], small["sg_ln_g"], small["sg_ln_b"], small["w_sg"], small["b_sg"], dz, dp)
    after = tick([dp], [])
    d_in = _mm_wgrad("in_proj_wgrad", sv["h"], dp, D_MODEL, win.shape[2], False, True, after=after)
    dx, g["g_mix"] = _dgrad_norm_bwd("in_proj_dgrad", dp, win, sv["x"], small["g_mix"], dx_mid)
    return dx, g, tick([dx], [("w_in", d_in)])


def _mesh_pos():
    return lax.axis_index("x"), lax.axis_index("y"), lax.axis_index("c")


def _other_chips(x, y):
    return [(1 - x, y), (x, 1 - y), (1 - x, 1 - y)]


def _remote(src, dst, send_sems, recv_sems, k, to):
    return pltpu.make_async_remote_copy(src_ref=src, dst_ref=dst, send_sem=send_sems.at[k], recv_sem=recv_sems.at[k],
                                        device_id=to, device_id_type=MESH)


def _gather_call(name, body, buf, after):
    scratch = [pltpu.SemaphoreType.DMA((7,)), pltpu.SemaphoreType.DMA((7,))]
    return pl.pallas_call(
        body, name=name, in_specs=[ANY] * (1 + len(after)), out_specs=ANY,
        out_shape=jax.ShapeDtypeStruct(buf.shape, buf.dtype), scratch_shapes=scratch, input_output_aliases={0: 0})(buf, *after)


HBM = pl.BlockSpec(memory_space=pltpu.HBM)
SEM = pl.BlockSpec(memory_space=pltpu.SEMAPHORE)
DATAFLOW = pltpu.SideEffectType.DATAFLOW_SIDE_EFFECTING


def _split_start(name, bufs, copies, n_copies, after=()):
    n = len(bufs)

    def body(*refs):
        send_sems, recv_sems = refs[n + len(after)], refs[n + len(after) + 1]
        for cp in copies(refs[:n], send_sems, recv_sems):
            cp.start()
        refs[-1][...] = jnp.zeros_like(refs[-1])

    outs = pl.pallas_call(
        body, name=name,
        out_shape=(pltpu.SemaphoreType.DMA((n_copies,)), pltpu.SemaphoreType.DMA((n_copies,)),
                   *[pltpu.HBM(b.shape, b.dtype) for b in bufs], jax.ShapeDtypeStruct((SUBLANES, LANES), F32)),
        in_specs=[HBM] * n + [ANY] * len(after),
        out_specs=(SEM, SEM, *[HBM] * n, pl.BlockSpec(memory_space=pltpu.VMEM)),
        input_output_aliases={t: 2 + t for t in range(n)},
        compiler_params=pltpu.CompilerParams(has_side_effects=DATAFLOW),
    )(*[pltpu.with_memory_space_constraint(b, pltpu.HBM) for b in bufs], *after)
    return outs[0], outs[1], list(outs[2:2 + n]), outs[-1]


def _split_wait(name, started, copies, after):
    send_sems, recv_sems, bufs, _ = started
    n = len(bufs)

    def body(*refs):
        for cp in copies(refs[:n], refs[n], refs[n + 1]):
            cp.wait_send()
            cp.wait_recv()

    return list(pl.pallas_call(
        body, name=name, out_shape=tuple(pltpu.HBM(b.shape, b.dtype) for b in bufs),
        in_specs=[HBM] * n + [SEM, SEM] + [ANY] * len(after), out_specs=tuple([HBM] * n),
        input_output_aliases={t: t for t in range(n)},
        compiler_params=pltpu.CompilerParams(has_side_effects=DATAFLOW),
    )(*bufs, send_sems, recv_sems, *after))


def _weight_ici_copies(refs, send_sems, recv_sems):
    x, y, c = _mesh_pos()
    out = []
    for t, ref in enumerate(refs):
        rh = ref.shape[1] // 2
        mine = ref.at[2 * x + y, pl.ds(c * rh, rh), :]
        out += [_remote(mine, mine, send_sems, recv_sems, 3 * t + j, (*chip, c)) for j, chip in enumerate(_other_chips(x, y))]
    return out


def _weight_d2d_copies(refs, send_sems, recv_sems):
    x, y, c = _mesh_pos()
    out = []
    for t, ref in enumerate(refs):
        rh = ref.shape[1] // 2
        for j, chip in enumerate(_other_chips(x, y)):
            blk = ref.at[2 * chip[0] + chip[1], pl.ds(c * rh, rh), :]
            out.append(_remote(blk, blk, send_sems, recv_sems, 3 * t + j, (x, y, 1 - c)))
    return out


def _swap_part(refs, send_sems, recv_sems, s0):
    x, y, c = _mesh_pos()
    n = len(refs) // 2
    out = []
    for t in range(n):
        rh = refs[t].shape[1] // 2
        out.append(_remote(refs[t].at[:, pl.ds((1 - c) * rh, rh), :], refs[n + t], send_sems, recv_sems, s0 + t, (x, y, 1 - c)))
    return out


def _exchange_part(refs, send_sems, recv_sems, s0):
    x, y, c = _mesh_pos()
    n = len(refs) // 2
    out = []
    for t in range(n):
        for j, chip in enumerate(_other_chips(x, y)):
            out.append(_remote(refs[t].at[2 * chip[0] + chip[1]], refs[n + t].at[j], send_sems, recv_sems, s0 + 3 * t + j,
                               (*chip, c)))
    return out


def _gather_part(refs, send_sems, recv_sems, s0):
    x, y, c = _mesh_pos()
    return [_remote(ref.at[c], ref.at[c], send_sems, recv_sems, s0 + t, (x, y, 1 - c)) for t, ref in enumerate(refs)]


def _all_to_all_copies(refs, send_sems, recv_sems):
    x, y, c = _mesh_pos()
    blk = refs[0].at[4 * x + 2 * y + c]
    peers = [(x, y, 1 - c)] + [(*chip, cc) for chip in _other_chips(x, y) for cc in (c, 1 - c)]
    return [_remote(blk, blk, send_sems, recv_sems, k, peer) for k, peer in enumerate(peers)]


class _GradPipeline:
    def __init__(self, pos):
        self.pos = pos
        self.groups, self.pending, self.count = [], None, 0
        self.reduced = {n: [None] * DEPTH for n in BIG_NAMES}

    def busy(self):
        return bool(self.groups) or self.pending is not None

    def tick(self, deps, new):
        if self.pending is not None:
            started, copies, owners = self.pending
            bufs = _split_wait("grad_pipe_wait_%d" % self.count, started, copies, after=list(deps))
            for grp, lo, hi in owners:
                grp["bufs"] = bufs[lo:hi]
            self.pending = None
        parts = []
        for grp in list(self.groups):
            n, names = len(grp["names"]), grp["names"]
            if grp["stage"] == "swap":
                pair = [_pair_sum("grad_pair_sum_" + nm, f, r, self.pos)
                        for nm, f, r in zip(names, grp["bufs"][:n], grp["bufs"][n:])]
                grp["own32"] = [p32 for p32, _ in pair]
                landing = [lax.empty((3, *p16.shape[1:]), BF16) for _, p16 in pair]
                grp["stage"] = "exchange"
                parts.append((grp, [p16 for _, p16 in pair] + landing, _exchange_part, 3 * n))
            elif grp["stage"] == "exchange":
                halves = [_chip_sum("grad_chip_sum_" + nm, p32, r, self.pos)
                          for nm, p32, r in zip(names, grp["own32"], grp["bufs"][n:])]
                grp["stage"] = "gather"
                parts.append((grp, halves, _gather_part, n))
            else:
                for nm, b in zip(names, grp["bufs"]):
                    self.reduced[nm][grp["layer"]] = b.reshape(-1, b.shape[-1])
                self.groups.remove(grp)
        if new:
            grp = dict(names=[nm for nm, _, _ in new], layer=new[0][1], stage="swap")
            self.groups.append(grp)
            fulls = [g for _, _, g in new]
            landing = [lax.empty((N_CHIP, g.shape[1] // 2, g.shape[2]), F32) for g in fulls]
            parts.append((grp, fulls + landing, _swap_part, len(fulls)))
        if not parts:
            return ()
        bufs, layout, owners, sems = [], [], [], 0
        for grp, part_bufs, fn, n_sems in parts:
            layout.append((len(bufs), len(bufs) + len(part_bufs), fn, sems))
            owners.append((grp, len(bufs), len(bufs) + len(part_bufs)))
            bufs += part_bufs
            sems += n_sems

        def copies(refs, send_sems, recv_sems):
            out = []
            for lo, hi, fn, s0 in layout:
                out += fn(refs[lo:hi], send_sems, recv_sems, s0)
            return out

        started = _split_start("grad_pipe_start_%d" % self.count, bufs, copies, sems)
        self.pending = (started, copies, owners)
        self.count += 1
        return (started[3],)


def _gather_all(name, block, slot, after=()):
    buf = lax.dynamic_update_slice(jnp.zeros((8, *block.shape), block.dtype), block[None], (slot, 0, 0))

    def body(*refs):
        out_ref, send_sems, recv_sems = refs[1 + len(after):]
        x, y, c = _mesh_pos()
        chips = _other_chips(x, y)
        sibling = (x, y, 1 - c)
        slot_of = lambda px, py, pc: out_ref.at[4 * px + 2 * py + pc]
        started = [_remote(slot_of(x, y, c), slot_of(x, y, c), send_sems, recv_sems, 0, sibling)]
        started += [_remote(slot_of(x, y, c), slot_of(x, y, c), send_sems, recv_sems, 1 + j, (*chip, c))
                    for j, chip in enumerate(chips)]
        for cp in started:
            cp.start()
        for j, chip in enumerate(chips):
            _remote(slot_of(*chip, c), slot_of(*chip, c), send_sems, recv_sems, 1 + j, (*chip, c)).wait_recv()
            fw = _remote(slot_of(*chip, c), slot_of(*chip, c), send_sems, recv_sems, 4 + j, sibling)
            fw.start()
            started.append(fw)
        _remote(slot_of(x, y, 1 - c), slot_of(x, y, 1 - c), send_sems, recv_sems, 0, sibling).wait_recv()
        for j, chip in enumerate(chips):
            _remote(slot_of(*chip, 1 - c), slot_of(*chip, 1 - c), send_sems, recv_sems, 4 + j, sibling).wait_recv()
        for cp in started:
            cp.wait_send()

    return _gather_call(name, body, buf, after)


def _row_tile(rows, cols):
    cap = max(SUBLANES, ELEMWISE_BLOCK_BYTES // (4 * cols))
    tr = rows
    while tr > cap and tr % 2 == 0:
        tr //= 2
    return tr


def _pair_sum(name, grad, recv, pos):
    _, rh, cols = recv.shape
    tr = _row_tile(rh, cols)
    per = rh // tr

    def body(pos_ref, g_ref, r_ref, own_ref, out16_ref):
        s = g_ref[...] + r_ref[...]
        out16_ref[...] = s.astype(BF16)

        @pl.when(pl.program_id(1) == pos_ref[0])
        def _():
            own_ref[...] = s

    blk = pl.BlockSpec((None, tr, cols), lambda i, k, pos_ref: (k, i, 0))
    return pl.pallas_call(
        body, name=name,
        grid_spec=pltpu.PrefetchScalarGridSpec(
            num_scalar_prefetch=1, grid=(per, N_CHIP),
            in_specs=[pl.BlockSpec((None, tr, cols), lambda i, k, pos_ref: (k, pos_ref[1] * per + i, 0)), blk],
            out_specs=[pl.BlockSpec((tr, cols), lambda i, k, pos_ref: (i, 0)), blk]),
        out_shape=[jax.ShapeDtypeStruct((rh, cols), F32), jax.ShapeDtypeStruct(recv.shape, BF16)],
        compiler_params=_params(2))(pos, grad, recv)


def _chip_sum(name, own32, recv, pos):
    rh, cols = own32.shape
    tr = _row_tile(rh, cols)

    def body(pos_ref, own_ref, r_ref, out_ref):
        del pos_ref
        out_ref[...] = ((own_ref[...] + r_ref[0].astype(F32)) + r_ref[1].astype(F32)) + r_ref[2].astype(F32)

    return pl.pallas_call(
        body, name=name,
        grid_spec=pltpu.PrefetchScalarGridSpec(
            num_scalar_prefetch=1, grid=(rh // tr,),
            in_specs=[pl.BlockSpec((tr, cols), lambda i, pos_ref: (i, 0)),
                      pl.BlockSpec((3, tr, cols), lambda i, pos_ref: (0, i, 0))],
            out_specs=pl.BlockSpec((None, tr, cols), lambda i, pos_ref: (pos_ref[1], i, 0))),
        out_shape=jax.ShapeDtypeStruct((2, rh, cols), F32), compiler_params=_params(1))(pos, own32, recv)


def _cast_into_slot(name, w, layer, pos, after=()):
    _, rows, cols = w.shape
    tr = _row_tile(rows, cols)

    def body(pos_ref, w_ref, *rest):
        del pos_ref
        rest[-1][...] = w_ref[...].astype(BF16)

    return pl.pallas_call(
        body, name=name,
        grid_spec=pltpu.PrefetchScalarGridSpec(
            num_scalar_prefetch=1, grid=(rows // tr,),
            in_specs=[pl.BlockSpec((None, tr, cols), lambda i, pos_ref: (layer, i, 0))] + [ANY] * len(after),
            out_specs=pl.BlockSpec((None, tr, cols), lambda i, pos_ref: (pos_ref[0], i, 0))),
        out_shape=jax.ShapeDtypeStruct((N_CHIP, rows, cols), BF16), compiler_params=_params(1))(pos, w, *after)


def _adamw_math(w, g, m, v):
    m = ADAM_B1 * m + (1.0 - ADAM_B1) * g
    v = ADAM_B2 * v + (1.0 - ADAM_B2) * jnp.square(g)
    m_hat = m / (1.0 - ADAM_B1 ** ADAM_STEP)
    v_hat = v / (1.0 - ADAM_B2 ** ADAM_STEP)
    delta = -ADAM_LR * (m_hat / (jnp.sqrt(v_hat) + ADAM_EPS) + ADAM_WD * w)
    return delta, m, v


def _adamw_layers(name, w, m, v, grads, first, into=None, after=()):
    _, rows, cols = w.shape
    tr = _row_tile(rows, cols)
    n_layers = len(grads)

    def body(w_ref, m_ref, v_ref, *rest):
        g_refs, (grad_ref, d_ref, nm_ref, nv_ref) = rest[:n_layers], rest[len(rest) - 4:]
        layer = pl.program_id(0)
        g = g_refs[0][...]
        for l in range(1, n_layers):
            g = jnp.where(layer == l, g_refs[l][...], g)
        grad_ref[...] = g
        d_ref[...], nm_ref[...], nv_ref[...] = _adamw_math(w_ref[...], g, m_ref[...], v_ref[...])

    blk = pl.BlockSpec((None, tr, cols), lambda l, i: (first + l, i, 0))
    g_spec = lambda k: pl.BlockSpec((tr, cols), lambda l, i: (jnp.where(l == k, i, 0), 0))
    passed = list(into or []) + list(after)
    return pl.pallas_call(
        body, name=name, grid=(n_layers, rows // tr),
        in_specs=[blk, blk, blk] + [g_spec(k) for k in range(n_layers)] + [ANY] * len(passed), out_specs=[blk] * 4,
        out_shape=[jax.ShapeDtypeStruct(w.shape, F32)] * 4,
        input_output_aliases={3 + n_layers + t: t for t in range(4)} if into else {},
        compiler_params=_params(2))(w, m, v, *grads, *passed)


def _sum_devices(gathered):
    _, rows, cols = gathered.shape

    def body(g_ref, out_ref):
        s = g_ref[0]
        for d in range(1, 8):
            s = s + g_ref[d]
        out_ref[...] = s

    return pl.pallas_call(body, name="sum_devices", out_shape=jax.ShapeDtypeStruct((rows, cols), F32),
                          compiler_params=pltpu.CompilerParams(vmem_limit_bytes=VMEM_LIMIT_BYTES))(gathered)


def _adamw_flat(w, g, m, v):
    def body(w_ref, g_ref, m_ref, v_ref, d_ref, nm_ref, nv_ref):
        d_ref[...], nm_ref[...], nv_ref[...] = _adamw_math(w_ref[...], g_ref[...], m_ref[...], v_ref[...])

    return pl.pallas_call(body, name="adamw_small", out_shape=[jax.ShapeDtypeStruct(w.shape, F32)] * 3,
                          compiler_params=pltpu.CompilerParams(vmem_limit_bytes=VMEM_LIMIT_BYTES))(w, g, m, v)


SMALL_NAMES = ["g_mix", "lower_bounds", "g_hgrn_out", "w_conv", "sg_ln_g", "sg_ln_b", "w_sg", "b_sg", "g_ffn", "g_final"]
BIG_NAMES = ["w_in", "w_branch", "w_o", "w_ff1", "w_ff2"]
WEIGHT_ORDER = ["w_in", "g_mix", "lower_bounds", "g_hgrn_out", "w_conv", "sg_ln_g", "sg_ln_b", "w_sg", "b_sg", "w_branch",
                "w_o", "g_ffn", "w_ff1", "w_ff2", "g_final"]


def _padded_rows(n):
    return -(-n // SUBLANES) * SUBLANES


def _pack(arrays):
    parts = []
    for a in arrays:
        a = a.reshape(-1, LANES)
        parts.append(jnp.pad(a, ((0, _padded_rows(a.shape[0]) - a.shape[0]), (0, 0))))
    return jnp.concatenate(parts, axis=0)


def _unpack(flat, shapes):
    out, row = [], 0
    for s in shapes:
        n = int(np.prod(s)) // LANES
        out.append(flat[row:row + n].reshape(s))
        row += _padded_rows(n)
    return out


def _as_2d(name, a):
    return a.reshape(DEPTH, N_BRANCH * WIDTH, BRANCH_COLS) if name == "w_branch" else a


def kernel(x, w_in, g_mix, lower_bounds, g_hgrn_out, w_conv, sg_ln_g, sg_ln_b, w_sg, b_sg, w_branch, w_o, g_ffn, w_ff1, w_ff2, g_final, loss_target, m_w_in, m_g_mix, m_lower_bounds, m_g_hgrn_out, m_w_conv, m_sg_ln_g, m_sg_ln_b, m_w_sg, m_b_sg, m_w_branch, m_w_o, m_g_ffn, m_w_ff1, m_w_ff2, m_g_final, v_w_in, v_g_mix, v_lower_bounds, v_g_hgrn_out, v_w_conv, v_sg_ln_g, v_sg_ln_b, v_w_sg, v_b_sg, v_w_branch, v_w_o, v_g_ffn, v_w_ff1, v_w_ff2, v_g_final):
    weights = dict(w_in=w_in, g_mix=g_mix, lower_bounds=lower_bounds, g_hgrn_out=g_hgrn_out, w_conv=w_conv,
                   sg_ln_g=sg_ln_g, sg_ln_b=sg_ln_b, w_sg=w_sg, b_sg=b_sg, w_branch=w_branch, w_o=w_o, g_ffn=g_ffn,
                   w_ff1=w_ff1, w_ff2=w_ff2, g_final=g_final)
    mom1 = dict(w_in=m_w_in, g_mix=m_g_mix, lower_bounds=m_lower_bounds, g_hgrn_out=m_g_hgrn_out, w_conv=m_w_conv,
                sg_ln_g=m_sg_ln_g, sg_ln_b=m_sg_ln_b, w_sg=m_w_sg, b_sg=m_b_sg, w_branch=m_w_branch, w_o=m_w_o,
                g_ffn=m_g_ffn, w_ff1=m_w_ff1, w_ff2=m_w_ff2, g_final=m_g_final)
    mom2 = dict(w_in=v_w_in, g_mix=v_g_mix, lower_bounds=v_lower_bounds, g_hgrn_out=v_g_hgrn_out, w_conv=v_w_conv,
                sg_ln_g=v_sg_ln_g, sg_ln_b=v_sg_ln_b, w_sg=v_w_sg, b_sg=v_b_sg, w_branch=v_w_branch, w_o=v_w_o,
                g_ffn=v_g_ffn, w_ff1=v_w_ff1, w_ff2=v_w_ff2, g_final=v_g_final)
    xi, yi, ci = _mesh_pos()
    pos = jnp.stack([2 * xi + yi, ci]).astype(jnp.int32)
    device = 4 * xi + 2 * yi + ci
    conv_cols = w_conv.shape[2]

    conv_all = _gather_all("gather_w_conv", w_conv.reshape(DEPTH * CONV_K, conv_cols), device)
    conv_full = conv_all.reshape(N_CHIP, 2, DEPTH, CONV_K, conv_cols)[:, 0].transpose(1, 2, 0, 3).reshape(DEPTH, CONV_K, WIDTH)

    ici, d2d = {}, {}
    cast = lambda l, names, after: [_cast_into_slot("cast_" + n, _as_2d(n, weights[n]), l, pos, after=after) for n in names]
    token = (conv_all,)
    for l in range(DEPTH):
        for part, names in (("w_in", BIG_NAMES[:1]), ("rest", BIG_NAMES[1:])):
            ici[l, part] = _split_start("weights_ici_start_%d_%s" % (l, part), cast(l, names, token), _weight_ici_copies,
                                        3 * len(names), after=token)
            token = (ici[l, part][3],)
    lbs = _lbs_fwd(lower_bounds)

    def forward_to_sibling(l, part, deps):
        landed = _split_wait("weights_ici_wait_%d_%s" % (l, part), ici.pop((l, part)), _weight_ici_copies, after=deps)
        d2d[l, part] = _split_start("weights_d2d_start_%d_%s" % (l, part), landed, _weight_d2d_copies, 3 * len(landed))
        return (d2d[l, part][3],)

    def gathered(l, part, deps):
        return _split_wait("weights_d2d_wait_%d_%s" % (l, part), d2d.pop((l, part)), _weight_d2d_copies, after=deps)

    act = x[0]
    normed = _rms_fwd("rms_mix", act, g_mix[0:1], after=token)
    layers = []
    forward_to_sibling(0, "w_in", [normed, lbs])
    for l in range(DEPTH):
        small = dict(g_mix=g_mix[l:l + 1], lbs=lbs[l:l + 1], g_hgrn_out=g_hgrn_out[l:l + 1], w_conv=conv_full[l],
                     sg_ln_g=sg_ln_g[l:l + 1], sg_ln_b=sg_ln_b[l:l + 1], w_sg=w_sg[l],
                     b_sg=b_sg[l].reshape(SG_GROUPS, SG_CHUNK, 1), g_ffn=g_ffn[l:l + 1])
        hooks = dict(after_hgrn=lambda deps, l=l: forward_to_sibling(l, "rest", deps),
                     late_weights=lambda deps, l=l: gathered(l, "rest", deps),
                     before_last=(lambda deps, l=l: forward_to_sibling(l + 1, "w_in", deps)) if l + 1 < DEPTH
                     else (lambda deps: ()))
        act, normed, saved, wts = _layer_fwd(act, normed, gathered(l, "w_in", [act])[0], small,
                                             g_mix[l + 1:l + 2] if l + 1 < DEPTH else None, hooks)
        layers.append((wts, small, saved))
    loss_blk, dact, dg_final = _loss_head(act, g_final.reshape(1, D_MODEL), loss_target[0])

    pipe = _GradPipeline(pos)
    small_grads = [None] * DEPTH
    after = ()
    for l in reversed(range(DEPTH)):
        wts, small, saved = layers[l]
        tick = lambda deps, new, l=l: pipe.tick(deps, [(nm, l, g) for nm, g in new])
        dact, small_grads[l], after = _layer_bwd(dact, saved, wts, small, tick, after)
    grad_x = dact[None]

    stack = lambda key, rows=None: jnp.stack([small_grads[l][key][0] if rows is None else small_grads[l][key][:rows]
                                              for l in range(DEPTH)])
    local_small = dict(
        g_mix=stack("g_mix"), lower_bounds=stack("lbs"), g_hgrn_out=stack("g_hgrn_out"), w_conv=stack("w_conv", CONV_K),
        sg_ln_g=stack("sg_ln_g"), sg_ln_b=stack("sg_ln_b"), w_sg=jnp.stack([small_grads[l]["w_sg"] for l in range(DEPTH)]),
        b_sg=jnp.stack([small_grads[l]["b_sg"].reshape(SG_GROUPS, SG_CHUNK) for l in range(DEPTH)]),
        g_ffn=stack("g_ffn"), g_final=dg_final[0])
    shapes = [local_small[n].shape for n in SMALL_NAMES] + [(SUBLANES, LANES)]
    packed = _pack([local_small[n] for n in SMALL_NAMES] + [loss_blk])
    packed = lax.dynamic_update_slice(jnp.zeros((8, *packed.shape), F32), packed[None], (device, 0, 0))
    small_flight = _split_start("small_grads_start", [packed], _all_to_all_copies, 7, after=after)

    def adam(n, first, layer_grads, into=None, after=()):
        return _adamw_layers("adamw_%s_%d" % (n, first), _as_2d(n, weights[n]), _as_2d(n, mom1[n]), _as_2d(n, mom2[n]),
                             layer_grads, first, into, after)

    done = {"w_ff1": adam("w_ff1", 0, pipe.reduced["w_ff1"], after=(small_flight[3],))}
    token = pipe.tick([done["w_ff1"][1]], [])
    done["w_ff2"] = adam("w_ff2", 0, pipe.reduced["w_ff2"], after=token)
    summed = _sum_devices(_split_wait("small_grads_wait", small_flight, _all_to_all_copies, after=[done["w_ff2"][1]])[0])
    parts = _unpack(summed, shapes)
    loss = parts[-1][0, 0]
    small_grad = dict(zip(SMALL_NAMES, parts[:-1]))
    small_grad["lower_bounds"] = _lbs_bwd(lower_bounds, small_grad["lower_bounds"])
    small_grad["w_conv"] = lax.dynamic_slice_in_dim(small_grad["w_conv"], pos[0] * conv_cols, conv_cols, axis=2)
    g_flat = _pack([small_grad[n] for n in SMALL_NAMES])
    d_flat, m_flat, v_flat = _adamw_flat(_pack([weights[n] for n in SMALL_NAMES]), g_flat,
                                         _pack([mom1[n] for n in SMALL_NAMES]), _pack([mom2[n] for n in SMALL_NAMES]))
    small_shapes = [weights[n].shape for n in SMALL_NAMES]
    grads = dict(small_grad)
    delta = dict(zip(SMALL_NAMES, _unpack(d_flat, small_shapes)))
    new_m = dict(zip(SMALL_NAMES, _unpack(m_flat, small_shapes)))
    new_v = dict(zip(SMALL_NAMES, _unpack(v_flat, small_shapes)))

    for n in ("w_o", "w_branch"):
        done[n] = adam(n, 0, pipe.reduced[n], after=(d_flat,))
    token = pipe.tick([done["w_branch"][1]], [])
    rest = adam("w_in", 1, pipe.reduced["w_in"][1:], after=token)
    pipe.tick([rest[1]], [])
    assert not pipe.busy()
    done["w_in"] = adam("w_in", 0, pipe.reduced["w_in"][:1], into=rest)
    for n in BIG_NAMES:
        grads[n], delta[n], new_m[n], new_v[n] = [o.reshape(weights[n].shape) for o in done[n]]

    return (loss, grad_x, *[grads[n] for n in WEIGHT_ORDER], *[delta[n] for n in WEIGHT_ORDER],
            *[new_m[n] for n in WEIGHT_ORDER], *[new_v[n] for n in WEIGHT_ORDER])
```

```python
import numpy as np
import jax
import jax.numpy as jnp
from jax import lax
from jax.experimental import pallas as pl
from jax.experimental.pallas import tpu as pltpu

F32, BF16 = jnp.float32, jnp.bfloat16

D_MODEL = 1024
WIDTH = 512
N_BRANCH = 3
N_HEAD = 4
HEAD = 128
H_CHUNK = 64
CONV_K = 3
SG_CHUNK = 128
SG_GROUPS = 4
D_FF = 4096
DEPTH = 4
N_CHIP = 4
IN_COLS = 9 * WIDTH + N_BRANCH * D_MODEL
GATE_COL0 = 9 * WIDTH
LB_FLOOR = 1e-30
NORM_EPS = 1e-6
LN_EPS = 1e-5
ADAM_LR, ADAM_B1, ADAM_B2, ADAM_EPS, ADAM_WD, ADAM_STEP = 0.001, 0.9, 0.999, 1e-08, 0.01, 10

VMEM_LIMIT_BYTES = 56 * 1024 * 1024
VMEM_BLOCK_BUDGET = 44 * 1024 * 1024
SUBLANES, LANES = 8, 128
ELEMWISE_BLOCK_BYTES = 2 * 1024 * 1024

NN = (((1,), (0,)), ((), ()))
NT = (((1,), (1,)), ((), ()))
TN = (((0,), (0,)), ((), ()))
MESH = pl.DeviceIdType.MESH
ANY = pl.BlockSpec(memory_space=pl.ANY)


def _dot(a, b, dims=NN):
    return lax.dot_general(a.astype(BF16), b.astype(BF16), dims, preferred_element_type=F32)


def _params(n_axes):
    return pltpu.CompilerParams(dimension_semantics=("arbitrary",) * n_axes, vmem_limit_bytes=VMEM_LIMIT_BYTES)


def _row0(part, rows=SUBLANES):
    r = lax.broadcasted_iota(jnp.int32, (rows, part.shape[1]), 0)
    return jnp.where(r == 0, part, 0.0)


def _token_tile(T):
    return min(512, T)


def _matmul(name, a, b, *, dims, grid, a_spec, b_spec, out_specs, out_shapes, acc_shape,
            extra=(), extra_specs=(), epilogue=None, after=()):
    nk = grid[2]
    n_extra, n_out, n_in = len(extra), len(out_shapes), 2 + len(extra) + len(after)
    one_step = nk == 1

    def body(*refs):
        a_ref, b_ref = refs[0], refs[1]
        ex = refs[2:2 + n_extra]
        outs = refs[n_in:n_in + n_out]
        part = _dot(a_ref[...], b_ref[...], dims)

        def finish(total):
            res = epilogue(total, *[e[...] for e in ex]) if epilogue else (total,)
            for o, r in zip(outs, res):
                o[...] = r.astype(o.dtype)

        if one_step:
            finish(part)
            return
        acc = refs[-1]
        kk = pl.program_id(2)

        @pl.when(kk == 0)
        def _():
            acc[...] = part

        @pl.when(kk > 0)
        def _():
            acc[...] += part

        @pl.when(kk == nk - 1)
        def _():
            finish(acc[...])

    return pl.pallas_call(
        body, name=name, grid=grid,
        in_specs=[a_spec, b_spec, *extra_specs, *[ANY] * len(after)], out_specs=list(out_specs),
        out_shape=list(out_shapes), scratch_shapes=[] if one_step else [pltpu.VMEM(acc_shape, F32)],
        compiler_params=_params(3),
    )(a, b, *extra, *after)


def _mm_cols(name, a, w, out_dtypes, epilogue=None, extra=()):
    T, K = a.shape
    N = w.shape[2]
    tm = min(2 * _token_tile(T), T)
    blk = pl.BlockSpec((tm, N), lambda j, i, kk: (i, j))
    return _matmul(
        name, a, w, dims=NN, grid=(N_CHIP, T // tm, 1),
        a_spec=pl.BlockSpec((tm, K), lambda j, i, kk: (i, 0)),
        b_spec=pl.BlockSpec((None, K, N), lambda j, i, kk: (j, 0, 0)),
        out_specs=[blk] * len(out_dtypes),
        out_shapes=[jax.ShapeDtypeStruct((T, N_CHIP * N), dt) for dt in out_dtypes],
        acc_shape=(tm, N), extra=extra, extra_specs=[blk] * len(extra), epilogue=epilogue)


def _mm_rows(name, a, w, res, norm_gain=None, after=()):
    T = a.shape[0]
    K, N = N_CHIP * w.shape[1], w.shape[2]
    tm = _token_tile(T)
    blk = pl.BlockSpec((tm, N), lambda i, j, kk: (i, 0))

    def with_norm(acc, r, gain):
        xv = acc + r
        return xv, xv * lax.rsqrt(jnp.mean(xv * xv, axis=-1, keepdims=True) + NORM_EPS) * gain

    normed = norm_gain is not None
    outs = _matmul(
        name, a, w.reshape(K, N), dims=NN, grid=(T // tm, 1, 1),
        a_spec=pl.BlockSpec((tm, K), lambda i, j, kk: (i, 0)),
        b_spec=pl.BlockSpec((K, N), lambda i, j, kk: (0, 0)),
        out_specs=[blk] * (2 if normed else 1),
        out_shapes=[jax.ShapeDtypeStruct((T, N), F32)] + ([jax.ShapeDtypeStruct((T, N), BF16)] if normed else []),
        acc_shape=(tm, N), extra=(res, norm_gain) if normed else (res,),
        extra_specs=[blk] + ([pl.BlockSpec((1, N), lambda i, j, kk: (0, 0))] if normed else []),
        epilogue=with_norm if normed else (lambda acc, r: (acc + r,)), after=after)
    return outs if normed else outs[0]


def _mm_cols_t(name, g, w, out_dtype, epilogue=None, extra=(), after=()):
    T, N = g.shape
    K = N_CHIP * w.shape[1]
    tm = _token_tile(T) if K <= 2 * D_MODEL else _token_tile(T) // 2
    blk = pl.BlockSpec((tm, K), lambda i, j, kk: (i, 0))
    return _matmul(
        name, g, w.reshape(K, N), dims=NT, grid=(T // tm, 1, 1),
        a_spec=pl.BlockSpec((tm, N), lambda i, j, kk: (i, 0)),
        b_spec=pl.BlockSpec((K, N), lambda i, j, kk: (0, 0), pipeline_mode=pl.Buffered(1)),
        out_specs=[blk], out_shapes=[jax.ShapeDtypeStruct((T, K), out_dtype)], acc_shape=(tm, K),
        extra=extra, extra_specs=[blk] * len(extra), epilogue=epilogue, after=after)[0]


def _dgrad_norm_bwd(name, g, w, x, gain, dres):
    T = g.shape[0]
    K, N = w.shape[1], w.shape[2]
    tm = _token_tile(T)
    whole = w.size * w.dtype.itemsize <= VMEM_BLOCK_BUDGET // 2

    def norm_bwd(i, dhv, x_ref, gain_ref, dres_ref, dx_ref, dgain_ref):
        xv = x_ref[...]
        r = lax.rsqrt(jnp.mean(xv * xv, axis=-1, keepdims=True) + NORM_EPS)
        xn = xv * r
        dxn = dhv * gain_ref[...]
        dx_ref[...] = dres_ref[...] + r * (dxn - xn * jnp.mean(dxn * xn, axis=-1, keepdims=True))

        @pl.when(i == 0)
        def _():
            dgain_ref[...] = jnp.zeros_like(dgain_ref)

        dgain_ref[...] += _row0(jnp.sum(dhv * xn, axis=0, keepdims=True))

    def body_whole(g_ref, w_ref, *rest):
        dhv = _dot(g_ref[:, pl.ds(0, N)], w_ref[0], NT)
        for k in range(1, N_CHIP):
            dhv = dhv + _dot(g_ref[:, pl.ds(k * N, N)], w_ref[k], NT)
        norm_bwd(pl.program_id(0), dhv, *rest)

    def body_steps(g_ref, w_ref, x_ref, gain_ref, dres_ref, dx_ref, dgain_ref, acc):
        kk = pl.program_id(1)
        part = _dot(g_ref[...], w_ref[...], NT)

        @pl.when(kk == 0)
        def _():
            acc[...] = part

        @pl.when(kk > 0)
        def _():
            acc[...] += part

        @pl.when(kk == N_CHIP - 1)
        def _():
            norm_bwd(pl.program_id(0), acc[...], x_ref, gain_ref, dres_ref, dx_ref, dgain_ref)

    tile = pl.BlockSpec((tm, K), lambda i, *kk: (i, 0))
    row = lambda cols: pl.BlockSpec((SUBLANES if cols is None else 1, K), lambda i, *kk: (0, 0))
    if whole:
        g_spec = pl.BlockSpec((tm, N_CHIP * N), lambda i: (i, 0))
        w_spec = pl.BlockSpec(w.shape, lambda i: (0, 0, 0), pipeline_mode=pl.Buffered(1))
    else:
        g_spec = pl.BlockSpec((tm, N), lambda i, kk: (i, kk))
        w_spec = pl.BlockSpec((None, K, N), lambda i, kk: (kk, 0, 0))
    return pl.pallas_call(
        body_whole if whole else body_steps, name=name, grid=(T // tm,) if whole else (T // tm, N_CHIP),
        in_specs=[g_spec, w_spec, tile, row(1), tile], out_specs=[tile, row(None)],
        out_shape=[jax.ShapeDtypeStruct((T, K), F32), jax.ShapeDtypeStruct((SUBLANES, K), F32)],
        scratch_shapes=[] if whole else [pltpu.VMEM((tm, K), F32)],
        compiler_params=_params(1 if whole else 2))(g, w, x, gain, dres)


def _mm_wgrad(name, a, g, a_cols, g_cols, a_blocked, g_blocked, after=()):
    T = a.shape[0]
    tt = T
    while tt > LANES and 2 * 2 * tt * (a_cols + g_cols) + (2 if tt == T else 3) * 4 * a_cols * g_cols > VMEM_BLOCK_BUDGET:
        tt //= 2
    return _matmul(
        name, a, g, dims=TN, grid=(N_CHIP, 1, T // tt),
        a_spec=pl.BlockSpec((tt, a_cols), (lambda j, i, kk: (kk, j)) if a_blocked else (lambda j, i, kk: (kk, 0))),
        b_spec=pl.BlockSpec((tt, g_cols), (lambda j, i, kk: (kk, j)) if g_blocked else (lambda j, i, kk: (kk, 0))),
        out_specs=[pl.BlockSpec((None, a_cols, g_cols), lambda j, i, kk: (j, 0, 0))],
        out_shapes=[jax.ShapeDtypeStruct((N_CHIP, a_cols, g_cols), F32)], acc_shape=(a_cols, g_cols), after=after)[0]


def _rms_fwd(name, x, g, after=()):
    T, Dm = x.shape
    tm = min(256, T)

    def body(x_ref, g_ref, *rest):
        xv = x_ref[...]
        r = lax.rsqrt(jnp.mean(xv * xv, axis=-1, keepdims=True) + NORM_EPS)
        rest[-1][...] = (xv * r * g_ref[...]).astype(BF16)

    return pl.pallas_call(
        body, name=name, grid=(T // tm,),
        in_specs=[pl.BlockSpec((tm, Dm), lambda i: (i, 0)), pl.BlockSpec((1, Dm), lambda i: (0, 0))] + [ANY] * len(after),
        out_specs=pl.BlockSpec((tm, Dm), lambda i: (i, 0)),
        out_shape=jax.ShapeDtypeStruct((T, Dm), BF16), compiler_params=_params(1))(x, g, *after)


def _loss_head(x, g, tgt):
    T, Dm = x.shape
    tm = min(256, T)

    def body(x_ref, g_ref, t_ref, loss_ref, dx_ref, dg_ref):
        xv = x_ref[...]
        gv = g_ref[...]
        r = lax.rsqrt(jnp.mean(xv * xv, axis=-1, keepdims=True) + NORM_EPS)
        xn = xv * r
        err = xn * gv - t_ref[...]
        dy = err * (1.0 / Dm)
        dxn = dy * gv
        dx_ref[...] = r * (dxn - xn * jnp.mean(dxn * xn, axis=-1, keepdims=True))

        @pl.when(pl.program_id(0) == 0)
        def _():
            dg_ref[...] = jnp.zeros_like(dg_ref)
            loss_ref[...] = jnp.zeros_like(loss_ref)

        dg_ref[...] += _row0(jnp.sum(dy * xn, axis=0, keepdims=True))
        part = jnp.sum(jnp.sum(err * err, axis=-1, keepdims=True), axis=0, keepdims=True) * (0.5 / Dm)
        loss_ref[...] += jnp.broadcast_to(part, loss_ref.shape)

    tile = pl.BlockSpec((tm, Dm), lambda i: (i, 0))
    return pl.pallas_call(
        body, name="loss_head", grid=(T // tm,),
        in_specs=[tile, pl.BlockSpec((1, Dm), lambda i: (0, 0)), tile],
        out_specs=[pl.BlockSpec((SUBLANES, LANES), lambda i: (0, 0)), tile,
                   pl.BlockSpec((SUBLANES, Dm), lambda i: (0, 0))],
        out_shape=[jax.ShapeDtypeStruct((SUBLANES, LANES), F32), jax.ShapeDtypeStruct((T, Dm), F32),
                   jax.ShapeDtypeStruct((SUBLANES, Dm), F32)],
        compiler_params=_params(1))(x, g, tgt)


def _softmax_rows(lb_ref):
    rows = [lb_ref[pl.ds(i, 1), :] for i in range(DEPTH)]
    mx = rows[0]
    for r in rows[1:]:
        mx = jnp.maximum(mx, r)
    es = [jnp.exp(r - mx) for r in rows]
    tot = es[0]
    for e in es[1:]:
        tot = tot + e
    return [e / tot for e in es]


def _lbs_fwd(lower_bounds):
    def body(lb_ref, out_ref):
        sm = _softmax_rows(lb_ref)
        run = jnp.zeros_like(sm[0])
        out_ref[pl.ds(0, 1), :] = run
        for i in range(1, DEPTH):
            run = run + sm[i]
            out_ref[pl.ds(i, 1), :] = run

    return pl.pallas_call(body, name="lbs_fwd", out_shape=jax.ShapeDtypeStruct(lower_bounds.shape, F32))(lower_bounds)


def _lbs_bwd(lower_bounds, dlbs):
    def body(lb_ref, d_ref, out_ref):
        sm = _softmax_rows(lb_ref)
        dsm = [jnp.zeros_like(sm[0])]
        for i in range(1, DEPTH):
            acc = d_ref[pl.ds(i, 1), :]
            for l in range(i + 1, DEPTH):
                acc = acc + d_ref[pl.ds(l, 1), :]
            dsm.append(acc)
        inner = dsm[0] * sm[0]
        for i in range(1, DEPTH):
            inner = inner + dsm[i] * sm[i]
        for i in range(DEPTH):
            out_ref[pl.ds(i, 1), :] = sm[i] * (dsm[i] - inner)

    return pl.pallas_call(body, name="lbs_bwd", out_shape=jax.ShapeDtypeStruct(lower_bounds.shape, F32))(lower_bounds, dlbs)


N_LEVEL = 6


def _hgrn_consts():
    L = H_CHUNK
    t = np.arange(L)
    blocks = [(t[:, None] >= t[None, :]).astype(np.float32)]
    masks = []
    m = L // 2
    while m >= 1:
        blk, pos = t // (2 * m), t % (2 * m)
        start = blk * 2 * m
        mat = np.zeros((L, L), np.float32)
        for r in range(L):
            if pos[r] >= m:
                mat[r, start[r] + m:r + 1] = 1.0
            else:
                mat[r, r + 1:start[r] + m] = -1.0
        blocks.append(mat)
        masks.append(((blk[:, None] == blk[None, :]) & (pos[:, None] >= m) & (pos[None, :] < m)).astype(np.float32))
        m //= 2
    blocks.append(np.ones((L, L), np.float32))
    return jnp.asarray(np.concatenate(blocks, 0), BF16), jnp.asarray(np.stack(masks), F32)


def _hgrn_core(qraw, fp, lb, sum_mat, mask_ref):
    L = H_CHUNK
    sq = jax.nn.sigmoid(qraw)
    q = qraw * sq
    sneg = jax.nn.sigmoid(-fp)
    log_sig = jnp.minimum(fp, 0.0) - jnp.log1p(jnp.exp(-jnp.abs(fp)))
    a1 = jnp.log(jnp.maximum(lb, LB_FLOOR))
    a2 = jnp.log1p(-lb) + log_sig
    logf = jnp.maximum(a1, a2) + jnp.log1p(jnp.exp(-jnp.abs(a1 - a2)))
    w1 = jnp.exp(a1 - logf)
    w2 = jnp.exp(a2 - logf)
    k = (1.0 - lb) * sneg
    hi = logf.astype(BF16)
    r1 = logf - hi.astype(F32)
    mid = r1.astype(BF16)
    lo = (r1 - mid.astype(F32)).astype(BF16)
    sums = lax.dot_general(sum_mat, jnp.concatenate([hi, mid, lo], axis=1), NN, preferred_element_type=F32)
    sums = sums[:, 0:HEAD] + sums[:, HEAD:2 * HEAD] + sums[:, 2 * HEAD:3 * HEAD]
    b = sums[0:L]
    b_last = sums[(N_LEVEL + 1) * L:(N_LEVEL + 2) * L]
    eye = lax.broadcasted_iota(jnp.int32, (L, L), 0) == lax.broadcasted_iota(jnp.int32, (L, L), 1)
    attn = jnp.where(eye, jnp.sum(q * k, axis=1, keepdims=True), 0.0)
    fa, fb, ea, eb = [], [], [], []
    for l in range(N_LEVEL):
        d = sums[(l + 1) * L:(l + 2) * L]
        e_a = jnp.exp(jnp.minimum(d, 0.0))
        e_b = jnp.exp(jnp.minimum(-d, 0.0))
        a_l, b_l = q * e_a, k * e_b
        attn = attn + mask_ref[l] * _dot(a_l, b_l, NT)
        fa.append(a_l), fb.append(b_l), ea.append(e_a), eb.append(e_b)
    return dict(sq=sq, q=q, sneg=sneg, logf=logf, w1=w1, w2=w2, k=k, b=b, b_last=b_last, attn=attn,
                fa=fa, fb=fb, ea=ea, eb=eb)


def _hgrn_fwd(p, lbrow, gout):
    T = p.shape[0]
    nch = T // H_CHUNK
    sum_mat, masks = _hgrn_consts()

    def body(p_ref, lb_ref, g_ref, m_ref, mask_ref, o_ref, z_ref, st_ref, state):
        @pl.when(pl.program_id(0) == 0)
        def _():
            state[...] = jnp.zeros_like(state)

        sum_m = m_ref[...]
        for h in range(N_HEAD):
            col = lambda part: pl.ds(part * WIDTH + h * HEAD, HEAD)
            hs = pl.ds(h * HEAD, HEAD)
            v = p_ref[:, col(2)]
            c = _hgrn_core(p_ref[:, col(0)], p_ref[:, col(1)], lb_ref[:, hs], sum_m, mask_ref)
            s0 = state[h]
            st_ref[h] = s0
            o = _dot(c["attn"], v) + _dot(c["q"] * jnp.exp(c["b"]), s0, NT)
            k_dec = c["k"] * jnp.exp(c["b_last"] - c["b"])
            decay = jnp.exp(jnp.max(c["b_last"], axis=0, keepdims=True))
            state[h] = s0 * decay + _dot(v, k_dec, TN)
            o_ref[:, hs] = o
            r = lax.rsqrt(jnp.mean(o * o, axis=-1, keepdims=True) + NORM_EPS)
            z_ref[:, hs] = (o * r * g_ref[:, hs] * jax.nn.sigmoid(p_ref[:, col(3)])).astype(BF16)

    full = lambda shape: pl.BlockSpec(shape, lambda c: (0,) * len(shape))
    return pl.pallas_call(
        body, name="hgrn_fwd", grid=(nch,),
        in_specs=[pl.BlockSpec((H_CHUNK, 4 * WIDTH), lambda c: (c, 0)), full((1, WIDTH)), full((1, WIDTH)),
                  full(sum_mat.shape), full(masks.shape)],
        out_specs=[pl.BlockSpec((H_CHUNK, WIDTH), lambda c: (c, 0)),
                   pl.BlockSpec((None, H_CHUNK, WIDTH), lambda c: (0, c, 0)),
                   pl.BlockSpec((None, N_HEAD, HEAD, HEAD), lambda c: (c, 0, 0, 0))],
        out_shape=[jax.ShapeDtypeStruct((T, WIDTH), F32), jax.ShapeDtypeStruct((N_BRANCH, T, WIDTH), BF16),
                   jax.ShapeDtypeStruct((nch, N_HEAD, HEAD, HEAD), F32)],
        scratch_shapes=[pltpu.VMEM((N_HEAD, HEAD, HEAD), F32)], compiler_params=_params(1),
    )(p, lbrow, gout, sum_mat, masks)


def _hgrn_bwd(p, o_saved, dz, states, lbrow, gout, dp, after=()):
    T = p.shape[0]
    nch = T // H_CHUNK
    L = H_CHUNK
    sum_mat, masks = _hgrn_consts()

    def body(p_ref, o_ref, dz_ref, st_ref, lb_ref, g_ref, m_ref, mask_ref, dp_in, *rest):
        del dp_in
        dp_ref, dlb_ref, dg_ref, dstate = rest[len(after):]

        @pl.when(pl.program_id(0) == 0)
        def _():
            dstate[...] = jnp.zeros_like(dstate)
            dlb_ref[...] = jnp.zeros_like(dlb_ref)
            dg_ref[...] = jnp.zeros_like(dg_ref)

        sum_m = m_ref[...]
        for h in range(N_HEAD):
            col = lambda part: pl.ds(part * WIDTH + h * HEAD, HEAD)
            hs = pl.ds(h * HEAD, HEAD)
            qraw, fp, v, go = p_ref[:, col(0)], p_ref[:, col(1)], p_ref[:, col(2)], p_ref[:, col(3)]
            lb, g = lb_ref[:, hs], g_ref[:, hs]
            c = _hgrn_core(qraw, fp, lb, sum_m, mask_ref)
            q, k, b, b_last = c["q"], c["k"], c["b"], c["b_last"]
            s0, ds1 = st_ref[h], dstate[h]
            e_b = jnp.exp(b)
            q_dec = q * e_b
            e_bl = jnp.exp(b_last - b)
            k_dec = k * e_bl
            decay = jnp.exp(jnp.max(b_last, axis=0, keepdims=True))
            o = o_ref[:, hs]
            r = lax.rsqrt(jnp.mean(o * o, axis=-1, keepdims=True) + NORM_EPS)
            n = o * r
            sgo = jax.nn.sigmoid(go)
            dza = dz_ref[:, hs]
            dgo = dza * n * g * sgo * (1.0 - sgo)
            dg_ref[:, hs] += _row0(jnp.sum(dza * n * sgo, axis=0, keepdims=True))
            dn = dza * g * sgo
            do = r * (dn - n * jnp.mean(dn * n, axis=-1, keepdims=True))
            dattn = _dot(do, v, NT)
            dv = _dot(c["attn"], do, TN) + _dot(k_dec, ds1, NT)
            dq_dec = _dot(do, s0)
            dk_dec = _dot(v, ds1)
            ddiag = jnp.sum(do * v, axis=1, keepdims=True)
            dq = dq_dec * e_b + ddiag * k
            dk = dk_dec * e_bl + ddiag * q
            dsums = [dq_dec * q_dec - dk_dec * k_dec]
            for l in range(N_LEVEL):
                dm = mask_ref[l] * dattn
                da = _dot(dm, c["fb"][l])
                db = _dot(dm, c["fa"][l], TN)
                dq = dq + da * c["ea"][l]
                dk = dk + db * c["eb"][l]
                dsums.append(da * c["fa"][l] - db * c["fb"][l])
            dlast = jnp.sum(ds1 * s0, axis=0, keepdims=True) * decay
            dsums.append(dk_dec * k_dec + _row0(dlast, L))
            dlogf = _dot(sum_m, jnp.concatenate(dsums, axis=0), TN)
            dstate[h] = ds1 * decay + _dot(do, q_dec, TN)
            sq, sneg = c["sq"], c["sneg"]
            dqraw = dq * sq * (1.0 + qraw * (1.0 - sq))
            dfp = dlogf * c["w2"] * sneg - dk * (1.0 - lb) * sneg * (1.0 - sneg)
            inv_lb = jnp.where(lb > LB_FLOOR, 1.0 / jnp.maximum(lb, LB_FLOOR), 0.0)
            dlb_tok = dlogf * (c["w1"] * inv_lb - c["w2"] / (1.0 - lb)) - dk * sneg
            dlb_ref[:, hs] += _row0(jnp.sum(dlb_tok, axis=0, keepdims=True))
            dp_ref[:, col(0)] = dqraw.astype(BF16)
            dp_ref[:, col(1)] = dfp.astype(BF16)
            dp_ref[:, col(2)] = dv.astype(BF16)
            dp_ref[:, col(3)] = dgo.astype(BF16)

    full = lambda shape: pl.BlockSpec(shape, lambda c: (0,) * len(shape))
    rev = lambda c: nch - 1 - c
    return pl.pallas_call(
        body, name="hgrn_bwd", grid=(nch,),
        in_specs=[pl.BlockSpec((L, 4 * WIDTH), lambda c: (rev(c), 0)), pl.BlockSpec((L, WIDTH), lambda c: (rev(c), 0)),
                  pl.BlockSpec((None, L, WIDTH), lambda c: (0, rev(c), 0)),
                  pl.BlockSpec((None, N_HEAD, HEAD, HEAD), lambda c: (rev(c), 0, 0, 0)),
                  full((1, WIDTH)), full((1, WIDTH)), full(sum_mat.shape), full(masks.shape), ANY, *[ANY] * len(after)],
        out_specs=[pl.BlockSpec((L, 4 * WIDTH), lambda c: (rev(c), 0)), full((SUBLANES, WIDTH)), full((SUBLANES, WIDTH))],
        out_shape=[jax.ShapeDtypeStruct(dp.shape, dp.dtype), jax.ShapeDtypeStruct((SUBLANES, WIDTH), F32),
                   jax.ShapeDtypeStruct((SUBLANES, WIDTH), F32)],
        scratch_shapes=[pltpu.VMEM((N_HEAD, HEAD, HEAD), F32)], input_output_aliases={8: 0},
        compiler_params=_params(1),
    )(p, o_saved, dz, states, lbrow, gout, sum_mat, masks, dp, *after)


def _shift_down(tile, halo, s):
    tm = tile.shape[0]
    rows = lax.broadcasted_iota(jnp.int32, tile.shape, 0)
    head = jnp.concatenate([pltpu.roll(halo, s, 0), jnp.zeros((tm - SUBLANES, tile.shape[1]), tile.dtype)], axis=0)
    return jnp.where(rows < s, head, pltpu.roll(tile, s, 0))


def _shift_up(tile, halo, s):
    tm = tile.shape[0]
    rows = lax.broadcasted_iota(jnp.int32, tile.shape, 0)
    tail = jnp.concatenate([jnp.zeros((tm - SUBLANES, tile.shape[1]), tile.dtype), pltpu.roll(halo, SUBLANES - s, 0)], axis=0)
    return jnp.where(rows >= tm - s, tail, pltpu.roll(tile, tm - s, 0))


def _conv_fwd(p, w, z, after=()):
    T = p.shape[0]
    tm = _token_tile(T)
    per = tm // SUBLANES

    def body(bg_ref, cg_ref, xc_ref, hcg_ref, hxc_ref, w_ref, *rest):
        z_ref = rest[-1]
        zc = cg_ref[...] * xc_ref[...]
        hz = jnp.where(pl.program_id(0) > 0, hcg_ref[...] * hxc_ref[...], 0.0)
        y = (w_ref[pl.ds(0, 1), :] * _shift_down(zc, hz, 2) + w_ref[pl.ds(1, 1), :] * _shift_down(zc, hz, 1)
             + w_ref[pl.ds(2, 1), :] * zc)
        z_ref[...] = (bg_ref[...] * y).astype(BF16)

    tile = lambda cb: pl.BlockSpec((tm, WIDTH), lambda i: (i, cb))
    prev = lambda cb: pl.BlockSpec((SUBLANES, WIDTH), lambda i: (jnp.maximum(i * per - 1, 0), cb))
    return pl.pallas_call(
        body, name="conv_fwd", grid=(T // tm,),
        in_specs=[tile(4), tile(5), tile(6), prev(5), prev(6), pl.BlockSpec((CONV_K, WIDTH), lambda i: (0, 0)), ANY,
                  *[ANY] * len(after)],
        out_specs=pl.BlockSpec((None, tm, WIDTH), lambda i: (1, i, 0)),
        out_shape=jax.ShapeDtypeStruct(z.shape, z.dtype), input_output_aliases={6: 0}, compiler_params=_params(1),
    )(p, p, p, p, p, w, z, *after)


def _conv_bwd(p, w, dz, dp):
    T = p.shape[0]
    tm = _token_tile(T)
    per = tm // SUBLANES
    last = T // SUBLANES - 1

    def body(bg_ref, cg_ref, xc_ref, hcg_ref, hxc_ref, nbg_ref, dzb_ref, ndzb_ref, w_ref, dp_in, dp_ref, dw_ref, stash):
        del dp_in
        i, jj = pl.program_id(0), pl.program_id(1)

        @pl.when(jnp.logical_and(i == 0, jj == 0))
        def _():
            dw_ref[...] = jnp.zeros_like(dw_ref)

        @pl.when(jj == 0)
        def _():
            cg, xc, bg = cg_ref[...], xc_ref[...], bg_ref[...]
            w0, w1, w2 = w_ref[pl.ds(0, 1), :], w_ref[pl.ds(1, 1), :], w_ref[pl.ds(2, 1), :]
            zc = cg * xc
            hz = jnp.where(i > 0, hcg_ref[...] * hxc_ref[...], 0.0)
            z2, z1 = _shift_down(zc, hz, 2), _shift_down(zc, hz, 1)
            y = w0 * z2 + w1 * z1 + w2 * zc
            dzb = dzb_ref[...]
            dy = dzb * bg
            hdy = jnp.where(i < pl.num_programs(0) - 1, ndzb_ref[...] * nbg_ref[...], 0.0)
            dzc = w2 * dy + w1 * _shift_up(dy, hdy, 1) + w0 * _shift_up(dy, hdy, 2)
            rows = lax.broadcasted_iota(jnp.int32, (SUBLANES, WIDTH), 0)
            colsum = lambda t: jnp.sum(t, axis=0, keepdims=True)
            dw_ref[...] += (jnp.where(rows == 0, colsum(dy * z2), 0.0) + jnp.where(rows == 1, colsum(dy * z1), 0.0)
                            + jnp.where(rows == 2, colsum(dy * zc), 0.0))
            dp_ref[...] = (dzb * y).astype(BF16)
            stash[0] = dzc * xc
            stash[1] = dzc * cg

        @pl.when(jj > 0)
        def _():
            dp_ref[...] = stash[jj - 1].astype(BF16)

    n_tiles = T // tm
    tile = lambda cb: pl.BlockSpec((tm, WIDTH), lambda i, jj: (i, cb))
    prev = lambda cb: pl.BlockSpec((SUBLANES, WIDTH), lambda i, jj: (jnp.maximum(i * per - 1, 0), cb))
    nxt = lambda i: jnp.minimum((i + 1) * per, last)
    return pl.pallas_call(
        body, name="conv_bwd", grid=(n_tiles, 3),
        in_specs=[tile(4), tile(5), tile(6), prev(5), prev(6),
                  pl.BlockSpec((SUBLANES, WIDTH), lambda i, jj: (nxt(i), 4)),
                  pl.BlockSpec((None, tm, WIDTH), lambda i, jj: (1, i, 0)),
                  pl.BlockSpec((None, SUBLANES, WIDTH), lambda i, jj: (1, nxt(i), 0)),
                  pl.BlockSpec((CONV_K, WIDTH), lambda i, jj: (0, 0)), ANY],
        out_specs=[pl.BlockSpec((tm, WIDTH), lambda i, jj: (i, 4 + jj)),
                   pl.BlockSpec((SUBLANES, WIDTH), lambda i, jj: (0, 0))],
        out_shape=[jax.ShapeDtypeStruct(dp.shape, dp.dtype), jax.ShapeDtypeStruct((SUBLANES, WIDTH), F32)],
        scratch_shapes=[pltpu.VMEM((2, tm, WIDTH), F32)], input_output_aliases={9: 0}, compiler_params=_params(2),
    )(p, p, p, p, p, p, dz, dz, w, dp)


GELU_C = float(np.sqrt(2.0 / np.pi))
GELU_A = 0.044715


def _gelu(x):
    th = jnp.tanh(GELU_C * (x + GELU_A * x * x * x))
    return 0.5 * x * (1.0 + th), th


def _gelu_grad(x, th):
    return 0.5 * (1.0 + th) + 0.5 * x * (1.0 - th * th) * GELU_C * (1.0 + 3.0 * GELU_A * x * x)


def _sg_core(u, v, lng, lnb, ws_ref, bs_ref):
    gu, thu = _gelu(u)
    gv, thv = _gelu(v)
    xc = gv - jnp.mean(gv, axis=-1, keepdims=True)
    rs = lax.rsqrt(jnp.mean(xc * xc, axis=-1, keepdims=True) + LN_EPS)
    xh = xc * rs
    vp = xh * lng + lnb
    tril = (lax.broadcasted_iota(jnp.int32, (SG_CHUNK, SG_CHUNK), 0)
            >= lax.broadcasted_iota(jnp.int32, (SG_CHUNK, SG_CHUNK), 1))
    wm = [jnp.where(tril, ws_ref[g], 0.0).astype(BF16) for g in range(SG_GROUPS)]
    gs = lambda t, g: t[:, g * LANES:(g + 1) * LANES]
    sv = jnp.concatenate([_dot(wm[g], gs(vp, g)) + bs_ref[g] for g in range(SG_GROUPS)], axis=1)
    return dict(gu=gu, thu=thu, thv=thv, rs=rs, xh=xh, vp=vp, tril=tril, wm=wm, sv=sv)


def _sg_fwd(p, lng, lnb, ws, bs, z):
    T = p.shape[0]

    def body(u_ref, v_ref, lng_ref, lnb_ref, ws_ref, bs_ref, z_in, z_ref):
        del z_in
        c = _sg_core(u_ref[...], v_ref[...], lng_ref[...], lnb_ref[...], ws_ref, bs_ref)
        z_ref[...] = (c["gu"] * c["sv"]).astype(BF16)

    full = lambda shape: pl.BlockSpec(shape, lambda c: (0,) * len(shape))
    return pl.pallas_call(
        body, name="sg_fwd", grid=(T // SG_CHUNK,),
        in_specs=[pl.BlockSpec((SG_CHUNK, WIDTH), lambda c: (c, 7)), pl.BlockSpec((SG_CHUNK, WIDTH), lambda c: (c, 8)),
                  full((1, WIDTH)), full((1, WIDTH)), full(ws.shape), full(bs.shape), ANY],
        out_specs=pl.BlockSpec((None, SG_CHUNK, WIDTH), lambda c: (2, c, 0)),
        out_shape=jax.ShapeDtypeStruct(z.shape, z.dtype), input_output_aliases={6: 0}, compiler_params=_params(1),
    )(p, p, lng, lnb, ws, bs, z)


def _sg_bwd(p, lng, lnb, ws, bs, dz, dp):
    T = p.shape[0]

    def body(u_ref, v_ref, lng_ref, lnb_ref, ws_ref, bs_ref, dz_ref, dp_in, dp_ref, dws_ref, dbs_ref, dlng_ref, dlnb_ref,
             stash):
        del dp_in
        cidx, jj = pl.program_id(0), pl.program_id(1)

        @pl.when(jnp.logical_and(cidx == 0, jj == 0))
        def _():
            dws_ref[...] = jnp.zeros_like(dws_ref)
            dbs_ref[...] = jnp.zeros_like(dbs_ref)
            dlng_ref[...] = jnp.zeros_like(dlng_ref)
            dlnb_ref[...] = jnp.zeros_like(dlnb_ref)

        @pl.when(jj == 0)
        def _():
            u, v, lng = u_ref[...], v_ref[...], lng_ref[...]
            c = _sg_core(u, v, lng, lnb_ref[...], ws_ref, bs_ref)
            dzc = dz_ref[...]
            gs = lambda t, g: t[:, g * LANES:(g + 1) * LANES]
            dsv = dzc * c["gu"]
            dvp = []
            for g in range(SG_GROUPS):
                dsv_g = gs(dsv, g)
                dws_ref[g] += jnp.where(c["tril"], _dot(dsv_g, gs(c["vp"], g), NT), 0.0)
                dbs_ref[g] += jnp.sum(dsv_g, axis=1, keepdims=True)
                dvp.append(_dot(c["wm"][g], dsv_g, TN))
            dvp = jnp.concatenate(dvp, axis=1)
            xh = c["xh"]
            dlng_ref[...] += _row0(jnp.sum(dvp * xh, axis=0, keepdims=True))
            dlnb_ref[...] += _row0(jnp.sum(dvp, axis=0, keepdims=True))
            dxh = dvp * lng
            dgv = c["rs"] * (dxh - jnp.mean(dxh, axis=-1, keepdims=True) - xh * jnp.mean(dxh * xh, axis=-1, keepdims=True))
            dp_ref[...] = (dzc * c["sv"] * _gelu_grad(u, c["thu"])).astype(BF16)
            stash[...] = dgv * _gelu_grad(v, c["thv"])

        @pl.when(jj == 1)
        def _():
            dp_ref[...] = stash[...].astype(BF16)

    full = lambda shape: pl.BlockSpec(shape, lambda c, jj: (0,) * len(shape))
    return pl.pallas_call(
        body, name="sg_bwd", grid=(T // SG_CHUNK, 2),
        in_specs=[pl.BlockSpec((SG_CHUNK, WIDTH), lambda c, jj: (c, 7)), pl.BlockSpec((SG_CHUNK, WIDTH), lambda c, jj: (c, 8)),
                  full((1, WIDTH)), full((1, WIDTH)), full(ws.shape), full(bs.shape),
                  pl.BlockSpec((None, SG_CHUNK, WIDTH), lambda c, jj: (2, c, 0)), ANY],
        out_specs=[pl.BlockSpec((SG_CHUNK, WIDTH), lambda c, jj: (c, 7 + jj)), full(ws.shape), full(bs.shape),
                   full((SUBLANES, WIDTH)), full((SUBLANES, WIDTH))],
        out_shape=[jax.ShapeDtypeStruct(dp.shape, dp.dtype), jax.ShapeDtypeStruct(ws.shape, F32),
                   jax.ShapeDtypeStruct(bs.shape, F32), jax.ShapeDtypeStruct((SUBLANES, WIDTH), F32),
                   jax.ShapeDtypeStruct((SUBLANES, WIDTH), F32)],
        scratch_shapes=[pltpu.VMEM((SG_CHUNK, WIDTH), F32)], input_output_aliases={7: 0}, compiler_params=_params(2),
    )(p, p, lng, lnb, ws, bs, dz, dp)


BRANCH_COLS = D_MODEL // N_CHIP
GATE_UNIT0 = GATE_COL0 // WIDTH
UNITS = D_MODEL // WIDTH


def _unit_specs(order):
    def spec(which):
        def index(*g):
            _, n, u = order(*g)
            return (2 * u + which, n, 0, 0)
        return pl.BlockSpec((None, None, WIDTH, BRANCH_COLS), index)
    return [spec(0), spec(1)]


def _merge_fwd(z, p, wb):
    T = z.shape[1]
    tm = min(2 * _token_tile(T), T)
    order = lambda i, u, n: (i, n, u)

    def body(z_ref, wa_ref, wb_ref, gt_ref, out_ref, acc):
        n = pl.program_id(2)
        zv = z_ref[...]
        y = jnp.concatenate([_dot(zv, wa_ref[...]), _dot(zv, wb_ref[...])], axis=1)
        part = jax.nn.sigmoid(gt_ref[...]) * y

        @pl.when(n == 0)
        def _():
            acc[...] = part

        @pl.when(n > 0)
        def _():
            acc[...] += part

        @pl.when(n == N_BRANCH - 1)
        def _():
            out_ref[...] = acc[...].astype(BF16)

    return pl.pallas_call(
        body, name="merge_fwd", grid=(T // tm, UNITS, N_BRANCH),
        in_specs=[pl.BlockSpec((None, tm, WIDTH), lambda i, u, n: (n, i, 0)), *_unit_specs(order),
                  pl.BlockSpec((tm, WIDTH), lambda i, u, n: (i, GATE_UNIT0 + UNITS * n + u))],
        out_specs=pl.BlockSpec((tm, WIDTH), lambda i, u, n: (i, u)),
        out_shape=jax.ShapeDtypeStruct((T, D_MODEL), BF16),
        scratch_shapes=[pltpu.VMEM((tm, WIDTH), F32)], compiler_params=_params(3))(z, wb, wb, p)


def _merge_bwd(z, p, wb, dmerged):
    T = z.shape[1]
    tm = min(2 * _token_tile(T), T)
    order = lambda n, u, i: (i, n, u)

    def body(z_ref, wa_ref, wb_ref, gt_ref, dm_ref, dp_ref, dw_ref, dz_ref):
        u, i = pl.program_id(1), pl.program_id(2)
        zv, wa, wbv = z_ref[...], wa_ref[...], wb_ref[...]
        y = jnp.concatenate([_dot(zv, wa), _dot(zv, wbv)], axis=1)
        gate = jax.nn.sigmoid(gt_ref[...])
        dm = dm_ref[...]
        dp_ref[...] = (dm * y * gate * (1.0 - gate)).astype(BF16)
        dyv = (dm * gate).astype(BF16)
        dw = _dot(zv, dyv, TN)
        part = _dot(dyv[:, :BRANCH_COLS], wa, NT) + _dot(dyv[:, BRANCH_COLS:], wbv, NT)
        rows = pl.ds(pl.multiple_of(i * tm, tm), tm)

        @pl.when(i == 0)
        def _():
            dw_ref[0] = dw[:, :BRANCH_COLS]
            dw_ref[1] = dw[:, BRANCH_COLS:]

        @pl.when(i > 0)
        def _():
            dw_ref[0] += dw[:, :BRANCH_COLS]
            dw_ref[1] += dw[:, BRANCH_COLS:]

        @pl.when(u == 0)
        def _():
            dz_ref[rows, :] = part

        @pl.when(u > 0)
        def _():
            dz_ref[rows, :] += part

    unit = lambda n, u, i: (i, GATE_UNIT0 + UNITS * n + u)
    return pl.pallas_call(
        body, name="merge_bwd", grid=(N_BRANCH, UNITS, T // tm),
        in_specs=[pl.BlockSpec((None, tm, WIDTH), lambda n, u, i: (n, i, 0)), *_unit_specs(order),
                  pl.BlockSpec((tm, WIDTH), unit), pl.BlockSpec((tm, WIDTH), lambda n, u, i: (i, u))],
        out_specs=[pl.BlockSpec((tm, WIDTH), unit),
                   pl.BlockSpec((2, None, WIDTH, BRANCH_COLS), lambda n, u, i: (u, n, 0, 0)),
                   pl.BlockSpec((None, T, WIDTH), lambda n, u, i: (n, 0, 0))],
        out_shape=[jax.ShapeDtypeStruct((T, IN_COLS), BF16),
                   jax.ShapeDtypeStruct((N_CHIP, N_BRANCH, WIDTH, BRANCH_COLS), F32),
                   jax.ShapeDtypeStruct((N_BRANCH, T, WIDTH), F32)],
        compiler_params=_params(3))(z, wb, wb, p, dmerged)


def _layer_fwd(x, h, win, small, next_gain, hooks):
    p = _mm_cols("in_proj", h, win, [F32])[0]
    o_hgrn, z, states = _hgrn_fwd(p, small["lbs"], small["g_hgrn_out"])
    z = _conv_fwd(p, small["w_conv"], z, after=hooks["after_hgrn"]([o_hgrn]))
    z = _sg_fwd(p, small["sg_ln_g"], small["sg_ln_b"], small["w_sg"], small["b_sg"], z)
    wb, wo, w1, w2 = hooks["late_weights"]([z])
    wb = wb.reshape(N_CHIP, N_BRANCH, WIDTH, BRANCH_COLS)
    merged = _merge_fwd(z, p, wb)
    x_mid, h2 = _mm_rows("out_proj", merged, wo, x, small["g_ffn"])
    s = _mm_cols("ff1", h2, w1, [BF16], epilogue=lambda acc: (jnp.square(jnp.maximum(acc, 0.0)),))[0]
    if next_gain is None:
        x_out, h_next = _mm_rows("ff2_last", s, w2, x_mid, after=hooks["before_last"]([s])), None
    else:
        x_out, h_next = _mm_rows("ff2", s, w2, x_mid, next_gain, after=hooks["before_last"]([s]))
    saved = dict(x=x, h=h, p=p, o_hgrn=o_hgrn, z=z, states=states, merged=merged, x_mid=x_mid, h2=h2, s=s)
    return x_out, h_next, saved, [win, wb, wo, w1, w2]


def _layer_bwd(dx_out, sv, wts, small, tick, after):
    win, wb, wo, w1, w2 = wts
    g = {}
    da = _mm_cols_t("ff2_dgrad", dx_out, w2, BF16, extra=(sv["s"],), after=after,
                    epilogue=lambda acc, s: (acc * 2.0 * jnp.sqrt(s.astype(F32)),))
    d_ff2 = _mm_wgrad("ff2_wgrad", sv["s"], dx_out, w2.shape[1], D_MODEL, True, False)
    d_ff1 = _mm_wgrad("ff1_wgrad", sv["h2"], da, D_MODEL, w1.shape[2], False, True)
    dx_mid, g["g_ffn"] = _dgrad_norm_bwd("ff1_dgrad", da, w1, sv["x_mid"], small["g_ffn"], dx_out)
    after = tick([dx_mid], [("w_ff1", d_ff1), ("w_ff2", d_ff2)])
    dmerged = _mm_cols_t("out_proj_dgrad", dx_mid, wo, F32, after=after)
    d_o = _mm_wgrad("out_proj_wgrad", sv["merged"], dx_mid, wo.shape[1], D_MODEL, True, False)
    dp, d_branch, dz = _merge_bwd(sv["z"], sv["p"], wb, dmerged)
    d_branch = d_branch.reshape(N_CHIP, N_BRANCH * WIDTH, BRANCH_COLS)
    dp, g["w_conv"] = _conv_bwd(sv["p"], small["w_conv"], dz, dp)
    dp, g["w_sg"], g["b_sg"], g["sg_ln_g"], g["sg_ln_b"] = _sg_bwd(
        sv["p"], small["sg_ln_g"], small["sg_ln_b"], small["w_sg"], small["b_sg"], dz, dp)
    after = tick([dp], [("w_branch", d_branch), ("w_o", d_o)])
    dp, g["lbs"], g["g_hgrn_out"] = _hgrn_bwd(sv["p"], sv["o_hgrn"], dz, sv["states"], small["lbs"],
                                              small["g_hgrn_out"], dp, after=after)
    after = tick([dp], [])
    d_in = _mm_wgrad("in_proj_wgrad", sv["h"], dp, D_MODEL, win.shape[2], False, True, after=after)
    dx, g["g_mix"] = _dgrad_norm_bwd("in_proj_dgrad", dp, win, sv["x"], small["g_mix"], dx_mid)
    return dx, g, tick([dx], [("w_in", d_in)])


def _mesh_pos():
    return lax.axis_index("x"), lax.axis_index("y"), lax.axis_index("c")


def _other_chips(x, y):
    return [(1 - x, y), (x, 1 - y), (1 - x, 1 - y)]


def _remote(src, dst, send_sems, recv_sems, k, to):
    return pltpu.make_async_remote_copy(src_ref=src, dst_ref=dst, send_sem=send_sems.at[k], recv_sem=recv_sems.at[k],
                                        device_id=to, device_id_type=MESH)


def _gather_call(name, body, buf, after):
    scratch = [pltpu.SemaphoreType.DMA((7,)), pltpu.SemaphoreType.DMA((7,))]
    return pl.pallas_call(
        body, name=name, in_specs=[ANY] * (1 + len(after)), out_specs=ANY,
        out_shape=jax.ShapeDtypeStruct(buf.shape, buf.dtype), scratch_shapes=scratch, input_output_aliases={0: 0})(buf, *after)


HBM = pl.BlockSpec(memory_space=pltpu.HBM)
SEM = pl.BlockSpec(memory_space=pltpu.SEMAPHORE)
DATAFLOW = pltpu.SideEffectType.DATAFLOW_SIDE_EFFECTING


def _split_start(name, bufs, copies, n_copies, after=()):
    n = len(bufs)

    def body(*refs):
        send_sems, recv_sems = refs[n + len(after)], refs[n + len(after) + 1]
        for cp in copies(refs[:n], send_sems, recv_sems):
            cp.start()
        refs[-1][...] = jnp.zeros_like(refs[-1])

    outs = pl.pallas_call(
        body, name=name,
        out_shape=(pltpu.SemaphoreType.DMA((n_copies,)), pltpu.SemaphoreType.DMA((n_copies,)),
                   *[pltpu.HBM(b.shape, b.dtype) for b in bufs], jax.ShapeDtypeStruct((SUBLANES, LANES), F32)),
        in_specs=[HBM] * n + [ANY] * len(after),
        out_specs=(SEM, SEM, *[HBM] * n, pl.BlockSpec(memory_space=pltpu.VMEM)),
        input_output_aliases={t: 2 + t for t in range(n)},
        compiler_params=pltpu.CompilerParams(has_side_effects=DATAFLOW),
    )(*[pltpu.with_memory_space_constraint(b, pltpu.HBM) for b in bufs], *after)
    return outs[0], outs[1], list(outs[2:2 + n]), outs[-1]


def _split_wait(name, started, copies, after):
    send_sems, recv_sems, bufs, _ = started
    n = len(bufs)

    def body(*refs):
        for cp in copies(refs[:n], refs[n], refs[n + 1]):
            cp.wait_send()
            cp.wait_recv()

    return list(pl.pallas_call(
        body, name=name, out_shape=tuple(pltpu.HBM(b.shape, b.dtype) for b in bufs),
        in_specs=[HBM] * n + [SEM, SEM] + [ANY] * len(after), out_specs=tuple([HBM] * n),
        input_output_aliases={t: t for t in range(n)},
        compiler_params=pltpu.CompilerParams(has_side_effects=DATAFLOW),
    )(*bufs, send_sems, recv_sems, *after))


def _weight_ici_copies(refs, send_sems, recv_sems):
    x, y, c = _mesh_pos()
    out = []
    for t, ref in enumerate(refs):
        rh = ref.shape[1] // 2
        mine = ref.at[2 * x + y, pl.ds(c * rh, rh), :]
        out += [_remote(mine, mine, send_sems, recv_sems, 3 * t + j, (*chip, c)) for j, chip in enumerate(_other_chips(x, y))]
    return out


def _weight_d2d_copies(refs, send_sems, recv_sems):
    x, y, c = _mesh_pos()
    out = []
    for t, ref in enumerate(refs):
        rh = ref.shape[1] // 2
        for j, chip in enumerate(_other_chips(x, y)):
            blk = ref.at[2 * chip[0] + chip[1], pl.ds(c * rh, rh), :]
            out.append(_remote(blk, blk, send_sems, recv_sems, 3 * t + j, (x, y, 1 - c)))
    return out


def _swap_part(refs, send_sems, recv_sems, s0):
    x, y, c = _mesh_pos()
    n = len(refs) // 2
    out = []
    for t in range(n):
        rh = refs[t].shape[1] // 2
        out.append(_remote(refs[t].at[:, pl.ds((1 - c) * rh, rh), :], refs[n + t], send_sems, recv_sems, s0 + t, (x, y, 1 - c)))
    return out


def _exchange_part(refs, send_sems, recv_sems, s0):
    x, y, c = _mesh_pos()
    n = len(refs) // 2
    out = []
    for t in range(n):
        for j, chip in enumerate(_other_chips(x, y)):
            out.append(_remote(refs[t].at[2 * chip[0] + chip[1]], refs[n + t].at[j], send_sems, recv_sems, s0 + 3 * t + j,
                               (*chip, c)))
    return out


def _gather_part(refs, send_sems, recv_sems, s0):
    x, y, c = _mesh_pos()
    return [_remote(ref.at[c], ref.at[c], send_sems, recv_sems, s0 + t, (x, y, 1 - c)) for t, ref in enumerate(refs)]


def _all_to_all_copies(refs, send_sems, recv_sems):
    x, y, c = _mesh_pos()
    blk = refs[0].at[4 * x + 2 * y + c]
    peers = [(x, y, 1 - c)] + [(*chip, cc) for chip in _other_chips(x, y) for cc in (c, 1 - c)]
    return [_remote(blk, blk, send_sems, recv_sems, k, peer) for k, peer in enumerate(peers)]


class _GradPipeline:
    def __init__(self, pos):
        self.pos = pos
        self.groups, self.pending, self.count = [], None, 0
        self.reduced = {n: [None] * DEPTH for n in BIG_NAMES}

    def busy(self):
        return bool(self.groups) or self.pending is not None

    def tick(self, deps, new):
        if self.pending is not None:
            started, copies, owners = self.pending
            bufs = _split_wait("grad_pipe_wait_%d" % self.count, started, copies, after=list(deps))
            for grp, lo, hi in owners:
                grp["bufs"] = bufs[lo:hi]
            self.pending = None
        parts = []
        for grp in list(self.groups):
            n, names = len(grp["names"]), grp["names"]
            if grp["stage"] == "swap":
                pair = [_pair_sum("grad_pair_sum_" + nm, f, r, self.pos)
                        for nm, f, r in zip(names, grp["bufs"][:n], grp["bufs"][n:])]
                grp["own32"] = [p32 for p32, _ in pair]
                landing = [lax.empty((3, *p16.shape[1:]), BF16) for _, p16 in pair]
                grp["stage"] = "exchange"
                parts.append((grp, [p16 for _, p16 in pair] + landing, _exchange_part, 3 * n))
            elif grp["stage"] == "exchange":
                halves = [_chip_sum("grad_chip_sum_" + nm, p32, r, self.pos)
                          for nm, p32, r in zip(names, grp["own32"], grp["bufs"][n:])]
                grp["stage"] = "gather"
                parts.append((grp, halves, _gather_part, n))
            else:
                for nm, b in zip(names, grp["bufs"]):
                    self.reduced[nm][grp["layer"]] = b.reshape(-1, b.shape[-1])
                self.groups.remove(grp)
        if new:
            grp = dict(names=[nm for nm, _, _ in new], layer=new[0][1], stage="swap")
            self.groups.append(grp)
            fulls = [g for _, _, g in new]
            landing = [lax.empty((N_CHIP, g.shape[1] // 2, g.shape[2]), F32) for g in fulls]
            parts.append((grp, fulls + landing, _swap_part, len(fulls)))
        if not parts:
            return ()
        bufs, layout, owners, sems = [], [], [], 0
        for grp, part_bufs, fn, n_sems in parts:
            layout.append((len(bufs), len(bufs) + len(part_bufs), fn, sems))
            owners.append((grp, len(bufs), len(bufs) + len(part_bufs)))
            bufs += part_bufs
            sems += n_sems

        def copies(refs, send_sems, recv_sems):
            out = []
            for lo, hi, fn, s0 in layout:
                out += fn(refs[lo:hi], send_sems, recv_sems, s0)
            return out

        started = _split_start("grad_pipe_start_%d" % self.count, bufs, copies, sems)
        self.pending = (started, copies, owners)
        self.count += 1
        return (started[3],)


def _gather_all(name, block, slot, after=()):
    buf = lax.dynamic_update_slice(jnp.zeros((8, *block.shape), block.dtype), block[None], (slot, 0, 0))

    def body(*refs):
        out_ref, send_sems, recv_sems = refs[1 + len(after):]
        x, y, c = _mesh_pos()
        chips = _other_chips(x, y)
        sibling = (x, y, 1 - c)
        slot_of = lambda px, py, pc: out_ref.at[4 * px + 2 * py + pc]
        started = [_remote(slot_of(x, y, c), slot_of(x, y, c), send_sems, recv_sems, 0, sibling)]
        started += [_remote(slot_of(x, y, c), slot_of(x, y, c), send_sems, recv_sems, 1 + j, (*chip, c))
                    for j, chip in enumerate(chips)]
        for cp in started:
            cp.start()
        for j, chip in enumerate(chips):
            _remote(slot_of(*chip, c), slot_of(*chip, c), send_sems, recv_sems, 1 + j, (*chip, c)).wait_recv()
            fw = _remote(slot_of(*chip, c), slot_of(*chip, c), send_sems, recv_sems, 4 + j, sibling)
            fw.start()
            started.append(fw)
        _remote(slot_of(x, y, 1 - c), slot_of(x, y, 1 - c), send_sems, recv_sems, 0, sibling).wait_recv()
        for j, chip in enumerate(chips):
            _remote(slot_of(*chip, 1 - c), slot_of(*chip, 1 - c), send_sems, recv_sems, 4 + j, sibling).wait_recv()
        for cp in started:
            cp.wait_send()

    return _gather_call(name, body, buf, after)


def _row_tile(rows, cols):
    cap = max(SUBLANES, ELEMWISE_BLOCK_BYTES // (4 * cols))
    tr = rows
    while tr > cap and tr % 2 == 0:
        tr //= 2
    return tr


def _pair_sum(name, grad, recv, pos):
    _, rh, cols = recv.shape
    tr = _row_tile(rh, cols)
    per = rh // tr

    def body(pos_ref, g_ref, r_ref, own_ref, out16_ref):
        s = g_ref[...] + r_ref[...]
        out16_ref[...] = s.astype(BF16)

        @pl.when(pl.program_id(1) == pos_ref[0])
        def _():
            own_ref[...] = s

    blk = pl.BlockSpec((None, tr, cols), lambda i, k, pos_ref: (k, i, 0))
    return pl.pallas_call(
        body, name=name,
        grid_spec=pltpu.PrefetchScalarGridSpec(
            num_scalar_prefetch=1, grid=(per, N_CHIP),
            in_specs=[pl.BlockSpec((None, tr, cols), lambda i, k, pos_ref: (k, pos_ref[1] * per + i, 0)), blk],
            out_specs=[pl.BlockSpec((tr, cols), lambda i, k, pos_ref: (i, 0)), blk]),
        out_shape=[jax.ShapeDtypeStruct((rh, cols), F32), jax.ShapeDtypeStruct(recv.shape, BF16)],
        compiler_params=_params(2))(pos, grad, recv)


def _chip_sum(name, own32, recv, pos):
    rh, cols = own32.shape
    tr = _row_tile(rh, cols)

    def body(pos_ref, own_ref, r_ref, out_ref):
        del pos_ref
        out_ref[...] = ((own_ref[...] + r_ref[0].astype(F32)) + r_ref[1].astype(F32)) + r_ref[2].astype(F32)

    return pl.pallas_call(
        body, name=name,
        grid_spec=pltpu.PrefetchScalarGridSpec(
            num_scalar_prefetch=1, grid=(rh // tr,),
            in_specs=[pl.BlockSpec((tr, cols), lambda i, pos_ref: (i, 0)),
                      pl.BlockSpec((3, tr, cols), lambda i, pos_ref: (0, i, 0))],
            out_specs=pl.BlockSpec((None, tr, cols), lambda i, pos_ref: (pos_ref[1], i, 0))),
        out_shape=jax.ShapeDtypeStruct((2, rh, cols), F32), compiler_params=_params(1))(pos, own32, recv)


def _cast_into_slot(name, w, layer, pos, after=()):
    _, rows, cols = w.shape
    tr = _row_tile(rows, cols)

    def body(pos_ref, w_ref, *rest):
        del pos_ref
        rest[-1][...] = w_ref[...].astype(BF16)

    return pl.pallas_call(
        body, name=name,
        grid_spec=pltpu.PrefetchScalarGridSpec(
            num_scalar_prefetch=1, grid=(rows // tr,),
            in_specs=[pl.BlockSpec((None, tr, cols), lambda i, pos_ref: (layer, i, 0))] + [ANY] * len(after),
            out_specs=pl.BlockSpec((None, tr, cols), lambda i, pos_ref: (pos_ref[0], i, 0))),
        out_shape=jax.ShapeDtypeStruct((N_CHIP, rows, cols), BF16), compiler_params=_params(1))(pos, w, *after)


def _adamw_math(w, g, m, v):
    m = ADAM_B1 * m + (1.0 - ADAM_B1) * g
    v = ADAM_B2 * v + (1.0 - ADAM_B2) * jnp.square(g)
    m_hat = m / (1.0 - ADAM_B1 ** ADAM_STEP)
    v_hat = v / (1.0 - ADAM_B2 ** ADAM_STEP)
    delta = -ADAM_LR * (m_hat / (jnp.sqrt(v_hat) + ADAM_EPS) + ADAM_WD * w)
    return delta, m, v


def _adamw_layers(name, w, m, v, grads, first, into=None, after=()):
    _, rows, cols = w.shape
    tr = _row_tile(rows, cols)
    n_layers = len(grads)

    def body(w_ref, m_ref, v_ref, *rest):
        g_refs, (grad_ref, d_ref, nm_ref, nv_ref) = rest[:n_layers], rest[len(rest) - 4:]
        layer = pl.program_id(0)
        g = g_refs[0][...]
        for l in range(1, n_layers):
            g = jnp.where(layer == l, g_refs[l][...], g)
        grad_ref[...] = g
        d_ref[...], nm_ref[...], nv_ref[...] = _adamw_math(w_ref[...], g, m_ref[...], v_ref[...])

    blk = pl.BlockSpec((None, tr, cols), lambda l, i: (first + l, i, 0))
    g_spec = lambda k: pl.BlockSpec((tr, cols), lambda l, i: (jnp.where(l == k, i, 0), 0))
    passed = list(into or []) + list(after)
    return pl.pallas_call(
        body, name=name, grid=(n_layers, rows // tr),
        in_specs=[blk, blk, blk] + [g_spec(k) for k in range(n_layers)] + [ANY] * len(passed), out_specs=[blk] * 4,
        out_shape=[jax.ShapeDtypeStruct(w.shape, F32)] * 4,
        input_output_aliases={3 + n_layers + t: t for t in range(4)} if into else {},
        compiler_params=_params(2))(w, m, v, *grads, *passed)


def _sum_devices(gathered):
    _, rows, cols = gathered.shape

    def body(g_ref, out_ref):
        s = g_ref[0]
        for d in range(1, 8):
            s = s + g_ref[d]
        out_ref[...] = s

    return pl.pallas_call(body, name="sum_devices", out_shape=jax.ShapeDtypeStruct((rows, cols), F32),
                          compiler_params=pltpu.CompilerParams(vmem_limit_bytes=VMEM_LIMIT_BYTES))(gathered)


def _adamw_flat(w, g, m, v):
    def body(w_ref, g_ref, m_ref, v_ref, d_ref, nm_ref, nv_ref):
        d_ref[...], nm_ref[...], nv_ref[...] = _adamw_math(w_ref[...], g_ref[...], m_ref[...], v_ref[...])

    return pl.pallas_call(body, name="adamw_small", out_shape=[jax.ShapeDtypeStruct(w.shape, F32)] * 3,
                          compiler_params=pltpu.CompilerParams(vmem_limit_bytes=VMEM_LIMIT_BYTES))(w, g, m, v)


SMALL_NAMES = ["g_mix", "lower_bounds", "g_hgrn_out", "w_conv", "sg_ln_g", "sg_ln_b", "w_sg", "b_sg", "g_ffn", "g_final"]
BIG_NAMES = ["w_in", "w_branch", "w_o", "w_ff1", "w_ff2"]
WEIGHT_ORDER = ["w_in", "g_mix", "lower_bounds", "g_hgrn_out", "w_conv", "sg_ln_g", "sg_ln_b", "w_sg", "b_sg", "w_branch",
                "w_o", "g_ffn", "w_ff1", "w_ff2", "g_final"]


def _padded_rows(n):
    return -(-n // SUBLANES) * SUBLANES


def _pack(arrays):
    parts = []
    for a in arrays:
        a = a.reshape(-1, LANES)
        parts.append(jnp.pad(a, ((0, _padded_rows(a.shape[0]) - a.shape[0]), (0, 0))))
    return jnp.concatenate(parts, axis=0)


def _unpack(flat, shapes):
    out, row = [], 0
    for s in shapes:
        n = int(np.prod(s)) // LANES
        out.append(flat[row:row + n].reshape(s))
        row += _padded_rows(n)
    return out


def _as_2d(name, a):
    return a.reshape(DEPTH, N_BRANCH * WIDTH, BRANCH_COLS) if name == "w_branch" else a


def kernel(x, w_in, g_mix, lower_bounds, g_hgrn_out, w_conv, sg_ln_g, sg_ln_b, w_sg, b_sg, w_branch, w_o, g_ffn, w_ff1, w_ff2, g_final, loss_target, m_w_in, m_g_mix, m_lower_bounds, m_g_hgrn_out, m_w_conv, m_sg_ln_g, m_sg_ln_b, m_w_sg, m_b_sg, m_w_branch, m_w_o, m_g_ffn, m_w_ff1, m_w_ff2, m_g_final, v_w_in, v_g_mix, v_lower_bounds, v_g_hgrn_out, v_w_conv, v_sg_ln_g, v_sg_ln_b, v_w_sg, v_b_sg, v_w_branch, v_w_o, v_g_ffn, v_w_ff1, v_w_ff2, v_g_final):
    weights = dict(w_in=w_in, g_mix=g_mix, lower_bounds=lower_bounds, g_hgrn_out=g_hgrn_out, w_conv=w_conv,
                   sg_ln_g=sg_ln_g, sg_ln_b=sg_ln_b, w_sg=w_sg, b_sg=b_sg, w_branch=w_branch, w_o=w_o, g_ffn=g_ffn,
                   w_ff1=w_ff1, w_ff2=w_ff2, g_final=g_final)
    mom1 = dict(w_in=m_w_in, g_mix=m_g_mix, lower_bounds=m_lower_bounds, g_hgrn_out=m_g_hgrn_out, w_conv=m_w_conv,
                sg_ln_g=m_sg_ln_g, sg_ln_b=m_sg_ln_b, w_sg=m_w_sg, b_sg=m_b_sg, w_branch=m_w_branch, w_o=m_w_o,
                g_ffn=m_g_ffn, w_ff1=m_w_ff1, w_ff2=m_w_ff2, g_final=m_g_final)
    mom2 = dict(w_in=v_w_in, g_mix=v_g_mix, lower_bounds=v_lower_bounds, g_hgrn_out=v_g_hgrn_out, w_conv=v_w_conv,
                sg_ln_g=v_sg_ln_g, sg_ln_b=v_sg_ln_b, w_sg=v_w_sg, b_sg=v_b_sg, w_branch=v_w_branch, w_o=v_w_o,
                g_ffn=v_g_ffn, w_ff1=v_w_ff1, w_ff2=v_w_ff2, g_final=v_g_final)
    xi, yi, ci = _mesh_pos()
    pos = jnp.stack([2 * xi + yi, ci]).astype(jnp.int32)
    device = 4 * xi + 2 * yi + ci
    conv_cols = w_conv.shape[2]

    conv_all = _gather_all("gather_w_conv", w_conv.reshape(DEPTH * CONV_K, conv_cols), device)
    conv_full = conv_all.reshape(N_CHIP, 2, DEPTH, CONV_K, conv_cols)[:, 0].transpose(1, 2, 0, 3).reshape(DEPTH, CONV_K, WIDTH)

    ici, d2d = {}, {}
    cast = lambda l, names, after: [_cast_into_slot("cast_" + n, _as_2d(n, weights[n]), l, pos, after=after) for n in names]
    token = (conv_all,)
    for l in range(DEPTH):
        for part, names in (("w_in", BIG_NAMES[:1]), ("rest", BIG_NAMES[1:])):
            ici[l, part] = _split_start("weights_ici_start_%d_%s" % (l, part), cast(l, names, token), _weight_ici_copies,
                                        3 * len(names), after=token)
            token = (ici[l, part][3],)
    lbs = _lbs_fwd(lower_bounds)

    def forward_to_sibling(l, part, deps):
        landed = _split_wait("weights_ici_wait_%d_%s" % (l, part), ici.pop((l, part)), _weight_ici_copies, after=deps)
        d2d[l, part] = _split_start("weights_d2d_start_%d_%s" % (l, part), landed, _weight_d2d_copies, 3 * len(landed))
        return (d2d[l, part][3],)

    def gathered(l, part, deps):
        return _split_wait("weights_d2d_wait_%d_%s" % (l, part), d2d.pop((l, part)), _weight_d2d_copies, after=deps)

    act = x[0]
    normed = _rms_fwd("rms_mix", act, g_mix[0:1], after=token)
    layers = []
    forward_to_sibling(0, "w_in", [normed, lbs])
    for l in range(DEPTH):
        small = dict(g_mix=g_mix[l:l + 1], lbs=lbs[l:l + 1], g_hgrn_out=g_hgrn_out[l:l + 1], w_conv=conv_full[l],
                     sg_ln_g=sg_ln_g[l:l + 1], sg_ln_b=sg_ln_b[l:l + 1], w_sg=w_sg[l],
                     b_sg=b_sg[l].reshape(SG_GROUPS, SG_CHUNK, 1), g_ffn=g_ffn[l:l + 1])
        hooks = dict(after_hgrn=lambda deps, l=l: forward_to_sibling(l, "rest", deps),
                     late_weights=lambda deps, l=l: gathered(l, "rest", deps),
                     before_last=(lambda deps, l=l: forward_to_sibling(l + 1, "w_in", deps)) if l + 1 < DEPTH
                     else (lambda deps: ()))
        act, normed, saved, wts = _layer_fwd(act, normed, gathered(l, "w_in", [act])[0], small,
                                             g_mix[l + 1:l + 2] if l + 1 < DEPTH else None, hooks)
        layers.append((wts, small, saved))
    loss_blk, dact, dg_final = _loss_head(act, g_final.reshape(1, D_MODEL), loss_target[0])

    pipe = _GradPipeline(pos)
    small_grads = [None] * DEPTH
    after = ()
    for l in reversed(range(DEPTH)):
        wts, small, saved = layers[l]
        tick = lambda deps, new, l=l: pipe.tick(deps, [(nm, l, g) for nm, g in new])
        dact, small_grads[l], after = _layer_bwd(dact, saved, wts, small, tick, after)
    grad_x = dact[None]

    stack = lambda key, rows=None: jnp.stack([small_grads[l][key][0] if rows is None else small_grads[l][key][:rows]
                                              for l in range(DEPTH)])
    local_small = dict(
        g_mix=stack("g_mix"), lower_bounds=stack("lbs"), g_hgrn_out=stack("g_hgrn_out"), w_conv=stack("w_conv", CONV_K),
        sg_ln_g=stack("sg_ln_g"), sg_ln_b=stack("sg_ln_b"), w_sg=jnp.stack([small_grads[l]["w_sg"] for l in range(DEPTH)]),
        b_sg=jnp.stack([small_grads[l]["b_sg"].reshape(SG_GROUPS, SG_CHUNK) for l in range(DEPTH)]),
        g_ffn=stack("g_ffn"), g_final=dg_final[0])
    shapes = [local_small[n].shape for n in SMALL_NAMES] + [(SUBLANES, LANES)]
    packed = _pack([local_small[n] for n in SMALL_NAMES] + [loss_blk])
    packed = lax.dynamic_update_slice(jnp.zeros((8, *packed.shape), F32), packed[None], (device, 0, 0))
    small_flight = _split_start("small_grads_start", [packed], _all_to_all_copies, 7, after=after)

    def adam(n, first, layer_grads, into=None, after=()):
        return _adamw_layers("adamw_%s_%d" % (n, first), _as_2d(n, weights[n]), _as_2d(n, mom1[n]), _as_2d(n, mom2[n]),
                             layer_grads, first, into, after)

    done = {"w_ff1": adam("w_ff1", 0, pipe.reduced["w_ff1"], after=(small_flight[3],))}
    token = pipe.tick([done["w_ff1"][1]], [])
    done["w_ff2"] = adam("w_ff2", 0, pipe.reduced["w_ff2"], after=token)
    summed = _sum_devices(_split_wait("small_grads_wait", small_flight, _all_to_all_copies, after=[done["w_ff2"][1]])[0])
    parts = _unpack(summed, shapes)
    loss = parts[-1][0, 0]
    small_grad = dict(zip(SMALL_NAMES, parts[:-1]))
    small_grad["lower_bounds"] = _lbs_bwd(lower_bounds, small_grad["lower_bounds"])
    small_grad["w_conv"] = lax.dynamic_slice_in_dim(small_grad["w_conv"], pos[0] * conv_cols, conv_cols, axis=2)
    g_flat = _pack([small_grad[n] for n in SMALL_NAMES])
    d_flat, m_flat, v_flat = _adamw_flat(_pack([weights[n] for n in SMALL_NAMES]), g_flat,
                                         _pack([mom1[n] for n in SMALL_NAMES]), _pack([mom2[n] for n in SMALL_NAMES]))
    small_shapes = [weights[n].shape for n in SMALL_NAMES]
    grads = dict(small_grad)
    delta = dict(zip(SMALL_NAMES, _unpack(d_flat, small_shapes)))
    new_m = dict(zip(SMALL_NAMES, _unpack(m_flat, small_shapes)))
    new_v = dict(zip(SMALL_NAMES, _unpack(v_flat, small_shapes)))

    for n in ("w_o", "w_branch"):
        done[n] = adam(n, 0, pipe.reduced[n], after=(d_flat,))
    token = pipe.tick([done["w_branch"][1]], [])
    rest = adam("w_in", 1, pipe.reduced["w_in"][1:], after=token)
    pipe.tick([rest[1]], [])
    assert not pipe.busy()
    done["w_in"] = adam("w_in", 0, pipe.reduced["w_in"][:1], into=rest)
    for n in BIG_NAMES:
        grads[n], delta[n], new_m[n], new_v[n] = [o.reshape(weights[n].shape) for o in done[n]]

    return (loss, grad_x, *[grads[n] for n in WEIGHT_ORDER], *[delta[n] for n in WEIGHT_ORDER],
            *[new_m[n] for n in WEIGHT_ORDER], *[new_v[n] for n in WEIGHT_ORDER])
```

```python
import numpy as np
import jax
import jax.numpy as jnp
from jax import lax
from jax.experimental import pallas as pl
from jax.experimental.pallas import tpu as pltpu

F32, BF16 = jnp.float32, jnp.bfloat16

D_MODEL = 1024
WIDTH = 512
N_BRANCH = 3
N_HEAD = 4
HEAD = 128
H_CHUNK = 64
CONV_K = 3
SG_CHUNK = 128
SG_GROUPS = 4
D_FF = 4096
DEPTH = 4
N_CHIP = 4
IN_COLS = 9 * WIDTH + N_BRANCH * D_MODEL
GATE_COL0 = 9 * WIDTH
LB_FLOOR = 1e-30
NORM_EPS = 1e-6
LN_EPS = 1e-5
ADAM_LR, ADAM_B1, ADAM_B2, ADAM_EPS, ADAM_WD, ADAM_STEP = 0.001, 0.9, 0.999, 1e-08, 0.01, 10

VMEM_LIMIT_BYTES = 56 * 1024 * 1024
VMEM_BLOCK_BUDGET = 44 * 1024 * 1024
SUBLANES, LANES = 8, 128
ELEMWISE_BLOCK_BYTES = 2 * 1024 * 1024

NN = (((1,), (0,)), ((), ()))
NT = (((1,), (1,)), ((), ()))
TN = (((0,), (0,)), ((), ()))
MESH = pl.DeviceIdType.MESH
ANY = pl.BlockSpec(memory_space=pl.ANY)


def _dot(a, b, dims=NN):
    return lax.dot_general(a.astype(BF16), b.astype(BF16), dims, preferred_element_type=F32)


def _params(n_axes):
    return pltpu.CompilerParams(dimension_semantics=("arbitrary",) * n_axes, vmem_limit_bytes=VMEM_LIMIT_BYTES)


def _row0(part, rows=SUBLANES):
    r = lax.broadcasted_iota(jnp.int32, (rows, part.shape[1]), 0)
    return jnp.where(r == 0, part, 0.0)


def _token_tile(T):
    return min(512, T)


def _matmul(name, a, b, *, dims, grid, a_spec, b_spec, out_specs, out_shapes, acc_shape,
            extra=(), extra_specs=(), epilogue=None, after=()):
    nk = grid[2]
    n_extra, n_out, n_in = len(extra), len(out_shapes), 2 + len(extra) + len(after)
    one_step = nk == 1

    def body(*refs):
        a_ref, b_ref = refs[0], refs[1]
        ex = refs[2:2 + n_extra]
        outs = refs[n_in:n_in + n_out]
        part = _dot(a_ref[...], b_ref[...], dims)

        def finish(total):
            res = epilogue(total, *[e[...] for e in ex]) if epilogue else (total,)
            for o, r in zip(outs, res):
                o[...] = r.astype(o.dtype)

        if one_step:
            finish(part)
            return
        acc = refs[-1]
        kk = pl.program_id(2)

        @pl.when(kk == 0)
        def _():
            acc[...] = part

        @pl.when(kk > 0)
        def _():
            acc[...] += part

        @pl.when(kk == nk - 1)
        def _():
            finish(acc[...])

    return pl.pallas_call(
        body, name=name, grid=grid,
        in_specs=[a_spec, b_spec, *extra_specs, *[ANY] * len(after)], out_specs=list(out_specs),
        out_shape=list(out_shapes), scratch_shapes=[] if one_step else [pltpu.VMEM(acc_shape, F32)],
        compiler_params=_params(3),
    )(a, b, *extra, *after)


def _mm_cols(name, a, w, out_dtypes, epilogue=None, extra=()):
    T, K = a.shape
    N = w.shape[2]
    tm = min(2 * _token_tile(T), T)
    blk = pl.BlockSpec((tm, N), lambda j, i, kk: (i, j))
    return _matmul(
        name, a, w, dims=NN, grid=(N_CHIP, T // tm, 1),
        a_spec=pl.BlockSpec((tm, K), lambda j, i, kk: (i, 0)),
        b_spec=pl.BlockSpec((None, K, N), lambda j, i, kk: (j, 0, 0)),
        out_specs=[blk] * len(out_dtypes),
        out_shapes=[jax.ShapeDtypeStruct((T, N_CHIP * N), dt) for dt in out_dtypes],
        acc_shape=(tm, N), extra=extra, extra_specs=[blk] * len(extra), epilogue=epilogue)


def _mm_rows(name, a, w, res, norm_gain=None, after=()):
    T = a.shape[0]
    K, N = N_CHIP * w.shape[1], w.shape[2]
    tm = _token_tile(T)
    blk = pl.BlockSpec((tm, N), lambda i, j, kk: (i, 0))

    def with_norm(acc, r, gain):
        xv = acc + r
        return xv, xv * lax.rsqrt(jnp.mean(xv * xv, axis=-1, keepdims=True) + NORM_EPS) * gain

    normed = norm_gain is not None
    outs = _matmul(
        name, a, w.reshape(K, N), dims=NN, grid=(T // tm, 1, 1),
        a_spec=pl.BlockSpec((tm, K), lambda i, j, kk: (i, 0)),
        b_spec=pl.BlockSpec((K, N), lambda i, j, kk: (0, 0)),
        out_specs=[blk] * (2 if normed else 1),
        out_shapes=[jax.ShapeDtypeStruct((T, N), F32)] + ([jax.ShapeDtypeStruct((T, N), BF16)] if normed else []),
        acc_shape=(tm, N), extra=(res, norm_gain) if normed else (res,),
        extra_specs=[blk] + ([pl.BlockSpec((1, N), lambda i, j, kk: (0, 0))] if normed else []),
        epilogue=with_norm if normed else (lambda acc, r: (acc + r,)), after=after)
    return outs if normed else outs[0]


def _mm_cols_t(name, g, w, out_dtype, epilogue=None, extra=(), after=()):
    T, N = g.shape
    K = N_CHIP * w.shape[1]
    tm = _token_tile(T) if K <= 2 * D_MODEL else _token_tile(T) // 2
    blk = pl.BlockSpec((tm, K), lambda i, j, kk: (i, 0))
    return _matmul(
        name, g, w.reshape(K, N), dims=NT, grid=(T // tm, 1, 1),
        a_spec=pl.BlockSpec((tm, N), lambda i, j, kk: (i, 0)),
        b_spec=pl.BlockSpec((K, N), lambda i, j, kk: (0, 0), pipeline_mode=pl.Buffered(1)),
        out_specs=[blk], out_shapes=[jax.ShapeDtypeStruct((T, K), out_dtype)], acc_shape=(tm, K),
        extra=extra, extra_specs=[blk] * len(extra), epilogue=epilogue, after=after)[0]


def _dgrad_norm_bwd(name, g, w, x, gain, dres):
    T = g.shape[0]
    K, N = w.shape[1], w.shape[2]
    tm = _token_tile(T)
    whole = w.size * w.dtype.itemsize <= VMEM_BLOCK_BUDGET // 2

    def norm_bwd(i, dhv, x_ref, gain_ref, dres_ref, dx_ref, dgain_ref):
        xv = x_ref[...]
        r = lax.rsqrt(jnp.mean(xv * xv, axis=-1, keepdims=True) + NORM_EPS)
        xn = xv * r
        dxn = dhv * gain_ref[...]
        dx_ref[...] = dres_ref[...] + r * (dxn - xn * jnp.mean(dxn * xn, axis=-1, keepdims=True))

        @pl.when(i == 0)
        def _():
            dgain_ref[...] = jnp.zeros_like(dgain_ref)

        dgain_ref[...] += _row0(jnp.sum(dhv * xn, axis=0, keepdims=True))

    def body_whole(g_ref, w_ref, *rest):
        dhv = _dot(g_ref[:, pl.ds(0, N)], w_ref[0], NT)
        for k in range(1, N_CHIP):
            dhv = dhv + _dot(g_ref[:, pl.ds(k * N, N)], w_ref[k], NT)
        norm_bwd(pl.program_id(0), dhv, *rest)

    def body_steps(g_ref, w_ref, x_ref, gain_ref, dres_ref, dx_ref, dgain_ref, acc):
        kk = pl.program_id(1)
        part = _dot(g_ref[...], w_ref[...], NT)

        @pl.when(kk == 0)
        def _():
            acc[...] = part

        @pl.when(kk > 0)
        def _():
            acc[...] += part

        @pl.when(kk == N_CHIP - 1)
        def _():
            norm_bwd(pl.program_id(0), acc[...], x_ref, gain_ref, dres_ref, dx_ref, dgain_ref)

    tile = pl.BlockSpec((tm, K), lambda i, *kk: (i, 0))
    row = lambda cols: pl.BlockSpec((SUBLANES if cols is None else 1, K), lambda i, *kk: (0, 0))
    if whole:
        g_spec = pl.BlockSpec((tm, N_CHIP * N), lambda i: (i, 0))
        w_spec = pl.BlockSpec(w.shape, lambda i: (0, 0, 0), pipeline_mode=pl.Buffered(1))
    else:
        g_spec = pl.BlockSpec((tm, N), lambda i, kk: (i, kk))
        w_spec = pl.BlockSpec((None, K, N), lambda i, kk: (kk, 0, 0))
    return pl.pallas_call(
        body_whole if whole else body_steps, name=name, grid=(T // tm,) if whole else (T // tm, N_CHIP),
        in_specs=[g_spec, w_spec, tile, row(1), tile], out_specs=[tile, row(None)],
        out_shape=[jax.ShapeDtypeStruct((T, K), F32), jax.ShapeDtypeStruct((SUBLANES, K), F32)],
        scratch_shapes=[] if whole else [pltpu.VMEM((tm, K), F32)],
        compiler_params=_params(1 if whole else 2))(g, w, x, gain, dres)


def _mm_wgrad(name, a, g, a_cols, g_cols, a_blocked, g_blocked, after=()):
    T = a.shape[0]
    tt = T
    while tt > LANES and 2 * 2 * tt * (a_cols + g_cols) + (2 if tt == T else 3) * 4 * a_cols * g_cols > VMEM_BLOCK_BUDGET:
        tt //= 2
    return _matmul(
        name, a, g, dims=TN, grid=(N_CHIP, 1, T // tt),
        a_spec=pl.BlockSpec((tt, a_cols), (lambda j, i, kk: (kk, j)) if a_blocked else (lambda j, i, kk: (kk, 0))),
        b_spec=pl.BlockSpec((tt, g_cols), (lambda j, i, kk: (kk, j)) if g_blocked else (lambda j, i, kk: (kk, 0))),
        out_specs=[pl.BlockSpec((None, a_cols, g_cols), lambda j, i, kk: (j, 0, 0))],
        out_shapes=[jax.ShapeDtypeStruct((N_CHIP, a_cols, g_cols), F32)], acc_shape=(a_cols, g_cols), after=after)[0]


def _rms_fwd(name, x, g, after=()):
    T, Dm = x.shape
    tm = min(256, T)

    def body(x_ref, g_ref, *rest):
        xv = x_ref[...]
        r = lax.rsqrt(jnp.mean(xv * xv, axis=-1, keepdims=True) + NORM_EPS)
        rest[-1][...] = (xv * r * g_ref[...]).astype(BF16)

    return pl.pallas_call(
        body, name=name, grid=(T // tm,),
        in_specs=[pl.BlockSpec((tm, Dm), lambda i: (i, 0)), pl.BlockSpec((1, Dm), lambda i: (0, 0))] + [ANY] * len(after),
        out_specs=pl.BlockSpec((tm, Dm), lambda i: (i, 0)),
        out_shape=jax.ShapeDtypeStruct((T, Dm), BF16), compiler_params=_params(1))(x, g, *after)


def _loss_head(x, g, tgt):
    T, Dm = x.shape
    tm = min(256, T)

    def body(x_ref, g_ref, t_ref, loss_ref, dx_ref, dg_ref):
        xv = x_ref[...]
        gv = g_ref[...]
        r = lax.rsqrt(jnp.mean(xv * xv, axis=-1, keepdims=True) + NORM_EPS)
        xn = xv * r
        err = xn * gv - t_ref[...]
        dy = err * (1.0 / Dm)
        dxn = dy * gv
        dx_ref[...] = r * (dxn - xn * jnp.mean(dxn * xn, axis=-1, keepdims=True))

        @pl.when(pl.program_id(0) == 0)
        def _():
            dg_ref[...] = jnp.zeros_like(dg_ref)
            loss_ref[...] = jnp.zeros_like(loss_ref)

        dg_ref[...] += _row0(jnp.sum(dy * xn, axis=0, keepdims=True))
        part = jnp.sum(jnp.sum(err * err, axis=-1, keepdims=True), axis=0, keepdims=True) * (0.5 / Dm)
        loss_ref[...] += jnp.broadcast_to(part, loss_ref.shape)

    tile = pl.BlockSpec((tm, Dm), lambda i: (i, 0))
    return pl.pallas_call(
        body, name="loss_head", grid=(T // tm,),
        in_specs=[tile, pl.BlockSpec((1, Dm), lambda i: (0, 0)), tile],
        out_specs=[pl.BlockSpec((SUBLANES, LANES), lambda i: (0, 0)), tile,
                   pl.BlockSpec((SUBLANES, Dm), lambda i: (0, 0))],
        out_shape=[jax.ShapeDtypeStruct((SUBLANES, LANES), F32), jax.ShapeDtypeStruct((T, Dm), F32),
                   jax.ShapeDtypeStruct((SUBLANES, Dm), F32)],
        compiler_params=_params(1))(x, g, tgt)


def _softmax_rows(lb_ref):
    rows = [lb_ref[pl.ds(i, 1), :] for i in range(DEPTH)]
    mx = rows[0]
    for r in rows[1:]:
        mx = jnp.maximum(mx, r)
    es = [jnp.exp(r - mx) for r in rows]
    tot = es[0]
    for e in es[1:]:
        tot = tot + e
    return [e / tot for e in es]


def _lbs_fwd(lower_bounds):
    def body(lb_ref, out_ref):
        sm = _softmax_rows(lb_ref)
        run = jnp.zeros_like(sm[0])
        out_ref[pl.ds(0, 1), :] = run
        for i in range(1, DEPTH):
            run = run + sm[i]
            out_ref[pl.ds(i, 1), :] = run

    return pl.pallas_call(body, name="lbs_fwd", out_shape=jax.ShapeDtypeStruct(lower_bounds.shape, F32))(lower_bounds)


def _lbs_bwd(lower_bounds, dlbs):
    def body(lb_ref, d_ref, out_ref):
        sm = _softmax_rows(lb_ref)
        dsm = [jnp.zeros_like(sm[0])]
        for i in range(1, DEPTH):
            acc = d_ref[pl.ds(i, 1), :]
            for l in range(i + 1, DEPTH):
                acc = acc + d_ref[pl.ds(l, 1), :]
            dsm.append(acc)
        inner = dsm[0] * sm[0]
        for i in range(1, DEPTH):
            inner = inner + dsm[i] * sm[i]
        for i in range(DEPTH):
            out_ref[pl.ds(i, 1), :] = sm[i] * (dsm[i] - inner)

    return pl.pallas_call(body, name="lbs_bwd", out_shape=jax.ShapeDtypeStruct(lower_bounds.shape, F32))(lower_bounds, dlbs)


N_LEVEL = 6


def _hgrn_consts():
    L = H_CHUNK
    t = np.arange(L)
    blocks = [(t[:, None] >= t[None, :]).astype(np.float32)]
    masks = []
    m = L // 2
    while m >= 1:
        blk, pos = t // (2 * m), t % (2 * m)
        start = blk * 2 * m
        mat = np.zeros((L, L), np.float32)
        for r in range(L):
            if pos[r] >= m:
                mat[r, start[r] + m:r + 1] = 1.0
            else:
                mat[r, r + 1:start[r] + m] = -1.0
        blocks.append(mat)
        masks.append(((blk[:, None] == blk[None, :]) & (pos[:, None] >= m) & (pos[None, :] < m)).astype(np.float32))
        m //= 2
    blocks.append(np.ones((L, L), np.float32))
    return jnp.asarray(np.concatenate(blocks, 0), BF16), jnp.asarray(np.stack(masks), F32)


def _hgrn_core(qraw, fp, lb, sum_mat, mask_ref):
    L = H_CHUNK
    sq = jax.nn.sigmoid(qraw)
    q = qraw * sq
    sneg = jax.nn.sigmoid(-fp)
    log_sig = jnp.minimum(fp, 0.0) - jnp.log1p(jnp.exp(-jnp.abs(fp)))
    a1 = jnp.log(jnp.maximum(lb, LB_FLOOR))
    a2 = jnp.log1p(-lb) + log_sig
    logf = jnp.maximum(a1, a2) + jnp.log1p(jnp.exp(-jnp.abs(a1 - a2)))
    w1 = jnp.exp(a1 - logf)
    w2 = jnp.exp(a2 - logf)
    k = (1.0 - lb) * sneg
    hi = logf.astype(BF16)
    r1 = logf - hi.astype(F32)
    mid = r1.astype(BF16)
    lo = (r1 - mid.astype(F32)).astype(BF16)
    sums = lax.dot_general(sum_mat, jnp.concatenate([hi, mid, lo], axis=1), NN, preferred_element_type=F32)
    sums = sums[:, 0:HEAD] + sums[:, HEAD:2 * HEAD] + sums[:, 2 * HEAD:3 * HEAD]
    b = sums[0:L]
    b_last = sums[(N_LEVEL + 1) * L:(N_LEVEL + 2) * L]
    eye = lax.broadcasted_iota(jnp.int32, (L, L), 0) == lax.broadcasted_iota(jnp.int32, (L, L), 1)
    attn = jnp.where(eye, jnp.sum(q * k, axis=1, keepdims=True), 0.0)
    fa, fb, ea, eb = [], [], [], []
    for l in range(N_LEVEL):
        d = sums[(l + 1) * L:(l + 2) * L]
        e_a = jnp.exp(jnp.minimum(d, 0.0))
        e_b = jnp.exp(jnp.minimum(-d, 0.0))
        a_l, b_l = q * e_a, k * e_b
        attn = attn + mask_ref[l] * _dot(a_l, b_l, NT)
        fa.append(a_l), fb.append(b_l), ea.append(e_a), eb.append(e_b)
    return dict(sq=sq, q=q, sneg=sneg, logf=logf, w1=w1, w2=w2, k=k, b=b, b_last=b_last, attn=attn,
                fa=fa, fb=fb, ea=ea, eb=eb)


def _hgrn_fwd(p, lbrow, gout):
    T = p.shape[0]
    nch = T // H_CHUNK
    sum_mat, masks = _hgrn_consts()

    def body(p_ref, lb_ref, g_ref, m_ref, mask_ref, o_ref, z_ref, st_ref, state):
        @pl.when(pl.program_id(0) == 0)
        def _():
            state[...] = jnp.zeros_like(state)

        sum_m = m_ref[...]
        for h in range(N_HEAD):
            col = lambda part: pl.ds(part * WIDTH + h * HEAD, HEAD)
            hs = pl.ds(h * HEAD, HEAD)
            v = p_ref[:, col(2)]
            c = _hgrn_core(p_ref[:, col(0)], p_ref[:, col(1)], lb_ref[:, hs], sum_m, mask_ref)
            s0 = state[h]
            st_ref[h] = s0
            o = _dot(c["attn"], v) + _dot(c["q"] * jnp.exp(c["b"]), s0, NT)
            k_dec = c["k"] * jnp.exp(c["b_last"] - c["b"])
            decay = jnp.exp(jnp.max(c["b_last"], axis=0, keepdims=True))
            state[h] = s0 * decay + _dot(v, k_dec, TN)
            o_ref[:, hs] = o
            r = lax.rsqrt(jnp.mean(o * o, axis=-1, keepdims=True) + NORM_EPS)
            z_ref[:, hs] = (o * r * g_ref[:, hs] * jax.nn.sigmoid(p_ref[:, col(3)])).astype(BF16)

    full = lambda shape: pl.BlockSpec(shape, lambda c: (0,) * len(shape))
    return pl.pallas_call(
        body, name="hgrn_fwd", grid=(nch,),
        in_specs=[pl.BlockSpec((H_CHUNK, 4 * WIDTH), lambda c: (c, 0)), full((1, WIDTH)), full((1, WIDTH)),
                  full(sum_mat.shape), full(masks.shape)],
        out_specs=[pl.BlockSpec((H_CHUNK, WIDTH), lambda c: (c, 0)),
                   pl.BlockSpec((None, H_CHUNK, WIDTH), lambda c: (0, c, 0)),
                   pl.BlockSpec((None, N_HEAD, HEAD, HEAD), lambda c: (c, 0, 0, 0))],
        out_shape=[jax.ShapeDtypeStruct((T, WIDTH), F32), jax.ShapeDtypeStruct((N_BRANCH, T, WIDTH), BF16),
                   jax.ShapeDtypeStruct((nch, N_HEAD, HEAD, HEAD), F32)],
        scratch_shapes=[pltpu.VMEM((N_HEAD, HEAD, HEAD), F32)], compiler_params=_params(1),
    )(p, lbrow, gout, sum_mat, masks)


def _hgrn_bwd(p, o_saved, dz, states, lbrow, gout, dp, after=()):
    T = p.shape[0]
    nch = T // H_CHUNK
    L = H_CHUNK
    sum_mat, masks = _hgrn_consts()

    def body(p_ref, o_ref, dz_ref, st_ref, lb_ref, g_ref, m_ref, mask_ref, dp_in, *rest):
        del dp_in
        dp_ref, dlb_ref, dg_ref, dstate = rest[len(after):]

        @pl.when(pl.program_id(0) == 0)
        def _():
            dstate[...] = jnp.zeros_like(dstate)
            dlb_ref[...] = jnp.zeros_like(dlb_ref)
            dg_ref[...] = jnp.zeros_like(dg_ref)

        sum_m = m_ref[...]
        for h in range(N_HEAD):
            col = lambda part: pl.ds(part * WIDTH + h * HEAD, HEAD)
            hs = pl.ds(h * HEAD, HEAD)
            qraw, fp, v, go = p_ref[:, col(0)], p_ref[:, col(1)], p_ref[:, col(2)], p_ref[:, col(3)]
            lb, g = lb_ref[:, hs], g_ref[:, hs]
            c = _hgrn_core(qraw, fp, lb, sum_m, mask_ref)
            q, k, b, b_last = c["q"], c["k"], c["b"], c["b_last"]
            s0, ds1 = st_ref[h], dstate[h]
            e_b = jnp.exp(b)
            q_dec = q * e_b
            e_bl = jnp.exp(b_last - b)
            k_dec = k * e_bl
            decay = jnp.exp(jnp.max(b_last, axis=0, keepdims=True))
            o = o_ref[:, hs]
            r = lax.rsqrt(jnp.mean(o * o, axis=-1, keepdims=True) + NORM_EPS)
            n = o * r
            sgo = jax.nn.sigmoid(go)
            dza = dz_ref[:, hs]
            dgo = dza * n * g * sgo * (1.0 - sgo)
            dg_ref[:, hs] += _row0(jnp.sum(dza * n * sgo, axis=0, keepdims=True))
            dn = dza * g * sgo
            do = r * (dn - n * jnp.mean(dn * n, axis=-1, keepdims=True))
            dattn = _dot(do, v, NT)
            dv = _dot(c["attn"], do, TN) + _dot(k_dec, ds1, NT)
            dq_dec = _dot(do, s0)
            dk_dec = _dot(v, ds1)
            ddiag = jnp.sum(do * v, axis=1, keepdims=True)
            dq = dq_dec * e_b + ddiag * k
            dk = dk_dec * e_bl + ddiag * q
            dsums = [dq_dec * q_dec - dk_dec * k_dec]
            for l in range(N_LEVEL):
                dm = mask_ref[l] * dattn
                da = _dot(dm, c["fb"][l])
                db = _dot(dm, c["fa"][l], TN)
                dq = dq + da * c["ea"][l]
                dk = dk + db * c["eb"][l]
                dsums.append(da * c["fa"][l] - db * c["fb"][l])
            dlast = jnp.sum(ds1 * s0, axis=0, keepdims=True) * decay
            dsums.append(dk_dec * k_dec + _row0(dlast, L))
            dlogf = _dot(sum_m, jnp.concatenate(dsums, axis=0), TN)
            dstate[h] = ds1 * decay + _dot(do, q_dec, TN)
            sq, sneg = c["sq"], c["sneg"]
            dqraw = dq * sq * (1.0 + qraw * (1.0 - sq))
            dfp = dlogf * c["w2"] * sneg - dk * (1.0 - lb) * sneg * (1.0 - sneg)
            inv_lb = jnp.where(lb > LB_FLOOR, 1.0 / jnp.maximum(lb, LB_FLOOR), 0.0)
            dlb_tok = dlogf * (c["w1"] * inv_lb - c["w2"] / (1.0 - lb)) - dk * sneg
            dlb_ref[:, hs] += _row0(jnp.sum(dlb_tok, axis=0, keepdims=True))
            dp_ref[:, col(0)] = dqraw.astype(BF16)
            dp_ref[:, col(1)] = dfp.astype(BF16)
            dp_ref[:, col(2)] = dv.astype(BF16)
            dp_ref[:, col(3)] = dgo.astype(BF16)

    full = lambda shape: pl.BlockSpec(shape, lambda c: (0,) * len(shape))
    rev = lambda c: nch - 1 - c
    return pl.pallas_call(
        body, name="hgrn_bwd", grid=(nch,),
        in_specs=[pl.BlockSpec((L, 4 * WIDTH), lambda c: (rev(c), 0)), pl.BlockSpec((L, WIDTH), lambda c: (rev(c), 0)),
                  pl.BlockSpec((None, L, WIDTH), lambda c: (0, rev(c), 0)),
                  pl.BlockSpec((None, N_HEAD, HEAD, HEAD), lambda c: (rev(c), 0, 0, 0)),
                  full((1, WIDTH)), full((1, WIDTH)), full(sum_mat.shape), full(masks.shape), ANY, *[ANY] * len(after)],
        out_specs=[pl.BlockSpec((L, 4 * WIDTH), lambda c: (rev(c), 0)), full((SUBLANES, WIDTH)), full((SUBLANES, WIDTH))],
        out_shape=[jax.ShapeDtypeStruct(dp.shape, dp.dtype), jax.ShapeDtypeStruct((SUBLANES, WIDTH), F32),
                   jax.ShapeDtypeStruct((SUBLANES, WIDTH), F32)],
        scratch_shapes=[pltpu.VMEM((N_HEAD, HEAD, HEAD), F32)], input_output_aliases={8: 0},
        compiler_params=_params(1),
    )(p, o_saved, dz, states, lbrow, gout, sum_mat, masks, dp, *after)


def _shift_down(tile, halo, s):
    tm = tile.shape[0]
    rows = lax.broadcasted_iota(jnp.int32, tile.shape, 0)
    head = jnp.concatenate([pltpu.roll(halo, s, 0), jnp.zeros((tm - SUBLANES, tile.shape[1]), tile.dtype)], axis=0)
    return jnp.where(rows < s, head, pltpu.roll(tile, s, 0))


def _shift_up(tile, halo, s):
    tm = tile.shape[0]
    rows = lax.broadcasted_iota(jnp.int32, tile.shape, 0)
    tail = jnp.concatenate([jnp.zeros((tm - SUBLANES, tile.shape[1]), tile.dtype), pltpu.roll(halo, SUBLANES - s, 0)], axis=0)
    return jnp.where(rows >= tm - s, tail, pltpu.roll(tile, tm - s, 0))


def _conv_fwd(p, w, z, after=()):
    T = p.shape[0]
    tm = min(2 * _token_tile(T), T)
    per = tm // SUBLANES

    def body(bg_ref, cg_ref, xc_ref, hcg_ref, hxc_ref, w_ref, *rest):
        z_ref = rest[-1]
        zc = cg_ref[...] * xc_ref[...]
        hz = jnp.where(pl.program_id(0) > 0, hcg_ref[...] * hxc_ref[...], 0.0)
        y = (w_ref[pl.ds(0, 1), :] * _shift_down(zc, hz, 2) + w_ref[pl.ds(1, 1), :] * _shift_down(zc, hz, 1)
             + w_ref[pl.ds(2, 1), :] * zc)
        z_ref[...] = (bg_ref[...] * y).astype(BF16)

    tile = lambda cb: pl.BlockSpec((tm, WIDTH), lambda i: (i, cb))
    prev = lambda cb: pl.BlockSpec((SUBLANES, WIDTH), lambda i: (jnp.maximum(i * per - 1, 0), cb))
    return pl.pallas_call(
        body, name="conv_fwd", grid=(T // tm,),
        in_specs=[tile(4), tile(5), tile(6), prev(5), prev(6), pl.BlockSpec((CONV_K, WIDTH), lambda i: (0, 0)), ANY,
                  *[ANY] * len(after)],
        out_specs=pl.BlockSpec((None, tm, WIDTH), lambda i: (1, i, 0)),
        out_shape=jax.ShapeDtypeStruct(z.shape, z.dtype), input_output_aliases={6: 0}, compiler_params=_params(1),
    )(p, p, p, p, p, w, z, *after)


def _conv_bwd(p, w, dz, dp):
    T = p.shape[0]
    tm = min(2 * _token_tile(T), T)
    per = tm // SUBLANES
    last = T // SUBLANES - 1

    def body(bg_ref, cg_ref, xc_ref, hcg_ref, hxc_ref, nbg_ref, dzb_ref, ndzb_ref, w_ref, dp_in, dp_ref, dw_ref, stash):
        del dp_in
        i, jj = pl.program_id(0), pl.program_id(1)

        @pl.when(jnp.logical_and(i == 0, jj == 0))
        def _():
            dw_ref[...] = jnp.zeros_like(dw_ref)

        @pl.when(jj == 0)
        def _():
            cg, xc, bg = cg_ref[...], xc_ref[...], bg_ref[...]
            w0, w1, w2 = w_ref[pl.ds(0, 1), :], w_ref[pl.ds(1, 1), :], w_ref[pl.ds(2, 1), :]
            zc = cg * xc
            hz = jnp.where(i > 0, hcg_ref[...] * hxc_ref[...], 0.0)
            z2, z1 = _shift_down(zc, hz, 2), _shift_down(zc, hz, 1)
            y = w0 * z2 + w1 * z1 + w2 * zc
            dzb = dzb_ref[...]
            dy = dzb * bg
            hdy = jnp.where(i < pl.num_programs(0) - 1, ndzb_ref[...] * nbg_ref[...], 0.0)
            dzc = w2 * dy + w1 * _shift_up(dy, hdy, 1) + w0 * _shift_up(dy, hdy, 2)
            rows = lax.broadcasted_iota(jnp.int32, (SUBLANES, WIDTH), 0)
            colsum = lambda t: jnp.sum(t, axis=0, keepdims=True)
            dw_ref[...] += (jnp.where(rows == 0, colsum(dy * z2), 0.0) + jnp.where(rows == 1, colsum(dy * z1), 0.0)
                            + jnp.where(rows == 2, colsum(dy * zc), 0.0))
            dp_ref[...] = (dzb * y).astype(BF16)
            stash[0] = dzc * xc
            stash[1] = dzc * cg

        @pl.when(jj > 0)
        def _():
            dp_ref[...] = stash[jj - 1].astype(BF16)

    n_tiles = T // tm
    tile = lambda cb: pl.BlockSpec((tm, WIDTH), lambda i, jj: (i, cb))
    prev = lambda cb: pl.BlockSpec((SUBLANES, WIDTH), lambda i, jj: (jnp.maximum(i * per - 1, 0), cb))
    nxt = lambda i: jnp.minimum((i + 1) * per, last)
    return pl.pallas_call(
        body, name="conv_bwd", grid=(n_tiles, 3),
        in_specs=[tile(4), tile(5), tile(6), prev(5), prev(6),
                  pl.BlockSpec((SUBLANES, WIDTH), lambda i, jj: (nxt(i), 4)),
                  pl.BlockSpec((None, tm, WIDTH), lambda i, jj: (1, i, 0)),
                  pl.BlockSpec((None, SUBLANES, WIDTH), lambda i, jj: (1, nxt(i), 0)),
                  pl.BlockSpec((CONV_K, WIDTH), lambda i, jj: (0, 0)), ANY],
        out_specs=[pl.BlockSpec((tm, WIDTH), lambda i, jj: (i, 4 + jj)),
                   pl.BlockSpec((SUBLANES, WIDTH), lambda i, jj: (0, 0))],
        out_shape=[jax.ShapeDtypeStruct(dp.shape, dp.dtype), jax.ShapeDtypeStruct((SUBLANES, WIDTH), F32)],
        scratch_shapes=[pltpu.VMEM((2, tm, WIDTH), F32)], input_output_aliases={9: 0}, compiler_params=_params(2),
    )(p, p, p, p, p, p, dz, dz, w, dp)


GELU_C = float(np.sqrt(2.0 / np.pi))
GELU_A = 0.044715


def _gelu(x):
    th = jnp.tanh(GELU_C * (x + GELU_A * x * x * x))
    return 0.5 * x * (1.0 + th), th


def _gelu_grad(x, th):
    return 0.5 * (1.0 + th) + 0.5 * x * (1.0 - th * th) * GELU_C * (1.0 + 3.0 * GELU_A * x * x)


def _sg_core(u, v, lng, lnb, ws_ref, bs_ref):
    gu, thu = _gelu(u)
    gv, thv = _gelu(v)
    xc = gv - jnp.mean(gv, axis=-1, keepdims=True)
    rs = lax.rsqrt(jnp.mean(xc * xc, axis=-1, keepdims=True) + LN_EPS)
    xh = xc * rs
    vp = xh * lng + lnb
    tril = (lax.broadcasted_iota(jnp.int32, (SG_CHUNK, SG_CHUNK), 0)
            >= lax.broadcasted_iota(jnp.int32, (SG_CHUNK, SG_CHUNK), 1))
    wm = [jnp.where(tril, ws_ref[g], 0.0).astype(BF16) for g in range(SG_GROUPS)]
    gs = lambda t, g: t[:, g * LANES:(g + 1) * LANES]
    sv = jnp.concatenate([_dot(wm[g], gs(vp, g)) + bs_ref[g] for g in range(SG_GROUPS)], axis=1)
    return dict(gu=gu, thu=thu, thv=thv, rs=rs, xh=xh, vp=vp, tril=tril, wm=wm, sv=sv)


def _sg_fwd(p, lng, lnb, ws, bs, z):
    T = p.shape[0]

    def body(u_ref, v_ref, lng_ref, lnb_ref, ws_ref, bs_ref, z_in, z_ref):
        del z_in
        c = _sg_core(u_ref[...], v_ref[...], lng_ref[...], lnb_ref[...], ws_ref, bs_ref)
        z_ref[...] = (c["gu"] * c["sv"]).astype(BF16)

    full = lambda shape: pl.BlockSpec(shape, lambda c: (0,) * len(shape))
    return pl.pallas_call(
        body, name="sg_fwd", grid=(T // SG_CHUNK,),
        in_specs=[pl.BlockSpec((SG_CHUNK, WIDTH), lambda c: (c, 7)), pl.BlockSpec((SG_CHUNK, WIDTH), lambda c: (c, 8)),
                  full((1, WIDTH)), full((1, WIDTH)), full(ws.shape), full(bs.shape), ANY],
        out_specs=pl.BlockSpec((None, SG_CHUNK, WIDTH), lambda c: (2, c, 0)),
        out_shape=jax.ShapeDtypeStruct(z.shape, z.dtype), input_output_aliases={6: 0}, compiler_params=_params(1),
    )(p, p, lng, lnb, ws, bs, z)


def _sg_bwd(p, lng, lnb, ws, bs, dz, dp):
    T = p.shape[0]

    def body(u_ref, v_ref, lng_ref, lnb_ref, ws_ref, bs_ref, dz_ref, dp_in, dp_ref, dws_ref, dbs_ref, dlng_ref, dlnb_ref,
             stash):
        del dp_in
        cidx, jj = pl.program_id(0), pl.program_id(1)

        @pl.when(jnp.logical_and(cidx == 0, jj == 0))
        def _():
            dws_ref[...] = jnp.zeros_like(dws_ref)
            dbs_ref[...] = jnp.zeros_like(dbs_ref)
            dlng_ref[...] = jnp.zeros_like(dlng_ref)
            dlnb_ref[...] = jnp.zeros_like(dlnb_ref)

        @pl.when(jj == 0)
        def _():
            u, v, lng = u_ref[...], v_ref[...], lng_ref[...]
            c = _sg_core(u, v, lng, lnb_ref[...], ws_ref, bs_ref)
            dzc = dz_ref[...]
            gs = lambda t, g: t[:, g * LANES:(g + 1) * LANES]
            dsv = dzc * c["gu"]
            dvp = []
            for g in range(SG_GROUPS):
                dsv_g = gs(dsv, g)
                dws_ref[g] += jnp.where(c["tril"], _dot(dsv_g, gs(c["vp"], g), NT), 0.0)
                dbs_ref[g] += jnp.sum(dsv_g, axis=1, keepdims=True)
                dvp.append(_dot(c["wm"][g], dsv_g, TN))
            dvp = jnp.concatenate(dvp, axis=1)
            xh = c["xh"]
            dlng_ref[...] += _row0(jnp.sum(dvp * xh, axis=0, keepdims=True))
            dlnb_ref[...] += _row0(jnp.sum(dvp, axis=0, keepdims=True))
            dxh = dvp * lng
            dgv = c["rs"] * (dxh - jnp.mean(dxh, axis=-1, keepdims=True) - xh * jnp.mean(dxh * xh, axis=-1, keepdims=True))
            dp_ref[...] = (dzc * c["sv"] * _gelu_grad(u, c["thu"])).astype(BF16)
            stash[...] = dgv * _gelu_grad(v, c["thv"])

        @pl.when(jj == 1)
        def _():
            dp_ref[...] = stash[...].astype(BF16)

    full = lambda shape: pl.BlockSpec(shape, lambda c, jj: (0,) * len(shape))
    return pl.pallas_call(
        body, name="sg_bwd", grid=(T // SG_CHUNK, 2),
        in_specs=[pl.BlockSpec((SG_CHUNK, WIDTH), lambda c, jj: (c, 7)), pl.BlockSpec((SG_CHUNK, WIDTH), lambda c, jj: (c, 8)),
                  full((1, WIDTH)), full((1, WIDTH)), full(ws.shape), full(bs.shape),
                  pl.BlockSpec((None, SG_CHUNK, WIDTH), lambda c, jj: (2, c, 0)), ANY],
        out_specs=[pl.BlockSpec((SG_CHUNK, WIDTH), lambda c, jj: (c, 7 + jj)), full(ws.shape), full(bs.shape),
                   full((SUBLANES, WIDTH)), full((SUBLANES, WIDTH))],
        out_shape=[jax.ShapeDtypeStruct(dp.shape, dp.dtype), jax.ShapeDtypeStruct(ws.shape, F32),
                   jax.ShapeDtypeStruct(bs.shape, F32), jax.ShapeDtypeStruct((SUBLANES, WIDTH), F32),
                   jax.ShapeDtypeStruct((SUBLANES, WIDTH), F32)],
        scratch_shapes=[pltpu.VMEM((SG_CHUNK, WIDTH), F32)], input_output_aliases={7: 0}, compiler_params=_params(2),
    )(p, p, lng, lnb, ws, bs, dz, dp)


BRANCH_COLS = D_MODEL // N_CHIP
GATE_UNIT0 = GATE_COL0 // WIDTH
UNITS = D_MODEL // WIDTH


def _unit_specs(order):
    def spec(which):
        def index(*g):
            _, n, u = order(*g)
            return (2 * u + which, n, 0, 0)
        return pl.BlockSpec((None, None, WIDTH, BRANCH_COLS), index)
    return [spec(0), spec(1)]


def _merge_fwd(z, p, wb):
    T = z.shape[1]
    tm = min(2 * _token_tile(T), T)
    order = lambda i, u, n: (i, n, u)

    def body(z_ref, wa_ref, wb_ref, gt_ref, out_ref, acc):
        n = pl.program_id(2)
        zv = z_ref[...]
        y = jnp.concatenate([_dot(zv, wa_ref[...]), _dot(zv, wb_ref[...])], axis=1)
        part = jax.nn.sigmoid(gt_ref[...]) * y

        @pl.when(n == 0)
        def _():
            acc[...] = part

        @pl.when(n > 0)
        def _():
            acc[...] += part

        @pl.when(n == N_BRANCH - 1)
        def _():
            out_ref[...] = acc[...].astype(BF16)

    return pl.pallas_call(
        body, name="merge_fwd", grid=(T // tm, UNITS, N_BRANCH),
        in_specs=[pl.BlockSpec((None, tm, WIDTH), lambda i, u, n: (n, i, 0)), *_unit_specs(order),
                  pl.BlockSpec((tm, WIDTH), lambda i, u, n: (i, GATE_UNIT0 + UNITS * n + u))],
        out_specs=pl.BlockSpec((tm, WIDTH), lambda i, u, n: (i, u)),
        out_shape=jax.ShapeDtypeStruct((T, D_MODEL), BF16),
        scratch_shapes=[pltpu.VMEM((tm, WIDTH), F32)], compiler_params=_params(3))(z, wb, wb, p)


def _merge_bwd(z, p, wb, dmerged):
    T = z.shape[1]
    tm = min(2 * _token_tile(T), T)
    order = lambda n, u, i: (i, n, u)

    def body(z_ref, wa_ref, wb_ref, gt_ref, dm_ref, dp_ref, dw_ref, dz_ref):
        u, i = pl.program_id(1), pl.program_id(2)
        zv, wa, wbv = z_ref[...], wa_ref[...], wb_ref[...]
        y = jnp.concatenate([_dot(zv, wa), _dot(zv, wbv)], axis=1)
        gate = jax.nn.sigmoid(gt_ref[...])
        dm = dm_ref[...]
        dp_ref[...] = (dm * y * gate * (1.0 - gate)).astype(BF16)
        dyv = (dm * gate).astype(BF16)
        dw = _dot(zv, dyv, TN)
        part = _dot(dyv[:, :BRANCH_COLS], wa, NT) + _dot(dyv[:, BRANCH_COLS:], wbv, NT)
        rows = pl.ds(pl.multiple_of(i * tm, tm), tm)

        @pl.when(i == 0)
        def _():
            dw_ref[0] = dw[:, :BRANCH_COLS]
            dw_ref[1] = dw[:, BRANCH_COLS:]

        @pl.when(i > 0)
        def _():
            dw_ref[0] += dw[:, :BRANCH_COLS]
            dw_ref[1] += dw[:, BRANCH_COLS:]

        @pl.when(u == 0)
        def _():
            dz_ref[rows, :] = part

        @pl.when(u > 0)
        def _():
            dz_ref[rows, :] += part

    unit = lambda n, u, i: (i, GATE_UNIT0 + UNITS * n + u)
    return pl.pallas_call(
        body, name="merge_bwd", grid=(N_BRANCH, UNITS, T // tm),
        in_specs=[pl.BlockSpec((None, tm, WIDTH), lambda n, u, i: (n, i, 0)), *_unit_specs(order),
                  pl.BlockSpec((tm, WIDTH), unit), pl.BlockSpec((tm, WIDTH), lambda n, u, i: (i, u))],
        out_specs=[pl.BlockSpec((tm, WIDTH), unit),
                   pl.BlockSpec((2, None, WIDTH, BRANCH_COLS), lambda n, u, i: (u, n, 0, 0)),
                   pl.BlockSpec((None, T, WIDTH), lambda n, u, i: (n, 0, 0))],
        out_shape=[jax.ShapeDtypeStruct((T, IN_COLS), BF16),
                   jax.ShapeDtypeStruct((N_CHIP, N_BRANCH, WIDTH, BRANCH_COLS), F32),
                   jax.ShapeDtypeStruct((N_BRANCH, T, WIDTH), F32)],
        compiler_params=_params(3))(z, wb, wb, p, dmerged)


def _layer_fwd(x, h, win, small, next_gain, hooks):
    p = _mm_cols("in_proj", h, win, [F32])[0]
    o_hgrn, z, states = _hgrn_fwd(p, small["lbs"], small["g_hgrn_out"])
    z = _conv_fwd(p, small["w_conv"], z, after=hooks["after_hgrn"]([o_hgrn]))
    z = _sg_fwd(p, small["sg_ln_g"], small["sg_ln_b"], small["w_sg"], small["b_sg"], z)
    wb, wo, w1, w2 = hooks["late_weights"]([z])
    wb = wb.reshape(N_CHIP, N_BRANCH, WIDTH, BRANCH_COLS)
    merged = _merge_fwd(z, p, wb)
    x_mid, h2 = _mm_rows("out_proj", merged, wo, x, small["g_ffn"])
    s = _mm_cols("ff1", h2, w1, [BF16], epilogue=lambda acc: (jnp.square(jnp.maximum(acc, 0.0)),))[0]
    if next_gain is None:
        x_out, h_next = _mm_rows("ff2_last", s, w2, x_mid, after=hooks["before_last"]([s])), None
    else:
        x_out, h_next = _mm_rows("ff2", s, w2, x_mid, next_gain, after=hooks["before_last"]([s]))
    saved = dict(x=x, h=h, p=p, o_hgrn=o_hgrn, z=z, states=states, merged=merged, x_mid=x_mid, h2=h2, s=s)
    return x_out, h_next, saved, [win, wb, wo, w1, w2]


def _layer_bwd(dx_out, sv, wts, small, tick, after):
    win, wb, wo, w1, w2 = wts
    g = {}
    da = _mm_cols_t("ff2_dgrad", dx_out, w2, BF16, extra=(sv["s"],), after=after,
                    epilogue=lambda acc, s: (acc * 2.0 * jnp.sqrt(s.astype(F32)),))
    d_ff2 = _mm_wgrad("ff2_wgrad", sv["s"], dx_out, w2.shape[1], D_MODEL, True, False)
    d_ff1 = _mm_wgrad("ff1_wgrad", sv["h2"], da, D_MODEL, w1.shape[2], False, True)
    dx_mid, g["g_ffn"] = _dgrad_norm_bwd("ff1_dgrad", da, w1, sv["x_mid"], small["g_ffn"], dx_out)
    after = tick([dx_mid], [("w_ff1", d_ff1), ("w_ff2", d_ff2)])
    dmerged = _mm_cols_t("out_proj_dgrad", dx_mid, wo, F32, after=after)
    d_o = _mm_wgrad("out_proj_wgrad", sv["merged"], dx_mid, wo.shape[1], D_MODEL, True, False)
    dp, d_branch, dz = _merge_bwd(sv["z"], sv["p"], wb, dmerged)
    d_branch = d_branch.reshape(N_CHIP, N_BRANCH * WIDTH, BRANCH_COLS)
    dp, g["w_conv"] = _conv_bwd(sv["p"], small["w_conv"], dz, dp)
    dp, g["w_sg"], g["b_sg"], g["sg_ln_g"], g["sg_ln_b"] = _sg_bwd(
        sv["p"], small["sg_ln_g"], small["sg_ln_b"], small["w_sg"], small["b_sg"], dz, dp)
    after = tick([dp], [("w_branch", d_branch), ("w_o", d_o)])
    dp, g["lbs"], g["g_hgrn_out"] = _hgrn_bwd(sv["p"], sv["o_hgrn"], dz, sv["states"], small["lbs"],
                                              small["g_hgrn_out"], dp, after=after)
    after = tick([dp], [])
    d_in = _mm_wgrad("in_proj_wgrad", sv["h"], dp, D_MODEL, win.shape[2], False, True, after=after)
    dx, g["g_mix"] = _dgrad_norm_bwd("in_proj_dgrad", dp, win, sv["x"], small["g_mix"], dx_mid)
    return dx, g, tick([dx], [("w_in", d_in)])


def _mesh_pos():
    return lax.axis_index("x"), lax.axis_index("y"), lax.axis_index("c")


def _other_chips(x, y):
    return [(1 - x, y), (x, 1 - y), (1 - x, 1 - y)]


def _remote(src, dst, send_sems, recv_sems, k, to):
    return pltpu.make_async_remote_copy(src_ref=src, dst_ref=dst, send_sem=send_sems.at[k], recv_sem=recv_sems.at[k],
                                        device_id=to, device_id_type=MESH)


def _gather_call(name, body, buf, after):
    scratch = [pltpu.SemaphoreType.DMA((7,)), pltpu.SemaphoreType.DMA((7,))]
    return pl.pallas_call(
        body, name=name, in_specs=[ANY] * (1 + len(after)), out_specs=ANY,
        out_shape=jax.ShapeDtypeStruct(buf.shape, buf.dtype), scratch_shapes=scratch, input_output_aliases={0: 0})(buf, *after)


HBM = pl.BlockSpec(memory_space=pltpu.HBM)
SEM = pl.BlockSpec(memory_space=pltpu.SEMAPHORE)
DATAFLOW = pltpu.SideEffectType.DATAFLOW_SIDE_EFFECTING


def _split_start(name, bufs, copies, n_copies, after=()):
    n = len(bufs)

    def body(*refs):
        send_sems, recv_sems = refs[n + len(after)], refs[n + len(after) + 1]
        for cp in copies(refs[:n], send_sems, recv_sems):
            cp.start()
        refs[-1][...] = jnp.zeros_like(refs[-1])

    outs = pl.pallas_call(
        body, name=name,
        out_shape=(pltpu.SemaphoreType.DMA((n_copies,)), pltpu.SemaphoreType.DMA((n_copies,)),
                   *[pltpu.HBM(b.shape, b.dtype) for b in bufs], jax.ShapeDtypeStruct((SUBLANES, LANES), F32)),
        in_specs=[HBM] * n + [ANY] * len(after),
        out_specs=(SEM, SEM, *[HBM] * n, pl.BlockSpec(memory_space=pltpu.VMEM)),
        input_output_aliases={t: 2 + t for t in range(n)},
        compiler_params=pltpu.CompilerParams(has_side_effects=DATAFLOW),
    )(*[pltpu.with_memory_space_constraint(b, pltpu.HBM) for b in bufs], *after)
    return outs[0], outs[1], list(outs[2:2 + n]), outs[-1]


def _split_wait(name, started, copies, after):
    send_sems, recv_sems, bufs, _ = started
    n = len(bufs)

    def body(*refs):
        for cp in copies(refs[:n], refs[n], refs[n + 1]):
            cp.wait_send()
            cp.wait_recv()

    return list(pl.pallas_call(
        body, name=name, out_shape=tuple(pltpu.HBM(b.shape, b.dtype) for b in bufs),
        in_specs=[HBM] * n + [SEM, SEM] + [ANY] * len(after), out_specs=tuple([HBM] * n),
        input_output_aliases={t: t for t in range(n)},
        compiler_params=pltpu.CompilerParams(has_side_effects=DATAFLOW),
    )(*bufs, send_sems, recv_sems, *after))


def _weight_ici_copies(refs, send_sems, recv_sems):
    x, y, c = _mesh_pos()
    out = []
    for t, ref in enumerate(refs):
        rh = ref.shape[1] // 2
        mine = ref.at[2 * x + y, pl.ds(c * rh, rh), :]
        out += [_remote(mine, mine, send_sems, recv_sems, 3 * t + j, (*chip, c)) for j, chip in enumerate(_other_chips(x, y))]
    return out


def _weight_d2d_copies(refs, send_sems, recv_sems):
    x, y, c = _mesh_pos()
    out = []
    for t, ref in enumerate(refs):
        rh = ref.shape[1] // 2
        for j, chip in enumerate(_other_chips(x, y)):
            blk = ref.at[2 * chip[0] + chip[1], pl.ds(c * rh, rh), :]
            out.append(_remote(blk, blk, send_sems, recv_sems, 3 * t + j, (x, y, 1 - c)))
    return out


def _swap_part(refs, send_sems, recv_sems, s0):
    x, y, c = _mesh_pos()
    n = len(refs) // 2
    out = []
    for t in range(n):
        rh = refs[t].shape[1] // 2
        out.append(_remote(refs[t].at[:, pl.ds((1 - c) * rh, rh), :], refs[n + t], send_sems, recv_sems, s0 + t, (x, y, 1 - c)))
    return out


def _exchange_part(refs, send_sems, recv_sems, s0):
    x, y, c = _mesh_pos()
    n = len(refs) // 2
    out = []
    for t in range(n):
        for j, chip in enumerate(_other_chips(x, y)):
            out.append(_remote(refs[t].at[2 * chip[0] + chip[1]], refs[n + t].at[j], send_sems, recv_sems, s0 + 3 * t + j,
                               (*chip, c)))
    return out


def _gather_part(refs, send_sems, recv_sems, s0):
    x, y, c = _mesh_pos()
    return [_remote(ref.at[c], ref.at[c], send_sems, recv_sems, s0 + t, (x, y, 1 - c)) for t, ref in enumerate(refs)]


def _all_to_all_copies(refs, send_sems, recv_sems):
    x, y, c = _mesh_pos()
    blk = refs[0].at[4 * x + 2 * y + c]
    peers = [(x, y, 1 - c)] + [(*chip, cc) for chip in _other_chips(x, y) for cc in (c, 1 - c)]
    return [_remote(blk, blk, send_sems, recv_sems, k, peer) for k, peer in enumerate(peers)]


class _GradPipeline:
    def __init__(self, pos):
        self.pos = pos
        self.groups, self.pending, self.count = [], None, 0
        self.reduced = {n: [None] * DEPTH for n in BIG_NAMES}

    def busy(self):
        return bool(self.groups) or self.pending is not None

    def tick(self, deps, new):
        if self.pending is not None:
            started, copies, owners = self.pending
            bufs = _split_wait("grad_pipe_wait_%d" % self.count, started, copies, after=list(deps))
            for grp, lo, hi in owners:
                grp["bufs"] = bufs[lo:hi]
            self.pending = None
        parts = []
        for grp in list(self.groups):
            n, names = len(grp["names"]), grp["names"]
            if grp["stage"] == "swap":
                pair = [_pair_sum("grad_pair_sum_" + nm, f, r, self.pos)
                        for nm, f, r in zip(names, grp["bufs"][:n], grp["bufs"][n:])]
                grp["own32"] = [p32 for p32, _ in pair]
                landing = [lax.empty((3, *p16.shape[1:]), BF16) for _, p16 in pair]
                grp["stage"] = "exchange"
                parts.append((grp, [p16 for _, p16 in pair] + landing, _exchange_part, 3 * n))
            elif grp["stage"] == "exchange":
                halves = [_chip_sum("grad_chip_sum_" + nm, p32, r, self.pos)
                          for nm, p32, r in zip(names, grp["own32"], grp["bufs"][n:])]
                grp["stage"] = "gather"
                parts.append((grp, halves, _gather_part, n))
            else:
                for nm, b in zip(names, grp["bufs"]):
                    self.reduced[nm][grp["layer"]] = b.reshape(-1, b.shape[-1])
                self.groups.remove(grp)
        if new:
            grp = dict(names=[nm for nm, _, _ in new], layer=new[0][1], stage="swap")
            self.groups.append(grp)
            fulls = [g for _, _, g in new]
            landing = [lax.empty((N_CHIP, g.shape[1] // 2, g.shape[2]), F32) for g in fulls]
            parts.append((grp, fulls + landing, _swap_part, len(fulls)))
        if not parts:
            return ()
        bufs, layout, owners, sems = [], [], [], 0
        for grp, part_bufs, fn, n_sems in parts:
            layout.append((len(bufs), len(bufs) + len(part_bufs), fn, sems))
            owners.append((grp, len(bufs), len(bufs) + len(part_bufs)))
            bufs += part_bufs
            sems += n_sems

        def copies(refs, send_sems, recv_sems):
            out = []
            for lo, hi, fn, s0 in layout:
                out += fn(refs[lo:hi], send_sems, recv_sems, s0)
            return out

        started = _split_start("grad_pipe_start_%d" % self.count, bufs, copies, sems)
        self.pending = (started, copies, owners)
        self.count += 1
        return (started[3],)


def _gather_all(name, block, slot, after=()):
    buf = lax.dynamic_update_slice(jnp.zeros((8, *block.shape), block.dtype), block[None], (slot, 0, 0))

    def body(*refs):
        out_ref, send_sems, recv_sems = refs[1 + len(after):]
        x, y, c = _mesh_pos()
        chips = _other_chips(x, y)
        sibling = (x, y, 1 - c)
        slot_of = lambda px, py, pc: out_ref.at[4 * px + 2 * py + pc]
        started = [_remote(slot_of(x, y, c), slot_of(x, y, c), send_sems, recv_sems, 0, sibling)]
        started += [_remote(slot_of(x, y, c), slot_of(x, y, c), send_sems, recv_sems, 1 + j, (*chip, c))
                    for j, chip in enumerate(chips)]
        for cp in started:
            cp.start()
        for j, chip in enumerate(chips):
            _remote(slot_of(*chip, c), slot_of(*chip, c), send_sems, recv_sems, 1 + j, (*chip, c)).wait_recv()
            fw = _remote(slot_of(*chip, c), slot_of(*chip, c), send_sems, recv_sems, 4 + j, sibling)
            fw.start()
            started.append(fw)
        _remote(slot_of(x, y, 1 - c), slot_of(x, y, 1 - c), send_sems, recv_sems, 0, sibling).wait_recv()
        for j, chip in enumerate(chips):
            _remote(slot_of(*chip, 1 - c), slot_of(*chip, 1 - c), send_sems, recv_sems, 4 + j, sibling).wait_recv()
        for cp in started:
            cp.wait_send()

    return _gather_call(name, body, buf, after)


def _row_tile(rows, cols, block_bytes=ELEMWISE_BLOCK_BYTES):
    cap = max(SUBLANES, block_bytes // (4 * cols))
    tr = rows
    while tr > cap and tr % 2 == 0:
        tr //= 2
    return tr


def _pair_sum(name, grad, recv, pos):
    _, rh, cols = recv.shape
    tr = _row_tile(rh, cols, 2 * ELEMWISE_BLOCK_BYTES)
    per = rh // tr

    def body(pos_ref, g_ref, r_ref, own_ref, out16_ref):
        s = g_ref[...] + r_ref[...]
        out16_ref[...] = s.astype(BF16)

        @pl.when(pl.program_id(1) == pos_ref[0])
        def _():
            own_ref[...] = s

    blk = pl.BlockSpec((None, tr, cols), lambda i, k, pos_ref: (k, i, 0))
    return pl.pallas_call(
        body, name=name,
        grid_spec=pltpu.PrefetchScalarGridSpec(
            num_scalar_prefetch=1, grid=(per, N_CHIP),
            in_specs=[pl.BlockSpec((None, tr, cols), lambda i, k, pos_ref: (k, pos_ref[1] * per + i, 0)), blk],
            out_specs=[pl.BlockSpec((tr, cols), lambda i, k, pos_ref: (i, 0)), blk]),
        out_shape=[jax.ShapeDtypeStruct((rh, cols), F32), jax.ShapeDtypeStruct(recv.shape, BF16)],
        compiler_params=_params(2))(pos, grad, recv)


def _chip_sum(name, own32, recv, pos):
    rh, cols = own32.shape
    tr = _row_tile(rh, cols, 2 * ELEMWISE_BLOCK_BYTES)

    def body(pos_ref, own_ref, r_ref, out_ref):
        del pos_ref
        out_ref[...] = ((own_ref[...] + r_ref[0].astype(F32)) + r_ref[1].astype(F32)) + r_ref[2].astype(F32)

    return pl.pallas_call(
        body, name=name,
        grid_spec=pltpu.PrefetchScalarGridSpec(
            num_scalar_prefetch=1, grid=(rh // tr,),
            in_specs=[pl.BlockSpec((tr, cols), lambda i, pos_ref: (i, 0)),
                      pl.BlockSpec((3, tr, cols), lambda i, pos_ref: (0, i, 0))],
            out_specs=pl.BlockSpec((None, tr, cols), lambda i, pos_ref: (pos_ref[1], i, 0))),
        out_shape=jax.ShapeDtypeStruct((2, rh, cols), F32), compiler_params=_params(1))(pos, own32, recv)


def _cast_into_slot(name, w, layer, pos, after=()):
    _, rows, cols = w.shape
    tr = _row_tile(rows, cols, 2 * ELEMWISE_BLOCK_BYTES)

    def body(pos_ref, w_ref, *rest):
        del pos_ref
        rest[-1][...] = w_ref[...].astype(BF16)

    return pl.pallas_call(
        body, name=name,
        grid_spec=pltpu.PrefetchScalarGridSpec(
            num_scalar_prefetch=1, grid=(rows // tr,),
            in_specs=[pl.BlockSpec((None, tr, cols), lambda i, pos_ref: (layer, i, 0))] + [ANY] * len(after),
            out_specs=pl.BlockSpec((None, tr, cols), lambda i, pos_ref: (pos_ref[0], i, 0))),
        out_shape=jax.ShapeDtypeStruct((N_CHIP, rows, cols), BF16), compiler_params=_params(1))(pos, w, *after)


def _adamw_math(w, g, m, v):
    m = ADAM_B1 * m + (1.0 - ADAM_B1) * g
    v = ADAM_B2 * v + (1.0 - ADAM_B2) * jnp.square(g)
    m_hat = m / (1.0 - ADAM_B1 ** ADAM_STEP)
    v_hat = v / (1.0 - ADAM_B2 ** ADAM_STEP)
    delta = -ADAM_LR * (m_hat / (jnp.sqrt(v_hat) + ADAM_EPS) + ADAM_WD * w)
    return delta, m, v


def _adamw_layers(name, w, m, v, grads, first, into=None, after=()):
    _, rows, cols = w.shape
    tr = _row_tile(rows, cols)
    n_layers = len(grads)

    def body(w_ref, m_ref, v_ref, *rest):
        g_refs, (grad_ref, d_ref, nm_ref, nv_ref) = rest[:n_layers], rest[len(rest) - 4:]
        layer = pl.program_id(0)
        g = g_refs[0][...]
        for l in range(1, n_layers):
            g = jnp.where(layer == l, g_refs[l][...], g)
        grad_ref[...] = g
        d_ref[...], nm_ref[...], nv_ref[...] = _adamw_math(w_ref[...], g, m_ref[...], v_ref[...])

    blk = pl.BlockSpec((None, tr, cols), lambda l, i: (first + l, i, 0))
    g_spec = lambda k: pl.BlockSpec((tr, cols), lambda l, i: (jnp.where(l == k, i, 0), 0))
    passed = list(into or []) + list(after)
    return pl.pallas_call(
        body, name=name, grid=(n_layers, rows // tr),
        in_specs=[blk, blk, blk] + [g_spec(k) for k in range(n_layers)] + [ANY] * len(passed), out_specs=[blk] * 4,
        out_shape=[jax.ShapeDtypeStruct(w.shape, F32)] * 4,
        input_output_aliases={3 + n_layers + t: t for t in range(4)} if into else {},
        compiler_params=_params(2))(w, m, v, *grads, *passed)


def _sum_devices(gathered):
    _, rows, cols = gathered.shape

    def body(g_ref, out_ref):
        s = g_ref[0]
        for d in range(1, 8):
            s = s + g_ref[d]
        out_ref[...] = s

    return pl.pallas_call(body, name="sum_devices", out_shape=jax.ShapeDtypeStruct((rows, cols), F32),
                          compiler_params=pltpu.CompilerParams(vmem_limit_bytes=VMEM_LIMIT_BYTES))(gathered)


def _adamw_flat(w, g, m, v):
    def body(w_ref, g_ref, m_ref, v_ref, d_ref, nm_ref, nv_ref):
        d_ref[...], nm_ref[...], nv_ref[...] = _adamw_math(w_ref[...], g_ref[...], m_ref[...], v_ref[...])

    return pl.pallas_call(body, name="adamw_small", out_shape=[jax.ShapeDtypeStruct(w.shape, F32)] * 3,
                          compiler_params=pltpu.CompilerParams(vmem_limit_bytes=VMEM_LIMIT_BYTES))(w, g, m, v)


SMALL_NAMES = ["g_mix", "lower_bounds", "g_hgrn_out", "w_conv", "sg_ln_g", "sg_ln_b", "w_sg", "b_sg", "g_ffn", "g_final"]
BIG_NAMES = ["w_in", "w_branch", "w_o", "w_ff1", "w_ff2"]
WEIGHT_ORDER = ["w_in", "g_mix", "lower_bounds", "g_hgrn_out", "w_conv", "sg_ln_g", "sg_ln_b", "w_sg", "b_sg", "w_branch",
                "w_o", "g_ffn", "w_ff1", "w_ff2", "g_final"]


def _padded_rows(n):
    return -(-n // SUBLANES) * SUBLANES


def _pack(arrays):
    parts = []
    for a in arrays:
        a = a.reshape(-1, LANES)
        parts.append(jnp.pad(a, ((0, _padded_rows(a.shape[0]) - a.shape[0]), (0, 0))))
    return jnp.concatenate(parts, axis=0)


def _unpack(flat, shapes):
    out, row = [], 0
    for s in shapes:
        n = int(np.prod(s)) // LANES
        out.append(flat[row:row + n].reshape(s))
        row += _padded_rows(n)
    return out


def _as_2d(name, a):
    return a.reshape(DEPTH, N_BRANCH * WIDTH, BRANCH_COLS) if name == "w_branch" else a


def kernel(x, w_in, g_mix, lower_bounds, g_hgrn_out, w_conv, sg_ln_g, sg_ln_b, w_sg, b_sg, w_branch, w_o, g_ffn, w_ff1, w_ff2, g_final, loss_target, m_w_in, m_g_mix, m_lower_bounds, m_g_hgrn_out, m_w_conv, m_sg_ln_g, m_sg_ln_b, m_w_sg, m_b_sg, m_w_branch, m_w_o, m_g_ffn, m_w_ff1, m_w_ff2, m_g_final, v_w_in, v_g_mix, v_lower_bounds, v_g_hgrn_out, v_w_conv, v_sg_ln_g, v_sg_ln_b, v_w_sg, v_b_sg, v_w_branch, v_w_o, v_g_ffn, v_w_ff1, v_w_ff2, v_g_final):
    weights = dict(w_in=w_in, g_mix=g_mix, lower_bounds=lower_bounds, g_hgrn_out=g_hgrn_out, w_conv=w_conv,
                   sg_ln_g=sg_ln_g, sg_ln_b=sg_ln_b, w_sg=w_sg, b_sg=b_sg, w_branch=w_branch, w_o=w_o, g_ffn=g_ffn,
                   w_ff1=w_ff1, w_ff2=w_ff2, g_final=g_final)
    mom1 = dict(w_in=m_w_in, g_mix=m_g_mix, lower_bounds=m_lower_bounds, g_hgrn_out=m_g_hgrn_out, w_conv=m_w_conv,
                sg_ln_g=m_sg_ln_g, sg_ln_b=m_sg_ln_b, w_sg=m_w_sg, b_sg=m_b_sg, w_branch=m_w_branch, w_o=m_w_o,
                g_ffn=m_g_ffn, w_ff1=m_w_ff1, w_ff2=m_w_ff2, g_final=m_g_final)
    mom2 = dict(w_in=v_w_in, g_mix=v_g_mix, lower_bounds=v_lower_bounds, g_hgrn_out=v_g_hgrn_out, w_conv=v_w_conv,
                sg_ln_g=v_sg_ln_g, sg_ln_b=v_sg_ln_b, w_sg=v_w_sg, b_sg=v_b_sg, w_branch=v_w_branch, w_o=v_w_o,
                g_ffn=v_g_ffn, w_ff1=v_w_ff1, w_ff2=v_w_ff2, g_final=v_g_final)
    xi, yi, ci = _mesh_pos()
    pos = jnp.stack([2 * xi + yi, ci]).astype(jnp.int32)
    device = 4 * xi + 2 * yi + ci
    conv_cols = w_conv.shape[2]

    conv_all = _gather_all("gather_w_conv", w_conv.reshape(DEPTH * CONV_K, conv_cols), device)
    conv_full = conv_all.reshape(N_CHIP, 2, DEPTH, CONV_K, conv_cols)[:, 0].transpose(1, 2, 0, 3).reshape(DEPTH, CONV_K, WIDTH)

    ici, d2d = {}, {}
    cast = lambda l, names, after: [_cast_into_slot("cast_" + n, _as_2d(n, weights[n]), l, pos, after=after) for n in names]
    token = (conv_all,)
    for l in range(DEPTH):
        for part, names in (("w_in", BIG_NAMES[:1]), ("rest", BIG_NAMES[1:])):
            ici[l, part] = _split_start("weights_ici_start_%d_%s" % (l, part), cast(l, names, token), _weight_ici_copies,
                                        3 * len(names), after=token)
            token = (ici[l, part][3],)
    lbs = _lbs_fwd(lower_bounds)

    def forward_to_sibling(l, part, deps):
        landed = _split_wait("weights_ici_wait_%d_%s" % (l, part), ici.pop((l, part)), _weight_ici_copies, after=deps)
        d2d[l, part] = _split_start("weights_d2d_start_%d_%s" % (l, part), landed, _weight_d2d_copies, 3 * len(landed))
        return (d2d[l, part][3],)

    def gathered(l, part, deps):
        return _split_wait("weights_d2d_wait_%d_%s" % (l, part), d2d.pop((l, part)), _weight_d2d_copies, after=deps)

    act = x[0]
    normed = _rms_fwd("rms_mix", act, g_mix[0:1], after=token)
    layers = []
    forward_to_sibling(0, "w_in", [normed, lbs])
    for l in range(DEPTH):
        small = dict(g_mix=g_mix[l:l + 1], lbs=lbs[l:l + 1], g_hgrn_out=g_hgrn_out[l:l + 1], w_conv=conv_full[l],
                     sg_ln_g=sg_ln_g[l:l + 1], sg_ln_b=sg_ln_b[l:l + 1], w_sg=w_sg[l],
                     b_sg=b_sg[l].reshape(SG_GROUPS, SG_CHUNK, 1), g_ffn=g_ffn[l:l + 1])
        hooks = dict(after_hgrn=lambda deps, l=l: forward_to_sibling(l, "rest", deps),
                     late_weights=lambda deps, l=l: gathered(l, "rest", deps),
                     before_last=(lambda deps, l=l: forward_to_sibling(l + 1, "w_in", deps)) if l + 1 < DEPTH
                     else (lambda deps: ()))
        act, normed, saved, wts = _layer_fwd(act, normed, gathered(l, "w_in", [act])[0], small,
                                             g_mix[l + 1:l + 2] if l + 1 < DEPTH else None, hooks)
        layers.append((wts, small, saved))
    loss_blk, dact, dg_final = _loss_head(act, g_final.reshape(1, D_MODEL), loss_target[0])

    pipe = _GradPipeline(pos)
    small_grads = [None] * DEPTH
    after = ()
    for l in reversed(range(DEPTH)):
        wts, small, saved = layers[l]
        tick = lambda deps, new, l=l: pipe.tick(deps, [(nm, l, g) for nm, g in new])
        dact, small_grads[l], after = _layer_bwd(dact, saved, wts, small, tick, after)
    grad_x = dact[None]

    stack = lambda key, rows=None: jnp.stack([small_grads[l][key][0] if rows is None else small_grads[l][key][:rows]
                                              for l in range(DEPTH)])
    local_small = dict(
        g_mix=stack("g_mix"), lower_bounds=stack("lbs"), g_hgrn_out=stack("g_hgrn_out"), w_conv=stack("w_conv", CONV_K),
        sg_ln_g=stack("sg_ln_g"), sg_ln_b=stack("sg_ln_b"), w_sg=jnp.stack([small_grads[l]["w_sg"] for l in range(DEPTH)]),
        b_sg=jnp.stack([small_grads[l]["b_sg"].reshape(SG_GROUPS, SG_CHUNK) for l in range(DEPTH)]),
        g_ffn=stack("g_ffn"), g_final=dg_final[0])
    shapes = [local_small[n].shape for n in SMALL_NAMES] + [(SUBLANES, LANES)]
    packed = _pack([local_small[n] for n in SMALL_NAMES] + [loss_blk])
    packed = lax.dynamic_update_slice(jnp.zeros((8, *packed.shape), F32), packed[None], (device, 0, 0))
    small_flight = _split_start("small_grads_start", [packed], _all_to_all_copies, 7, after=after)

    def adam(n, first, layer_grads, into=None, after=()):
        return _adamw_layers("adamw_%s_%d" % (n, first), _as_2d(n, weights[n]), _as_2d(n, mom1[n]), _as_2d(n, mom2[n]),
                             layer_grads, first, into, after)

    done = {"w_ff1": adam("w_ff1", 0, pipe.reduced["w_ff1"], after=(small_flight[3],))}
    token = pipe.tick([done["w_ff1"][1]], [])
    done["w_ff2"] = adam("w_ff2", 0, pipe.reduced["w_ff2"], after=token)
    summed = _sum_devices(_split_wait("small_grads_wait", small_flight, _all_to_all_copies, after=[done["w_ff2"][1]])[0])
    parts = _unpack(summed, shapes)
    loss = parts[-1][0, 0]
    small_grad = dict(zip(SMALL_NAMES, parts[:-1]))
    small_grad["lower_bounds"] = _lbs_bwd(lower_bounds, small_grad["lower_bounds"])
    small_grad["w_conv"] = lax.dynamic_slice_in_dim(small_grad["w_conv"], pos[0] * conv_cols, conv_cols, axis=2)
    g_flat = _pack([small_grad[n] for n in SMALL_NAMES])
    d_flat, m_flat, v_flat = _adamw_flat(_pack([weights[n] for n in SMALL_NAMES]), g_flat,
                                         _pack([mom1[n] for n in SMALL_NAMES]), _pack([mom2[n] for n in SMALL_NAMES]))
    small_shapes = [weights[n].shape for n in SMALL_NAMES]
    grads = dict(small_grad)
    delta = dict(zip(SMALL_NAMES, _unpack(d_flat, small_shapes)))
    new_m = dict(zip(SMALL_NAMES, _unpack(m_flat, small_shapes)))
    new_v = dict(zip(SMALL_NAMES, _unpack(v_flat, small_shapes)))

    for n in ("w_o", "w_branch"):
        done[n] = adam(n, 0, pipe.reduced[n], after=(d_flat,))
    token = pipe.tick([done["w_branch"][1]], [])
    rest = adam("w_in", 1, pipe.reduced["w_in"][1:], after=token)
    pipe.tick([rest[1]], [])
    assert not pipe.busy()
    done["w_in"] = adam("w_in", 0, pipe.reduced["w_in"][:1], into=rest)
    for n in BIG_NAMES:
        grads[n], delta[n], new_m[n], new_v[n] = [o.reshape(weights[n].shape) for o in done[n]]

    return (loss, grad_x, *[grads[n] for n in WEIGHT_ORDER], *[delta[n] for n in WEIGHT_ORDER],
            *[new_m[n] for n in WEIGHT_ORDER], *[new_v[n] for n in WEIGHT_ORDER])
```

```python
import numpy as np
import jax
import jax.numpy as jnp
from jax import lax
from jax.experimental import pallas as pl
from jax.experimental.pallas import tpu as pltpu

F32, BF16 = jnp.float32, jnp.bfloat16

D_MODEL = 1024
WIDTH = 512
N_BRANCH = 3
N_HEAD = 4
HEAD = 128
H_CHUNK = 64
CONV_K = 3
SG_CHUNK = 128
SG_GROUPS = 4
D_FF = 4096
DEPTH = 4
N_CHIP = 4
IN_COLS = 9 * WIDTH + N_BRANCH * D_MODEL
GATE_COL0 = 9 * WIDTH
LB_FLOOR = 1e-30
NORM_EPS = 1e-6
LN_EPS = 1e-5
ADAM_LR, ADAM_B1, ADAM_B2, ADAM_EPS, ADAM_WD, ADAM_STEP = 0.001, 0.9, 0.999, 1e-08, 0.01, 10

VMEM_LIMIT_BYTES = 56 * 1024 * 1024
VMEM_BLOCK_BUDGET = 44 * 1024 * 1024
SUBLANES, LANES = 8, 128
ELEMWISE_BLOCK_BYTES = 2 * 1024 * 1024

NN = (((1,), (0,)), ((), ()))
NT = (((1,), (1,)), ((), ()))
TN = (((0,), (0,)), ((), ()))
MESH = pl.DeviceIdType.MESH
ANY = pl.BlockSpec(memory_space=pl.ANY)


def _dot(a, b, dims=NN):
    return lax.dot_general(a.astype(BF16), b.astype(BF16), dims, preferred_element_type=F32)


def _params(n_axes):
    return pltpu.CompilerParams(dimension_semantics=("arbitrary",) * n_axes, vmem_limit_bytes=VMEM_LIMIT_BYTES)


def _row0(part, rows=SUBLANES):
    r = lax.broadcasted_iota(jnp.int32, (rows, part.shape[1]), 0)
    return jnp.where(r == 0, part, 0.0)


def _token_tile(T):
    return min(512, T)


def _matmul(name, a, b, *, dims, grid, a_spec, b_spec, out_specs, out_shapes, acc_shape,
            extra=(), extra_specs=(), epilogue=None, after=()):
    nk = grid[2]
    n_extra, n_out, n_in = len(extra), len(out_shapes), 2 + len(extra) + len(after)
    one_step = nk == 1

    def body(*refs):
        a_ref, b_ref = refs[0], refs[1]
        ex = refs[2:2 + n_extra]
        outs = refs[n_in:n_in + n_out]
        part = _dot(a_ref[...], b_ref[...], dims)

        def finish(total):
            res = epilogue(total, *[e[...] for e in ex]) if epilogue else (total,)
            for o, r in zip(outs, res):
                o[...] = r.astype(o.dtype)

        if one_step:
            finish(part)
            return
        acc = refs[-1]
        kk = pl.program_id(2)

        @pl.when(kk == 0)
        def _():
            acc[...] = part

        @pl.when(kk > 0)
        def _():
            acc[...] += part

        @pl.when(kk == nk - 1)
        def _():
            finish(acc[...])

    return pl.pallas_call(
        body, name=name, grid=grid,
        in_specs=[a_spec, b_spec, *extra_specs, *[ANY] * len(after)], out_specs=list(out_specs),
        out_shape=list(out_shapes), scratch_shapes=[] if one_step else [pltpu.VMEM(acc_shape, F32)],
        compiler_params=_params(3),
    )(a, b, *extra, *after)


def _mm_cols(name, a, w, out_dtypes, epilogue=None, extra=()):
    T, K = a.shape
    N = w.shape[2]
    tm = min(2 * _token_tile(T), T)
    blk = pl.BlockSpec((tm, N), lambda j, i, kk: (i, j))
    return _matmul(
        name, a, w, dims=NN, grid=(N_CHIP, T // tm, 1),
        a_spec=pl.BlockSpec((tm, K), lambda j, i, kk: (i, 0)),
        b_spec=pl.BlockSpec((None, K, N), lambda j, i, kk: (j, 0, 0)),
        out_specs=[blk] * len(out_dtypes),
        out_shapes=[jax.ShapeDtypeStruct((T, N_CHIP * N), dt) for dt in out_dtypes],
        acc_shape=(tm, N), extra=extra, extra_specs=[blk] * len(extra), epilogue=epilogue)


def _mm_rows(name, a, w, res, norm_gain=None, after=()):
    T = a.shape[0]
    K, N = N_CHIP * w.shape[1], w.shape[2]
    tm = _token_tile(T)
    blk = pl.BlockSpec((tm, N), lambda i, j, kk: (i, 0))

    def with_norm(acc, r, gain):
        xv = acc + r
        return xv, xv * lax.rsqrt(jnp.mean(xv * xv, axis=-1, keepdims=True) + NORM_EPS) * gain

    normed = norm_gain is not None
    outs = _matmul(
        name, a, w.reshape(K, N), dims=NN, grid=(T // tm, 1, 1),
        a_spec=pl.BlockSpec((tm, K), lambda i, j, kk: (i, 0)),
        b_spec=pl.BlockSpec((K, N), lambda i, j, kk: (0, 0)),
        out_specs=[blk] * (2 if normed else 1),
        out_shapes=[jax.ShapeDtypeStruct((T, N), F32)] + ([jax.ShapeDtypeStruct((T, N), BF16)] if normed else []),
        acc_shape=(tm, N), extra=(res, norm_gain) if normed else (res,),
        extra_specs=[blk] + ([pl.BlockSpec((1, N), lambda i, j, kk: (0, 0))] if normed else []),
        epilogue=with_norm if normed else (lambda acc, r: (acc + r,)), after=after)
    return outs if normed else outs[0]


def _mm_cols_t(name, g, w, out_dtype, epilogue=None, extra=(), after=()):
    T, N = g.shape
    K = N_CHIP * w.shape[1]
    tm = _token_tile(T) if K <= 2 * D_MODEL else _token_tile(T) // 2
    blk = pl.BlockSpec((tm, K), lambda i, j, kk: (i, 0))
    return _matmul(
        name, g, w.reshape(K, N), dims=NT, grid=(T // tm, 1, 1),
        a_spec=pl.BlockSpec((tm, N), lambda i, j, kk: (i, 0)),
        b_spec=pl.BlockSpec((K, N), lambda i, j, kk: (0, 0), pipeline_mode=pl.Buffered(1)),
        out_specs=[blk], out_shapes=[jax.ShapeDtypeStruct((T, K), out_dtype)], acc_shape=(tm, K),
        extra=extra, extra_specs=[blk] * len(extra), epilogue=epilogue, after=after)[0]


def _dgrad_norm_bwd(name, g, w, x, gain, dres):
    T = g.shape[0]
    K, N = w.shape[1], w.shape[2]
    tm = _token_tile(T)
    whole = w.size * w.dtype.itemsize <= VMEM_BLOCK_BUDGET // 2

    def norm_bwd(i, dhv, x_ref, gain_ref, dres_ref, dx_ref, dgain_ref):
        xv = x_ref[...]
        r = lax.rsqrt(jnp.mean(xv * xv, axis=-1, keepdims=True) + NORM_EPS)
        xn = xv * r
        dxn = dhv * gain_ref[...]
        dx_ref[...] = dres_ref[...] + r * (dxn - xn * jnp.mean(dxn * xn, axis=-1, keepdims=True))

        @pl.when(i == 0)
        def _():
            dgain_ref[...] = jnp.zeros_like(dgain_ref)

        dgain_ref[...] += _row0(jnp.sum(dhv * xn, axis=0, keepdims=True))

    def body_whole(g_ref, w_ref, *rest):
        dhv = _dot(g_ref[:, pl.ds(0, N)], w_ref[0], NT)
        for k in range(1, N_CHIP):
            dhv = dhv + _dot(g_ref[:, pl.ds(k * N, N)], w_ref[k], NT)
        norm_bwd(pl.program_id(0), dhv, *rest)

    def body_steps(g_ref, w_ref, x_ref, gain_ref, dres_ref, dx_ref, dgain_ref, acc):
        kk = pl.program_id(1)
        part = _dot(g_ref[...], w_ref[...], NT)

        @pl.when(kk == 0)
        def _():
            acc[...] = part

        @pl.when(kk > 0)
        def _():
            acc[...] += part

        @pl.when(kk == N_CHIP - 1)
        def _():
            norm_bwd(pl.program_id(0), acc[...], x_ref, gain_ref, dres_ref, dx_ref, dgain_ref)

    tile = pl.BlockSpec((tm, K), lambda i, *kk: (i, 0))
    row = lambda cols: pl.BlockSpec((SUBLANES if cols is None else 1, K), lambda i, *kk: (0, 0))
    if whole:
        g_spec = pl.BlockSpec((tm, N_CHIP * N), lambda i: (i, 0))
        w_spec = pl.BlockSpec(w.shape, lambda i: (0, 0, 0), pipeline_mode=pl.Buffered(1))
    else:
        g_spec = pl.BlockSpec((tm, N), lambda i, kk: (i, kk))
        w_spec = pl.BlockSpec((None, K, N), lambda i, kk: (kk, 0, 0))
    return pl.pallas_call(
        body_whole if whole else body_steps, name=name, grid=(T // tm,) if whole else (T // tm, N_CHIP),
        in_specs=[g_spec, w_spec, tile, row(1), tile], out_specs=[tile, row(None)],
        out_shape=[jax.ShapeDtypeStruct((T, K), F32), jax.ShapeDtypeStruct((SUBLANES, K), F32)],
        scratch_shapes=[] if whole else [pltpu.VMEM((tm, K), F32)],
        compiler_params=_params(1 if whole else 2))(g, w, x, gain, dres)


def _mm_wgrad(name, a, g, a_cols, g_cols, a_blocked, g_blocked, after=()):
    T = a.shape[0]
    tt = T
    while tt > LANES and 2 * 2 * tt * (a_cols + g_cols) + (2 if tt == T else 3) * 4 * a_cols * g_cols > VMEM_BLOCK_BUDGET:
        tt //= 2
    return _matmul(
        name, a, g, dims=TN, grid=(N_CHIP, 1, T // tt),
        a_spec=pl.BlockSpec((tt, a_cols), (lambda j, i, kk: (kk, j)) if a_blocked else (lambda j, i, kk: (kk, 0))),
        b_spec=pl.BlockSpec((tt, g_cols), (lambda j, i, kk: (kk, j)) if g_blocked else (lambda j, i, kk: (kk, 0))),
        out_specs=[pl.BlockSpec((None, a_cols, g_cols), lambda j, i, kk: (j, 0, 0))],
        out_shapes=[jax.ShapeDtypeStruct((N_CHIP, a_cols, g_cols), F32)], acc_shape=(a_cols, g_cols), after=after)[0]


def _rms_fwd(name, x, g, after=()):
    T, Dm = x.shape
    tm = min(256, T)

    def body(x_ref, g_ref, *rest):
        xv = x_ref[...]
        r = lax.rsqrt(jnp.mean(xv * xv, axis=-1, keepdims=True) + NORM_EPS)
        rest[-1][...] = (xv * r * g_ref[...]).astype(BF16)

    return pl.pallas_call(
        body, name=name, grid=(T // tm,),
        in_specs=[pl.BlockSpec((tm, Dm), lambda i: (i, 0)), pl.BlockSpec((1, Dm), lambda i: (0, 0))] + [ANY] * len(after),
        out_specs=pl.BlockSpec((tm, Dm), lambda i: (i, 0)),
        out_shape=jax.ShapeDtypeStruct((T, Dm), BF16), compiler_params=_params(1))(x, g, *after)


def _loss_head(x, g, tgt):
    T, Dm = x.shape
    tm = min(256, T)

    def body(x_ref, g_ref, t_ref, loss_ref, dx_ref, dg_ref):
        xv = x_ref[...]
        gv = g_ref[...]
        r = lax.rsqrt(jnp.mean(xv * xv, axis=-1, keepdims=True) + NORM_EPS)
        xn = xv * r
        err = xn * gv - t_ref[...]
        dy = err * (1.0 / Dm)
        dxn = dy * gv
        dx_ref[...] = r * (dxn - xn * jnp.mean(dxn * xn, axis=-1, keepdims=True))

        @pl.when(pl.program_id(0) == 0)
        def _():
            dg_ref[...] = jnp.zeros_like(dg_ref)
            loss_ref[...] = jnp.zeros_like(loss_ref)

        dg_ref[...] += _row0(jnp.sum(dy * xn, axis=0, keepdims=True))
        part = jnp.sum(jnp.sum(err * err, axis=-1, keepdims=True), axis=0, keepdims=True) * (0.5 / Dm)
        loss_ref[...] += jnp.broadcast_to(part, loss_ref.shape)

    tile = pl.BlockSpec((tm, Dm), lambda i: (i, 0))
    return pl.pallas_call(
        body, name="loss_head", grid=(T // tm,),
        in_specs=[tile, pl.BlockSpec((1, Dm), lambda i: (0, 0)), tile],
        out_specs=[pl.BlockSpec((SUBLANES, LANES), lambda i: (0, 0)), tile,
                   pl.BlockSpec((SUBLANES, Dm), lambda i: (0, 0))],
        out_shape=[jax.ShapeDtypeStruct((SUBLANES, LANES), F32), jax.ShapeDtypeStruct((T, Dm), F32),
                   jax.ShapeDtypeStruct((SUBLANES, Dm), F32)],
        compiler_params=_params(1))(x, g, tgt)


def _softmax_rows(lb_ref):
    rows = [lb_ref[pl.ds(i, 1), :] for i in range(DEPTH)]
    mx = rows[0]
    for r in rows[1:]:
        mx = jnp.maximum(mx, r)
    es = [jnp.exp(r - mx) for r in rows]
    tot = es[0]
    for e in es[1:]:
        tot = tot + e
    return [e / tot for e in es]


def _lbs_fwd(lower_bounds):
    def body(lb_ref, out_ref):
        sm = _softmax_rows(lb_ref)
        run = jnp.zeros_like(sm[0])
        out_ref[pl.ds(0, 1), :] = run
        for i in range(1, DEPTH):
            run = run + sm[i]
            out_ref[pl.ds(i, 1), :] = run

    return pl.pallas_call(body, name="lbs_fwd", out_shape=jax.ShapeDtypeStruct(lower_bounds.shape, F32))(lower_bounds)


def _lbs_bwd(lower_bounds, dlbs):
    def body(lb_ref, d_ref, out_ref):
        sm = _softmax_rows(lb_ref)
        dsm = [jnp.zeros_like(sm[0])]
        for i in range(1, DEPTH):
            acc = d_ref[pl.ds(i, 1), :]
            for l in range(i + 1, DEPTH):
                acc = acc + d_ref[pl.ds(l, 1), :]
            dsm.append(acc)
        inner = dsm[0] * sm[0]
        for i in range(1, DEPTH):
            inner = inner + dsm[i] * sm[i]
        for i in range(DEPTH):
            out_ref[pl.ds(i, 1), :] = sm[i] * (dsm[i] - inner)

    return pl.pallas_call(body, name="lbs_bwd", out_shape=jax.ShapeDtypeStruct(lower_bounds.shape, F32))(lower_bounds, dlbs)


N_LEVEL = 6


def _hgrn_consts():
    L = H_CHUNK
    t = np.arange(L)
    blocks = [(t[:, None] >= t[None, :]).astype(np.float32)]
    masks = []
    m = L // 2
    while m >= 1:
        blk, pos = t // (2 * m), t % (2 * m)
        start = blk * 2 * m
        mat = np.zeros((L, L), np.float32)
        for r in range(L):
            if pos[r] >= m:
                mat[r, start[r] + m:r + 1] = 1.0
            else:
                mat[r, r + 1:start[r] + m] = -1.0
        blocks.append(mat)
        masks.append(((blk[:, None] == blk[None, :]) & (pos[:, None] >= m) & (pos[None, :] < m)).astype(np.float32))
        m //= 2
    blocks.append(np.ones((L, L), np.float32))
    return jnp.asarray(np.concatenate(blocks, 0), BF16), jnp.asarray(np.stack(masks), F32)


def _hgrn_core(qraw, fp, lb, sum_mat, mask_ref):
    L = H_CHUNK
    sq = jax.nn.sigmoid(qraw)
    q = qraw * sq
    sneg = jax.nn.sigmoid(-fp)
    log_sig = jnp.minimum(fp, 0.0) - jnp.log1p(jnp.exp(-jnp.abs(fp)))
    a1 = jnp.log(jnp.maximum(lb, LB_FLOOR))
    a2 = jnp.log1p(-lb) + log_sig
    logf = jnp.maximum(a1, a2) + jnp.log1p(jnp.exp(-jnp.abs(a1 - a2)))
    w1 = jnp.exp(a1 - logf)
    w2 = jnp.exp(a2 - logf)
    k = (1.0 - lb) * sneg
    hi = logf.astype(BF16)
    r1 = logf - hi.astype(F32)
    mid = r1.astype(BF16)
    lo = (r1 - mid.astype(F32)).astype(BF16)
    sums = lax.dot_general(sum_mat, jnp.concatenate([hi, mid, lo], axis=1), NN, preferred_element_type=F32)
    sums = sums[:, 0:HEAD] + sums[:, HEAD:2 * HEAD] + sums[:, 2 * HEAD:3 * HEAD]
    b = sums[0:L]
    b_last = sums[(N_LEVEL + 1) * L:(N_LEVEL + 2) * L]
    eye = lax.broadcasted_iota(jnp.int32, (L, L), 0) == lax.broadcasted_iota(jnp.int32, (L, L), 1)
    attn = jnp.where(eye, jnp.sum(q * k, axis=1, keepdims=True), 0.0)
    fa, fb, ea, eb = [], [], [], []
    for l in range(N_LEVEL):
        d = sums[(l + 1) * L:(l + 2) * L]
        e_a = jnp.exp(jnp.minimum(d, 0.0))
        e_b = jnp.exp(jnp.minimum(-d, 0.0))
        a_l, b_l = q * e_a, k * e_b
        attn = attn + mask_ref[l] * _dot(a_l, b_l, NT)
        fa.append(a_l), fb.append(b_l), ea.append(e_a), eb.append(e_b)
    return dict(sq=sq, q=q, sneg=sneg, logf=logf, w1=w1, w2=w2, k=k, b=b, b_last=b_last, attn=attn,
                fa=fa, fb=fb, ea=ea, eb=eb)


def _hgrn_fwd(p, lbrow, gout):
    T = p.shape[0]
    nch = T // H_CHUNK
    sum_mat, masks = _hgrn_consts()

    def body(p_ref, lb_ref, g_ref, m_ref, mask_ref, o_ref, z_ref, st_ref, state):
        @pl.when(pl.program_id(0) == 0)
        def _():
            state[...] = jnp.zeros_like(state)

        sum_m = m_ref[...]
        for h in range(N_HEAD):
            col = lambda part: pl.ds(part * WIDTH + h * HEAD, HEAD)
            hs = pl.ds(h * HEAD, HEAD)
            v = p_ref[:, col(2)]
            c = _hgrn_core(p_ref[:, col(0)], p_ref[:, col(1)], lb_ref[:, hs], sum_m, mask_ref)
            s0 = state[h]
            st_ref[h] = s0
            o = _dot(c["attn"], v) + _dot(c["q"] * jnp.exp(c["b"]), s0, NT)
            k_dec = c["k"] * jnp.exp(c["b_last"] - c["b"])
            decay = jnp.exp(jnp.max(c["b_last"], axis=0, keepdims=True))
            state[h] = s0 * decay + _dot(v, k_dec, TN)
            o_ref[:, hs] = o
            r = lax.rsqrt(jnp.mean(o * o, axis=-1, keepdims=True) + NORM_EPS)
            z_ref[:, hs] = (o * r * g_ref[:, hs] * jax.nn.sigmoid(p_ref[:, col(3)])).astype(BF16)

    full = lambda shape: pl.BlockSpec(shape, lambda c: (0,) * len(shape))
    return pl.pallas_call(
        body, name="hgrn_fwd", grid=(nch,),
        in_specs=[pl.BlockSpec((H_CHUNK, 4 * WIDTH), lambda c: (c, 0)), full((1, WIDTH)), full((1, WIDTH)),
                  full(sum_mat.shape), full(masks.shape)],
        out_specs=[pl.BlockSpec((H_CHUNK, WIDTH), lambda c: (c, 0)),
                   pl.BlockSpec((None, H_CHUNK, WIDTH), lambda c: (0, c, 0)),
                   pl.BlockSpec((None, N_HEAD, HEAD, HEAD), lambda c: (c, 0, 0, 0))],
        out_shape=[jax.ShapeDtypeStruct((T, WIDTH), F32), jax.ShapeDtypeStruct((N_BRANCH, T, WIDTH), BF16),
                   jax.ShapeDtypeStruct((nch, N_HEAD, HEAD, HEAD), F32)],
        scratch_shapes=[pltpu.VMEM((N_HEAD, HEAD, HEAD), F32)], compiler_params=_params(1),
    )(p, lbrow, gout, sum_mat, masks)


def _hgrn_bwd(p, o_saved, dz, states, lbrow, gout, dp, after=()):
    T = p.shape[0]
    nch = T // H_CHUNK
    L = H_CHUNK
    sum_mat, masks = _hgrn_consts()

    def body(p_ref, o_ref, dz_ref, st_ref, lb_ref, g_ref, m_ref, mask_ref, dp_in, *rest):
        del dp_in
        dp_ref, dlb_ref, dg_ref, dstate = rest[len(after):]

        @pl.when(pl.program_id(0) == 0)
        def _():
            dstate[...] = jnp.zeros_like(dstate)
            dlb_ref[...] = jnp.zeros_like(dlb_ref)
            dg_ref[...] = jnp.zeros_like(dg_ref)

        sum_m = m_ref[...]
        for h in range(N_HEAD):
            col = lambda part: pl.ds(part * WIDTH + h * HEAD, HEAD)
            hs = pl.ds(h * HEAD, HEAD)
            qraw, fp, v, go = p_ref[:, col(0)], p_ref[:, col(1)], p_ref[:, col(2)], p_ref[:, col(3)]
            lb, g = lb_ref[:, hs], g_ref[:, hs]
            c = _hgrn_core(qraw, fp, lb, sum_m, mask_ref)
            q, k, b, b_last = c["q"], c["k"], c["b"], c["b_last"]
            s0, ds1 = st_ref[h], dstate[h]
            e_b = jnp.exp(b)
            q_dec = q * e_b
            e_bl = jnp.exp(b_last - b)
            k_dec = k * e_bl
            decay = jnp.exp(jnp.max(b_last, axis=0, keepdims=True))
            o = o_ref[:, hs]
            r = lax.rsqrt(jnp.mean(o * o, axis=-1, keepdims=True) + NORM_EPS)
            n = o * r
            sgo = jax.nn.sigmoid(go)
            dza = dz_ref[:, hs]
            dgo = dza * n * g * sgo * (1.0 - sgo)
            dg_ref[:, hs] += _row0(jnp.sum(dza * n * sgo, axis=0, keepdims=True))
            dn = dza * g * sgo
            do = r * (dn - n * jnp.mean(dn * n, axis=-1, keepdims=True))
            dattn = _dot(do, v, NT)
            dv = _dot(c["attn"], do, TN) + _dot(k_dec, ds1, NT)
            dq_dec = _dot(do, s0)
            dk_dec = _dot(v, ds1)
            ddiag = jnp.sum(do * v, axis=1, keepdims=True)
            dq = dq_dec * e_b + ddiag * k
            dk = dk_dec * e_bl + ddiag * q
            dsums = [dq_dec * q_dec - dk_dec * k_dec]
            for l in range(N_LEVEL):
                dm = mask_ref[l] * dattn
                da = _dot(dm, c["fb"][l])
                db = _dot(dm, c["fa"][l], TN)
                dq = dq + da * c["ea"][l]
                dk = dk + db * c["eb"][l]
                dsums.append(da * c["fa"][l] - db * c["fb"][l])
            dlast = jnp.sum(ds1 * s0, axis=0, keepdims=True) * decay
            dsums.append(dk_dec * k_dec + _row0(dlast, L))
            dlogf = _dot(sum_m, jnp.concatenate(dsums, axis=0), TN)
            dstate[h] = ds1 * decay + _dot(do, q_dec, TN)
            sq, sneg = c["sq"], c["sneg"]
            dqraw = dq * sq * (1.0 + qraw * (1.0 - sq))
            dfp = dlogf * c["w2"] * sneg - dk * (1.0 - lb) * sneg * (1.0 - sneg)
            inv_lb = jnp.where(lb > LB_FLOOR, 1.0 / jnp.maximum(lb, LB_FLOOR), 0.0)
            dlb_tok = dlogf * (c["w1"] * inv_lb - c["w2"] / (1.0 - lb)) - dk * sneg
            dlb_ref[:, hs] += _row0(jnp.sum(dlb_tok, axis=0, keepdims=True))
            dp_ref[:, col(0)] = dqraw.astype(BF16)
            dp_ref[:, col(1)] = dfp.astype(BF16)
            dp_ref[:, col(2)] = dv.astype(BF16)
            dp_ref[:, col(3)] = dgo.astype(BF16)

    full = lambda shape: pl.BlockSpec(shape, lambda c: (0,) * len(shape))
    rev = lambda c: nch - 1 - c
    return pl.pallas_call(
        body, name="hgrn_bwd", grid=(nch,),
        in_specs=[pl.BlockSpec((L, 4 * WIDTH), lambda c: (rev(c), 0)), pl.BlockSpec((L, WIDTH), lambda c: (rev(c), 0)),
                  pl.BlockSpec((None, L, WIDTH), lambda c: (0, rev(c), 0)),
                  pl.BlockSpec((None, N_HEAD, HEAD, HEAD), lambda c: (rev(c), 0, 0, 0)),
                  full((1, WIDTH)), full((1, WIDTH)), full(sum_mat.shape), full(masks.shape), ANY, *[ANY] * len(after)],
        out_specs=[pl.BlockSpec((L, 4 * WIDTH), lambda c: (rev(c), 0)), full((SUBLANES, WIDTH)), full((SUBLANES, WIDTH))],
        out_shape=[jax.ShapeDtypeStruct(dp.shape, dp.dtype), jax.ShapeDtypeStruct((SUBLANES, WIDTH), F32),
                   jax.ShapeDtypeStruct((SUBLANES, WIDTH), F32)],
        scratch_shapes=[pltpu.VMEM((N_HEAD, HEAD, HEAD), F32)], input_output_aliases={8: 0},
        compiler_params=_params(1),
    )(p, o_saved, dz, states, lbrow, gout, sum_mat, masks, dp, *after)


def _shift_down(tile, halo, s):
    tm = tile.shape[0]
    rows = lax.broadcasted_iota(jnp.int32, tile.shape, 0)
    head = jnp.concatenate([pltpu.roll(halo, s, 0), jnp.zeros((tm - SUBLANES, tile.shape[1]), tile.dtype)], axis=0)
    return jnp.where(rows < s, head, pltpu.roll(tile, s, 0))


def _shift_up(tile, halo, s):
    tm = tile.shape[0]
    rows = lax.broadcasted_iota(jnp.int32, tile.shape, 0)
    tail = jnp.concatenate([jnp.zeros((tm - SUBLANES, tile.shape[1]), tile.dtype), pltpu.roll(halo, SUBLANES - s, 0)], axis=0)
    return jnp.where(rows >= tm - s, tail, pltpu.roll(tile, tm - s, 0))


def _conv_fwd(p, w, z, after=()):
    T = p.shape[0]
    tm = min(2 * _token_tile(T), T)
    per = tm // SUBLANES

    def body(bg_ref, cg_ref, xc_ref, hcg_ref, hxc_ref, w_ref, *rest):
        z_ref = rest[-1]
        zc = cg_ref[...] * xc_ref[...]
        hz = jnp.where(pl.program_id(0) > 0, hcg_ref[...] * hxc_ref[...], 0.0)
        y = (w_ref[pl.ds(0, 1), :] * _shift_down(zc, hz, 2) + w_ref[pl.ds(1, 1), :] * _shift_down(zc, hz, 1)
             + w_ref[pl.ds(2, 1), :] * zc)
        z_ref[...] = (bg_ref[...] * y).astype(BF16)

    tile = lambda cb: pl.BlockSpec((tm, WIDTH), lambda i: (i, cb))
    prev = lambda cb: pl.BlockSpec((SUBLANES, WIDTH), lambda i: (jnp.maximum(i * per - 1, 0), cb))
    return pl.pallas_call(
        body, name="conv_fwd", grid=(T // tm,),
        in_specs=[tile(4), tile(5), tile(6), prev(5), prev(6), pl.BlockSpec((CONV_K, WIDTH), lambda i: (0, 0)), ANY,
                  *[ANY] * len(after)],
        out_specs=pl.BlockSpec((None, tm, WIDTH), lambda i: (1, i, 0)),
        out_shape=jax.ShapeDtypeStruct(z.shape, z.dtype), input_output_aliases={6: 0}, compiler_params=_params(1),
    )(p, p, p, p, p, w, z, *after)


def _conv_bwd(p, w, dz, dp):
    T = p.shape[0]
    tm = min(2 * _token_tile(T), T)
    per = tm // SUBLANES
    last = T // SUBLANES - 1

    def body(bg_ref, cg_ref, xc_ref, hcg_ref, hxc_ref, nbg_ref, dzb_ref, ndzb_ref, w_ref, dp_in, dp_ref, dw_ref, stash):
        del dp_in
        i, jj = pl.program_id(0), pl.program_id(1)

        @pl.when(jnp.logical_and(i == 0, jj == 0))
        def _():
            dw_ref[...] = jnp.zeros_like(dw_ref)

        @pl.when(jj == 0)
        def _():
            cg, xc, bg = cg_ref[...], xc_ref[...], bg_ref[...]
            w0, w1, w2 = w_ref[pl.ds(0, 1), :], w_ref[pl.ds(1, 1), :], w_ref[pl.ds(2, 1), :]
            zc = cg * xc
            hz = jnp.where(i > 0, hcg_ref[...] * hxc_ref[...], 0.0)
            z2, z1 = _shift_down(zc, hz, 2), _shift_down(zc, hz, 1)
            y = w0 * z2 + w1 * z1 + w2 * zc
            dzb = dzb_ref[...]
            dy = dzb * bg
            hdy = jnp.where(i < pl.num_programs(0) - 1, ndzb_ref[...] * nbg_ref[...], 0.0)
            dzc = w2 * dy + w1 * _shift_up(dy, hdy, 1) + w0 * _shift_up(dy, hdy, 2)
            rows = lax.broadcasted_iota(jnp.int32, (SUBLANES, WIDTH), 0)
            colsum = lambda t: jnp.sum(t, axis=0, keepdims=True)
            dw_ref[...] += (jnp.where(rows == 0, colsum(dy * z2), 0.0) + jnp.where(rows == 1, colsum(dy * z1), 0.0)
                            + jnp.where(rows == 2, colsum(dy * zc), 0.0))
            dp_ref[...] = (dzb * y).astype(BF16)
            stash[0] = dzc * xc
            stash[1] = dzc * cg

        @pl.when(jj > 0)
        def _():
            dp_ref[...] = stash[jj - 1].astype(BF16)

    n_tiles = T // tm
    tile = lambda cb: pl.BlockSpec((tm, WIDTH), lambda i, jj: (i, cb))
    prev = lambda cb: pl.BlockSpec((SUBLANES, WIDTH), lambda i, jj: (jnp.maximum(i * per - 1, 0), cb))
    nxt = lambda i: jnp.minimum((i + 1) * per, last)
    return pl.pallas_call(
        body, name="conv_bwd", grid=(n_tiles, 3),
        in_specs=[tile(4), tile(5), tile(6), prev(5), prev(6),
                  pl.BlockSpec((SUBLANES, WIDTH), lambda i, jj: (nxt(i), 4)),
                  pl.BlockSpec((None, tm, WIDTH), lambda i, jj: (1, i, 0)),
                  pl.BlockSpec((None, SUBLANES, WIDTH), lambda i, jj: (1, nxt(i), 0)),
                  pl.BlockSpec((CONV_K, WIDTH), lambda i, jj: (0, 0)), ANY],
        out_specs=[pl.BlockSpec((tm, WIDTH), lambda i, jj: (i, 4 + jj)),
                   pl.BlockSpec((SUBLANES, WIDTH), lambda i, jj: (0, 0))],
        out_shape=[jax.ShapeDtypeStruct(dp.shape, dp.dtype), jax.ShapeDtypeStruct((SUBLANES, WIDTH), F32)],
        scratch_shapes=[pltpu.VMEM((2, tm, WIDTH), F32)], input_output_aliases={9: 0}, compiler_params=_params(2),
    )(p, p, p, p, p, p, dz, dz, w, dp)


GELU_C = float(np.sqrt(2.0 / np.pi))
GELU_A = 0.044715


def _gelu(x):
    th = jnp.tanh(GELU_C * (x + GELU_A * x * x * x))
    return 0.5 * x * (1.0 + th), th


def _gelu_grad(x, th):
    return 0.5 * (1.0 + th) + 0.5 * x * (1.0 - th * th) * GELU_C * (1.0 + 3.0 * GELU_A * x * x)


def _sg_core(u, v, lng, lnb, ws_ref, bs_ref):
    gu, thu = _gelu(u)
    gv, thv = _gelu(v)
    xc = gv - jnp.mean(gv, axis=-1, keepdims=True)
    rs = lax.rsqrt(jnp.mean(xc * xc, axis=-1, keepdims=True) + LN_EPS)
    xh = xc * rs
    vp = xh * lng + lnb
    tril = (lax.broadcasted_iota(jnp.int32, (SG_CHUNK, SG_CHUNK), 0)
            >= lax.broadcasted_iota(jnp.int32, (SG_CHUNK, SG_CHUNK), 1))
    wm = [jnp.where(tril, ws_ref[g], 0.0).astype(BF16) for g in range(SG_GROUPS)]
    gs = lambda t, g: t[:, g * LANES:(g + 1) * LANES]
    sv = jnp.concatenate([_dot(wm[g], gs(vp, g)) + bs_ref[g] for g in range(SG_GROUPS)], axis=1)
    return dict(gu=gu, thu=thu, thv=thv, rs=rs, xh=xh, vp=vp, tril=tril, wm=wm, sv=sv)


def _sg_step_rows(T):
    return 2 * SG_CHUNK if T % (2 * SG_CHUNK) == 0 else SG_CHUNK


def _sg_fwd(p, lng, lnb, ws, bs, z):
    T = p.shape[0]
    step = _sg_step_rows(T)

    def body(u_ref, v_ref, lng_ref, lnb_ref, ws_ref, bs_ref, z_in, z_ref):
        del z_in
        for sub in range(step // SG_CHUNK):
            rows = pl.ds(sub * SG_CHUNK, SG_CHUNK)
            c = _sg_core(u_ref[rows, :], v_ref[rows, :], lng_ref[...], lnb_ref[...], ws_ref, bs_ref)
            z_ref[rows, :] = (c["gu"] * c["sv"]).astype(BF16)

    full = lambda shape: pl.BlockSpec(shape, lambda c: (0,) * len(shape))
    return pl.pallas_call(
        body, name="sg_fwd", grid=(T // step,),
        in_specs=[pl.BlockSpec((step, WIDTH), lambda c: (c, 7)), pl.BlockSpec((step, WIDTH), lambda c: (c, 8)),
                  full((1, WIDTH)), full((1, WIDTH)), full(ws.shape), full(bs.shape), ANY],
        out_specs=pl.BlockSpec((None, step, WIDTH), lambda c: (2, c, 0)),
        out_shape=jax.ShapeDtypeStruct(z.shape, z.dtype), input_output_aliases={6: 0}, compiler_params=_params(1),
    )(p, p, lng, lnb, ws, bs, z)


def _sg_bwd(p, lng, lnb, ws, bs, dz, dp):
    T = p.shape[0]
    step = _sg_step_rows(T)

    def body(u_ref, v_ref, lng_ref, lnb_ref, ws_ref, bs_ref, dz_ref, dp_in, dp_ref, dws_ref, dbs_ref, dlng_ref, dlnb_ref,
             stash):
        del dp_in
        cidx, jj = pl.program_id(0), pl.program_id(1)

        @pl.when(jnp.logical_and(cidx == 0, jj == 0))
        def _():
            dws_ref[...] = jnp.zeros_like(dws_ref)
            dbs_ref[...] = jnp.zeros_like(dbs_ref)
            dlng_ref[...] = jnp.zeros_like(dlng_ref)
            dlnb_ref[...] = jnp.zeros_like(dlnb_ref)

        def chunk(rows):
            u, v, lng = u_ref[rows, :], v_ref[rows, :], lng_ref[...]
            c = _sg_core(u, v, lng, lnb_ref[...], ws_ref, bs_ref)
            dzc = dz_ref[rows, :]
            gs = lambda t, g: t[:, g * LANES:(g + 1) * LANES]
            dsv = dzc * c["gu"]
            dvp = []
            for g in range(SG_GROUPS):
                dsv_g = gs(dsv, g)
                dws_ref[g] += jnp.where(c["tril"], _dot(dsv_g, gs(c["vp"], g), NT), 0.0)
                dbs_ref[g] += jnp.sum(dsv_g, axis=1, keepdims=True)
                dvp.append(_dot(c["wm"][g], dsv_g, TN))
            dvp = jnp.concatenate(dvp, axis=1)
            xh = c["xh"]
            dlng_ref[...] += _row0(jnp.sum(dvp * xh, axis=0, keepdims=True))
            dlnb_ref[...] += _row0(jnp.sum(dvp, axis=0, keepdims=True))
            dxh = dvp * lng
            dgv = c["rs"] * (dxh - jnp.mean(dxh, axis=-1, keepdims=True) - xh * jnp.mean(dxh * xh, axis=-1, keepdims=True))
            dp_ref[rows, :] = (dzc * c["sv"] * _gelu_grad(u, c["thu"])).astype(BF16)
            stash[rows, :] = dgv * _gelu_grad(v, c["thv"])

        @pl.when(jj == 0)
        def _():
            for sub in range(step // SG_CHUNK):
                chunk(pl.ds(sub * SG_CHUNK, SG_CHUNK))

        @pl.when(jj == 1)
        def _():
            dp_ref[...] = stash[...].astype(BF16)

    full = lambda shape: pl.BlockSpec(shape, lambda c, jj: (0,) * len(shape))
    return pl.pallas_call(
        body, name="sg_bwd", grid=(T // step, 2),
        in_specs=[pl.BlockSpec((step, WIDTH), lambda c, jj: (c, 7)), pl.BlockSpec((step, WIDTH), lambda c, jj: (c, 8)),
                  full((1, WIDTH)), full((1, WIDTH)), full(ws.shape), full(bs.shape),
                  pl.BlockSpec((None, step, WIDTH), lambda c, jj: (2, c, 0)), ANY],
        out_specs=[pl.BlockSpec((step, WIDTH), lambda c, jj: (c, 7 + jj)), full(ws.shape), full(bs.shape),
                   full((SUBLANES, WIDTH)), full((SUBLANES, WIDTH))],
        out_shape=[jax.ShapeDtypeStruct(dp.shape, dp.dtype), jax.ShapeDtypeStruct(ws.shape, F32),
                   jax.ShapeDtypeStruct(bs.shape, F32), jax.ShapeDtypeStruct((SUBLANES, WIDTH), F32),
                   jax.ShapeDtypeStruct((SUBLANES, WIDTH), F32)],
        scratch_shapes=[pltpu.VMEM((step, WIDTH), F32)], input_output_aliases={7: 0}, compiler_params=_params(2),
    )(p, p, lng, lnb, ws, bs, dz, dp)


BRANCH_COLS = D_MODEL // N_CHIP
GATE_UNIT0 = GATE_COL0 // WIDTH
UNITS = D_MODEL // WIDTH


def _unit_specs(order):
    def spec(which):
        def index(*g):
            _, n, u = order(*g)
            return (2 * u + which, n, 0, 0)
        return pl.BlockSpec((None, None, WIDTH, BRANCH_COLS), index)
    return [spec(0), spec(1)]


def _merge_fwd(z, p, wb):
    T = z.shape[1]
    tm = min(2 * _token_tile(T), T)
    order = lambda i, u, n: (i, n, u)

    def body(z_ref, wa_ref, wb_ref, gt_ref, out_ref, acc):
        n = pl.program_id(2)
        zv = z_ref[...]
        y = jnp.concatenate([_dot(zv, wa_ref[...]), _dot(zv, wb_ref[...])], axis=1)
        part = jax.nn.sigmoid(gt_ref[...]) * y

        @pl.when(n == 0)
        def _():
            acc[...] = part

        @pl.when(n > 0)
        def _():
            acc[...] += part

        @pl.when(n == N_BRANCH - 1)
        def _():
            out_ref[...] = acc[...].astype(BF16)

    return pl.pallas_call(
        body, name="merge_fwd", grid=(T // tm, UNITS, N_BRANCH),
        in_specs=[pl.BlockSpec((None, tm, WIDTH), lambda i, u, n: (n, i, 0)), *_unit_specs(order),
                  pl.BlockSpec((tm, WIDTH), lambda i, u, n: (i, GATE_UNIT0 + UNITS * n + u))],
        out_specs=pl.BlockSpec((tm, WIDTH), lambda i, u, n: (i, u)),
        out_shape=jax.ShapeDtypeStruct((T, D_MODEL), BF16),
        scratch_shapes=[pltpu.VMEM((tm, WIDTH), F32)], compiler_params=_params(3))(z, wb, wb, p)


def _merge_bwd(z, p, wb, dmerged):
    T = z.shape[1]
    tm = min(2 * _token_tile(T), T)
    order = lambda n, u, i: (i, n, u)

    def body(z_ref, wa_ref, wb_ref, gt_ref, dm_ref, dp_ref, dw_ref, dz_ref):
        u, i = pl.program_id(1), pl.program_id(2)
        zv, wa, wbv = z_ref[...], wa_ref[...], wb_ref[...]
        y = jnp.concatenate([_dot(zv, wa), _dot(zv, wbv)], axis=1)
        gate = jax.nn.sigmoid(gt_ref[...])
        dm = dm_ref[...]
        dp_ref[...] = (dm * y * gate * (1.0 - gate)).astype(BF16)
        dyv = (dm * gate).astype(BF16)
        dw = _dot(zv, dyv, TN)
        part = _dot(dyv[:, :BRANCH_COLS], wa, NT) + _dot(dyv[:, BRANCH_COLS:], wbv, NT)
        rows = pl.ds(pl.multiple_of(i * tm, tm), tm)

        @pl.when(i == 0)
        def _():
            dw_ref[0] = dw[:, :BRANCH_COLS]
            dw_ref[1] = dw[:, BRANCH_COLS:]

        @pl.when(i > 0)
        def _():
            dw_ref[0] += dw[:, :BRANCH_COLS]
            dw_ref[1] += dw[:, BRANCH_COLS:]

        @pl.when(u == 0)
        def _():
            dz_ref[rows, :] = part

        @pl.when(u > 0)
        def _():
            dz_ref[rows, :] += part

    unit = lambda n, u, i: (i, GATE_UNIT0 + UNITS * n + u)
    return pl.pallas_call(
        body, name="merge_bwd", grid=(N_BRANCH, UNITS, T // tm),
        in_specs=[pl.BlockSpec((None, tm, WIDTH), lambda n, u, i: (n, i, 0)), *_unit_specs(order),
                  pl.BlockSpec((tm, WIDTH), unit), pl.BlockSpec((tm, WIDTH), lambda n, u, i: (i, u))],
        out_specs=[pl.BlockSpec((tm, WIDTH), unit),
                   pl.BlockSpec((2, None, WIDTH, BRANCH_COLS), lambda n, u, i: (u, n, 0, 0)),
                   pl.BlockSpec((None, T, WIDTH), lambda n, u, i: (n, 0, 0))],
        out_shape=[jax.ShapeDtypeStruct((T, IN_COLS), BF16),
                   jax.ShapeDtypeStruct((N_CHIP, N_BRANCH, WIDTH, BRANCH_COLS), F32),
                   jax.ShapeDtypeStruct((N_BRANCH, T, WIDTH), F32)],
        compiler_params=_params(3))(z, wb, wb, p, dmerged)


def _layer_fwd(x, h, win, small, next_gain, hooks):
    p = _mm_cols("in_proj", h, win, [F32])[0]
    o_hgrn, z, states = _hgrn_fwd(p, small["lbs"], small["g_hgrn_out"])
    z = _conv_fwd(p, small["w_conv"], z, after=hooks["after_hgrn"]([o_hgrn]))
    z = _sg_fwd(p, small["sg_ln_g"], small["sg_ln_b"], small["w_sg"], small["b_sg"], z)
    wb, wo, w1, w2 = hooks["late_weights"]([z])
    wb = wb.reshape(N_CHIP, N_BRANCH, WIDTH, BRANCH_COLS)
    merged = _merge_fwd(z, p, wb)
    x_mid, h2 = _mm_rows("out_proj", merged, wo, x, small["g_ffn"])
    s = _mm_cols("ff1", h2, w1, [BF16], epilogue=lambda acc: (jnp.square(jnp.maximum(acc, 0.0)),))[0]
    if next_gain is None:
        x_out, h_next = _mm_rows("ff2_last", s, w2, x_mid, after=hooks["before_last"]([s])), None
    else:
        x_out, h_next = _mm_rows("ff2", s, w2, x_mid, next_gain, after=hooks["before_last"]([s]))
    saved = dict(x=x, h=h, p=p, o_hgrn=o_hgrn, z=z, states=states, merged=merged, x_mid=x_mid, h2=h2, s=s)
    return x_out, h_next, saved, [win, wb, wo, w1, w2]


def _layer_bwd(dx_out, sv, wts, small, tick, after):
    win, wb, wo, w1, w2 = wts
    g = {}
    da = _mm_cols_t("ff2_dgrad", dx_out, w2, BF16, extra=(sv["s"],), after=after,
                    epilogue=lambda acc, s: (acc * 2.0 * jnp.sqrt(s.astype(F32)),))
    d_ff2 = _mm_wgrad("ff2_wgrad", sv["s"], dx_out, w2.shape[1], D_MODEL, True, False)
    d_ff1 = _mm_wgrad("ff1_wgrad", sv["h2"], da, D_MODEL, w1.shape[2], False, True)
    dx_mid, g["g_ffn"] = _dgrad_norm_bwd("ff1_dgrad", da, w1, sv["x_mid"], small["g_ffn"], dx_out)
    after = tick([dx_mid], [("w_ff1", d_ff1), ("w_ff2", d_ff2)])
    dmerged = _mm_cols_t("out_proj_dgrad", dx_mid, wo, F32, after=after)
    d_o = _mm_wgrad("out_proj_wgrad", sv["merged"], dx_mid, wo.shape[1], D_MODEL, True, False)
    dp, d_branch, dz = _merge_bwd(sv["z"], sv["p"], wb, dmerged)
    d_branch = d_branch.reshape(N_CHIP, N_BRANCH * WIDTH, BRANCH_COLS)
    dp, g["w_conv"] = _conv_bwd(sv["p"], small["w_conv"], dz, dp)
    dp, g["w_sg"], g["b_sg"], g["sg_ln_g"], g["sg_ln_b"] = _sg_bwd(
        sv["p"], small["sg_ln_g"], small["sg_ln_b"], small["w_sg"], small["b_sg"], dz, dp)
    after = tick([dp], [("w_branch", d_branch), ("w_o", d_o)])
    dp, g["lbs"], g["g_hgrn_out"] = _hgrn_bwd(sv["p"], sv["o_hgrn"], dz, sv["states"], small["lbs"],
                                              small["g_hgrn_out"], dp, after=after)
    after = tick([dp], [])
    d_in = _mm_wgrad("in_proj_wgrad", sv["h"], dp, D_MODEL, win.shape[2], False, True, after=after)
    dx, g["g_mix"] = _dgrad_norm_bwd("in_proj_dgrad", dp, win, sv["x"], small["g_mix"], dx_mid)
    return dx, g, tick([dx], [("w_in", d_in)])


def _mesh_pos():
    return lax.axis_index("x"), lax.axis_index("y"), lax.axis_index("c")


def _other_chips(x, y):
    return [(1 - x, y), (x, 1 - y), (1 - x, 1 - y)]


def _remote(src, dst, send_sems, recv_sems, k, to):
    return pltpu.make_async_remote_copy(src_ref=src, dst_ref=dst, send_sem=send_sems.at[k], recv_sem=recv_sems.at[k],
                                        device_id=to, device_id_type=MESH)


def _gather_call(name, body, buf, after):
    scratch = [pltpu.SemaphoreType.DMA((7,)), pltpu.SemaphoreType.DMA((7,))]
    return pl.pallas_call(
        body, name=name, in_specs=[ANY] * (1 + len(after)), out_specs=ANY,
        out_shape=jax.ShapeDtypeStruct(buf.shape, buf.dtype), scratch_shapes=scratch, input_output_aliases={0: 0})(buf, *after)


HBM = pl.BlockSpec(memory_space=pltpu.HBM)
SEM = pl.BlockSpec(memory_space=pltpu.SEMAPHORE)
DATAFLOW = pltpu.SideEffectType.DATAFLOW_SIDE_EFFECTING


def _split_start(name, bufs, copies, n_copies, after=()):
    n = len(bufs)

    def body(*refs):
        send_sems, recv_sems = refs[n + len(after)], refs[n + len(after) + 1]
        for cp in copies(refs[:n], send_sems, recv_sems):
            cp.start()
        refs[-1][...] = jnp.zeros_like(refs[-1])

    outs = pl.pallas_call(
        body, name=name,
        out_shape=(pltpu.SemaphoreType.DMA((n_copies,)), pltpu.SemaphoreType.DMA((n_copies,)),
                   *[pltpu.HBM(b.shape, b.dtype) for b in bufs], jax.ShapeDtypeStruct((SUBLANES, LANES), F32)),
        in_specs=[HBM] * n + [ANY] * len(after),
        out_specs=(SEM, SEM, *[HBM] * n, pl.BlockSpec(memory_space=pltpu.VMEM)),
        input_output_aliases={t: 2 + t for t in range(n)},
        compiler_params=pltpu.CompilerParams(has_side_effects=DATAFLOW),
    )(*[pltpu.with_memory_space_constraint(b, pltpu.HBM) for b in bufs], *after)
    return outs[0], outs[1], list(outs[2:2 + n]), outs[-1]


def _split_wait(name, started, copies, after):
    send_sems, recv_sems, bufs, _ = started
    n = len(bufs)

    def body(*refs):
        for cp in copies(refs[:n], refs[n], refs[n + 1]):
            cp.wait_send()
            cp.wait_recv()

    return list(pl.pallas_call(
        body, name=name, out_shape=tuple(pltpu.HBM(b.shape, b.dtype) for b in bufs),
        in_specs=[HBM] * n + [SEM, SEM] + [ANY] * len(after), out_specs=tuple([HBM] * n),
        input_output_aliases={t: t for t in range(n)},
        compiler_params=pltpu.CompilerParams(has_side_effects=DATAFLOW),
    )(*bufs, send_sems, recv_sems, *after))


def _weight_ici_copies(refs, send_sems, recv_sems):
    x, y, c = _mesh_pos()
    out = []
    for t, ref in enumerate(refs):
        rh = ref.shape[1] // 2
        mine = ref.at[2 * x + y, pl.ds(c * rh, rh), :]
        out += [_remote(mine, mine, send_sems, recv_sems, 3 * t + j, (*chip, c)) for j, chip in enumerate(_other_chips(x, y))]
    return out


def _weight_d2d_copies(refs, send_sems, recv_sems):
    x, y, c = _mesh_pos()
    out = []
    for t, ref in enumerate(refs):
        rh = ref.shape[1] // 2
        for j, chip in enumerate(_other_chips(x, y)):
            blk = ref.at[2 * chip[0] + chip[1], pl.ds(c * rh, rh), :]
            out.append(_remote(blk, blk, send_sems, recv_sems, 3 * t + j, (x, y, 1 - c)))
    return out


def _swap_part(refs, send_sems, recv_sems, s0):
    x, y, c = _mesh_pos()
    n = len(refs) // 2
    out = []
    for t in range(n):
        rh = refs[t].shape[1] // 2
        out.append(_remote(refs[t].at[:, pl.ds((1 - c) * rh, rh), :], refs[n + t], send_sems, recv_sems, s0 + t, (x, y, 1 - c)))
    return out


def _exchange_part(refs, send_sems, recv_sems, s0):
    x, y, c = _mesh_pos()
    n = len(refs) // 2
    out = []
    for t in range(n):
        for j, chip in enumerate(_other_chips(x, y)):
            out.append(_remote(refs[t].at[2 * chip[0] + chip[1]], refs[n + t].at[j], send_sems, recv_sems, s0 + 3 * t + j,
                               (*chip, c)))
    return out


def _gather_part(refs, send_sems, recv_sems, s0):
    x, y, c = _mesh_pos()
    return [_remote(ref.at[c], ref.at[c], send_sems, recv_sems, s0 + t, (x, y, 1 - c)) for t, ref in enumerate(refs)]


def _all_to_all_copies(refs, send_sems, recv_sems):
    x, y, c = _mesh_pos()
    blk = refs[0].at[4 * x + 2 * y + c]
    peers = [(x, y, 1 - c)] + [(*chip, cc) for chip in _other_chips(x, y) for cc in (c, 1 - c)]
    return [_remote(blk, blk, send_sems, recv_sems, k, peer) for k, peer in enumerate(peers)]


class _GradPipeline:
    def __init__(self, pos):
        self.pos = pos
        self.groups, self.pending, self.count = [], None, 0
        self.reduced = {n: [None] * DEPTH for n in BIG_NAMES}

    def busy(self):
        return bool(self.groups) or self.pending is not None

    def tick(self, deps, new):
        if self.pending is not None:
            started, copies, owners = self.pending
            bufs = _split_wait("grad_pipe_wait_%d" % self.count, started, copies, after=list(deps))
            for grp, lo, hi in owners:
                grp["bufs"] = bufs[lo:hi]
            self.pending = None
        parts = []
        for grp in list(self.groups):
            n, names = len(grp["names"]), grp["names"]
            if grp["stage"] == "swap":
                pair = [_pair_sum("grad_pair_sum_" + nm, f, r, self.pos)
                        for nm, f, r in zip(names, grp["bufs"][:n], grp["bufs"][n:])]
                grp["own32"] = [p32 for p32, _ in pair]
                landing = [lax.empty((3, *p16.shape[1:]), BF16) for _, p16 in pair]
                grp["stage"] = "exchange"
                parts.append((grp, [p16 for _, p16 in pair] + landing, _exchange_part, 3 * n))
            elif grp["stage"] == "exchange":
                halves = [_chip_sum("grad_chip_sum_" + nm, p32, r, self.pos)
                          for nm, p32, r in zip(names, grp["own32"], grp["bufs"][n:])]
                grp["stage"] = "gather"
                parts.append((grp, halves, _gather_part, n))
            else:
                for nm, b in zip(names, grp["bufs"]):
                    self.reduced[nm][grp["layer"]] = b.reshape(-1, b.shape[-1])
                self.groups.remove(grp)
        if new:
            grp = dict(names=[nm for nm, _, _ in new], layer=new[0][1], stage="swap")
            self.groups.append(grp)
            fulls = [g for _, _, g in new]
            landing = [lax.empty((N_CHIP, g.shape[1] // 2, g.shape[2]), F32) for g in fulls]
            parts.append((grp, fulls + landing, _swap_part, len(fulls)))
        if not parts:
            return ()
        bufs, layout, owners, sems = [], [], [], 0
        for grp, part_bufs, fn, n_sems in parts:
            layout.append((len(bufs), len(bufs) + len(part_bufs), fn, sems))
            owners.append((grp, len(bufs), len(bufs) + len(part_bufs)))
            bufs += part_bufs
            sems += n_sems

        def copies(refs, send_sems, recv_sems):
            out = []
            for lo, hi, fn, s0 in layout:
                out += fn(refs[lo:hi], send_sems, recv_sems, s0)
            return out

        started = _split_start("grad_pipe_start_%d" % self.count, bufs, copies, sems)
        self.pending = (started, copies, owners)
        self.count += 1
        return (started[3],)


def _gather_all(name, block, slot, after=()):
    buf = lax.dynamic_update_slice(jnp.zeros((8, *block.shape), block.dtype), block[None], (slot, 0, 0))

    def body(*refs):
        out_ref, send_sems, recv_sems = refs[1 + len(after):]
        x, y, c = _mesh_pos()
        chips = _other_chips(x, y)
        sibling = (x, y, 1 - c)
        slot_of = lambda px, py, pc: out_ref.at[4 * px + 2 * py + pc]
        started = [_remote(slot_of(x, y, c), slot_of(x, y, c), send_sems, recv_sems, 0, sibling)]
        started += [_remote(slot_of(x, y, c), slot_of(x, y, c), send_sems, recv_sems, 1 + j, (*chip, c))
                    for j, chip in enumerate(chips)]
        for cp in started:
            cp.start()
        for j, chip in enumerate(chips):
            _remote(slot_of(*chip, c), slot_of(*chip, c), send_sems, recv_sems, 1 + j, (*chip, c)).wait_recv()
            fw = _remote(slot_of(*chip, c), slot_of(*chip, c), send_sems, recv_sems, 4 + j, sibling)
            fw.start()
            started.append(fw)
        _remote(slot_of(x, y, 1 - c), slot_of(x, y, 1 - c), send_sems, recv_sems, 0, sibling).wait_recv()
        for j, chip in enumerate(chips):
            _remote(slot_of(*chip, 1 - c), slot_of(*chip, 1 - c), send_sems, recv_sems, 4 + j, sibling).wait_recv()
        for cp in started:
            cp.wait_send()

    return _gather_call(name, body, buf, after)


def _row_tile(rows, cols, block_bytes=ELEMWISE_BLOCK_BYTES):
    cap = max(SUBLANES, block_bytes // (4 * cols))
    tr = rows
    while tr > cap and tr % 2 == 0:
        tr //= 2
    return tr


def _pair_sum(name, grad, recv, pos):
    _, rh, cols = recv.shape
    tr = _row_tile(rh, cols, 2 * ELEMWISE_BLOCK_BYTES)
    per = rh // tr

    def body(pos_ref, g_ref, r_ref, own_ref, out16_ref):
        s = g_ref[...] + r_ref[...]
        out16_ref[...] = s.astype(BF16)

        @pl.when(pl.program_id(1) == pos_ref[0])
        def _():
            own_ref[...] = s

    blk = pl.BlockSpec((None, tr, cols), lambda i, k, pos_ref: (k, i, 0))
    return pl.pallas_call(
        body, name=name,
        grid_spec=pltpu.PrefetchScalarGridSpec(
            num_scalar_prefetch=1, grid=(per, N_CHIP),
            in_specs=[pl.BlockSpec((None, tr, cols), lambda i, k, pos_ref: (k, pos_ref[1] * per + i, 0)), blk],
            out_specs=[pl.BlockSpec((tr, cols), lambda i, k, pos_ref: (i, 0)), blk]),
        out_shape=[jax.ShapeDtypeStruct((rh, cols), F32), jax.ShapeDtypeStruct(recv.shape, BF16)],
        compiler_params=_params(2))(pos, grad, recv)


def _chip_sum(name, own32, recv, pos):
    rh, cols = own32.shape
    tr = _row_tile(rh, cols, 2 * ELEMWISE_BLOCK_BYTES)

    def body(pos_ref, own_ref, r_ref, out_ref):
        del pos_ref
        out_ref[...] = ((own_ref[...] + r_ref[0].astype(F32)) + r_ref[1].astype(F32)) + r_ref[2].astype(F32)

    return pl.pallas_call(
        body, name=name,
        grid_spec=pltpu.PrefetchScalarGridSpec(
            num_scalar_prefetch=1, grid=(rh // tr,),
            in_specs=[pl.BlockSpec((tr, cols), lambda i, pos_ref: (i, 0)),
                      pl.BlockSpec((3, tr, cols), lambda i, pos_ref: (0, i, 0))],
            out_specs=pl.BlockSpec((None, tr, cols), lambda i, pos_ref: (pos_ref[1], i, 0))),
        out_shape=jax.ShapeDtypeStruct((2, rh, cols), F32), compiler_params=_params(1))(pos, own32, recv)


def _cast_into_slot(name, w, layer, pos, after=()):
    _, rows, cols = w.shape
    tr = _row_tile(rows, cols, 2 * ELEMWISE_BLOCK_BYTES)

    def body(pos_ref, w_ref, *rest):
        del pos_ref
        rest[-1][...] = w_ref[...].astype(BF16)

    return pl.pallas_call(
        body, name=name,
        grid_spec=pltpu.PrefetchScalarGridSpec(
            num_scalar_prefetch=1, grid=(rows // tr,),
            in_specs=[pl.BlockSpec((None, tr, cols), lambda i, pos_ref: (layer, i, 0))] + [ANY] * len(after),
            out_specs=pl.BlockSpec((None, tr, cols), lambda i, pos_ref: (pos_ref[0], i, 0))),
        out_shape=jax.ShapeDtypeStruct((N_CHIP, rows, cols), BF16), compiler_params=_params(1))(pos, w, *after)


def _adamw_math(w, g, m, v):
    m = ADAM_B1 * m + (1.0 - ADAM_B1) * g
    v = ADAM_B2 * v + (1.0 - ADAM_B2) * jnp.square(g)
    m_hat = m / (1.0 - ADAM_B1 ** ADAM_STEP)
    v_hat = v / (1.0 - ADAM_B2 ** ADAM_STEP)
    delta = -ADAM_LR * (m_hat / (jnp.sqrt(v_hat) + ADAM_EPS) + ADAM_WD * w)
    return delta, m, v


def _adamw_layers(name, w, m, v, grads, first, into=None, after=()):
    _, rows, cols = w.shape
    tr = _row_tile(rows, cols)
    n_layers = len(grads)

    def body(w_ref, m_ref, v_ref, *rest):
        g_refs, (grad_ref, d_ref, nm_ref, nv_ref) = rest[:n_layers], rest[len(rest) - 4:]
        layer = pl.program_id(0)
        g = g_refs[0][...]
        for l in range(1, n_layers):
            g = jnp.where(layer == l, g_refs[l][...], g)
        grad_ref[...] = g
        d_ref[...], nm_ref[...], nv_ref[...] = _adamw_math(w_ref[...], g, m_ref[...], v_ref[...])

    blk = pl.BlockSpec((None, tr, cols), lambda l, i: (first + l, i, 0))
    g_spec = lambda k: pl.BlockSpec((tr, cols), lambda l, i: (jnp.where(l == k, i, 0), 0))
    passed = list(into or []) + list(after)
    return pl.pallas_call(
        body, name=name, grid=(n_layers, rows // tr),
        in_specs=[blk, blk, blk] + [g_spec(k) for k in range(n_layers)] + [ANY] * len(passed), out_specs=[blk] * 4,
        out_shape=[jax.ShapeDtypeStruct(w.shape, F32)] * 4,
        input_output_aliases={3 + n_layers + t: t for t in range(4)} if into else {},
        compiler_params=_params(2))(w, m, v, *grads, *passed)


def _sum_devices(gathered):
    _, rows, cols = gathered.shape

    def body(g_ref, out_ref):
        s = g_ref[0]
        for d in range(1, 8):
            s = s + g_ref[d]
        out_ref[...] = s

    return pl.pallas_call(body, name="sum_devices", out_shape=jax.ShapeDtypeStruct((rows, cols), F32),
                          compiler_params=pltpu.CompilerParams(vmem_limit_bytes=VMEM_LIMIT_BYTES))(gathered)


def _adamw_flat(w, g, m, v):
    def body(w_ref, g_ref, m_ref, v_ref, d_ref, nm_ref, nv_ref):
        d_ref[...], nm_ref[...], nv_ref[...] = _adamw_math(w_ref[...], g_ref[...], m_ref[...], v_ref[...])

    return pl.pallas_call(body, name="adamw_small", out_shape=[jax.ShapeDtypeStruct(w.shape, F32)] * 3,
                          compiler_params=pltpu.CompilerParams(vmem_limit_bytes=VMEM_LIMIT_BYTES))(w, g, m, v)


SMALL_NAMES = ["g_mix", "lower_bounds", "g_hgrn_out", "w_conv", "sg_ln_g", "sg_ln_b", "w_sg", "b_sg", "g_ffn", "g_final"]
BIG_NAMES = ["w_in", "w_branch", "w_o", "w_ff1", "w_ff2"]
WEIGHT_ORDER = ["w_in", "g_mix", "lower_bounds", "g_hgrn_out", "w_conv", "sg_ln_g", "sg_ln_b", "w_sg", "b_sg", "w_branch",
                "w_o", "g_ffn", "w_ff1", "w_ff2", "g_final"]


def _padded_rows(n):
    return -(-n // SUBLANES) * SUBLANES


def _pack(arrays):
    parts = []
    for a in arrays:
        a = a.reshape(-1, LANES)
        parts.append(jnp.pad(a, ((0, _padded_rows(a.shape[0]) - a.shape[0]), (0, 0))))
    return jnp.concatenate(parts, axis=0)


def _unpack(flat, shapes):
    out, row = [], 0
    for s in shapes:
        n = int(np.prod(s)) // LANES
        out.append(flat[row:row + n].reshape(s))
        row += _padded_rows(n)
    return out


def _as_2d(name, a):
    return a.reshape(DEPTH, N_BRANCH * WIDTH, BRANCH_COLS) if name == "w_branch" else a


def kernel(x, w_in, g_mix, lower_bounds, g_hgrn_out, w_conv, sg_ln_g, sg_ln_b, w_sg, b_sg, w_branch, w_o, g_ffn, w_ff1, w_ff2, g_final, loss_target, m_w_in, m_g_mix, m_lower_bounds, m_g_hgrn_out, m_w_conv, m_sg_ln_g, m_sg_ln_b, m_w_sg, m_b_sg, m_w_branch, m_w_o, m_g_ffn, m_w_ff1, m_w_ff2, m_g_final, v_w_in, v_g_mix, v_lower_bounds, v_g_hgrn_out, v_w_conv, v_sg_ln_g, v_sg_ln_b, v_w_sg, v_b_sg, v_w_branch, v_w_o, v_g_ffn, v_w_ff1, v_w_ff2, v_g_final):
    weights = dict(w_in=w_in, g_mix=g_mix, lower_bounds=lower_bounds, g_hgrn_out=g_hgrn_out, w_conv=w_conv,
                   sg_ln_g=sg_ln_g, sg_ln_b=sg_ln_b, w_sg=w_sg, b_sg=b_sg, w_branch=w_branch, w_o=w_o, g_ffn=g_ffn,
                   w_ff1=w_ff1, w_ff2=w_ff2, g_final=g_final)
    mom1 = dict(w_in=m_w_in, g_mix=m_g_mix, lower_bounds=m_lower_bounds, g_hgrn_out=m_g_hgrn_out, w_conv=m_w_conv,
                sg_ln_g=m_sg_ln_g, sg_ln_b=m_sg_ln_b, w_sg=m_w_sg, b_sg=m_b_sg, w_branch=m_w_branch, w_o=m_w_o,
                g_ffn=m_g_ffn, w_ff1=m_w_ff1, w_ff2=m_w_ff2, g_final=m_g_final)
    mom2 = dict(w_in=v_w_in, g_mix=v_g_mix, lower_bounds=v_lower_bounds, g_hgrn_out=v_g_hgrn_out, w_conv=v_w_conv,
                sg_ln_g=v_sg_ln_g, sg_ln_b=v_sg_ln_b, w_sg=v_w_sg, b_sg=v_b_sg, w_branch=v_w_branch, w_o=v_w_o,
                g_ffn=v_g_ffn, w_ff1=v_w_ff1, w_ff2=v_w_ff2, g_final=v_g_final)
    xi, yi, ci = _mesh_pos()
    pos = jnp.stack([2 * xi + yi, ci]).astype(jnp.int32)
    device = 4 * xi + 2 * yi + ci
    conv_cols = w_conv.shape[2]

    conv_all = _gather_all("gather_w_conv", w_conv.reshape(DEPTH * CONV_K, conv_cols), device)
    conv_full = conv_all.reshape(N_CHIP, 2, DEPTH, CONV_K, conv_cols)[:, 0].transpose(1, 2, 0, 3).reshape(DEPTH, CONV_K, WIDTH)

    ici, d2d = {}, {}
    cast = lambda l, names, after: [_cast_into_slot("cast_" + n, _as_2d(n, weights[n]), l, pos, after=after) for n in names]
    token = (conv_all,)
    for l in range(DEPTH):
        for part, names in (("w_in", BIG_NAMES[:1]), ("rest", BIG_NAMES[1:])):
            ici[l, part] = _split_start("weights_ici_start_%d_%s" % (l, part), cast(l, names, token), _weight_ici_copies,
                                        3 * len(names), after=token)
            token = (ici[l, part][3],)
    lbs = _lbs_fwd(lower_bounds)

    def forward_to_sibling(l, part, deps):
        landed = _split_wait("weights_ici_wait_%d_%s" % (l, part), ici.pop((l, part)), _weight_ici_copies, after=deps)
        d2d[l, part] = _split_start("weights_d2d_start_%d_%s" % (l, part), landed, _weight_d2d_copies, 3 * len(landed))
        return (d2d[l, part][3],)

    def gathered(l, part, deps):
        return _split_wait("weights_d2d_wait_%d_%s" % (l, part), d2d.pop((l, part)), _weight_d2d_copies, after=deps)

    act = x[0]
    normed = _rms_fwd("rms_mix", act, g_mix[0:1], after=token)
    layers = []
    forward_to_sibling(0, "w_in", [normed, lbs])
    for l in range(DEPTH):
        small = dict(g_mix=g_mix[l:l + 1], lbs=lbs[l:l + 1], g_hgrn_out=g_hgrn_out[l:l + 1], w_conv=conv_full[l],
                     sg_ln_g=sg_ln_g[l:l + 1], sg_ln_b=sg_ln_b[l:l + 1], w_sg=w_sg[l],
                     b_sg=b_sg[l].reshape(SG_GROUPS, SG_CHUNK, 1), g_ffn=g_ffn[l:l + 1])
        hooks = dict(after_hgrn=lambda deps, l=l: forward_to_sibling(l, "rest", deps),
                     late_weights=lambda deps, l=l: gathered(l, "rest", deps),
                     before_last=(lambda deps, l=l: forward_to_sibling(l + 1, "w_in", deps)) if l + 1 < DEPTH
                     else (lambda deps: ()))
        act, normed, saved, wts = _layer_fwd(act, normed, gathered(l, "w_in", [act])[0], small,
                                             g_mix[l + 1:l + 2] if l + 1 < DEPTH else None, hooks)
        layers.append((wts, small, saved))
    loss_blk, dact, dg_final = _loss_head(act, g_final.reshape(1, D_MODEL), loss_target[0])

    pipe = _GradPipeline(pos)
    small_grads = [None] * DEPTH
    after = ()
    for l in reversed(range(DEPTH)):
        wts, small, saved = layers[l]
        tick = lambda deps, new, l=l: pipe.tick(deps, [(nm, l, g) for nm, g in new])
        dact, small_grads[l], after = _layer_bwd(dact, saved, wts, small, tick, after)
    grad_x = dact[None]

    stack = lambda key, rows=None: jnp.stack([small_grads[l][key][0] if rows is None else small_grads[l][key][:rows]
                                              for l in range(DEPTH)])
    local_small = dict(
        g_mix=stack("g_mix"), lower_bounds=stack("lbs"), g_hgrn_out=stack("g_hgrn_out"), w_conv=stack("w_conv", CONV_K),
        sg_ln_g=stack("sg_ln_g"), sg_ln_b=stack("sg_ln_b"), w_sg=jnp.stack([small_grads[l]["w_sg"] for l in range(DEPTH)]),
        b_sg=jnp.stack([small_grads[l]["b_sg"].reshape(SG_GROUPS, SG_CHUNK) for l in range(DEPTH)]),
        g_ffn=stack("g_ffn"), g_final=dg_final[0])
    shapes = [local_small[n].shape for n in SMALL_NAMES] + [(SUBLANES, LANES)]
    packed = _pack([local_small[n] for n in SMALL_NAMES] + [loss_blk])
    packed = lax.dynamic_update_slice(jnp.zeros((8, *packed.shape), F32), packed[None], (device, 0, 0))
    small_flight = _split_start("small_grads_start", [packed], _all_to_all_copies, 7, after=after)

    def adam(n, first, layer_grads, into=None, after=()):
        return _adamw_layers("adamw_%s_%d" % (n, first), _as_2d(n, weights[n]), _as_2d(n, mom1[n]), _as_2d(n, mom2[n]),
                             layer_grads, first, into, after)

    done = {"w_ff1": adam("w_ff1", 0, pipe.reduced["w_ff1"], after=(small_flight[3],))}
    token = pipe.tick([done["w_ff1"][1]], [])
    done["w_ff2"] = adam("w_ff2", 0, pipe.reduced["w_ff2"], after=token)
    summed = _sum_devices(_split_wait("small_grads_wait", small_flight, _all_to_all_copies, after=[done["w_ff2"][1]])[0])
    parts = _unpack(summed, shapes)
    loss = parts[-1][0, 0]
    small_grad = dict(zip(SMALL_NAMES, parts[:-1]))
    small_grad["lower_bounds"] = _lbs_bwd(lower_bounds, small_grad["lower_bounds"])
    small_grad["w_conv"] = lax.dynamic_slice_in_dim(small_grad["w_conv"], pos[0] * conv_cols, conv_cols, axis=2)
    g_flat = _pack([small_grad[n] for n in SMALL_NAMES])
    d_flat, m_flat, v_flat = _adamw_flat(_pack([weights[n] for n in SMALL_NAMES]), g_flat,
                                         _pack([mom1[n] for n in SMALL_NAMES]), _pack([mom2[n] for n in SMALL_NAMES]))
    small_shapes = [weights[n].shape for n in SMALL_NAMES]
    grads = dict(small_grad)
    delta = dict(zip(SMALL_NAMES, _unpack(d_flat, small_shapes)))
    new_m = dict(zip(SMALL_NAMES, _unpack(m_flat, small_shapes)))
    new_v = dict(zip(SMALL_NAMES, _unpack(v_flat, small_shapes)))

    for n in ("w_o", "w_branch"):
        done[n] = adam(n, 0, pipe.reduced[n], after=(d_flat,))
    token = pipe.tick([done["w_branch"][1]], [])
    rest = adam("w_in", 1, pipe.reduced["w_in"][1:], after=token)
    pipe.tick([rest[1]], [])
    assert not pipe.busy()
    done["w_in"] = adam("w_in", 0, pipe.reduced["w_in"][:1], into=rest)
    for n in BIG_NAMES:
        grads[n], delta[n], new_m[n], new_v[n] = [o.reshape(weights[n].shape) for o in done[n]]

    return (loss, grad_x, *[grads[n] for n in WEIGHT_ORDER], *[delta[n] for n in WEIGHT_ORDER],
            *[new_m[n] for n in WEIGHT_ORDER], *[new_v[n] for n in WEIGHT_ORDER])
```

```python
import numpy as np
import jax
import jax.numpy as jnp
from jax import lax
from jax.experimental import pallas as pl
from jax.experimental.pallas import tpu as pltpu

F32, BF16 = jnp.float32, jnp.bfloat16

D_MODEL = 1024
WIDTH = 512
N_BRANCH = 3
N_HEAD = 4
HEAD = 128
H_CHUNK = 64
CONV_K = 3
SG_CHUNK = 128
SG_GROUPS = 4
D_FF = 4096
DEPTH = 4
N_CHIP = 4
IN_COLS = 9 * WIDTH + N_BRANCH * D_MODEL
GATE_COL0 = 9 * WIDTH
LB_FLOOR = 1e-30
NORM_EPS = 1e-6
LN_EPS = 1e-5
ADAM_LR, ADAM_B1, ADAM_B2, ADAM_EPS, ADAM_WD, ADAM_STEP = 0.001, 0.9, 0.999, 1e-08, 0.01, 10

VMEM_LIMIT_BYTES = 56 * 1024 * 1024
VMEM_BLOCK_BUDGET = 44 * 1024 * 1024
SUBLANES, LANES = 8, 128
ELEMWISE_BLOCK_BYTES = 2 * 1024 * 1024

NN = (((1,), (0,)), ((), ()))
NT = (((1,), (1,)), ((), ()))
TN = (((0,), (0,)), ((), ()))
MESH = pl.DeviceIdType.MESH
ANY = pl.BlockSpec(memory_space=pl.ANY)


def _dot(a, b, dims=NN):
    return lax.dot_general(a.astype(BF16), b.astype(BF16), dims, preferred_element_type=F32)


def _params(n_axes):
    return pltpu.CompilerParams(dimension_semantics=("arbitrary",) * n_axes, vmem_limit_bytes=VMEM_LIMIT_BYTES)


def _row0(part, rows=SUBLANES):
    r = lax.broadcasted_iota(jnp.int32, (rows, part.shape[1]), 0)
    return jnp.where(r == 0, part, 0.0)


def _token_tile(T):
    return min(512, T)


def _matmul(name, a, b, *, dims, grid, a_spec, b_spec, out_specs, out_shapes, acc_shape,
            extra=(), extra_specs=(), epilogue=None, after=()):
    nk = grid[2]
    n_extra, n_out, n_in = len(extra), len(out_shapes), 2 + len(extra) + len(after)
    one_step = nk == 1

    def body(*refs):
        a_ref, b_ref = refs[0], refs[1]
        ex = refs[2:2 + n_extra]
        outs = refs[n_in:n_in + n_out]
        part = _dot(a_ref[...], b_ref[...], dims)

        def finish(total):
            res = epilogue(total, *[e[...] for e in ex]) if epilogue else (total,)
            for o, r in zip(outs, res):
                o[...] = r.astype(o.dtype)

        if one_step:
            finish(part)
            return
        acc = refs[-1]
        kk = pl.program_id(2)

        @pl.when(kk == 0)
        def _():
            acc[...] = part

        @pl.when(kk > 0)
        def _():
            acc[...] += part

        @pl.when(kk == nk - 1)
        def _():
            finish(acc[...])

    return pl.pallas_call(
        body, name=name, grid=grid,
        in_specs=[a_spec, b_spec, *extra_specs, *[ANY] * len(after)], out_specs=list(out_specs),
        out_shape=list(out_shapes), scratch_shapes=[] if one_step else [pltpu.VMEM(acc_shape, F32)],
        compiler_params=_params(3),
    )(a, b, *extra, *after)


def _mm_cols(name, a, w, out_dtypes, epilogue=None, extra=()):
    T, K = a.shape
    N = w.shape[2]
    tm = min(2 * _token_tile(T), T)
    blk = pl.BlockSpec((tm, N), lambda j, i, kk: (i, j))
    return _matmul(
        name, a, w, dims=NN, grid=(N_CHIP, T // tm, 1),
        a_spec=pl.BlockSpec((tm, K), lambda j, i, kk: (i, 0)),
        b_spec=pl.BlockSpec((None, K, N), lambda j, i, kk: (j, 0, 0)),
        out_specs=[blk] * len(out_dtypes),
        out_shapes=[jax.ShapeDtypeStruct((T, N_CHIP * N), dt) for dt in out_dtypes],
        acc_shape=(tm, N), extra=extra, extra_specs=[blk] * len(extra), epilogue=epilogue)


def _mm_rows(name, a, w, res, norm_gain=None, after=()):
    T = a.shape[0]
    K, N = N_CHIP * w.shape[1], w.shape[2]
    tm = _token_tile(T)
    blk = pl.BlockSpec((tm, N), lambda i, j, kk: (i, 0))

    def with_norm(acc, r, gain):
        xv = acc + r
        return xv, xv * lax.rsqrt(jnp.mean(xv * xv, axis=-1, keepdims=True) + NORM_EPS) * gain

    normed = norm_gain is not None
    outs = _matmul(
        name, a, w.reshape(K, N), dims=NN, grid=(T // tm, 1, 1),
        a_spec=pl.BlockSpec((tm, K), lambda i, j, kk: (i, 0)),
        b_spec=pl.BlockSpec((K, N), lambda i, j, kk: (0, 0)),
        out_specs=[blk] * (2 if normed else 1),
        out_shapes=[jax.ShapeDtypeStruct((T, N), F32)] + ([jax.ShapeDtypeStruct((T, N), BF16)] if normed else []),
        acc_shape=(tm, N), extra=(res, norm_gain) if normed else (res,),
        extra_specs=[blk] + ([pl.BlockSpec((1, N), lambda i, j, kk: (0, 0))] if normed else []),
        epilogue=with_norm if normed else (lambda acc, r: (acc + r,)), after=after)
    return outs if normed else outs[0]


def _mm_cols_t(name, g, w, out_dtype, epilogue=None, extra=(), after=()):
    T, N = g.shape
    K = N_CHIP * w.shape[1]
    tm = _token_tile(T) if K <= 2 * D_MODEL else _token_tile(T) // 2
    blk = pl.BlockSpec((tm, K), lambda i, j, kk: (i, 0))
    return _matmul(
        name, g, w.reshape(K, N), dims=NT, grid=(T // tm, 1, 1),
        a_spec=pl.BlockSpec((tm, N), lambda i, j, kk: (i, 0)),
        b_spec=pl.BlockSpec((K, N), lambda i, j, kk: (0, 0), pipeline_mode=pl.Buffered(1)),
        out_specs=[blk], out_shapes=[jax.ShapeDtypeStruct((T, K), out_dtype)], acc_shape=(tm, K),
        extra=extra, extra_specs=[blk] * len(extra), epilogue=epilogue, after=after)[0]


def _dgrad_norm_bwd(name, g, w, x, gain, dres):
    T = g.shape[0]
    K, N = w.shape[1], w.shape[2]
    tm = _token_tile(T)
    whole = w.size * w.dtype.itemsize <= VMEM_BLOCK_BUDGET // 2

    def norm_bwd(i, dhv, x_ref, gain_ref, dres_ref, dx_ref, dgain_ref):
        xv = x_ref[...]
        r = lax.rsqrt(jnp.mean(xv * xv, axis=-1, keepdims=True) + NORM_EPS)
        xn = xv * r
        dxn = dhv * gain_ref[...]
        dx_ref[...] = dres_ref[...] + r * (dxn - xn * jnp.mean(dxn * xn, axis=-1, keepdims=True))

        @pl.when(i == 0)
        def _():
            dgain_ref[...] = jnp.zeros_like(dgain_ref)

        dgain_ref[...] += _row0(jnp.sum(dhv * xn, axis=0, keepdims=True))

    def body_whole(g_ref, w_ref, *rest):
        dhv = _dot(g_ref[:, pl.ds(0, N)], w_ref[0], NT)
        for k in range(1, N_CHIP):
            dhv = dhv + _dot(g_ref[:, pl.ds(k * N, N)], w_ref[k], NT)
        norm_bwd(pl.program_id(0), dhv, *rest)

    def body_steps(g_ref, w_ref, x_ref, gain_ref, dres_ref, dx_ref, dgain_ref, acc):
        kk = pl.program_id(1)
        part = _dot(g_ref[...], w_ref[...], NT)

        @pl.when(kk == 0)
        def _():
            acc[...] = part

        @pl.when(kk > 0)
        def _():
            acc[...] += part

        @pl.when(kk == N_CHIP - 1)
        def _():
            norm_bwd(pl.program_id(0), acc[...], x_ref, gain_ref, dres_ref, dx_ref, dgain_ref)

    tile = pl.BlockSpec((tm, K), lambda i, *kk: (i, 0))
    row = lambda cols: pl.BlockSpec((SUBLANES if cols is None else 1, K), lambda i, *kk: (0, 0))
    if whole:
        g_spec = pl.BlockSpec((tm, N_CHIP * N), lambda i: (i, 0))
        w_spec = pl.BlockSpec(w.shape, lambda i: (0, 0, 0), pipeline_mode=pl.Buffered(1))
    else:
        g_spec = pl.BlockSpec((tm, N), lambda i, kk: (i, kk))
        w_spec = pl.BlockSpec((None, K, N), lambda i, kk: (kk, 0, 0))
    return pl.pallas_call(
        body_whole if whole else body_steps, name=name, grid=(T // tm,) if whole else (T // tm, N_CHIP),
        in_specs=[g_spec, w_spec, tile, row(1), tile], out_specs=[tile, row(None)],
        out_shape=[jax.ShapeDtypeStruct((T, K), F32), jax.ShapeDtypeStruct((SUBLANES, K), F32)],
        scratch_shapes=[] if whole else [pltpu.VMEM((tm, K), F32)],
        compiler_params=_params(1 if whole else 2))(g, w, x, gain, dres)


def _mm_wgrad(name, a, g, a_cols, g_cols, a_blocked, g_blocked, after=()):
    T = a.shape[0]
    tt = T
    while tt > LANES and 2 * 2 * tt * (a_cols + g_cols) + (2 if tt == T else 3) * 4 * a_cols * g_cols > VMEM_BLOCK_BUDGET:
        tt //= 2
    return _matmul(
        name, a, g, dims=TN, grid=(N_CHIP, 1, T // tt),
        a_spec=pl.BlockSpec((tt, a_cols), (lambda j, i, kk: (kk, j)) if a_blocked else (lambda j, i, kk: (kk, 0))),
        b_spec=pl.BlockSpec((tt, g_cols), (lambda j, i, kk: (kk, j)) if g_blocked else (lambda j, i, kk: (kk, 0))),
        out_specs=[pl.BlockSpec((None, a_cols, g_cols), lambda j, i, kk: (j, 0, 0))],
        out_shapes=[jax.ShapeDtypeStruct((N_CHIP, a_cols, g_cols), F32)], acc_shape=(a_cols, g_cols), after=after)[0]


def _rms_fwd(name, x, g, after=()):
    T, Dm = x.shape
    tm = min(256, T)

    def body(x_ref, g_ref, *rest):
        xv = x_ref[...]
        r = lax.rsqrt(jnp.mean(xv * xv, axis=-1, keepdims=True) + NORM_EPS)
        rest[-1][...] = (xv * r * g_ref[...]).astype(BF16)

    return pl.pallas_call(
        body, name=name, grid=(T // tm,),
        in_specs=[pl.BlockSpec((tm, Dm), lambda i: (i, 0)), pl.BlockSpec((1, Dm), lambda i: (0, 0))] + [ANY] * len(after),
        out_specs=pl.BlockSpec((tm, Dm), lambda i: (i, 0)),
        out_shape=jax.ShapeDtypeStruct((T, Dm), BF16), compiler_params=_params(1))(x, g, *after)


def _loss_head(x, g, tgt):
    T, Dm = x.shape
    tm = min(256, T)

    def body(x_ref, g_ref, t_ref, loss_ref, dx_ref, dg_ref):
        xv = x_ref[...]
        gv = g_ref[...]
        r = lax.rsqrt(jnp.mean(xv * xv, axis=-1, keepdims=True) + NORM_EPS)
        xn = xv * r
        err = xn * gv - t_ref[...]
        dy = err * (1.0 / Dm)
        dxn = dy * gv
        dx_ref[...] = r * (dxn - xn * jnp.mean(dxn * xn, axis=-1, keepdims=True))

        @pl.when(pl.program_id(0) == 0)
        def _():
            dg_ref[...] = jnp.zeros_like(dg_ref)
            loss_ref[...] = jnp.zeros_like(loss_ref)

        dg_ref[...] += _row0(jnp.sum(dy * xn, axis=0, keepdims=True))
        part = jnp.sum(jnp.sum(err * err, axis=-1, keepdims=True), axis=0, keepdims=True) * (0.5 / Dm)
        loss_ref[...] += jnp.broadcast_to(part, loss_ref.shape)

    tile = pl.BlockSpec((tm, Dm), lambda i: (i, 0))
    return pl.pallas_call(
        body, name="loss_head", grid=(T // tm,),
        in_specs=[tile, pl.BlockSpec((1, Dm), lambda i: (0, 0)), tile],
        out_specs=[pl.BlockSpec((SUBLANES, LANES), lambda i: (0, 0)), tile,
                   pl.BlockSpec((SUBLANES, Dm), lambda i: (0, 0))],
        out_shape=[jax.ShapeDtypeStruct((SUBLANES, LANES), F32), jax.ShapeDtypeStruct((T, Dm), F32),
                   jax.ShapeDtypeStruct((SUBLANES, Dm), F32)],
        compiler_params=_params(1))(x, g, tgt)


def _softmax_rows(lb_ref):
    rows = [lb_ref[pl.ds(i, 1), :] for i in range(DEPTH)]
    mx = rows[0]
    for r in rows[1:]:
        mx = jnp.maximum(mx, r)
    es = [jnp.exp(r - mx) for r in rows]
    tot = es[0]
    for e in es[1:]:
        tot = tot + e
    return [e / tot for e in es]


def _lbs_fwd(lower_bounds):
    def body(lb_ref, out_ref):
        sm = _softmax_rows(lb_ref)
        run = jnp.zeros_like(sm[0])
        out_ref[pl.ds(0, 1), :] = run
        for i in range(1, DEPTH):
            run = run + sm[i]
            out_ref[pl.ds(i, 1), :] = run

    return pl.pallas_call(body, name="lbs_fwd", out_shape=jax.ShapeDtypeStruct(lower_bounds.shape, F32))(lower_bounds)


def _lbs_bwd(lower_bounds, dlbs):
    def body(lb_ref, d_ref, out_ref):
        sm = _softmax_rows(lb_ref)
        dsm = [jnp.zeros_like(sm[0])]
        for i in range(1, DEPTH):
            acc = d_ref[pl.ds(i, 1), :]
            for l in range(i + 1, DEPTH):
                acc = acc + d_ref[pl.ds(l, 1), :]
            dsm.append(acc)
        inner = dsm[0] * sm[0]
        for i in range(1, DEPTH):
            inner = inner + dsm[i] * sm[i]
        for i in range(DEPTH):
            out_ref[pl.ds(i, 1), :] = sm[i] * (dsm[i] - inner)

    return pl.pallas_call(body, name="lbs_bwd", out_shape=jax.ShapeDtypeStruct(lower_bounds.shape, F32))(lower_bounds, dlbs)


N_LEVEL = 6


def _hgrn_consts():
    L = H_CHUNK
    t = np.arange(L)
    blocks = [(t[:, None] >= t[None, :]).astype(np.float32)]
    masks = []
    m = L // 2
    while m >= 1:
        blk, pos = t // (2 * m), t % (2 * m)
        start = blk * 2 * m
        mat = np.zeros((L, L), np.float32)
        for r in range(L):
            if pos[r] >= m:
                mat[r, start[r] + m:r + 1] = 1.0
            else:
                mat[r, r + 1:start[r] + m] = -1.0
        blocks.append(mat)
        masks.append(((blk[:, None] == blk[None, :]) & (pos[:, None] >= m) & (pos[None, :] < m)).astype(np.float32))
        m //= 2
    blocks.append(np.ones((L, L), np.float32))
    return jnp.asarray(np.concatenate(blocks, 0), BF16), jnp.asarray(np.stack(masks), F32)


def _hgrn_core(qraw, fp, lb, sum_mat, mask_ref):
    L = H_CHUNK
    sq = jax.nn.sigmoid(qraw)
    q = qraw * sq
    sneg = jax.nn.sigmoid(-fp)
    log_sig = jnp.minimum(fp, 0.0) - jnp.log1p(jnp.exp(-jnp.abs(fp)))
    a1 = jnp.log(jnp.maximum(lb, LB_FLOOR))
    a2 = jnp.log1p(-lb) + log_sig
    logf = jnp.maximum(a1, a2) + jnp.log1p(jnp.exp(-jnp.abs(a1 - a2)))
    w1 = jnp.exp(a1 - logf)
    w2 = jnp.exp(a2 - logf)
    k = (1.0 - lb) * sneg
    hi = logf.astype(BF16)
    r1 = logf - hi.astype(F32)
    mid = r1.astype(BF16)
    lo = (r1 - mid.astype(F32)).astype(BF16)
    sums = lax.dot_general(sum_mat, jnp.concatenate([hi, mid, lo], axis=1), NN, preferred_element_type=F32)
    sums = sums[:, 0:HEAD] + sums[:, HEAD:2 * HEAD] + sums[:, 2 * HEAD:3 * HEAD]
    b = sums[0:L]
    b_last = sums[(N_LEVEL + 1) * L:(N_LEVEL + 2) * L]
    eye = lax.broadcasted_iota(jnp.int32, (L, L), 0) == lax.broadcasted_iota(jnp.int32, (L, L), 1)
    attn = jnp.where(eye, jnp.sum(q * k, axis=1, keepdims=True), 0.0)
    fa, fb, ea, eb = [], [], [], []
    for l in range(N_LEVEL):
        d = sums[(l + 1) * L:(l + 2) * L]
        e_a = jnp.exp(jnp.minimum(d, 0.0))
        e_b = jnp.exp(jnp.minimum(-d, 0.0))
        a_l, b_l = q * e_a, k * e_b
        attn = attn + mask_ref[l] * _dot(a_l, b_l, NT)
        fa.append(a_l), fb.append(b_l), ea.append(e_a), eb.append(e_b)
    return dict(sq=sq, q=q, sneg=sneg, logf=logf, w1=w1, w2=w2, k=k, b=b, b_last=b_last, attn=attn,
                fa=fa, fb=fb, ea=ea, eb=eb)


def _hgrn_fwd(p, lbrow, gout):
    T = p.shape[0]
    nch = T // H_CHUNK
    per = 2 if nch % 2 == 0 else 1
    step, subs = per * H_CHUNK, range(per)
    sum_mat, masks = _hgrn_consts()

    def body(p_ref, lb_ref, g_ref, m_ref, mask_ref, o_ref, z_ref, st_ref, state):
        @pl.when(pl.program_id(0) == 0)
        def _():
            state[...] = jnp.zeros_like(state)

        sum_m = m_ref[...]
        for sub, h in [(s_, h_) for s_ in subs for h_ in range(N_HEAD)]:
            rows = pl.ds(sub * H_CHUNK, H_CHUNK)
            col = lambda part: pl.ds(part * WIDTH + h * HEAD, HEAD)
            hs = pl.ds(h * HEAD, HEAD)
            v = p_ref[rows, col(2)]
            c = _hgrn_core(p_ref[rows, col(0)], p_ref[rows, col(1)], lb_ref[:, hs], sum_m, mask_ref)
            s0 = state[h]
            st_ref[sub, h] = s0
            o = _dot(c["attn"], v) + _dot(c["q"] * jnp.exp(c["b"]), s0, NT)
            k_dec = c["k"] * jnp.exp(c["b_last"] - c["b"])
            decay = jnp.exp(jnp.max(c["b_last"], axis=0, keepdims=True))
            state[h] = s0 * decay + _dot(v, k_dec, TN)
            o_ref[rows, hs] = o
            r = lax.rsqrt(jnp.mean(o * o, axis=-1, keepdims=True) + NORM_EPS)
            z_ref[rows, hs] = (o * r * g_ref[:, hs] * jax.nn.sigmoid(p_ref[rows, col(3)])).astype(BF16)

    full = lambda shape: pl.BlockSpec(shape, lambda c: (0,) * len(shape))
    return pl.pallas_call(
        body, name="hgrn_fwd", grid=(nch // per,),
        in_specs=[pl.BlockSpec((step, 4 * WIDTH), lambda c: (c, 0)), full((1, WIDTH)), full((1, WIDTH)),
                  full(sum_mat.shape), full(masks.shape)],
        out_specs=[pl.BlockSpec((step, WIDTH), lambda c: (c, 0)),
                   pl.BlockSpec((None, step, WIDTH), lambda c: (0, c, 0)),
                   pl.BlockSpec((per, N_HEAD, HEAD, HEAD), lambda c: (c, 0, 0, 0))],
        out_shape=[jax.ShapeDtypeStruct((T, WIDTH), F32), jax.ShapeDtypeStruct((N_BRANCH, T, WIDTH), BF16),
                   jax.ShapeDtypeStruct((nch, N_HEAD, HEAD, HEAD), F32)],
        scratch_shapes=[pltpu.VMEM((N_HEAD, HEAD, HEAD), F32)], compiler_params=_params(1),
    )(p, lbrow, gout, sum_mat, masks)


def _hgrn_bwd(p, o_saved, dz, states, lbrow, gout, dp, after=()):
    T = p.shape[0]
    nch = T // H_CHUNK
    L = H_CHUNK
    per = 2 if nch % 2 == 0 else 1
    step, subs = per * L, range(per - 1, -1, -1)
    sum_mat, masks = _hgrn_consts()

    def body(p_ref, o_ref, dz_ref, st_ref, lb_ref, g_ref, m_ref, mask_ref, dp_in, *rest):
        del dp_in
        dp_ref, dlb_ref, dg_ref, dstate = rest[len(after):]

        @pl.when(pl.program_id(0) == 0)
        def _():
            dstate[...] = jnp.zeros_like(dstate)
            dlb_ref[...] = jnp.zeros_like(dlb_ref)
            dg_ref[...] = jnp.zeros_like(dg_ref)

        sum_m = m_ref[...]
        for sub, h in [(s_, h_) for s_ in subs for h_ in range(N_HEAD)]:
            rows = pl.ds(sub * H_CHUNK, H_CHUNK)
            col = lambda part: pl.ds(part * WIDTH + h * HEAD, HEAD)
            hs = pl.ds(h * HEAD, HEAD)
            qraw, fp, v, go = p_ref[rows, col(0)], p_ref[rows, col(1)], p_ref[rows, col(2)], p_ref[rows, col(3)]
            lb, g = lb_ref[:, hs], g_ref[:, hs]
            c = _hgrn_core(qraw, fp, lb, sum_m, mask_ref)
            q, k, b, b_last = c["q"], c["k"], c["b"], c["b_last"]
            s0, ds1 = st_ref[sub, h], dstate[h]
            e_b = jnp.exp(b)
            q_dec = q * e_b
            e_bl = jnp.exp(b_last - b)
            k_dec = k * e_bl
            decay = jnp.exp(jnp.max(b_last, axis=0, keepdims=True))
            o = o_ref[rows, hs]
            r = lax.rsqrt(jnp.mean(o * o, axis=-1, keepdims=True) + NORM_EPS)
            n = o * r
            sgo = jax.nn.sigmoid(go)
            dza = dz_ref[rows, hs]
            dgo = dza * n * g * sgo * (1.0 - sgo)
            dg_ref[:, hs] += _row0(jnp.sum(dza * n * sgo, axis=0, keepdims=True))
            dn = dza * g * sgo
            do = r * (dn - n * jnp.mean(dn * n, axis=-1, keepdims=True))
            dattn = _dot(do, v, NT)
            dv = _dot(c["attn"], do, TN) + _dot(k_dec, ds1, NT)
            dq_dec = _dot(do, s0)
            dk_dec = _dot(v, ds1)
            ddiag = jnp.sum(do * v, axis=1, keepdims=True)
            dq = dq_dec * e_b + ddiag * k
            dk = dk_dec * e_bl + ddiag * q
            dsums = [dq_dec * q_dec - dk_dec * k_dec]
            for l in range(N_LEVEL):
                dm = mask_ref[l] * dattn
                da = _dot(dm, c["fb"][l])
                db = _dot(dm, c["fa"][l], TN)
                dq = dq + da * c["ea"][l]
                dk = dk + db * c["eb"][l]
                dsums.append(da * c["fa"][l] - db * c["fb"][l])
            dlast = jnp.sum(ds1 * s0, axis=0, keepdims=True) * decay
            dsums.append(dk_dec * k_dec + _row0(dlast, L))
            dlogf = _dot(sum_m, jnp.concatenate(dsums, axis=0), TN)
            dstate[h] = ds1 * decay + _dot(do, q_dec, TN)
            sq, sneg = c["sq"], c["sneg"]
            dqraw = dq * sq * (1.0 + qraw * (1.0 - sq))
            dfp = dlogf * c["w2"] * sneg - dk * (1.0 - lb) * sneg * (1.0 - sneg)
            inv_lb = jnp.where(lb > LB_FLOOR, 1.0 / jnp.maximum(lb, LB_FLOOR), 0.0)
            dlb_tok = dlogf * (c["w1"] * inv_lb - c["w2"] / (1.0 - lb)) - dk * sneg
            dlb_ref[:, hs] += _row0(jnp.sum(dlb_tok, axis=0, keepdims=True))
            dp_ref[rows, col(0)] = dqraw.astype(BF16)
            dp_ref[rows, col(1)] = dfp.astype(BF16)
            dp_ref[rows, col(2)] = dv.astype(BF16)
            dp_ref[rows, col(3)] = dgo.astype(BF16)

    full = lambda shape: pl.BlockSpec(shape, lambda c: (0,) * len(shape))
    rev = lambda c: nch // per - 1 - c
    return pl.pallas_call(
        body, name="hgrn_bwd", grid=(nch // per,),
        in_specs=[pl.BlockSpec((step, 4 * WIDTH), lambda c: (rev(c), 0)), pl.BlockSpec((step, WIDTH), lambda c: (rev(c), 0)),
                  pl.BlockSpec((None, step, WIDTH), lambda c: (0, rev(c), 0)),
                  pl.BlockSpec((per, N_HEAD, HEAD, HEAD), lambda c: (rev(c), 0, 0, 0)),
                  full((1, WIDTH)), full((1, WIDTH)), full(sum_mat.shape), full(masks.shape), ANY, *[ANY] * len(after)],
        out_specs=[pl.BlockSpec((step, 4 * WIDTH), lambda c: (rev(c), 0)), full((SUBLANES, WIDTH)), full((SUBLANES, WIDTH))],
        out_shape=[jax.ShapeDtypeStruct(dp.shape, dp.dtype), jax.ShapeDtypeStruct((SUBLANES, WIDTH), F32),
                   jax.ShapeDtypeStruct((SUBLANES, WIDTH), F32)],
        scratch_shapes=[pltpu.VMEM((N_HEAD, HEAD, HEAD), F32)], input_output_aliases={8: 0},
        compiler_params=_params(1),
    )(p, o_saved, dz, states, lbrow, gout, sum_mat, masks, dp, *after)


def _shift_down(tile, halo, s):
    tm = tile.shape[0]
    rows = lax.broadcasted_iota(jnp.int32, tile.shape, 0)
    head = jnp.concatenate([pltpu.roll(halo, s, 0), jnp.zeros((tm - SUBLANES, tile.shape[1]), tile.dtype)], axis=0)
    return jnp.where(rows < s, head, pltpu.roll(tile, s, 0))


def _shift_up(tile, halo, s):
    tm = tile.shape[0]
    rows = lax.broadcasted_iota(jnp.int32, tile.shape, 0)
    tail = jnp.concatenate([jnp.zeros((tm - SUBLANES, tile.shape[1]), tile.dtype), pltpu.roll(halo, SUBLANES - s, 0)], axis=0)
    return jnp.where(rows >= tm - s, tail, pltpu.roll(tile, tm - s, 0))


def _conv_fwd(p, w, z, after=()):
    T = p.shape[0]
    tm = min(2 * _token_tile(T), T)
    per = tm // SUBLANES

    def body(bg_ref, cg_ref, xc_ref, hcg_ref, hxc_ref, w_ref, *rest):
        z_ref = rest[-1]
        zc = cg_ref[...] * xc_ref[...]
        hz = jnp.where(pl.program_id(0) > 0, hcg_ref[...] * hxc_ref[...], 0.0)
        y = (w_ref[pl.ds(0, 1), :] * _shift_down(zc, hz, 2) + w_ref[pl.ds(1, 1), :] * _shift_down(zc, hz, 1)
             + w_ref[pl.ds(2, 1), :] * zc)
        z_ref[...] = (bg_ref[...] * y).astype(BF16)

    tile = lambda cb: pl.BlockSpec((tm, WIDTH), lambda i: (i, cb))
    prev = lambda cb: pl.BlockSpec((SUBLANES, WIDTH), lambda i: (jnp.maximum(i * per - 1, 0), cb))
    return pl.pallas_call(
        body, name="conv_fwd", grid=(T // tm,),
        in_specs=[tile(4), tile(5), tile(6), prev(5), prev(6), pl.BlockSpec((CONV_K, WIDTH), lambda i: (0, 0)), ANY,
                  *[ANY] * len(after)],
        out_specs=pl.BlockSpec((None, tm, WIDTH), lambda i: (1, i, 0)),
        out_shape=jax.ShapeDtypeStruct(z.shape, z.dtype), input_output_aliases={6: 0}, compiler_params=_params(1),
    )(p, p, p, p, p, w, z, *after)


def _conv_bwd(p, w, dz, dp):
    T = p.shape[0]
    tm = min(2 * _token_tile(T), T)
    per = tm // SUBLANES
    last = T // SUBLANES - 1

    def body(bg_ref, cg_ref, xc_ref, hcg_ref, hxc_ref, nbg_ref, dzb_ref, ndzb_ref, w_ref, dp_in, dp_ref, dw_ref, stash):
        del dp_in
        i, jj = pl.program_id(0), pl.program_id(1)

        @pl.when(jnp.logical_and(i == 0, jj == 0))
        def _():
            dw_ref[...] = jnp.zeros_like(dw_ref)

        @pl.when(jj == 0)
        def _():
            cg, xc, bg = cg_ref[...], xc_ref[...], bg_ref[...]
            w0, w1, w2 = w_ref[pl.ds(0, 1), :], w_ref[pl.ds(1, 1), :], w_ref[pl.ds(2, 1), :]
            zc = cg * xc
            hz = jnp.where(i > 0, hcg_ref[...] * hxc_ref[...], 0.0)
            z2, z1 = _shift_down(zc, hz, 2), _shift_down(zc, hz, 1)
            y = w0 * z2 + w1 * z1 + w2 * zc
            dzb = dzb_ref[...]
            dy = dzb * bg
            hdy = jnp.where(i < pl.num_programs(0) - 1, ndzb_ref[...] * nbg_ref[...], 0.0)
            dzc = w2 * dy + w1 * _shift_up(dy, hdy, 1) + w0 * _shift_up(dy, hdy, 2)
            rows = lax.broadcasted_iota(jnp.int32, (SUBLANES, WIDTH), 0)
            colsum = lambda t: jnp.sum(t, axis=0, keepdims=True)
            dw_ref[...] += (jnp.where(rows == 0, colsum(dy * z2), 0.0) + jnp.where(rows == 1, colsum(dy * z1), 0.0)
                            + jnp.where(rows == 2, colsum(dy * zc), 0.0))
            dp_ref[...] = (dzb * y).astype(BF16)
            stash[0] = dzc * xc
            stash[1] = dzc * cg

        @pl.when(jj > 0)
        def _():
            dp_ref[...] = stash[jj - 1].astype(BF16)

    n_tiles = T // tm
    tile = lambda cb: pl.BlockSpec((tm, WIDTH), lambda i, jj: (i, cb))
    prev = lambda cb: pl.BlockSpec((SUBLANES, WIDTH), lambda i, jj: (jnp.maximum(i * per - 1, 0), cb))
    nxt = lambda i: jnp.minimum((i + 1) * per, last)
    return pl.pallas_call(
        body, name="conv_bwd", grid=(n_tiles, 3),
        in_specs=[tile(4), tile(5), tile(6), prev(5), prev(6),
                  pl.BlockSpec((SUBLANES, WIDTH), lambda i, jj: (nxt(i), 4)),
                  pl.BlockSpec((None, tm, WIDTH), lambda i, jj: (1, i, 0)),
                  pl.BlockSpec((None, SUBLANES, WIDTH), lambda i, jj: (1, nxt(i), 0)),
                  pl.BlockSpec((CONV_K, WIDTH), lambda i, jj: (0, 0)), ANY],
        out_specs=[pl.BlockSpec((tm, WIDTH), lambda i, jj: (i, 4 + jj)),
                   pl.BlockSpec((SUBLANES, WIDTH), lambda i, jj: (0, 0))],
        out_shape=[jax.ShapeDtypeStruct(dp.shape, dp.dtype), jax.ShapeDtypeStruct((SUBLANES, WIDTH), F32)],
        scratch_shapes=[pltpu.VMEM((2, tm, WIDTH), F32)], input_output_aliases={9: 0}, compiler_params=_params(2),
    )(p, p, p, p, p, p, dz, dz, w, dp)


GELU_C = float(np.sqrt(2.0 / np.pi))
GELU_A = 0.044715


def _gelu(x):
    th = jnp.tanh(GELU_C * (x + GELU_A * x * x * x))
    return 0.5 * x * (1.0 + th), th


def _gelu_grad(x, th):
    return 0.5 * (1.0 + th) + 0.5 * x * (1.0 - th * th) * GELU_C * (1.0 + 3.0 * GELU_A * x * x)


def _sg_core(u, v, lng, lnb, ws_ref, bs_ref):
    gu, thu = _gelu(u)
    gv, thv = _gelu(v)
    xc = gv - jnp.mean(gv, axis=-1, keepdims=True)
    rs = lax.rsqrt(jnp.mean(xc * xc, axis=-1, keepdims=True) + LN_EPS)
    xh = xc * rs
    vp = xh * lng + lnb
    tril = (lax.broadcasted_iota(jnp.int32, (SG_CHUNK, SG_CHUNK), 0)
            >= lax.broadcasted_iota(jnp.int32, (SG_CHUNK, SG_CHUNK), 1))
    wm = [jnp.where(tril, ws_ref[g], 0.0).astype(BF16) for g in range(SG_GROUPS)]
    gs = lambda t, g: t[:, g * LANES:(g + 1) * LANES]
    sv = jnp.concatenate([_dot(wm[g], gs(vp, g)) + bs_ref[g] for g in range(SG_GROUPS)], axis=1)
    return dict(gu=gu, thu=thu, thv=thv, rs=rs, xh=xh, vp=vp, tril=tril, wm=wm, sv=sv)


def _sg_step_rows(T):
    return 2 * SG_CHUNK if T % (2 * SG_CHUNK) == 0 else SG_CHUNK


def _sg_fwd(p, lng, lnb, ws, bs, z):
    T = p.shape[0]
    step = _sg_step_rows(T)

    def body(u_ref, v_ref, lng_ref, lnb_ref, ws_ref, bs_ref, z_in, z_ref):
        del z_in
        for sub in range(step // SG_CHUNK):
            rows = pl.ds(sub * SG_CHUNK, SG_CHUNK)
            c = _sg_core(u_ref[rows, :], v_ref[rows, :], lng_ref[...], lnb_ref[...], ws_ref, bs_ref)
            z_ref[rows, :] = (c["gu"] * c["sv"]).astype(BF16)

    full = lambda shape: pl.BlockSpec(shape, lambda c: (0,) * len(shape))
    return pl.pallas_call(
        body, name="sg_fwd", grid=(T // step,),
        in_specs=[pl.BlockSpec((step, WIDTH), lambda c: (c, 7)), pl.BlockSpec((step, WIDTH), lambda c: (c, 8)),
                  full((1, WIDTH)), full((1, WIDTH)), full(ws.shape), full(bs.shape), ANY],
        out_specs=pl.BlockSpec((None, step, WIDTH), lambda c: (2, c, 0)),
        out_shape=jax.ShapeDtypeStruct(z.shape, z.dtype), input_output_aliases={6: 0}, compiler_params=_params(1),
    )(p, p, lng, lnb, ws, bs, z)


def _sg_bwd(p, lng, lnb, ws, bs, dz, dp):
    T = p.shape[0]
    step = _sg_step_rows(T)

    def body(u_ref, v_ref, lng_ref, lnb_ref, ws_ref, bs_ref, dz_ref, dp_in, dp_ref, dws_ref, dbs_ref, dlng_ref, dlnb_ref,
             stash):
        del dp_in
        cidx, jj = pl.program_id(0), pl.program_id(1)

        @pl.when(jnp.logical_and(cidx == 0, jj == 0))
        def _():
            dws_ref[...] = jnp.zeros_like(dws_ref)
            dbs_ref[...] = jnp.zeros_like(dbs_ref)
            dlng_ref[...] = jnp.zeros_like(dlng_ref)
            dlnb_ref[...] = jnp.zeros_like(dlnb_ref)

        def chunk(rows):
            u, v, lng = u_ref[rows, :], v_ref[rows, :], lng_ref[...]
            c = _sg_core(u, v, lng, lnb_ref[...], ws_ref, bs_ref)
            dzc = dz_ref[rows, :]
            gs = lambda t, g: t[:, g * LANES:(g + 1) * LANES]
            dsv = dzc * c["gu"]
            dvp = []
            for g in range(SG_GROUPS):
                dsv_g = gs(dsv, g)
                dws_ref[g] += jnp.where(c["tril"], _dot(dsv_g, gs(c["vp"], g), NT), 0.0)
                dbs_ref[g] += jnp.sum(dsv_g, axis=1, keepdims=True)
                dvp.append(_dot(c["wm"][g], dsv_g, TN))
            dvp = jnp.concatenate(dvp, axis=1)
            xh = c["xh"]
            dlng_ref[...] += _row0(jnp.sum(dvp * xh, axis=0, keepdims=True))
            dlnb_ref[...] += _row0(jnp.sum(dvp, axis=0, keepdims=True))
            dxh = dvp * lng
            dgv = c["rs"] * (dxh - jnp.mean(dxh, axis=-1, keepdims=True) - xh * jnp.mean(dxh * xh, axis=-1, keepdims=True))
            dp_ref[rows, :] = (dzc * c["sv"] * _gelu_grad(u, c["thu"])).astype(BF16)
            stash[rows, :] = dgv * _gelu_grad(v, c["thv"])

        @pl.when(jj == 0)
        def _():
            for sub in range(step // SG_CHUNK):
                chunk(pl.ds(sub * SG_CHUNK, SG_CHUNK))

        @pl.when(jj == 1)
        def _():
            dp_ref[...] = stash[...].astype(BF16)

    full = lambda shape: pl.BlockSpec(shape, lambda c, jj: (0,) * len(shape))
    return pl.pallas_call(
        body, name="sg_bwd", grid=(T // step, 2),
        in_specs=[pl.BlockSpec((step, WIDTH), lambda c, jj: (c, 7)), pl.BlockSpec((step, WIDTH), lambda c, jj: (c, 8)),
                  full((1, WIDTH)), full((1, WIDTH)), full(ws.shape), full(bs.shape),
                  pl.BlockSpec((None, step, WIDTH), lambda c, jj: (2, c, 0)), ANY],
        out_specs=[pl.BlockSpec((step, WIDTH), lambda c, jj: (c, 7 + jj)), full(ws.shape), full(bs.shape),
                   full((SUBLANES, WIDTH)), full((SUBLANES, WIDTH))],
        out_shape=[jax.ShapeDtypeStruct(dp.shape, dp.dtype), jax.ShapeDtypeStruct(ws.shape, F32),
                   jax.ShapeDtypeStruct(bs.shape, F32), jax.ShapeDtypeStruct((SUBLANES, WIDTH), F32),
                   jax.ShapeDtypeStruct((SUBLANES, WIDTH), F32)],
        scratch_shapes=[pltpu.VMEM((step, WIDTH), F32)], input_output_aliases={7: 0}, compiler_params=_params(2),
    )(p, p, lng, lnb, ws, bs, dz, dp)


BRANCH_COLS = D_MODEL // N_CHIP
GATE_UNIT0 = GATE_COL0 // WIDTH
UNITS = D_MODEL // WIDTH


def _unit_specs(order):
    def spec(which):
        def index(*g):
            _, n, u = order(*g)
            return (2 * u + which, n, 0, 0)
        return pl.BlockSpec((None, None, WIDTH, BRANCH_COLS), index)
    return [spec(0), spec(1)]


def _merge_fwd(z, p, wb):
    T = z.shape[1]
    tm = min(2 * _token_tile(T), T)
    order = lambda i, u, n: (i, n, u)

    def body(z_ref, wa_ref, wb_ref, gt_ref, out_ref, acc):
        n = pl.program_id(2)
        zv = z_ref[...]
        y = jnp.concatenate([_dot(zv, wa_ref[...]), _dot(zv, wb_ref[...])], axis=1)
        part = jax.nn.sigmoid(gt_ref[...]) * y

        @pl.when(n == 0)
        def _():
            acc[...] = part

        @pl.when(n > 0)
        def _():
            acc[...] += part

        @pl.when(n == N_BRANCH - 1)
        def _():
            out_ref[...] = acc[...].astype(BF16)

    return pl.pallas_call(
        body, name="merge_fwd", grid=(T // tm, UNITS, N_BRANCH),
        in_specs=[pl.BlockSpec((None, tm, WIDTH), lambda i, u, n: (n, i, 0)), *_unit_specs(order),
                  pl.BlockSpec((tm, WIDTH), lambda i, u, n: (i, GATE_UNIT0 + UNITS * n + u))],
        out_specs=pl.BlockSpec((tm, WIDTH), lambda i, u, n: (i, u)),
        out_shape=jax.ShapeDtypeStruct((T, D_MODEL), BF16),
        scratch_shapes=[pltpu.VMEM((tm, WIDTH), F32)], compiler_params=_params(3))(z, wb, wb, p)


def _merge_bwd(z, p, wb, dmerged):
    T = z.shape[1]
    tm = min(2 * _token_tile(T), T)
    order = lambda n, u, i: (i, n, u)

    def body(z_ref, wa_ref, wb_ref, gt_ref, dm_ref, dp_ref, dw_ref, dz_ref):
        u, i = pl.program_id(1), pl.program_id(2)
        zv, wa, wbv = z_ref[...], wa_ref[...], wb_ref[...]
        y = jnp.concatenate([_dot(zv, wa), _dot(zv, wbv)], axis=1)
        gate = jax.nn.sigmoid(gt_ref[...])
        dm = dm_ref[...]
        dp_ref[...] = (dm * y * gate * (1.0 - gate)).astype(BF16)
        dyv = (dm * gate).astype(BF16)
        dw = _dot(zv, dyv, TN)
        part = _dot(dyv[:, :BRANCH_COLS], wa, NT) + _dot(dyv[:, BRANCH_COLS:], wbv, NT)
        rows = pl.ds(pl.multiple_of(i * tm, tm), tm)

        @pl.when(i == 0)
        def _():
            dw_ref[0] = dw[:, :BRANCH_COLS]
            dw_ref[1] = dw[:, BRANCH_COLS:]

        @pl.when(i > 0)
        def _():
            dw_ref[0] += dw[:, :BRANCH_COLS]
            dw_ref[1] += dw[:, BRANCH_COLS:]

        @pl.when(u == 0)
        def _():
            dz_ref[rows, :] = part

        @pl.when(u > 0)
        def _():
            dz_ref[rows, :] += part

    unit = lambda n, u, i: (i, GATE_UNIT0 + UNITS * n + u)
    return pl.pallas_call(
        body, name="merge_bwd", grid=(N_BRANCH, UNITS, T // tm),
        in_specs=[pl.BlockSpec((None, tm, WIDTH), lambda n, u, i: (n, i, 0)), *_unit_specs(order),
                  pl.BlockSpec((tm, WIDTH), unit), pl.BlockSpec((tm, WIDTH), lambda n, u, i: (i, u))],
        out_specs=[pl.BlockSpec((tm, WIDTH), unit),
                   pl.BlockSpec((2, None, WIDTH, BRANCH_COLS), lambda n, u, i: (u, n, 0, 0)),
                   pl.BlockSpec((None, T, WIDTH), lambda n, u, i: (n, 0, 0))],
        out_shape=[jax.ShapeDtypeStruct((T, IN_COLS), BF16),
                   jax.ShapeDtypeStruct((N_CHIP, N_BRANCH, WIDTH, BRANCH_COLS), F32),
                   jax.ShapeDtypeStruct((N_BRANCH, T, WIDTH), F32)],
        compiler_params=_params(3))(z, wb, wb, p, dmerged)


def _layer_fwd(x, h, win, small, next_gain, hooks):
    p = _mm_cols("in_proj", h, win, [F32])[0]
    o_hgrn, z, states = _hgrn_fwd(p, small["lbs"], small["g_hgrn_out"])
    z = _conv_fwd(p, small["w_conv"], z, after=hooks["after_hgrn"]([o_hgrn]))
    z = _sg_fwd(p, small["sg_ln_g"], small["sg_ln_b"], small["w_sg"], small["b_sg"], z)
    wb, wo, w1, w2 = hooks["late_weights"]([z])
    wb = wb.reshape(N_CHIP, N_BRANCH, WIDTH, BRANCH_COLS)
    merged = _merge_fwd(z, p, wb)
    x_mid, h2 = _mm_rows("out_proj", merged, wo, x, small["g_ffn"])
    s = _mm_cols("ff1", h2, w1, [BF16], epilogue=lambda acc: (jnp.square(jnp.maximum(acc, 0.0)),))[0]
    if next_gain is None:
        x_out, h_next = _mm_rows("ff2_last", s, w2, x_mid, after=hooks["before_last"]([s])), None
    else:
        x_out, h_next = _mm_rows("ff2", s, w2, x_mid, next_gain, after=hooks["before_last"]([s]))
    saved = dict(x=x, h=h, p=p, o_hgrn=o_hgrn, z=z, states=states, merged=merged, x_mid=x_mid, h2=h2, s=s)
    return x_out, h_next, saved, [win, wb, wo, w1, w2]


def _layer_bwd(dx_out, sv, wts, small, tick, after):
    win, wb, wo, w1, w2 = wts
    g = {}
    da = _mm_cols_t("ff2_dgrad", dx_out, w2, BF16, extra=(sv["s"],), after=after,
                    epilogue=lambda acc, s: (acc * 2.0 * jnp.sqrt(s.astype(F32)),))
    d_ff2 = _mm_wgrad("ff2_wgrad", sv["s"], dx_out, w2.shape[1], D_MODEL, True, False)
    d_ff1 = _mm_wgrad("ff1_wgrad", sv["h2"], da, D_MODEL, w1.shape[2], False, True)
    dx_mid, g["g_ffn"] = _dgrad_norm_bwd("ff1_dgrad", da, w1, sv["x_mid"], small["g_ffn"], dx_out)
    after = tick([dx_mid], [("w_ff1", d_ff1), ("w_ff2", d_ff2)])
    dmerged = _mm_cols_t("out_proj_dgrad", dx_mid, wo, F32, after=after)
    d_o = _mm_wgrad("out_proj_wgrad", sv["merged"], dx_mid, wo.shape[1], D_MODEL, True, False)
    dp, d_branch, dz = _merge_bwd(sv["z"], sv["p"], wb, dmerged)
    d_branch = d_branch.reshape(N_CHIP, N_BRANCH * WIDTH, BRANCH_COLS)
    dp, g["w_conv"] = _conv_bwd(sv["p"], small["w_conv"], dz, dp)
    dp, g["w_sg"], g["b_sg"], g["sg_ln_g"], g["sg_ln_b"] = _sg_bwd(
        sv["p"], small["sg_ln_g"], small["sg_ln_b"], small["w_sg"], small["b_sg"], dz, dp)
    after = tick([dp], [("w_branch", d_branch), ("w_o", d_o)])
    dp, g["lbs"], g["g_hgrn_out"] = _hgrn_bwd(sv["p"], sv["o_hgrn"], dz, sv["states"], small["lbs"],
                                              small["g_hgrn_out"], dp, after=after)
    after = tick([dp], [])
    d_in = _mm_wgrad("in_proj_wgrad", sv["h"], dp, D_MODEL, win.shape[2], False, True, after=after)
    dx, g["g_mix"] = _dgrad_norm_bwd("in_proj_dgrad", dp, win, sv["x"], small["g_mix"], dx_mid)
    return dx, g, tick([dx], [("w_in", d_in)])


def _mesh_pos():
    return lax.axis_index("x"), lax.axis_index("y"), lax.axis_index("c")


def _other_chips(x, y):
    return [(1 - x, y), (x, 1 - y), (1 - x, 1 - y)]


def _remote(src, dst, send_sems, recv_sems, k, to):
    return pltpu.make_async_remote_copy(src_ref=src, dst_ref=dst, send_sem=send_sems.at[k], recv_sem=recv_sems.at[k],
                                        device_id=to, device_id_type=MESH)


def _gather_call(name, body, buf, after):
    scratch = [pltpu.SemaphoreType.DMA((7,)), pltpu.SemaphoreType.DMA((7,))]
    return pl.pallas_call(
        body, name=name, in_specs=[ANY] * (1 + len(after)), out_specs=ANY,
        out_shape=jax.ShapeDtypeStruct(buf.shape, buf.dtype), scratch_shapes=scratch, input_output_aliases={0: 0})(buf, *after)


HBM = pl.BlockSpec(memory_space=pltpu.HBM)
SEM = pl.BlockSpec(memory_space=pltpu.SEMAPHORE)
DATAFLOW = pltpu.SideEffectType.DATAFLOW_SIDE_EFFECTING


def _split_start(name, bufs, copies, n_copies, after=()):
    n = len(bufs)

    def body(*refs):
        send_sems, recv_sems = refs[n + len(after)], refs[n + len(after) + 1]
        for cp in copies(refs[:n], send_sems, recv_sems):
            cp.start()
        refs[-1][...] = jnp.zeros_like(refs[-1])

    outs = pl.pallas_call(
        body, name=name,
        out_shape=(pltpu.SemaphoreType.DMA((n_copies,)), pltpu.SemaphoreType.DMA((n_copies,)),
                   *[pltpu.HBM(b.shape, b.dtype) for b in bufs], jax.ShapeDtypeStruct((SUBLANES, LANES), F32)),
        in_specs=[HBM] * n + [ANY] * len(after),
        out_specs=(SEM, SEM, *[HBM] * n, pl.BlockSpec(memory_space=pltpu.VMEM)),
        input_output_aliases={t: 2 + t for t in range(n)},
        compiler_params=pltpu.CompilerParams(has_side_effects=DATAFLOW),
    )(*[pltpu.with_memory_space_constraint(b, pltpu.HBM) for b in bufs], *after)
    return outs[0], outs[1], list(outs[2:2 + n]), outs[-1]


def _split_wait(name, started, copies, after):
    send_sems, recv_sems, bufs, _ = started
    n = len(bufs)

    def body(*refs):
        for cp in copies(refs[:n], refs[n], refs[n + 1]):
            cp.wait_send()
            cp.wait_recv()

    return list(pl.pallas_call(
        body, name=name, out_shape=tuple(pltpu.HBM(b.shape, b.dtype) for b in bufs),
        in_specs=[HBM] * n + [SEM, SEM] + [ANY] * len(after), out_specs=tuple([HBM] * n),
        input_output_aliases={t: t for t in range(n)},
        compiler_params=pltpu.CompilerParams(has_side_effects=DATAFLOW),
    )(*bufs, send_sems, recv_sems, *after))


def _weight_ici_copies(refs, send_sems, recv_sems):
    x, y, c = _mesh_pos()
    out = []
    for t, ref in enumerate(refs):
        rh = ref.shape[1] // 2
        mine = ref.at[2 * x + y, pl.ds(c * rh, rh), :]
        out += [_remote(mine, mine, send_sems, recv_sems, 3 * t + j, (*chip, c)) for j, chip in enumerate(_other_chips(x, y))]
    return out


def _weight_d2d_copies(refs, send_sems, recv_sems):
    x, y, c = _mesh_pos()
    out = []
    for t, ref in enumerate(refs):
        rh = ref.shape[1] // 2
        for j, chip in enumerate(_other_chips(x, y)):
            blk = ref.at[2 * chip[0] + chip[1], pl.ds(c * rh, rh), :]
            out.append(_remote(blk, blk, send_sems, recv_sems, 3 * t + j, (x, y, 1 - c)))
    return out


def _swap_part(refs, send_sems, recv_sems, s0):
    x, y, c = _mesh_pos()
    n = len(refs) // 2
    out = []
    for t in range(n):
        rh = refs[t].shape[1] // 2
        out.append(_remote(refs[t].at[:, pl.ds((1 - c) * rh, rh), :], refs[n + t], send_sems, recv_sems, s0 + t, (x, y, 1 - c)))
    return out


def _exchange_part(refs, send_sems, recv_sems, s0):
    x, y, c = _mesh_pos()
    n = len(refs) // 2
    out = []
    for t in range(n):
        for j, chip in enumerate(_other_chips(x, y)):
            out.append(_remote(refs[t].at[2 * chip[0] + chip[1]], refs[n + t].at[j], send_sems, recv_sems, s0 + 3 * t + j,
                               (*chip, c)))
    return out


def _gather_part(refs, send_sems, recv_sems, s0):
    x, y, c = _mesh_pos()
    return [_remote(ref.at[c], ref.at[c], send_sems, recv_sems, s0 + t, (x, y, 1 - c)) for t, ref in enumerate(refs)]


def _all_to_all_copies(refs, send_sems, recv_sems):
    x, y, c = _mesh_pos()
    blk = refs[0].at[4 * x + 2 * y + c]
    peers = [(x, y, 1 - c)] + [(*chip, cc) for chip in _other_chips(x, y) for cc in (c, 1 - c)]
    return [_remote(blk, blk, send_sems, recv_sems, k, peer) for k, peer in enumerate(peers)]


class _GradPipeline:
    def __init__(self, pos):
        self.pos = pos
        self.groups, self.pending, self.count = [], None, 0
        self.reduced = {n: [None] * DEPTH for n in BIG_NAMES}

    def busy(self):
        return bool(self.groups) or self.pending is not None

    def tick(self, deps, new):
        if self.pending is not None:
            started, copies, owners = self.pending
            bufs = _split_wait("grad_pipe_wait_%d" % self.count, started, copies, after=list(deps))
            for grp, lo, hi in owners:
                grp["bufs"] = bufs[lo:hi]
            self.pending = None
        parts = []
        for grp in list(self.groups):
            n, names = len(grp["names"]), grp["names"]
            if grp["stage"] == "swap":
                pair = [_pair_sum("grad_pair_sum_" + nm, f, r, self.pos)
                        for nm, f, r in zip(names, grp["bufs"][:n], grp["bufs"][n:])]
                grp["own32"] = [p32 for p32, _ in pair]
                landing = [lax.empty((3, *p16.shape[1:]), BF16) for _, p16 in pair]
                grp["stage"] = "exchange"
                parts.append((grp, [p16 for _, p16 in pair] + landing, _exchange_part, 3 * n))
            elif grp["stage"] == "exchange":
                halves = [_chip_sum("grad_chip_sum_" + nm, p32, r, self.pos)
                          for nm, p32, r in zip(names, grp["own32"], grp["bufs"][n:])]
                grp["stage"] = "gather"
                parts.append((grp, halves, _gather_part, n))
            else:
                for nm, b in zip(names, grp["bufs"]):
                    self.reduced[nm][grp["layer"]] = b.reshape(-1, b.shape[-1])
                self.groups.remove(grp)
        if new:
            grp = dict(names=[nm for nm, _, _ in new], layer=new[0][1], stage="swap")
            self.groups.append(grp)
            fulls = [g for _, _, g in new]
            landing = [lax.empty((N_CHIP, g.shape[1] // 2, g.shape[2]), F32) for g in fulls]
            parts.append((grp, fulls + landing, _swap_part, len(fulls)))
        if not parts:
            return ()
        bufs, layout, owners, sems = [], [], [], 0
        for grp, part_bufs, fn, n_sems in parts:
            layout.append((len(bufs), len(bufs) + len(part_bufs), fn, sems))
            owners.append((grp, len(bufs), len(bufs) + len(part_bufs)))
            bufs += part_bufs
            sems += n_sems

        def copies(refs, send_sems, recv_sems):
            out = []
            for lo, hi, fn, s0 in layout:
                out += fn(refs[lo:hi], send_sems, recv_sems, s0)
            return out

        started = _split_start("grad_pipe_start_%d" % self.count, bufs, copies, sems)
        self.pending = (started, copies, owners)
        self.count += 1
        return (started[3],)


def _gather_all(name, block, slot, after=()):
    buf = lax.dynamic_update_slice(jnp.zeros((8, *block.shape), block.dtype), block[None], (slot, 0, 0))

    def body(*refs):
        out_ref, send_sems, recv_sems = refs[1 + len(after):]
        x, y, c = _mesh_pos()
        chips = _other_chips(x, y)
        sibling = (x, y, 1 - c)
        slot_of = lambda px, py, pc: out_ref.at[4 * px + 2 * py + pc]
        started = [_remote(slot_of(x, y, c), slot_of(x, y, c), send_sems, recv_sems, 0, sibling)]
        started += [_remote(slot_of(x, y, c), slot_of(x, y, c), send_sems, recv_sems, 1 + j, (*chip, c))
                    for j, chip in enumerate(chips)]
        for cp in started:
            cp.start()
        for j, chip in enumerate(chips):
            _remote(slot_of(*chip, c), slot_of(*chip, c), send_sems, recv_sems, 1 + j, (*chip, c)).wait_recv()
            fw = _remote(slot_of(*chip, c), slot_of(*chip, c), send_sems, recv_sems, 4 + j, sibling)
            fw.start()
            started.append(fw)
        _remote(slot_of(x, y, 1 - c), slot_of(x, y, 1 - c), send_sems, recv_sems, 0, sibling).wait_recv()
        for j, chip in enumerate(chips):
            _remote(slot_of(*chip, 1 - c), slot_of(*chip, 1 - c), send_sems, recv_sems, 4 + j, sibling).wait_recv()
        for cp in started:
            cp.wait_send()

    return _gather_call(name, body, buf, after)


def _row_tile(rows, cols, block_bytes=ELEMWISE_BLOCK_BYTES):
    cap = max(SUBLANES, block_bytes // (4 * cols))
    tr = rows
    while tr > cap and tr % 2 == 0:
        tr //= 2
    return tr


def _pair_sum(name, grad, recv, pos):
    _, rh, cols = recv.shape
    tr = _row_tile(rh, cols, 2 * ELEMWISE_BLOCK_BYTES)
    per = rh // tr

    def body(pos_ref, g_ref, r_ref, own_ref, out16_ref):
        s = g_ref[...] + r_ref[...]
        out16_ref[...] = s.astype(BF16)

        @pl.when(pl.program_id(1) == pos_ref[0])
        def _():
            own_ref[...] = s

    blk = pl.BlockSpec((None, tr, cols), lambda i, k, pos_ref: (k, i, 0))
    return pl.pallas_call(
        body, name=name,
        grid_spec=pltpu.PrefetchScalarGridSpec(
            num_scalar_prefetch=1, grid=(per, N_CHIP),
            in_specs=[pl.BlockSpec((None, tr, cols), lambda i, k, pos_ref: (k, pos_ref[1] * per + i, 0)), blk],
            out_specs=[pl.BlockSpec((tr, cols), lambda i, k, pos_ref: (i, 0)), blk]),
        out_shape=[jax.ShapeDtypeStruct((rh, cols), F32), jax.ShapeDtypeStruct(recv.shape, BF16)],
        compiler_params=_params(2))(pos, grad, recv)


def _chip_sum(name, own32, recv, pos):
    rh, cols = own32.shape
    tr = _row_tile(rh, cols, 2 * ELEMWISE_BLOCK_BYTES)

    def body(pos_ref, own_ref, r_ref, out_ref):
        del pos_ref
        out_ref[...] = ((own_ref[...] + r_ref[0].astype(F32)) + r_ref[1].astype(F32)) + r_ref[2].astype(F32)

    return pl.pallas_call(
        body, name=name,
        grid_spec=pltpu.PrefetchScalarGridSpec(
            num_scalar_prefetch=1, grid=(rh // tr,),
            in_specs=[pl.BlockSpec((tr, cols), lambda i, pos_ref: (i, 0)),
                      pl.BlockSpec((3, tr, cols), lambda i, pos_ref: (0, i, 0))],
            out_specs=pl.BlockSpec((None, tr, cols), lambda i, pos_ref: (pos_ref[1], i, 0))),
        out_shape=jax.ShapeDtypeStruct((2, rh, cols), F32), compiler_params=_params(1))(pos, own32, recv)


def _cast_into_slot(name, w, layer, pos, after=()):
    _, rows, cols = w.shape
    tr = _row_tile(rows, cols, 2 * ELEMWISE_BLOCK_BYTES)

    def body(pos_ref, w_ref, *rest):
        del pos_ref
        rest[-1][...] = w_ref[...].astype(BF16)

    return pl.pallas_call(
        body, name=name,
        grid_spec=pltpu.PrefetchScalarGridSpec(
            num_scalar_prefetch=1, grid=(rows // tr,),
            in_specs=[pl.BlockSpec((None, tr, cols), lambda i, pos_ref: (layer, i, 0))] + [ANY] * len(after),
            out_specs=pl.BlockSpec((None, tr, cols), lambda i, pos_ref: (pos_ref[0], i, 0))),
        out_shape=jax.ShapeDtypeStruct((N_CHIP, rows, cols), BF16), compiler_params=_params(1))(pos, w, *after)


def _adamw_math(w, g, m, v):
    m = ADAM_B1 * m + (1.0 - ADAM_B1) * g
    v = ADAM_B2 * v + (1.0 - ADAM_B2) * jnp.square(g)
    m_hat = m / (1.0 - ADAM_B1 ** ADAM_STEP)
    v_hat = v / (1.0 - ADAM_B2 ** ADAM_STEP)
    delta = -ADAM_LR * (m_hat / (jnp.sqrt(v_hat) + ADAM_EPS) + ADAM_WD * w)
    return delta, m, v


def _adamw_layers(name, w, m, v, grads, first, into=None, after=()):
    _, rows, cols = w.shape
    tr = _row_tile(rows, cols)
    n_layers = len(grads)

    def body(w_ref, m_ref, v_ref, *rest):
        g_refs, (grad_ref, d_ref, nm_ref, nv_ref) = rest[:n_layers], rest[len(rest) - 4:]
        layer = pl.program_id(0)
        g = g_refs[0][...]
        for l in range(1, n_layers):
            g = jnp.where(layer == l, g_refs[l][...], g)
        grad_ref[...] = g
        d_ref[...], nm_ref[...], nv_ref[...] = _adamw_math(w_ref[...], g, m_ref[...], v_ref[...])

    blk = pl.BlockSpec((None, tr, cols), lambda l, i: (first + l, i, 0))
    g_spec = lambda k: pl.BlockSpec((tr, cols), lambda l, i: (jnp.where(l == k, i, 0), 0))
    passed = list(into or []) + list(after)
    return pl.pallas_call(
        body, name=name, grid=(n_layers, rows // tr),
        in_specs=[blk, blk, blk] + [g_spec(k) for k in range(n_layers)] + [ANY] * len(passed), out_specs=[blk] * 4,
        out_shape=[jax.ShapeDtypeStruct(w.shape, F32)] * 4,
        input_output_aliases={3 + n_layers + t: t for t in range(4)} if into else {},
        compiler_params=_params(2))(w, m, v, *grads, *passed)


def _sum_devices(gathered):
    _, rows, cols = gathered.shape

    def body(g_ref, out_ref):
        s = g_ref[0]
        for d in range(1, 8):
            s = s + g_ref[d]
        out_ref[...] = s

    return pl.pallas_call(body, name="sum_devices", out_shape=jax.ShapeDtypeStruct((rows, cols), F32),
                          compiler_params=pltpu.CompilerParams(vmem_limit_bytes=VMEM_LIMIT_BYTES))(gathered)


def _adamw_flat(w, g, m, v):
    def body(w_ref, g_ref, m_ref, v_ref, d_ref, nm_ref, nv_ref):
        d_ref[...], nm_ref[...], nv_ref[...] = _adamw_math(w_ref[...], g_ref[...], m_ref[...], v_ref[...])

    return pl.pallas_call(body, name="adamw_small", out_shape=[jax.ShapeDtypeStruct(w.shape, F32)] * 3,
                          compiler_params=pltpu.CompilerParams(vmem_limit_bytes=VMEM_LIMIT_BYTES))(w, g, m, v)


SMALL_NAMES = ["g_mix", "lower_bounds", "g_hgrn_out", "w_conv", "sg_ln_g", "sg_ln_b", "w_sg", "b_sg", "g_ffn", "g_final"]
BIG_NAMES = ["w_in", "w_branch", "w_o", "w_ff1", "w_ff2"]
WEIGHT_ORDER = ["w_in", "g_mix", "lower_bounds", "g_hgrn_out", "w_conv", "sg_ln_g", "sg_ln_b", "w_sg", "b_sg", "w_branch",
                "w_o", "g_ffn", "w_ff1", "w_ff2", "g_final"]


def _padded_rows(n):
    return -(-n // SUBLANES) * SUBLANES


def _pack(arrays):
    parts = []
    for a in arrays:
        a = a.reshape(-1, LANES)
        parts.append(jnp.pad(a, ((0, _padded_rows(a.shape[0]) - a.shape[0]), (0, 0))))
    return jnp.concatenate(parts, axis=0)


def _unpack(flat, shapes):
    out, row = [], 0
    for s in shapes:
        n = int(np.prod(s)) // LANES
        out.append(flat[row:row + n].reshape(s))
        row += _padded_rows(n)
    return out


def _as_2d(name, a):
    return a.reshape(DEPTH, N_BRANCH * WIDTH, BRANCH_COLS) if name == "w_branch" else a


def kernel(x, w_in, g_mix, lower_bounds, g_hgrn_out, w_conv, sg_ln_g, sg_ln_b, w_sg, b_sg, w_branch, w_o, g_ffn, w_ff1, w_ff2, g_final, loss_target, m_w_in, m_g_mix, m_lower_bounds, m_g_hgrn_out, m_w_conv, m_sg_ln_g, m_sg_ln_b, m_w_sg, m_b_sg, m_w_branch, m_w_o, m_g_ffn, m_w_ff1, m_w_ff2, m_g_final, v_w_in, v_g_mix, v_lower_bounds, v_g_hgrn_out, v_w_conv, v_sg_ln_g, v_sg_ln_b, v_w_sg, v_b_sg, v_w_branch, v_w_o, v_g_ffn, v_w_ff1, v_w_ff2, v_g_final):
    weights = dict(w_in=w_in, g_mix=g_mix, lower_bounds=lower_bounds, g_hgrn_out=g_hgrn_out, w_conv=w_conv,
                   sg_ln_g=sg_ln_g, sg_ln_b=sg_ln_b, w_sg=w_sg, b_sg=b_sg, w_branch=w_branch, w_o=w_o, g_ffn=g_ffn,
                   w_ff1=w_ff1, w_ff2=w_ff2, g_final=g_final)
    mom1 = dict(w_in=m_w_in, g_mix=m_g_mix, lower_bounds=m_lower_bounds, g_hgrn_out=m_g_hgrn_out, w_conv=m_w_conv,
                sg_ln_g=m_sg_ln_g, sg_ln_b=m_sg_ln_b, w_sg=m_w_sg, b_sg=m_b_sg, w_branch=m_w_branch, w_o=m_w_o,
                g_ffn=m_g_ffn, w_ff1=m_w_ff1, w_ff2=m_w_ff2, g_final=m_g_final)
    mom2 = dict(w_in=v_w_in, g_mix=v_g_mix, lower_bounds=v_lower_bounds, g_hgrn_out=v_g_hgrn_out, w_conv=v_w_conv,
                sg_ln_g=v_sg_ln_g, sg_ln_b=v_sg_ln_b, w_sg=v_w_sg, b_sg=v_b_sg, w_branch=v_w_branch, w_o=v_w_o,
                g_ffn=v_g_ffn, w_ff1=v_w_ff1, w_ff2=v_w_ff2, g_final=v_g_final)
    xi, yi, ci = _mesh_pos()
    pos = jnp.stack([2 * xi + yi, ci]).astype(jnp.int32)
    device = 4 * xi + 2 * yi + ci
    conv_cols = w_conv.shape[2]

    conv_all = _gather_all("gather_w_conv", w_conv.reshape(DEPTH * CONV_K, conv_cols), device)
    conv_full = conv_all.reshape(N_CHIP, 2, DEPTH, CONV_K, conv_cols)[:, 0].transpose(1, 2, 0, 3).reshape(DEPTH, CONV_K, WIDTH)

    ici, d2d = {}, {}
    cast = lambda l, names, after: [_cast_into_slot("cast_" + n, _as_2d(n, weights[n]), l, pos, after=after) for n in names]
    token = (conv_all,)
    for l in range(DEPTH):
        for part, names in (("w_in", BIG_NAMES[:1]), ("rest", BIG_NAMES[1:])):
            ici[l, part] = _split_start("weights_ici_start_%d_%s" % (l, part), cast(l, names, token), _weight_ici_copies,
                                        3 * len(names), after=token)
            token = (ici[l, part][3],)
    lbs = _lbs_fwd(lower_bounds)

    def forward_to_sibling(l, part, deps):
        landed = _split_wait("weights_ici_wait_%d_%s" % (l, part), ici.pop((l, part)), _weight_ici_copies, after=deps)
        d2d[l, part] = _split_start("weights_d2d_start_%d_%s" % (l, part), landed, _weight_d2d_copies, 3 * len(landed))
        return (d2d[l, part][3],)

    def gathered(l, part, deps):
        return _split_wait("weights_d2d_wait_%d_%s" % (l, part), d2d.pop((l, part)), _weight_d2d_copies, after=deps)

    act = x[0]
    normed = _rms_fwd("rms_mix", act, g_mix[0:1], after=token)
    layers = []
    forward_to_sibling(0, "w_in", [normed, lbs])
    for l in range(DEPTH):
        small = dict(g_mix=g_mix[l:l + 1], lbs=lbs[l:l + 1], g_hgrn_out=g_hgrn_out[l:l + 1], w_conv=conv_full[l],
                     sg_ln_g=sg_ln_g[l:l + 1], sg_ln_b=sg_ln_b[l:l + 1], w_sg=w_sg[l],
                     b_sg=b_sg[l].reshape(SG_GROUPS, SG_CHUNK, 1), g_ffn=g_ffn[l:l + 1])
        hooks = dict(after_hgrn=lambda deps, l=l: forward_to_sibling(l, "rest", deps),
                     late_weights=lambda deps, l=l: gathered(l, "rest", deps),
                     before_last=(lambda deps, l=l: forward_to_sibling(l + 1, "w_in", deps)) if l + 1 < DEPTH
                     else (lambda deps: ()))
        act, normed, saved, wts = _layer_fwd(act, normed, gathered(l, "w_in", [act])[0], small,
                                             g_mix[l + 1:l + 2] if l + 1 < DEPTH else None, hooks)
        layers.append((wts, small, saved))
    loss_blk, dact, dg_final = _loss_head(act, g_final.reshape(1, D_MODEL), loss_target[0])

    pipe = _GradPipeline(pos)
    small_grads = [None] * DEPTH
    after = ()
    for l in reversed(range(DEPTH)):
        wts, small, saved = layers[l]
        tick = lambda deps, new, l=l: pipe.tick(deps, [(nm, l, g) for nm, g in new])
        dact, small_grads[l], after = _layer_bwd(dact, saved, wts, small, tick, after)
    grad_x = dact[None]

    stack = lambda key, rows=None: jnp.stack([small_grads[l][key][0] if rows is None else small_grads[l][key][:rows]
                                              for l in range(DEPTH)])
    local_small = dict(
        g_mix=stack("g_mix"), lower_bounds=stack("lbs"), g_hgrn_out=stack("g_hgrn_out"), w_conv=stack("w_conv", CONV_K),
        sg_ln_g=stack("sg_ln_g"), sg_ln_b=stack("sg_ln_b"), w_sg=jnp.stack([small_grads[l]["w_sg"] for l in range(DEPTH)]),
        b_sg=jnp.stack([small_grads[l]["b_sg"].reshape(SG_GROUPS, SG_CHUNK) for l in range(DEPTH)]),
        g_ffn=stack("g_ffn"), g_final=dg_final[0])
    shapes = [local_small[n].shape for n in SMALL_NAMES] + [(SUBLANES, LANES)]
    packed = _pack([local_small[n] for n in SMALL_NAMES] + [loss_blk])
    packed = lax.dynamic_update_slice(jnp.zeros((8, *packed.shape), F32), packed[None], (device, 0, 0))
    small_flight = _split_start("small_grads_start", [packed], _all_to_all_copies, 7, after=after)

    def adam(n, first, layer_grads, into=None, after=()):
        return _adamw_layers("adamw_%s_%d" % (n, first), _as_2d(n, weights[n]), _as_2d(n, mom1[n]), _as_2d(n, mom2[n]),
                             layer_grads, first, into, after)

    done = {"w_ff1": adam("w_ff1", 0, pipe.reduced["w_ff1"], after=(small_flight[3],))}
    token = pipe.tick([done["w_ff1"][1]], [])
    done["w_ff2"] = adam("w_ff2", 0, pipe.reduced["w_ff2"], after=token)
    summed = _sum_devices(_split_wait("small_grads_wait", small_flight, _all_to_all_copies, after=[done["w_ff2"][1]])[0])
    parts = _unpack(summed, shapes)
    loss = parts[-1][0, 0]
    small_grad = dict(zip(SMALL_NAMES, parts[:-1]))
    small_grad["lower_bounds"] = _lbs_bwd(lower_bounds, small_grad["lower_bounds"])
    small_grad["w_conv"] = lax.dynamic_slice_in_dim(small_grad["w_conv"], pos[0] * conv_cols, conv_cols, axis=2)
    g_flat = _pack([small_grad[n] for n in SMALL_NAMES])
    d_flat, m_flat, v_flat = _adamw_flat(_pack([weights[n] for n in SMALL_NAMES]), g_flat,
                                         _pack([mom1[n] for n in SMALL_NAMES]), _pack([mom2[n] for n in SMALL_NAMES]))
    small_shapes = [weights[n].shape for n in SMALL_NAMES]
    grads = dict(small_grad)
    delta = dict(zip(SMALL_NAMES, _unpack(d_flat, small_shapes)))
    new_m = dict(zip(SMALL_NAMES, _unpack(m_flat, small_shapes)))
    new_v = dict(zip(SMALL_NAMES, _unpack(v_flat, small_shapes)))

    for n in ("w_o", "w_branch"):
        done[n] = adam(n, 0, pipe.reduced[n], after=(d_flat,))
    token = pipe.tick([done["w_branch"][1]], [])
    rest = adam("w_in", 1, pipe.reduced["w_in"][1:], after=token)
    pipe.tick([rest[1]], [])
    assert not pipe.busy()
    done["w_in"] = adam("w_in", 0, pipe.reduced["w_in"][:1], into=rest)
    for n in BIG_NAMES:
        grads[n], delta[n], new_m[n], new_v[n] = [o.reshape(weights[n].shape) for o in done[n]]

    return (loss, grad_x, *[grads[n] for n in WEIGHT_ORDER], *[delta[n] for n in WEIGHT_ORDER],
            *[new_m[n] for n in WEIGHT_ORDER], *[new_v[n] for n in WEIGHT_ORDER])
```

```python
import numpy as np
import jax
import jax.numpy as jnp
from jax import lax
from jax.experimental import pallas as pl
from jax.experimental.pallas import tpu as pltpu

F32, BF16 = jnp.float32, jnp.bfloat16

D_MODEL = 1024
WIDTH = 512
N_BRANCH = 3
N_HEAD = 4
HEAD = 128
H_CHUNK = 64
CONV_K = 3
SG_CHUNK = 128
SG_GROUPS = 4
D_FF = 4096
DEPTH = 4
N_CHIP = 4
IN_COLS = 9 * WIDTH + N_BRANCH * D_MODEL
GATE_COL0 = 9 * WIDTH
LB_FLOOR = 1e-30
NORM_EPS = 1e-6
LN_EPS = 1e-5
ADAM_LR, ADAM_B1, ADAM_B2, ADAM_EPS, ADAM_WD, ADAM_STEP = 0.001, 0.9, 0.999, 1e-08, 0.01, 10

VMEM_LIMIT_BYTES = 56 * 1024 * 1024
VMEM_BLOCK_BUDGET = 44 * 1024 * 1024
SUBLANES, LANES = 8, 128
ELEMWISE_BLOCK_BYTES = 2 * 1024 * 1024

NN = (((1,), (0,)), ((), ()))
NT = (((1,), (1,)), ((), ()))
TN = (((0,), (0,)), ((), ()))
MESH = pl.DeviceIdType.MESH
ANY = pl.BlockSpec(memory_space=pl.ANY)


def _dot(a, b, dims=NN):
    return lax.dot_general(a.astype(BF16), b.astype(BF16), dims, preferred_element_type=F32)


def _params(n_axes):
    return pltpu.CompilerParams(dimension_semantics=("arbitrary",) * n_axes, vmem_limit_bytes=VMEM_LIMIT_BYTES)


def _row0(part, rows=SUBLANES):
    r = lax.broadcasted_iota(jnp.int32, (rows, part.shape[1]), 0)
    return jnp.where(r == 0, part, 0.0)


def _token_tile(T):
    return min(512, T)


def _matmul(name, a, b, *, dims, grid, a_spec, b_spec, out_specs, out_shapes, acc_shape,
            extra=(), extra_specs=(), epilogue=None, after=()):
    nk = grid[2]
    n_extra, n_out, n_in = len(extra), len(out_shapes), 2 + len(extra) + len(after)
    one_step = nk == 1

    def body(*refs):
        a_ref, b_ref = refs[0], refs[1]
        ex = refs[2:2 + n_extra]
        outs = refs[n_in:n_in + n_out]
        part = _dot(a_ref[...], b_ref[...], dims)

        def finish(total):
            res = epilogue(total, *[e[...] for e in ex]) if epilogue else (total,)
            for o, r in zip(outs, res):
                o[...] = r.astype(o.dtype)

        if one_step:
            finish(part)
            return
        acc = refs[-1]
        kk = pl.program_id(2)

        @pl.when(kk == 0)
        def _():
            acc[...] = part

        @pl.when(kk > 0)
        def _():
            acc[...] += part

        @pl.when(kk == nk - 1)
        def _():
            finish(acc[...])

    return pl.pallas_call(
        body, name=name, grid=grid,
        in_specs=[a_spec, b_spec, *extra_specs, *[ANY] * len(after)], out_specs=list(out_specs),
        out_shape=list(out_shapes), scratch_shapes=[] if one_step else [pltpu.VMEM(acc_shape, F32)],
        compiler_params=_params(3),
    )(a, b, *extra, *after)


def _mm_cols(name, a, w, out_dtypes, epilogue=None, extra=()):
    T, K = a.shape
    N = w.shape[2]
    tm = min(2 * _token_tile(T), T)
    blk = pl.BlockSpec((tm, N), lambda j, i, kk: (i, j))
    return _matmul(
        name, a, w, dims=NN, grid=(N_CHIP, T // tm, 1),
        a_spec=pl.BlockSpec((tm, K), lambda j, i, kk: (i, 0)),
        b_spec=pl.BlockSpec((None, K, N), lambda j, i, kk: (j, 0, 0)),
        out_specs=[blk] * len(out_dtypes),
        out_shapes=[jax.ShapeDtypeStruct((T, N_CHIP * N), dt) for dt in out_dtypes],
        acc_shape=(tm, N), extra=extra, extra_specs=[blk] * len(extra), epilogue=epilogue)


def _mm_rows(name, a, w, res, norm_gain=None, after=()):
    T = a.shape[0]
    K, N = N_CHIP * w.shape[1], w.shape[2]
    tm = _token_tile(T)
    blk = pl.BlockSpec((tm, N), lambda i, j, kk: (i, 0))

    def with_norm(acc, r, gain):
        xv = acc + r
        return xv, xv * lax.rsqrt(jnp.mean(xv * xv, axis=-1, keepdims=True) + NORM_EPS) * gain

    normed = norm_gain is not None
    outs = _matmul(
        name, a, w.reshape(K, N), dims=NN, grid=(T // tm, 1, 1),
        a_spec=pl.BlockSpec((tm, K), lambda i, j, kk: (i, 0)),
        b_spec=pl.BlockSpec((K, N), lambda i, j, kk: (0, 0)),
        out_specs=[blk] * (2 if normed else 1),
        out_shapes=[jax.ShapeDtypeStruct((T, N), F32)] + ([jax.ShapeDtypeStruct((T, N), BF16)] if normed else []),
        acc_shape=(tm, N), extra=(res, norm_gain) if normed else (res,),
        extra_specs=[blk] + ([pl.BlockSpec((1, N), lambda i, j, kk: (0, 0))] if normed else []),
        epilogue=with_norm if normed else (lambda acc, r: (acc + r,)), after=after)
    return outs if normed else outs[0]


def _mm_cols_t(name, g, w, out_dtype, epilogue=None, extra=(), after=()):
    T, N = g.shape
    K = N_CHIP * w.shape[1]
    tm = _token_tile(T) if K <= 2 * D_MODEL else _token_tile(T) // 2
    blk = pl.BlockSpec((tm, K), lambda i, j, kk: (i, 0))
    return _matmul(
        name, g, w.reshape(K, N), dims=NT, grid=(T // tm, 1, 1),
        a_spec=pl.BlockSpec((tm, N), lambda i, j, kk: (i, 0)),
        b_spec=pl.BlockSpec((K, N), lambda i, j, kk: (0, 0), pipeline_mode=pl.Buffered(1)),
        out_specs=[blk], out_shapes=[jax.ShapeDtypeStruct((T, K), out_dtype)], acc_shape=(tm, K),
        extra=extra, extra_specs=[blk] * len(extra), epilogue=epilogue, after=after)[0]


def _dgrad_norm_bwd(name, g, w, x, gain, dres):
    T = g.shape[0]
    K, N = w.shape[1], w.shape[2]
    tm = _token_tile(T)
    whole = w.size * w.dtype.itemsize <= VMEM_BLOCK_BUDGET // 2

    def norm_bwd(i, dhv, x_ref, gain_ref, dres_ref, dx_ref, dgain_ref):
        xv = x_ref[...]
        r = lax.rsqrt(jnp.mean(xv * xv, axis=-1, keepdims=True) + NORM_EPS)
        xn = xv * r
        dxn = dhv * gain_ref[...]
        dx_ref[...] = dres_ref[...] + r * (dxn - xn * jnp.mean(dxn * xn, axis=-1, keepdims=True))

        @pl.when(i == 0)
        def _():
            dgain_ref[...] = jnp.zeros_like(dgain_ref)

        dgain_ref[...] += _row0(jnp.sum(dhv * xn, axis=0, keepdims=True))

    def body_whole(g_ref, w_ref, *rest):
        dhv = _dot(g_ref[:, pl.ds(0, N)], w_ref[0], NT)
        for k in range(1, N_CHIP):
            dhv = dhv + _dot(g_ref[:, pl.ds(k * N, N)], w_ref[k], NT)
        norm_bwd(pl.program_id(0), dhv, *rest)

    def body_steps(g_ref, w_ref, x_ref, gain_ref, dres_ref, dx_ref, dgain_ref, acc):
        kk = pl.program_id(1)
        part = _dot(g_ref[...], w_ref[...], NT)

        @pl.when(kk == 0)
        def _():
            acc[...] = part

        @pl.when(kk > 0)
        def _():
            acc[...] += part

        @pl.when(kk == N_CHIP - 1)
        def _():
            norm_bwd(pl.program_id(0), acc[...], x_ref, gain_ref, dres_ref, dx_ref, dgain_ref)

    tile = pl.BlockSpec((tm, K), lambda i, *kk: (i, 0))
    row = lambda cols: pl.BlockSpec((SUBLANES if cols is None else 1, K), lambda i, *kk: (0, 0))
    if whole:
        g_spec = pl.BlockSpec((tm, N_CHIP * N), lambda i: (i, 0))
        w_spec = pl.BlockSpec(w.shape, lambda i: (0, 0, 0), pipeline_mode=pl.Buffered(1))
    else:
        g_spec = pl.BlockSpec((tm, N), lambda i, kk: (i, kk))
        w_spec = pl.BlockSpec((None, K, N), lambda i, kk: (kk, 0, 0))
    return pl.pallas_call(
        body_whole if whole else body_steps, name=name, grid=(T // tm,) if whole else (T // tm, N_CHIP),
        in_specs=[g_spec, w_spec, tile, row(1), tile], out_specs=[tile, row(None)],
        out_shape=[jax.ShapeDtypeStruct((T, K), F32), jax.ShapeDtypeStruct((SUBLANES, K), F32)],
        scratch_shapes=[] if whole else [pltpu.VMEM((tm, K), F32)],
        compiler_params=_params(1 if whole else 2))(g, w, x, gain, dres)


def _mm_wgrad(name, a, g, a_cols, g_cols, a_blocked, g_blocked, after=()):
    T = a.shape[0]
    tt = T
    while tt > LANES and 2 * 2 * tt * (a_cols + g_cols) + (2 if tt == T else 3) * 4 * a_cols * g_cols > VMEM_BLOCK_BUDGET:
        tt //= 2
    return _matmul(
        name, a, g, dims=TN, grid=(N_CHIP, 1, T // tt),
        a_spec=pl.BlockSpec((tt, a_cols), (lambda j, i, kk: (kk, j)) if a_blocked else (lambda j, i, kk: (kk, 0))),
        b_spec=pl.BlockSpec((tt, g_cols), (lambda j, i, kk: (kk, j)) if g_blocked else (lambda j, i, kk: (kk, 0))),
        out_specs=[pl.BlockSpec((None, a_cols, g_cols), lambda j, i, kk: (j, 0, 0))],
        out_shapes=[jax.ShapeDtypeStruct((N_CHIP, a_cols, g_cols), F32)], acc_shape=(a_cols, g_cols), after=after)[0]


def _rms_fwd(name, x, g, after=()):
    T, Dm = x.shape
    tm = min(256, T)

    def body(x_ref, g_ref, *rest):
        xv = x_ref[...]
        r = lax.rsqrt(jnp.mean(xv * xv, axis=-1, keepdims=True) + NORM_EPS)
        rest[-1][...] = (xv * r * g_ref[...]).astype(BF16)

    return pl.pallas_call(
        body, name=name, grid=(T // tm,),
        in_specs=[pl.BlockSpec((tm, Dm), lambda i: (i, 0)), pl.BlockSpec((1, Dm), lambda i: (0, 0))] + [ANY] * len(after),
        out_specs=pl.BlockSpec((tm, Dm), lambda i: (i, 0)),
        out_shape=jax.ShapeDtypeStruct((T, Dm), BF16), compiler_params=_params(1))(x, g, *after)


def _loss_head(x, g, tgt):
    T, Dm = x.shape
    tm = min(256, T)

    def body(x_ref, g_ref, t_ref, loss_ref, dx_ref, dg_ref):
        xv = x_ref[...]
        gv = g_ref[...]
        r = lax.rsqrt(jnp.mean(xv * xv, axis=-1, keepdims=True) + NORM_EPS)
        xn = xv * r
        err = xn * gv - t_ref[...]
        dy = err * (1.0 / Dm)
        dxn = dy * gv
        dx_ref[...] = r * (dxn - xn * jnp.mean(dxn * xn, axis=-1, keepdims=True))

        @pl.when(pl.program_id(0) == 0)
        def _():
            dg_ref[...] = jnp.zeros_like(dg_ref)
            loss_ref[...] = jnp.zeros_like(loss_ref)

        dg_ref[...] += _row0(jnp.sum(dy * xn, axis=0, keepdims=True))
        part = jnp.sum(jnp.sum(err * err, axis=-1, keepdims=True), axis=0, keepdims=True) * (0.5 / Dm)
        loss_ref[...] += jnp.broadcast_to(part, loss_ref.shape)

    tile = pl.BlockSpec((tm, Dm), lambda i: (i, 0))
    return pl.pallas_call(
        body, name="loss_head", grid=(T // tm,),
        in_specs=[tile, pl.BlockSpec((1, Dm), lambda i: (0, 0)), tile],
        out_specs=[pl.BlockSpec((SUBLANES, LANES), lambda i: (0, 0)), tile,
                   pl.BlockSpec((SUBLANES, Dm), lambda i: (0, 0))],
        out_shape=[jax.ShapeDtypeStruct((SUBLANES, LANES), F32), jax.ShapeDtypeStruct((T, Dm), F32),
                   jax.ShapeDtypeStruct((SUBLANES, Dm), F32)],
        compiler_params=_params(1))(x, g, tgt)


def _softmax_rows(lb_ref):
    rows = [lb_ref[pl.ds(i, 1), :] for i in range(DEPTH)]
    mx = rows[0]
    for r in rows[1:]:
        mx = jnp.maximum(mx, r)
    es = [jnp.exp(r - mx) for r in rows]
    tot = es[0]
    for e in es[1:]:
        tot = tot + e
    return [e / tot for e in es]


def _lbs_fwd(lower_bounds):
    def body(lb_ref, out_ref):
        sm = _softmax_rows(lb_ref)
        run = jnp.zeros_like(sm[0])
        out_ref[pl.ds(0, 1), :] = run
        for i in range(1, DEPTH):
            run = run + sm[i]
            out_ref[pl.ds(i, 1), :] = run

    return pl.pallas_call(body, name="lbs_fwd", out_shape=jax.ShapeDtypeStruct(lower_bounds.shape, F32))(lower_bounds)


def _lbs_bwd(lower_bounds, dlbs):
    def body(lb_ref, d_ref, out_ref):
        sm = _softmax_rows(lb_ref)
        dsm = [jnp.zeros_like(sm[0])]
        for i in range(1, DEPTH):
            acc = d_ref[pl.ds(i, 1), :]
            for l in range(i + 1, DEPTH):
                acc = acc + d_ref[pl.ds(l, 1), :]
            dsm.append(acc)
        inner = dsm[0] * sm[0]
        for i in range(1, DEPTH):
            inner = inner + dsm[i] * sm[i]
        for i in range(DEPTH):
            out_ref[pl.ds(i, 1), :] = sm[i] * (dsm[i] - inner)

    return pl.pallas_call(body, name="lbs_bwd", out_shape=jax.ShapeDtypeStruct(lower_bounds.shape, F32))(lower_bounds, dlbs)


N_LEVEL = 6


def _hgrn_consts():
    L = H_CHUNK
    t = np.arange(L)
    blocks = [(t[:, None] >= t[None, :]).astype(np.float32)]
    masks = []
    m = L // 2
    while m >= 1:
        blk, pos = t // (2 * m), t % (2 * m)
        start = blk * 2 * m
        mat = np.zeros((L, L), np.float32)
        for r in range(L):
            if pos[r] >= m:
                mat[r, start[r] + m:r + 1] = 1.0
            else:
                mat[r, r + 1:start[r] + m] = -1.0
        blocks.append(mat)
        masks.append(((blk[:, None] == blk[None, :]) & (pos[:, None] >= m) & (pos[None, :] < m)).astype(np.float32))
        m //= 2
    blocks.append(np.ones((L, L), np.float32))
    return jnp.asarray(np.concatenate(blocks, 0), BF16), jnp.asarray(np.stack(masks), F32)


def _hgrn_core(qraw, fp, lb, sum_mat, mask_ref):
    L = H_CHUNK
    sq = jax.nn.sigmoid(qraw)
    q = qraw * sq
    sneg = jax.nn.sigmoid(-fp)
    log_sig = jnp.minimum(fp, 0.0) - jnp.log1p(jnp.exp(-jnp.abs(fp)))
    a1 = jnp.log(jnp.maximum(lb, LB_FLOOR))
    a2 = jnp.log1p(-lb) + log_sig
    logf = jnp.maximum(a1, a2) + jnp.log1p(jnp.exp(-jnp.abs(a1 - a2)))
    w1 = jnp.exp(a1 - logf)
    w2 = jnp.exp(a2 - logf)
    k = (1.0 - lb) * sneg
    hi = logf.astype(BF16)
    r1 = logf - hi.astype(F32)
    mid = r1.astype(BF16)
    lo = (r1 - mid.astype(F32)).astype(BF16)
    sums = lax.dot_general(sum_mat, jnp.concatenate([hi, mid, lo], axis=1), NN, preferred_element_type=F32)
    sums = sums[:, 0:HEAD] + sums[:, HEAD:2 * HEAD] + sums[:, 2 * HEAD:3 * HEAD]
    b = sums[0:L]
    b_last = sums[(N_LEVEL + 1) * L:(N_LEVEL + 2) * L]
    eye = lax.broadcasted_iota(jnp.int32, (L, L), 0) == lax.broadcasted_iota(jnp.int32, (L, L), 1)
    attn = jnp.where(eye, jnp.sum(q * k, axis=1, keepdims=True), 0.0)
    fa, fb, ea, eb = [], [], [], []
    for l in range(N_LEVEL):
        d = sums[(l + 1) * L:(l + 2) * L]
        e_a = jnp.exp(jnp.minimum(d, 0.0))
        e_b = jnp.exp(jnp.minimum(-d, 0.0))
        a_l, b_l = q * e_a, k * e_b
        attn = attn + mask_ref[l] * _dot(a_l, b_l, NT)
        fa.append(a_l), fb.append(b_l), ea.append(e_a), eb.append(e_b)
    return dict(sq=sq, q=q, sneg=sneg, logf=logf, w1=w1, w2=w2, k=k, b=b, b_last=b_last, attn=attn,
                fa=fa, fb=fb, ea=ea, eb=eb)


def _hgrn_fwd(p, lbrow, gout):
    T = p.shape[0]
    nch = T // H_CHUNK
    per = 4 if nch % 4 == 0 else 2 if nch % 2 == 0 else 1
    step, subs = per * H_CHUNK, range(per)
    sum_mat, masks = _hgrn_consts()

    def body(p_ref, lb_ref, g_ref, m_ref, mask_ref, o_ref, z_ref, st_ref, state):
        @pl.when(pl.program_id(0) == 0)
        def _():
            state[...] = jnp.zeros_like(state)

        sum_m = m_ref[...]
        for sub, h in [(s_, h_) for s_ in subs for h_ in range(N_HEAD)]:
            rows = pl.ds(sub * H_CHUNK, H_CHUNK)
            col = lambda part: pl.ds(part * WIDTH + h * HEAD, HEAD)
            hs = pl.ds(h * HEAD, HEAD)
            v = p_ref[rows, col(2)]
            c = _hgrn_core(p_ref[rows, col(0)], p_ref[rows, col(1)], lb_ref[:, hs], sum_m, mask_ref)
            s0 = state[h]
            st_ref[sub, h] = s0
            o = _dot(c["attn"], v) + _dot(c["q"] * jnp.exp(c["b"]), s0, NT)
            k_dec = c["k"] * jnp.exp(c["b_last"] - c["b"])
            decay = jnp.exp(jnp.max(c["b_last"], axis=0, keepdims=True))
            state[h] = s0 * decay + _dot(v, k_dec, TN)
            o_ref[rows, hs] = o
            r = lax.rsqrt(jnp.mean(o * o, axis=-1, keepdims=True) + NORM_EPS)
            z_ref[rows, hs] = (o * r * g_ref[:, hs] * jax.nn.sigmoid(p_ref[rows, col(3)])).astype(BF16)

    full = lambda shape: pl.BlockSpec(shape, lambda c: (0,) * len(shape))
    return pl.pallas_call(
        body, name="hgrn_fwd", grid=(nch // per,),
        in_specs=[pl.BlockSpec((step, 4 * WIDTH), lambda c: (c, 0)), full((1, WIDTH)), full((1, WIDTH)),
                  full(sum_mat.shape), full(masks.shape)],
        out_specs=[pl.BlockSpec((step, WIDTH), lambda c: (c, 0)),
                   pl.BlockSpec((None, step, WIDTH), lambda c: (0, c, 0)),
                   pl.BlockSpec((per, N_HEAD, HEAD, HEAD), lambda c: (c, 0, 0, 0))],
        out_shape=[jax.ShapeDtypeStruct((T, WIDTH), F32), jax.ShapeDtypeStruct((N_BRANCH, T, WIDTH), BF16),
                   jax.ShapeDtypeStruct((nch, N_HEAD, HEAD, HEAD), F32)],
        scratch_shapes=[pltpu.VMEM((N_HEAD, HEAD, HEAD), F32)], compiler_params=_params(1),
    )(p, lbrow, gout, sum_mat, masks)


def _hgrn_bwd(p, o_saved, dz, states, lbrow, gout, dp, after=()):
    T = p.shape[0]
    nch = T // H_CHUNK
    L = H_CHUNK
    per = 4 if nch % 4 == 0 else 2 if nch % 2 == 0 else 1
    step, subs = per * L, range(per - 1, -1, -1)
    sum_mat, masks = _hgrn_consts()

    def body(p_ref, o_ref, dz_ref, st_ref, lb_ref, g_ref, m_ref, mask_ref, dp_in, *rest):
        del dp_in
        dp_ref, dlb_ref, dg_ref, dstate = rest[len(after):]

        @pl.when(pl.program_id(0) == 0)
        def _():
            dstate[...] = jnp.zeros_like(dstate)
            dlb_ref[...] = jnp.zeros_like(dlb_ref)
            dg_ref[...] = jnp.zeros_like(dg_ref)

        sum_m = m_ref[...]
        for sub, h in [(s_, h_) for s_ in subs for h_ in range(N_HEAD)]:
            rows = pl.ds(sub * H_CHUNK, H_CHUNK)
            col = lambda part: pl.ds(part * WIDTH + h * HEAD, HEAD)
            hs = pl.ds(h * HEAD, HEAD)
            qraw, fp, v, go = p_ref[rows, col(0)], p_ref[rows, col(1)], p_ref[rows, col(2)], p_ref[rows, col(3)]
            lb, g = lb_ref[:, hs], g_ref[:, hs]
            c = _hgrn_core(qraw, fp, lb, sum_m, mask_ref)
            q, k, b, b_last = c["q"], c["k"], c["b"], c["b_last"]
            s0, ds1 = st_ref[sub, h], dstate[h]
            e_b = jnp.exp(b)
            q_dec = q * e_b
            e_bl = jnp.exp(b_last - b)
            k_dec = k * e_bl
            decay = jnp.exp(jnp.max(b_last, axis=0, keepdims=True))
            o = o_ref[rows, hs]
            r = lax.rsqrt(jnp.mean(o * o, axis=-1, keepdims=True) + NORM_EPS)
            n = o * r
            sgo = jax.nn.sigmoid(go)
            dza = dz_ref[rows, hs]
            dgo = dza * n * g * sgo * (1.0 - sgo)
            dg_ref[:, hs] += _row0(jnp.sum(dza * n * sgo, axis=0, keepdims=True))
            dn = dza * g * sgo
            do = r * (dn - n * jnp.mean(dn * n, axis=-1, keepdims=True))
            dattn = _dot(do, v, NT)
            dv = _dot(c["attn"], do, TN) + _dot(k_dec, ds1, NT)
            dq_dec = _dot(do, s0)
            dk_dec = _dot(v, ds1)
            ddiag = jnp.sum(do * v, axis=1, keepdims=True)
            dq = dq_dec * e_b + ddiag * k
            dk = dk_dec * e_bl + ddiag * q
            dsums = [dq_dec * q_dec - dk_dec * k_dec]
            for l in range(N_LEVEL):
                dm = mask_ref[l] * dattn
                da = _dot(dm, c["fb"][l])
                db = _dot(dm, c["fa"][l], TN)
                dq = dq + da * c["ea"][l]
                dk = dk + db * c["eb"][l]
                dsums.append(da * c["fa"][l] - db * c["fb"][l])
            dlast = jnp.sum(ds1 * s0, axis=0, keepdims=True) * decay
            dsums.append(dk_dec * k_dec + _row0(dlast, L))
            dlogf = _dot(sum_m, jnp.concatenate(dsums, axis=0), TN)
            dstate[h] = ds1 * decay + _dot(do, q_dec, TN)
            sq, sneg = c["sq"], c["sneg"]
            dqraw = dq * sq * (1.0 + qraw * (1.0 - sq))
            dfp = dlogf * c["w2"] * sneg - dk * (1.0 - lb) * sneg * (1.0 - sneg)
            inv_lb = jnp.where(lb > LB_FLOOR, 1.0 / jnp.maximum(lb, LB_FLOOR), 0.0)
            dlb_tok = dlogf * (c["w1"] * inv_lb - c["w2"] / (1.0 - lb)) - dk * sneg
            dlb_ref[:, hs] += _row0(jnp.sum(dlb_tok, axis=0, keepdims=True))
            dp_ref[rows, col(0)] = dqraw.astype(BF16)
            dp_ref[rows, col(1)] = dfp.astype(BF16)
            dp_ref[rows, col(2)] = dv.astype(BF16)
            dp_ref[rows, col(3)] = dgo.astype(BF16)

    full = lambda shape: pl.BlockSpec(shape, lambda c: (0,) * len(shape))
    rev = lambda c: nch // per - 1 - c
    return pl.pallas_call(
        body, name="hgrn_bwd", grid=(nch // per,),
        in_specs=[pl.BlockSpec((step, 4 * WIDTH), lambda c: (rev(c), 0)), pl.BlockSpec((step, WIDTH), lambda c: (rev(c), 0)),
                  pl.BlockSpec((None, step, WIDTH), lambda c: (0, rev(c), 0)),
                  pl.BlockSpec((per, N_HEAD, HEAD, HEAD), lambda c: (rev(c), 0, 0, 0)),
                  full((1, WIDTH)), full((1, WIDTH)), full(sum_mat.shape), full(masks.shape), ANY, *[ANY] * len(after)],
        out_specs=[pl.BlockSpec((step, 4 * WIDTH), lambda c: (rev(c), 0)), full((SUBLANES, WIDTH)), full((SUBLANES, WIDTH))],
        out_shape=[jax.ShapeDtypeStruct(dp.shape, dp.dtype), jax.ShapeDtypeStruct((SUBLANES, WIDTH), F32),
                   jax.ShapeDtypeStruct((SUBLANES, WIDTH), F32)],
        scratch_shapes=[pltpu.VMEM((N_HEAD, HEAD, HEAD), F32)], input_output_aliases={8: 0},
        compiler_params=_params(1),
    )(p, o_saved, dz, states, lbrow, gout, sum_mat, masks, dp, *after)


def _shift_down(tile, halo, s):
    tm = tile.shape[0]
    rows = lax.broadcasted_iota(jnp.int32, tile.shape, 0)
    head = jnp.concatenate([pltpu.roll(halo, s, 0), jnp.zeros((tm - SUBLANES, tile.shape[1]), tile.dtype)], axis=0)
    return jnp.where(rows < s, head, pltpu.roll(tile, s, 0))


def _shift_up(tile, halo, s):
    tm = tile.shape[0]
    rows = lax.broadcasted_iota(jnp.int32, tile.shape, 0)
    tail = jnp.concatenate([jnp.zeros((tm - SUBLANES, tile.shape[1]), tile.dtype), pltpu.roll(halo, SUBLANES - s, 0)], axis=0)
    return jnp.where(rows >= tm - s, tail, pltpu.roll(tile, tm - s, 0))


def _conv_fwd(p, w, z, after=()):
    T = p.shape[0]
    tm = min(2 * _token_tile(T), T)
    per = tm // SUBLANES

    def body(bg_ref, cg_ref, xc_ref, hcg_ref, hxc_ref, w_ref, *rest):
        z_ref = rest[-1]
        zc = cg_ref[...] * xc_ref[...]
        hz = jnp.where(pl.program_id(0) > 0, hcg_ref[...] * hxc_ref[...], 0.0)
        y = (w_ref[pl.ds(0, 1), :] * _shift_down(zc, hz, 2) + w_ref[pl.ds(1, 1), :] * _shift_down(zc, hz, 1)
             + w_ref[pl.ds(2, 1), :] * zc)
        z_ref[...] = (bg_ref[...] * y).astype(BF16)

    tile = lambda cb: pl.BlockSpec((tm, WIDTH), lambda i: (i, cb))
    prev = lambda cb: pl.BlockSpec((SUBLANES, WIDTH), lambda i: (jnp.maximum(i * per - 1, 0), cb))
    return pl.pallas_call(
        body, name="conv_fwd", grid=(T // tm,),
        in_specs=[tile(4), tile(5), tile(6), prev(5), prev(6), pl.BlockSpec((CONV_K, WIDTH), lambda i: (0, 0)), ANY,
                  *[ANY] * len(after)],
        out_specs=pl.BlockSpec((None, tm, WIDTH), lambda i: (1, i, 0)),
        out_shape=jax.ShapeDtypeStruct(z.shape, z.dtype), input_output_aliases={6: 0}, compiler_params=_params(1),
    )(p, p, p, p, p, w, z, *after)


def _conv_bwd(p, w, dz, dp):
    T = p.shape[0]
    tm = min(2 * _token_tile(T), T)
    per = tm // SUBLANES
    last = T // SUBLANES - 1

    def body(bg_ref, cg_ref, xc_ref, hcg_ref, hxc_ref, nbg_ref, dzb_ref, ndzb_ref, w_ref, dp_in, dp_ref, dw_ref, stash):
        del dp_in
        i, jj = pl.program_id(0), pl.program_id(1)

        @pl.when(jnp.logical_and(i == 0, jj == 0))
        def _():
            dw_ref[...] = jnp.zeros_like(dw_ref)

        @pl.when(jj == 0)
        def _():
            cg, xc, bg = cg_ref[...], xc_ref[...], bg_ref[...]
            w0, w1, w2 = w_ref[pl.ds(0, 1), :], w_ref[pl.ds(1, 1), :], w_ref[pl.ds(2, 1), :]
            zc = cg * xc
            hz = jnp.where(i > 0, hcg_ref[...] * hxc_ref[...], 0.0)
            z2, z1 = _shift_down(zc, hz, 2), _shift_down(zc, hz, 1)
            y = w0 * z2 + w1 * z1 + w2 * zc
            dzb = dzb_ref[...]
            dy = dzb * bg
            hdy = jnp.where(i < pl.num_programs(0) - 1, ndzb_ref[...] * nbg_ref[...], 0.0)
            dzc = w2 * dy + w1 * _shift_up(dy, hdy, 1) + w0 * _shift_up(dy, hdy, 2)
            rows = lax.broadcasted_iota(jnp.int32, (SUBLANES, WIDTH), 0)
            colsum = lambda t: jnp.sum(t, axis=0, keepdims=True)
            dw_ref[...] += (jnp.where(rows == 0, colsum(dy * z2), 0.0) + jnp.where(rows == 1, colsum(dy * z1), 0.0)
                            + jnp.where(rows == 2, colsum(dy * zc), 0.0))
            dp_ref[...] = (dzb * y).astype(BF16)
            stash[0] = dzc * xc
            stash[1] = dzc * cg

        @pl.when(jj > 0)
        def _():
            dp_ref[...] = stash[jj - 1].astype(BF16)

    n_tiles = T // tm
    tile = lambda cb: pl.BlockSpec((tm, WIDTH), lambda i, jj: (i, cb))
    prev = lambda cb: pl.BlockSpec((SUBLANES, WIDTH), lambda i, jj: (jnp.maximum(i * per - 1, 0), cb))
    nxt = lambda i: jnp.minimum((i + 1) * per, last)
    return pl.pallas_call(
        body, name="conv_bwd", grid=(n_tiles, 3),
        in_specs=[tile(4), tile(5), tile(6), prev(5), prev(6),
                  pl.BlockSpec((SUBLANES, WIDTH), lambda i, jj: (nxt(i), 4)),
                  pl.BlockSpec((None, tm, WIDTH), lambda i, jj: (1, i, 0)),
                  pl.BlockSpec((None, SUBLANES, WIDTH), lambda i, jj: (1, nxt(i), 0)),
                  pl.BlockSpec((CONV_K, WIDTH), lambda i, jj: (0, 0)), ANY],
        out_specs=[pl.BlockSpec((tm, WIDTH), lambda i, jj: (i, 4 + jj)),
                   pl.BlockSpec((SUBLANES, WIDTH), lambda i, jj: (0, 0))],
        out_shape=[jax.ShapeDtypeStruct(dp.shape, dp.dtype), jax.ShapeDtypeStruct((SUBLANES, WIDTH), F32)],
        scratch_shapes=[pltpu.VMEM((2, tm, WIDTH), F32)], input_output_aliases={9: 0}, compiler_params=_params(2),
    )(p, p, p, p, p, p, dz, dz, w, dp)


GELU_C = float(np.sqrt(2.0 / np.pi))
GELU_A = 0.044715


def _gelu(x):
    th = jnp.tanh(GELU_C * (x + GELU_A * x * x * x))
    return 0.5 * x * (1.0 + th), th


def _gelu_grad(x, th):
    return 0.5 * (1.0 + th) + 0.5 * x * (1.0 - th * th) * GELU_C * (1.0 + 3.0 * GELU_A * x * x)


def _sg_core(u, v, lng, lnb, ws_ref, bs_ref):
    gu, thu = _gelu(u)
    gv, thv = _gelu(v)
    xc = gv - jnp.mean(gv, axis=-1, keepdims=True)
    rs = lax.rsqrt(jnp.mean(xc * xc, axis=-1, keepdims=True) + LN_EPS)
    xh = xc * rs
    vp = xh * lng + lnb
    tril = (lax.broadcasted_iota(jnp.int32, (SG_CHUNK, SG_CHUNK), 0)
            >= lax.broadcasted_iota(jnp.int32, (SG_CHUNK, SG_CHUNK), 1))
    wm = [jnp.where(tril, ws_ref[g], 0.0).astype(BF16) for g in range(SG_GROUPS)]
    gs = lambda t, g: t[:, g * LANES:(g + 1) * LANES]
    sv = jnp.concatenate([_dot(wm[g], gs(vp, g)) + bs_ref[g] for g in range(SG_GROUPS)], axis=1)
    return dict(gu=gu, thu=thu, thv=thv, rs=rs, xh=xh, vp=vp, tril=tril, wm=wm, sv=sv)


def _sg_step_rows(T):
    return next(n * SG_CHUNK for n in (4, 2, 1) if T % (n * SG_CHUNK) == 0)


def _sg_fwd(p, lng, lnb, ws, bs, z):
    T = p.shape[0]
    step = _sg_step_rows(T)

    def body(u_ref, v_ref, lng_ref, lnb_ref, ws_ref, bs_ref, z_in, z_ref):
        del z_in
        for sub in range(step // SG_CHUNK):
            rows = pl.ds(sub * SG_CHUNK, SG_CHUNK)
            c = _sg_core(u_ref[rows, :], v_ref[rows, :], lng_ref[...], lnb_ref[...], ws_ref, bs_ref)
            z_ref[rows, :] = (c["gu"] * c["sv"]).astype(BF16)

    full = lambda shape: pl.BlockSpec(shape, lambda c: (0,) * len(shape))
    return pl.pallas_call(
        body, name="sg_fwd", grid=(T // step,),
        in_specs=[pl.BlockSpec((step, WIDTH), lambda c: (c, 7)), pl.BlockSpec((step, WIDTH), lambda c: (c, 8)),
                  full((1, WIDTH)), full((1, WIDTH)), full(ws.shape), full(bs.shape), ANY],
        out_specs=pl.BlockSpec((None, step, WIDTH), lambda c: (2, c, 0)),
        out_shape=jax.ShapeDtypeStruct(z.shape, z.dtype), input_output_aliases={6: 0}, compiler_params=_params(1),
    )(p, p, lng, lnb, ws, bs, z)


def _sg_bwd(p, lng, lnb, ws, bs, dz, dp):
    T = p.shape[0]
    step = _sg_step_rows(T)

    def body(u_ref, v_ref, lng_ref, lnb_ref, ws_ref, bs_ref, dz_ref, dp_in, dp_ref, dws_ref, dbs_ref, dlng_ref, dlnb_ref,
             stash):
        del dp_in
        cidx, jj = pl.program_id(0), pl.program_id(1)

        @pl.when(jnp.logical_and(cidx == 0, jj == 0))
        def _():
            dws_ref[...] = jnp.zeros_like(dws_ref)
            dbs_ref[...] = jnp.zeros_like(dbs_ref)
            dlng_ref[...] = jnp.zeros_like(dlng_ref)
            dlnb_ref[...] = jnp.zeros_like(dlnb_ref)

        def chunk(rows):
            u, v, lng = u_ref[rows, :], v_ref[rows, :], lng_ref[...]
            c = _sg_core(u, v, lng, lnb_ref[...], ws_ref, bs_ref)
            dzc = dz_ref[rows, :]
            gs = lambda t, g: t[:, g * LANES:(g + 1) * LANES]
            dsv = dzc * c["gu"]
            dvp = []
            for g in range(SG_GROUPS):
                dsv_g = gs(dsv, g)
                dws_ref[g] += jnp.where(c["tril"], _dot(dsv_g, gs(c["vp"], g), NT), 0.0)
                dbs_ref[g] += jnp.sum(dsv_g, axis=1, keepdims=True)
                dvp.append(_dot(c["wm"][g], dsv_g, TN))
            dvp = jnp.concatenate(dvp, axis=1)
            xh = c["xh"]
            dlng_ref[...] += _row0(jnp.sum(dvp * xh, axis=0, keepdims=True))
            dlnb_ref[...] += _row0(jnp.sum(dvp, axis=0, keepdims=True))
            dxh = dvp * lng
            dgv = c["rs"] * (dxh - jnp.mean(dxh, axis=-1, keepdims=True) - xh * jnp.mean(dxh * xh, axis=-1, keepdims=True))
            dp_ref[rows, :] = (dzc * c["sv"] * _gelu_grad(u, c["thu"])).astype(BF16)
            stash[rows, :] = dgv * _gelu_grad(v, c["thv"])

        @pl.when(jj == 0)
        def _():
            for sub in range(step // SG_CHUNK):
                chunk(pl.ds(sub * SG_CHUNK, SG_CHUNK))

        @pl.when(jj == 1)
        def _():
            dp_ref[...] = stash[...].astype(BF16)

    full = lambda shape: pl.BlockSpec(shape, lambda c, jj: (0,) * len(shape))
    return pl.pallas_call(
        body, name="sg_bwd", grid=(T // step, 2),
        in_specs=[pl.BlockSpec((step, WIDTH), lambda c, jj: (c, 7)), pl.BlockSpec((step, WIDTH), lambda c, jj: (c, 8)),
                  full((1, WIDTH)), full((1, WIDTH)), full(ws.shape), full(bs.shape),
                  pl.BlockSpec((None, step, WIDTH), lambda c, jj: (2, c, 0)), ANY],
        out_specs=[pl.BlockSpec((step, WIDTH), lambda c, jj: (c, 7 + jj)), full(ws.shape), full(bs.shape),
                   full((SUBLANES, WIDTH)), full((SUBLANES, WIDTH))],
        out_shape=[jax.ShapeDtypeStruct(dp.shape, dp.dtype), jax.ShapeDtypeStruct(ws.shape, F32),
                   jax.ShapeDtypeStruct(bs.shape, F32), jax.ShapeDtypeStruct((SUBLANES, WIDTH), F32),
                   jax.ShapeDtypeStruct((SUBLANES, WIDTH), F32)],
        scratch_shapes=[pltpu.VMEM((step, WIDTH), F32)], input_output_aliases={7: 0}, compiler_params=_params(2),
    )(p, p, lng, lnb, ws, bs, dz, dp)


BRANCH_COLS = D_MODEL // N_CHIP
GATE_UNIT0 = GATE_COL0 // WIDTH
UNITS = D_MODEL // WIDTH


def _unit_specs(order):
    def spec(which):
        def index(*g):
            _, n, u = order(*g)
            return (2 * u + which, n, 0, 0)
        return pl.BlockSpec((None, None, WIDTH, BRANCH_COLS), index)
    return [spec(0), spec(1)]


def _merge_fwd(z, p, wb):
    T = z.shape[1]
    tm = min(2 * _token_tile(T), T)
    order = lambda i, u, n: (i, n, u)

    def body(z_ref, wa_ref, wb_ref, gt_ref, out_ref, acc):
        n = pl.program_id(2)
        zv = z_ref[...]
        y = jnp.concatenate([_dot(zv, wa_ref[...]), _dot(zv, wb_ref[...])], axis=1)
        part = jax.nn.sigmoid(gt_ref[...]) * y

        @pl.when(n == 0)
        def _():
            acc[...] = part

        @pl.when(n > 0)
        def _():
            acc[...] += part

        @pl.when(n == N_BRANCH - 1)
        def _():
            out_ref[...] = acc[...].astype(BF16)

    return pl.pallas_call(
        body, name="merge_fwd", grid=(T // tm, UNITS, N_BRANCH),
        in_specs=[pl.BlockSpec((None, tm, WIDTH), lambda i, u, n: (n, i, 0)), *_unit_specs(order),
                  pl.BlockSpec((tm, WIDTH), lambda i, u, n: (i, GATE_UNIT0 + UNITS * n + u))],
        out_specs=pl.BlockSpec((tm, WIDTH), lambda i, u, n: (i, u)),
        out_shape=jax.ShapeDtypeStruct((T, D_MODEL), BF16),
        scratch_shapes=[pltpu.VMEM((tm, WIDTH), F32)], compiler_params=_params(3))(z, wb, wb, p)


def _merge_bwd(z, p, wb, dmerged):
    T = z.shape[1]
    tm = min(2 * _token_tile(T), T)
    order = lambda n, u, i: (i, n, u)

    def body(z_ref, wa_ref, wb_ref, gt_ref, dm_ref, dp_ref, dw_ref, dz_ref):
        u, i = pl.program_id(1), pl.program_id(2)
        zv, wa, wbv = z_ref[...], wa_ref[...], wb_ref[...]
        y = jnp.concatenate([_dot(zv, wa), _dot(zv, wbv)], axis=1)
        gate = jax.nn.sigmoid(gt_ref[...])
        dm = dm_ref[...]
        dp_ref[...] = (dm * y * gate * (1.0 - gate)).astype(BF16)
        dyv = (dm * gate).astype(BF16)
        dw = _dot(zv, dyv, TN)
        part = _dot(dyv[:, :BRANCH_COLS], wa, NT) + _dot(dyv[:, BRANCH_COLS:], wbv, NT)
        rows = pl.ds(pl.multiple_of(i * tm, tm), tm)

        @pl.when(i == 0)
        def _():
            dw_ref[0] = dw[:, :BRANCH_COLS]
            dw_ref[1] = dw[:, BRANCH_COLS:]

        @pl.when(i > 0)
        def _():
            dw_ref[0] += dw[:, :BRANCH_COLS]
            dw_ref[1] += dw[:, BRANCH_COLS:]

        @pl.when(u == 0)
        def _():
            dz_ref[rows, :] = part

        @pl.when(u > 0)
        def _():
            dz_ref[rows, :] += part

    unit = lambda n, u, i: (i, GATE_UNIT0 + UNITS * n + u)
    return pl.pallas_call(
        body, name="merge_bwd", grid=(N_BRANCH, UNITS, T // tm),
        in_specs=[pl.BlockSpec((None, tm, WIDTH), lambda n, u, i: (n, i, 0)), *_unit_specs(order),
                  pl.BlockSpec((tm, WIDTH), unit), pl.BlockSpec((tm, WIDTH), lambda n, u, i: (i, u))],
        out_specs=[pl.BlockSpec((tm, WIDTH), unit),
                   pl.BlockSpec((2, None, WIDTH, BRANCH_COLS), lambda n, u, i: (u, n, 0, 0)),
                   pl.BlockSpec((None, T, WIDTH), lambda n, u, i: (n, 0, 0))],
        out_shape=[jax.ShapeDtypeStruct((T, IN_COLS), BF16),
                   jax.ShapeDtypeStruct((N_CHIP, N_BRANCH, WIDTH, BRANCH_COLS), F32),
                   jax.ShapeDtypeStruct((N_BRANCH, T, WIDTH), F32)],
        compiler_params=_params(3))(z, wb, wb, p, dmerged)


def _layer_fwd(x, h, win, small, next_gain, hooks):
    p = _mm_cols("in_proj", h, win, [F32])[0]
    o_hgrn, z, states = _hgrn_fwd(p, small["lbs"], small["g_hgrn_out"])
    z = _conv_fwd(p, small["w_conv"], z, after=hooks["after_hgrn"]([o_hgrn]))
    z = _sg_fwd(p, small["sg_ln_g"], small["sg_ln_b"], small["w_sg"], small["b_sg"], z)
    wb, wo, w1, w2 = hooks["late_weights"]([z])
    wb = wb.reshape(N_CHIP, N_BRANCH, WIDTH, BRANCH_COLS)
    merged = _merge_fwd(z, p, wb)
    x_mid, h2 = _mm_rows("out_proj", merged, wo, x, small["g_ffn"])
    s = _mm_cols("ff1", h2, w1, [BF16], epilogue=lambda acc: (jnp.square(jnp.maximum(acc, 0.0)),))[0]
    if next_gain is None:
        x_out, h_next = _mm_rows("ff2_last", s, w2, x_mid, after=hooks["before_last"]([s])), None
    else:
        x_out, h_next = _mm_rows("ff2", s, w2, x_mid, next_gain, after=hooks["before_last"]([s]))
    saved = dict(x=x, h=h, p=p, o_hgrn=o_hgrn, z=z, states=states, merged=merged, x_mid=x_mid, h2=h2, s=s)
    return x_out, h_next, saved, [win, wb, wo, w1, w2]


def _layer_bwd(dx_out, sv, wts, small, tick, after):
    win, wb, wo, w1, w2 = wts
    g = {}
    da = _mm_cols_t("ff2_dgrad", dx_out, w2, BF16, extra=(sv["s"],), after=after,
                    epilogue=lambda acc, s: (acc * 2.0 * jnp.sqrt(s.astype(F32)),))
    d_ff2 = _mm_wgrad("ff2_wgrad", sv["s"], dx_out, w2.shape[1], D_MODEL, True, False)
    d_ff1 = _mm_wgrad("ff1_wgrad", sv["h2"], da, D_MODEL, w1.shape[2], False, True)
    dx_mid, g["g_ffn"] = _dgrad_norm_bwd("ff1_dgrad", da, w1, sv["x_mid"], small["g_ffn"], dx_out)
    after = tick([dx_mid], [("w_ff1", d_ff1), ("w_ff2", d_ff2)])
    dmerged = _mm_cols_t("out_proj_dgrad", dx_mid, wo, F32, after=after)
    d_o = _mm_wgrad("out_proj_wgrad", sv["merged"], dx_mid, wo.shape[1], D_MODEL, True, False)
    dp, d_branch, dz = _merge_bwd(sv["z"], sv["p"], wb, dmerged)
    d_branch = d_branch.reshape(N_CHIP, N_BRANCH * WIDTH, BRANCH_COLS)
    dp, g["w_conv"] = _conv_bwd(sv["p"], small["w_conv"], dz, dp)
    dp, g["w_sg"], g["b_sg"], g["sg_ln_g"], g["sg_ln_b"] = _sg_bwd(
        sv["p"], small["sg_ln_g"], small["sg_ln_b"], small["w_sg"], small["b_sg"], dz, dp)
    after = tick([dp], [("w_branch", d_branch), ("w_o", d_o)])
    dp, g["lbs"], g["g_hgrn_out"] = _hgrn_bwd(sv["p"], sv["o_hgrn"], dz, sv["states"], small["lbs"],
                                              small["g_hgrn_out"], dp, after=after)
    after = tick([dp], [])
    d_in = _mm_wgrad("in_proj_wgrad", sv["h"], dp, D_MODEL, win.shape[2], False, True, after=after)
    dx, g["g_mix"] = _dgrad_norm_bwd("in_proj_dgrad", dp, win, sv["x"], small["g_mix"], dx_mid)
    return dx, g, tick([dx], [("w_in", d_in)])


def _mesh_pos():
    return lax.axis_index("x"), lax.axis_index("y"), lax.axis_index("c")


def _other_chips(x, y):
    return [(1 - x, y), (x, 1 - y), (1 - x, 1 - y)]


def _remote(src, dst, send_sems, recv_sems, k, to):
    return pltpu.make_async_remote_copy(src_ref=src, dst_ref=dst, send_sem=send_sems.at[k], recv_sem=recv_sems.at[k],
                                        device_id=to, device_id_type=MESH)


def _gather_call(name, body, buf, after):
    scratch = [pltpu.SemaphoreType.DMA((7,)), pltpu.SemaphoreType.DMA((7,))]
    return pl.pallas_call(
        body, name=name, in_specs=[ANY] * (1 + len(after)), out_specs=ANY,
        out_shape=jax.ShapeDtypeStruct(buf.shape, buf.dtype), scratch_shapes=scratch, input_output_aliases={0: 0})(buf, *after)


HBM = pl.BlockSpec(memory_space=pltpu.HBM)
SEM = pl.BlockSpec(memory_space=pltpu.SEMAPHORE)
DATAFLOW = pltpu.SideEffectType.DATAFLOW_SIDE_EFFECTING


def _split_start(name, bufs, copies, n_copies, after=()):
    n = len(bufs)

    def body(*refs):
        send_sems, recv_sems = refs[n + len(after)], refs[n + len(after) + 1]
        for cp in copies(refs[:n], send_sems, recv_sems):
            cp.start()
        refs[-1][...] = jnp.zeros_like(refs[-1])

    outs = pl.pallas_call(
        body, name=name,
        out_shape=(pltpu.SemaphoreType.DMA((n_copies,)), pltpu.SemaphoreType.DMA((n_copies,)),
                   *[pltpu.HBM(b.shape, b.dtype) for b in bufs], jax.ShapeDtypeStruct((SUBLANES, LANES), F32)),
        in_specs=[HBM] * n + [ANY] * len(after),
        out_specs=(SEM, SEM, *[HBM] * n, pl.BlockSpec(memory_space=pltpu.VMEM)),
        input_output_aliases={t: 2 + t for t in range(n)},
        compiler_params=pltpu.CompilerParams(has_side_effects=DATAFLOW),
    )(*[pltpu.with_memory_space_constraint(b, pltpu.HBM) for b in bufs], *after)
    return outs[0], outs[1], list(outs[2:2 + n]), outs[-1]


def _split_wait(name, started, copies, after):
    send_sems, recv_sems, bufs, _ = started
    n = len(bufs)

    def body(*refs):
        for cp in copies(refs[:n], refs[n], refs[n + 1]):
            cp.wait_send()
            cp.wait_recv()

    return list(pl.pallas_call(
        body, name=name, out_shape=tuple(pltpu.HBM(b.shape, b.dtype) for b in bufs),
        in_specs=[HBM] * n + [SEM, SEM] + [ANY] * len(after), out_specs=tuple([HBM] * n),
        input_output_aliases={t: t for t in range(n)},
        compiler_params=pltpu.CompilerParams(has_side_effects=DATAFLOW),
    )(*bufs, send_sems, recv_sems, *after))


def _weight_ici_copies(refs, send_sems, recv_sems):
    x, y, c = _mesh_pos()
    out = []
    for t, ref in enumerate(refs):
        rh = ref.shape[1] // 2
        mine = ref.at[2 * x + y, pl.ds(c * rh, rh), :]
        out += [_remote(mine, mine, send_sems, recv_sems, 3 * t + j, (*chip, c)) for j, chip in enumerate(_other_chips(x, y))]
    return out


def _weight_d2d_copies(refs, send_sems, recv_sems):
    x, y, c = _mesh_pos()
    out = []
    for t, ref in enumerate(refs):
        rh = ref.shape[1] // 2
        for j, chip in enumerate(_other_chips(x, y)):
            blk = ref.at[2 * chip[0] + chip[1], pl.ds(c * rh, rh), :]
            out.append(_remote(blk, blk, send_sems, recv_sems, 3 * t + j, (x, y, 1 - c)))
    return out


def _swap_part(refs, send_sems, recv_sems, s0):
    x, y, c = _mesh_pos()
    n = len(refs) // 2
    out = []
    for t in range(n):
        rh = refs[t].shape[1] // 2
        out.append(_remote(refs[t].at[:, pl.ds((1 - c) * rh, rh), :], refs[n + t], send_sems, recv_sems, s0 + t, (x, y, 1 - c)))
    return out


def _exchange_part(refs, send_sems, recv_sems, s0):
    x, y, c = _mesh_pos()
    n = len(refs) // 2
    out = []
    for t in range(n):
        for j, chip in enumerate(_other_chips(x, y)):
            out.append(_remote(refs[t].at[2 * chip[0] + chip[1]], refs[n + t].at[j], send_sems, recv_sems, s0 + 3 * t + j,
                               (*chip, c)))
    return out


def _gather_part(refs, send_sems, recv_sems, s0):
    x, y, c = _mesh_pos()
    return [_remote(ref.at[c], ref.at[c], send_sems, recv_sems, s0 + t, (x, y, 1 - c)) for t, ref in enumerate(refs)]


def _all_to_all_copies(refs, send_sems, recv_sems):
    x, y, c = _mesh_pos()
    blk = refs[0].at[4 * x + 2 * y + c]
    peers = [(x, y, 1 - c)] + [(*chip, cc) for chip in _other_chips(x, y) for cc in (c, 1 - c)]
    return [_remote(blk, blk, send_sems, recv_sems, k, peer) for k, peer in enumerate(peers)]


class _GradPipeline:
    def __init__(self, pos):
        self.pos = pos
        self.groups, self.pending, self.count = [], None, 0
        self.reduced = {n: [None] * DEPTH for n in BIG_NAMES}

    def busy(self):
        return bool(self.groups) or self.pending is not None

    def tick(self, deps, new):
        if self.pending is not None:
            started, copies, owners = self.pending
            bufs = _split_wait("grad_pipe_wait_%d" % self.count, started, copies, after=list(deps))
            for grp, lo, hi in owners:
                grp["bufs"] = bufs[lo:hi]
            self.pending = None
        parts = []
        for grp in list(self.groups):
            n, names = len(grp["names"]), grp["names"]
            if grp["stage"] == "swap":
                pair = [_pair_sum("grad_pair_sum_" + nm, f, r, self.pos)
                        for nm, f, r in zip(names, grp["bufs"][:n], grp["bufs"][n:])]
                grp["own32"] = [p32 for p32, _ in pair]
                landing = [lax.empty((3, *p16.shape[1:]), BF16) for _, p16 in pair]
                grp["stage"] = "exchange"
                parts.append((grp, [p16 for _, p16 in pair] + landing, _exchange_part, 3 * n))
            elif grp["stage"] == "exchange":
                halves = [_chip_sum("grad_chip_sum_" + nm, p32, r, self.pos)
                          for nm, p32, r in zip(names, grp["own32"], grp["bufs"][n:])]
                grp["stage"] = "gather"
                parts.append((grp, halves, _gather_part, n))
            else:
                for nm, b in zip(names, grp["bufs"]):
                    self.reduced[nm][grp["layer"]] = b.reshape(-1, b.shape[-1])
                self.groups.remove(grp)
        if new:
            grp = dict(names=[nm for nm, _, _ in new], layer=new[0][1], stage="swap")
            self.groups.append(grp)
            fulls = [g for _, _, g in new]
            landing = [lax.empty((N_CHIP, g.shape[1] // 2, g.shape[2]), F32) for g in fulls]
            parts.append((grp, fulls + landing, _swap_part, len(fulls)))
        if not parts:
            return ()
        bufs, layout, owners, sems = [], [], [], 0
        for grp, part_bufs, fn, n_sems in parts:
            layout.append((len(bufs), len(bufs) + len(part_bufs), fn, sems))
            owners.append((grp, len(bufs), len(bufs) + len(part_bufs)))
            bufs += part_bufs
            sems += n_sems

        def copies(refs, send_sems, recv_sems):
            out = []
            for lo, hi, fn, s0 in layout:
                out += fn(refs[lo:hi], send_sems, recv_sems, s0)
            return out

        started = _split_start("grad_pipe_start_%d" % self.count, bufs, copies, sems)
        self.pending = (started, copies, owners)
        self.count += 1
        return (started[3],)


def _gather_all(name, block, slot, after=()):
    buf = lax.dynamic_update_slice(jnp.zeros((8, *block.shape), block.dtype), block[None], (slot, 0, 0))

    def body(*refs):
        out_ref, send_sems, recv_sems = refs[1 + len(after):]
        x, y, c = _mesh_pos()
        chips = _other_chips(x, y)
        sibling = (x, y, 1 - c)
        slot_of = lambda px, py, pc: out_ref.at[4 * px + 2 * py + pc]
        started = [_remote(slot_of(x, y, c), slot_of(x, y, c), send_sems, recv_sems, 0, sibling)]
        started += [_remote(slot_of(x, y, c), slot_of(x, y, c), send_sems, recv_sems, 1 + j, (*chip, c))
                    for j, chip in enumerate(chips)]
        for cp in started:
            cp.start()
        for j, chip in enumerate(chips):
            _remote(slot_of(*chip, c), slot_of(*chip, c), send_sems, recv_sems, 1 + j, (*chip, c)).wait_recv()
            fw = _remote(slot_of(*chip, c), slot_of(*chip, c), send_sems, recv_sems, 4 + j, sibling)
            fw.start()
            started.append(fw)
        _remote(slot_of(x, y, 1 - c), slot_of(x, y, 1 - c), send_sems, recv_sems, 0, sibling).wait_recv()
        for j, chip in enumerate(chips):
            _remote(slot_of(*chip, 1 - c), slot_of(*chip, 1 - c), send_sems, recv_sems, 4 + j, sibling).wait_recv()
        for cp in started:
            cp.wait_send()

    return _gather_call(name, body, buf, after)


def _row_tile(rows, cols, block_bytes=ELEMWISE_BLOCK_BYTES):
    cap = max(SUBLANES, block_bytes // (4 * cols))
    tr = rows
    while tr > cap and tr % 2 == 0:
        tr //= 2
    return tr


def _pair_sum(name, grad, recv, pos):
    _, rh, cols = recv.shape
    tr = _row_tile(rh, cols, 2 * ELEMWISE_BLOCK_BYTES)
    per = rh // tr

    def body(pos_ref, g_ref, r_ref, own_ref, out16_ref):
        s = g_ref[...] + r_ref[...]
        out16_ref[...] = s.astype(BF16)

        @pl.when(pl.program_id(1) == pos_ref[0])
        def _():
            own_ref[...] = s

    blk = pl.BlockSpec((None, tr, cols), lambda i, k, pos_ref: (k, i, 0))
    return pl.pallas_call(
        body, name=name,
        grid_spec=pltpu.PrefetchScalarGridSpec(
            num_scalar_prefetch=1, grid=(per, N_CHIP),
            in_specs=[pl.BlockSpec((None, tr, cols), lambda i, k, pos_ref: (k, pos_ref[1] * per + i, 0)), blk],
            out_specs=[pl.BlockSpec((tr, cols), lambda i, k, pos_ref: (i, 0)), blk]),
        out_shape=[jax.ShapeDtypeStruct((rh, cols), F32), jax.ShapeDtypeStruct(recv.shape, BF16)],
        compiler_params=_params(2))(pos, grad, recv)


def _chip_sum(name, own32, recv, pos):
    rh, cols = own32.shape
    tr = _row_tile(rh, cols, 2 * ELEMWISE_BLOCK_BYTES)

    def body(pos_ref, own_ref, r_ref, out_ref):
        del pos_ref
        out_ref[...] = ((own_ref[...] + r_ref[0].astype(F32)) + r_ref[1].astype(F32)) + r_ref[2].astype(F32)

    return pl.pallas_call(
        body, name=name,
        grid_spec=pltpu.PrefetchScalarGridSpec(
            num_scalar_prefetch=1, grid=(rh // tr,),
            in_specs=[pl.BlockSpec((tr, cols), lambda i, pos_ref: (i, 0)),
                      pl.BlockSpec((3, tr, cols), lambda i, pos_ref: (0, i, 0))],
            out_specs=pl.BlockSpec((None, tr, cols), lambda i, pos_ref: (pos_ref[1], i, 0))),
        out_shape=jax.ShapeDtypeStruct((2, rh, cols), F32), compiler_params=_params(1))(pos, own32, recv)


def _cast_into_slot(name, w, layer, pos, after=()):
    _, rows, cols = w.shape
    tr = _row_tile(rows, cols, 2 * ELEMWISE_BLOCK_BYTES)

    def body(pos_ref, w_ref, *rest):
        del pos_ref
        rest[-1][...] = w_ref[...].astype(BF16)

    return pl.pallas_call(
        body, name=name,
        grid_spec=pltpu.PrefetchScalarGridSpec(
            num_scalar_prefetch=1, grid=(rows // tr,),
            in_specs=[pl.BlockSpec((None, tr, cols), lambda i, pos_ref: (layer, i, 0))] + [ANY] * len(after),
            out_specs=pl.BlockSpec((None, tr, cols), lambda i, pos_ref: (pos_ref[0], i, 0))),
        out_shape=jax.ShapeDtypeStruct((N_CHIP, rows, cols), BF16), compiler_params=_params(1))(pos, w, *after)


def _adamw_math(w, g, m, v):
    m = ADAM_B1 * m + (1.0 - ADAM_B1) * g
    v = ADAM_B2 * v + (1.0 - ADAM_B2) * jnp.square(g)
    m_hat = m / (1.0 - ADAM_B1 ** ADAM_STEP)
    v_hat = v / (1.0 - ADAM_B2 ** ADAM_STEP)
    delta = -ADAM_LR * (m_hat / (jnp.sqrt(v_hat) + ADAM_EPS) + ADAM_WD * w)
    return delta, m, v


def _adamw_layers(name, w, m, v, grads, first, into=None, after=()):
    _, rows, cols = w.shape
    tr = _row_tile(rows, cols)
    n_layers = len(grads)

    def body(w_ref, m_ref, v_ref, *rest):
        g_refs, (grad_ref, d_ref, nm_ref, nv_ref) = rest[:n_layers], rest[len(rest) - 4:]
        layer = pl.program_id(0)
        g = g_refs[0][...]
        for l in range(1, n_layers):
            g = jnp.where(layer == l, g_refs[l][...], g)
        grad_ref[...] = g
        d_ref[...], nm_ref[...], nv_ref[...] = _adamw_math(w_ref[...], g, m_ref[...], v_ref[...])

    blk = pl.BlockSpec((None, tr, cols), lambda l, i: (first + l, i, 0))
    g_spec = lambda k: pl.BlockSpec((tr, cols), lambda l, i: (jnp.where(l == k, i, 0), 0))
    passed = list(into or []) + list(after)
    return pl.pallas_call(
        body, name=name, grid=(n_layers, rows // tr),
        in_specs=[blk, blk, blk] + [g_spec(k) for k in range(n_layers)] + [ANY] * len(passed), out_specs=[blk] * 4,
        out_shape=[jax.ShapeDtypeStruct(w.shape, F32)] * 4,
        input_output_aliases={3 + n_layers + t: t for t in range(4)} if into else {},
        compiler_params=_params(2))(w, m, v, *grads, *passed)


def _sum_devices(gathered):
    _, rows, cols = gathered.shape

    def body(g_ref, out_ref):
        s = g_ref[0]
        for d in range(1, 8):
            s = s + g_ref[d]
        out_ref[...] = s

    return pl.pallas_call(body, name="sum_devices", out_shape=jax.ShapeDtypeStruct((rows, cols), F32),
                          compiler_params=pltpu.CompilerParams(vmem_limit_bytes=VMEM_LIMIT_BYTES))(gathered)


def _adamw_flat(w, g, m, v):
    def body(w_ref, g_ref, m_ref, v_ref, d_ref, nm_ref, nv_ref):
        d_ref[...], nm_ref[...], nv_ref[...] = _adamw_math(w_ref[...], g_ref[...], m_ref[...], v_ref[...])

    return pl.pallas_call(body, name="adamw_small", out_shape=[jax.ShapeDtypeStruct(w.shape, F32)] * 3,
                          compiler_params=pltpu.CompilerParams(vmem_limit_bytes=VMEM_LIMIT_BYTES))(w, g, m, v)


SMALL_NAMES = ["g_mix", "lower_bounds", "g_hgrn_out", "w_conv", "sg_ln_g", "sg_ln_b", "w_sg", "b_sg", "g_ffn", "g_final"]
BIG_NAMES = ["w_in", "w_branch", "w_o", "w_ff1", "w_ff2"]
WEIGHT_ORDER = ["w_in", "g_mix", "lower_bounds", "g_hgrn_out", "w_conv", "sg_ln_g", "sg_ln_b", "w_sg", "b_sg", "w_branch",
                "w_o", "g_ffn", "w_ff1", "w_ff2", "g_final"]


def _padded_rows(n):
    return -(-n // SUBLANES) * SUBLANES


def _pack(arrays):
    parts = []
    for a in arrays:
        a = a.reshape(-1, LANES)
        parts.append(jnp.pad(a, ((0, _padded_rows(a.shape[0]) - a.shape[0]), (0, 0))))
    return jnp.concatenate(parts, axis=0)


def _unpack(flat, shapes):
    out, row = [], 0
    for s in shapes:
        n = int(np.prod(s)) // LANES
        out.append(flat[row:row + n].reshape(s))
        row += _padded_rows(n)
    return out


def _as_2d(name, a):
    return a.reshape(DEPTH, N_BRANCH * WIDTH, BRANCH_COLS) if name == "w_branch" else a


def kernel(x, w_in, g_mix, lower_bounds, g_hgrn_out, w_conv, sg_ln_g, sg_ln_b, w_sg, b_sg, w_branch, w_o, g_ffn, w_ff1, w_ff2, g_final, loss_target, m_w_in, m_g_mix, m_lower_bounds, m_g_hgrn_out, m_w_conv, m_sg_ln_g, m_sg_ln_b, m_w_sg, m_b_sg, m_w_branch, m_w_o, m_g_ffn, m_w_ff1, m_w_ff2, m_g_final, v_w_in, v_g_mix, v_lower_bounds, v_g_hgrn_out, v_w_conv, v_sg_ln_g, v_sg_ln_b, v_w_sg, v_b_sg, v_w_branch, v_w_o, v_g_ffn, v_w_ff1, v_w_ff2, v_g_final):
    weights = dict(w_in=w_in, g_mix=g_mix, lower_bounds=lower_bounds, g_hgrn_out=g_hgrn_out, w_conv=w_conv,
                   sg_ln_g=sg_ln_g, sg_ln_b=sg_ln_b, w_sg=w_sg, b_sg=b_sg, w_branch=w_branch, w_o=w_o, g_ffn=g_ffn,
                   w_ff1=w_ff1, w_ff2=w_ff2, g_final=g_final)
    mom1 = dict(w_in=m_w_in, g_mix=m_g_mix, lower_bounds=m_lower_bounds, g_hgrn_out=m_g_hgrn_out, w_conv=m_w_conv,
                sg_ln_g=m_sg_ln_g, sg_ln_b=m_sg_ln_b, w_sg=m_w_sg, b_sg=m_b_sg, w_branch=m_w_branch, w_o=m_w_o,
                g_ffn=m_g_ffn, w_ff1=m_w_ff1, w_ff2=m_w_ff2, g_final=m_g_final)
    mom2 = dict(w_in=v_w_in, g_mix=v_g_mix, lower_bounds=v_lower_bounds, g_hgrn_out=v_g_hgrn_out, w_conv=v_w_conv,
                sg_ln_g=v_sg_ln_g, sg_ln_b=v_sg_ln_b, w_sg=v_w_sg, b_sg=v_b_sg, w_branch=v_w_branch, w_o=v_w_o,
                g_ffn=v_g_ffn, w_ff1=v_w_ff1, w_ff2=v_w_ff2, g_final=v_g_final)
    xi, yi, ci = _mesh_pos()
    pos = jnp.stack([2 * xi + yi, ci]).astype(jnp.int32)
    device = 4 * xi + 2 * yi + ci
    conv_cols = w_conv.shape[2]

    conv_all = _gather_all("gather_w_conv", w_conv.reshape(DEPTH * CONV_K, conv_cols), device)
    conv_full = conv_all.reshape(N_CHIP, 2, DEPTH, CONV_K, conv_cols)[:, 0].transpose(1, 2, 0, 3).reshape(DEPTH, CONV_K, WIDTH)

    ici, d2d = {}, {}
    cast = lambda l, names, after: [_cast_into_slot("cast_" + n, _as_2d(n, weights[n]), l, pos, after=after) for n in names]
    token = (conv_all,)
    for l in range(DEPTH):
        for part, names in (("w_in", BIG_NAMES[:1]), ("rest", BIG_NAMES[1:])):
            ici[l, part] = _split_start("weights_ici_start_%d_%s" % (l, part), cast(l, names, token), _weight_ici_copies,
                                        3 * len(names), after=token)
            token = (ici[l, part][3],)
    lbs = _lbs_fwd(lower_bounds)

    def forward_to_sibling(l, part, deps):
        landed = _split_wait("weights_ici_wait_%d_%s" % (l, part), ici.pop((l, part)), _weight_ici_copies, after=deps)
        d2d[l, part] = _split_start("weights_d2d_start_%d_%s" % (l, part), landed, _weight_d2d_copies, 3 * len(landed))
        return (d2d[l, part][3],)

    def gathered(l, part, deps):
        return _split_wait("weights_d2d_wait_%d_%s" % (l, part), d2d.pop((l, part)), _weight_d2d_copies, after=deps)

    act = x[0]
    normed = _rms_fwd("rms_mix", act, g_mix[0:1], after=token)
    layers = []
    forward_to_sibling(0, "w_in", [normed, lbs])
    for l in range(DEPTH):
        small = dict(g_mix=g_mix[l:l + 1], lbs=lbs[l:l + 1], g_hgrn_out=g_hgrn_out[l:l + 1], w_conv=conv_full[l],
                     sg_ln_g=sg_ln_g[l:l + 1], sg_ln_b=sg_ln_b[l:l + 1], w_sg=w_sg[l],
                     b_sg=b_sg[l].reshape(SG_GROUPS, SG_CHUNK, 1), g_ffn=g_ffn[l:l + 1])
        hooks = dict(after_hgrn=lambda deps, l=l: forward_to_sibling(l, "rest", deps),
                     late_weights=lambda deps, l=l: gathered(l, "rest", deps),
                     before_last=(lambda deps, l=l: forward_to_sibling(l + 1, "w_in", deps)) if l + 1 < DEPTH
                     else (lambda deps: ()))
        act, normed, saved, wts = _layer_fwd(act, normed, gathered(l, "w_in", [act])[0], small,
                                             g_mix[l + 1:l + 2] if l + 1 < DEPTH else None, hooks)
        layers.append((wts, small, saved))
    loss_blk, dact, dg_final = _loss_head(act, g_final.reshape(1, D_MODEL), loss_target[0])

    pipe = _GradPipeline(pos)
    small_grads = [None] * DEPTH
    after = ()
    for l in reversed(range(DEPTH)):
        wts, small, saved = layers[l]
        tick = lambda deps, new, l=l: pipe.tick(deps, [(nm, l, g) for nm, g in new])
        dact, small_grads[l], after = _layer_bwd(dact, saved, wts, small, tick, after)
    grad_x = dact[None]

    stack = lambda key, rows=None: jnp.stack([small_grads[l][key][0] if rows is None else small_grads[l][key][:rows]
                                              for l in range(DEPTH)])
    local_small = dict(
        g_mix=stack("g_mix"), lower_bounds=stack("lbs"), g_hgrn_out=stack("g_hgrn_out"), w_conv=stack("w_conv", CONV_K),
        sg_ln_g=stack("sg_ln_g"), sg_ln_b=stack("sg_ln_b"), w_sg=jnp.stack([small_grads[l]["w_sg"] for l in range(DEPTH)]),
        b_sg=jnp.stack([small_grads[l]["b_sg"].reshape(SG_GROUPS, SG_CHUNK) for l in range(DEPTH)]),
        g_ffn=stack("g_ffn"), g_final=dg_final[0])
    shapes = [local_small[n].shape for n in SMALL_NAMES] + [(SUBLANES, LANES)]
    packed = _pack([local_small[n] for n in SMALL_NAMES] + [loss_blk])
    packed = lax.dynamic_update_slice(jnp.zeros((8, *packed.shape), F32), packed[None], (device, 0, 0))
    small_flight = _split_start("small_grads_start", [packed], _all_to_all_copies, 7, after=after)

    def adam(n, first, layer_grads, into=None, after=()):
        return _adamw_layers("adamw_%s_%d" % (n, first), _as_2d(n, weights[n]), _as_2d(n, mom1[n]), _as_2d(n, mom2[n]),
                             layer_grads, first, into, after)

    done = {"w_ff1": adam("w_ff1", 0, pipe.reduced["w_ff1"], after=(small_flight[3],))}
    token = pipe.tick([done["w_ff1"][1]], [])
    done["w_ff2"] = adam("w_ff2", 0, pipe.reduced["w_ff2"], after=token)
    summed = _sum_devices(_split_wait("small_grads_wait", small_flight, _all_to_all_copies, after=[done["w_ff2"][1]])[0])
    parts = _unpack(summed, shapes)
    loss = parts[-1][0, 0]
    small_grad = dict(zip(SMALL_NAMES, parts[:-1]))
    small_grad["lower_bounds"] = _lbs_bwd(lower_bounds, small_grad["lower_bounds"])
    small_grad["w_conv"] = lax.dynamic_slice_in_dim(small_grad["w_conv"], pos[0] * conv_cols, conv_cols, axis=2)
    g_flat = _pack([small_grad[n] for n in SMALL_NAMES])
    d_flat, m_flat, v_flat = _adamw_flat(_pack([weights[n] for n in SMALL_NAMES]), g_flat,
                                         _pack([mom1[n] for n in SMALL_NAMES]), _pack([mom2[n] for n in SMALL_NAMES]))
    small_shapes = [weights[n].shape for n in SMALL_NAMES]
    grads = dict(small_grad)
    delta = dict(zip(SMALL_NAMES, _unpack(d_flat, small_shapes)))
    new_m = dict(zip(SMALL_NAMES, _unpack(m_flat, small_shapes)))
    new_v = dict(zip(SMALL_NAMES, _unpack(v_flat, small_shapes)))

    for n in ("w_o", "w_branch"):
        done[n] = adam(n, 0, pipe.reduced[n], after=(d_flat,))
    token = pipe.tick([done["w_branch"][1]], [])
    rest = adam("w_in", 1, pipe.reduced["w_in"][1:], after=token)
    pipe.tick([rest[1]], [])
    assert not pipe.busy()
    done["w_in"] = adam("w_in", 0, pipe.reduced["w_in"][:1], into=rest)
    for n in BIG_NAMES:
        grads[n], delta[n], new_m[n], new_v[n] = [o.reshape(weights[n].shape) for o in done[n]]

    return (loss, grad_x, *[grads[n] for n in WEIGHT_ORDER], *[delta[n] for n in WEIGHT_ORDER],
            *[new_m[n] for n in WEIGHT_ORDER], *[new_v[n] for n in WEIGHT_ORDER])
```

```python
import numpy as np
import jax
import jax.numpy as jnp
from jax import lax
from jax.experimental import pallas as pl
from jax.experimental.pallas import tpu as pltpu

F32, BF16 = jnp.float32, jnp.bfloat16

D_MODEL = 1024
WIDTH = 512
N_BRANCH = 3
N_HEAD = 4
HEAD = 128
H_CHUNK = 64
CONV_K = 3
SG_CHUNK = 128
SG_GROUPS = 4
D_FF = 4096
DEPTH = 4
N_CHIP = 4
IN_COLS = 9 * WIDTH + N_BRANCH * D_MODEL
GATE_COL0 = 9 * WIDTH
LB_FLOOR = 1e-30
NORM_EPS = 1e-6
LN_EPS = 1e-5
ADAM_LR, ADAM_B1, ADAM_B2, ADAM_EPS, ADAM_WD, ADAM_STEP = 0.001, 0.9, 0.999, 1e-08, 0.01, 10

VMEM_LIMIT_BYTES = 56 * 1024 * 1024
VMEM_BLOCK_BUDGET = 44 * 1024 * 1024
SUBLANES, LANES = 8, 128
ELEMWISE_BLOCK_BYTES = 2 * 1024 * 1024

NN = (((1,), (0,)), ((), ()))
NT = (((1,), (1,)), ((), ()))
TN = (((0,), (0,)), ((), ()))
MESH = pl.DeviceIdType.MESH
ANY = pl.BlockSpec(memory_space=pl.ANY)


def _dot(a, b, dims=NN):
    return lax.dot_general(a.astype(BF16), b.astype(BF16), dims, preferred_element_type=F32)


def _params(n_axes):
    return pltpu.CompilerParams(dimension_semantics=("arbitrary",) * n_axes, vmem_limit_bytes=VMEM_LIMIT_BYTES)


def _row0(part, rows=SUBLANES):
    r = lax.broadcasted_iota(jnp.int32, (rows, part.shape[1]), 0)
    return jnp.where(r == 0, part, 0.0)


def _token_tile(T):
    return min(512, T)


def _matmul(name, a, b, *, dims, grid, a_spec, b_spec, out_specs, out_shapes, acc_shape,
            extra=(), extra_specs=(), epilogue=None, after=()):
    nk = grid[2]
    n_extra, n_out, n_in = len(extra), len(out_shapes), 2 + len(extra) + len(after)
    one_step = nk == 1

    def body(*refs):
        a_ref, b_ref = refs[0], refs[1]
        ex = refs[2:2 + n_extra]
        outs = refs[n_in:n_in + n_out]
        part = _dot(a_ref[...], b_ref[...], dims)

        def finish(total):
            res = epilogue(total, *[e[...] for e in ex]) if epilogue else (total,)
            for o, r in zip(outs, res):
                o[...] = r.astype(o.dtype)

        if one_step:
            finish(part)
            return
        acc = refs[-1]
        kk = pl.program_id(2)

        @pl.when(kk == 0)
        def _():
            acc[...] = part

        @pl.when(kk > 0)
        def _():
            acc[...] += part

        @pl.when(kk == nk - 1)
        def _():
            finish(acc[...])

    return pl.pallas_call(
        body, name=name, grid=grid,
        in_specs=[a_spec, b_spec, *extra_specs, *[ANY] * len(after)], out_specs=list(out_specs),
        out_shape=list(out_shapes), scratch_shapes=[] if one_step else [pltpu.VMEM(acc_shape, F32)],
        compiler_params=_params(3),
    )(a, b, *extra, *after)


def _mm_cols(name, a, w, out_dtypes, epilogue=None, extra=()):
    T, K = a.shape
    N = w.shape[2]
    tm = min(2 * _token_tile(T), T)
    blk = pl.BlockSpec((tm, N), lambda j, i, kk: (i, j))
    return _matmul(
        name, a, w, dims=NN, grid=(N_CHIP, T // tm, 1),
        a_spec=pl.BlockSpec((tm, K), lambda j, i, kk: (i, 0)),
        b_spec=pl.BlockSpec((None, K, N), lambda j, i, kk: (j, 0, 0)),
        out_specs=[blk] * len(out_dtypes),
        out_shapes=[jax.ShapeDtypeStruct((T, N_CHIP * N), dt) for dt in out_dtypes],
        acc_shape=(tm, N), extra=extra, extra_specs=[blk] * len(extra), epilogue=epilogue)


def _mm_rows(name, a, w, res, norm_gain=None, after=()):
    T = a.shape[0]
    K, N = N_CHIP * w.shape[1], w.shape[2]
    tm = _token_tile(T)
    blk = pl.BlockSpec((tm, N), lambda i, j, kk: (i, 0))

    def with_norm(acc, r, gain):
        xv = acc + r
        return xv, xv * lax.rsqrt(jnp.mean(xv * xv, axis=-1, keepdims=True) + NORM_EPS) * gain

    normed = norm_gain is not None
    outs = _matmul(
        name, a, w.reshape(K, N), dims=NN, grid=(T // tm, 1, 1),
        a_spec=pl.BlockSpec((tm, K), lambda i, j, kk: (i, 0)),
        b_spec=pl.BlockSpec((K, N), lambda i, j, kk: (0, 0)),
        out_specs=[blk] * (2 if normed else 1),
        out_shapes=[jax.ShapeDtypeStruct((T, N), F32)] + ([jax.ShapeDtypeStruct((T, N), BF16)] if normed else []),
        acc_shape=(tm, N), extra=(res, norm_gain) if normed else (res,),
        extra_specs=[blk] + ([pl.BlockSpec((1, N), lambda i, j, kk: (0, 0))] if normed else []),
        epilogue=with_norm if normed else (lambda acc, r: (acc + r,)), after=after)
    return outs if normed else outs[0]


def _mm_cols_t(name, g, w, out_dtype, epilogue=None, extra=(), after=()):
    T, N = g.shape
    K = N_CHIP * w.shape[1]
    tm = _token_tile(T) if K <= 2 * D_MODEL else _token_tile(T) // 2
    blk = pl.BlockSpec((tm, K), lambda i, j, kk: (i, 0))
    return _matmul(
        name, g, w.reshape(K, N), dims=NT, grid=(T // tm, 1, 1),
        a_spec=pl.BlockSpec((tm, N), lambda i, j, kk: (i, 0)),
        b_spec=pl.BlockSpec((K, N), lambda i, j, kk: (0, 0), pipeline_mode=pl.Buffered(1)),
        out_specs=[blk], out_shapes=[jax.ShapeDtypeStruct((T, K), out_dtype)], acc_shape=(tm, K),
        extra=extra, extra_specs=[blk] * len(extra), epilogue=epilogue, after=after)[0]


def _dgrad_norm_bwd(name, g, w, x, gain, dres):
    T = g.shape[0]
    K, N = w.shape[1], w.shape[2]
    tm = _token_tile(T)
    whole = w.size * w.dtype.itemsize <= VMEM_BLOCK_BUDGET // 2

    def norm_bwd(i, dhv, x_ref, gain_ref, dres_ref, dx_ref, dgain_ref):
        xv = x_ref[...]
        r = lax.rsqrt(jnp.mean(xv * xv, axis=-1, keepdims=True) + NORM_EPS)
        xn = xv * r
        dxn = dhv * gain_ref[...]
        dx_ref[...] = dres_ref[...] + r * (dxn - xn * jnp.mean(dxn * xn, axis=-1, keepdims=True))

        @pl.when(i == 0)
        def _():
            dgain_ref[...] = jnp.zeros_like(dgain_ref)

        dgain_ref[...] += _row0(jnp.sum(dhv * xn, axis=0, keepdims=True))

    def body_whole(g_ref, w_ref, *rest):
        dhv = _dot(g_ref[:, pl.ds(0, N)], w_ref[0], NT)
        for k in range(1, N_CHIP):
            dhv = dhv + _dot(g_ref[:, pl.ds(k * N, N)], w_ref[k], NT)
        norm_bwd(pl.program_id(0), dhv, *rest)

    def body_steps(g_ref, w_ref, x_ref, gain_ref, dres_ref, dx_ref, dgain_ref, acc):
        kk = pl.program_id(1)
        part = _dot(g_ref[...], w_ref[...], NT)

        @pl.when(kk == 0)
        def _():
            acc[...] = part

        @pl.when(kk > 0)
        def _():
            acc[...] += part

        @pl.when(kk == N_CHIP - 1)
        def _():
            norm_bwd(pl.program_id(0), acc[...], x_ref, gain_ref, dres_ref, dx_ref, dgain_ref)

    tile = pl.BlockSpec((tm, K), lambda i, *kk: (i, 0))
    row = lambda cols: pl.BlockSpec((SUBLANES if cols is None else 1, K), lambda i, *kk: (0, 0))
    if whole:
        g_spec = pl.BlockSpec((tm, N_CHIP * N), lambda i: (i, 0))
        w_spec = pl.BlockSpec(w.shape, lambda i: (0, 0, 0), pipeline_mode=pl.Buffered(1))
    else:
        g_spec = pl.BlockSpec((tm, N), lambda i, kk: (i, kk))
        w_spec = pl.BlockSpec((None, K, N), lambda i, kk: (kk, 0, 0))
    return pl.pallas_call(
        body_whole if whole else body_steps, name=name, grid=(T // tm,) if whole else (T // tm, N_CHIP),
        in_specs=[g_spec, w_spec, tile, row(1), tile], out_specs=[tile, row(None)],
        out_shape=[jax.ShapeDtypeStruct((T, K), F32), jax.ShapeDtypeStruct((SUBLANES, K), F32)],
        scratch_shapes=[] if whole else [pltpu.VMEM((tm, K), F32)],
        compiler_params=_params(1 if whole else 2))(g, w, x, gain, dres)


def _mm_wgrad(name, a, g, a_cols, g_cols, a_blocked, g_blocked, after=()):
    T = a.shape[0]
    tt = T
    while tt > LANES and 2 * 2 * tt * (a_cols + g_cols) + (2 if tt == T else 3) * 4 * a_cols * g_cols > VMEM_BLOCK_BUDGET:
        tt //= 2
    return _matmul(
        name, a, g, dims=TN, grid=(N_CHIP, 1, T // tt),
        a_spec=pl.BlockSpec((tt, a_cols), (lambda j, i, kk: (kk, j)) if a_blocked else (lambda j, i, kk: (kk, 0))),
        b_spec=pl.BlockSpec((tt, g_cols), (lambda j, i, kk: (kk, j)) if g_blocked else (lambda j, i, kk: (kk, 0))),
        out_specs=[pl.BlockSpec((None, a_cols, g_cols), lambda j, i, kk: (j, 0, 0))],
        out_shapes=[jax.ShapeDtypeStruct((N_CHIP, a_cols, g_cols), F32)], acc_shape=(a_cols, g_cols), after=after)[0]


def _rms_fwd(name, x, g, after=()):
    T, Dm = x.shape
    tm = min(256, T)

    def body(x_ref, g_ref, *rest):
        xv = x_ref[...]
        r = lax.rsqrt(jnp.mean(xv * xv, axis=-1, keepdims=True) + NORM_EPS)
        rest[-1][...] = (xv * r * g_ref[...]).astype(BF16)

    return pl.pallas_call(
        body, name=name, grid=(T // tm,),
        in_specs=[pl.BlockSpec((tm, Dm), lambda i: (i, 0)), pl.BlockSpec((1, Dm), lambda i: (0, 0))] + [ANY] * len(after),
        out_specs=pl.BlockSpec((tm, Dm), lambda i: (i, 0)),
        out_shape=jax.ShapeDtypeStruct((T, Dm), BF16), compiler_params=_params(1))(x, g, *after)


def _loss_head(x, g, tgt):
    T, Dm = x.shape
    tm = min(256, T)

    def body(x_ref, g_ref, t_ref, loss_ref, dx_ref, dg_ref):
        xv = x_ref[...]
        gv = g_ref[...]
        r = lax.rsqrt(jnp.mean(xv * xv, axis=-1, keepdims=True) + NORM_EPS)
        xn = xv * r
        err = xn * gv - t_ref[...]
        dy = err * (1.0 / Dm)
        dxn = dy * gv
        dx_ref[...] = r * (dxn - xn * jnp.mean(dxn * xn, axis=-1, keepdims=True))

        @pl.when(pl.program_id(0) == 0)
        def _():
            dg_ref[...] = jnp.zeros_like(dg_ref)
            loss_ref[...] = jnp.zeros_like(loss_ref)

        dg_ref[...] += _row0(jnp.sum(dy * xn, axis=0, keepdims=True))
        part = jnp.sum(jnp.sum(err * err, axis=-1, keepdims=True), axis=0, keepdims=True) * (0.5 / Dm)
        loss_ref[...] += jnp.broadcast_to(part, loss_ref.shape)

    tile = pl.BlockSpec((tm, Dm), lambda i: (i, 0))
    return pl.pallas_call(
        body, name="loss_head", grid=(T // tm,),
        in_specs=[tile, pl.BlockSpec((1, Dm), lambda i: (0, 0)), tile],
        out_specs=[pl.BlockSpec((SUBLANES, LANES), lambda i: (0, 0)), tile,
                   pl.BlockSpec((SUBLANES, Dm), lambda i: (0, 0))],
        out_shape=[jax.ShapeDtypeStruct((SUBLANES, LANES), F32), jax.ShapeDtypeStruct((T, Dm), F32),
                   jax.ShapeDtypeStruct((SUBLANES, Dm), F32)],
        compiler_params=_params(1))(x, g, tgt)


def _softmax_rows(lb_ref):
    rows = [lb_ref[pl.ds(i, 1), :] for i in range(DEPTH)]
    mx = rows[0]
    for r in rows[1:]:
        mx = jnp.maximum(mx, r)
    es = [jnp.exp(r - mx) for r in rows]
    tot = es[0]
    for e in es[1:]:
        tot = tot + e
    return [e / tot for e in es]


def _lbs_fwd(lower_bounds):
    def body(lb_ref, out_ref):
        sm = _softmax_rows(lb_ref)
        run = jnp.zeros_like(sm[0])
        out_ref[pl.ds(0, 1), :] = run
        for i in range(1, DEPTH):
            run = run + sm[i]
            out_ref[pl.ds(i, 1), :] = run

    return pl.pallas_call(body, name="lbs_fwd", out_shape=jax.ShapeDtypeStruct(lower_bounds.shape, F32))(lower_bounds)


def _lbs_bwd(lower_bounds, dlbs):
    def body(lb_ref, d_ref, out_ref):
        sm = _softmax_rows(lb_ref)
        dsm = [jnp.zeros_like(sm[0])]
        for i in range(1, DEPTH):
            acc = d_ref[pl.ds(i, 1), :]
            for l in range(i + 1, DEPTH):
                acc = acc + d_ref[pl.ds(l, 1), :]
            dsm.append(acc)
        inner = dsm[0] * sm[0]
        for i in range(1, DEPTH):
            inner = inner + dsm[i] * sm[i]
        for i in range(DEPTH):
            out_ref[pl.ds(i, 1), :] = sm[i] * (dsm[i] - inner)

    return pl.pallas_call(body, name="lbs_bwd", out_shape=jax.ShapeDtypeStruct(lower_bounds.shape, F32))(lower_bounds, dlbs)


N_LEVEL = 6


def _hgrn_consts():
    L = H_CHUNK
    t = np.arange(L)
    blocks = [(t[:, None] >= t[None, :]).astype(np.float32)]
    masks = []
    m = L // 2
    while m >= 1:
        blk, pos = t // (2 * m), t % (2 * m)
        start = blk * 2 * m
        mat = np.zeros((L, L), np.float32)
        for r in range(L):
            if pos[r] >= m:
                mat[r, start[r] + m:r + 1] = 1.0
            else:
                mat[r, r + 1:start[r] + m] = -1.0
        blocks.append(mat)
        masks.append(((blk[:, None] == blk[None, :]) & (pos[:, None] >= m) & (pos[None, :] < m)).astype(np.float32))
        m //= 2
    blocks.append(np.ones((L, L), np.float32))
    return jnp.asarray(np.concatenate(blocks, 0), BF16), jnp.asarray(np.stack(masks), F32)


def _hgrn_core(qraw, fp, lb, sum_mat, mask_ref):
    L = H_CHUNK
    sq = jax.nn.sigmoid(qraw)
    q = qraw * sq
    sneg = jax.nn.sigmoid(-fp)
    log_sig = jnp.minimum(fp, 0.0) - jnp.log1p(jnp.exp(-jnp.abs(fp)))
    a1 = jnp.log(jnp.maximum(lb, LB_FLOOR))
    a2 = jnp.log1p(-lb) + log_sig
    logf = jnp.maximum(a1, a2) + jnp.log1p(jnp.exp(-jnp.abs(a1 - a2)))
    w1 = jnp.exp(a1 - logf)
    w2 = jnp.exp(a2 - logf)
    k = (1.0 - lb) * sneg
    hi = logf.astype(BF16)
    r1 = logf - hi.astype(F32)
    mid = r1.astype(BF16)
    lo = (r1 - mid.astype(F32)).astype(BF16)
    sums = lax.dot_general(sum_mat, jnp.concatenate([hi, mid, lo], axis=1), NN, preferred_element_type=F32)
    sums = sums[:, 0:HEAD] + sums[:, HEAD:2 * HEAD] + sums[:, 2 * HEAD:3 * HEAD]
    b = sums[0:L]
    b_last = sums[(N_LEVEL + 1) * L:(N_LEVEL + 2) * L]
    eye = lax.broadcasted_iota(jnp.int32, (L, L), 0) == lax.broadcasted_iota(jnp.int32, (L, L), 1)
    attn = jnp.where(eye, jnp.sum(q * k, axis=1, keepdims=True), 0.0)
    fa, fb, ea, eb = [], [], [], []
    for l in range(N_LEVEL):
        d = sums[(l + 1) * L:(l + 2) * L]
        e_a = jnp.exp(jnp.minimum(d, 0.0))
        e_b = jnp.exp(jnp.minimum(-d, 0.0))
        a_l, b_l = q * e_a, k * e_b
        attn = attn + mask_ref[l] * _dot(a_l, b_l, NT)
        fa.append(a_l), fb.append(b_l), ea.append(e_a), eb.append(e_b)
    return dict(sq=sq, q=q, sneg=sneg, logf=logf, w1=w1, w2=w2, k=k, b=b, b_last=b_last, attn=attn,
                fa=fa, fb=fb, ea=ea, eb=eb)


def _hgrn_fwd(p, lbrow, gout):
    T = p.shape[0]
    nch = T // H_CHUNK
    per = 4 if nch % 4 == 0 else 2 if nch % 2 == 0 else 1
    step, subs = per * H_CHUNK, range(per)
    sum_mat, masks = _hgrn_consts()

    def body(p_ref, lb_ref, g_ref, m_ref, mask_ref, o_ref, z_ref, st_ref, state):
        @pl.when(pl.program_id(0) == 0)
        def _():
            state[...] = jnp.zeros_like(state)

        sum_m = m_ref[...]
        for sub, h in [(s_, h_) for s_ in subs for h_ in range(N_HEAD)]:
            rows = pl.ds(sub * H_CHUNK, H_CHUNK)
            col = lambda part: pl.ds(part * WIDTH + h * HEAD, HEAD)
            hs = pl.ds(h * HEAD, HEAD)
            v = p_ref[rows, col(2)]
            c = _hgrn_core(p_ref[rows, col(0)], p_ref[rows, col(1)], lb_ref[:, hs], sum_m, mask_ref)
            s0 = state[h]
            st_ref[sub, h] = s0
            o = _dot(c["attn"], v) + _dot(c["q"] * jnp.exp(c["b"]), s0, NT)
            k_dec = c["k"] * jnp.exp(c["b_last"] - c["b"])
            decay = jnp.exp(jnp.max(c["b_last"], axis=0, keepdims=True))
            state[h] = s0 * decay + _dot(v, k_dec, TN)
            o_ref[rows, hs] = o
            r = lax.rsqrt(jnp.mean(o * o, axis=-1, keepdims=True) + NORM_EPS)
            z_ref[rows, hs] = (o * r * g_ref[:, hs] * jax.nn.sigmoid(p_ref[rows, col(3)])).astype(BF16)

    full = lambda shape: pl.BlockSpec(shape, lambda c: (0,) * len(shape))
    return pl.pallas_call(
        body, name="hgrn_fwd", grid=(nch // per,),
        in_specs=[pl.BlockSpec((step, 4 * WIDTH), lambda c: (c, 0)), full((1, WIDTH)), full((1, WIDTH)),
                  full(sum_mat.shape), full(masks.shape)],
        out_specs=[pl.BlockSpec((step, WIDTH), lambda c: (c, 0)),
                   pl.BlockSpec((None, step, WIDTH), lambda c: (0, c, 0)),
                   pl.BlockSpec((per, N_HEAD, HEAD, HEAD), lambda c: (c, 0, 0, 0))],
        out_shape=[jax.ShapeDtypeStruct((T, WIDTH), F32), jax.ShapeDtypeStruct((N_BRANCH, T, WIDTH), BF16),
                   jax.ShapeDtypeStruct((nch, N_HEAD, HEAD, HEAD), F32)],
        scratch_shapes=[pltpu.VMEM((N_HEAD, HEAD, HEAD), F32)], compiler_params=_params(1),
    )(p, lbrow, gout, sum_mat, masks)


def _hgrn_bwd(p, o_saved, dz, states, lbrow, gout, dp, after=()):
    T = p.shape[0]
    nch = T // H_CHUNK
    L = H_CHUNK
    per = 4 if nch % 4 == 0 else 2 if nch % 2 == 0 else 1
    step, subs = per * L, range(per - 1, -1, -1)
    sum_mat, masks = _hgrn_consts()

    def body(p_ref, o_ref, dz_ref, st_ref, lb_ref, g_ref, m_ref, mask_ref, dp_in, *rest):
        del dp_in
        dp_ref, dlb_ref, dg_ref, dstate = rest[len(after):]

        @pl.when(pl.program_id(0) == 0)
        def _():
            dstate[...] = jnp.zeros_like(dstate)
            dlb_ref[...] = jnp.zeros_like(dlb_ref)
            dg_ref[...] = jnp.zeros_like(dg_ref)

        sum_m = m_ref[...]
        for sub, h in [(s_, h_) for s_ in subs for h_ in range(N_HEAD)]:
            rows = pl.ds(sub * H_CHUNK, H_CHUNK)
            col = lambda part: pl.ds(part * WIDTH + h * HEAD, HEAD)
            hs = pl.ds(h * HEAD, HEAD)
            qraw, fp, v, go = p_ref[rows, col(0)], p_ref[rows, col(1)], p_ref[rows, col(2)], p_ref[rows, col(3)]
            lb, g = lb_ref[:, hs], g_ref[:, hs]
            c = _hgrn_core(qraw, fp, lb, sum_m, mask_ref)
            q, k, b, b_last = c["q"], c["k"], c["b"], c["b_last"]
            s0, ds1 = st_ref[sub, h], dstate[h]
            e_b = jnp.exp(b)
            q_dec = q * e_b
            e_bl = jnp.exp(b_last - b)
            k_dec = k * e_bl
            decay = jnp.exp(jnp.max(b_last, axis=0, keepdims=True))
            o = o_ref[rows, hs]
            r = lax.rsqrt(jnp.mean(o * o, axis=-1, keepdims=True) + NORM_EPS)
            n = o * r
            sgo = jax.nn.sigmoid(go)
            dza = dz_ref[rows, hs]
            dgo = dza * n * g * sgo * (1.0 - sgo)
            dg_ref[:, hs] += _row0(jnp.sum(dza * n * sgo, axis=0, keepdims=True))
            dn = dza * g * sgo
            do = r * (dn - n * jnp.mean(dn * n, axis=-1, keepdims=True))
            dattn = _dot(do, v, NT)
            dv = _dot(c["attn"], do, TN) + _dot(k_dec, ds1, NT)
            dq_dec = _dot(do, s0)
            dk_dec = _dot(v, ds1)
            ddiag = jnp.sum(do * v, axis=1, keepdims=True)
            dq = dq_dec * e_b + ddiag * k
            dk = dk_dec * e_bl + ddiag * q
            dsums = [dq_dec * q_dec - dk_dec * k_dec]
            for l in range(N_LEVEL):
                dm = mask_ref[l] * dattn
                da = _dot(dm, c["fb"][l])
                db = _dot(dm, c["fa"][l], TN)
                dq = dq + da * c["ea"][l]
                dk = dk + db * c["eb"][l]
                dsums.append(da * c["fa"][l] - db * c["fb"][l])
            dlast = jnp.sum(ds1 * s0, axis=0, keepdims=True) * decay
            dsums.append(dk_dec * k_dec + _row0(dlast, L))
            dlogf = _dot(sum_m, jnp.concatenate(dsums, axis=0), TN)
            dstate[h] = ds1 * decay + _dot(do, q_dec, TN)
            sq, sneg = c["sq"], c["sneg"]
            dqraw = dq * sq * (1.0 + qraw * (1.0 - sq))
            dfp = dlogf * c["w2"] * sneg - dk * (1.0 - lb) * sneg * (1.0 - sneg)
            inv_lb = jnp.where(lb > LB_FLOOR, 1.0 / jnp.maximum(lb, LB_FLOOR), 0.0)
            dlb_tok = dlogf * (c["w1"] * inv_lb - c["w2"] / (1.0 - lb)) - dk * sneg
            dlb_ref[:, hs] += _row0(jnp.sum(dlb_tok, axis=0, keepdims=True))
            dp_ref[rows, col(0)] = dqraw.astype(BF16)
            dp_ref[rows, col(1)] = dfp.astype(BF16)
            dp_ref[rows, col(2)] = dv.astype(BF16)
            dp_ref[rows, col(3)] = dgo.astype(BF16)

    full = lambda shape: pl.BlockSpec(shape, lambda c: (0,) * len(shape))
    rev = lambda c: nch // per - 1 - c
    return pl.pallas_call(
        body, name="hgrn_bwd", grid=(nch // per,),
        in_specs=[pl.BlockSpec((step, 4 * WIDTH), lambda c: (rev(c), 0)), pl.BlockSpec((step, WIDTH), lambda c: (rev(c), 0)),
                  pl.BlockSpec((None, step, WIDTH), lambda c: (0, rev(c), 0)),
                  pl.BlockSpec((per, N_HEAD, HEAD, HEAD), lambda c: (rev(c), 0, 0, 0)),
                  full((1, WIDTH)), full((1, WIDTH)), full(sum_mat.shape), full(masks.shape), ANY, *[ANY] * len(after)],
        out_specs=[pl.BlockSpec((step, 4 * WIDTH), lambda c: (rev(c), 0)), full((SUBLANES, WIDTH)), full((SUBLANES, WIDTH))],
        out_shape=[jax.ShapeDtypeStruct(dp.shape, dp.dtype), jax.ShapeDtypeStruct((SUBLANES, WIDTH), F32),
                   jax.ShapeDtypeStruct((SUBLANES, WIDTH), F32)],
        scratch_shapes=[pltpu.VMEM((N_HEAD, HEAD, HEAD), F32)], input_output_aliases={8: 0},
        compiler_params=_params(1),
    )(p, o_saved, dz, states, lbrow, gout, sum_mat, masks, dp, *after)


def _shift_down(tile, halo, s):
    tm = tile.shape[0]
    rows = lax.broadcasted_iota(jnp.int32, tile.shape, 0)
    head = jnp.concatenate([pltpu.roll(halo, s, 0), jnp.zeros((tm - SUBLANES, tile.shape[1]), tile.dtype)], axis=0)
    return jnp.where(rows < s, head, pltpu.roll(tile, s, 0))


def _shift_up(tile, halo, s):
    tm = tile.shape[0]
    rows = lax.broadcasted_iota(jnp.int32, tile.shape, 0)
    tail = jnp.concatenate([jnp.zeros((tm - SUBLANES, tile.shape[1]), tile.dtype), pltpu.roll(halo, SUBLANES - s, 0)], axis=0)
    return jnp.where(rows >= tm - s, tail, pltpu.roll(tile, tm - s, 0))


def _conv_fwd(p, w, z, after=()):
    T = p.shape[0]
    tm = min(2 * _token_tile(T), T)
    per = tm // SUBLANES

    def body(bg_ref, cg_ref, xc_ref, hcg_ref, hxc_ref, w_ref, *rest):
        z_ref = rest[-1]
        zc = cg_ref[...] * xc_ref[...]
        hz = jnp.where(pl.program_id(0) > 0, hcg_ref[...] * hxc_ref[...], 0.0)
        y = (w_ref[pl.ds(0, 1), :] * _shift_down(zc, hz, 2) + w_ref[pl.ds(1, 1), :] * _shift_down(zc, hz, 1)
             + w_ref[pl.ds(2, 1), :] * zc)
        z_ref[...] = (bg_ref[...] * y).astype(BF16)

    tile = lambda cb: pl.BlockSpec((tm, WIDTH), lambda i: (i, cb))
    prev = lambda cb: pl.BlockSpec((SUBLANES, WIDTH), lambda i: (jnp.maximum(i * per - 1, 0), cb))
    return pl.pallas_call(
        body, name="conv_fwd", grid=(T // tm,),
        in_specs=[tile(4), tile(5), tile(6), prev(5), prev(6), pl.BlockSpec((CONV_K, WIDTH), lambda i: (0, 0)), ANY,
                  *[ANY] * len(after)],
        out_specs=pl.BlockSpec((None, tm, WIDTH), lambda i: (1, i, 0)),
        out_shape=jax.ShapeDtypeStruct(z.shape, z.dtype), input_output_aliases={6: 0}, compiler_params=_params(1),
    )(p, p, p, p, p, w, z, *after)


def _conv_bwd(p, w, dz, dp):
    T = p.shape[0]
    tm = min(2 * _token_tile(T), T)
    per = tm // SUBLANES
    last = T // SUBLANES - 1

    def body(bg_ref, cg_ref, xc_ref, hcg_ref, hxc_ref, nbg_ref, dzb_ref, ndzb_ref, w_ref, dp_in, dp_ref, dw_ref, stash):
        del dp_in
        i, jj = pl.program_id(0), pl.program_id(1)

        @pl.when(jnp.logical_and(i == 0, jj == 0))
        def _():
            dw_ref[...] = jnp.zeros_like(dw_ref)

        @pl.when(jj == 0)
        def _():
            cg, xc, bg = cg_ref[...], xc_ref[...], bg_ref[...]
            w0, w1, w2 = w_ref[pl.ds(0, 1), :], w_ref[pl.ds(1, 1), :], w_ref[pl.ds(2, 1), :]
            zc = cg * xc
            hz = jnp.where(i > 0, hcg_ref[...] * hxc_ref[...], 0.0)
            z2, z1 = _shift_down(zc, hz, 2), _shift_down(zc, hz, 1)
            y = w0 * z2 + w1 * z1 + w2 * zc
            dzb = dzb_ref[...]
            dy = dzb * bg
            hdy = jnp.where(i < pl.num_programs(0) - 1, ndzb_ref[...] * nbg_ref[...], 0.0)
            dzc = w2 * dy + w1 * _shift_up(dy, hdy, 1) + w0 * _shift_up(dy, hdy, 2)
            rows = lax.broadcasted_iota(jnp.int32, (SUBLANES, WIDTH), 0)
            colsum = lambda t: jnp.sum(t, axis=0, keepdims=True)
            dw_ref[...] += (jnp.where(rows == 0, colsum(dy * z2), 0.0) + jnp.where(rows == 1, colsum(dy * z1), 0.0)
                            + jnp.where(rows == 2, colsum(dy * zc), 0.0))
            dp_ref[...] = (dzb * y).astype(BF16)
            stash[0] = dzc * xc
            stash[1] = dzc * cg

        @pl.when(jj > 0)
        def _():
            dp_ref[...] = stash[jj - 1].astype(BF16)

    n_tiles = T // tm
    tile = lambda cb: pl.BlockSpec((tm, WIDTH), lambda i, jj: (i, cb))
    prev = lambda cb: pl.BlockSpec((SUBLANES, WIDTH), lambda i, jj: (jnp.maximum(i * per - 1, 0), cb))
    nxt = lambda i: jnp.minimum((i + 1) * per, last)
    return pl.pallas_call(
        body, name="conv_bwd", grid=(n_tiles, 3),
        in_specs=[tile(4), tile(5), tile(6), prev(5), prev(6),
                  pl.BlockSpec((SUBLANES, WIDTH), lambda i, jj: (nxt(i), 4)),
                  pl.BlockSpec((None, tm, WIDTH), lambda i, jj: (1, i, 0)),
                  pl.BlockSpec((None, SUBLANES, WIDTH), lambda i, jj: (1, nxt(i), 0)),
                  pl.BlockSpec((CONV_K, WIDTH), lambda i, jj: (0, 0)), ANY],
        out_specs=[pl.BlockSpec((tm, WIDTH), lambda i, jj: (i, 4 + jj)),
                   pl.BlockSpec((SUBLANES, WIDTH), lambda i, jj: (0, 0))],
        out_shape=[jax.ShapeDtypeStruct(dp.shape, dp.dtype), jax.ShapeDtypeStruct((SUBLANES, WIDTH), F32)],
        scratch_shapes=[pltpu.VMEM((2, tm, WIDTH), F32)], input_output_aliases={9: 0}, compiler_params=_params(2),
    )(p, p, p, p, p, p, dz, dz, w, dp)


GELU_C = float(np.sqrt(2.0 / np.pi))
GELU_A = 0.044715


def _gelu(x):
    th = jnp.tanh(GELU_C * (x + GELU_A * x * x * x))
    return 0.5 * x * (1.0 + th), th


def _gelu_grad(x, th):
    return 0.5 * (1.0 + th) + 0.5 * x * (1.0 - th * th) * GELU_C * (1.0 + 3.0 * GELU_A * x * x)


def _sg_core(u, v, lng, lnb, ws_ref, bs_ref):
    gu, thu = _gelu(u)
    gv, thv = _gelu(v)
    xc = gv - jnp.mean(gv, axis=-1, keepdims=True)
    rs = lax.rsqrt(jnp.mean(xc * xc, axis=-1, keepdims=True) + LN_EPS)
    xh = xc * rs
    vp = xh * lng + lnb
    tril = (lax.broadcasted_iota(jnp.int32, (SG_CHUNK, SG_CHUNK), 0)
            >= lax.broadcasted_iota(jnp.int32, (SG_CHUNK, SG_CHUNK), 1))
    wm = [jnp.where(tril, ws_ref[g], 0.0).astype(BF16) for g in range(SG_GROUPS)]
    gs = lambda t, g: t[:, g * LANES:(g + 1) * LANES]
    sv = jnp.concatenate([_dot(wm[g], gs(vp, g)) + bs_ref[g] for g in range(SG_GROUPS)], axis=1)
    return dict(gu=gu, thu=thu, thv=thv, rs=rs, xh=xh, vp=vp, tril=tril, wm=wm, sv=sv)


def _sg_step_rows(T):
    return next(n * SG_CHUNK for n in (4, 2, 1) if T % (n * SG_CHUNK) == 0)


def _sg_fwd(p, lng, lnb, ws, bs, z):
    T = p.shape[0]
    step = _sg_step_rows(T)

    def body(u_ref, v_ref, lng_ref, lnb_ref, ws_ref, bs_ref, z_in, z_ref):
        del z_in
        for sub in range(step // SG_CHUNK):
            rows = pl.ds(sub * SG_CHUNK, SG_CHUNK)
            c = _sg_core(u_ref[rows, :], v_ref[rows, :], lng_ref[...], lnb_ref[...], ws_ref, bs_ref)
            z_ref[rows, :] = (c["gu"] * c["sv"]).astype(BF16)

    full = lambda shape: pl.BlockSpec(shape, lambda c: (0,) * len(shape))
    return pl.pallas_call(
        body, name="sg_fwd", grid=(T // step,),
        in_specs=[pl.BlockSpec((step, WIDTH), lambda c: (c, 7)), pl.BlockSpec((step, WIDTH), lambda c: (c, 8)),
                  full((1, WIDTH)), full((1, WIDTH)), full(ws.shape), full(bs.shape), ANY],
        out_specs=pl.BlockSpec((None, step, WIDTH), lambda c: (2, c, 0)),
        out_shape=jax.ShapeDtypeStruct(z.shape, z.dtype), input_output_aliases={6: 0}, compiler_params=_params(1),
    )(p, p, lng, lnb, ws, bs, z)


def _sg_bwd(p, lng, lnb, ws, bs, dz, dp):
    T = p.shape[0]
    step = _sg_step_rows(T)

    def body(u_ref, v_ref, lng_ref, lnb_ref, ws_ref, bs_ref, dz_ref, dp_in, dp_ref, dws_ref, dbs_ref, dlng_ref, dlnb_ref,
             stash):
        del dp_in
        cidx, jj = pl.program_id(0), pl.program_id(1)

        @pl.when(jnp.logical_and(cidx == 0, jj == 0))
        def _():
            dws_ref[...] = jnp.zeros_like(dws_ref)
            dbs_ref[...] = jnp.zeros_like(dbs_ref)
            dlng_ref[...] = jnp.zeros_like(dlng_ref)
            dlnb_ref[...] = jnp.zeros_like(dlnb_ref)

        def chunk(rows):
            u, v, lng = u_ref[rows, :], v_ref[rows, :], lng_ref[...]
            c = _sg_core(u, v, lng, lnb_ref[...], ws_ref, bs_ref)
            dzc = dz_ref[rows, :]
            gs = lambda t, g: t[:, g * LANES:(g + 1) * LANES]
            dsv = dzc * c["gu"]
            dvp = []
            for g in range(SG_GROUPS):
                dsv_g = gs(dsv, g)
                dws_ref[g] += jnp.where(c["tril"], _dot(dsv_g, gs(c["vp"], g), NT), 0.0)
                dbs_ref[g] += jnp.sum(dsv_g, axis=1, keepdims=True)
                dvp.append(_dot(c["wm"][g], dsv_g, TN))
            dvp = jnp.concatenate(dvp, axis=1)
            xh = c["xh"]
            dlng_ref[...] += _row0(jnp.sum(dvp * xh, axis=0, keepdims=True))
            dlnb_ref[...] += _row0(jnp.sum(dvp, axis=0, keepdims=True))
            dxh = dvp * lng
            dgv = c["rs"] * (dxh - jnp.mean(dxh, axis=-1, keepdims=True) - xh * jnp.mean(dxh * xh, axis=-1, keepdims=True))
            dp_ref[rows, :] = (dzc * c["sv"] * _gelu_grad(u, c["thu"])).astype(BF16)
            stash[rows, :] = dgv * _gelu_grad(v, c["thv"])

        @pl.when(jj == 0)
        def _():
            for sub in range(step // SG_CHUNK):
                chunk(pl.ds(sub * SG_CHUNK, SG_CHUNK))

        @pl.when(jj == 1)
        def _():
            dp_ref[...] = stash[...].astype(BF16)

    full = lambda shape: pl.BlockSpec(shape, lambda c, jj: (0,) * len(shape))
    return pl.pallas_call(
        body, name="sg_bwd", grid=(T // step, 2),
        in_specs=[pl.BlockSpec((step, WIDTH), lambda c, jj: (c, 7)), pl.BlockSpec((step, WIDTH), lambda c, jj: (c, 8)),
                  full((1, WIDTH)), full((1, WIDTH)), full(ws.shape), full(bs.shape),
                  pl.BlockSpec((None, step, WIDTH), lambda c, jj: (2, c, 0)), ANY],
        out_specs=[pl.BlockSpec((step, WIDTH), lambda c, jj: (c, 7 + jj)), full(ws.shape), full(bs.shape),
                   full((SUBLANES, WIDTH)), full((SUBLANES, WIDTH))],
        out_shape=[jax.ShapeDtypeStruct(dp.shape, dp.dtype), jax.ShapeDtypeStruct(ws.shape, F32),
                   jax.ShapeDtypeStruct(bs.shape, F32), jax.ShapeDtypeStruct((SUBLANES, WIDTH), F32),
                   jax.ShapeDtypeStruct((SUBLANES, WIDTH), F32)],
        scratch_shapes=[pltpu.VMEM((step, WIDTH), F32)], input_output_aliases={7: 0}, compiler_params=_params(2),
    )(p, p, lng, lnb, ws, bs, dz, dp)


BRANCH_COLS = D_MODEL // N_CHIP
GATE_UNIT0 = GATE_COL0 // WIDTH
UNITS = D_MODEL // WIDTH


def _unit_specs(order):
    def spec(which):
        def index(*g):
            _, n, u = order(*g)
            return (2 * u + which, n, 0, 0)
        return pl.BlockSpec((None, None, WIDTH, BRANCH_COLS), index)
    return [spec(0), spec(1)]


def _merge_fwd(z, p, wb):
    T = z.shape[1]
    tm = min(2 * _token_tile(T), T)
    order = lambda i, u, n: (i, n, u)

    def body(z_ref, wa_ref, wb_ref, gt_ref, out_ref, acc):
        n = pl.program_id(2)
        zv = z_ref[...]
        y = jnp.concatenate([_dot(zv, wa_ref[...]), _dot(zv, wb_ref[...])], axis=1)
        part = jax.nn.sigmoid(gt_ref[...]) * y

        @pl.when(n == 0)
        def _():
            acc[...] = part

        @pl.when(n > 0)
        def _():
            acc[...] += part

        @pl.when(n == N_BRANCH - 1)
        def _():
            out_ref[...] = acc[...].astype(BF16)

    return pl.pallas_call(
        body, name="merge_fwd", grid=(T // tm, UNITS, N_BRANCH),
        in_specs=[pl.BlockSpec((None, tm, WIDTH), lambda i, u, n: (n, i, 0)), *_unit_specs(order),
                  pl.BlockSpec((tm, WIDTH), lambda i, u, n: (i, GATE_UNIT0 + UNITS * n + u))],
        out_specs=pl.BlockSpec((tm, WIDTH), lambda i, u, n: (i, u)),
        out_shape=jax.ShapeDtypeStruct((T, D_MODEL), BF16),
        scratch_shapes=[pltpu.VMEM((tm, WIDTH), F32)], compiler_params=_params(3))(z, wb, wb, p)


def _merge_bwd(z, p, wb, dmerged):
    T = z.shape[1]
    tm = min(2 * _token_tile(T), T)
    order = lambda n, u, i: (i, n, u)

    def body(z_ref, wa_ref, wb_ref, gt_ref, dm_ref, dp_ref, dw_ref, dz_ref):
        u, i = pl.program_id(1), pl.program_id(2)
        zv, wa, wbv = z_ref[...], wa_ref[...], wb_ref[...]
        y = jnp.concatenate([_dot(zv, wa), _dot(zv, wbv)], axis=1)
        gate = jax.nn.sigmoid(gt_ref[...])
        dm = dm_ref[...].astype(F32)
        dp_ref[...] = (dm * y * gate * (1.0 - gate)).astype(BF16)
        dyv = (dm * gate).astype(BF16)
        dw = _dot(zv, dyv, TN)
        part = _dot(dyv[:, :BRANCH_COLS], wa, NT) + _dot(dyv[:, BRANCH_COLS:], wbv, NT)
        rows = pl.ds(pl.multiple_of(i * tm, tm), tm)

        @pl.when(i == 0)
        def _():
            dw_ref[0] = dw[:, :BRANCH_COLS]
            dw_ref[1] = dw[:, BRANCH_COLS:]

        @pl.when(i > 0)
        def _():
            dw_ref[0] += dw[:, :BRANCH_COLS]
            dw_ref[1] += dw[:, BRANCH_COLS:]

        @pl.when(u == 0)
        def _():
            dz_ref[rows, :] = part

        @pl.when(u > 0)
        def _():
            dz_ref[rows, :] += part

    unit = lambda n, u, i: (i, GATE_UNIT0 + UNITS * n + u)
    return pl.pallas_call(
        body, name="merge_bwd", grid=(N_BRANCH, UNITS, T // tm),
        in_specs=[pl.BlockSpec((None, tm, WIDTH), lambda n, u, i: (n, i, 0)), *_unit_specs(order),
                  pl.BlockSpec((tm, WIDTH), unit), pl.BlockSpec((tm, WIDTH), lambda n, u, i: (i, u))],
        out_specs=[pl.BlockSpec((tm, WIDTH), unit),
                   pl.BlockSpec((2, None, WIDTH, BRANCH_COLS), lambda n, u, i: (u, n, 0, 0)),
                   pl.BlockSpec((None, T, WIDTH), lambda n, u, i: (n, 0, 0))],
        out_shape=[jax.ShapeDtypeStruct((T, IN_COLS), BF16),
                   jax.ShapeDtypeStruct((N_CHIP, N_BRANCH, WIDTH, BRANCH_COLS), F32),
                   jax.ShapeDtypeStruct((N_BRANCH, T, WIDTH), F32)],
        compiler_params=_params(3))(z, wb, wb, p, dmerged)


def _layer_fwd(x, h, win, small, next_gain, hooks):
    p = _mm_cols("in_proj", h, win, [F32])[0]
    o_hgrn, z, states = _hgrn_fwd(p, small["lbs"], small["g_hgrn_out"])
    z = _conv_fwd(p, small["w_conv"], z, after=hooks["after_hgrn"]([o_hgrn]))
    z = _sg_fwd(p, small["sg_ln_g"], small["sg_ln_b"], small["w_sg"], small["b_sg"], z)
    wb, wo, w1, w2 = hooks["late_weights"]([z])
    wb = wb.reshape(N_CHIP, N_BRANCH, WIDTH, BRANCH_COLS)
    merged = _merge_fwd(z, p, wb)
    x_mid, h2 = _mm_rows("out_proj", merged, wo, x, small["g_ffn"])
    s = _mm_cols("ff1", h2, w1, [BF16], epilogue=lambda acc: (jnp.square(jnp.maximum(acc, 0.0)),))[0]
    if next_gain is None:
        x_out, h_next = _mm_rows("ff2_last", s, w2, x_mid, after=hooks["before_last"]([s])), None
    else:
        x_out, h_next = _mm_rows("ff2", s, w2, x_mid, next_gain, after=hooks["before_last"]([s]))
    saved = dict(x=x, h=h, p=p, o_hgrn=o_hgrn, z=z, states=states, merged=merged, x_mid=x_mid, h2=h2, s=s)
    return x_out, h_next, saved, [win, wb, wo, w1, w2]


def _layer_bwd(dx_out, sv, wts, small, tick, after):
    win, wb, wo, w1, w2 = wts
    g = {}
    da = _mm_cols_t("ff2_dgrad", dx_out, w2, BF16, extra=(sv["s"],), after=after,
                    epilogue=lambda acc, s: (acc * 2.0 * jnp.sqrt(s.astype(F32)),))
    d_ff2 = _mm_wgrad("ff2_wgrad", sv["s"], dx_out, w2.shape[1], D_MODEL, True, False)
    d_ff1 = _mm_wgrad("ff1_wgrad", sv["h2"], da, D_MODEL, w1.shape[2], False, True)
    dx_mid, g["g_ffn"] = _dgrad_norm_bwd("ff1_dgrad", da, w1, sv["x_mid"], small["g_ffn"], dx_out)
    after = tick([dx_mid], [("w_ff1", d_ff1), ("w_ff2", d_ff2)])
    dmerged = _mm_cols_t("out_proj_dgrad", dx_mid, wo, BF16, after=after)
    d_o = _mm_wgrad("out_proj_wgrad", sv["merged"], dx_mid, wo.shape[1], D_MODEL, True, False)
    dp, d_branch, dz = _merge_bwd(sv["z"], sv["p"], wb, dmerged)
    d_branch = d_branch.reshape(N_CHIP, N_BRANCH * WIDTH, BRANCH_COLS)
    dp, g["w_conv"] = _conv_bwd(sv["p"], small["w_conv"], dz, dp)
    dp, g["w_sg"], g["b_sg"], g["sg_ln_g"], g["sg_ln_b"] = _sg_bwd(
        sv["p"], small["sg_ln_g"], small["sg_ln_b"], small["w_sg"], small["b_sg"], dz, dp)
    after = tick([dp], [("w_branch", d_branch), ("w_o", d_o)])
    dp, g["lbs"], g["g_hgrn_out"] = _hgrn_bwd(sv["p"], sv["o_hgrn"], dz, sv["states"], small["lbs"],
                                              small["g_hgrn_out"], dp, after=after)
    after = tick([dp], [])
    d_in = _mm_wgrad("in_proj_wgrad", sv["h"], dp, D_MODEL, win.shape[2], False, True, after=after)
    dx, g["g_mix"] = _dgrad_norm_bwd("in_proj_dgrad", dp, win, sv["x"], small["g_mix"], dx_mid)
    return dx, g, tick([dx], [("w_in", d_in)])


def _mesh_pos():
    return lax.axis_index("x"), lax.axis_index("y"), lax.axis_index("c")


def _other_chips(x, y):
    return [(1 - x, y), (x, 1 - y), (1 - x, 1 - y)]


def _remote(src, dst, send_sems, recv_sems, k, to):
    return pltpu.make_async_remote_copy(src_ref=src, dst_ref=dst, send_sem=send_sems.at[k], recv_sem=recv_sems.at[k],
                                        device_id=to, device_id_type=MESH)


def _gather_call(name, body, buf, after):
    scratch = [pltpu.SemaphoreType.DMA((7,)), pltpu.SemaphoreType.DMA((7,))]
    return pl.pallas_call(
        body, name=name, in_specs=[ANY] * (1 + len(after)), out_specs=ANY,
        out_shape=jax.ShapeDtypeStruct(buf.shape, buf.dtype), scratch_shapes=scratch, input_output_aliases={0: 0})(buf, *after)


HBM = pl.BlockSpec(memory_space=pltpu.HBM)
SEM = pl.BlockSpec(memory_space=pltpu.SEMAPHORE)
DATAFLOW = pltpu.SideEffectType.DATAFLOW_SIDE_EFFECTING


def _split_start(name, bufs, copies, n_copies, after=()):
    n = len(bufs)

    def body(*refs):
        send_sems, recv_sems = refs[n + len(after)], refs[n + len(after) + 1]
        for cp in copies(refs[:n], send_sems, recv_sems):
            cp.start()
        refs[-1][...] = jnp.zeros_like(refs[-1])

    outs = pl.pallas_call(
        body, name=name,
        out_shape=(pltpu.SemaphoreType.DMA((n_copies,)), pltpu.SemaphoreType.DMA((n_copies,)),
                   *[pltpu.HBM(b.shape, b.dtype) for b in bufs], jax.ShapeDtypeStruct((SUBLANES, LANES), F32)),
        in_specs=[HBM] * n + [ANY] * len(after),
        out_specs=(SEM, SEM, *[HBM] * n, pl.BlockSpec(memory_space=pltpu.VMEM)),
        input_output_aliases={t: 2 + t for t in range(n)},
        compiler_params=pltpu.CompilerParams(has_side_effects=DATAFLOW),
    )(*[pltpu.with_memory_space_constraint(b, pltpu.HBM) for b in bufs], *after)
    return outs[0], outs[1], list(outs[2:2 + n]), outs[-1]


def _split_wait(name, started, copies, after):
    send_sems, recv_sems, bufs, _ = started
    n = len(bufs)

    def body(*refs):
        for cp in copies(refs[:n], refs[n], refs[n + 1]):
            cp.wait_send()
            cp.wait_recv()

    return list(pl.pallas_call(
        body, name=name, out_shape=tuple(pltpu.HBM(b.shape, b.dtype) for b in bufs),
        in_specs=[HBM] * n + [SEM, SEM] + [ANY] * len(after), out_specs=tuple([HBM] * n),
        input_output_aliases={t: t for t in range(n)},
        compiler_params=pltpu.CompilerParams(has_side_effects=DATAFLOW),
    )(*bufs, send_sems, recv_sems, *after))


def _weight_ici_copies(refs, send_sems, recv_sems):
    x, y, c = _mesh_pos()
    out = []
    for t, ref in enumerate(refs):
        rh = ref.shape[1] // 2
        mine = ref.at[2 * x + y, pl.ds(c * rh, rh), :]
        out += [_remote(mine, mine, send_sems, recv_sems, 3 * t + j, (*chip, c)) for j, chip in enumerate(_other_chips(x, y))]
    return out


def _weight_d2d_copies(refs, send_sems, recv_sems):
    x, y, c = _mesh_pos()
    out = []
    for t, ref in enumerate(refs):
        rh = ref.shape[1] // 2
        for j, chip in enumerate(_other_chips(x, y)):
            blk = ref.at[2 * chip[0] + chip[1], pl.ds(c * rh, rh), :]
            out.append(_remote(blk, blk, send_sems, recv_sems, 3 * t + j, (x, y, 1 - c)))
    return out


def _swap_part(refs, send_sems, recv_sems, s0):
    x, y, c = _mesh_pos()
    n = len(refs) // 2
    out = []
    for t in range(n):
        rh = refs[t].shape[1] // 2
        out.append(_remote(refs[t].at[:, pl.ds((1 - c) * rh, rh), :], refs[n + t], send_sems, recv_sems, s0 + t, (x, y, 1 - c)))
    return out


def _exchange_part(refs, send_sems, recv_sems, s0):
    x, y, c = _mesh_pos()
    n = len(refs) // 2
    out = []
    for t in range(n):
        for j, chip in enumerate(_other_chips(x, y)):
            out.append(_remote(refs[t].at[2 * chip[0] + chip[1]], refs[n + t].at[j], send_sems, recv_sems, s0 + 3 * t + j,
                               (*chip, c)))
    return out


def _gather_part(refs, send_sems, recv_sems, s0):
    x, y, c = _mesh_pos()
    return [_remote(ref.at[c], ref.at[c], send_sems, recv_sems, s0 + t, (x, y, 1 - c)) for t, ref in enumerate(refs)]


def _all_to_all_copies(refs, send_sems, recv_sems):
    x, y, c = _mesh_pos()
    blk = refs[0].at[4 * x + 2 * y + c]
    peers = [(x, y, 1 - c)] + [(*chip, cc) for chip in _other_chips(x, y) for cc in (c, 1 - c)]
    return [_remote(blk, blk, send_sems, recv_sems, k, peer) for k, peer in enumerate(peers)]


class _GradPipeline:
    def __init__(self, pos):
        self.pos = pos
        self.groups, self.pending, self.count = [], None, 0
        self.reduced = {n: [None] * DEPTH for n in BIG_NAMES}

    def busy(self):
        return bool(self.groups) or self.pending is not None

    def tick(self, deps, new):
        if self.pending is not None:
            started, copies, owners = self.pending
            bufs = _split_wait("grad_pipe_wait_%d" % self.count, started, copies, after=list(deps))
            for grp, lo, hi in owners:
                grp["bufs"] = bufs[lo:hi]
            self.pending = None
        parts = []
        for grp in list(self.groups):
            n, names = len(grp["names"]), grp["names"]
            if grp["stage"] == "swap":
                pair = [_pair_sum("grad_pair_sum_" + nm, f, r, self.pos)
                        for nm, f, r in zip(names, grp["bufs"][:n], grp["bufs"][n:])]
                grp["own32"] = [p32 for p32, _ in pair]
                landing = [lax.empty((3, *p16.shape[1:]), BF16) for _, p16 in pair]
                grp["stage"] = "exchange"
                parts.append((grp, [p16 for _, p16 in pair] + landing, _exchange_part, 3 * n))
            elif grp["stage"] == "exchange":
                halves = [_chip_sum("grad_chip_sum_" + nm, p32, r, self.pos)
                          for nm, p32, r in zip(names, grp["own32"], grp["bufs"][n:])]
                grp["stage"] = "gather"
                parts.append((grp, halves, _gather_part, n))
            else:
                for nm, b in zip(names, grp["bufs"]):
                    self.reduced[nm][grp["layer"]] = b.reshape(-1, b.shape[-1])
                self.groups.remove(grp)
        if new:
            grp = dict(names=[nm for nm, _, _ in new], layer=new[0][1], stage="swap")
            self.groups.append(grp)
            fulls = [g for _, _, g in new]
            landing = [lax.empty((N_CHIP, g.shape[1] // 2, g.shape[2]), F32) for g in fulls]
            parts.append((grp, fulls + landing, _swap_part, len(fulls)))
        if not parts:
            return ()
        bufs, layout, owners, sems = [], [], [], 0
        for grp, part_bufs, fn, n_sems in parts:
            layout.append((len(bufs), len(bufs) + len(part_bufs), fn, sems))
            owners.append((grp, len(bufs), len(bufs) + len(part_bufs)))
            bufs += part_bufs
            sems += n_sems

        def copies(refs, send_sems, recv_sems):
            out = []
            for lo, hi, fn, s0 in layout:
                out += fn(refs[lo:hi], send_sems, recv_sems, s0)
            return out

        started = _split_start("grad_pipe_start_%d" % self.count, bufs, copies, sems)
        self.pending = (started, copies, owners)
        self.count += 1
        return (started[3],)


def _gather_all(name, block, slot, after=()):
    buf = lax.dynamic_update_slice(jnp.zeros((8, *block.shape), block.dtype), block[None], (slot, 0, 0))

    def body(*refs):
        out_ref, send_sems, recv_sems = refs[1 + len(after):]
        x, y, c = _mesh_pos()
        chips = _other_chips(x, y)
        sibling = (x, y, 1 - c)
        slot_of = lambda px, py, pc: out_ref.at[4 * px + 2 * py + pc]
        started = [_remote(slot_of(x, y, c), slot_of(x, y, c), send_sems, recv_sems, 0, sibling)]
        started += [_remote(slot_of(x, y, c), slot_of(x, y, c), send_sems, recv_sems, 1 + j, (*chip, c))
                    for j, chip in enumerate(chips)]
        for cp in started:
            cp.start()
        for j, chip in enumerate(chips):
            _remote(slot_of(*chip, c), slot_of(*chip, c), send_sems, recv_sems, 1 + j, (*chip, c)).wait_recv()
            fw = _remote(slot_of(*chip, c), slot_of(*chip, c), send_sems, recv_sems, 4 + j, sibling)
            fw.start()
            started.append(fw)
        _remote(slot_of(x, y, 1 - c), slot_of(x, y, 1 - c), send_sems, recv_sems, 0, sibling).wait_recv()
        for j, chip in enumerate(chips):
            _remote(slot_of(*chip, 1 - c), slot_of(*chip, 1 - c), send_sems, recv_sems, 4 + j, sibling).wait_recv()
        for cp in started:
            cp.wait_send()

    return _gather_call(name, body, buf, after)


def _row_tile(rows, cols, block_bytes=ELEMWISE_BLOCK_BYTES):
    cap = max(SUBLANES, block_bytes // (4 * cols))
    tr = rows
    while tr > cap and tr % 2 == 0:
        tr //= 2
    return tr


def _pair_sum(name, grad, recv, pos):
    _, rh, cols = recv.shape
    tr = _row_tile(rh, cols, 2 * ELEMWISE_BLOCK_BYTES)
    per = rh // tr

    def body(pos_ref, g_ref, r_ref, own_ref, out16_ref):
        s = g_ref[...] + r_ref[...]
        out16_ref[...] = s.astype(BF16)

        @pl.when(pl.program_id(1) == pos_ref[0])
        def _():
            own_ref[...] = s

    blk = pl.BlockSpec((None, tr, cols), lambda i, k, pos_ref: (k, i, 0))
    return pl.pallas_call(
        body, name=name,
        grid_spec=pltpu.PrefetchScalarGridSpec(
            num_scalar_prefetch=1, grid=(per, N_CHIP),
            in_specs=[pl.BlockSpec((None, tr, cols), lambda i, k, pos_ref: (k, pos_ref[1] * per + i, 0)), blk],
            out_specs=[pl.BlockSpec((tr, cols), lambda i, k, pos_ref: (i, 0)), blk]),
        out_shape=[jax.ShapeDtypeStruct((rh, cols), F32), jax.ShapeDtypeStruct(recv.shape, BF16)],
        compiler_params=_params(2))(pos, grad, recv)


def _chip_sum(name, own32, recv, pos):
    rh, cols = own32.shape
    tr = _row_tile(rh, cols, 2 * ELEMWISE_BLOCK_BYTES)

    def body(pos_ref, own_ref, r_ref, out_ref):
        del pos_ref
        out_ref[...] = ((own_ref[...] + r_ref[0].astype(F32)) + r_ref[1].astype(F32)) + r_ref[2].astype(F32)

    return pl.pallas_call(
        body, name=name,
        grid_spec=pltpu.PrefetchScalarGridSpec(
            num_scalar_prefetch=1, grid=(rh // tr,),
            in_specs=[pl.BlockSpec((tr, cols), lambda i, pos_ref: (i, 0)),
                      pl.BlockSpec((3, tr, cols), lambda i, pos_ref: (0, i, 0))],
            out_specs=pl.BlockSpec((None, tr, cols), lambda i, pos_ref: (pos_ref[1], i, 0))),
        out_shape=jax.ShapeDtypeStruct((2, rh, cols), F32), compiler_params=_params(1))(pos, own32, recv)


def _cast_into_slot(name, w, layer, pos, after=()):
    _, rows, cols = w.shape
    tr = _row_tile(rows, cols, 2 * ELEMWISE_BLOCK_BYTES)

    def body(pos_ref, w_ref, *rest):
        del pos_ref
        rest[-1][...] = w_ref[...].astype(BF16)

    return pl.pallas_call(
        body, name=name,
        grid_spec=pltpu.PrefetchScalarGridSpec(
            num_scalar_prefetch=1, grid=(rows // tr,),
            in_specs=[pl.BlockSpec((None, tr, cols), lambda i, pos_ref: (layer, i, 0))] + [ANY] * len(after),
            out_specs=pl.BlockSpec((None, tr, cols), lambda i, pos_ref: (pos_ref[0], i, 0))),
        out_shape=jax.ShapeDtypeStruct((N_CHIP, rows, cols), BF16), compiler_params=_params(1))(pos, w, *after)


def _adamw_math(w, g, m, v):
    m = ADAM_B1 * m + (1.0 - ADAM_B1) * g
    v = ADAM_B2 * v + (1.0 - ADAM_B2) * jnp.square(g)
    m_hat = m / (1.0 - ADAM_B1 ** ADAM_STEP)
    v_hat = v / (1.0 - ADAM_B2 ** ADAM_STEP)
    delta = -ADAM_LR * (m_hat / (jnp.sqrt(v_hat) + ADAM_EPS) + ADAM_WD * w)
    return delta, m, v


def _adamw_layers(name, w, m, v, grads, first, into=None, after=()):
    _, rows, cols = w.shape
    tr = _row_tile(rows, cols)
    n_layers = len(grads)

    def body(w_ref, m_ref, v_ref, *rest):
        g_refs, (grad_ref, d_ref, nm_ref, nv_ref) = rest[:n_layers], rest[len(rest) - 4:]
        layer = pl.program_id(0)
        g = g_refs[0][...]
        for l in range(1, n_layers):
            g = jnp.where(layer == l, g_refs[l][...], g)
        grad_ref[...] = g
        d_ref[...], nm_ref[...], nv_ref[...] = _adamw_math(w_ref[...], g, m_ref[...], v_ref[...])

    blk = pl.BlockSpec((None, tr, cols), lambda l, i: (first + l, i, 0))
    g_spec = lambda k: pl.BlockSpec((tr, cols), lambda l, i: (jnp.where(l == k, i, 0), 0))
    passed = list(into or []) + list(after)
    return pl.pallas_call(
        body, name=name, grid=(n_layers, rows // tr),
        in_specs=[blk, blk, blk] + [g_spec(k) for k in range(n_layers)] + [ANY] * len(passed), out_specs=[blk] * 4,
        out_shape=[jax.ShapeDtypeStruct(w.shape, F32)] * 4,
        input_output_aliases={3 + n_layers + t: t for t in range(4)} if into else {},
        compiler_params=_params(2))(w, m, v, *grads, *passed)


def _sum_devices(gathered):
    _, rows, cols = gathered.shape

    def body(g_ref, out_ref):
        s = g_ref[0]
        for d in range(1, 8):
            s = s + g_ref[d]
        out_ref[...] = s

    return pl.pallas_call(body, name="sum_devices", out_shape=jax.ShapeDtypeStruct((rows, cols), F32),
                          compiler_params=pltpu.CompilerParams(vmem_limit_bytes=VMEM_LIMIT_BYTES))(gathered)


def _adamw_flat(w, g, m, v):
    def body(w_ref, g_ref, m_ref, v_ref, d_ref, nm_ref, nv_ref):
        d_ref[...], nm_ref[...], nv_ref[...] = _adamw_math(w_ref[...], g_ref[...], m_ref[...], v_ref[...])

    return pl.pallas_call(body, name="adamw_small", out_shape=[jax.ShapeDtypeStruct(w.shape, F32)] * 3,
                          compiler_params=pltpu.CompilerParams(vmem_limit_bytes=VMEM_LIMIT_BYTES))(w, g, m, v)


SMALL_NAMES = ["g_mix", "lower_bounds", "g_hgrn_out", "w_conv", "sg_ln_g", "sg_ln_b", "w_sg", "b_sg", "g_ffn", "g_final"]
BIG_NAMES = ["w_in", "w_branch", "w_o", "w_ff1", "w_ff2"]
WEIGHT_ORDER = ["w_in", "g_mix", "lower_bounds", "g_hgrn_out", "w_conv", "sg_ln_g", "sg_ln_b", "w_sg", "b_sg", "w_branch",
                "w_o", "g_ffn", "w_ff1", "w_ff2", "g_final"]


def _padded_rows(n):
    return -(-n // SUBLANES) * SUBLANES


def _pack(arrays):
    parts = []
    for a in arrays:
        a = a.reshape(-1, LANES)
        parts.append(jnp.pad(a, ((0, _padded_rows(a.shape[0]) - a.shape[0]), (0, 0))))
    return jnp.concatenate(parts, axis=0)


def _unpack(flat, shapes):
    out, row = [], 0
    for s in shapes:
        n = int(np.prod(s)) // LANES
        out.append(flat[row:row + n].reshape(s))
        row += _padded_rows(n)
    return out


def _as_2d(name, a):
    return a.reshape(DEPTH, N_BRANCH * WIDTH, BRANCH_COLS) if name == "w_branch" else a


def kernel(x, w_in, g_mix, lower_bounds, g_hgrn_out, w_conv, sg_ln_g, sg_ln_b, w_sg, b_sg, w_branch, w_o, g_ffn, w_ff1, w_ff2, g_final, loss_target, m_w_in, m_g_mix, m_lower_bounds, m_g_hgrn_out, m_w_conv, m_sg_ln_g, m_sg_ln_b, m_w_sg, m_b_sg, m_w_branch, m_w_o, m_g_ffn, m_w_ff1, m_w_ff2, m_g_final, v_w_in, v_g_mix, v_lower_bounds, v_g_hgrn_out, v_w_conv, v_sg_ln_g, v_sg_ln_b, v_w_sg, v_b_sg, v_w_branch, v_w_o, v_g_ffn, v_w_ff1, v_w_ff2, v_g_final):
    weights = dict(w_in=w_in, g_mix=g_mix, lower_bounds=lower_bounds, g_hgrn_out=g_hgrn_out, w_conv=w_conv,
                   sg_ln_g=sg_ln_g, sg_ln_b=sg_ln_b, w_sg=w_sg, b_sg=b_sg, w_branch=w_branch, w_o=w_o, g_ffn=g_ffn,
                   w_ff1=w_ff1, w_ff2=w_ff2, g_final=g_final)
    mom1 = dict(w_in=m_w_in, g_mix=m_g_mix, lower_bounds=m_lower_bounds, g_hgrn_out=m_g_hgrn_out, w_conv=m_w_conv,
                sg_ln_g=m_sg_ln_g, sg_ln_b=m_sg_ln_b, w_sg=m_w_sg, b_sg=m_b_sg, w_branch=m_w_branch, w_o=m_w_o,
                g_ffn=m_g_ffn, w_ff1=m_w_ff1, w_ff2=m_w_ff2, g_final=m_g_final)
    mom2 = dict(w_in=v_w_in, g_mix=v_g_mix, lower_bounds=v_lower_bounds, g_hgrn_out=v_g_hgrn_out, w_conv=v_w_conv,
                sg_ln_g=v_sg_ln_g, sg_ln_b=v_sg_ln_b, w_sg=v_w_sg, b_sg=v_b_sg, w_branch=v_w_branch, w_o=v_w_o,
                g_ffn=v_g_ffn, w_ff1=v_w_ff1, w_ff2=v_w_ff2, g_final=v_g_final)
    xi, yi, ci = _mesh_pos()
    pos = jnp.stack([2 * xi + yi, ci]).astype(jnp.int32)
    device = 4 * xi + 2 * yi + ci
    conv_cols = w_conv.shape[2]

    conv_all = _gather_all("gather_w_conv", w_conv.reshape(DEPTH * CONV_K, conv_cols), device)
    conv_full = conv_all.reshape(N_CHIP, 2, DEPTH, CONV_K, conv_cols)[:, 0].transpose(1, 2, 0, 3).reshape(DEPTH, CONV_K, WIDTH)

    ici, d2d = {}, {}
    cast = lambda l, names, after: [_cast_into_slot("cast_" + n, _as_2d(n, weights[n]), l, pos, after=after) for n in names]
    token = (conv_all,)
    for l in range(DEPTH):
        for part, names in (("w_in", BIG_NAMES[:1]), ("rest", BIG_NAMES[1:])):
            ici[l, part] = _split_start("weights_ici_start_%d_%s" % (l, part), cast(l, names, token), _weight_ici_copies,
                                        3 * len(names), after=token)
            token = (ici[l, part][3],)
    lbs = _lbs_fwd(lower_bounds)

    def forward_to_sibling(l, part, deps):
        landed = _split_wait("weights_ici_wait_%d_%s" % (l, part), ici.pop((l, part)), _weight_ici_copies, after=deps)
        d2d[l, part] = _split_start("weights_d2d_start_%d_%s" % (l, part), landed, _weight_d2d_copies, 3 * len(landed))
        return (d2d[l, part][3],)

    def gathered(l, part, deps):
        return _split_wait("weights_d2d_wait_%d_%s" % (l, part), d2d.pop((l, part)), _weight_d2d_copies, after=deps)

    act = x[0]
    normed = _rms_fwd("rms_mix", act, g_mix[0:1], after=token)
    layers = []
    forward_to_sibling(0, "w_in", [normed, lbs])
    for l in range(DEPTH):
        small = dict(g_mix=g_mix[l:l + 1], lbs=lbs[l:l + 1], g_hgrn_out=g_hgrn_out[l:l + 1], w_conv=conv_full[l],
                     sg_ln_g=sg_ln_g[l:l + 1], sg_ln_b=sg_ln_b[l:l + 1], w_sg=w_sg[l],
                     b_sg=b_sg[l].reshape(SG_GROUPS, SG_CHUNK, 1), g_ffn=g_ffn[l:l + 1])
        hooks = dict(after_hgrn=lambda deps, l=l: forward_to_sibling(l, "rest", deps),
                     late_weights=lambda deps, l=l: gathered(l, "rest", deps),
                     before_last=(lambda deps, l=l: forward_to_sibling(l + 1, "w_in", deps)) if l + 1 < DEPTH
                     else (lambda deps: ()))
        act, normed, saved, wts = _layer_fwd(act, normed, gathered(l, "w_in", [act])[0], small,
                                             g_mix[l + 1:l + 2] if l + 1 < DEPTH else None, hooks)
        layers.append((wts, small, saved))
    loss_blk, dact, dg_final = _loss_head(act, g_final.reshape(1, D_MODEL), loss_target[0])

    pipe = _GradPipeline(pos)
    small_grads = [None] * DEPTH
    after = ()
    for l in reversed(range(DEPTH)):
        wts, small, saved = layers[l]
        tick = lambda deps, new, l=l: pipe.tick(deps, [(nm, l, g) for nm, g in new])
        dact, small_grads[l], after = _layer_bwd(dact, saved, wts, small, tick, after)
    grad_x = dact[None]

    stack = lambda key, rows=None: jnp.stack([small_grads[l][key][0] if rows is None else small_grads[l][key][:rows]
                                              for l in range(DEPTH)])
    local_small = dict(
        g_mix=stack("g_mix"), lower_bounds=stack("lbs"), g_hgrn_out=stack("g_hgrn_out"), w_conv=stack("w_conv", CONV_K),
        sg_ln_g=stack("sg_ln_g"), sg_ln_b=stack("sg_ln_b"), w_sg=jnp.stack([small_grads[l]["w_sg"] for l in range(DEPTH)]),
        b_sg=jnp.stack([small_grads[l]["b_sg"].reshape(SG_GROUPS, SG_CHUNK) for l in range(DEPTH)]),
        g_ffn=stack("g_ffn"), g_final=dg_final[0])
    shapes = [local_small[n].shape for n in SMALL_NAMES] + [(SUBLANES, LANES)]
    packed = _pack([local_small[n] for n in SMALL_NAMES] + [loss_blk])
    packed = lax.dynamic_update_slice(jnp.zeros((8, *packed.shape), F32), packed[None], (device, 0, 0))
    small_flight = _split_start("small_grads_start", [packed], _all_to_all_copies, 7, after=after)

    def adam(n, first, layer_grads, into=None, after=()):
        return _adamw_layers("adamw_%s_%d" % (n, first), _as_2d(n, weights[n]), _as_2d(n, mom1[n]), _as_2d(n, mom2[n]),
                             layer_grads, first, into, after)

    done = {"w_ff1": adam("w_ff1", 0, pipe.reduced["w_ff1"], after=(small_flight[3],))}
    token = pipe.tick([done["w_ff1"][1]], [])
    done["w_ff2"] = adam("w_ff2", 0, pipe.reduced["w_ff2"], after=token)
    summed = _sum_devices(_split_wait("small_grads_wait", small_flight, _all_to_all_copies, after=[done["w_ff2"][1]])[0])
    parts = _unpack(summed, shapes)
    loss = parts[-1][0, 0]
    small_grad = dict(zip(SMALL_NAMES, parts[:-1]))
    small_grad["lower_bounds"] = _lbs_bwd(lower_bounds, small_grad["lower_bounds"])
    small_grad["w_conv"] = lax.dynamic_slice_in_dim(small_grad["w_conv"], pos[0] * conv_cols, conv_cols, axis=2)
    g_flat = _pack([small_grad[n] for n in SMALL_NAMES])
    d_flat, m_flat, v_flat = _adamw_flat(_pack([weights[n] for n in SMALL_NAMES]), g_flat,
                                         _pack([mom1[n] for n in SMALL_NAMES]), _pack([mom2[n] for n in SMALL_NAMES]))
    small_shapes = [weights[n].shape for n in SMALL_NAMES]
    grads = dict(small_grad)
    delta = dict(zip(SMALL_NAMES, _unpack(d_flat, small_shapes)))
    new_m = dict(zip(SMALL_NAMES, _unpack(m_flat, small_shapes)))
    new_v = dict(zip(SMALL_NAMES, _unpack(v_flat, small_shapes)))

    for n in ("w_o", "w_branch"):
        done[n] = adam(n, 0, pipe.reduced[n], after=(d_flat,))
    token = pipe.tick([done["w_branch"][1]], [])
    rest = adam("w_in", 1, pipe.reduced["w_in"][1:], after=token)
    pipe.tick([rest[1]], [])
    assert not pipe.busy()
    done["w_in"] = adam("w_in", 0, pipe.reduced["w_in"][:1], into=rest)
    for n in BIG_NAMES:
        grads[n], delta[n], new_m[n], new_v[n] = [o.reshape(weights[n].shape) for o in done[n]]

    return (loss, grad_x, *[grads[n] for n in WEIGHT_ORDER], *[delta[n] for n in WEIGHT_ORDER],
            *[new_m[n] for n in WEIGHT_ORDER], *[new_v[n] for n in WEIGHT_ORDER])
```
